```python
import jax, jax.numpy as jnp
from jax import lax
import numpy as np

D_MODEL = 1024
BATCH = 8
SEQ = 2048
DEPTH = 1

N_META = 16
ATT_HEADS = 8
ATT_HEAD_DIM = D_MODEL // 16
ATT_WIDTH = ATT_HEADS * ATT_HEAD_DIM
KV_LATENT = D_MODEL // 8
IDX_HEADS = 4
IDX_DIM = D_MODEL // 16
TOPK_MAX = 256
Q_BLOCK = 128
ML_HEADS = 4
ML_V_DIM = D_MODEL // 8
ML_QK_DIM = ML_V_DIM // 2
ML_WIDTH = ML_HEADS * ML_V_DIM
ML_CHUNK = 64
CONV_WIDTH = 4
GATE_SOFTCAP = 15.0
M_INIT = -1e30
D_FF = ((8 * D_MODEL // 3 + 255) // 256) * 256
LN_EPS = 1e-5
IN_SIZES = (ATT_WIDTH, KV_LATENT, IDX_HEADS * IDX_DIM, IDX_DIM, IDX_HEADS,
            2 * ML_HEADS * ML_QK_DIM, ML_WIDTH, ML_WIDTH, ML_HEADS, ML_HEADS)
D_IN = sum(IN_SIZES)

kernel_name = "hymba_dsa_mlstm_macaron_deepnorm"


def layer_norm(x, g, b):
    x32 = x.astype(jnp.float32)
    mu = jnp.mean(x32, axis=-1, keepdims=True)
    var = jnp.mean(jnp.square(x32 - mu), axis=-1, keepdims=True)
    return ((x32 - mu) * lax.rsqrt(var + LN_EPS) * g + b).astype(x.dtype)


def rms_norm(x, g):
    x32 = x.astype(jnp.float32)
    return (x32 * lax.rsqrt(jnp.mean(jnp.square(x32), axis=-1, keepdims=True) + LN_EPS) * g).astype(x.dtype)


def swiglu(x, w_gate, w_up, w_down):
    return (jax.nn.silu(x @ w_gate) * (x @ w_up)) @ w_down


def softcap(x):
    return GATE_SOFTCAP * jnp.tanh(x / GATE_SOFTCAP)


def causal_dwconv(x, w):
    t_len = x.shape[1]
    xp = jnp.pad(x, ((0, 0), (CONV_WIDTH - 1, 0), (0, 0)))
    return sum(xp[:, j:j + t_len] * w[j] for j in range(CONV_WIDTH))


def dsa_block(qpos, q_lat, q_idx, w_idx, c_kv, k_idx, topk):
    t_len = c_kv.shape[1]
    logits = jnp.einsum('bqhd,bsd->bqhs', q_idx, k_idx).astype(jnp.float32)
    score = jnp.einsum('bqhs,bqh->bqs', jax.nn.relu(logits), w_idx.astype(jnp.float32))
    causal = jnp.arange(t_len)[None, :] <= qpos[:, None]
    score = jnp.where(causal[None], score, -jnp.inf)
    _, sel = lax.top_k(score, topk)
    kv_sel = jax.vmap(lambda c, i: c[i])(c_kv, sel)
    s = jnp.einsum('bqhc,bqkc->bqhk', q_lat, kv_sel).astype(jnp.float32) * (ATT_HEAD_DIM ** -0.5)
    valid = sel <= qpos[None, :, None]
    s = jnp.where(valid[:, :, None, :], s, -jnp.inf)
    p = jax.nn.softmax(s, axis=-1).astype(c_kv.dtype)
    return jnp.einsum('bqhk,bqkc->bqhc', p, kv_sel)


def dsa_attention(q_lat, q_idx, w_idx, c_kv, k_idx):
    bsz, t_len = c_kv.shape[:2]
    s_len = t_len - N_META
    topk = min(TOPK_MAX, s_len // 4)
    out_meta = dsa_block(jnp.arange(N_META), q_lat[:, :N_META], q_idx[:, :N_META],
                         w_idx[:, :N_META], c_kv, k_idx, topk)
    nb = s_len // Q_BLOCK

    def blocks(a):
        a = a[:, N_META:]
        return jnp.moveaxis(a.reshape(bsz, nb, Q_BLOCK, *a.shape[2:]), 1, 0)

    pos = (N_META + jnp.arange(s_len)).reshape(nb, Q_BLOCK)
    out = lax.map(lambda xs: dsa_block(xs[0], xs[1], xs[2], xs[3], c_kv, k_idx, topk),
                  (pos, blocks(q_lat), blocks(q_idx), blocks(w_idx)))
    out = jnp.moveaxis(out, 0, 1).reshape(bsz, s_len, *q_lat.shape[2:])
    return jnp.concatenate([out_meta, out], axis=1)


def mlstm_chunk(state, inp):
    c_st, n_st, m_st = state
    q, k, v, ig, lf = inp
    chunk = q.shape[2]
    b = jnp.cumsum(lf, axis=-1)
    causal = jnp.tril(jnp.ones((chunk, chunk), dtype=bool))
    d = jnp.where(causal, b[..., :, None] - b[..., None, :] + ig[..., None, :], -jnp.inf)
    inter = b + m_st[..., None]
    m_t = jnp.maximum(jnp.max(d, axis=-1), inter)
    w_inter = jnp.exp(inter - m_t)
    s = jnp.einsum('bhld,bhsd->bhls', q, k) * jnp.exp(d - m_t[..., None])
    num = jnp.einsum('bhls,bhsv->bhlv', s, v) + w_inter[..., None] * jnp.einsum('bhvd,bhld->bhlv', c_st, q)
    den = jnp.sum(s, axis=-1) + w_inter * jnp.einsum('bhd,bhld->bhl', n_st, q)
    h = num / jnp.maximum(jnp.abs(den), jnp.exp(-m_t))[..., None]
    b_end = b[..., -1]
    g = b_end[..., None] - b + ig
    m_new = jnp.maximum(b_end + m_st, jnp.max(g, axis=-1))
    decay = jnp.exp(b_end + m_st - m_new)
    wg = jnp.exp(g - m_new[..., None])
    c_new = decay[..., None, None] * c_st + jnp.einsum('bhl,bhlv,bhld->bhvd', wg, v, k)
    n_new = decay[..., None] * n_st + jnp.einsum('bhl,bhld->bhd', wg, k)
    return (c_new, n_new, m_new), h


def mlstm_chunkwise(q, k, v, ig, lf):
    bsz, nh, t_len, dk = q.shape
    dv = v.shape[-1]
    s_len = t_len - N_META
    nc = s_len // ML_CHUNK
    state = (jnp.zeros((bsz, nh, dv, dk), jnp.float32), jnp.zeros((bsz, nh, dk), jnp.float32),
             jnp.full((bsz, nh), M_INIT, jnp.float32))
    state, h_meta = mlstm_chunk(state, (q[:, :, :N_META], k[:, :, :N_META], v[:, :, :N_META],
                                        ig[:, :, :N_META], lf[:, :, :N_META]))

    def chunks(a):
        a = a[:, :, N_META:]
        return jnp.moveaxis(a.reshape(bsz, nh, nc, ML_CHUNK, *a.shape[3:]), 2, 0)

    _, h_real = lax.scan(mlstm_chunk, state, (chunks(q), chunks(k), chunks(v), chunks(ig), chunks(lf)))
    h_real = jnp.moveaxis(h_real, 0, 2).reshape(bsz, nh, s_len, dv)
    return jnp.concatenate([h_meta, h_real], axis=2)


def hybrid_mixer(x, w_in, w_uk, w_uv, kv_norm_g, conv_w, b_igate, b_fgate, ml_norm_g, w_out):
    bsz, t_len, _ = x.shape
    split_points = np.cumsum(IN_SIZES)[:-1].tolist()
    q_a, c_kv, q_idx, k_idx, w_idx, qk_m, v_m, o_m, i_m, f_m = jnp.split(x @ w_in, split_points, axis=-1)

    q_a = q_a.reshape(bsz, t_len, ATT_HEADS, ATT_HEAD_DIM)
    c_kv = rms_norm(c_kv, kv_norm_g)
    q_lat = jnp.einsum('bthd,hdc->bthc', q_a, w_uk)
    q_idx = q_idx.reshape(bsz, t_len, IDX_HEADS, IDX_DIM)
    w_idx = w_idx * (IDX_HEADS ** -0.5 * IDX_DIM ** -0.5)
    o_lat = dsa_attention(q_lat, q_idx, w_idx, c_kv, k_idx)
    y_att = jnp.einsum('bthc,hcd->bthd', o_lat, w_uv).reshape(bsz, t_len, ATT_WIDTH)

    qk_m = jax.nn.silu(causal_dwconv(qk_m, conv_w))
    q_m, k_m = jnp.split(qk_m, 2, axis=-1)

    def to_heads(a, dim):
        return a.reshape(bsz, t_len, ML_HEADS, dim).transpose(0, 2, 1, 3).astype(jnp.float32)

    ig = softcap(i_m + b_igate).astype(jnp.float32).transpose(0, 2, 1)
    lf = jax.nn.log_sigmoid(softcap(f_m + b_fgate).astype(jnp.float32)).transpose(0, 2, 1)
    h = mlstm_chunkwise(to_heads(q_m, ML_QK_DIM), to_heads(k_m, ML_QK_DIM) * (ML_QK_DIM ** -0.5),
                        to_heads(v_m, ML_V_DIM), ig, lf)
    mu = jnp.mean(h, axis=-1, keepdims=True)
    var = jnp.mean(jnp.square(h - mu), axis=-1, keepdims=True)
    h = (h - mu) * lax.rsqrt(var + LN_EPS) * ml_norm_g.reshape(ML_HEADS, ML_V_DIM)[:, None, :]
    h = h.transpose(0, 2, 1, 3).reshape(bsz, t_len, ML_WIDTH).astype(x.dtype)
    y_ml = jax.nn.sigmoid(o_m) * h

    return jnp.concatenate([y_att, y_ml], axis=-1) @ w_out


def setup_inputs(seed: int = 0) -> dict:
    key = jax.random.key(seed)
    ks = jax.random.split(key, 24)
    beta = (8 * DEPTH) ** -0.25
    f32 = jnp.float32

    def nrm(k, shape, scale):
        return jax.random.normal(k, shape, f32) * scale

    return {
        "x": nrm(ks[0], (BATCH, SEQ, D_MODEL), 1.0),
        "meta_tokens": nrm(ks[1], (N_META, D_MODEL), 1.0),
        "ln1_g": 1.0 + nrm(ks[2], (DEPTH, D_MODEL), 0.02),
        "ln1_b": nrm(ks[3], (DEPTH, D_MODEL), 0.02),
        "ffn1_w_gate": nrm(ks[4], (DEPTH, D_MODEL, D_FF), D_MODEL ** -0.5),
        "ffn1_w_up": nrm(ks[5], (DEPTH, D_MODEL, D_FF), D_MODEL ** -0.5),
        "ffn1_w_down": nrm(ks[6], (DEPTH, D_FF, D_MODEL), D_FF ** -0.5 * beta),
        "w_in": nrm(ks[7], (DEPTH, D_MODEL, D_IN), D_MODEL ** -0.5),
        "w_uk": nrm(ks[8], (DEPTH, ATT_HEADS, ATT_HEAD_DIM, KV_LATENT), ATT_HEAD_DIM ** -0.5),
        "w_uv": nrm(ks[9], (DEPTH, ATT_HEADS, KV_LATENT, ATT_HEAD_DIM), KV_LATENT ** -0.5),
        "kv_norm_g": 1.0 + nrm(ks[10], (DEPTH, KV_LATENT), 0.02),
        "conv_w": nrm(ks[11], (DEPTH, CONV_WIDTH, 2 * ML_HEADS * ML_QK_DIM), CONV_WIDTH ** -0.5),
        "b_igate": nrm(ks[12], (DEPTH, ML_HEADS), 0.1),
        "b_fgate": jnp.linspace(3.0, 6.0, ML_HEADS, dtype=f32)[None, :] + nrm(ks[13], (DEPTH, ML_HEADS), 0.1),
        "ml_norm_g": 1.0 + nrm(ks[14], (DEPTH, ML_WIDTH), 0.02),
        "w_out": nrm(ks[15], (DEPTH, D_MODEL, D_MODEL), D_MODEL ** -0.5 * beta),
        "ln2_g": 1.0 + nrm(ks[16], (DEPTH, D_MODEL), 0.02),
        "ln2_b": nrm(ks[17], (DEPTH, D_MODEL), 0.02),
        "ffn2_w_gate": nrm(ks[18], (DEPTH, D_MODEL, D_FF), D_MODEL ** -0.5),
        "ffn2_w_up": nrm(ks[19], (DEPTH, D_MODEL, D_FF), D_MODEL ** -0.5),
        "ffn2_w_down": nrm(ks[20], (DEPTH, D_FF, D_MODEL), D_FF ** -0.5 * beta),
        "ln3_g": 1.0 + nrm(ks[21], (DEPTH, D_MODEL), 0.02),
        "ln3_b": nrm(ks[22], (DEPTH, D_MODEL), 0.02),
    }


def reference(x, meta_tokens, ln1_g, ln1_b, ffn1_w_gate, ffn1_w_up, ffn1_w_down, w_in, w_uk, w_uv,
              kv_norm_g, conv_w, b_igate, b_fgate, ml_norm_g, w_out, ln2_g, ln2_b,
              ffn2_w_gate, ffn2_w_up, ffn2_w_down, ln3_g, ln3_b):
    alpha = (2 * DEPTH) ** 0.25
    bsz = x.shape[0]
    meta = jnp.broadcast_to(meta_tokens.astype(x.dtype)[None], (bsz, N_META, x.shape[-1]))
    h = jnp.concatenate([meta, x], axis=1)
    for l in range(DEPTH):
        h = layer_norm(alpha * h + 0.5 * swiglu(h, ffn1_w_gate[l], ffn1_w_up[l], ffn1_w_down[l]),
                       ln1_g[l], ln1_b[l])
        h = layer_norm(alpha * h + hybrid_mixer(h, w_in[l], w_uk[l], w_uv[l], kv_norm_g[l], conv_w[l],
                                                b_igate[l], b_fgate[l], ml_norm_g[l], w_out[l]),
                       ln2_g[l], ln2_b[l])
        h = layer_norm(alpha * h + 0.5 * swiglu(h, ffn2_w_gate[l], ffn2_w_up[l], ffn2_w_down[l]),
                       ln3_g[l], ln3_b[l])
    return h[:, N_META:]
```

```python
import functools

import jax
import jax.numpy as jnp
from jax import lax
from jax.experimental import pallas as pl
from jax.experimental.pallas import tpu as pltpu

F32 = jnp.float32
BF16 = jnp.bfloat16

N_META = 16
ATT_HEADS = 8
ATT_HEAD_DIM = 64
KV_LATENT = 128
IDX_HEADS = 4
IDX_DIM = 64
TOPK_MAX = 256
ML_HEADS = 4
ML_V_DIM = 128
ML_QK_DIM = 64
CONV_WIDTH = 4
GATE_SOFTCAP = 15.0
M_INIT = -1e30
LN_EPS = 1e-5
NEG_BIG = -1e30

LANES = 128
SUBLANES = 8
VMEM_LIMIT_BYTES = 56 * 1024 * 1024

FF_CHUNK = 256
ROW_TILE = 512
Q_TILE = 128
KEY_CHUNK = 256
N_BISECT = 22
ML_CHUNK = 128


def _dot(a, b):
    return jnp.dot(a, b, preferred_element_type=F32)


def _dot_nt(a, b):
    return lax.dot_general(a, b, (((1,), (1,)), ((), ())), preferred_element_type=F32)


def _layer_norm(z, g, b):
    mu = jnp.mean(z, axis=-1, keepdims=True)
    zc = z - mu
    var = jnp.mean(zc * zc, axis=-1, keepdims=True)
    return zc * lax.rsqrt(var + LN_EPS) * g + b


def _sigmoid(x):
    return 1.0 / (1.0 + jnp.exp(-x))


def _swiglu(xb, wg_ref, wu_ref, wd_ref):
    d_ff = wg_ref.shape[1]
    acc = jnp.zeros((xb.shape[0], wd_ref.shape[1]), F32)
    for c in range(d_ff // FF_CHUNK):
        sl = slice(c * FF_CHUNK, (c + 1) * FF_CHUNK)
        g = _dot(xb, wg_ref[:, sl])
        u = _dot(xb, wu_ref[:, sl])
        a = (g * _sigmoid(g) * u).astype(BF16)
        acc = acc + _dot(a, wd_ref[sl, :])
    return acc


def _ffn_ln(x, wg_ref, wu_ref, wd_ref, g, b, alpha):
    y = _swiglu(x.astype(BF16), wg_ref, wu_ref, wd_ref)
    return _layer_norm(alpha * x + 0.5 * y, g, b)


def _ffn_ln_kernel(x_ref, wg_ref, wu_ref, wd_ref, g_ref, b_ref, o_ref, *, alpha):
    o_ref[...] = _ffn_ln(x_ref[...], wg_ref, wu_ref, wd_ref, g_ref[...], b_ref[...], alpha)


def _resident(shape):
    return pl.BlockSpec(shape, lambda *_: (0,) * len(shape), pipeline_mode=pl.Buffered(1))


def _ffn_ln_call(x, wg, wu, wd, g, b, alpha, tm):
    rows, d = x.shape
    return pl.pallas_call(
        functools.partial(_ffn_ln_kernel, alpha=alpha),
        grid=(rows // tm,),
        in_specs=[
            pl.BlockSpec((tm, d), lambda i: (i, 0)),
            _resident(wg.shape), _resident(wu.shape), _resident(wd.shape),
            _resident(g.shape), _resident(b.shape),
        ],
        out_specs=pl.BlockSpec((tm, d), lambda i: (i, 0)),
        out_shape=jax.ShapeDtypeStruct((rows, d), F32),
        compiler_params=pltpu.CompilerParams(
            dimension_semantics=("arbitrary",), vmem_limit_bytes=VMEM_LIMIT_BYTES),
        name="ffn_ln",
    )(x, wg, wu, wd, g, b)


ATT_WIDTH = ATT_HEADS * ATT_HEAD_DIM
IDX_WIDTH = IDX_HEADS * IDX_DIM
MLQK_WIDTH = 2 * ML_HEADS * ML_QK_DIM
ML_WIDTH = ML_HEADS * ML_V_DIM
LAT_WIDTH = ATT_HEADS * KV_LATENT
CONV_HIST = SUBLANES
GATE_W0, GATE_I0, GATE_F0, GATE_END = 0, IDX_HEADS, IDX_HEADS + ML_HEADS, IDX_HEADS + 2 * ML_HEADS


def _inproj_kernel(h_ref, tail_ref, wa_ref, wm_ref, wgt_ref, wuk_ref, kvg_ref, convw_ref, gbias_ref,
                   qlat_ref, ckv_ref, qidx_ref, kidx_ref, qk_ref, v_ref, og_ref, gates_ref, tailout_ref,
                   carry_ref):
    tm = h_ref.shape[1]

    @pl.when(pl.program_id(1) == 0)
    def _():
        carry_ref[...] = tail_ref[...]

    xb = h_ref[0].astype(BF16)

    pa = _dot(xb, wa_ref[...])
    q_a = pa[:, :ATT_WIDTH].astype(BF16)
    c0 = ATT_WIDTH
    ckv = pa[:, c0:c0 + KV_LATENT]
    c1 = c0 + KV_LATENT
    ckv = ckv * lax.rsqrt(jnp.mean(ckv * ckv, axis=-1, keepdims=True) + LN_EPS) * kvg_ref[...]
    ckv_ref[0] = ckv.astype(BF16)
    qidx_ref[0] = pa[:, c1:c1 + IDX_WIDTH].astype(BF16)
    c2 = c1 + IDX_WIDTH
    kidx_ref[0] = pa[:, c2:c2 + IDX_DIM].astype(BF16)
    qlat_ref[0] = (_dot(q_a, wuk_ref[...]) * (ATT_HEAD_DIM ** -0.5)).astype(BF16)

    pm = _dot(xb, wm_ref[...])
    qk_raw = pm[:, :MLQK_WIDTH]
    v_ref[0] = pm[:, MLQK_WIDTH:MLQK_WIDTH + ML_WIDTH].astype(BF16)
    og_ref[0] = _sigmoid(pm[:, MLQK_WIDTH + ML_WIDTH:])

    ext = jnp.concatenate([carry_ref[...], qk_raw], axis=0)
    cw = convw_ref[...]
    conv = jnp.zeros_like(qk_raw)
    for j in range(CONV_WIDTH):
        s0 = CONV_HIST - (CONV_WIDTH - 1) + j
        conv = conv + ext[s0:s0 + tm] * cw[j:j + 1]
    act = conv * _sigmoid(conv)
    half = MLQK_WIDTH // 2
    qk_ref[0, :, :half] = act[:, :half].astype(BF16)
    qk_ref[0, :, half:] = (act[:, half:] * (ML_QK_DIM ** -0.5)).astype(BF16)
    carry_ref[...] = qk_raw[tm - CONV_HIST:]
    tailout_ref[0] = qk_raw[tm - CONV_HIST:]

    gr = _dot(xb, wgt_ref[...])
    lane = lax.broadcasted_iota(jnp.int32, gr.shape, 1)
    sc = GATE_SOFTCAP * jnp.tanh((gr + gbias_ref[...]) / GATE_SOFTCAP)
    lf = -(jnp.maximum(-sc, 0.0) + jnp.log1p(jnp.exp(-jnp.abs(sc))))
    w_scaled = gr * (IDX_HEADS ** -0.5 * IDX_DIM ** -0.5)
    gates_ref[0] = jnp.where(lane < GATE_I0, w_scaled,
                             jnp.where(lane < GATE_F0, sc, jnp.where(lane < GATE_END, lf, 0.0)))


def _inproj_call(h, tail, wa, wm, wgt, wuk_bd, kvg, convw, gbias, tm):
    bn, rows, d = h.shape
    nblk = rows // tm

    def row_spec(width):
        return pl.BlockSpec((1, tm, width), lambda b, j: (b, j, 0))

    outs = [
        (LAT_WIDTH, BF16), (KV_LATENT, BF16), (IDX_WIDTH, BF16), (IDX_DIM, BF16),
        (MLQK_WIDTH, BF16), (ML_WIDTH, BF16), (ML_WIDTH, F32), (LANES, F32),
    ]
    out_shape = [jax.ShapeDtypeStruct((bn, rows, w), dt) for w, dt in outs]
    out_specs = [row_spec(w) for w, _ in outs]
    out_shape.append(jax.ShapeDtypeStruct((bn, CONV_HIST, MLQK_WIDTH), F32))
    out_specs.append(pl.BlockSpec((1, CONV_HIST, MLQK_WIDTH), lambda b, j: (b, 0, 0)))
    return pl.pallas_call(
        _inproj_kernel,
        grid=(bn, nblk),
        in_specs=[
            row_spec(d),
            _resident(tail.shape), _resident(wa.shape), _resident(wm.shape), _resident(wgt.shape),
            _resident(wuk_bd.shape), _resident(kvg.shape), _resident(convw.shape), _resident(gbias.shape),
        ],
        out_specs=out_specs,
        out_shape=out_shape,
        scratch_shapes=[pltpu.VMEM((CONV_HIST, MLQK_WIDTH), F32)],
        compiler_params=pltpu.CompilerParams(
            dimension_semantics=("arbitrary", "arbitrary"), vmem_limit_bytes=VMEM_LIMIT_BYTES),
        name="in_proj",
    )(h, tail, wa, wm, wgt, wuk_bd, kvg, convw, gbias)


META_PAD = LANES


def _dsa_kernel(qlat_ref, qidx_ref, gates_ref, ckv_ref, kidx_ref, wuv_ref, y_ref, s_ref, *, topk):
    kc = s_ref.shape[2]
    tq = s_ref.shape[1]
    i = pl.program_id(1)
    nch = (META_PAD + tq * (i + 1) + kc - 1) // kc
    rowpos = i * tq + lax.broadcasted_iota(jnp.int32, (tq, 1), 0)
    kf = float(topk)

    gates = gates_ref[0]
    qidx = qidx_ref[0]
    qi = [qidx[:, h * IDX_DIM:(h + 1) * IDX_DIM] for h in range(IDX_HEADS)]
    wi = [gates[:, GATE_W0 + h:GATE_W0 + h + 1] for h in range(IDX_HEADS)]

    def score_chunk(c, carry):
        off = pl.multiple_of(c * kc, kc)
        kch = kidx_ref[0, pl.ds(off, kc), :]
        sc = jnp.zeros((tq, kc), F32)
        for h in range(IDX_HEADS):
            sc = sc + jnp.maximum(_dot_nt(qi[h], kch), 0.0) * wi[h]
        col = off + lax.broadcasted_iota(jnp.int32, (tq, kc), 1)
        valid = (col < N_META) | ((col >= META_PAD) & (col - META_PAD <= rowpos))
        s_ref[c] = jnp.where(valid, sc, -jnp.inf)
        return carry

    lax.fori_loop(0, nch, score_chunk, 0)

    def lane_reduce(fn, per_chunk, init):
        def body(c, acc):
            return per_chunk(acc, s_ref[c])
        acc = lax.fori_loop(0, nch, body, jnp.full((tq, kc), init, F32))
        return fn(acc, axis=1, keepdims=True)

    def count(pred):
        return lane_reduce(jnp.sum, lambda acc, sc: acc + jnp.where(pred(sc), 1.0, 0.0), 0.0)

    def max_where(pred):
        return lane_reduce(jnp.max, lambda acc, sc: jnp.maximum(acc, jnp.where(pred(sc), sc, -jnp.inf)),
                           -jnp.inf)

    lo = lane_reduce(jnp.min, lambda acc, sc: jnp.minimum(acc, jnp.where(sc == -jnp.inf, jnp.inf, sc)),
                     jnp.inf)
    hi = max_where(lambda sc: sc == sc)

    def bisect(_, carry):
        lo, hi = carry
        mid = 0.5 * lo + 0.5 * hi
        up = count(lambda sc: sc > mid) >= kf
        return jnp.where(up, mid, lo), jnp.where(up, hi, mid)

    lo, hi = lax.fori_loop(0, N_BISECT, bisect, (lo, hi))

    n_valid = (rowpos + (N_META + 1)).astype(F32)
    small = n_valid <= kf
    cand = max_where(lambda sc: sc <= hi)
    done = jnp.where(small | (count(lambda sc: sc >= cand) >= kf), 1.0, 0.0)

    def not_finished(state):
        return jnp.min(state[1]) < 0.5

    def step_down(state):
        cand, done = state
        nxt = jnp.where(done > 0.5, cand, max_where(lambda sc: sc < cand))
        fin = count(lambda sc: sc >= nxt) >= kf
        return nxt, jnp.where(fin, 1.0, done)

    cand, _ = lax.while_loop(not_finished, step_down, (cand, done))
    thr = jnp.where(small, -jnp.inf, cand)
    n_gt = count(lambda sc: sc > thr)
    need = jnp.where(small, 0.0, kf - n_gt)

    qlat = qlat_ref[0]
    q_all = jnp.concatenate([qlat[:, h * KV_LATENT:(h + 1) * KV_LATENT] for h in range(ATT_HEADS)], axis=0)
    tri = (lax.broadcasted_iota(jnp.int32, (kc, kc), 0)
           <= lax.broadcasted_iota(jnp.int32, (kc, kc), 1)).astype(BF16)
    hq = ATT_HEADS * tq

    def attend(c, carry):
        m, l, acc, eq_seen = carry
        sc = s_ref[c]
        eq = sc == thr
        rank = _dot(jnp.where(eq, 1.0, 0.0).astype(BF16), tri) + eq_seen
        keep = (sc > thr) | (eq & (rank <= need))
        bias = jnp.where(keep, 0.0, NEG_BIG)
        off = pl.multiple_of(c * kc, kc)
        kv = ckv_ref[0, pl.ds(off, kc), :]
        s = _dot_nt(q_all, kv)
        s = (s.reshape(ATT_HEADS, tq, kc) + bias[None]).reshape(hq, kc)
        m_new = jnp.maximum(m, jnp.max(s, axis=1, keepdims=True))
        a = jnp.exp(m - m_new)
        p = jnp.exp(s - m_new)
        l = a * l + jnp.sum(p, axis=1, keepdims=True)
        acc = a * acc + _dot(p.astype(BF16), kv)
        return m_new, l, acc, rank[:, kc - 1:kc]

    init = (jnp.full((hq, 1), NEG_BIG, F32), jnp.zeros((hq, 1), F32),
            jnp.zeros((hq, KV_LATENT), F32), jnp.zeros((tq, 1), F32))
    _, l, acc, _ = lax.fori_loop(0, nch, attend, init)
    o = acc / l
    o_cat = jnp.concatenate([o[h * tq:(h + 1) * tq] for h in range(ATT_HEADS)], axis=1)
    y_ref[0] = _dot(o_cat.astype(BF16), wuv_ref[...]).astype(BF16)


def _dsa_call(qlat, qidx, gates, ckv_all, kidx_all, wuv_bd, topk):
    bn, rows, _ = qlat.shape
    nkp = ckv_all.shape[1]
    nq = rows // Q_TILE

    def q_spec(width):
        return pl.BlockSpec((1, Q_TILE, width), lambda b, i: (b, i, 0))

    def k_spec(width):
        return pl.BlockSpec((1, nkp, width), lambda b, i: (b, 0, 0))

    return pl.pallas_call(
        functools.partial(_dsa_kernel, topk=topk),
        grid=(bn, nq),
        in_specs=[q_spec(LAT_WIDTH), q_spec(IDX_WIDTH), q_spec(LANES),
                  k_spec(KV_LATENT), k_spec(IDX_DIM), _resident(wuv_bd.shape)],
        out_specs=q_spec(ATT_WIDTH),
        out_shape=jax.ShapeDtypeStruct((bn, rows, ATT_WIDTH), BF16),
        scratch_shapes=[pltpu.VMEM((nkp // KEY_CHUNK, Q_TILE, KEY_CHUNK), F32)],
        compiler_params=pltpu.CompilerParams(
            dimension_semantics=("arbitrary", "arbitrary"), vmem_limit_bytes=VMEM_LIMIT_BYTES),
        name="dsa",
    )(qlat, qidx, gates, ckv_all, kidx_all, wuv_bd)


def _split3(x):
    hi = x.astype(BF16)
    r = x - hi.astype(F32)
    mid = r.astype(BF16)
    lo = (r - mid.astype(F32)).astype(BF16)
    return hi, mid, lo


def _mlstm_chunk(qk, v, g, state, norm_g):
    L = qk.shape[0]
    row = lax.broadcasted_iota(jnp.int32, (L, L), 0)
    col = lax.broadcasted_iota(jnp.int32, (L, L), 1)
    causal = col <= row
    tri = causal.astype(BF16)
    b_all = sum(_dot(tri, part) for part in _split3(g))
    g_t = g.T
    b_t = b_all.T
    ones_col = jnp.where(lax.broadcasted_iota(jnp.int32, (L, ML_V_DIM), 1) == 0, 1.0, 0.0).astype(BF16)
    kq = ML_HEADS * ML_QK_DIM

    outs, new_state, kws, vexts, decays = [], [], [], [], []
    for h in range(ML_HEADS):
        ce, m_prev = state[h]
        ig_c = g[:, GATE_I0 + h:GATE_I0 + h + 1]
        b_c = b_all[:, GATE_F0 + h:GATE_F0 + h + 1]
        ig_r = g_t[GATE_I0 + h:GATE_I0 + h + 1, :]
        b_r = b_t[GATE_F0 + h:GATE_F0 + h + 1, :]
        qh = qk[:, h * ML_QK_DIM:(h + 1) * ML_QK_DIM]
        kh = qk[:, kq + h * ML_QK_DIM:kq + (h + 1) * ML_QK_DIM]
        vext = jnp.concatenate([v[:, h * ML_V_DIM:(h + 1) * ML_V_DIM], ones_col], axis=1)

        d = jnp.where(causal, b_c - b_r + ig_r, -jnp.inf)
        inter = b_c + m_prev
        m_t = jnp.maximum(jnp.max(d, axis=1, keepdims=True), inter)
        w_inter = jnp.exp(inter - m_t)
        s = _dot_nt(qh, kh) * jnp.exp(d - m_t)
        r = _dot(s.astype(BF16), vext) + w_inter * _dot(qh, ce.astype(BF16))
        num = r[:, :ML_V_DIM]
        den = r[:, ML_V_DIM:ML_V_DIM + 1]
        hh = num / jnp.maximum(jnp.abs(den), jnp.exp(-m_t))
        mu = jnp.mean(hh, axis=-1, keepdims=True)
        hc = hh - mu
        var = jnp.mean(hc * hc, axis=-1, keepdims=True)
        outs.append(hc * lax.rsqrt(var + LN_EPS) * norm_g[:, h * ML_V_DIM:(h + 1) * ML_V_DIM])

        b_end = b_c[L - 1:L]
        g_c = b_end - b_c + ig_c
        g_r = b_end - b_r + ig_r
        m_new = jnp.maximum(b_end + m_prev, jnp.max(g_r, axis=1, keepdims=True))
        decays.append(jnp.exp(b_end + m_prev - m_new))
        kws.append(kh.astype(F32) * jnp.exp(g_c - m_new))
        vexts.append(vext)
        new_state.append(m_new)

    kw_t = jnp.concatenate(kws, axis=1).T
    for h in range(ML_HEADS):
        ce, _ = state[h]
        upd = _dot(kw_t[h * ML_QK_DIM:(h + 1) * ML_QK_DIM].astype(BF16), vexts[h])
        new_state[h] = (decays[h] * ce + upd, new_state[h])
    return outs, new_state


def _mlstm_kernel(qk_ref, v_ref, og_ref, gates_ref, mqk_ref, mv_ref, mgates_ref, ng_ref, y_ref):
    L = mqk_ref.shape[0]
    n_chunks = qk_ref.shape[1] // L
    norm_g = ng_ref[...]
    state = [(jnp.zeros((ML_QK_DIM, 2 * ML_V_DIM), F32), jnp.full((1, 1), M_INIT, F32))
             for _ in range(ML_HEADS)]
    _, state = _mlstm_chunk(mqk_ref[...], mv_ref[...], mgates_ref[...], state, norm_g)

    def body(c, flat):
        state = [(flat[2 * h], flat[2 * h + 1]) for h in range(ML_HEADS)]
        rows = pl.ds(pl.multiple_of(c * L, L), L)
        outs, state = _mlstm_chunk(qk_ref[0, rows, :], v_ref[0, rows, :], gates_ref[0, rows, :], state, norm_g)
        og = og_ref[0, rows, :]
        y = jnp.concatenate(outs, axis=1) * og
        y_ref[0, rows, :] = y.astype(BF16)
        return tuple(x for pair in state for x in pair)

    lax.fori_loop(0, n_chunks, body, tuple(x for pair in state for x in pair))


def _mlstm_call(qk, v, og, gates, mqk, mv, mgates, norm_g):
    bn, rows, _ = qk.shape

    def b_spec(width):
        return pl.BlockSpec((1, rows, width), lambda b: (b, 0, 0))

    return pl.pallas_call(
        _mlstm_kernel,
        grid=(bn,),
        in_specs=[b_spec(MLQK_WIDTH), b_spec(ML_WIDTH), b_spec(ML_WIDTH), b_spec(LANES),
                  _resident(mqk.shape), _resident(mv.shape), _resident(mgates.shape), _resident(norm_g.shape)],
        out_specs=b_spec(ML_WIDTH),
        out_shape=jax.ShapeDtypeStruct((bn, rows, ML_WIDTH), BF16),
        compiler_params=pltpu.CompilerParams(
            dimension_semantics=("arbitrary",), vmem_limit_bytes=VMEM_LIMIT_BYTES),
        name="mlstm",
    )(qk, v, og, gates, mqk, mv, mgates, norm_g)


def _out_ffn_ln_kernel(ya_ref, ym_ref, h_ref, wo_ref, g2_ref, b2_ref, wg_ref, wu_ref, wd_ref, g3_ref, b3_ref,
                       o_ref, *, alpha):
    mix = _dot(ya_ref[...], wo_ref[:ATT_WIDTH, :]) + _dot(ym_ref[...], wo_ref[ATT_WIDTH:, :])
    h2 = _layer_norm(alpha * h_ref[...] + mix, g2_ref[...], b2_ref[...])
    o_ref[...] = _ffn_ln(h2, wg_ref, wu_ref, wd_ref, g3_ref[...], b3_ref[...], alpha)


def _out_ffn_ln_call(ya, ym, h, wo, g2, b2, wg, wu, wd, g3, b3, alpha, tm):
    rows, d = h.shape

    def row_spec(width):
        return pl.BlockSpec((tm, width), lambda i: (i, 0))

    consts = (wo, g2, b2, wg, wu, wd, g3, b3)
    return pl.pallas_call(
        functools.partial(_out_ffn_ln_kernel, alpha=alpha),
        grid=(rows // tm,),
        in_specs=[row_spec(ya.shape[1]), row_spec(ym.shape[1]), row_spec(d)] + [_resident(c.shape) for c in consts],
        out_specs=row_spec(d),
        out_shape=jax.ShapeDtypeStruct((rows, d), F32),
        compiler_params=pltpu.CompilerParams(
            dimension_semantics=("arbitrary",), vmem_limit_bytes=VMEM_LIMIT_BYTES),
        name="out_ffn_ln",
    )(ya, ym, h, *consts)


def _block_diag(w):
    nh, a, b = w.shape
    eye = jnp.eye(nh, dtype=w.dtype)
    return (eye[:, None, :, None] * w[:, :, None, :]).reshape(nh * a, nh * b)


def _pad_rows(a, rows, value=0.0):
    return jnp.pad(a, ((0, rows - a.shape[0]), (0, 0)), constant_values=value)


def kernel(x, meta_tokens, ln1_g, ln1_b, ffn1_w_gate, ffn1_w_up, ffn1_w_down, w_in, w_uk, w_uv, kv_norm_g,
           conv_w, b_igate, b_fgate, ml_norm_g, w_out, ln2_g, ln2_b, ffn2_w_gate, ffn2_w_up, ffn2_w_down,
           ln3_g, ln3_b):
    depth = ln1_g.shape[0]
    assert depth == 1, "the meta-token shortcut below is only valid for a single layer"
    bsz, seq, d = x.shape
    assert seq % ROW_TILE == 0 and seq % ML_CHUNK == 0 and seq % Q_TILE == 0
    alpha = (2 * depth) ** 0.25
    topk = min(TOPK_MAX, seq // 4)

    row2 = lambda p: p[0].reshape(1, -1).astype(F32)
    bf = lambda w: w[0].astype(BF16)

    w = w_in[0]
    o_qa, o_ckv, o_qi, o_ki, o_wi = 0, 512, 640, 896, 960
    o_qk, o_v, o_o, o_ig, o_fg, o_end = 964, 1476, 1988, 2500, 2504, 2508
    wa = jnp.concatenate([w[:, o_qa:o_wi], jnp.zeros((d, 1024 - o_wi), F32)], axis=1).astype(BF16)
    wm = w[:, o_qk:o_ig].astype(BF16)
    wgt = jnp.concatenate([w[:, o_wi:o_qk], w[:, o_ig:o_end], jnp.zeros((d, LANES - GATE_END), F32)],
                          axis=1).astype(BF16)
    gbias = jnp.concatenate([jnp.zeros((IDX_HEADS,), F32), b_igate[0], b_fgate[0],
                             jnp.zeros((LANES - GATE_END,), F32)]).reshape(1, LANES)
    wuk_bd = _block_diag(w_uk[0]).astype(BF16)
    wuv_bd = _block_diag(w_uv[0]).astype(BF16)
    kvg = row2(kv_norm_g)
    convw = conv_w[0].astype(F32)
    ffn1 = (bf(ffn1_w_gate), bf(ffn1_w_up), bf(ffn1_w_down), row2(ln1_g), row2(ln1_b))

    meta = meta_tokens.astype(F32)
    h1_meta = _ffn_ln_call(meta, *ffn1, alpha, N_META)
    zero_tail = jnp.zeros((CONV_HIST, MLQK_WIDTH), F32)
    (_, m_ckv, _, m_kidx, m_qk, m_v, _, m_gates, m_tail) = _inproj_call(
        h1_meta[None], zero_tail, wa, wm, wgt, wuk_bd, kvg, convw, gbias, N_META)

    h1 = _ffn_ln_call(x.reshape(bsz * seq, d), *ffn1, alpha, ROW_TILE)
    (qlat, ckv, qidx, kidx, qk, v, og, gates, _) = _inproj_call(
        h1.reshape(bsz, seq, d), m_tail[0], wa, wm, wgt, wuk_bd, kvg, convw, gbias, ROW_TILE)

    nkp = -(-(META_PAD + seq) // KEY_CHUNK) * KEY_CHUNK

    def with_meta(m, a):
        m = jnp.broadcast_to(_pad_rows(m[0], META_PAD)[None], (bsz, META_PAD, a.shape[2]))
        tail = jnp.zeros((bsz, nkp - META_PAD - seq, a.shape[2]), a.dtype)
        return jnp.concatenate([m, a, tail], axis=1)

    y_att = _dsa_call(qlat, qidx, gates, with_meta(m_ckv, ckv), with_meta(m_kidx, kidx), wuv_bd, topk)

    lane = jnp.arange(LANES)
    pad_gate = jnp.where((lane >= GATE_I0) & (lane < GATE_F0), NEG_BIG, 0.0).astype(F32)
    mg = jnp.concatenate([m_gates[0], jnp.broadcast_to(pad_gate, (ML_CHUNK - N_META, LANES))], axis=0)
    y_ml = _mlstm_call(qk, v, og, gates, _pad_rows(m_qk[0], ML_CHUNK), _pad_rows(m_v[0], ML_CHUNK), mg,
                       row2(ml_norm_g))

    out = _out_ffn_ln_call(
        y_att.reshape(bsz * seq, ATT_WIDTH), y_ml.reshape(bsz * seq, ML_WIDTH), h1, bf(w_out),
        row2(ln2_g), row2(ln2_b), bf(ffn2_w_gate), bf(ffn2_w_up), bf(ffn2_w_down), row2(ln3_g), row2(ln3_b),
        alpha, ROW_TILE)
    return out.reshape(bsz, seq, d)
```

```python
import functools

import jax
import jax.numpy as jnp
from jax import lax
from jax.experimental import pallas as pl
from jax.experimental.pallas import tpu as pltpu

F32 = jnp.float32
BF16 = jnp.bfloat16

N_META = 16
ATT_HEADS = 8
ATT_HEAD_DIM = 64
KV_LATENT = 128
IDX_HEADS = 4
IDX_DIM = 64
TOPK_MAX = 256
ML_HEADS = 4
ML_V_DIM = 128
ML_QK_DIM = 64
CONV_WIDTH = 4
GATE_SOFTCAP = 15.0
M_INIT = -1e30
LN_EPS = 1e-5
NEG_BIG = -1e30

LANES = 128
SUBLANES = 8
VMEM_LIMIT_BYTES = 56 * 1024 * 1024

FF_CHUNK = 256
ROW_TILE = 512
Q_TILE = 128
KEY_CHUNK = 256
N_BISECT = 22
REDUCE_ROWS = 32
ML_CHUNK = 128


def _dot(a, b):
    return jnp.dot(a, b, preferred_element_type=F32)


def _dot_nt(a, b):
    return lax.dot_general(a, b, (((1,), (1,)), ((), ())), preferred_element_type=F32)


def _layer_norm(z, g, b):
    mu = jnp.mean(z, axis=-1, keepdims=True)
    zc = z - mu
    var = jnp.mean(zc * zc, axis=-1, keepdims=True)
    return zc * lax.rsqrt(var + LN_EPS) * g + b


def _sigmoid(x):
    return 1.0 / (1.0 + jnp.exp(-x))


def _swiglu(xb, wg_ref, wu_ref, wd_ref):
    d_ff = wg_ref.shape[1]
    acc = jnp.zeros((xb.shape[0], wd_ref.shape[1]), F32)
    for c in range(d_ff // FF_CHUNK):
        sl = slice(c * FF_CHUNK, (c + 1) * FF_CHUNK)
        g = _dot(xb, wg_ref[:, sl])
        u = _dot(xb, wu_ref[:, sl])
        a = (g * _sigmoid(g) * u).astype(BF16)
        acc = acc + _dot(a, wd_ref[sl, :])
    return acc


def _ffn_ln(x, wg_ref, wu_ref, wd_ref, g, b, alpha):
    y = _swiglu(x.astype(BF16), wg_ref, wu_ref, wd_ref)
    return _layer_norm(alpha * x + 0.5 * y, g, b)


def _ffn_ln_kernel(x_ref, wg_ref, wu_ref, wd_ref, g_ref, b_ref, o_ref, *, alpha):
    o_ref[...] = _ffn_ln(x_ref[...], wg_ref, wu_ref, wd_ref, g_ref[...], b_ref[...], alpha)


def _resident(shape):
    return pl.BlockSpec(shape, lambda *_: (0,) * len(shape), pipeline_mode=pl.Buffered(1))


def _ffn_ln_call(x, wg, wu, wd, g, b, alpha, tm):
    rows, d = x.shape
    return pl.pallas_call(
        functools.partial(_ffn_ln_kernel, alpha=alpha),
        grid=(rows // tm,),
        in_specs=[
            pl.BlockSpec((tm, d), lambda i: (i, 0)),
            _resident(wg.shape), _resident(wu.shape), _resident(wd.shape),
            _resident(g.shape), _resident(b.shape),
        ],
        out_specs=pl.BlockSpec((tm, d), lambda i: (i, 0)),
        out_shape=jax.ShapeDtypeStruct((rows, d), F32),
        compiler_params=pltpu.CompilerParams(
            dimension_semantics=("arbitrary",), vmem_limit_bytes=VMEM_LIMIT_BYTES),
        name="ffn_ln",
    )(x, wg, wu, wd, g, b)


ATT_WIDTH = ATT_HEADS * ATT_HEAD_DIM
IDX_WIDTH = IDX_HEADS * IDX_DIM
MLQK_WIDTH = 2 * ML_HEADS * ML_QK_DIM
ML_WIDTH = ML_HEADS * ML_V_DIM
LAT_WIDTH = ATT_HEADS * KV_LATENT
CONV_HIST = SUBLANES
GATE_W0, GATE_I0, GATE_F0, GATE_END = 0, IDX_HEADS, IDX_HEADS + ML_HEADS, IDX_HEADS + 2 * ML_HEADS


def _inproj_kernel(h_ref, tail_ref, wa_ref, wm_ref, wgt_ref, wuk_ref, kvg_ref, convw_ref, gbias_ref,
                   qlat_ref, ckv_ref, qidx_ref, kidx_ref, qk_ref, v_ref, og_ref, gates_ref, tailout_ref,
                   carry_ref):
    tm = h_ref.shape[1]

    @pl.when(pl.program_id(1) == 0)
    def _():
        carry_ref[...] = tail_ref[...]

    xb = h_ref[0].astype(BF16)

    pa = _dot(xb, wa_ref[...])
    q_a = pa[:, :ATT_WIDTH].astype(BF16)
    c0 = ATT_WIDTH
    ckv = pa[:, c0:c0 + KV_LATENT]
    c1 = c0 + KV_LATENT
    ckv = ckv * lax.rsqrt(jnp.mean(ckv * ckv, axis=-1, keepdims=True) + LN_EPS) * kvg_ref[...]
    ckv_ref[0] = ckv.astype(BF16)
    qidx_ref[0] = pa[:, c1:c1 + IDX_WIDTH].astype(BF16)
    c2 = c1 + IDX_WIDTH
    kidx_ref[0] = pa[:, c2:c2 + IDX_DIM].astype(BF16)
    qlat_ref[0] = (_dot(q_a, wuk_ref[...]) * (ATT_HEAD_DIM ** -0.5)).astype(BF16)

    pm = _dot(xb, wm_ref[...])
    qk_raw = pm[:, :MLQK_WIDTH]
    v_ref[0] = pm[:, MLQK_WIDTH:MLQK_WIDTH + ML_WIDTH].astype(BF16)
    og_ref[0] = _sigmoid(pm[:, MLQK_WIDTH + ML_WIDTH:])

    ext = jnp.concatenate([carry_ref[...], qk_raw], axis=0)
    cw = convw_ref[...]
    conv = jnp.zeros_like(qk_raw)
    for j in range(CONV_WIDTH):
        s0 = CONV_HIST - (CONV_WIDTH - 1) + j
        conv = conv + ext[s0:s0 + tm] * cw[j:j + 1]
    act = conv * _sigmoid(conv)
    half = MLQK_WIDTH // 2
    qk_ref[0, :, :half] = act[:, :half].astype(BF16)
    qk_ref[0, :, half:] = (act[:, half:] * (ML_QK_DIM ** -0.5)).astype(BF16)
    carry_ref[...] = qk_raw[tm - CONV_HIST:]
    tailout_ref[0] = qk_raw[tm - CONV_HIST:]

    gr = _dot(xb, wgt_ref[...])
    lane = lax.broadcasted_iota(jnp.int32, gr.shape, 1)
    sc = GATE_SOFTCAP * jnp.tanh((gr + gbias_ref[...]) / GATE_SOFTCAP)
    lf = -(jnp.maximum(-sc, 0.0) + jnp.log1p(jnp.exp(-jnp.abs(sc))))
    w_scaled = gr * (IDX_HEADS ** -0.5 * IDX_DIM ** -0.5)
    gates_ref[0] = jnp.where(lane < GATE_I0, w_scaled,
                             jnp.where(lane < GATE_F0, sc, jnp.where(lane < GATE_END, lf, 0.0)))


def _inproj_call(h, tail, wa, wm, wgt, wuk_bd, kvg, convw, gbias, tm):
    bn, rows, d = h.shape
    nblk = rows // tm

    def row_spec(width):
        return pl.BlockSpec((1, tm, width), lambda b, j: (b, j, 0))

    outs = [
        (LAT_WIDTH, BF16), (KV_LATENT, BF16), (IDX_WIDTH, BF16), (IDX_DIM, BF16),
        (MLQK_WIDTH, BF16), (ML_WIDTH, BF16), (ML_WIDTH, F32), (LANES, F32),
    ]
    out_shape = [jax.ShapeDtypeStruct((bn, rows, w), dt) for w, dt in outs]
    out_specs = [row_spec(w) for w, _ in outs]
    out_shape.append(jax.ShapeDtypeStruct((bn, CONV_HIST, MLQK_WIDTH), F32))
    out_specs.append(pl.BlockSpec((1, CONV_HIST, MLQK_WIDTH), lambda b, j: (b, 0, 0)))
    return pl.pallas_call(
        _inproj_kernel,
        grid=(bn, nblk),
        in_specs=[
            row_spec(d),
            _resident(tail.shape), _resident(wa.shape), _resident(wm.shape), _resident(wgt.shape),
            _resident(wuk_bd.shape), _resident(kvg.shape), _resident(convw.shape), _resident(gbias.shape),
        ],
        out_specs=out_specs,
        out_shape=out_shape,
        scratch_shapes=[pltpu.VMEM((CONV_HIST, MLQK_WIDTH), F32)],
        compiler_params=pltpu.CompilerParams(
            dimension_semantics=("arbitrary", "arbitrary"), vmem_limit_bytes=VMEM_LIMIT_BYTES),
        name="in_proj",
    )(h, tail, wa, wm, wgt, wuk_bd, kvg, convw, gbias)


def _dsa_kernel(qlat_ref, qidx_ref, wrow_ref, ckv_ref, ckvt_ref, kidx_ref, wuv_ref, y_ref, s_ref, *, topk):
    _, kc, tq = s_ref.shape
    i = pl.program_id(1)
    nch = (N_META + tq * (i + 1) + kc - 1) // kc
    qpos = N_META + i * tq + lax.broadcasted_iota(jnp.int32, (1, tq), 1)
    kf = float(topk)

    wrow = wrow_ref[0]
    qidx = qidx_ref[0]
    q_idx_all = jnp.concatenate([qidx[:, h * IDX_DIM:(h + 1) * IDX_DIM] for h in range(IDX_HEADS)], axis=0)
    wi = [wrow[GATE_W0 + h:GATE_W0 + h + 1, :] for h in range(IDX_HEADS)]

    def score_chunk(c, carry):
        lg = _dot_nt(kidx_ref[0, c], q_idx_all)
        sc = jnp.zeros((kc, tq), F32)
        for h in range(IDX_HEADS):
            sc = sc + jnp.maximum(lg[:, h * tq:(h + 1) * tq], 0.0) * wi[h]
        key = c * kc + lax.broadcasted_iota(jnp.int32, (kc, tq), 0)
        s_ref[c] = jnp.where(key <= qpos, sc, -jnp.inf)
        return carry

    lax.fori_loop(0, nch, score_chunk, 0)

    def key_reduce(reduce, combine, per_chunk, init):
        def body(c, acc):
            x = per_chunk(s_ref[c]).reshape(kc // REDUCE_ROWS, REDUCE_ROWS, tq)
            return combine(acc, reduce(x, axis=0))
        acc = lax.fori_loop(0, nch, body, jnp.full((REDUCE_ROWS, tq), init, F32))
        return reduce(acc, axis=0, keepdims=True)

    def count(pred):
        return key_reduce(jnp.sum, jnp.add, lambda sc: jnp.where(pred(sc), 1.0, 0.0), 0.0)

    def max_where(pred):
        return key_reduce(jnp.max, jnp.maximum, lambda sc: jnp.where(pred(sc), sc, -jnp.inf), -jnp.inf)

    lo = key_reduce(jnp.min, jnp.minimum, lambda sc: jnp.where(sc == -jnp.inf, jnp.inf, sc), jnp.inf)
    hi = max_where(lambda sc: sc == sc)

    def bisect(_, carry):
        lo, hi = carry
        mid = 0.5 * lo + 0.5 * hi
        up = count(lambda sc: sc > mid) >= kf
        return jnp.where(up, mid, lo), jnp.where(up, hi, mid)

    lo, hi = lax.fori_loop(0, N_BISECT, bisect, (lo, hi))

    n_valid = (qpos + 1).astype(F32)
    small = n_valid <= kf
    cand = max_where(lambda sc: sc <= hi)
    done = jnp.where(small | (count(lambda sc: sc >= cand) >= kf), 1.0, 0.0)

    def not_finished(state):
        return jnp.min(state[1]) < 0.5

    def step_down(state):
        cand, done = state
        nxt = jnp.where(done > 0.5, cand, max_where(lambda sc: sc < cand))
        fin = count(lambda sc: sc >= nxt) >= kf
        return nxt, jnp.where(fin, 1.0, done)

    cand, _ = lax.while_loop(not_finished, step_down, (cand, done))
    thr = jnp.where(small, -jnp.inf, cand)
    n_gt = count(lambda sc: sc > thr)
    need = jnp.where(small, 0.0, kf - n_gt)

    qlat = qlat_ref[0]
    q_all = jnp.concatenate([qlat[:, h * KV_LATENT:(h + 1) * KV_LATENT] for h in range(ATT_HEADS)], axis=0)
    tri = (lax.broadcasted_iota(jnp.int32, (kc, kc), 1)
           <= lax.broadcasted_iota(jnp.int32, (kc, kc), 0)).astype(BF16)
    hq = ATT_HEADS * tq

    def attend(c, carry):
        m, l, acc, eq_seen = carry
        sc = s_ref[c]
        eq = sc == thr
        rank = _dot(tri, jnp.where(eq, 1.0, 0.0).astype(BF16)) + eq_seen
        keep = (sc > thr) | (eq & (rank <= need))
        bias = jnp.where(keep, 0.0, NEG_BIG)
        s = _dot_nt(ckv_ref[0, c], q_all)
        s = jnp.concatenate([s[:, h * tq:(h + 1) * tq] + bias for h in range(ATT_HEADS)], axis=1)
        m_new = jnp.maximum(m, jnp.max(s, axis=0, keepdims=True))
        a = jnp.exp(m - m_new)
        p = jnp.exp(s - m_new)
        l = a * l + jnp.sum(p, axis=0, keepdims=True)
        acc = a * acc + _dot(ckvt_ref[0, c], p.astype(BF16))
        return m_new, l, acc, rank[kc - 1:kc, :]

    init = (jnp.full((1, hq), NEG_BIG, F32), jnp.zeros((1, hq), F32),
            jnp.zeros((KV_LATENT, hq), F32), jnp.zeros((1, tq), F32))
    _, l, acc, _ = lax.fori_loop(0, nch, attend, init)
    o = acc / l
    o_cat = jnp.concatenate([o[:, h * tq:(h + 1) * tq].T for h in range(ATT_HEADS)], axis=1)
    y_ref[0] = _dot(o_cat.astype(BF16), wuv_ref[...]).astype(BF16)


def _dsa_call(qlat, qidx, wrow, ckv_c, ckvt_c, kidx_c, wuv_bd, topk):
    bn, rows, _ = qlat.shape
    nchunks, kc = ckv_c.shape[1], ckv_c.shape[2]
    nq = rows // Q_TILE

    def q_spec(width):
        return pl.BlockSpec((1, Q_TILE, width), lambda b, i: (b, i, 0))

    def k_spec(a):
        return pl.BlockSpec((1,) + a.shape[1:], lambda b, i: (b, 0, 0, 0))

    return pl.pallas_call(
        functools.partial(_dsa_kernel, topk=topk),
        grid=(bn, nq),
        in_specs=[q_spec(LAT_WIDTH), q_spec(IDX_WIDTH),
                  pl.BlockSpec((1, SUBLANES, Q_TILE), lambda b, i: (b, 0, i)),
                  k_spec(ckv_c), k_spec(ckvt_c), k_spec(kidx_c), _resident(wuv_bd.shape)],
        out_specs=q_spec(ATT_WIDTH),
        out_shape=jax.ShapeDtypeStruct((bn, rows, ATT_WIDTH), BF16),
        scratch_shapes=[pltpu.VMEM((nchunks, kc, Q_TILE), F32)],
        compiler_params=pltpu.CompilerParams(
            dimension_semantics=("arbitrary", "arbitrary"), vmem_limit_bytes=VMEM_LIMIT_BYTES),
        name="dsa",
    )(qlat, qidx, wrow, ckv_c, ckvt_c, kidx_c, wuv_bd)


def _split3(x):
    hi = x.astype(BF16)
    r = x - hi.astype(F32)
    mid = r.astype(BF16)
    lo = (r - mid.astype(F32)).astype(BF16)
    return hi, mid, lo


def _mlstm_chunk(qk, v, g, state, norm_g):
    L = qk.shape[0]
    row = lax.broadcasted_iota(jnp.int32, (L, L), 0)
    col = lax.broadcasted_iota(jnp.int32, (L, L), 1)
    causal = col <= row
    tri = causal.astype(BF16)
    b_all = sum(_dot(tri, part) for part in _split3(g))
    g_t = g.T
    b_t = b_all.T
    ones_col = jnp.where(lax.broadcasted_iota(jnp.int32, (L, ML_V_DIM), 1) == 0, 1.0, 0.0).astype(BF16)
    kq = ML_HEADS * ML_QK_DIM

    outs, new_state, kws, vexts, decays = [], [], [], [], []
    for h in range(ML_HEADS):
        ce, m_prev = state[h]
        ig_c = g[:, GATE_I0 + h:GATE_I0 + h + 1]
        b_c = b_all[:, GATE_F0 + h:GATE_F0 + h + 1]
        ig_r = g_t[GATE_I0 + h:GATE_I0 + h + 1, :]
        b_r = b_t[GATE_F0 + h:GATE_F0 + h + 1, :]
        qh = qk[:, h * ML_QK_DIM:(h + 1) * ML_QK_DIM]
        kh = qk[:, kq + h * ML_QK_DIM:kq + (h + 1) * ML_QK_DIM]
        vext = jnp.concatenate([v[:, h * ML_V_DIM:(h + 1) * ML_V_DIM], ones_col], axis=1)

        d = jnp.where(causal, b_c - b_r + ig_r, -jnp.inf)
        inter = b_c + m_prev
        m_t = jnp.maximum(jnp.max(d, axis=1, keepdims=True), inter)
        w_inter = jnp.exp(inter - m_t)
        s = _dot_nt(qh, kh) * jnp.exp(d - m_t)
        r = _dot(s.astype(BF16), vext) + w_inter * _dot(qh, ce.astype(BF16))
        num = r[:, :ML_V_DIM]
        den = r[:, ML_V_DIM:ML_V_DIM + 1]
        hh = num / jnp.maximum(jnp.abs(den), jnp.exp(-m_t))
        mu = jnp.mean(hh, axis=-1, keepdims=True)
        hc = hh - mu
        var = jnp.mean(hc * hc, axis=-1, keepdims=True)
        outs.append(hc * lax.rsqrt(var + LN_EPS) * norm_g[:, h * ML_V_DIM:(h + 1) * ML_V_DIM])

        b_end = b_c[L - 1:L]
        g_c = b_end - b_c + ig_c
        g_r = b_end - b_r + ig_r
        m_new = jnp.maximum(b_end + m_prev, jnp.max(g_r, axis=1, keepdims=True))
        decays.append(jnp.exp(b_end + m_prev - m_new))
        kws.append(kh.astype(F32) * jnp.exp(g_c - m_new))
        vexts.append(vext)
        new_state.append(m_new)

    kw_t = jnp.concatenate(kws, axis=1).T
    for h in range(ML_HEADS):
        ce, _ = state[h]
        upd = _dot(kw_t[h * ML_QK_DIM:(h + 1) * ML_QK_DIM].astype(BF16), vexts[h])
        new_state[h] = (decays[h] * ce + upd, new_state[h])
    return outs, new_state


def _mlstm_kernel(qk_ref, v_ref, og_ref, gates_ref, mqk_ref, mv_ref, mgates_ref, ng_ref, y_ref):
    L = mqk_ref.shape[0]
    n_chunks = qk_ref.shape[1] // L
    norm_g = ng_ref[...]
    state = [(jnp.zeros((ML_QK_DIM, 2 * ML_V_DIM), F32), jnp.full((1, 1), M_INIT, F32))
             for _ in range(ML_HEADS)]
    _, state = _mlstm_chunk(mqk_ref[...], mv_ref[...], mgates_ref[...], state, norm_g)

    def body(c, flat):
        state = [(flat[2 * h], flat[2 * h + 1]) for h in range(ML_HEADS)]
        rows = pl.ds(pl.multiple_of(c * L, L), L)
        outs, state = _mlstm_chunk(qk_ref[0, rows, :], v_ref[0, rows, :], gates_ref[0, rows, :], state, norm_g)
        og = og_ref[0, rows, :]
        y = jnp.concatenate(outs, axis=1) * og
        y_ref[0, rows, :] = y.astype(BF16)
        return tuple(x for pair in state for x in pair)

    lax.fori_loop(0, n_chunks, body, tuple(x for pair in state for x in pair))


def _mlstm_call(qk, v, og, gates, mqk, mv, mgates, norm_g):
    bn, rows, _ = qk.shape

    def b_spec(width):
        return pl.BlockSpec((1, rows, width), lambda b: (b, 0, 0))

    return pl.pallas_call(
        _mlstm_kernel,
        grid=(bn,),
        in_specs=[b_spec(MLQK_WIDTH), b_spec(ML_WIDTH), b_spec(ML_WIDTH), b_spec(LANES),
                  _resident(mqk.shape), _resident(mv.shape), _resident(mgates.shape), _resident(norm_g.shape)],
        out_specs=b_spec(ML_WIDTH),
        out_shape=jax.ShapeDtypeStruct((bn, rows, ML_WIDTH), BF16),
        compiler_params=pltpu.CompilerParams(
            dimension_semantics=("arbitrary",), vmem_limit_bytes=VMEM_LIMIT_BYTES),
        name="mlstm",
    )(qk, v, og, gates, mqk, mv, mgates, norm_g)


def _out_ffn_ln_kernel(ya_ref, ym_ref, h_ref, wo_ref, g2_ref, b2_ref, wg_ref, wu_ref, wd_ref, g3_ref, b3_ref,
                       o_ref, *, alpha):
    mix = _dot(ya_ref[...], wo_ref[:ATT_WIDTH, :]) + _dot(ym_ref[...], wo_ref[ATT_WIDTH:, :])
    h2 = _layer_norm(alpha * h_ref[...] + mix, g2_ref[...], b2_ref[...])
    o_ref[...] = _ffn_ln(h2, wg_ref, wu_ref, wd_ref, g3_ref[...], b3_ref[...], alpha)


def _out_ffn_ln_call(ya, ym, h, wo, g2, b2, wg, wu, wd, g3, b3, alpha, tm):
    rows, d = h.shape

    def row_spec(width):
        return pl.BlockSpec((tm, width), lambda i: (i, 0))

    consts = (wo, g2, b2, wg, wu, wd, g3, b3)
    return pl.pallas_call(
        functools.partial(_out_ffn_ln_kernel, alpha=alpha),
        grid=(rows // tm,),
        in_specs=[row_spec(ya.shape[1]), row_spec(ym.shape[1]), row_spec(d)] + [_resident(c.shape) for c in consts],
        out_specs=row_spec(d),
        out_shape=jax.ShapeDtypeStruct((rows, d), F32),
        compiler_params=pltpu.CompilerParams(
            dimension_semantics=("arbitrary",), vmem_limit_bytes=VMEM_LIMIT_BYTES),
        name="out_ffn_ln",
    )(ya, ym, h, *consts)


def _block_diag(w):
    nh, a, b = w.shape
    eye = jnp.eye(nh, dtype=w.dtype)
    return (eye[:, None, :, None] * w[:, :, None, :]).reshape(nh * a, nh * b)


def _pad_rows(a, rows, value=0.0):
    return jnp.pad(a, ((0, rows - a.shape[0]), (0, 0)), constant_values=value)


def kernel(x, meta_tokens, ln1_g, ln1_b, ffn1_w_gate, ffn1_w_up, ffn1_w_down, w_in, w_uk, w_uv, kv_norm_g,
           conv_w, b_igate, b_fgate, ml_norm_g, w_out, ln2_g, ln2_b, ffn2_w_gate, ffn2_w_up, ffn2_w_down,
           ln3_g, ln3_b):
    depth = ln1_g.shape[0]
    assert depth == 1, "the meta-token shortcut below is only valid for a single layer"
    bsz, seq, d = x.shape
    assert seq % ROW_TILE == 0 and seq % ML_CHUNK == 0 and seq % Q_TILE == 0
    alpha = (2 * depth) ** 0.25
    topk = min(TOPK_MAX, seq // 4)

    row2 = lambda p: p[0].reshape(1, -1).astype(F32)
    bf = lambda w: w[0].astype(BF16)

    w = w_in[0]
    o_qa, o_ckv, o_qi, o_ki, o_wi = 0, 512, 640, 896, 960
    o_qk, o_v, o_o, o_ig, o_fg, o_end = 964, 1476, 1988, 2500, 2504, 2508
    wa = jnp.concatenate([w[:, o_qa:o_wi], jnp.zeros((d, 1024 - o_wi), F32)], axis=1).astype(BF16)
    wm = w[:, o_qk:o_ig].astype(BF16)
    wgt = jnp.concatenate([w[:, o_wi:o_qk], w[:, o_ig:o_end], jnp.zeros((d, LANES - GATE_END), F32)],
                          axis=1).astype(BF16)
    gbias = jnp.concatenate([jnp.zeros((IDX_HEADS,), F32), b_igate[0], b_fgate[0],
                             jnp.zeros((LANES - GATE_END,), F32)]).reshape(1, LANES)
    wuk_bd = _block_diag(w_uk[0]).astype(BF16)
    wuv_bd = _block_diag(w_uv[0]).astype(BF16)
    kvg = row2(kv_norm_g)
    convw = conv_w[0].astype(F32)
    ffn1 = (bf(ffn1_w_gate), bf(ffn1_w_up), bf(ffn1_w_down), row2(ln1_g), row2(ln1_b))

    meta = meta_tokens.astype(F32)
    h1_meta = _ffn_ln_call(meta, *ffn1, alpha, N_META)
    zero_tail = jnp.zeros((CONV_HIST, MLQK_WIDTH), F32)
    (_, m_ckv, _, m_kidx, m_qk, m_v, _, m_gates, m_tail) = _inproj_call(
        h1_meta[None], zero_tail, wa, wm, wgt, wuk_bd, kvg, convw, gbias, N_META)

    h1 = _ffn_ln_call(x.reshape(bsz * seq, d), *ffn1, alpha, ROW_TILE)
    (qlat, ckv, qidx, kidx, qk, v, og, gates, _) = _inproj_call(
        h1.reshape(bsz, seq, d), m_tail[0], wa, wm, wgt, wuk_bd, kvg, convw, gbias, ROW_TILE)

    nchunks = -(-(N_META + seq) // KEY_CHUNK)

    def key_chunks(m, a):
        m = jnp.broadcast_to(m, (bsz, N_META, a.shape[2]))
        tail = jnp.zeros((bsz, nchunks * KEY_CHUNK - N_META - seq, a.shape[2]), a.dtype)
        return jnp.concatenate([m, a, tail], axis=1).reshape(bsz, nchunks, KEY_CHUNK, a.shape[2])

    ckv_c = key_chunks(m_ckv, ckv)
    wrow = jnp.swapaxes(gates[:, :, :SUBLANES], 1, 2)
    y_att = _dsa_call(qlat, qidx, wrow, ckv_c, jnp.swapaxes(ckv_c, 2, 3), key_chunks(m_kidx, kidx), wuv_bd, topk)

    lane = jnp.arange(LANES)
    pad_gate = jnp.where((lane >= GATE_I0) & (lane < GATE_F0), NEG_BIG, 0.0).astype(F32)
    mg = jnp.concatenate([m_gates[0], jnp.broadcast_to(pad_gate, (ML_CHUNK - N_META, LANES))], axis=0)
    y_ml = _mlstm_call(qk, v, og, gates, _pad_rows(m_qk[0], ML_CHUNK), _pad_rows(m_v[0], ML_CHUNK), mg,
                       row2(ml_norm_g))

    out = _out_ffn_ln_call(
        y_att.reshape(bsz * seq, ATT_WIDTH), y_ml.reshape(bsz * seq, ML_WIDTH), h1, bf(w_out),
        row2(ln2_g), row2(ln2_b), bf(ffn2_w_gate), bf(ffn2_w_up), bf(ffn2_w_down), row2(ln3_g), row2(ln3_b),
        alpha, ROW_TILE)
    return out.reshape(bsz, seq, d)
```

```python
import functools

import jax
import jax.numpy as jnp
from jax import lax
from jax.experimental import pallas as pl
from jax.experimental.pallas import tpu as pltpu

F32 = jnp.float32
BF16 = jnp.bfloat16

N_META = 16
ATT_HEADS = 8
ATT_HEAD_DIM = 64
KV_LATENT = 128
IDX_HEADS = 4
IDX_DIM = 64
TOPK_MAX = 256
ML_HEADS = 4
ML_V_DIM = 128
ML_QK_DIM = 64
CONV_WIDTH = 4
GATE_SOFTCAP = 15.0
M_INIT = -1e30
LN_EPS = 1e-5
NEG_BIG = -1e30
LOG2_E = 1.4426950408889634

LANES = 128
SUBLANES = 8
VMEM_LIMIT_BYTES = 56 * 1024 * 1024

FF_CHUNK = 256
ROW_TILE = 512
Q_TILE = 256
ATT_GROUP = 128
KEY_CHUNK = 256
N_BISECT = 22
REDUCE_ROWS = 32
ML_CHUNK = 128


def _dot(a, b):
    return jnp.dot(a, b, preferred_element_type=F32)


def _dot_nt(a, b):
    return lax.dot_general(a, b, (((1,), (1,)), ((), ())), preferred_element_type=F32)


def _layer_norm(z, g, b):
    mu = jnp.mean(z, axis=-1, keepdims=True)
    zc = z - mu
    var = jnp.mean(zc * zc, axis=-1, keepdims=True)
    return zc * lax.rsqrt(var + LN_EPS) * g + b


def _sigmoid(x):
    return 1.0 / (1.0 + jnp.exp(-x))


def _swiglu(xb, wg_ref, wu_ref, wd_ref):
    d_ff = wg_ref.shape[1]
    acc = jnp.zeros((xb.shape[0], wd_ref.shape[1]), F32)
    for c in range(d_ff // FF_CHUNK):
        sl = slice(c * FF_CHUNK, (c + 1) * FF_CHUNK)
        g = _dot(xb, wg_ref[:, sl])
        u = _dot(xb, wu_ref[:, sl])
        a = (g * _sigmoid(g) * u).astype(BF16)
        acc = acc + _dot(a, wd_ref[sl, :])
    return acc


def _ffn_ln(x, wg_ref, wu_ref, wd_ref, g, b, alpha):
    y = _swiglu(x.astype(BF16), wg_ref, wu_ref, wd_ref)
    return _layer_norm(alpha * x + 0.5 * y, g, b)


def _ffn_ln_kernel(x_ref, wg_ref, wu_ref, wd_ref, g_ref, b_ref, o_ref, *, alpha):
    o_ref[...] = _ffn_ln(x_ref[...], wg_ref, wu_ref, wd_ref, g_ref[...], b_ref[...], alpha)


def _resident(shape):
    return pl.BlockSpec(shape, lambda *_: (0,) * len(shape), pipeline_mode=pl.Buffered(1))


def _ffn_ln_call(x, wg, wu, wd, g, b, alpha, tm):
    rows, d = x.shape
    return pl.pallas_call(
        functools.partial(_ffn_ln_kernel, alpha=alpha),
        grid=(rows // tm,),
        in_specs=[
            pl.BlockSpec((tm, d), lambda i: (i, 0)),
            _resident(wg.shape), _resident(wu.shape), _resident(wd.shape),
            _resident(g.shape), _resident(b.shape),
        ],
        out_specs=pl.BlockSpec((tm, d), lambda i: (i, 0)),
        out_shape=jax.ShapeDtypeStruct((rows, d), F32),
        compiler_params=pltpu.CompilerParams(
            dimension_semantics=("arbitrary",), vmem_limit_bytes=VMEM_LIMIT_BYTES),
        name="ffn_ln",
    )(x, wg, wu, wd, g, b)


ATT_WIDTH = ATT_HEADS * ATT_HEAD_DIM
IDX_WIDTH = IDX_HEADS * IDX_DIM
MLQK_WIDTH = 2 * ML_HEADS * ML_QK_DIM
ML_WIDTH = ML_HEADS * ML_V_DIM
LAT_WIDTH = ATT_HEADS * KV_LATENT
CONV_HIST = SUBLANES
GATE_W0, GATE_I0, GATE_F0, GATE_END = 0, IDX_HEADS, IDX_HEADS + ML_HEADS, IDX_HEADS + 2 * ML_HEADS


def _inproj_kernel(h_ref, tail_ref, wa_ref, wm_ref, wgt_ref, wuk_ref, kvg_ref, convw_ref, gbias_ref,
                   qlat_ref, ckv_ref, qidx_ref, kidx_ref, qk_ref, v_ref, og_ref, gates_ref, tailout_ref,
                   carry_ref):
    tm = h_ref.shape[1]

    @pl.when(pl.program_id(1) == 0)
    def _():
        carry_ref[...] = tail_ref[...]

    xb = h_ref[0].astype(BF16)

    pa = _dot(xb, wa_ref[...])
    q_a = pa[:, :ATT_WIDTH].astype(BF16)
    c0 = ATT_WIDTH
    ckv = pa[:, c0:c0 + KV_LATENT]
    c1 = c0 + KV_LATENT
    ckv = ckv * lax.rsqrt(jnp.mean(ckv * ckv, axis=-1, keepdims=True) + LN_EPS) * kvg_ref[...]
    ckv_ref[0] = ckv.astype(BF16)
    qidx_ref[0] = pa[:, c1:c1 + IDX_WIDTH].astype(BF16)
    c2 = c1 + IDX_WIDTH
    kidx_ref[0] = pa[:, c2:c2 + IDX_DIM].astype(BF16)
    qlat_ref[0] = (_dot(q_a, wuk_ref[...]) * (ATT_HEAD_DIM ** -0.5 * LOG2_E)).astype(BF16)

    pm = _dot(xb, wm_ref[...])
    qk_raw = pm[:, :MLQK_WIDTH]
    v_ref[0] = pm[:, MLQK_WIDTH:MLQK_WIDTH + ML_WIDTH].astype(BF16)
    og_ref[0] = _sigmoid(pm[:, MLQK_WIDTH + ML_WIDTH:])

    ext = jnp.concatenate([carry_ref[...], qk_raw], axis=0)
    cw = convw_ref[...]
    conv = jnp.zeros_like(qk_raw)
    for j in range(CONV_WIDTH):
        s0 = CONV_HIST - (CONV_WIDTH - 1) + j
        conv = conv + ext[s0:s0 + tm] * cw[j:j + 1]
    act = conv * _sigmoid(conv)
    half = MLQK_WIDTH // 2
    qk_ref[0, :, :half] = act[:, :half].astype(BF16)
    qk_ref[0, :, half:] = (act[:, half:] * (ML_QK_DIM ** -0.5)).astype(BF16)
    carry_ref[...] = qk_raw[tm - CONV_HIST:]
    tailout_ref[0] = qk_raw[tm - CONV_HIST:]

    gr = _dot(xb, wgt_ref[...])
    lane = lax.broadcasted_iota(jnp.int32, gr.shape, 1)
    sc = GATE_SOFTCAP * jnp.tanh((gr + gbias_ref[...]) / GATE_SOFTCAP)
    lf = -(jnp.maximum(-sc, 0.0) + jnp.log1p(jnp.exp(-jnp.abs(sc))))
    w_scaled = gr * (IDX_HEADS ** -0.5 * IDX_DIM ** -0.5)
    gates_ref[0] = jnp.where(lane < GATE_I0, w_scaled,
                             jnp.where(lane < GATE_F0, sc, jnp.where(lane < GATE_END, lf, 0.0)))


def _inproj_call(h, tail, wa, wm, wgt, wuk_bd, kvg, convw, gbias, tm):
    bn, rows, d = h.shape
    nblk = rows // tm

    def row_spec(width):
        return pl.BlockSpec((1, tm, width), lambda b, j: (b, j, 0))

    outs = [
        (LAT_WIDTH, BF16), (KV_LATENT, BF16), (IDX_WIDTH, BF16), (IDX_DIM, BF16),
        (MLQK_WIDTH, BF16), (ML_WIDTH, BF16), (ML_WIDTH, F32), (LANES, F32),
    ]
    out_shape = [jax.ShapeDtypeStruct((bn, rows, w), dt) for w, dt in outs]
    out_specs = [row_spec(w) for w, _ in outs]
    out_shape.append(jax.ShapeDtypeStruct((bn, CONV_HIST, MLQK_WIDTH), F32))
    out_specs.append(pl.BlockSpec((1, CONV_HIST, MLQK_WIDTH), lambda b, j: (b, 0, 0)))
    return pl.pallas_call(
        _inproj_kernel,
        grid=(bn, nblk),
        in_specs=[
            row_spec(d),
            _resident(tail.shape), _resident(wa.shape), _resident(wm.shape), _resident(wgt.shape),
            _resident(wuk_bd.shape), _resident(kvg.shape), _resident(convw.shape), _resident(gbias.shape),
        ],
        out_specs=out_specs,
        out_shape=out_shape,
        scratch_shapes=[pltpu.VMEM((CONV_HIST, MLQK_WIDTH), F32)],
        compiler_params=pltpu.CompilerParams(
            dimension_semantics=("arbitrary", "arbitrary"), vmem_limit_bytes=VMEM_LIMIT_BYTES),
        name="in_proj",
    )(h, tail, wa, wm, wgt, wuk_bd, kvg, convw, gbias)


def _dsa_kernel(qlat_ref, qidx_ref, wrow_ref, ckv_ref, ckvt_ref, kidx_ref, mckv_ref, mckvt_ref, mkidx_ref,
                wuv_ref, y_ref, s_ref, *, topk):
    _, kc, tq = s_ref.shape
    i = pl.program_id(1)
    nch = ((i + 1) * tq + kc - 1) // kc
    qreal = i * tq + lax.broadcasted_iota(jnp.int32, (1, tq), 1)
    kf = float(topk)

    wrow = wrow_ref[0]
    qidx = qidx_ref[0]
    q_idx_all = jnp.concatenate([qidx[:, h * IDX_DIM:(h + 1) * IDX_DIM] for h in range(IDX_HEADS)], axis=0)
    wi = [wrow[GATE_W0 + h:GATE_W0 + h + 1, :] for h in range(IDX_HEADS)]

    def scores(k_rows):
        lg = _dot_nt(k_rows, q_idx_all)
        sc = jnp.zeros((k_rows.shape[0], tq), F32)
        for h in range(IDX_HEADS):
            sc = sc + jnp.maximum(lg[:, h * tq:(h + 1) * tq], 0.0) * wi[h]
        return sc

    s_meta = scores(mkidx_ref[...])

    def score_chunk(c, lo, hi):
        sc = scores(kidx_ref[0, c])
        valid = c * kc + lax.broadcasted_iota(jnp.int32, (kc, tq), 0) <= qreal
        s_ref[c] = jnp.where(valid, sc, -jnp.inf)
        groups = (kc // REDUCE_ROWS, REDUCE_ROWS, tq)
        lo = jnp.minimum(lo, jnp.min(jnp.where(valid, sc, jnp.inf).reshape(groups), axis=0))
        hi = jnp.maximum(hi, jnp.max(jnp.where(valid, sc, -jnp.inf).reshape(groups), axis=0))
        return lo, hi

    def score_pair(c2, carry):
        lo, hi = score_chunk(2 * c2, *carry)
        return score_chunk(jnp.minimum(2 * c2 + 1, nch - 1), lo, hi)

    lo, hi = lax.fori_loop(0, (nch + 1) // 2, score_pair,
                           (jnp.full((REDUCE_ROWS, tq), jnp.inf, F32), jnp.full((REDUCE_ROWS, tq), -jnp.inf, F32)))
    lo = jnp.minimum(jnp.min(lo, axis=0, keepdims=True), jnp.min(s_meta, axis=0, keepdims=True))
    hi = jnp.maximum(jnp.max(hi, axis=0, keepdims=True), jnp.max(s_meta, axis=0, keepdims=True))

    def key_reduce(reduce, combine, per_chunk, init):
        def body(c, acc):
            x = per_chunk(s_ref[c]).reshape(kc // REDUCE_ROWS, REDUCE_ROWS, tq)
            return combine(acc, reduce(x, axis=0))
        acc = lax.fori_loop(0, nch, body, jnp.full((REDUCE_ROWS, tq), init, F32))
        return combine(reduce(acc, axis=0, keepdims=True), reduce(per_chunk(s_meta), axis=0, keepdims=True))

    def count(pred):
        return key_reduce(jnp.sum, jnp.add, lambda sc: jnp.where(pred(sc), 1.0, 0.0), 0.0)

    def max_where(pred):
        return key_reduce(jnp.max, jnp.maximum, lambda sc: jnp.where(pred(sc), sc, -jnp.inf), -jnp.inf)

    def bisect(_, carry):
        lo, hi = carry
        mid = 0.5 * lo + 0.5 * hi
        up = count(lambda sc: sc > mid) >= kf
        return jnp.where(up, mid, lo), jnp.where(up, hi, mid)

    lo, hi = lax.fori_loop(0, N_BISECT, bisect, (lo, hi))

    n_valid = (qreal + (N_META + 1)).astype(F32)
    small = n_valid <= kf
    cand = max_where(lambda sc: sc <= hi)
    done = jnp.where(small | (count(lambda sc: sc >= cand) >= kf), 1.0, 0.0)

    def not_finished(state):
        return jnp.min(state[1]) < 0.5

    def step_down(state):
        cand, done = state
        nxt = jnp.where(done > 0.5, cand, max_where(lambda sc: sc < cand))
        fin = count(lambda sc: sc >= nxt) >= kf
        return nxt, jnp.where(fin, 1.0, done)

    cand, _ = lax.while_loop(not_finished, step_down, (cand, done))
    thr = jnp.where(small, -jnp.inf, cand)
    n_gt = count(lambda sc: sc > thr)
    need = jnp.where(small, 0.0, kf - n_gt)

    qlat = qlat_ref[0]
    n_groups = tq // ATT_GROUP
    onehot = (lax.broadcasted_iota(jnp.int32, (ATT_GROUP, ATT_GROUP), 0)
              == lax.broadcasted_iota(jnp.int32, (ATT_GROUP, ATT_GROUP), 1)).astype(BF16)
    q_aug = [jnp.concatenate(
        [jnp.concatenate([qlat[g * ATT_GROUP:(g + 1) * ATT_GROUP, h * KV_LATENT:(h + 1) * KV_LATENT], onehot], axis=1)
         for h in range(ATT_HEADS)], axis=0) for g in range(n_groups)]
    hq = ATT_HEADS * ATT_GROUP

    def lower_tri(n):
        return (lax.broadcasted_iota(jnp.int32, (n, n), 1) <= lax.broadcasted_iota(jnp.int32, (n, n), 0)).astype(BF16)

    def mask_bias(sc, eq_seen):
        n = sc.shape[0]
        eq = sc == thr
        rank = _dot(lower_tri(n), jnp.where(eq, 1.0, 0.0).astype(BF16)) + eq_seen
        keep = (sc > thr) | (eq & (rank <= need))
        return jnp.where(keep, 0.0, NEG_BIG).astype(BF16), rank[n - 1:n, :]

    def update(state, kv, kvt, bias):
        new = []
        for g in range(n_groups):
            m, l, acc = state[g]
            k_aug = jnp.concatenate([kv, bias[:, g * ATT_GROUP:(g + 1) * ATT_GROUP]], axis=1)
            s = _dot_nt(k_aug, q_aug[g])
            m_new = jnp.maximum(m, jnp.max(s, axis=0, keepdims=True))
            a = jnp.exp2(m - m_new)
            p = jnp.exp2(s - m_new)
            new.append((m_new, a * l + jnp.sum(p, axis=0, keepdims=True), a * acc + _dot(kvt, p.astype(BF16))))
        return new

    def attend(c, carry):
        bias, eq_seen = mask_bias(s_ref[c], carry[0])
        state = [tuple(carry[1 + 3 * g:4 + 3 * g]) for g in range(n_groups)]
        state = update(state, ckv_ref[0, c], ckvt_ref[0, c], bias)
        return (eq_seen,) + tuple(x for st in state for x in st)

    state = [(jnp.full((1, hq), NEG_BIG, F32), jnp.zeros((1, hq), F32), jnp.zeros((KV_LATENT, hq), F32))
             for _ in range(n_groups)]
    bias, eq_seen = mask_bias(s_meta, jnp.zeros((1, tq), F32))
    state = update(state, mckv_ref[...], mckvt_ref[...], bias)
    carry = lax.fori_loop(0, nch, attend, (eq_seen,) + tuple(x for st in state for x in st))
    rows = []
    for g in range(n_groups):
        _, l, acc = carry[1 + 3 * g:4 + 3 * g]
        o = acc / l
        rows.append(jnp.concatenate([o[:, h * ATT_GROUP:(h + 1) * ATT_GROUP].T for h in range(ATT_HEADS)], axis=1))
    y_ref[0] = _dot(jnp.concatenate(rows, axis=0).astype(BF16), wuv_ref[...]).astype(BF16)


def _dsa_call(qlat, qidx, wrow, ckv_c, ckvt_c, kidx_c, m_ckv, m_ckvt, m_kidx, wuv_bd, topk):
    bn, rows, _ = qlat.shape
    nchunks, kc = ckv_c.shape[1], ckv_c.shape[2]
    nq = rows // Q_TILE

    def q_spec(width):
        return pl.BlockSpec((1, Q_TILE, width), lambda b, i: (b, i, 0))

    def k_spec(a):
        return pl.BlockSpec((1,) + a.shape[1:], lambda b, i: (b, 0, 0, 0))

    return pl.pallas_call(
        functools.partial(_dsa_kernel, topk=topk),
        grid=(bn, nq),
        in_specs=[q_spec(LAT_WIDTH), q_spec(IDX_WIDTH),
                  pl.BlockSpec((1, SUBLANES, Q_TILE), lambda b, i: (b, 0, i)),
                  k_spec(ckv_c), k_spec(ckvt_c), k_spec(kidx_c),
                  _resident(m_ckv.shape), _resident(m_ckvt.shape), _resident(m_kidx.shape), _resident(wuv_bd.shape)],
        out_specs=q_spec(ATT_WIDTH),
        out_shape=jax.ShapeDtypeStruct((bn, rows, ATT_WIDTH), BF16),
        scratch_shapes=[pltpu.VMEM((nchunks, kc, Q_TILE), F32)],
        compiler_params=pltpu.CompilerParams(
            dimension_semantics=("arbitrary", "arbitrary"), vmem_limit_bytes=VMEM_LIMIT_BYTES),
        name="dsa",
    )(qlat, qidx, wrow, ckv_c, ckvt_c, kidx_c, m_ckv, m_ckvt, m_kidx, wuv_bd)


def _split3(x):
    hi = x.astype(BF16)
    r = x - hi.astype(F32)
    mid = r.astype(BF16)
    lo = (r - mid.astype(F32)).astype(BF16)
    return hi, mid, lo


def _mlstm_chunk(qk, v, g, state, norm_g):
    L = qk.shape[0]
    row = lax.broadcasted_iota(jnp.int32, (L, L), 0)
    col = lax.broadcasted_iota(jnp.int32, (L, L), 1)
    causal = col <= row
    tri = causal.astype(BF16)
    b_all = sum(_dot(tri, part) for part in _split3(g))
    g_t = g.T
    b_t = b_all.T
    ones_col = jnp.where(lax.broadcasted_iota(jnp.int32, (L, ML_V_DIM), 1) == 0, 1.0, 0.0).astype(BF16)
    kq = ML_HEADS * ML_QK_DIM

    outs, new_state, kws, vexts, decays = [], [], [], [], []
    for h in range(ML_HEADS):
        ce, m_prev = state[h]
        ig_c = g[:, GATE_I0 + h:GATE_I0 + h + 1]
        b_c = b_all[:, GATE_F0 + h:GATE_F0 + h + 1]
        ig_r = g_t[GATE_I0 + h:GATE_I0 + h + 1, :]
        b_r = b_t[GATE_F0 + h:GATE_F0 + h + 1, :]
        qh = qk[:, h * ML_QK_DIM:(h + 1) * ML_QK_DIM]
        kh = qk[:, kq + h * ML_QK_DIM:kq + (h + 1) * ML_QK_DIM]
        vext = jnp.concatenate([v[:, h * ML_V_DIM:(h + 1) * ML_V_DIM], ones_col], axis=1)

        d = jnp.where(causal, b_c - b_r + ig_r, -jnp.inf)
        inter = b_c + m_prev
        m_t = jnp.maximum(jnp.max(d, axis=1, keepdims=True), inter)
        w_inter = jnp.exp(inter - m_t)
        s = _dot_nt(qh, kh) * jnp.exp(d - m_t)
        r = _dot(s.astype(BF16), vext) + w_inter * _dot(qh, ce.astype(BF16))
        num = r[:, :ML_V_DIM]
        den = r[:, ML_V_DIM:ML_V_DIM + 1]
        hh = num / jnp.maximum(jnp.abs(den), jnp.exp(-m_t))
        mu = jnp.mean(hh, axis=-1, keepdims=True)
        hc = hh - mu
        var = jnp.mean(hc * hc, axis=-1, keepdims=True)
        outs.append(hc * lax.rsqrt(var + LN_EPS) * norm_g[:, h * ML_V_DIM:(h + 1) * ML_V_DIM])

        b_end = b_c[L - 1:L]
        g_c = b_end - b_c + ig_c
        g_r = b_end - b_r + ig_r
        m_new = jnp.maximum(b_end + m_prev, jnp.max(g_r, axis=1, keepdims=True))
        decays.append(jnp.exp(b_end + m_prev - m_new))
        kws.append(kh.astype(F32) * jnp.exp(g_c - m_new))
        vexts.append(vext)
        new_state.append(m_new)

    kw_t = jnp.concatenate(kws, axis=1).T
    for h in range(ML_HEADS):
        ce, _ = state[h]
        upd = _dot(kw_t[h * ML_QK_DIM:(h + 1) * ML_QK_DIM].astype(BF16), vexts[h])
        new_state[h] = (decays[h] * ce + upd, new_state[h])
    return outs, new_state


def _mlstm_kernel(qk_ref, v_ref, og_ref, gates_ref, mqk_ref, mv_ref, mgates_ref, ng_ref, y_ref):
    L = mqk_ref.shape[0]
    n_chunks = qk_ref.shape[1] // L
    norm_g = ng_ref[...]
    state = [(jnp.zeros((ML_QK_DIM, 2 * ML_V_DIM), F32), jnp.full((1, 1), M_INIT, F32))
             for _ in range(ML_HEADS)]
    _, state = _mlstm_chunk(mqk_ref[...], mv_ref[...], mgates_ref[...], state, norm_g)

    def body(c, flat):
        state = [(flat[2 * h], flat[2 * h + 1]) for h in range(ML_HEADS)]
        rows = pl.ds(pl.multiple_of(c * L, L), L)
        outs, state = _mlstm_chunk(qk_ref[0, rows, :], v_ref[0, rows, :], gates_ref[0, rows, :], state, norm_g)
        og = og_ref[0, rows, :]
        y = jnp.concatenate(outs, axis=1) * og
        y_ref[0, rows, :] = y.astype(BF16)
        return tuple(x for pair in state for x in pair)

    lax.fori_loop(0, n_chunks, body, tuple(x for pair in state for x in pair))


def _mlstm_call(qk, v, og, gates, mqk, mv, mgates, norm_g):
    bn, rows, _ = qk.shape

    def b_spec(width):
        return pl.BlockSpec((1, rows, width), lambda b: (b, 0, 0))

    return pl.pallas_call(
        _mlstm_kernel,
        grid=(bn,),
        in_specs=[b_spec(MLQK_WIDTH), b_spec(ML_WIDTH), b_spec(ML_WIDTH), b_spec(LANES),
                  _resident(mqk.shape), _resident(mv.shape), _resident(mgates.shape), _resident(norm_g.shape)],
        out_specs=b_spec(ML_WIDTH),
        out_shape=jax.ShapeDtypeStruct((bn, rows, ML_WIDTH), BF16),
        compiler_params=pltpu.CompilerParams(
            dimension_semantics=("arbitrary",), vmem_limit_bytes=VMEM_LIMIT_BYTES),
        name="mlstm",
    )(qk, v, og, gates, mqk, mv, mgates, norm_g)


def _out_ffn_ln_kernel(ya_ref, ym_ref, h_ref, wo_ref, g2_ref, b2_ref, wg_ref, wu_ref, wd_ref, g3_ref, b3_ref,
                       o_ref, *, alpha):
    mix = _dot(ya_ref[...], wo_ref[:ATT_WIDTH, :]) + _dot(ym_ref[...], wo_ref[ATT_WIDTH:, :])
    h2 = _layer_norm(alpha * h_ref[...] + mix, g2_ref[...], b2_ref[...])
    o_ref[...] = _ffn_ln(h2, wg_ref, wu_ref, wd_ref, g3_ref[...], b3_ref[...], alpha)


def _out_ffn_ln_call(ya, ym, h, wo, g2, b2, wg, wu, wd, g3, b3, alpha, tm):
    rows, d = h.shape

    def row_spec(width):
        return pl.BlockSpec((tm, width), lambda i: (i, 0))

    consts = (wo, g2, b2, wg, wu, wd, g3, b3)
    return pl.pallas_call(
        functools.partial(_out_ffn_ln_kernel, alpha=alpha),
        grid=(rows // tm,),
        in_specs=[row_spec(ya.shape[1]), row_spec(ym.shape[1]), row_spec(d)] + [_resident(c.shape) for c in consts],
        out_specs=row_spec(d),
        out_shape=jax.ShapeDtypeStruct((rows, d), F32),
        compiler_params=pltpu.CompilerParams(
            dimension_semantics=("arbitrary",), vmem_limit_bytes=VMEM_LIMIT_BYTES),
        name="out_ffn_ln",
    )(ya, ym, h, *consts)


def _block_diag(w):
    nh, a, b = w.shape
    eye = jnp.eye(nh, dtype=w.dtype)
    return (eye[:, None, :, None] * w[:, :, None, :]).reshape(nh * a, nh * b)


def _pad_rows(a, rows, value=0.0):
    return jnp.pad(a, ((0, rows - a.shape[0]), (0, 0)), constant_values=value)


def kernel(x, meta_tokens, ln1_g, ln1_b, ffn1_w_gate, ffn1_w_up, ffn1_w_down, w_in, w_uk, w_uv, kv_norm_g,
           conv_w, b_igate, b_fgate, ml_norm_g, w_out, ln2_g, ln2_b, ffn2_w_gate, ffn2_w_up, ffn2_w_down,
           ln3_g, ln3_b):
    depth = ln1_g.shape[0]
    assert depth == 1, "the meta-token shortcut below is only valid for a single layer"
    bsz, seq, d = x.shape
    assert seq % ROW_TILE == 0 and seq % ML_CHUNK == 0 and seq % Q_TILE == 0
    alpha = (2 * depth) ** 0.25
    topk = min(TOPK_MAX, seq // 4)

    row2 = lambda p: p[0].reshape(1, -1).astype(F32)
    bf = lambda w: w[0].astype(BF16)

    w = w_in[0]
    o_qa, o_ckv, o_qi, o_ki, o_wi = 0, 512, 640, 896, 960
    o_qk, o_v, o_o, o_ig, o_fg, o_end = 964, 1476, 1988, 2500, 2504, 2508
    wa = jnp.concatenate([w[:, o_qa:o_wi], jnp.zeros((d, 1024 - o_wi), F32)], axis=1).astype(BF16)
    wm = w[:, o_qk:o_ig].astype(BF16)
    wgt = jnp.concatenate([w[:, o_wi:o_qk], w[:, o_ig:o_end], jnp.zeros((d, LANES - GATE_END), F32)],
                          axis=1).astype(BF16)
    gbias = jnp.concatenate([jnp.zeros((IDX_HEADS,), F32), b_igate[0], b_fgate[0],
                             jnp.zeros((LANES - GATE_END,), F32)]).reshape(1, LANES)
    wuk_bd = _block_diag(w_uk[0]).astype(BF16)
    wuv_bd = _block_diag(w_uv[0]).astype(BF16)
    kvg = row2(kv_norm_g)
    convw = conv_w[0].astype(F32)
    ffn1 = (bf(ffn1_w_gate), bf(ffn1_w_up), bf(ffn1_w_down), row2(ln1_g), row2(ln1_b))

    meta = meta_tokens.astype(F32)
    h1_meta = _ffn_ln_call(meta, *ffn1, alpha, N_META)
    zero_tail = jnp.zeros((CONV_HIST, MLQK_WIDTH), F32)
    (_, m_ckv, _, m_kidx, m_qk, m_v, _, m_gates, m_tail) = _inproj_call(
        h1_meta[None], zero_tail, wa, wm, wgt, wuk_bd, kvg, convw, gbias, N_META)

    h1 = _ffn_ln_call(x.reshape(bsz * seq, d), *ffn1, alpha, ROW_TILE)
    (qlat, ckv, qidx, kidx, qk, v, og, gates, _) = _inproj_call(
        h1.reshape(bsz, seq, d), m_tail[0], wa, wm, wgt, wuk_bd, kvg, convw, gbias, ROW_TILE)

    nchunks = seq // KEY_CHUNK
    ckv_c = ckv.reshape(bsz, nchunks, KEY_CHUNK, KV_LATENT)
    kidx_c = kidx.reshape(bsz, nchunks, KEY_CHUNK, IDX_DIM)
    wrow = jnp.swapaxes(gates[:, :, :SUBLANES], 1, 2)
    y_att = _dsa_call(qlat, qidx, wrow, ckv_c, jnp.swapaxes(ckv_c, 2, 3), kidx_c,
                      m_ckv[0], m_ckv[0].T, m_kidx[0], wuv_bd, topk)

    lane = jnp.arange(LANES)
    pad_gate = jnp.where((lane >= GATE_I0) & (lane < GATE_F0), NEG_BIG, 0.0).astype(F32)
    mg = jnp.concatenate([m_gates[0], jnp.broadcast_to(pad_gate, (ML_CHUNK - N_META, LANES))], axis=0)
    y_ml = _mlstm_call(qk, v, og, gates, _pad_rows(m_qk[0], ML_CHUNK), _pad_rows(m_v[0], ML_CHUNK), mg,
                       row2(ml_norm_g))

    out = _out_ffn_ln_call(
        y_att.reshape(bsz * seq, ATT_WIDTH), y_ml.reshape(bsz * seq, ML_WIDTH), h1, bf(w_out),
        row2(ln2_g), row2(ln2_b), bf(ffn2_w_gate), bf(ffn2_w_up), bf(ffn2_w_down), row2(ln3_g), row2(ln3_b),
        alpha, ROW_TILE)
    return out.reshape(bsz, seq, d)
```

```python
import functools

import jax
import jax.numpy as jnp
from jax import lax
from jax.experimental import pallas as pl
from jax.experimental.pallas import tpu as pltpu

F32 = jnp.float32
BF16 = jnp.bfloat16

N_META = 16
ATT_HEADS = 8
ATT_HEAD_DIM = 64
KV_LATENT = 128
IDX_HEADS = 4
IDX_DIM = 64
TOPK_MAX = 256
ML_HEADS = 4
ML_V_DIM = 128
ML_QK_DIM = 64
CONV_WIDTH = 4
GATE_SOFTCAP = 15.0
M_INIT = -1e30
LN_EPS = 1e-5
NEG_BIG = -1e30
LOG2_E = 1.4426950408889634

LANES = 128
SUBLANES = 8
VMEM_LIMIT_BYTES = 56 * 1024 * 1024

FF_CHUNK = 256
ROW_TILE = 512
Q_TILE = 256
ATT_GROUP = 128
KEY_CHUNK = 256
N_BISECT = 16
REDUCE_ROWS = 32
ML_CHUNK = 128


def _dot(a, b):
    return jnp.dot(a, b, preferred_element_type=F32)


def _dot_nt(a, b):
    return lax.dot_general(a, b, (((1,), (1,)), ((), ())), preferred_element_type=F32)


def _layer_norm(z, g, b):
    mu = jnp.mean(z, axis=-1, keepdims=True)
    zc = z - mu
    var = jnp.mean(zc * zc, axis=-1, keepdims=True)
    return zc * lax.rsqrt(var + LN_EPS) * g + b


def _sigmoid(x):
    return 1.0 / (1.0 + jnp.exp(-x))


def _swiglu(xb, wg_ref, wu_ref, wd_ref):
    d_ff = wg_ref.shape[1]
    acc = jnp.zeros((xb.shape[0], wd_ref.shape[1]), F32)
    for c in range(d_ff // FF_CHUNK):
        sl = slice(c * FF_CHUNK, (c + 1) * FF_CHUNK)
        g = _dot(xb, wg_ref[:, sl])
        u = _dot(xb, wu_ref[:, sl])
        a = (g * _sigmoid(g) * u).astype(BF16)
        acc = acc + _dot(a, wd_ref[sl, :])
    return acc


def _ffn_ln(x, wg_ref, wu_ref, wd_ref, g, b, alpha):
    y = _swiglu(x.astype(BF16), wg_ref, wu_ref, wd_ref)
    return _layer_norm(alpha * x + 0.5 * y, g, b)


def _ffn_ln_kernel(x_ref, wg_ref, wu_ref, wd_ref, g_ref, b_ref, o_ref, *, alpha):
    o_ref[...] = _ffn_ln(x_ref[...], wg_ref, wu_ref, wd_ref, g_ref[...], b_ref[...], alpha)


def _resident(shape):
    return pl.BlockSpec(shape, lambda *_: (0,) * len(shape), pipeline_mode=pl.Buffered(1))


def _ffn_ln_call(x, wg, wu, wd, g, b, alpha, tm):
    rows, d = x.shape
    return pl.pallas_call(
        functools.partial(_ffn_ln_kernel, alpha=alpha),
        grid=(rows // tm,),
        in_specs=[
            pl.BlockSpec((tm, d), lambda i: (i, 0)),
            _resident(wg.shape), _resident(wu.shape), _resident(wd.shape),
            _resident(g.shape), _resident(b.shape),
        ],
        out_specs=pl.BlockSpec((tm, d), lambda i: (i, 0)),
        out_shape=jax.ShapeDtypeStruct((rows, d), F32),
        compiler_params=pltpu.CompilerParams(
            dimension_semantics=("arbitrary",), vmem_limit_bytes=VMEM_LIMIT_BYTES),
        name="ffn_ln",
    )(x, wg, wu, wd, g, b)


ATT_WIDTH = ATT_HEADS * ATT_HEAD_DIM
IDX_WIDTH = IDX_HEADS * IDX_DIM
MLQK_WIDTH = 2 * ML_HEADS * ML_QK_DIM
ML_WIDTH = ML_HEADS * ML_V_DIM
LAT_WIDTH = ATT_HEADS * KV_LATENT
CONV_HIST = SUBLANES
GATE_W0, GATE_I0, GATE_F0, GATE_END = 0, IDX_HEADS, IDX_HEADS + ML_HEADS, IDX_HEADS + 2 * ML_HEADS


def _inproj_kernel(h_ref, tail_ref, wa_ref, wm_ref, wgt_ref, wuk_ref, kvg_ref, convw_ref, gbias_ref,
                   qlat_ref, ckv_ref, qidx_ref, kidx_ref, qk_ref, v_ref, og_ref, gates_ref, tailout_ref,
                   carry_ref):
    tm = h_ref.shape[1]

    @pl.when(pl.program_id(1) == 0)
    def _():
        carry_ref[...] = tail_ref[...]

    xb = h_ref[0].astype(BF16)

    pa = _dot(xb, wa_ref[...])
    q_a = pa[:, :ATT_WIDTH].astype(BF16)
    c0 = ATT_WIDTH
    ckv = pa[:, c0:c0 + KV_LATENT]
    c1 = c0 + KV_LATENT
    ckv = ckv * lax.rsqrt(jnp.mean(ckv * ckv, axis=-1, keepdims=True) + LN_EPS) * kvg_ref[...]
    ckv_ref[0] = ckv.astype(BF16)
    qidx_ref[0] = pa[:, c1:c1 + IDX_WIDTH].astype(BF16)
    c2 = c1 + IDX_WIDTH
    kidx_ref[0] = pa[:, c2:c2 + IDX_DIM].astype(BF16)
    qlat_ref[0] = (_dot(q_a, wuk_ref[...]) * (ATT_HEAD_DIM ** -0.5 * LOG2_E)).astype(BF16)

    pm = _dot(xb, wm_ref[...])
    qk_raw = pm[:, :MLQK_WIDTH]
    v_ref[0] = pm[:, MLQK_WIDTH:MLQK_WIDTH + ML_WIDTH].astype(BF16)
    og_ref[0] = _sigmoid(pm[:, MLQK_WIDTH + ML_WIDTH:])

    ext = jnp.concatenate([carry_ref[...], qk_raw], axis=0)
    cw = convw_ref[...]
    conv = jnp.zeros_like(qk_raw)
    for j in range(CONV_WIDTH):
        s0 = CONV_HIST - (CONV_WIDTH - 1) + j
        conv = conv + ext[s0:s0 + tm] * cw[j:j + 1]
    act = conv * _sigmoid(conv)
    half = MLQK_WIDTH // 2
    qk_ref[0, :, :half] = act[:, :half].astype(BF16)
    qk_ref[0, :, half:] = (act[:, half:] * (ML_QK_DIM ** -0.5)).astype(BF16)
    carry_ref[...] = qk_raw[tm - CONV_HIST:]
    tailout_ref[0] = qk_raw[tm - CONV_HIST:]

    gr = _dot(xb, wgt_ref[...])
    lane = lax.broadcasted_iota(jnp.int32, gr.shape, 1)
    sc = GATE_SOFTCAP * jnp.tanh((gr + gbias_ref[...]) / GATE_SOFTCAP)
    lf = -(jnp.maximum(-sc, 0.0) + jnp.log1p(jnp.exp(-jnp.abs(sc))))
    w_scaled = gr * (IDX_HEADS ** -0.5 * IDX_DIM ** -0.5)
    gates_ref[0] = jnp.where(lane < GATE_I0, w_scaled,
                             jnp.where(lane < GATE_F0, sc, jnp.where(lane < GATE_END, lf, 0.0)))


def _inproj_call(h, tail, wa, wm, wgt, wuk_bd, kvg, convw, gbias, tm):
    bn, rows, d = h.shape
    nblk = rows // tm

    def row_spec(width):
        return pl.BlockSpec((1, tm, width), lambda b, j: (b, j, 0))

    outs = [
        (LAT_WIDTH, BF16), (KV_LATENT, BF16), (IDX_WIDTH, BF16), (IDX_DIM, BF16),
        (MLQK_WIDTH, BF16), (ML_WIDTH, BF16), (ML_WIDTH, F32), (LANES, F32),
    ]
    out_shape = [jax.ShapeDtypeStruct((bn, rows, w), dt) for w, dt in outs]
    out_specs = [row_spec(w) for w, _ in outs]
    out_shape.append(jax.ShapeDtypeStruct((bn, CONV_HIST, MLQK_WIDTH), F32))
    out_specs.append(pl.BlockSpec((1, CONV_HIST, MLQK_WIDTH), lambda b, j: (b, 0, 0)))
    return pl.pallas_call(
        _inproj_kernel,
        grid=(bn, nblk),
        in_specs=[
            row_spec(d),
            _resident(tail.shape), _resident(wa.shape), _resident(wm.shape), _resident(wgt.shape),
            _resident(wuk_bd.shape), _resident(kvg.shape), _resident(convw.shape), _resident(gbias.shape),
        ],
        out_specs=out_specs,
        out_shape=out_shape,
        scratch_shapes=[pltpu.VMEM((CONV_HIST, MLQK_WIDTH), F32)],
        compiler_params=pltpu.CompilerParams(
            dimension_semantics=("arbitrary", "arbitrary"), vmem_limit_bytes=VMEM_LIMIT_BYTES),
        name="in_proj",
    )(h, tail, wa, wm, wgt, wuk_bd, kvg, convw, gbias)


def _dsa_kernel(qlat_ref, qidx_ref, wrow_ref, ckv_ref, ckvt_ref, kidx_ref, mckv_ref, mckvt_ref, mkidx_ref,
                wuv_ref, y_ref, s_ref, acc_ref, *, topk):
    _, kc, tq = s_ref.shape
    i = pl.program_id(1)
    nch = ((i + 1) * tq + kc - 1) // kc
    qreal = i * tq + lax.broadcasted_iota(jnp.int32, (1, tq), 1)
    kf = float(topk)

    wrow = wrow_ref[0]
    qidx = qidx_ref[0]
    q_idx_all = jnp.concatenate([qidx[:, h * IDX_DIM:(h + 1) * IDX_DIM] for h in range(IDX_HEADS)], axis=0)
    wi = [wrow[GATE_W0 + h:GATE_W0 + h + 1, :] for h in range(IDX_HEADS)]

    def scores(k_rows):
        lg = _dot_nt(k_rows, q_idx_all)
        sc = jnp.zeros((k_rows.shape[0], tq), F32)
        for h in range(IDX_HEADS):
            sc = sc + jnp.maximum(lg[:, h * tq:(h + 1) * tq], 0.0) * wi[h]
        return sc

    s_meta = scores(mkidx_ref[...])

    def score_chunk(c, lo, hi):
        sc = scores(kidx_ref[0, c])
        valid = c * kc + lax.broadcasted_iota(jnp.int32, (kc, tq), 0) <= qreal
        s_ref[c] = jnp.where(valid, sc, -jnp.inf)
        groups = (kc // REDUCE_ROWS, REDUCE_ROWS, tq)
        lo = jnp.minimum(lo, jnp.min(jnp.where(valid, sc, jnp.inf).reshape(groups), axis=0))
        hi = jnp.maximum(hi, jnp.max(jnp.where(valid, sc, -jnp.inf).reshape(groups), axis=0))
        return lo, hi

    def score_pair(c2, carry):
        lo, hi = score_chunk(2 * c2, *carry)
        return score_chunk(jnp.minimum(2 * c2 + 1, nch - 1), lo, hi)

    lo, hi = lax.fori_loop(0, (nch + 1) // 2, score_pair,
                           (jnp.full((REDUCE_ROWS, tq), jnp.inf, F32), jnp.full((REDUCE_ROWS, tq), -jnp.inf, F32)))
    lo = jnp.minimum(jnp.min(lo, axis=0, keepdims=True), jnp.min(s_meta, axis=0, keepdims=True))
    hi = jnp.maximum(jnp.max(hi, axis=0, keepdims=True), jnp.max(s_meta, axis=0, keepdims=True))

    def key_reduce(reduce, combine, per_chunk, init):
        def body(c, acc):
            x = per_chunk(s_ref[c]).reshape(kc // REDUCE_ROWS, REDUCE_ROWS, tq)
            return combine(acc, reduce(x, axis=0))
        acc = lax.fori_loop(0, nch, body, jnp.full((REDUCE_ROWS, tq), init, F32))
        return combine(reduce(acc, axis=0, keepdims=True), reduce(per_chunk(s_meta), axis=0, keepdims=True))

    def count(pred):
        return key_reduce(jnp.sum, jnp.add, lambda sc: jnp.where(pred(sc), 1.0, 0.0), 0.0)

    def max_where(pred):
        return key_reduce(jnp.max, jnp.maximum, lambda sc: jnp.where(pred(sc), sc, -jnp.inf), -jnp.inf)

    def bisect(_, carry):
        lo, hi = carry
        mid = 0.5 * lo + 0.5 * hi
        up = count(lambda sc: sc > mid) >= kf
        return jnp.where(up, mid, lo), jnp.where(up, hi, mid)

    lo, hi = lax.fori_loop(0, N_BISECT, bisect, (lo, hi))

    n_valid = (qreal + (N_META + 1)).astype(F32)
    small = n_valid <= kf
    cand = max_where(lambda sc: sc <= hi)
    done = jnp.where(small | (count(lambda sc: sc >= cand) >= kf), 1.0, 0.0)

    def not_finished(state):
        return jnp.min(state[1]) < 0.5

    def step_down(state):
        cand, done = state
        nxt = jnp.where(done > 0.5, cand, max_where(lambda sc: sc < cand))
        fin = count(lambda sc: sc >= nxt) >= kf
        return nxt, jnp.where(fin, 1.0, done)

    cand, _ = lax.while_loop(not_finished, step_down, (cand, done))
    thr = jnp.where(small, -jnp.inf, cand)
    n_gt = count(lambda sc: sc > thr)
    n_eq = count(lambda sc: sc == thr)
    need = jnp.where(small, 0.0, kf - n_gt)
    ranked_ties = jnp.max(jnp.where(n_eq > need, 1.0, 0.0)) > 0.5

    qlat = qlat_ref[0]
    n_groups = tq // ATT_GROUP
    onehot = (lax.broadcasted_iota(jnp.int32, (ATT_GROUP, ATT_GROUP), 0)
              == lax.broadcasted_iota(jnp.int32, (ATT_GROUP, ATT_GROUP), 1)).astype(BF16)
    q_aug = [jnp.concatenate(
        [jnp.concatenate([qlat[g * ATT_GROUP:(g + 1) * ATT_GROUP, h * KV_LATENT:(h + 1) * KV_LATENT], onehot], axis=1)
         for h in range(ATT_HEADS)], axis=0) for g in range(n_groups)]
    hq = ATT_HEADS * ATT_GROUP

    def lower_tri(n):
        return (lax.broadcasted_iota(jnp.int32, (n, n), 1) <= lax.broadcasted_iota(jnp.int32, (n, n), 0)).astype(BF16)

    def attention(ranked):
        def mask_bias(sc, eq_seen):
            if not ranked:
                return jnp.where(sc >= thr, 0.0, NEG_BIG).astype(BF16), eq_seen
            n = sc.shape[0]
            eq = sc == thr
            rank = _dot(lower_tri(n), jnp.where(eq, 1.0, 0.0).astype(BF16)) + eq_seen
            keep = (sc > thr) | (eq & (rank <= need))
            return jnp.where(keep, 0.0, NEG_BIG).astype(BF16), rank[n - 1:n, :]

        def logits(g, kv, bias):
            k_aug = jnp.concatenate([kv, bias[:, g * ATT_GROUP:(g + 1) * ATT_GROUP]], axis=1)
            return _dot_nt(k_aug, q_aug[g])

        def attend(c, carry):
            bias, eq_seen = mask_bias(s_ref[c], carry[0])
            out = [eq_seen]
            for g in range(n_groups):
                m, l = carry[1 + 2 * g:3 + 2 * g]
                s = logits(g, ckv_ref[0, c], bias)
                m_new = jnp.maximum(m, jnp.max(s, axis=0, keepdims=True))
                a = jnp.exp2(m - m_new)
                p = jnp.exp2(s - m_new)
                acc_ref[g] = a * acc_ref[g] + _dot(ckvt_ref[0, c], p.astype(BF16))
                out += [m_new, a * l + jnp.sum(p, axis=0, keepdims=True)]
            return tuple(out)

        bias_m, eq_seen = mask_bias(s_meta, jnp.zeros((1, tq), F32))
        bias_0, eq_seen = mask_bias(s_ref[0], eq_seen)
        init = [eq_seen]
        for g in range(n_groups):
            s_m = logits(g, mckv_ref[...], bias_m)
            s_0 = logits(g, ckv_ref[0, 0], bias_0)
            m = jnp.maximum(jnp.max(s_m, axis=0, keepdims=True), jnp.max(s_0, axis=0, keepdims=True))
            p_m = jnp.exp2(s_m - m)
            p_0 = jnp.exp2(s_0 - m)
            acc_ref[g] = _dot(mckvt_ref[...], p_m.astype(BF16)) + _dot(ckvt_ref[0, 0], p_0.astype(BF16))
            init += [m, jnp.sum(p_m, axis=0, keepdims=True) + jnp.sum(p_0, axis=0, keepdims=True)]
        carry = lax.fori_loop(1, nch, attend, tuple(init))
        rows = []
        for g in range(n_groups):
            o_t = (acc_ref[g] / carry[2 + 2 * g]).T
            rows.append(jnp.concatenate([o_t[h * ATT_GROUP:(h + 1) * ATT_GROUP] for h in range(ATT_HEADS)], axis=1))
        return _dot(jnp.concatenate(rows, axis=0).astype(BF16), wuv_ref[...]).astype(BF16)

    y_ref[0] = lax.cond(ranked_ties, lambda: attention(True), lambda: attention(False))


def _dsa_call(qlat, qidx, wrow, ckv_c, ckvt_c, kidx_c, m_ckv, m_ckvt, m_kidx, wuv_bd, topk):
    bn, rows, _ = qlat.shape
    nchunks, kc = ckv_c.shape[1], ckv_c.shape[2]
    nq = rows // Q_TILE

    def q_spec(width):
        return pl.BlockSpec((1, Q_TILE, width), lambda b, i: (b, i, 0))

    def k_spec(a):
        return pl.BlockSpec((1,) + a.shape[1:], lambda b, i: (b, 0, 0, 0))

    return pl.pallas_call(
        functools.partial(_dsa_kernel, topk=topk),
        grid=(bn, nq),
        in_specs=[q_spec(LAT_WIDTH), q_spec(IDX_WIDTH),
                  pl.BlockSpec((1, SUBLANES, Q_TILE), lambda b, i: (b, 0, i)),
                  k_spec(ckv_c), k_spec(ckvt_c), k_spec(kidx_c),
                  _resident(m_ckv.shape), _resident(m_ckvt.shape), _resident(m_kidx.shape), _resident(wuv_bd.shape)],
        out_specs=q_spec(ATT_WIDTH),
        out_shape=jax.ShapeDtypeStruct((bn, rows, ATT_WIDTH), BF16),
        scratch_shapes=[pltpu.VMEM((nchunks, kc, Q_TILE), F32),
                        pltpu.VMEM((Q_TILE // ATT_GROUP, KV_LATENT, ATT_HEADS * ATT_GROUP), F32)],
        compiler_params=pltpu.CompilerParams(
            dimension_semantics=("arbitrary", "arbitrary"), vmem_limit_bytes=VMEM_LIMIT_BYTES),
        name="dsa",
    )(qlat, qidx, wrow, ckv_c, ckvt_c, kidx_c, m_ckv, m_ckvt, m_kidx, wuv_bd)


def _split3(x):
    hi = x.astype(BF16)
    r = x - hi.astype(F32)
    mid = r.astype(BF16)
    lo = (r - mid.astype(F32)).astype(BF16)
    return hi, mid, lo


def _mlstm_chunk(qk, v, g, state, norm_g):
    L = qk.shape[0]
    row = lax.broadcasted_iota(jnp.int32, (L, L), 0)
    col = lax.broadcasted_iota(jnp.int32, (L, L), 1)
    causal = col <= row
    tri = causal.astype(BF16)
    b_all = sum(_dot(tri, part) for part in _split3(g))
    g_t = g.T
    b_t = b_all.T
    ones_col = jnp.where(lax.broadcasted_iota(jnp.int32, (L, ML_V_DIM), 1) == 0, 1.0, 0.0).astype(BF16)
    kq = ML_HEADS * ML_QK_DIM

    outs, new_state, kws, vexts, decays = [], [], [], [], []
    for h in range(ML_HEADS):
        ce, m_prev = state[h]
        ig_c = g[:, GATE_I0 + h:GATE_I0 + h + 1]
        b_c = b_all[:, GATE_F0 + h:GATE_F0 + h + 1]
        ig_r = g_t[GATE_I0 + h:GATE_I0 + h + 1, :]
        b_r = b_t[GATE_F0 + h:GATE_F0 + h + 1, :]
        qh = qk[:, h * ML_QK_DIM:(h + 1) * ML_QK_DIM]
        kh = qk[:, kq + h * ML_QK_DIM:kq + (h + 1) * ML_QK_DIM]
        vext = jnp.concatenate([v[:, h * ML_V_DIM:(h + 1) * ML_V_DIM], ones_col], axis=1)

        d = jnp.where(causal, b_c - b_r + ig_r, -jnp.inf)
        inter = b_c + m_prev
        m_t = jnp.maximum(jnp.max(d, axis=1, keepdims=True), inter)
        w_inter = jnp.exp(inter - m_t)
        s = _dot_nt(qh, kh) * jnp.exp(d - m_t)
        r = _dot(s.astype(BF16), vext) + w_inter * _dot(qh, ce.astype(BF16))
        num = r[:, :ML_V_DIM]
        den = r[:, ML_V_DIM:ML_V_DIM + 1]
        hh = num / jnp.maximum(jnp.abs(den), jnp.exp(-m_t))
        mu = jnp.mean(hh, axis=-1, keepdims=True)
        hc = hh - mu
        var = jnp.mean(hc * hc, axis=-1, keepdims=True)
        outs.append(hc * lax.rsqrt(var + LN_EPS) * norm_g[:, h * ML_V_DIM:(h + 1) * ML_V_DIM])

        b_end = b_c[L - 1:L]
        g_c = b_end - b_c + ig_c
        g_r = b_end - b_r + ig_r
        m_new = jnp.maximum(b_end + m_prev, jnp.max(g_r, axis=1, keepdims=True))
        decays.append(jnp.exp(b_end + m_prev - m_new))
        kws.append(kh.astype(F32) * jnp.exp(g_c - m_new))
        vexts.append(vext)
        new_state.append(m_new)

    kw_t = jnp.concatenate(kws, axis=1).T
    for h in range(ML_HEADS):
        ce, _ = state[h]
        upd = _dot(kw_t[h * ML_QK_DIM:(h + 1) * ML_QK_DIM].astype(BF16), vexts[h])
        new_state[h] = (decays[h] * ce + upd, new_state[h])
    return outs, new_state


def _mlstm_kernel(qk_ref, v_ref, og_ref, gates_ref, mqk_ref, mv_ref, mgates_ref, ng_ref, y_ref):
    L = mqk_ref.shape[0]
    n_chunks = qk_ref.shape[1] // L
    norm_g = ng_ref[...]
    state = [(jnp.zeros((ML_QK_DIM, 2 * ML_V_DIM), F32), jnp.full((1, 1), M_INIT, F32))
             for _ in range(ML_HEADS)]
    _, state = _mlstm_chunk(mqk_ref[...], mv_ref[...], mgates_ref[...], state, norm_g)

    def body(c, flat):
        state = [(flat[2 * h], flat[2 * h + 1]) for h in range(ML_HEADS)]
        rows = pl.ds(pl.multiple_of(c * L, L), L)
        outs, state = _mlstm_chunk(qk_ref[0, rows, :], v_ref[0, rows, :], gates_ref[0, rows, :], state, norm_g)
        og = og_ref[0, rows, :]
        y = jnp.concatenate(outs, axis=1) * og
        y_ref[0, rows, :] = y.astype(BF16)
        return tuple(x for pair in state for x in pair)

    lax.fori_loop(0, n_chunks, body, tuple(x for pair in state for x in pair))


def _mlstm_call(qk, v, og, gates, mqk, mv, mgates, norm_g):
    bn, rows, _ = qk.shape

    def b_spec(width):
        return pl.BlockSpec((1, rows, width), lambda b: (b, 0, 0))

    return pl.pallas_call(
        _mlstm_kernel,
        grid=(bn,),
        in_specs=[b_spec(MLQK_WIDTH), b_spec(ML_WIDTH), b_spec(ML_WIDTH), b_spec(LANES),
                  _resident(mqk.shape), _resident(mv.shape), _resident(mgates.shape), _resident(norm_g.shape)],
        out_specs=b_spec(ML_WIDTH),
        out_shape=jax.ShapeDtypeStruct((bn, rows, ML_WIDTH), BF16),
        compiler_params=pltpu.CompilerParams(
            dimension_semantics=("arbitrary",), vmem_limit_bytes=VMEM_LIMIT_BYTES),
        name="mlstm",
    )(qk, v, og, gates, mqk, mv, mgates, norm_g)


def _out_ffn_ln_kernel(ya_ref, ym_ref, h_ref, wo_ref, g2_ref, b2_ref, wg_ref, wu_ref, wd_ref, g3_ref, b3_ref,
                       o_ref, *, alpha):
    mix = _dot(ya_ref[...], wo_ref[:ATT_WIDTH, :]) + _dot(ym_ref[...], wo_ref[ATT_WIDTH:, :])
    h2 = _layer_norm(alpha * h_ref[...] + mix, g2_ref[...], b2_ref[...])
    o_ref[...] = _ffn_ln(h2, wg_ref, wu_ref, wd_ref, g3_ref[...], b3_ref[...], alpha)


def _out_ffn_ln_call(ya, ym, h, wo, g2, b2, wg, wu, wd, g3, b3, alpha, tm):
    rows, d = h.shape

    def row_spec(width):
        return pl.BlockSpec((tm, width), lambda i: (i, 0))

    consts = (wo, g2, b2, wg, wu, wd, g3, b3)
    return pl.pallas_call(
        functools.partial(_out_ffn_ln_kernel, alpha=alpha),
        grid=(rows // tm,),
        in_specs=[row_spec(ya.shape[1]), row_spec(ym.shape[1]), row_spec(d)] + [_resident(c.shape) for c in consts],
        out_specs=row_spec(d),
        out_shape=jax.ShapeDtypeStruct((rows, d), F32),
        compiler_params=pltpu.CompilerParams(
            dimension_semantics=("arbitrary",), vmem_limit_bytes=VMEM_LIMIT_BYTES),
        name="out_ffn_ln",
    )(ya, ym, h, *consts)


def _block_diag(w):
    nh, a, b = w.shape
    eye = jnp.eye(nh, dtype=w.dtype)
    return (eye[:, None, :, None] * w[:, :, None, :]).reshape(nh * a, nh * b)


def _pad_rows(a, rows, value=0.0):
    return jnp.pad(a, ((0, rows - a.shape[0]), (0, 0)), constant_values=value)


def kernel(x, meta_tokens, ln1_g, ln1_b, ffn1_w_gate, ffn1_w_up, ffn1_w_down, w_in, w_uk, w_uv, kv_norm_g,
           conv_w, b_igate, b_fgate, ml_norm_g, w_out, ln2_g, ln2_b, ffn2_w_gate, ffn2_w_up, ffn2_w_down,
           ln3_g, ln3_b):
    depth = ln1_g.shape[0]
    assert depth == 1, "the meta-token shortcut below is only valid for a single layer"
    bsz, seq, d = x.shape
    assert seq % ROW_TILE == 0 and seq % ML_CHUNK == 0 and seq % Q_TILE == 0
    alpha = (2 * depth) ** 0.25
    topk = min(TOPK_MAX, seq // 4)

    row2 = lambda p: p[0].reshape(1, -1).astype(F32)
    bf = lambda w: w[0].astype(BF16)

    w = w_in[0]
    o_qa, o_ckv, o_qi, o_ki, o_wi = 0, 512, 640, 896, 960
    o_qk, o_v, o_o, o_ig, o_fg, o_end = 964, 1476, 1988, 2500, 2504, 2508
    wa = jnp.concatenate([w[:, o_qa:o_wi], jnp.zeros((d, 1024 - o_wi), F32)], axis=1).astype(BF16)
    wm = w[:, o_qk:o_ig].astype(BF16)
    wgt = jnp.concatenate([w[:, o_wi:o_qk], w[:, o_ig:o_end], jnp.zeros((d, LANES - GATE_END), F32)],
                          axis=1).astype(BF16)
    gbias = jnp.concatenate([jnp.zeros((IDX_HEADS,), F32), b_igate[0], b_fgate[0],
                             jnp.zeros((LANES - GATE_END,), F32)]).reshape(1, LANES)
    wuk_bd = _block_diag(w_uk[0]).astype(BF16)
    wuv_bd = _block_diag(w_uv[0]).astype(BF16)
    kvg = row2(kv_norm_g)
    convw = conv_w[0].astype(F32)
    ffn1 = (bf(ffn1_w_gate), bf(ffn1_w_up), bf(ffn1_w_down), row2(ln1_g), row2(ln1_b))

    meta = meta_tokens.astype(F32)
    h1_meta = _ffn_ln_call(meta, *ffn1, alpha, N_META)
    zero_tail = jnp.zeros((CONV_HIST, MLQK_WIDTH), F32)
    (_, m_ckv, _, m_kidx, m_qk, m_v, _, m_gates, m_tail) = _inproj_call(
        h1_meta[None], zero_tail, wa, wm, wgt, wuk_bd, kvg, convw, gbias, N_META)

    h1 = _ffn_ln_call(x.reshape(bsz * seq, d), *ffn1, alpha, ROW_TILE)
    (qlat, ckv, qidx, kidx, qk, v, og, gates, _) = _inproj_call(
        h1.reshape(bsz, seq, d), m_tail[0], wa, wm, wgt, wuk_bd, kvg, convw, gbias, ROW_TILE)

    nchunks = seq // KEY_CHUNK
    ckv_c = ckv.reshape(bsz, nchunks, KEY_CHUNK, KV_LATENT)
    kidx_c = kidx.reshape(bsz, nchunks, KEY_CHUNK, IDX_DIM)
    wrow = jnp.swapaxes(gates[:, :, :SUBLANES], 1, 2)
    y_att = _dsa_call(qlat, qidx, wrow, ckv_c, jnp.swapaxes(ckv_c, 2, 3), kidx_c,
                      m_ckv[0], m_ckv[0].T, m_kidx[0], wuv_bd, topk)

    lane = jnp.arange(LANES)
    pad_gate = jnp.where((lane >= GATE_I0) & (lane < GATE_F0), NEG_BIG, 0.0).astype(F32)
    mg = jnp.concatenate([m_gates[0], jnp.broadcast_to(pad_gate, (ML_CHUNK - N_META, LANES))], axis=0)
    y_ml = _mlstm_call(qk, v, og, gates, _pad_rows(m_qk[0], ML_CHUNK), _pad_rows(m_v[0], ML_CHUNK), mg,
                       row2(ml_norm_g))

    out = _out_ffn_ln_call(
        y_att.reshape(bsz * seq, ATT_WIDTH), y_ml.reshape(bsz * seq, ML_WIDTH), h1, bf(w_out),
        row2(ln2_g), row2(ln2_b), bf(ffn2_w_gate), bf(ffn2_w_up), bf(ffn2_w_down), row2(ln3_g), row2(ln3_b),
        alpha, ROW_TILE)
    return out.reshape(bsz, seq, d)
```

```python
import functools

import jax
import jax.numpy as jnp
from jax import lax
from jax.experimental import pallas as pl
from jax.experimental.pallas import tpu as pltpu

F32 = jnp.float32
BF16 = jnp.bfloat16

N_META = 16
ATT_HEADS = 8
ATT_HEAD_DIM = 64
KV_LATENT = 128
IDX_HEADS = 4
IDX_DIM = 64
TOPK_MAX = 256
ML_HEADS = 4
ML_V_DIM = 128
ML_QK_DIM = 64
CONV_WIDTH = 4
GATE_SOFTCAP = 15.0
M_INIT = -1e30
LN_EPS = 1e-5
NEG_BIG = -1e30
LOG2_E = 1.4426950408889634

LANES = 128
SUBLANES = 8
VMEM_LIMIT_BYTES = 56 * 1024 * 1024

FF_CHUNK = 256
ROW_TILE = 512
Q_TILE = 256
ATT_GROUP = 128
KEY_CHUNK = 256
N_BISECT = 16
REDUCE_ROWS = 32
ML_CHUNK = 256


def _dot(a, b):
    return jnp.dot(a, b, preferred_element_type=F32)


def _dot_nt(a, b):
    return lax.dot_general(a, b, (((1,), (1,)), ((), ())), preferred_element_type=F32)


def _layer_norm(z, g, b):
    mu = jnp.mean(z, axis=-1, keepdims=True)
    zc = z - mu
    var = jnp.mean(zc * zc, axis=-1, keepdims=True)
    return zc * lax.rsqrt(var + LN_EPS) * g + b


def _sigmoid(x):
    return 1.0 / (1.0 + jnp.exp(-x))


def _swiglu(xb, wg_ref, wu_ref, wd_ref):
    d_ff = wg_ref.shape[1]
    acc = jnp.zeros((xb.shape[0], wd_ref.shape[1]), F32)
    for c in range(d_ff // FF_CHUNK):
        sl = slice(c * FF_CHUNK, (c + 1) * FF_CHUNK)
        g = _dot(xb, wg_ref[:, sl])
        u = _dot(xb, wu_ref[:, sl])
        a = (g * _sigmoid(g) * u).astype(BF16)
        acc = acc + _dot(a, wd_ref[sl, :])
    return acc


def _ffn_ln(x, wg_ref, wu_ref, wd_ref, g, b, alpha):
    y = _swiglu(x.astype(BF16), wg_ref, wu_ref, wd_ref)
    return _layer_norm(alpha * x + 0.5 * y, g, b)


def _ffn_ln_kernel(x_ref, wg_ref, wu_ref, wd_ref, g_ref, b_ref, o_ref, *, alpha):
    o_ref[...] = _ffn_ln(x_ref[...], wg_ref, wu_ref, wd_ref, g_ref[...], b_ref[...], alpha)


def _resident(shape):
    return pl.BlockSpec(shape, lambda *_: (0,) * len(shape), pipeline_mode=pl.Buffered(1))


def _ffn_ln_call(x, wg, wu, wd, g, b, alpha, tm):
    rows, d = x.shape
    return pl.pallas_call(
        functools.partial(_ffn_ln_kernel, alpha=alpha),
        grid=(rows // tm,),
        in_specs=[
            pl.BlockSpec((tm, d), lambda i: (i, 0)),
            _resident(wg.shape), _resident(wu.shape), _resident(wd.shape),
            _resident(g.shape), _resident(b.shape),
        ],
        out_specs=pl.BlockSpec((tm, d), lambda i: (i, 0)),
        out_shape=jax.ShapeDtypeStruct((rows, d), F32),
        compiler_params=pltpu.CompilerParams(
            dimension_semantics=("arbitrary",), vmem_limit_bytes=VMEM_LIMIT_BYTES),
        name="ffn_ln",
    )(x, wg, wu, wd, g, b)


ATT_WIDTH = ATT_HEADS * ATT_HEAD_DIM
IDX_WIDTH = IDX_HEADS * IDX_DIM
MLQK_WIDTH = 2 * ML_HEADS * ML_QK_DIM
ML_WIDTH = ML_HEADS * ML_V_DIM
LAT_WIDTH = ATT_HEADS * KV_LATENT
CONV_HIST = SUBLANES
GATE_W0, GATE_I0, GATE_F0, GATE_END = 0, IDX_HEADS, IDX_HEADS + ML_HEADS, IDX_HEADS + 2 * ML_HEADS


def _inproj_kernel(h_ref, tail_ref, wa_ref, wm_ref, wgt_ref, wuk_ref, kvg_ref, convw_ref, gbias_ref,
                   qlat_ref, ckv_ref, qidx_ref, kidx_ref, qk_ref, v_ref, og_ref, gates_ref, tailout_ref,
                   carry_ref):
    tm = h_ref.shape[1]

    @pl.when(pl.program_id(1) == 0)
    def _():
        carry_ref[...] = tail_ref[...]

    xb = h_ref[0].astype(BF16)

    pa = _dot(xb, wa_ref[...])
    q_a = pa[:, :ATT_WIDTH].astype(BF16)
    c0 = ATT_WIDTH
    ckv = pa[:, c0:c0 + KV_LATENT]
    c1 = c0 + KV_LATENT
    ckv = ckv * lax.rsqrt(jnp.mean(ckv * ckv, axis=-1, keepdims=True) + LN_EPS) * kvg_ref[...]
    ckv_ref[0] = ckv.astype(BF16)
    qidx_ref[0] = pa[:, c1:c1 + IDX_WIDTH].astype(BF16)
    c2 = c1 + IDX_WIDTH
    kidx_ref[0] = pa[:, c2:c2 + IDX_DIM].astype(BF16)
    qlat_ref[0] = (_dot(q_a, wuk_ref[...]) * (ATT_HEAD_DIM ** -0.5 * LOG2_E)).astype(BF16)

    pm = _dot(xb, wm_ref[...])
    qk_raw = pm[:, :MLQK_WIDTH]
    v_ref[0] = pm[:, MLQK_WIDTH:MLQK_WIDTH + ML_WIDTH].astype(BF16)
    og_ref[0] = _sigmoid(pm[:, MLQK_WIDTH + ML_WIDTH:])

    ext = jnp.concatenate([carry_ref[...], qk_raw], axis=0)
    cw = convw_ref[...]
    conv = jnp.zeros_like(qk_raw)
    for j in range(CONV_WIDTH):
        s0 = CONV_HIST - (CONV_WIDTH - 1) + j
        conv = conv + ext[s0:s0 + tm] * cw[j:j + 1]
    act = conv * _sigmoid(conv)
    half = MLQK_WIDTH // 2
    qk_ref[0, :, :half] = act[:, :half].astype(BF16)
    qk_ref[0, :, half:] = (act[:, half:] * (ML_QK_DIM ** -0.5)).astype(BF16)
    carry_ref[...] = qk_raw[tm - CONV_HIST:]
    tailout_ref[0] = qk_raw[tm - CONV_HIST:]

    gr = _dot(xb, wgt_ref[...])
    lane = lax.broadcasted_iota(jnp.int32, gr.shape, 1)
    sc = GATE_SOFTCAP * jnp.tanh((gr + gbias_ref[...]) / GATE_SOFTCAP)
    lf = -(jnp.maximum(-sc, 0.0) + jnp.log1p(jnp.exp(-jnp.abs(sc))))
    w_scaled = gr * (IDX_HEADS ** -0.5 * IDX_DIM ** -0.5)
    gates_ref[0] = jnp.where(lane < GATE_I0, w_scaled,
                             jnp.where(lane < GATE_F0, sc, jnp.where(lane < GATE_END, lf, 0.0)))


def _inproj_call(h, tail, wa, wm, wgt, wuk_bd, kvg, convw, gbias, tm):
    bn, rows, d = h.shape
    nblk = rows // tm

    def row_spec(width):
        return pl.BlockSpec((1, tm, width), lambda b, j: (b, j, 0))

    outs = [
        (LAT_WIDTH, BF16), (KV_LATENT, BF16), (IDX_WIDTH, BF16), (IDX_DIM, BF16),
        (MLQK_WIDTH, BF16), (ML_WIDTH, BF16), (ML_WIDTH, F32), (LANES, F32),
    ]
    out_shape = [jax.ShapeDtypeStruct((bn, rows, w), dt) for w, dt in outs]
    out_specs = [row_spec(w) for w, _ in outs]
    out_shape.append(jax.ShapeDtypeStruct((bn, CONV_HIST, MLQK_WIDTH), F32))
    out_specs.append(pl.BlockSpec((1, CONV_HIST, MLQK_WIDTH), lambda b, j: (b, 0, 0)))
    return pl.pallas_call(
        _inproj_kernel,
        grid=(bn, nblk),
        in_specs=[
            row_spec(d),
            _resident(tail.shape), _resident(wa.shape), _resident(wm.shape), _resident(wgt.shape),
            _resident(wuk_bd.shape), _resident(kvg.shape), _resident(convw.shape), _resident(gbias.shape),
        ],
        out_specs=out_specs,
        out_shape=out_shape,
        scratch_shapes=[pltpu.VMEM((CONV_HIST, MLQK_WIDTH), F32)],
        compiler_params=pltpu.CompilerParams(
            dimension_semantics=("arbitrary", "arbitrary"), vmem_limit_bytes=VMEM_LIMIT_BYTES),
        name="in_proj",
    )(h, tail, wa, wm, wgt, wuk_bd, kvg, convw, gbias)


def _dsa_kernel(qlat_ref, qidx_ref, wrow_ref, ckv_ref, ckvt_ref, kidx_ref, mckv_ref, mckvt_ref, mkidx_ref,
                wuv_ref, y_ref, s_ref, acc_ref, *, topk):
    _, kc, tq = s_ref.shape
    i = pl.program_id(1)
    nch = ((i + 1) * tq + kc - 1) // kc
    qreal = i * tq + lax.broadcasted_iota(jnp.int32, (1, tq), 1)
    kf = float(topk)

    wrow = wrow_ref[0]
    qidx = qidx_ref[0]
    q_idx_all = jnp.concatenate([qidx[:, h * IDX_DIM:(h + 1) * IDX_DIM] for h in range(IDX_HEADS)], axis=0)
    wi = [wrow[GATE_W0 + h:GATE_W0 + h + 1, :] for h in range(IDX_HEADS)]

    def scores(k_rows):
        lg = _dot_nt(k_rows, q_idx_all)
        sc = jnp.zeros((k_rows.shape[0], tq), F32)
        for h in range(IDX_HEADS):
            sc = sc + jnp.maximum(lg[:, h * tq:(h + 1) * tq], 0.0) * wi[h]
        return sc

    s_meta = scores(mkidx_ref[...])

    def score_chunk(c, lo, hi):
        sc = scores(kidx_ref[0, c])
        valid = c * kc + lax.broadcasted_iota(jnp.int32, (kc, tq), 0) <= qreal
        s_ref[c] = jnp.where(valid, sc, -jnp.inf)
        groups = (kc // REDUCE_ROWS, REDUCE_ROWS, tq)
        lo = jnp.minimum(lo, jnp.min(jnp.where(valid, sc, jnp.inf).reshape(groups), axis=0))
        hi = jnp.maximum(hi, jnp.max(jnp.where(valid, sc, -jnp.inf).reshape(groups), axis=0))
        return lo, hi

    def score_pair(c2, carry):
        lo, hi = score_chunk(2 * c2, *carry)
        return score_chunk(jnp.minimum(2 * c2 + 1, nch - 1), lo, hi)

    lo, hi = lax.fori_loop(0, (nch + 1) // 2, score_pair,
                           (jnp.full((REDUCE_ROWS, tq), jnp.inf, F32), jnp.full((REDUCE_ROWS, tq), -jnp.inf, F32)))
    lo = jnp.minimum(jnp.min(lo, axis=0, keepdims=True), jnp.min(s_meta, axis=0, keepdims=True))
    hi = jnp.maximum(jnp.max(hi, axis=0, keepdims=True), jnp.max(s_meta, axis=0, keepdims=True))

    def key_reduce(reduce, combine, per_chunk, init):
        def body(c, acc):
            x = per_chunk(s_ref[c]).reshape(kc // REDUCE_ROWS, REDUCE_ROWS, tq)
            return combine(acc, reduce(x, axis=0))
        acc = lax.fori_loop(0, nch, body, jnp.full((REDUCE_ROWS, tq), init, F32))
        return combine(reduce(acc, axis=0, keepdims=True), reduce(per_chunk(s_meta), axis=0, keepdims=True))

    def count(pred):
        return key_reduce(jnp.sum, jnp.add, lambda sc: jnp.where(pred(sc), 1.0, 0.0), 0.0)

    def max_where(pred):
        return key_reduce(jnp.max, jnp.maximum, lambda sc: jnp.where(pred(sc), sc, -jnp.inf), -jnp.inf)

    def bisect(_, carry):
        lo, hi = carry
        mid = 0.5 * lo + 0.5 * hi
        up = count(lambda sc: sc > mid) >= kf
        return jnp.where(up, mid, lo), jnp.where(up, hi, mid)

    lo, hi = lax.fori_loop(0, N_BISECT, bisect, (lo, hi))

    n_valid = (qreal + (N_META + 1)).astype(F32)
    small = n_valid <= kf
    cand = max_where(lambda sc: sc <= hi)
    done = jnp.where(small | (count(lambda sc: sc >= cand) >= kf), 1.0, 0.0)

    def not_finished(state):
        return jnp.min(state[1]) < 0.5

    def step_down(state):
        cand, done = state
        nxt = jnp.where(done > 0.5, cand, max_where(lambda sc: sc < cand))
        fin = count(lambda sc: sc >= nxt) >= kf
        return nxt, jnp.where(fin, 1.0, done)

    cand, _ = lax.while_loop(not_finished, step_down, (cand, done))
    thr = jnp.where(small, -jnp.inf, cand)
    n_gt = count(lambda sc: sc > thr)
    n_eq = count(lambda sc: sc == thr)
    need = jnp.where(small, 0.0, kf - n_gt)
    ranked_ties = jnp.max(jnp.where(n_eq > need, 1.0, 0.0)) > 0.5

    qlat = qlat_ref[0]
    n_groups = tq // ATT_GROUP
    onehot = (lax.broadcasted_iota(jnp.int32, (ATT_GROUP, ATT_GROUP), 0)
              == lax.broadcasted_iota(jnp.int32, (ATT_GROUP, ATT_GROUP), 1)).astype(BF16)
    q_aug = [jnp.concatenate(
        [jnp.concatenate([qlat[g * ATT_GROUP:(g + 1) * ATT_GROUP, h * KV_LATENT:(h + 1) * KV_LATENT], onehot], axis=1)
         for h in range(ATT_HEADS)], axis=0) for g in range(n_groups)]
    hq = ATT_HEADS * ATT_GROUP

    def lower_tri(n):
        return (lax.broadcasted_iota(jnp.int32, (n, n), 1) <= lax.broadcasted_iota(jnp.int32, (n, n), 0)).astype(BF16)

    def attention(ranked):
        def mask_bias(sc, eq_seen):
            if not ranked:
                return jnp.where(sc >= thr, 0.0, NEG_BIG).astype(BF16), eq_seen
            n = sc.shape[0]
            eq = sc == thr
            rank = _dot(lower_tri(n), jnp.where(eq, 1.0, 0.0).astype(BF16)) + eq_seen
            keep = (sc > thr) | (eq & (rank <= need))
            return jnp.where(keep, 0.0, NEG_BIG).astype(BF16), rank[n - 1:n, :]

        def logits(g, kv, bias):
            k_aug = jnp.concatenate([kv, bias[:, g * ATT_GROUP:(g + 1) * ATT_GROUP]], axis=1)
            return _dot_nt(k_aug, q_aug[g])

        def attend(c, carry):
            bias, eq_seen = mask_bias(s_ref[c], carry[0])
            out = [eq_seen]
            for g in range(n_groups):
                m, l = carry[1 + 2 * g:3 + 2 * g]
                s = logits(g, ckv_ref[0, c], bias)
                m_new = jnp.maximum(m, jnp.max(s, axis=0, keepdims=True))
                a = jnp.exp2(m - m_new)
                p = jnp.exp2(s - m_new)
                acc_ref[g] = a * acc_ref[g] + _dot(ckvt_ref[0, c], p.astype(BF16))
                out += [m_new, a * l + jnp.sum(p, axis=0, keepdims=True)]
            return tuple(out)

        bias_m, eq_seen = mask_bias(s_meta, jnp.zeros((1, tq), F32))
        bias_0, eq_seen = mask_bias(s_ref[0], eq_seen)
        init = [eq_seen]
        for g in range(n_groups):
            s_m = logits(g, mckv_ref[...], bias_m)
            s_0 = logits(g, ckv_ref[0, 0], bias_0)
            m = jnp.maximum(jnp.max(s_m, axis=0, keepdims=True), jnp.max(s_0, axis=0, keepdims=True))
            p_m = jnp.exp2(s_m - m)
            p_0 = jnp.exp2(s_0 - m)
            acc_ref[g] = _dot(mckvt_ref[...], p_m.astype(BF16)) + _dot(ckvt_ref[0, 0], p_0.astype(BF16))
            init += [m, jnp.sum(p_m, axis=0, keepdims=True) + jnp.sum(p_0, axis=0, keepdims=True)]
        carry = lax.fori_loop(1, nch, attend, tuple(init))
        rows = []
        for g in range(n_groups):
            o_t = (acc_ref[g] / carry[2 + 2 * g]).T
            rows.append(jnp.concatenate([o_t[h * ATT_GROUP:(h + 1) * ATT_GROUP] for h in range(ATT_HEADS)], axis=1))
        return _dot(jnp.concatenate(rows, axis=0).astype(BF16), wuv_ref[...]).astype(BF16)

    y_ref[0] = lax.cond(ranked_ties, lambda: attention(True), lambda: attention(False))


def _dsa_call(qlat, qidx, wrow, ckv_c, ckvt_c, kidx_c, m_ckv, m_ckvt, m_kidx, wuv_bd, topk):
    bn, rows, _ = qlat.shape
    nchunks, kc = ckv_c.shape[1], ckv_c.shape[2]
    nq = rows // Q_TILE

    def q_spec(width):
        return pl.BlockSpec((1, Q_TILE, width), lambda b, i: (b, i, 0))

    def k_spec(a):
        return pl.BlockSpec((1,) + a.shape[1:], lambda b, i: (b, 0, 0, 0))

    return pl.pallas_call(
        functools.partial(_dsa_kernel, topk=topk),
        grid=(bn, nq),
        in_specs=[q_spec(LAT_WIDTH), q_spec(IDX_WIDTH),
                  pl.BlockSpec((1, SUBLANES, Q_TILE), lambda b, i: (b, 0, i)),
                  k_spec(ckv_c), k_spec(ckvt_c), k_spec(kidx_c),
                  _resident(m_ckv.shape), _resident(m_ckvt.shape), _resident(m_kidx.shape), _resident(wuv_bd.shape)],
        out_specs=q_spec(ATT_WIDTH),
        out_shape=jax.ShapeDtypeStruct((bn, rows, ATT_WIDTH), BF16),
        scratch_shapes=[pltpu.VMEM((nchunks, kc, Q_TILE), F32),
                        pltpu.VMEM((Q_TILE // ATT_GROUP, KV_LATENT, ATT_HEADS * ATT_GROUP), F32)],
        compiler_params=pltpu.CompilerParams(
            dimension_semantics=("arbitrary", "arbitrary"), vmem_limit_bytes=VMEM_LIMIT_BYTES),
        name="dsa",
    )(qlat, qidx, wrow, ckv_c, ckvt_c, kidx_c, m_ckv, m_ckvt, m_kidx, wuv_bd)


def _split3(x):
    hi = x.astype(BF16)
    r = x - hi.astype(F32)
    mid = r.astype(BF16)
    lo = (r - mid.astype(F32)).astype(BF16)
    return hi, mid, lo


ML_EXT = ML_V_DIM + 16


def _mlstm_chunk(qk, vt, g, gt, state):
    L = qk.shape[0]
    s_idx = lax.broadcasted_iota(jnp.int32, (L, L), 0)
    t_idx = lax.broadcasted_iota(jnp.int32, (L, L), 1)
    causal = s_idx <= t_idx
    b_cols = sum(_dot((t_idx <= s_idx).astype(BF16), part) for part in _split3(g))
    b_rows = sum(_dot(part, causal.astype(BF16)) for part in _split3(gt))
    ones_blk = jnp.where(lax.broadcasted_iota(jnp.int32, (ML_EXT - ML_V_DIM, L), 0) == 0, 1.0, 0.0).astype(BF16)
    kq = ML_HEADS * ML_QK_DIM

    outs, new_state = [], []
    for h in range(ML_HEADS):
        ce, m_prev = state[h]
        c_col = g[:, GATE_I0 + h:GATE_I0 + h + 1] - b_cols[:, GATE_F0 + h:GATE_F0 + h + 1]
        b_row = b_rows[GATE_F0 + h:GATE_F0 + h + 1, :]
        ig_row = gt[GATE_I0 + h:GATE_I0 + h + 1, :]
        qh = qk[:, h * ML_QK_DIM:(h + 1) * ML_QK_DIM]
        kh = qk[:, kq + h * ML_QK_DIM:kq + (h + 1) * ML_QK_DIM]
        vt_ext = jnp.concatenate([vt[h * ML_V_DIM:(h + 1) * ML_V_DIM, :], ones_blk], axis=0)

        d_t = jnp.where(causal, c_col + b_row, -jnp.inf)
        inter = b_row + m_prev
        m_t = jnp.maximum(jnp.max(d_t, axis=0, keepdims=True), inter)
        w_inter = jnp.exp(inter - m_t)
        s_t = _dot_nt(kh, qh) * jnp.exp(d_t - m_t)
        r = _dot(vt_ext, s_t.astype(BF16)) + _dot_nt(ce.astype(BF16), qh) * w_inter
        num = r[:ML_V_DIM]
        den = r[ML_V_DIM:ML_V_DIM + 1]
        hh = num / jnp.maximum(jnp.abs(den), jnp.exp(-m_t))
        mu = jnp.mean(hh, axis=0, keepdims=True)
        hc = hh - mu
        var = jnp.mean(hc * hc, axis=0, keepdims=True)
        outs.append((hc * lax.rsqrt(var + LN_EPS)).T)

        b_end = b_row[:, L - 1:L]
        g_row = b_end - b_row + ig_row
        m_new = jnp.maximum(b_end + m_prev, jnp.max(g_row, axis=1, keepdims=True))
        decay = jnp.exp(b_end + m_prev - m_new)
        weighted = (vt_ext.astype(F32) * jnp.exp(g_row - m_new)).astype(BF16)
        new_state.append((decay * ce + _dot(weighted, kh), m_new))
    return outs, new_state


def _mlstm_kernel(qk_ref, vt_ref, og_ref, gates_ref, gt_ref, mqk_ref, mvt_ref, mgates_ref, mgt_ref, ng_ref, y_ref,
                  ce0_ref, m0_ref):
    L = mqk_ref.shape[0]
    n_chunks = qk_ref.shape[1] // L
    norm_g = ng_ref[...]

    @pl.when(pl.program_id(0) == 0)
    def _():
        state = [(jnp.zeros((ML_EXT, ML_QK_DIM), F32), jnp.full((1, 1), M_INIT, F32)) for _ in range(ML_HEADS)]
        _, state = _mlstm_chunk(mqk_ref[...], mvt_ref[...], mgates_ref[...], mgt_ref[...], state)
        for h in range(ML_HEADS):
            ce0_ref[h] = state[h][0]
            m0_ref[h] = jnp.broadcast_to(state[h][1], m0_ref.shape[1:])

    state = [(ce0_ref[h], m0_ref[h][0:1, 0:1]) for h in range(ML_HEADS)]

    def body(c, flat):
        state = [(flat[2 * h], flat[2 * h + 1]) for h in range(ML_HEADS)]
        rows = pl.ds(pl.multiple_of(c * L, L), L)
        outs, state = _mlstm_chunk(qk_ref[0, rows, :], vt_ref[0, c], gates_ref[0, rows, :], gt_ref[0, c], state)
        y = jnp.concatenate(outs, axis=1) * norm_g * og_ref[0, rows, :]
        y_ref[0, rows, :] = y.astype(BF16)
        return tuple(x for pair in state for x in pair)

    lax.fori_loop(0, n_chunks, body, tuple(x for pair in state for x in pair))


def _mlstm_call(qk, vt_c, og, gates, gt_c, mqk, mvt, mgates, mgt, norm_g):
    bn, rows, _ = qk.shape

    def b_spec(a):
        return pl.BlockSpec((1,) + a.shape[1:], lambda b: (b,) + (0,) * (a.ndim - 1))

    consts = (mqk, mvt, mgates, mgt, norm_g)
    return pl.pallas_call(
        _mlstm_kernel,
        grid=(bn,),
        in_specs=[b_spec(qk), b_spec(vt_c), b_spec(og), b_spec(gates), b_spec(gt_c)]
        + [_resident(c.shape) for c in consts],
        out_specs=pl.BlockSpec((1, rows, ML_WIDTH), lambda b: (b, 0, 0)),
        out_shape=jax.ShapeDtypeStruct((bn, rows, ML_WIDTH), BF16),
        scratch_shapes=[pltpu.VMEM((ML_HEADS, ML_EXT, ML_QK_DIM), F32), pltpu.VMEM((ML_HEADS, SUBLANES, LANES), F32)],
        compiler_params=pltpu.CompilerParams(
            dimension_semantics=("arbitrary",), vmem_limit_bytes=VMEM_LIMIT_BYTES),
        name="mlstm",
    )(qk, vt_c, og, gates, gt_c, *consts)


def _out_ffn_ln_kernel(ya_ref, ym_ref, h_ref, wo_ref, g2_ref, b2_ref, wg_ref, wu_ref, wd_ref, g3_ref, b3_ref,
                       o_ref, *, alpha):
    mix = _dot(ya_ref[...], wo_ref[:ATT_WIDTH, :]) + _dot(ym_ref[...], wo_ref[ATT_WIDTH:, :])
    h2 = _layer_norm(alpha * h_ref[...] + mix, g2_ref[...], b2_ref[...])
    o_ref[...] = _ffn_ln(h2, wg_ref, wu_ref, wd_ref, g3_ref[...], b3_ref[...], alpha)


def _out_ffn_ln_call(ya, ym, h, wo, g2, b2, wg, wu, wd, g3, b3, alpha, tm):
    rows, d = h.shape

    def row_spec(width):
        return pl.BlockSpec((tm, width), lambda i: (i, 0))

    consts = (wo, g2, b2, wg, wu, wd, g3, b3)
    return pl.pallas_call(
        functools.partial(_out_ffn_ln_kernel, alpha=alpha),
        grid=(rows // tm,),
        in_specs=[row_spec(ya.shape[1]), row_spec(ym.shape[1]), row_spec(d)] + [_resident(c.shape) for c in consts],
        out_specs=row_spec(d),
        out_shape=jax.ShapeDtypeStruct((rows, d), F32),
        compiler_params=pltpu.CompilerParams(
            dimension_semantics=("arbitrary",), vmem_limit_bytes=VMEM_LIMIT_BYTES),
        name="out_ffn_ln",
    )(ya, ym, h, *consts)


def _block_diag(w):
    nh, a, b = w.shape
    eye = jnp.eye(nh, dtype=w.dtype)
    return (eye[:, None, :, None] * w[:, :, None, :]).reshape(nh * a, nh * b)


def _pad_rows(a, rows, value=0.0):
    return jnp.pad(a, ((0, rows - a.shape[0]), (0, 0)), constant_values=value)


def kernel(x, meta_tokens, ln1_g, ln1_b, ffn1_w_gate, ffn1_w_up, ffn1_w_down, w_in, w_uk, w_uv, kv_norm_g,
           conv_w, b_igate, b_fgate, ml_norm_g, w_out, ln2_g, ln2_b, ffn2_w_gate, ffn2_w_up, ffn2_w_down,
           ln3_g, ln3_b):
    depth = ln1_g.shape[0]
    assert depth == 1, "the meta-token shortcut below is only valid for a single layer"
    bsz, seq, d = x.shape
    assert seq % ROW_TILE == 0 and seq % ML_CHUNK == 0 and seq % Q_TILE == 0
    alpha = (2 * depth) ** 0.25
    topk = min(TOPK_MAX, seq // 4)

    row2 = lambda p: p[0].reshape(1, -1).astype(F32)
    bf = lambda w: w[0].astype(BF16)

    w = w_in[0]
    o_qa, o_ckv, o_qi, o_ki, o_wi = 0, 512, 640, 896, 960
    o_qk, o_v, o_o, o_ig, o_fg, o_end = 964, 1476, 1988, 2500, 2504, 2508
    wa = jnp.concatenate([w[:, o_qa:o_wi], jnp.zeros((d, 1024 - o_wi), F32)], axis=1).astype(BF16)
    wm = w[:, o_qk:o_ig].astype(BF16)
    wgt = jnp.concatenate([w[:, o_wi:o_qk], w[:, o_ig:o_end], jnp.zeros((d, LANES - GATE_END), F32)],
                          axis=1).astype(BF16)
    gbias = jnp.concatenate([jnp.zeros((IDX_HEADS,), F32), b_igate[0], b_fgate[0],
                             jnp.zeros((LANES - GATE_END,), F32)]).reshape(1, LANES)
    wuk_bd = _block_diag(w_uk[0]).astype(BF16)
    wuv_bd = _block_diag(w_uv[0]).astype(BF16)
    kvg = row2(kv_norm_g)
    convw = conv_w[0].astype(F32)
    ffn1 = (bf(ffn1_w_gate), bf(ffn1_w_up), bf(ffn1_w_down), row2(ln1_g), row2(ln1_b))

    meta = meta_tokens.astype(F32)
    h1_meta = _ffn_ln_call(meta, *ffn1, alpha, N_META)
    zero_tail = jnp.zeros((CONV_HIST, MLQK_WIDTH), F32)
    (_, m_ckv, _, m_kidx, m_qk, m_v, _, m_gates, m_tail) = _inproj_call(
        h1_meta[None], zero_tail, wa, wm, wgt, wuk_bd, kvg, convw, gbias, N_META)

    h1 = _ffn_ln_call(x.reshape(bsz * seq, d), *ffn1, alpha, ROW_TILE)
    (qlat, ckv, qidx, kidx, qk, v, og, gates, _) = _inproj_call(
        h1.reshape(bsz, seq, d), m_tail[0], wa, wm, wgt, wuk_bd, kvg, convw, gbias, ROW_TILE)

    nchunks = seq // KEY_CHUNK
    ckv_c = ckv.reshape(bsz, nchunks, KEY_CHUNK, KV_LATENT)
    kidx_c = kidx.reshape(bsz, nchunks, KEY_CHUNK, IDX_DIM)
    wrow = jnp.swapaxes(gates[:, :, :SUBLANES], 1, 2)
    y_att = _dsa_call(qlat, qidx, wrow, ckv_c, jnp.swapaxes(ckv_c, 2, 3), kidx_c,
                      m_ckv[0], m_ckv[0].T, m_kidx[0], wuv_bd, topk)

    lane = jnp.arange(LANES)
    pad_gate = jnp.where((lane >= GATE_I0) & (lane < GATE_F0), NEG_BIG, 0.0).astype(F32)
    mg = jnp.concatenate([m_gates[0], jnp.broadcast_to(pad_gate, (ML_CHUNK - N_META, LANES))], axis=0)
    n_ml = seq // ML_CHUNK
    gate_rows = 2 * SUBLANES

    def chunk_t(a):
        return jnp.swapaxes(a.reshape(bsz, n_ml, ML_CHUNK, a.shape[2]), 2, 3)

    y_ml = _mlstm_call(qk, chunk_t(v), og, gates, chunk_t(gates[:, :, :gate_rows]),
                       _pad_rows(m_qk[0], ML_CHUNK), _pad_rows(m_v[0], ML_CHUNK).T, mg, mg[:, :gate_rows].T,
                       row2(ml_norm_g))

    out = _out_ffn_ln_call(
        y_att.reshape(bsz * seq, ATT_WIDTH), y_ml.reshape(bsz * seq, ML_WIDTH), h1, bf(w_out),
        row2(ln2_g), row2(ln2_b), bf(ffn2_w_gate), bf(ffn2_w_up), bf(ffn2_w_down), row2(ln3_g), row2(ln3_b),
        alpha, ROW_TILE)
    return out.reshape(bsz, seq, d)
```

```python
import functools

import jax
import jax.numpy as jnp
from jax import lax
from jax.experimental import pallas as pl
from jax.experimental.pallas import tpu as pltpu

F32 = jnp.float32
BF16 = jnp.bfloat16

N_META = 16
ATT_HEADS = 8
ATT_HEAD_DIM = 64
KV_LATENT = 128
IDX_HEADS = 4
IDX_DIM = 64
TOPK_MAX = 256
ML_HEADS = 4
ML_V_DIM = 128
ML_QK_DIM = 64
CONV_WIDTH = 4
GATE_SOFTCAP = 15.0
M_INIT = -1e30
LN_EPS = 1e-5
NEG_BIG = -1e30
LOG2_E = 1.4426950408889634

LANES = 128
SUBLANES = 8
VMEM_LIMIT_BYTES = 56 * 1024 * 1024

FF_CHUNK = 256
ROW_TILE = 512
FFN_TILE = 512
Q_TILE = 256
ATT_GROUP = 128
KEY_CHUNK = 256
N_BISECT = 16
REDUCE_ROWS = 32
ML_CHUNK = 256


def _dot(a, b):
    return jnp.dot(a, b, preferred_element_type=F32)


def _dot_nt(a, b):
    return lax.dot_general(a, b, (((1,), (1,)), ((), ())), preferred_element_type=F32)


def _layer_norm(z, g, b):
    mu = jnp.mean(z, axis=-1, keepdims=True)
    zc = z - mu
    var = jnp.mean(zc * zc, axis=-1, keepdims=True)
    return zc * lax.rsqrt(var + LN_EPS) * g + b


def _sigmoid(x):
    return 1.0 / (1.0 + jnp.exp(-x))


def _swiglu(xb, wg_ref, wu_ref, wd_ref):
    d_ff = wg_ref.shape[1]
    acc = jnp.zeros((xb.shape[0], wd_ref.shape[1]), F32)
    for c in range(d_ff // FF_CHUNK):
        sl = slice(c * FF_CHUNK, (c + 1) * FF_CHUNK)
        g = _dot(xb, wg_ref[:, sl])
        u = _dot(xb, wu_ref[:, sl])
        a = (g * _sigmoid(g) * u).astype(BF16)
        acc = acc + _dot(a, wd_ref[sl, :])
    return acc


def _ffn_ln(x, wg_ref, wu_ref, wd_ref, g, b, alpha):
    y = _swiglu(x.astype(BF16), wg_ref, wu_ref, wd_ref)
    return _layer_norm(alpha * x + 0.5 * y, g, b)


def _ffn_ln_kernel(x_ref, wg_ref, wu_ref, wd_ref, g_ref, b_ref, o_ref, *, alpha):
    o_ref[...] = _ffn_ln(x_ref[...], wg_ref, wu_ref, wd_ref, g_ref[...], b_ref[...], alpha)


def _resident(shape):
    return pl.BlockSpec(shape, lambda *_: (0,) * len(shape), pipeline_mode=pl.Buffered(1))


def _ffn_ln_call(x, wg, wu, wd, g, b, alpha, tm):
    rows, d = x.shape
    return pl.pallas_call(
        functools.partial(_ffn_ln_kernel, alpha=alpha),
        grid=(rows // tm,),
        in_specs=[
            pl.BlockSpec((tm, d), lambda i: (i, 0)),
            _resident(wg.shape), _resident(wu.shape), _resident(wd.shape),
            _resident(g.shape), _resident(b.shape),
        ],
        out_specs=pl.BlockSpec((tm, d), lambda i: (i, 0)),
        out_shape=jax.ShapeDtypeStruct((rows, d), F32),
        compiler_params=pltpu.CompilerParams(
            dimension_semantics=("arbitrary",), vmem_limit_bytes=VMEM_LIMIT_BYTES),
        name="ffn_ln",
    )(x, wg, wu, wd, g, b)


ATT_WIDTH = ATT_HEADS * ATT_HEAD_DIM
IDX_WIDTH = IDX_HEADS * IDX_DIM
MLQK_WIDTH = 2 * ML_HEADS * ML_QK_DIM
ML_WIDTH = ML_HEADS * ML_V_DIM
LAT_WIDTH = ATT_HEADS * KV_LATENT
CONV_HIST = SUBLANES
GATE_W0 = IDX_DIM
GATE_I0, GATE_F0, GATE_END = GATE_W0 + IDX_HEADS, GATE_W0 + IDX_HEADS + ML_HEADS, GATE_W0 + IDX_HEADS + 2 * ML_HEADS
GATE_ROWS = 2 * SUBLANES
HEAD_PAIRS = ATT_HEADS // 2


def _inproj_kernel(h_ref, tail_ref, wa_ref, wm_ref, wuk_ref, kvg_ref, convw_ref, gbias_ref,
                   qlat_ref, ckv_ref, qidx_ref, kidx_ref, qk_ref, v_ref, og_ref, gates_ref, tailout_ref,
                   carry_ref):
    tm = h_ref.shape[1]

    @pl.when(pl.program_id(1) == 0)
    def _():
        carry_ref[...] = tail_ref[...]

    xb = h_ref[0].astype(BF16)

    pa = _dot(xb, wa_ref[...])
    q_a = pa[:, :ATT_WIDTH].astype(BF16)
    c0 = ATT_WIDTH
    ckv = pa[:, c0:c0 + KV_LATENT]
    c1 = c0 + KV_LATENT
    ckv = ckv * lax.rsqrt(jnp.mean(ckv * ckv, axis=-1, keepdims=True) + LN_EPS) * kvg_ref[...]
    ckv_ref[0] = ckv.astype(BF16)
    qidx_ref[0] = pa[:, c1:c1 + IDX_WIDTH].astype(BF16)
    c2 = c1 + IDX_WIDTH
    kidx_ref[0] = pa[:, c2:c2 + IDX_DIM].astype(BF16)
    pair_in, pair_out = 2 * ATT_HEAD_DIM, 2 * KV_LATENT
    for p in range(HEAD_PAIRS):
        ql = _dot(q_a[:, p * pair_in:(p + 1) * pair_in], wuk_ref[p])
        qlat_ref[0, :, p * pair_out:(p + 1) * pair_out] = (ql * (ATT_HEAD_DIM ** -0.5 * LOG2_E)).astype(BF16)

    pm = _dot(xb, wm_ref[...])
    qk_raw = pm[:, :MLQK_WIDTH]
    v_ref[0] = pm[:, MLQK_WIDTH:MLQK_WIDTH + ML_WIDTH].astype(BF16)
    og_ref[0] = _sigmoid(pm[:, MLQK_WIDTH + ML_WIDTH:])

    ext = jnp.concatenate([carry_ref[...], qk_raw], axis=0)
    cw = convw_ref[...]
    conv = jnp.zeros_like(qk_raw)
    for j in range(CONV_WIDTH):
        s0 = CONV_HIST - (CONV_WIDTH - 1) + j
        conv = conv + ext[s0:s0 + tm] * cw[j:j + 1]
    act = conv * _sigmoid(conv)
    half = MLQK_WIDTH // 2
    qk_ref[0, :, :half] = act[:, :half].astype(BF16)
    qk_ref[0, :, half:] = (act[:, half:] * (ML_QK_DIM ** -0.5)).astype(BF16)
    carry_ref[...] = qk_raw[tm - CONV_HIST:]
    tailout_ref[0] = qk_raw[tm - CONV_HIST:]

    gr = pa[:, c2 + IDX_DIM - GATE_W0:]
    lane = lax.broadcasted_iota(jnp.int32, gr.shape, 1)
    sc = GATE_SOFTCAP * jnp.tanh((gr + gbias_ref[...]) / GATE_SOFTCAP)
    lf = -(jnp.maximum(-sc, 0.0) + jnp.log1p(jnp.exp(-jnp.abs(sc))))
    w_scaled = gr * (IDX_HEADS ** -0.5 * IDX_DIM ** -0.5)
    gates_ref[0] = jnp.where((lane < GATE_W0) | (lane >= GATE_END), 0.0,
                             jnp.where(lane < GATE_I0, w_scaled, jnp.where(lane < GATE_F0, sc, lf)))


def _inproj_call(h, tail, wa, wm, wuk_bd, kvg, convw, gbias, tm):
    bn, rows, d = h.shape
    nblk = rows // tm

    def row_spec(width):
        return pl.BlockSpec((1, tm, width), lambda b, j: (b, j, 0))

    outs = [
        (LAT_WIDTH, BF16), (KV_LATENT, BF16), (IDX_WIDTH, BF16), (IDX_DIM, BF16),
        (MLQK_WIDTH, BF16), (ML_WIDTH, BF16), (ML_WIDTH, F32), (LANES, F32),
    ]
    out_shape = [jax.ShapeDtypeStruct((bn, rows, w), dt) for w, dt in outs]
    out_specs = [row_spec(w) for w, _ in outs]
    out_shape.append(jax.ShapeDtypeStruct((bn, CONV_HIST, MLQK_WIDTH), F32))
    out_specs.append(pl.BlockSpec((1, CONV_HIST, MLQK_WIDTH), lambda b, j: (b, 0, 0)))
    return pl.pallas_call(
        _inproj_kernel,
        grid=(bn, nblk),
        in_specs=[
            row_spec(d),
            _resident(tail.shape), _resident(wa.shape), _resident(wm.shape),
            _resident(wuk_bd.shape), _resident(kvg.shape), _resident(convw.shape), _resident(gbias.shape),
        ],
        out_specs=out_specs,
        out_shape=out_shape,
        scratch_shapes=[pltpu.VMEM((CONV_HIST, MLQK_WIDTH), F32)],
        compiler_params=pltpu.CompilerParams(
            dimension_semantics=("arbitrary", "arbitrary"), vmem_limit_bytes=VMEM_LIMIT_BYTES),
        name="in_proj",
    )(h, tail, wa, wm, wuk_bd, kvg, convw, gbias)


def _dsa_kernel(qlat_ref, qidx_ref, wrow_ref, ckv_ref, ckvt_ref, kidx_ref, mckv_ref, mckvt_ref, mkidx_ref,
                wuv_ref, y_ref, s_ref, acc_ref, *, topk):
    _, kc, tq = s_ref.shape
    i = pl.program_id(1)
    nch = ((i + 1) * tq + kc - 1) // kc
    qreal = i * tq + lax.broadcasted_iota(jnp.int32, (1, tq), 1)
    kf = float(topk)

    wrow = wrow_ref[0]
    qidx = qidx_ref[0]
    q_idx_all = jnp.concatenate([qidx[:, h * IDX_DIM:(h + 1) * IDX_DIM] for h in range(IDX_HEADS)], axis=0)
    wi = [wrow[h:h + 1, :] for h in range(IDX_HEADS)]

    def scores(k_rows):
        lg = _dot_nt(k_rows, q_idx_all)
        sc = jnp.zeros((k_rows.shape[0], tq), F32)
        for h in range(IDX_HEADS):
            sc = sc + jnp.maximum(lg[:, h * tq:(h + 1) * tq], 0.0) * wi[h]
        return sc

    s_meta = scores(mkidx_ref[...])

    def score_chunk(c, lo, hi):
        sc = scores(kidx_ref[0, c])
        valid = c * kc + lax.broadcasted_iota(jnp.int32, (kc, tq), 0) <= qreal
        s_ref[c] = jnp.where(valid, sc, -jnp.inf)
        groups = (kc // REDUCE_ROWS, REDUCE_ROWS, tq)
        lo = jnp.minimum(lo, jnp.min(jnp.where(valid, sc, jnp.inf).reshape(groups), axis=0))
        hi = jnp.maximum(hi, jnp.max(jnp.where(valid, sc, -jnp.inf).reshape(groups), axis=0))
        return lo, hi

    def score_pair(c2, carry):
        lo, hi = score_chunk(2 * c2, *carry)
        return score_chunk(jnp.minimum(2 * c2 + 1, nch - 1), lo, hi)

    lo, hi = lax.fori_loop(0, (nch + 1) // 2, score_pair,
                           (jnp.full((REDUCE_ROWS, tq), jnp.inf, F32), jnp.full((REDUCE_ROWS, tq), -jnp.inf, F32)))
    lo = jnp.minimum(jnp.min(lo, axis=0, keepdims=True), jnp.min(s_meta, axis=0, keepdims=True))
    hi = jnp.maximum(jnp.max(hi, axis=0, keepdims=True), jnp.max(s_meta, axis=0, keepdims=True))

    def key_reduce(reduce, combine, per_chunk, init):
        def body(c, acc):
            x = per_chunk(s_ref[c]).reshape(kc // REDUCE_ROWS, REDUCE_ROWS, tq)
            return combine(acc, reduce(x, axis=0))
        acc = lax.fori_loop(0, nch, body, jnp.full((REDUCE_ROWS, tq), init, F32))
        return combine(reduce(acc, axis=0, keepdims=True), reduce(per_chunk(s_meta), axis=0, keepdims=True))

    def count(pred):
        return key_reduce(jnp.sum, jnp.add, lambda sc: jnp.where(pred(sc), 1.0, 0.0), 0.0)

    def max_where(pred):
        return key_reduce(jnp.max, jnp.maximum, lambda sc: jnp.where(pred(sc), sc, -jnp.inf), -jnp.inf)

    def bisect(_, carry):
        lo, hi = carry
        mid = 0.5 * lo + 0.5 * hi
        up = count(lambda sc: sc > mid) >= kf
        return jnp.where(up, mid, lo), jnp.where(up, hi, mid)

    lo, hi = lax.fori_loop(0, N_BISECT, bisect, (lo, hi))

    n_valid = (qreal + (N_META + 1)).astype(F32)
    small = n_valid <= kf
    cand = max_where(lambda sc: sc <= hi)
    done = jnp.where(small | (count(lambda sc: sc >= cand) >= kf), 1.0, 0.0)

    def not_finished(state):
        return jnp.min(state[1]) < 0.5

    def step_down(state):
        cand, done = state
        nxt = jnp.where(done > 0.5, cand, max_where(lambda sc: sc < cand))
        fin = count(lambda sc: sc >= nxt) >= kf
        return nxt, jnp.where(fin, 1.0, done)

    cand, _ = lax.while_loop(not_finished, step_down, (cand, done))
    thr = jnp.where(small, -jnp.inf, cand)
    n_gt = count(lambda sc: sc > thr)
    n_eq = count(lambda sc: sc == thr)
    need = jnp.where(small, 0.0, kf - n_gt)
    ranked_ties = jnp.max(jnp.where(n_eq > need, 1.0, 0.0)) > 0.5

    qlat = qlat_ref[0]
    n_groups = tq // ATT_GROUP
    onehot = (lax.broadcasted_iota(jnp.int32, (ATT_GROUP, ATT_GROUP), 0)
              == lax.broadcasted_iota(jnp.int32, (ATT_GROUP, ATT_GROUP), 1)).astype(BF16)
    q_aug = [jnp.concatenate(
        [jnp.concatenate([qlat[g * ATT_GROUP:(g + 1) * ATT_GROUP, h * KV_LATENT:(h + 1) * KV_LATENT], onehot], axis=1)
         for h in range(ATT_HEADS)], axis=0) for g in range(n_groups)]
    hq = ATT_HEADS * ATT_GROUP

    def lower_tri(n):
        return (lax.broadcasted_iota(jnp.int32, (n, n), 1) <= lax.broadcasted_iota(jnp.int32, (n, n), 0)).astype(BF16)

    def attention(ranked):
        def mask_bias(sc, eq_seen):
            if not ranked:
                return jnp.where(sc >= thr, 0.0, NEG_BIG).astype(BF16), eq_seen
            n = sc.shape[0]
            eq = sc == thr
            rank = _dot(lower_tri(n), jnp.where(eq, 1.0, 0.0).astype(BF16)) + eq_seen
            keep = (sc > thr) | (eq & (rank <= need))
            return jnp.where(keep, 0.0, NEG_BIG).astype(BF16), rank[n - 1:n, :]

        def logits(g, kv, bias):
            k_aug = jnp.concatenate([kv, bias[:, g * ATT_GROUP:(g + 1) * ATT_GROUP]], axis=1)
            return _dot_nt(k_aug, q_aug[g])

        def attend(c, carry):
            bias, eq_seen = mask_bias(s_ref[c], carry[0])
            out = [eq_seen]
            for g in range(n_groups):
                m, l = carry[1 + 2 * g:3 + 2 * g]
                s = logits(g, ckv_ref[0, c], bias)
                m_new = jnp.maximum(m, jnp.max(s, axis=0, keepdims=True))
                a = jnp.exp2(m - m_new)
                p = jnp.exp2(s - m_new)
                acc_ref[g] = a * acc_ref[g] + _dot(ckvt_ref[0, c], p.astype(BF16))
                out += [m_new, a * l + jnp.sum(p, axis=0, keepdims=True)]
            return tuple(out)

        bias_m, eq_seen = mask_bias(s_meta, jnp.zeros((1, tq), F32))
        bias_0, eq_seen = mask_bias(s_ref[0], eq_seen)
        init = [eq_seen]
        for g in range(n_groups):
            s_m = logits(g, mckv_ref[...], bias_m)
            s_0 = logits(g, ckv_ref[0, 0], bias_0)
            m = jnp.maximum(jnp.max(s_m, axis=0, keepdims=True), jnp.max(s_0, axis=0, keepdims=True))
            p_m = jnp.exp2(s_m - m)
            p_0 = jnp.exp2(s_0 - m)
            acc_ref[g] = _dot(mckvt_ref[...], p_m.astype(BF16)) + _dot(ckvt_ref[0, 0], p_0.astype(BF16))
            init += [m, jnp.sum(p_m, axis=0, keepdims=True) + jnp.sum(p_0, axis=0, keepdims=True)]
        carry = lax.fori_loop(1, nch, attend, tuple(init))
        rows = []
        for g in range(n_groups):
            o_t = (acc_ref[g] / carry[2 + 2 * g]).T
            rows.append(jnp.concatenate([o_t[h * ATT_GROUP:(h + 1) * ATT_GROUP] for h in range(ATT_HEADS)], axis=1))
        return _dot(jnp.concatenate(rows, axis=0).astype(BF16), wuv_ref[...]).astype(BF16)

    y_ref[0] = lax.cond(ranked_ties, lambda: attention(True), lambda: attention(False))


def _dsa_call(qlat, qidx, wrow, ckv_c, ckvt_c, kidx_c, m_ckv, m_ckvt, m_kidx, wuv_bd, topk):
    bn, rows, _ = qlat.shape
    nchunks, kc = ckv_c.shape[1], ckv_c.shape[2]
    nq = rows // Q_TILE

    def q_spec(width):
        return pl.BlockSpec((1, Q_TILE, width), lambda b, i: (b, i, 0))

    def k_spec(a):
        return pl.BlockSpec((1,) + a.shape[1:], lambda b, i: (b, 0, 0, 0))

    return pl.pallas_call(
        functools.partial(_dsa_kernel, topk=topk),
        grid=(bn, nq),
        in_specs=[q_spec(LAT_WIDTH), q_spec(IDX_WIDTH),
                  pl.BlockSpec((1, SUBLANES, Q_TILE), lambda b, i: (b, 0, i)),
                  k_spec(ckv_c), k_spec(ckvt_c), k_spec(kidx_c),
                  _resident(m_ckv.shape), _resident(m_ckvt.shape), _resident(m_kidx.shape), _resident(wuv_bd.shape)],
        out_specs=q_spec(ATT_WIDTH),
        out_shape=jax.ShapeDtypeStruct((bn, rows, ATT_WIDTH), BF16),
        scratch_shapes=[pltpu.VMEM((nchunks, kc, Q_TILE), F32),
                        pltpu.VMEM((Q_TILE // ATT_GROUP, ckvt_c.shape[2], ATT_HEADS * ATT_GROUP), F32)],
        compiler_params=pltpu.CompilerParams(
            dimension_semantics=("arbitrary", "arbitrary"), vmem_limit_bytes=VMEM_LIMIT_BYTES),
        name="dsa",
    )(qlat, qidx, wrow, ckv_c, ckvt_c, kidx_c, m_ckv, m_ckvt, m_kidx, wuv_bd)


def _split3(x):
    hi = x.astype(BF16)
    r = x - hi.astype(F32)
    mid = r.astype(BF16)
    lo = (r - mid.astype(F32)).astype(BF16)
    return hi, mid, lo


ML_EXT = ML_V_DIM + 16


def _mlstm_chunk(qk, vt, g, gt, state):
    L = qk.shape[0]
    s_idx = lax.broadcasted_iota(jnp.int32, (L, L), 0)
    t_idx = lax.broadcasted_iota(jnp.int32, (L, L), 1)
    causal = s_idx <= t_idx
    b_cols = sum(_dot((t_idx <= s_idx).astype(BF16), part) for part in _split3(g))
    b_rows = sum(_dot(part, causal.astype(BF16)) for part in _split3(gt))
    ones_blk = jnp.where(lax.broadcasted_iota(jnp.int32, (ML_EXT - ML_V_DIM, L), 0) == 0, 1.0, 0.0).astype(BF16)
    kq = ML_HEADS * ML_QK_DIM

    outs, new_state = [], []
    for h in range(ML_HEADS):
        ce, m_prev = state[h]
        c_col = g[:, GATE_I0 + h:GATE_I0 + h + 1] - b_cols[:, GATE_F0 + h:GATE_F0 + h + 1]
        b_row = b_rows[GATE_F0 - GATE_W0 + h:GATE_F0 - GATE_W0 + h + 1, :]
        ig_row = gt[GATE_I0 - GATE_W0 + h:GATE_I0 - GATE_W0 + h + 1, :]
        qh = qk[:, h * ML_QK_DIM:(h + 1) * ML_QK_DIM]
        kh = qk[:, kq + h * ML_QK_DIM:kq + (h + 1) * ML_QK_DIM]
        vt_ext = jnp.concatenate([vt[h * ML_V_DIM:(h + 1) * ML_V_DIM, :], ones_blk], axis=0)

        d_t = jnp.where(causal, c_col + b_row, -jnp.inf)
        inter = b_row + m_prev
        m_t = jnp.maximum(jnp.max(d_t, axis=0, keepdims=True), inter)
        w_inter = jnp.exp(inter - m_t)
        s_t = _dot_nt(kh, qh) * jnp.exp(d_t - m_t)
        r = _dot(vt_ext, s_t.astype(BF16)) + _dot_nt(ce.astype(BF16), qh) * w_inter
        num = r[:ML_V_DIM]
        den = r[ML_V_DIM:ML_V_DIM + 1]
        hh = num / jnp.maximum(jnp.abs(den), jnp.exp(-m_t))
        mu = jnp.mean(hh, axis=0, keepdims=True)
        hc = hh - mu
        var = jnp.mean(hc * hc, axis=0, keepdims=True)
        outs.append((hc * lax.rsqrt(var + LN_EPS)).T)

        b_end = b_row[:, L - 1:L]
        g_row = b_end - b_row + ig_row
        m_new = jnp.maximum(b_end + m_prev, jnp.max(g_row, axis=1, keepdims=True))
        decay = jnp.exp(b_end + m_prev - m_new)
        weighted = (vt_ext.astype(F32) * jnp.exp(g_row - m_new)).astype(BF16)
        new_state.append((decay * ce + _dot(weighted, kh), m_new))
    return outs, new_state


def _mlstm_kernel(qk_ref, vt_ref, og_ref, gates_ref, gt_ref, mqk_ref, mvt_ref, mgates_ref, mgt_ref, ng_ref, y_ref,
                  ce0_ref, m0_ref):
    L = mqk_ref.shape[0]
    n_chunks = qk_ref.shape[1] // L
    norm_g = ng_ref[...]

    @pl.when(pl.program_id(0) == 0)
    def _():
        state = [(jnp.zeros((ML_EXT, ML_QK_DIM), F32), jnp.full((1, 1), M_INIT, F32)) for _ in range(ML_HEADS)]
        _, state = _mlstm_chunk(mqk_ref[...], mvt_ref[...], mgates_ref[...], mgt_ref[...], state)
        for h in range(ML_HEADS):
            ce0_ref[h] = state[h][0]
            m0_ref[h] = jnp.broadcast_to(state[h][1], m0_ref.shape[1:])

    state = [(ce0_ref[h], m0_ref[h][0:1, 0:1]) for h in range(ML_HEADS)]

    def body(c, flat):
        state = [(flat[2 * h], flat[2 * h + 1]) for h in range(ML_HEADS)]
        rows = pl.ds(pl.multiple_of(c * L, L), L)
        outs, state = _mlstm_chunk(qk_ref[0, rows, :], vt_ref[0, c], gates_ref[0, rows, :], gt_ref[0, c], state)
        y = jnp.concatenate(outs, axis=1) * norm_g * og_ref[0, rows, :]
        y_ref[0, rows, :] = y.astype(BF16)
        return tuple(x for pair in state for x in pair)

    lax.fori_loop(0, n_chunks, body, tuple(x for pair in state for x in pair))


def _mlstm_call(qk, vt_c, og, gates, gt_c, mqk, mvt, mgates, mgt, norm_g):
    bn, rows, _ = qk.shape

    def b_spec(a):
        return pl.BlockSpec((1,) + a.shape[1:], lambda b: (b,) + (0,) * (a.ndim - 1))

    consts = (mqk, mvt, mgates, mgt, norm_g)
    return pl.pallas_call(
        _mlstm_kernel,
        grid=(bn,),
        in_specs=[b_spec(qk), b_spec(vt_c), b_spec(og), b_spec(gates), b_spec(gt_c)]
        + [_resident(c.shape) for c in consts],
        out_specs=pl.BlockSpec((1, rows, ML_WIDTH), lambda b: (b, 0, 0)),
        out_shape=jax.ShapeDtypeStruct((bn, rows, ML_WIDTH), BF16),
        scratch_shapes=[pltpu.VMEM((ML_HEADS, ML_EXT, ML_QK_DIM), F32), pltpu.VMEM((ML_HEADS, SUBLANES, LANES), F32)],
        compiler_params=pltpu.CompilerParams(
            dimension_semantics=("arbitrary",), vmem_limit_bytes=VMEM_LIMIT_BYTES),
        name="mlstm",
    )(qk, vt_c, og, gates, gt_c, *consts)


def _out_ffn_ln_kernel(ya_ref, ym_ref, h_ref, wo_ref, g2_ref, b2_ref, wg_ref, wu_ref, wd_ref, g3_ref, b3_ref,
                       o_ref, *, alpha):
    mix = _dot(ya_ref[...], wo_ref[:ATT_WIDTH, :]) + _dot(ym_ref[...], wo_ref[ATT_WIDTH:, :])
    h2 = _layer_norm(alpha * h_ref[...] + mix, g2_ref[...], b2_ref[...])
    o_ref[...] = _ffn_ln(h2, wg_ref, wu_ref, wd_ref, g3_ref[...], b3_ref[...], alpha)


def _out_ffn_ln_call(ya, ym, h, wo, g2, b2, wg, wu, wd, g3, b3, alpha, tm):
    rows, d = h.shape

    def row_spec(width):
        return pl.BlockSpec((tm, width), lambda i: (i, 0))

    consts = (wo, g2, b2, wg, wu, wd, g3, b3)
    return pl.pallas_call(
        functools.partial(_out_ffn_ln_kernel, alpha=alpha),
        grid=(rows // tm,),
        in_specs=[row_spec(ya.shape[1]), row_spec(ym.shape[1]), row_spec(d)] + [_resident(c.shape) for c in consts],
        out_specs=row_spec(d),
        out_shape=jax.ShapeDtypeStruct((rows, d), F32),
        compiler_params=pltpu.CompilerParams(
            dimension_semantics=("arbitrary",), vmem_limit_bytes=VMEM_LIMIT_BYTES),
        name="out_ffn_ln",
    )(ya, ym, h, *consts)


def _block_diag(w):
    nh, a, b = w.shape
    eye = jnp.eye(nh, dtype=w.dtype)
    return (eye[:, None, :, None] * w[:, :, None, :]).reshape(nh * a, nh * b)


def _pad_rows(a, rows, value=0.0):
    return jnp.pad(a, ((0, rows - a.shape[0]), (0, 0)), constant_values=value)


def kernel(x, meta_tokens, ln1_g, ln1_b, ffn1_w_gate, ffn1_w_up, ffn1_w_down, w_in, w_uk, w_uv, kv_norm_g,
           conv_w, b_igate, b_fgate, ml_norm_g, w_out, ln2_g, ln2_b, ffn2_w_gate, ffn2_w_up, ffn2_w_down,
           ln3_g, ln3_b):
    depth = ln1_g.shape[0]
    assert depth == 1, "the meta-token shortcut below is only valid for a single layer"
    bsz, seq, d = x.shape
    assert seq % ROW_TILE == 0 and seq % ML_CHUNK == 0 and seq % Q_TILE == 0
    alpha = (2 * depth) ** 0.25
    topk = min(TOPK_MAX, seq // 4)

    row2 = lambda p: p[0].reshape(1, -1).astype(F32)
    bf = lambda w: w[0].astype(BF16)

    w = w_in[0]
    o_qa, o_ckv, o_qi, o_ki, o_wi = 0, 512, 640, 896, 960
    o_qk, o_v, o_o, o_ig, o_fg, o_end = 964, 1476, 1988, 2500, 2504, 2508
    wa = jnp.concatenate([w[:, o_qa:o_wi], w[:, o_wi:o_qk], w[:, o_ig:o_end],
                          jnp.zeros((d, LANES - GATE_END), F32)], axis=1).astype(BF16)
    wm = w[:, o_qk:o_ig].astype(BF16)
    gbias = jnp.concatenate([jnp.zeros((GATE_I0,), F32), b_igate[0], b_fgate[0],
                             jnp.zeros((LANES - GATE_END,), F32)]).reshape(1, LANES)
    wuk_bd = jnp.stack([_block_diag(w_uk[0][2 * p:2 * p + 2]) for p in range(HEAD_PAIRS)]).astype(BF16)
    wuv_bd = _block_diag(w_uv[0]).astype(BF16)
    kvg = row2(kv_norm_g)
    convw = conv_w[0].astype(F32)
    ffn1 = (bf(ffn1_w_gate), bf(ffn1_w_up), bf(ffn1_w_down), row2(ln1_g), row2(ln1_b))

    meta = meta_tokens.astype(F32)
    h1_meta = _ffn_ln_call(meta, *ffn1, alpha, N_META)
    zero_tail = jnp.zeros((CONV_HIST, MLQK_WIDTH), F32)
    (_, m_ckv, _, m_kidx, m_qk, m_v, _, m_gates, m_tail) = _inproj_call(
        h1_meta[None], zero_tail, wa, wm, wuk_bd, kvg, convw, gbias, N_META)

    h1 = _ffn_ln_call(x.reshape(bsz * seq, d), *ffn1, alpha, FFN_TILE)
    (qlat, ckv, qidx, kidx, qk, v, og, gates, _) = _inproj_call(
        h1.reshape(bsz, seq, d), m_tail[0], wa, wm, wuk_bd, kvg, convw, gbias, ROW_TILE)

    nchunks = seq // KEY_CHUNK
    ckv_c = ckv.reshape(bsz, nchunks, KEY_CHUNK, KV_LATENT)
    kidx_c = kidx.reshape(bsz, nchunks, KEY_CHUNK, IDX_DIM)
    wrow = jnp.swapaxes(gates[:, :, GATE_W0:GATE_W0 + SUBLANES], 1, 2)
    y_att = _dsa_call(qlat, qidx, wrow, ckv_c, jnp.swapaxes(ckv_c, 2, 3), kidx_c,
                      m_ckv[0], m_ckv[0].T, m_kidx[0], wuv_bd, topk)

    lane = jnp.arange(LANES)
    pad_gate = jnp.where((lane >= GATE_I0) & (lane < GATE_F0), NEG_BIG, 0.0).astype(F32)
    mg = jnp.concatenate([m_gates[0], jnp.broadcast_to(pad_gate, (ML_CHUNK - N_META, LANES))], axis=0)
    n_ml = seq // ML_CHUNK
    gate_lanes = slice(GATE_W0, GATE_W0 + GATE_ROWS)

    def chunk_t(a):
        return jnp.swapaxes(a.reshape(bsz, n_ml, ML_CHUNK, a.shape[2]), 2, 3)

    y_ml = _mlstm_call(qk, chunk_t(v), og, gates, chunk_t(gates[:, :, gate_lanes]),
                       _pad_rows(m_qk[0], ML_CHUNK), _pad_rows(m_v[0], ML_CHUNK).T, mg, mg[:, gate_lanes].T,
                       row2(ml_norm_g))

    out = _out_ffn_ln_call(
        y_att.reshape(bsz * seq, ATT_WIDTH), y_ml.reshape(bsz * seq, ML_WIDTH), h1, bf(w_out),
        row2(ln2_g), row2(ln2_b), bf(ffn2_w_gate), bf(ffn2_w_up), bf(ffn2_w_down), row2(ln3_g), row2(ln3_b),
        alpha, FFN_TILE)
    return out.reshape(bsz, seq, d)
```

```python
import functools

import jax
import jax.numpy as jnp
from jax import lax
from jax.experimental import pallas as pl
from jax.experimental.pallas import tpu as pltpu

F32 = jnp.float32
BF16 = jnp.bfloat16

N_META = 16
ATT_HEADS = 8
ATT_HEAD_DIM = 64
KV_LATENT = 128
IDX_HEADS = 4
IDX_DIM = 64
TOPK_MAX = 256
ML_HEADS = 4
ML_V_DIM = 128
ML_QK_DIM = 64
CONV_WIDTH = 4
GATE_SOFTCAP = 15.0
M_INIT = -1e30
LN_EPS = 1e-5
NEG_BIG = -1e30
LOG2_E = 1.4426950408889634

LANES = 128
SUBLANES = 8
VMEM_LIMIT_BYTES = 56 * 1024 * 1024

FF_CHUNK = 256
ROW_TILE = 512
FFN_TILE = 512
Q_TILE = 256
ATT_GROUP = 128
KEY_CHUNK = 256
N_BISECT = 16
REDUCE_ROWS = 32
ML_CHUNK = 256


def _dot(a, b):
    return jnp.dot(a, b, preferred_element_type=F32)


def _dot_nt(a, b):
    return lax.dot_general(a, b, (((1,), (1,)), ((), ())), preferred_element_type=F32)


def _layer_norm(z, g, b):
    mu = jnp.mean(z, axis=-1, keepdims=True)
    zc = z - mu
    var = jnp.mean(zc * zc, axis=-1, keepdims=True)
    return zc * lax.rsqrt(var + LN_EPS) * g + b


def _sigmoid(x):
    return 1.0 / (1.0 + jnp.exp(-x))


def _swiglu(xb, wg_ref, wu_ref, wd_ref):
    d_ff = wg_ref.shape[1]
    acc = jnp.zeros((xb.shape[0], wd_ref.shape[1]), F32)
    for c in range(d_ff // FF_CHUNK):
        sl = slice(c * FF_CHUNK, (c + 1) * FF_CHUNK)
        g = _dot(xb, wg_ref[:, sl])
        u = _dot(xb, wu_ref[:, sl])
        a = (g * _sigmoid(g) * u).astype(BF16)
        acc = acc + _dot(a, wd_ref[sl, :])
    return acc


def _ffn_ln(x, wg_ref, wu_ref, wd_ref, g, b, alpha):
    y = _swiglu(x.astype(BF16), wg_ref, wu_ref, wd_ref)
    return _layer_norm(alpha * x + 0.5 * y, g, b)


def _ffn_ln_kernel(x_ref, wg_ref, wu_ref, wd_ref, g_ref, b_ref, o_ref, *, alpha):
    o_ref[...] = _ffn_ln(x_ref[...], wg_ref, wu_ref, wd_ref, g_ref[...], b_ref[...], alpha)


def _resident(shape):
    return pl.BlockSpec(shape, lambda *_: (0,) * len(shape), pipeline_mode=pl.Buffered(1))


def _ffn_ln_call(x, wg, wu, wd, g, b, alpha, tm):
    rows, d = x.shape
    return pl.pallas_call(
        functools.partial(_ffn_ln_kernel, alpha=alpha),
        grid=(rows // tm,),
        in_specs=[
            pl.BlockSpec((tm, d), lambda i: (i, 0)),
            _resident(wg.shape), _resident(wu.shape), _resident(wd.shape),
            _resident(g.shape), _resident(b.shape),
        ],
        out_specs=pl.BlockSpec((tm, d), lambda i: (i, 0)),
        out_shape=jax.ShapeDtypeStruct((rows, d), F32),
        compiler_params=pltpu.CompilerParams(
            dimension_semantics=("arbitrary",), vmem_limit_bytes=VMEM_LIMIT_BYTES),
        name="ffn_ln",
    )(x, wg, wu, wd, g, b)


ATT_WIDTH = ATT_HEADS * ATT_HEAD_DIM
IDX_WIDTH = IDX_HEADS * IDX_DIM
MLQK_WIDTH = 2 * ML_HEADS * ML_QK_DIM
ML_WIDTH = ML_HEADS * ML_V_DIM
LAT_WIDTH = ATT_HEADS * KV_LATENT
CONV_HIST = SUBLANES
GATE_W0 = IDX_DIM
GATE_I0, GATE_F0, GATE_END = GATE_W0 + IDX_HEADS, GATE_W0 + IDX_HEADS + ML_HEADS, GATE_W0 + IDX_HEADS + 2 * ML_HEADS
GATE_ROWS = 2 * SUBLANES
GT_CHUNK = 256
HEAD_PAIRS = ATT_HEADS // 2


def _inproj_kernel(h_ref, tail_ref, wa_ref, wm_ref, wuk_ref, kvg_ref, convw_ref, gbias_ref,
                   qlat_ref, ckv_ref, qidx_ref, kidx_ref, qk_ref, v_ref, og_ref, gates_ref, tailout_ref, *rest):
    carry_ref = rest[-1]
    tm = h_ref.shape[1]

    @pl.when(pl.program_id(1) == 0)
    def _():
        carry_ref[...] = tail_ref[...]

    xb = h_ref[0].astype(BF16)

    pa = _dot_nt(xb, wa_ref[...])
    q_a = pa[:, :ATT_WIDTH].astype(BF16)
    c0 = ATT_WIDTH
    ckv = pa[:, c0:c0 + KV_LATENT]
    c1 = c0 + KV_LATENT
    ckv = ckv * lax.rsqrt(jnp.mean(ckv * ckv, axis=-1, keepdims=True) + LN_EPS) * kvg_ref[...]
    ckv_ref[0] = ckv.astype(BF16)
    qidx_ref[0] = pa[:, c1:c1 + IDX_WIDTH].astype(BF16)
    c2 = c1 + IDX_WIDTH
    kidx_ref[0] = pa[:, c2:c2 + IDX_DIM].astype(BF16)
    pair_in, pair_out = 2 * ATT_HEAD_DIM, 2 * KV_LATENT
    for p in range(HEAD_PAIRS):
        ql = _dot(q_a[:, p * pair_in:(p + 1) * pair_in], wuk_ref[p])
        qlat_ref[0, :, p * pair_out:(p + 1) * pair_out] = (ql * (ATT_HEAD_DIM ** -0.5 * LOG2_E)).astype(BF16)

    pm = _dot_nt(xb, wm_ref[...])
    qk_raw = pm[:, :MLQK_WIDTH]
    v_ref[0] = pm[:, MLQK_WIDTH:MLQK_WIDTH + ML_WIDTH].astype(BF16)
    og_ref[0] = _sigmoid(pm[:, MLQK_WIDTH + ML_WIDTH:])

    ext = jnp.concatenate([carry_ref[...], qk_raw], axis=0)
    cw = convw_ref[...]
    conv = jnp.zeros_like(qk_raw)
    for j in range(CONV_WIDTH):
        s0 = CONV_HIST - (CONV_WIDTH - 1) + j
        conv = conv + ext[s0:s0 + tm] * cw[j:j + 1]
    act = conv * _sigmoid(conv)
    half = MLQK_WIDTH // 2
    qk_ref[0, :, :half] = act[:, :half].astype(BF16)
    qk_ref[0, :, half:] = (act[:, half:] * (ML_QK_DIM ** -0.5)).astype(BF16)
    carry_ref[...] = qk_raw[tm - CONV_HIST:]
    tailout_ref[0] = qk_raw[tm - CONV_HIST:]

    gr = pa[:, c2 + IDX_DIM - GATE_W0:]
    lane = lax.broadcasted_iota(jnp.int32, gr.shape, 1)
    sc = GATE_SOFTCAP * jnp.tanh((gr + gbias_ref[...]) / GATE_SOFTCAP)
    lf = -(jnp.maximum(-sc, 0.0) + jnp.log1p(jnp.exp(-jnp.abs(sc))))
    w_scaled = gr * (IDX_HEADS ** -0.5 * IDX_DIM ** -0.5)
    gates = jnp.where((lane < GATE_W0) | (lane >= GATE_END), 0.0,
                      jnp.where(lane < GATE_I0, w_scaled, jnp.where(lane < GATE_F0, sc, lf)))
    gates_ref[0] = gates
    if len(rest) == 2:
        gt_ref = rest[0]
        gates_t = gates.T[GATE_W0:GATE_W0 + GATE_ROWS]
        for j in range(tm // GT_CHUNK):
            gt_ref[0, j] = gates_t[:, j * GT_CHUNK:(j + 1) * GT_CHUNK]


def _inproj_call(h, tail, wa, wm, wuk_bd, kvg, convw, gbias, tm):
    bn, rows, d = h.shape
    nblk = rows // tm
    emit_gt = tm % GT_CHUNK == 0

    def row_spec(width):
        return pl.BlockSpec((1, tm, width), lambda b, j: (b, j, 0))

    outs = [
        (LAT_WIDTH, BF16), (KV_LATENT, BF16), (IDX_WIDTH, BF16), (IDX_DIM, BF16),
        (MLQK_WIDTH, BF16), (ML_WIDTH, BF16), (ML_WIDTH, F32), (LANES, F32),
    ]
    out_shape = [jax.ShapeDtypeStruct((bn, rows, w), dt) for w, dt in outs]
    out_specs = [row_spec(w) for w, _ in outs]
    out_shape.append(jax.ShapeDtypeStruct((bn, CONV_HIST, MLQK_WIDTH), F32))
    out_specs.append(pl.BlockSpec((1, CONV_HIST, MLQK_WIDTH), lambda b, j: (b, 0, 0)))
    if emit_gt:
        per_tile = tm // GT_CHUNK
        out_shape.append(jax.ShapeDtypeStruct((bn, rows // GT_CHUNK, GATE_ROWS, GT_CHUNK), F32))
        out_specs.append(pl.BlockSpec((1, per_tile, GATE_ROWS, GT_CHUNK), lambda b, j: (b, j, 0, 0)))
    return pl.pallas_call(
        _inproj_kernel,
        grid=(bn, nblk),
        in_specs=[
            row_spec(d),
            _resident(tail.shape), _resident(wa.shape), _resident(wm.shape),
            _resident(wuk_bd.shape), _resident(kvg.shape), _resident(convw.shape), _resident(gbias.shape),
        ],
        out_specs=out_specs,
        out_shape=out_shape,
        scratch_shapes=[pltpu.VMEM((CONV_HIST, MLQK_WIDTH), F32)],
        compiler_params=pltpu.CompilerParams(
            dimension_semantics=("arbitrary", "arbitrary"), vmem_limit_bytes=VMEM_LIMIT_BYTES),
        name="in_proj",
    )(h, tail, wa, wm, wuk_bd, kvg, convw, gbias)


def _dsa_kernel(qlat_ref, qidx_ref, wrow_ref, ckv_ref, ckvt_ref, kidx_ref, mckv_ref, mckvt_ref, mkidx_ref,
                wuv_ref, y_ref, s_ref, acc_ref, p_ref, *, topk):
    _, kc, tq = s_ref.shape
    i = pl.program_id(1)
    nch = ((i + 1) * tq + kc - 1) // kc
    qreal = i * tq + lax.broadcasted_iota(jnp.int32, (1, tq), 1)
    kf = float(topk)

    wrow = wrow_ref[0, 0]
    qidx = qidx_ref[0]
    q_idx_all = jnp.concatenate([qidx[:, h * IDX_DIM:(h + 1) * IDX_DIM] for h in range(IDX_HEADS)], axis=0)
    wi = [wrow[h:h + 1, :] for h in range(IDX_HEADS)]

    def scores(k_rows):
        lg = _dot_nt(k_rows, q_idx_all)
        sc = jnp.zeros((k_rows.shape[0], tq), F32)
        for h in range(IDX_HEADS):
            sc = sc + jnp.maximum(lg[:, h * tq:(h + 1) * tq], 0.0) * wi[h]
        return sc

    s_meta = scores(mkidx_ref[...])

    def score_chunk(c, lo, hi):
        sc = scores(kidx_ref[0, c])
        valid = c * kc + lax.broadcasted_iota(jnp.int32, (kc, tq), 0) <= qreal
        s_ref[c] = jnp.where(valid, sc, -jnp.inf)
        groups = (kc // REDUCE_ROWS, REDUCE_ROWS, tq)
        lo = jnp.minimum(lo, jnp.min(jnp.where(valid, sc, jnp.inf).reshape(groups), axis=0))
        hi = jnp.maximum(hi, jnp.max(jnp.where(valid, sc, -jnp.inf).reshape(groups), axis=0))
        return lo, hi

    def score_pair(c2, carry):
        lo, hi = score_chunk(2 * c2, *carry)
        return score_chunk(jnp.minimum(2 * c2 + 1, nch - 1), lo, hi)

    lo, hi = lax.fori_loop(0, (nch + 1) // 2, score_pair,
                           (jnp.full((REDUCE_ROWS, tq), jnp.inf, F32), jnp.full((REDUCE_ROWS, tq), -jnp.inf, F32)))
    lo = jnp.minimum(jnp.min(lo, axis=0, keepdims=True), jnp.min(s_meta, axis=0, keepdims=True))
    hi = jnp.maximum(jnp.max(hi, axis=0, keepdims=True), jnp.max(s_meta, axis=0, keepdims=True))

    def key_reduce(reduce, combine, per_chunk, init):
        def body(c, acc):
            x = per_chunk(s_ref[c]).reshape(kc // REDUCE_ROWS, REDUCE_ROWS, tq)
            return combine(acc, reduce(x, axis=0))
        acc = lax.fori_loop(0, nch, body, jnp.full((REDUCE_ROWS, tq), init, F32))
        return combine(reduce(acc, axis=0, keepdims=True), reduce(per_chunk(s_meta), axis=0, keepdims=True))

    def count(pred):
        return key_reduce(jnp.sum, jnp.add, lambda sc: jnp.where(pred(sc), 1.0, 0.0), 0.0)

    def max_where(pred):
        return key_reduce(jnp.max, jnp.maximum, lambda sc: jnp.where(pred(sc), sc, -jnp.inf), -jnp.inf)

    def bisect(_, carry):
        lo, hi = carry
        mid = 0.5 * lo + 0.5 * hi
        up = count(lambda sc: sc > mid) >= kf
        return jnp.where(up, mid, lo), jnp.where(up, hi, mid)

    lo, hi = lax.fori_loop(0, N_BISECT, bisect, (lo, hi))

    n_valid = (qreal + (N_META + 1)).astype(F32)
    small = n_valid <= kf
    cand = max_where(lambda sc: sc <= hi)
    done = jnp.where(small | (count(lambda sc: sc >= cand) >= kf), 1.0, 0.0)

    def not_finished(state):
        return jnp.min(state[1]) < 0.5

    def step_down(state):
        cand, done = state
        nxt = jnp.where(done > 0.5, cand, max_where(lambda sc: sc < cand))
        fin = count(lambda sc: sc >= nxt) >= kf
        return nxt, jnp.where(fin, 1.0, done)

    cand, _ = lax.while_loop(not_finished, step_down, (cand, done))
    thr = jnp.where(small, -jnp.inf, cand)
    n_gt = count(lambda sc: sc > thr)
    n_eq = count(lambda sc: sc == thr)
    need = jnp.where(small, 0.0, kf - n_gt)
    ranked_ties = jnp.max(jnp.where(n_eq > need, 1.0, 0.0)) > 0.5

    qlat = qlat_ref[0]
    n_groups = tq // ATT_GROUP
    onehot = (lax.broadcasted_iota(jnp.int32, (ATT_GROUP, ATT_GROUP), 0)
              == lax.broadcasted_iota(jnp.int32, (ATT_GROUP, ATT_GROUP), 1)).astype(BF16)
    q_aug = [jnp.concatenate(
        [jnp.concatenate([qlat[g * ATT_GROUP:(g + 1) * ATT_GROUP, h * KV_LATENT:(h + 1) * KV_LATENT], onehot], axis=1)
         for h in range(ATT_HEADS)], axis=0) for g in range(n_groups)]
    hq = ATT_HEADS * ATT_GROUP

    def lower_tri(n):
        return (lax.broadcasted_iota(jnp.int32, (n, n), 1) <= lax.broadcasted_iota(jnp.int32, (n, n), 0)).astype(BF16)

    def attention(ranked):
        def mask_bias(sc, eq_seen):
            if not ranked:
                return jnp.where(sc >= thr, 0.0, NEG_BIG).astype(BF16), eq_seen
            n = sc.shape[0]
            eq = sc == thr
            rank = _dot(lower_tri(n), jnp.where(eq, 1.0, 0.0).astype(BF16)) + eq_seen
            keep = (sc > thr) | (eq & (rank <= need))
            return jnp.where(keep, 0.0, NEG_BIG).astype(BF16), rank[n - 1:n, :]

        def logits(g, kv, bias):
            k_aug = jnp.concatenate([kv, bias[:, g * ATT_GROUP:(g + 1) * ATT_GROUP]], axis=1)
            return _dot_nt(k_aug, q_aug[g])

        def fold_in(g, c_prev, a_prev):
            acc_ref[g] = a_prev * acc_ref[g] + _dot(ckvt_ref[0, c_prev], p_ref[g])

        def attend(c, carry):
            bias, eq_seen = mask_bias(s_ref[c], carry[0])
            out = [eq_seen]
            for g in range(n_groups):
                m, l, a_prev = carry[1 + 3 * g:4 + 3 * g]
                fold_in(g, c - 1, a_prev)
                s = logits(g, ckv_ref[0, c], bias)
                m_new = jnp.maximum(m, jnp.max(s, axis=0, keepdims=True))
                a = jnp.exp2(m - m_new)
                p = jnp.exp2(s - m_new)
                p_ref[g] = p.astype(BF16)
                out += [m_new, a * l + jnp.sum(p, axis=0, keepdims=True), a]
            return tuple(out)

        bias_m, eq_seen = mask_bias(s_meta, jnp.zeros((1, tq), F32))
        bias_0, eq_seen = mask_bias(s_ref[0], eq_seen)
        init = [eq_seen]
        for g in range(n_groups):
            s_m = logits(g, mckv_ref[...], bias_m)
            s_0 = logits(g, ckv_ref[0, 0], bias_0)
            m = jnp.maximum(jnp.max(s_m, axis=0, keepdims=True), jnp.max(s_0, axis=0, keepdims=True))
            p_m = jnp.exp2(s_m - m)
            p_0 = jnp.exp2(s_0 - m)
            acc_ref[g] = _dot(mckvt_ref[...], p_m.astype(BF16))
            p_ref[g] = p_0.astype(BF16)
            init += [m, jnp.sum(p_m, axis=0, keepdims=True) + jnp.sum(p_0, axis=0, keepdims=True),
                     jnp.ones((1, hq), F32)]
        carry = lax.fori_loop(1, nch, attend, tuple(init))
        rows = []
        for g in range(n_groups):
            _, l, a_last = carry[1 + 3 * g:4 + 3 * g]
            fold_in(g, nch - 1, a_last)
            o_t = (acc_ref[g] / l).T
            rows.append(jnp.concatenate([o_t[h * ATT_GROUP:(h + 1) * ATT_GROUP] for h in range(ATT_HEADS)], axis=1))
        return _dot(jnp.concatenate(rows, axis=0).astype(BF16), wuv_ref[...]).astype(BF16)

    y_ref[0] = lax.cond(ranked_ties, lambda: attention(True), lambda: attention(False))


def _dsa_call(qlat, qidx, wrow, ckv_c, ckvt_c, kidx_c, m_ckv, m_ckvt, m_kidx, wuv_bd, topk):
    bn, rows, _ = qlat.shape
    nchunks, kc = ckv_c.shape[1], ckv_c.shape[2]
    nq = rows // Q_TILE

    def q_spec(width):
        return pl.BlockSpec((1, Q_TILE, width), lambda b, i: (b, i, 0))

    def k_spec(a):
        return pl.BlockSpec((1,) + a.shape[1:], lambda b, i: (b, 0, 0, 0))

    return pl.pallas_call(
        functools.partial(_dsa_kernel, topk=topk),
        grid=(bn, nq),
        in_specs=[q_spec(LAT_WIDTH), q_spec(IDX_WIDTH),
                  pl.BlockSpec((1, 1, GATE_ROWS, Q_TILE), lambda b, i: (b, i, 0, 0)),
                  k_spec(ckv_c), k_spec(ckvt_c), k_spec(kidx_c),
                  _resident(m_ckv.shape), _resident(m_ckvt.shape), _resident(m_kidx.shape), _resident(wuv_bd.shape)],
        out_specs=q_spec(ATT_WIDTH),
        out_shape=jax.ShapeDtypeStruct((bn, rows, ATT_WIDTH), BF16),
        scratch_shapes=[pltpu.VMEM((nchunks, kc, Q_TILE), F32),
                        pltpu.VMEM((Q_TILE // ATT_GROUP, KV_LATENT, ATT_HEADS * ATT_GROUP), F32),
                        pltpu.VMEM((Q_TILE // ATT_GROUP, kc, ATT_HEADS * ATT_GROUP), BF16)],
        compiler_params=pltpu.CompilerParams(
            dimension_semantics=("arbitrary", "arbitrary"), vmem_limit_bytes=VMEM_LIMIT_BYTES),
        name="dsa",
    )(qlat, qidx, wrow, ckv_c, ckvt_c, kidx_c, m_ckv, m_ckvt, m_kidx, wuv_bd)


def _split3(x):
    hi = x.astype(BF16)
    r = x - hi.astype(F32)
    mid = r.astype(BF16)
    lo = (r - mid.astype(F32)).astype(BF16)
    return hi, mid, lo


ML_EXT = ML_V_DIM + 16


def _mlstm_chunk(qk, vt, g, gt, state):
    L = qk.shape[0]
    s_idx = lax.broadcasted_iota(jnp.int32, (L, L), 0)
    t_idx = lax.broadcasted_iota(jnp.int32, (L, L), 1)
    causal = s_idx <= t_idx
    b_cols = sum(_dot((t_idx <= s_idx).astype(BF16), part) for part in _split3(g))
    b_rows = sum(_dot(part, causal.astype(BF16)) for part in _split3(gt))
    ones_blk = jnp.where(lax.broadcasted_iota(jnp.int32, (ML_EXT - ML_V_DIM, L), 0) == 0, 1.0, 0.0).astype(BF16)
    kq = ML_HEADS * ML_QK_DIM

    outs, new_state = [], []
    for h in range(ML_HEADS):
        ce, m_prev = state[h]
        c_col = g[:, GATE_I0 + h:GATE_I0 + h + 1] - b_cols[:, GATE_F0 + h:GATE_F0 + h + 1]
        b_row = b_rows[GATE_F0 - GATE_W0 + h:GATE_F0 - GATE_W0 + h + 1, :]
        ig_row = gt[GATE_I0 - GATE_W0 + h:GATE_I0 - GATE_W0 + h + 1, :]
        qh = qk[:, h * ML_QK_DIM:(h + 1) * ML_QK_DIM]
        kh = qk[:, kq + h * ML_QK_DIM:kq + (h + 1) * ML_QK_DIM]
        vt_ext = jnp.concatenate([vt[h * ML_V_DIM:(h + 1) * ML_V_DIM, :], ones_blk], axis=0)

        d_t = jnp.where(causal, c_col + b_row, -jnp.inf)
        inter = b_row + m_prev
        m_t = jnp.maximum(jnp.max(d_t, axis=0, keepdims=True), inter)
        w_inter = jnp.exp(inter - m_t)
        s_t = _dot_nt(kh, qh) * jnp.exp(d_t - m_t)
        r = _dot(vt_ext, s_t.astype(BF16)) + _dot_nt(ce.astype(BF16), qh) * w_inter
        num = r[:ML_V_DIM]
        den = r[ML_V_DIM:ML_V_DIM + 1]
        hh = num / jnp.maximum(jnp.abs(den), jnp.exp(-m_t))
        mu = jnp.mean(hh, axis=0, keepdims=True)
        hc = hh - mu
        var = jnp.mean(hc * hc, axis=0, keepdims=True)
        outs.append((hc * lax.rsqrt(var + LN_EPS)).T)

        b_end = b_row[:, L - 1:L]
        g_row = b_end - b_row + ig_row
        m_new = jnp.maximum(b_end + m_prev, jnp.max(g_row, axis=1, keepdims=True))
        decay = jnp.exp(b_end + m_prev - m_new)
        weighted = (vt_ext.astype(F32) * jnp.exp(g_row - m_new)).astype(BF16)
        new_state.append((decay * ce + _dot(weighted, kh), m_new))
    return outs, new_state


def _mlstm_kernel(qk_ref, vt_ref, og_ref, gates_ref, gt_ref, mqk_ref, mvt_ref, mgates_ref, mgt_ref, ng_ref, y_ref,
                  ce0_ref, m0_ref):
    L = mqk_ref.shape[0]
    n_chunks = qk_ref.shape[1] // L
    norm_g = ng_ref[...]

    @pl.when(pl.program_id(0) == 0)
    def _():
        state = [(jnp.zeros((ML_EXT, ML_QK_DIM), F32), jnp.full((1, 1), M_INIT, F32)) for _ in range(ML_HEADS)]
        _, state = _mlstm_chunk(mqk_ref[...], mvt_ref[...], mgates_ref[...], mgt_ref[...], state)
        for h in range(ML_HEADS):
            ce0_ref[h] = state[h][0]
            m0_ref[h] = jnp.broadcast_to(state[h][1], m0_ref.shape[1:])

    state = [(ce0_ref[h], m0_ref[h][0:1, 0:1]) for h in range(ML_HEADS)]

    def body(c, flat):
        state = [(flat[2 * h], flat[2 * h + 1]) for h in range(ML_HEADS)]
        rows = pl.ds(pl.multiple_of(c * L, L), L)
        outs, state = _mlstm_chunk(qk_ref[0, rows, :], vt_ref[0, c], gates_ref[0, rows, :], gt_ref[0, c], state)
        y = jnp.concatenate(outs, axis=1) * norm_g * og_ref[0, rows, :]
        y_ref[0, rows, :] = y.astype(BF16)
        return tuple(x for pair in state for x in pair)

    lax.fori_loop(0, n_chunks, body, tuple(x for pair in state for x in pair))


def _mlstm_call(qk, vt_c, og, gates, gt_c, mqk, mvt, mgates, mgt, norm_g):
    bn, rows, _ = qk.shape

    def b_spec(a):
        return pl.BlockSpec((1,) + a.shape[1:], lambda b: (b,) + (0,) * (a.ndim - 1))

    consts = (mqk, mvt, mgates, mgt, norm_g)
    return pl.pallas_call(
        _mlstm_kernel,
        grid=(bn,),
        in_specs=[b_spec(qk), b_spec(vt_c), b_spec(og), b_spec(gates), b_spec(gt_c)]
        + [_resident(c.shape) for c in consts],
        out_specs=pl.BlockSpec((1, rows, ML_WIDTH), lambda b: (b, 0, 0)),
        out_shape=jax.ShapeDtypeStruct((bn, rows, ML_WIDTH), BF16),
        scratch_shapes=[pltpu.VMEM((ML_HEADS, ML_EXT, ML_QK_DIM), F32), pltpu.VMEM((ML_HEADS, SUBLANES, LANES), F32)],
        compiler_params=pltpu.CompilerParams(
            dimension_semantics=("arbitrary",), vmem_limit_bytes=VMEM_LIMIT_BYTES),
        name="mlstm",
    )(qk, vt_c, og, gates, gt_c, *consts)


def _out_ffn_ln_kernel(ya_ref, ym_ref, h_ref, wo_ref, g2_ref, b2_ref, wg_ref, wu_ref, wd_ref, g3_ref, b3_ref,
                       o_ref, *, alpha):
    mix = _dot(ya_ref[...], wo_ref[:ATT_WIDTH, :]) + _dot(ym_ref[...], wo_ref[ATT_WIDTH:, :])
    h2 = _layer_norm(alpha * h_ref[...] + mix, g2_ref[...], b2_ref[...])
    o_ref[...] = _ffn_ln(h2, wg_ref, wu_ref, wd_ref, g3_ref[...], b3_ref[...], alpha)


def _out_ffn_ln_call(ya, ym, h, wo, g2, b2, wg, wu, wd, g3, b3, alpha, tm):
    rows, d = h.shape

    def row_spec(width):
        return pl.BlockSpec((tm, width), lambda i: (i, 0))

    consts = (wo, g2, b2, wg, wu, wd, g3, b3)
    return pl.pallas_call(
        functools.partial(_out_ffn_ln_kernel, alpha=alpha),
        grid=(rows // tm,),
        in_specs=[row_spec(ya.shape[1]), row_spec(ym.shape[1]), row_spec(d)] + [_resident(c.shape) for c in consts],
        out_specs=row_spec(d),
        out_shape=jax.ShapeDtypeStruct((rows, d), F32),
        compiler_params=pltpu.CompilerParams(
            dimension_semantics=("arbitrary",), vmem_limit_bytes=VMEM_LIMIT_BYTES),
        name="out_ffn_ln",
    )(ya, ym, h, *consts)


def _block_diag(w):
    nh, a, b = w.shape
    eye = jnp.eye(nh, dtype=w.dtype)
    return (eye[:, None, :, None] * w[:, :, None, :]).reshape(nh * a, nh * b)


def _pad_rows(a, rows, value=0.0):
    return jnp.pad(a, ((0, rows - a.shape[0]), (0, 0)), constant_values=value)


def kernel(x, meta_tokens, ln1_g, ln1_b, ffn1_w_gate, ffn1_w_up, ffn1_w_down, w_in, w_uk, w_uv, kv_norm_g,
           conv_w, b_igate, b_fgate, ml_norm_g, w_out, ln2_g, ln2_b, ffn2_w_gate, ffn2_w_up, ffn2_w_down,
           ln3_g, ln3_b):
    depth = ln1_g.shape[0]
    assert depth == 1, "the meta-token shortcut below is only valid for a single layer"
    bsz, seq, d = x.shape
    assert seq % ROW_TILE == 0 and seq % ML_CHUNK == 0 and seq % Q_TILE == 0
    assert Q_TILE == GT_CHUNK and ML_CHUNK == GT_CHUNK and ROW_TILE % GT_CHUNK == 0
    alpha = (2 * depth) ** 0.25
    topk = min(TOPK_MAX, seq // 4)

    row2 = lambda p: p[0].reshape(1, -1).astype(F32)
    bf = lambda w: w[0].astype(BF16)

    w_t = jnp.swapaxes(w_in[0], 0, 1)
    o_qa, o_ckv, o_qi, o_ki, o_wi = 0, 512, 640, 896, 960
    o_qk, o_v, o_o, o_ig, o_fg, o_end = 964, 1476, 1988, 2500, 2504, 2508
    wa = jnp.concatenate([w_t[o_qa:o_wi], w_t[o_wi:o_qk], w_t[o_ig:o_end],
                          jnp.zeros((LANES - GATE_END, d), F32)], axis=0).astype(BF16)
    wm = w_t[o_qk:o_ig].astype(BF16)
    gbias = jnp.concatenate([jnp.zeros((GATE_I0,), F32), b_igate[0], b_fgate[0],
                             jnp.zeros((LANES - GATE_END,), F32)]).reshape(1, LANES)
    wuk_bd = jnp.stack([_block_diag(w_uk[0][2 * p:2 * p + 2]) for p in range(HEAD_PAIRS)]).astype(BF16)
    wuv_bd = _block_diag(w_uv[0]).astype(BF16)
    kvg = row2(kv_norm_g)
    convw = conv_w[0].astype(F32)
    ffn1 = (bf(ffn1_w_gate), bf(ffn1_w_up), bf(ffn1_w_down), row2(ln1_g), row2(ln1_b))

    meta = meta_tokens.astype(F32)
    h1_meta = _ffn_ln_call(meta, *ffn1, alpha, N_META)
    zero_tail = jnp.zeros((CONV_HIST, MLQK_WIDTH), F32)
    (_, m_ckv, _, m_kidx, m_qk, m_v, _, m_gates, m_tail) = _inproj_call(
        h1_meta[None], zero_tail, wa, wm, wuk_bd, kvg, convw, gbias, N_META)

    h1 = _ffn_ln_call(x.reshape(bsz * seq, d), *ffn1, alpha, FFN_TILE)
    (qlat, ckv, qidx, kidx, qk, v, og, gates, _, gates_t) = _inproj_call(
        h1.reshape(bsz, seq, d), m_tail[0], wa, wm, wuk_bd, kvg, convw, gbias, ROW_TILE)

    nchunks = seq // KEY_CHUNK
    ckv_c = ckv.reshape(bsz, nchunks, KEY_CHUNK, KV_LATENT)
    kidx_c = kidx.reshape(bsz, nchunks, KEY_CHUNK, IDX_DIM)
    y_att = _dsa_call(qlat, qidx, gates_t, ckv_c, jnp.swapaxes(ckv_c, 2, 3), kidx_c,
                      m_ckv[0], m_ckv[0].T, m_kidx[0], wuv_bd, topk)

    lane = jnp.arange(LANES)
    pad_gate = jnp.where((lane >= GATE_I0) & (lane < GATE_F0), NEG_BIG, 0.0).astype(F32)
    mg = jnp.concatenate([m_gates[0], jnp.broadcast_to(pad_gate, (ML_CHUNK - N_META, LANES))], axis=0)
    n_ml = seq // ML_CHUNK
    gate_lanes = slice(GATE_W0, GATE_W0 + GATE_ROWS)

    def chunk_t(a):
        return jnp.swapaxes(a.reshape(bsz, n_ml, ML_CHUNK, a.shape[2]), 2, 3)

    y_ml = _mlstm_call(qk, chunk_t(v), og, gates, gates_t,
                       _pad_rows(m_qk[0], ML_CHUNK), _pad_rows(m_v[0], ML_CHUNK).T, mg, mg[:, gate_lanes].T,
                       row2(ml_norm_g))

    out = _out_ffn_ln_call(
        y_att.reshape(bsz * seq, ATT_WIDTH), y_ml.reshape(bsz * seq, ML_WIDTH), h1, bf(w_out),
        row2(ln2_g), row2(ln2_b), bf(ffn2_w_gate), bf(ffn2_w_up), bf(ffn2_w_down), row2(ln3_g), row2(ln3_b),
        alpha, FFN_TILE)
    return out.reshape(bsz, seq, d)
```

```python
import functools

import jax
import jax.numpy as jnp
from jax import lax
from jax.experimental import pallas as pl
from jax.experimental.pallas import tpu as pltpu

F32 = jnp.float32
BF16 = jnp.bfloat16

N_META = 16
ATT_HEADS = 8
ATT_HEAD_DIM = 64
KV_LATENT = 128
IDX_HEADS = 4
IDX_DIM = 64
TOPK_MAX = 256
ML_HEADS = 4
ML_V_DIM = 128
ML_QK_DIM = 64
CONV_WIDTH = 4
GATE_SOFTCAP = 15.0
M_INIT = -1e30
LN_EPS = 1e-5
NEG_BIG = -1e30
LOG2_E = 1.4426950408889634

LANES = 128
SUBLANES = 8
VMEM_LIMIT_BYTES = 56 * 1024 * 1024

FF_CHUNK = 256
ROW_TILE = 512
FFN_TILE = 512
Q_TILE = 256
ATT_GROUP = 128
KEY_CHUNK = 256
N_BISECT = 16
REDUCE_ROWS = 32
ML_CHUNK = 256


def _dot(a, b):
    return jnp.dot(a, b, preferred_element_type=F32)


def _dot_nt(a, b):
    return lax.dot_general(a, b, (((1,), (1,)), ((), ())), preferred_element_type=F32)


def _layer_norm(z, g, b):
    mu = jnp.mean(z, axis=-1, keepdims=True)
    zc = z - mu
    var = jnp.mean(zc * zc, axis=-1, keepdims=True)
    return zc * lax.rsqrt(var + LN_EPS) * g + b


def _sigmoid(x):
    return 1.0 / (1.0 + jnp.exp(-x))


def _swiglu(xb, wg_ref, wu_ref, wd_ref):
    d_ff = wg_ref.shape[1]
    acc = jnp.zeros((xb.shape[0], wd_ref.shape[1]), F32)
    for c in range(d_ff // FF_CHUNK):
        sl = slice(c * FF_CHUNK, (c + 1) * FF_CHUNK)
        g = _dot(xb, wg_ref[:, sl])
        u = _dot(xb, wu_ref[:, sl])
        a = (g * _sigmoid(g) * u).astype(BF16)
        acc = acc + _dot(a, wd_ref[sl, :])
    return acc


def _ffn_ln(x, wg_ref, wu_ref, wd_ref, g, b, alpha):
    y = _swiglu(x.astype(BF16), wg_ref, wu_ref, wd_ref)
    return _layer_norm(alpha * x + 0.5 * y, g, b)


def _ffn_ln_kernel(x_ref, wg_ref, wu_ref, wd_ref, g_ref, b_ref, o_ref, *, alpha):
    o_ref[...] = _ffn_ln(x_ref[...], wg_ref, wu_ref, wd_ref, g_ref[...], b_ref[...], alpha)


def _resident(shape):
    return pl.BlockSpec(shape, lambda *_: (0,) * len(shape), pipeline_mode=pl.Buffered(1))


def _ffn_ln_call(x, wg, wu, wd, g, b, alpha, tm):
    rows, d = x.shape
    return pl.pallas_call(
        functools.partial(_ffn_ln_kernel, alpha=alpha),
        grid=(rows // tm,),
        in_specs=[
            pl.BlockSpec((tm, d), lambda i: (i, 0)),
            _resident(wg.shape), _resident(wu.shape), _resident(wd.shape),
            _resident(g.shape), _resident(b.shape),
        ],
        out_specs=pl.BlockSpec((tm, d), lambda i: (i, 0)),
        out_shape=jax.ShapeDtypeStruct((rows, d), F32),
        compiler_params=pltpu.CompilerParams(
            dimension_semantics=("arbitrary",), vmem_limit_bytes=VMEM_LIMIT_BYTES),
        name="ffn_ln",
    )(x, wg, wu, wd, g, b)


ATT_WIDTH = ATT_HEADS * ATT_HEAD_DIM
IDX_WIDTH = IDX_HEADS * IDX_DIM
MLQK_WIDTH = 2 * ML_HEADS * ML_QK_DIM
ML_WIDTH = ML_HEADS * ML_V_DIM
LAT_WIDTH = ATT_HEADS * KV_LATENT
CONV_HIST = SUBLANES
GATE_W0 = IDX_DIM
GATE_I0, GATE_F0, GATE_END = GATE_W0 + IDX_HEADS, GATE_W0 + IDX_HEADS + ML_HEADS, GATE_W0 + IDX_HEADS + 2 * ML_HEADS
GATE_ROWS = 2 * SUBLANES
GT_CHUNK = 256
HEAD_PAIRS = ATT_HEADS // 2


def _inproj_kernel(h_ref, tail_ref, wa_ref, wm_ref, wuk_ref, kvg_ref, convw_ref, gbias_ref,
                   qlat_ref, ckv_ref, qidx_ref, kidx_ref, qk_ref, v_ref, og_ref, gates_ref, tailout_ref, *rest):
    carry_ref = rest[-1]
    tm = h_ref.shape[1]

    @pl.when(pl.program_id(1) == 0)
    def _():
        carry_ref[...] = tail_ref[...]

    xb = h_ref[0].astype(BF16)

    pa = _dot_nt(xb, wa_ref[...])
    q_a = pa[:, :ATT_WIDTH].astype(BF16)
    c0 = ATT_WIDTH
    ckv = pa[:, c0:c0 + KV_LATENT]
    c1 = c0 + KV_LATENT
    ckv = ckv * lax.rsqrt(jnp.mean(ckv * ckv, axis=-1, keepdims=True) + LN_EPS) * kvg_ref[...]
    ckv_ref[0] = ckv.astype(BF16)
    qidx_ref[0] = pa[:, c1:c1 + IDX_WIDTH].astype(BF16)
    c2 = c1 + IDX_WIDTH
    kidx_ref[0] = pa[:, c2:c2 + IDX_DIM].astype(BF16)
    pair_in, pair_out = 2 * ATT_HEAD_DIM, 2 * KV_LATENT
    for p in range(HEAD_PAIRS):
        ql = _dot(q_a[:, p * pair_in:(p + 1) * pair_in], wuk_ref[p])
        qlat_ref[0, :, p * pair_out:(p + 1) * pair_out] = (ql * (ATT_HEAD_DIM ** -0.5 * LOG2_E)).astype(BF16)

    pm = _dot_nt(xb, wm_ref[...])
    qk_raw = pm[:, :MLQK_WIDTH]
    v_ref[0] = pm[:, MLQK_WIDTH:MLQK_WIDTH + ML_WIDTH].astype(BF16)
    og_ref[0] = _sigmoid(pm[:, MLQK_WIDTH + ML_WIDTH:])

    ext = jnp.concatenate([carry_ref[...], qk_raw], axis=0)
    cw = convw_ref[...]
    conv = jnp.zeros_like(qk_raw)
    for j in range(CONV_WIDTH):
        s0 = CONV_HIST - (CONV_WIDTH - 1) + j
        conv = conv + ext[s0:s0 + tm] * cw[j:j + 1]
    act = conv * _sigmoid(conv)
    half = MLQK_WIDTH // 2
    qk_ref[0, :, :half] = act[:, :half].astype(BF16)
    qk_ref[0, :, half:] = (act[:, half:] * (ML_QK_DIM ** -0.5)).astype(BF16)
    carry_ref[...] = qk_raw[tm - CONV_HIST:]
    tailout_ref[0] = qk_raw[tm - CONV_HIST:]

    gr = pa[:, c2 + IDX_DIM - GATE_W0:]
    lane = lax.broadcasted_iota(jnp.int32, gr.shape, 1)
    sc = GATE_SOFTCAP * jnp.tanh((gr + gbias_ref[...]) / GATE_SOFTCAP)
    lf = -(jnp.maximum(-sc, 0.0) + jnp.log1p(jnp.exp(-jnp.abs(sc))))
    w_scaled = gr * (IDX_HEADS ** -0.5 * IDX_DIM ** -0.5)
    gates = jnp.where((lane < GATE_W0) | (lane >= GATE_END), 0.0,
                      jnp.where(lane < GATE_I0, w_scaled, jnp.where(lane < GATE_F0, sc, lf)))
    gates_ref[0] = gates
    if len(rest) == 2:
        gt_ref = rest[0]
        gates_t = gates.T[GATE_W0:GATE_W0 + GATE_ROWS]
        for j in range(tm // GT_CHUNK):
            gt_ref[0, j] = gates_t[:, j * GT_CHUNK:(j + 1) * GT_CHUNK]


def _inproj_call(h, tail, wa, wm, wuk_bd, kvg, convw, gbias, tm):
    bn, rows, d = h.shape
    nblk = rows // tm
    emit_gt = tm % GT_CHUNK == 0

    def row_spec(width):
        return pl.BlockSpec((1, tm, width), lambda b, j: (b, j, 0))

    outs = [
        (LAT_WIDTH, BF16), (KV_LATENT, BF16), (IDX_WIDTH, BF16), (IDX_DIM, BF16),
        (MLQK_WIDTH, BF16), (ML_WIDTH, BF16), (ML_WIDTH, F32), (LANES, F32),
    ]
    out_shape = [jax.ShapeDtypeStruct((bn, rows, w), dt) for w, dt in outs]
    out_specs = [row_spec(w) for w, _ in outs]
    out_shape.append(jax.ShapeDtypeStruct((bn, CONV_HIST, MLQK_WIDTH), F32))
    out_specs.append(pl.BlockSpec((1, CONV_HIST, MLQK_WIDTH), lambda b, j: (b, 0, 0)))
    if emit_gt:
        per_tile = tm // GT_CHUNK
        out_shape.append(jax.ShapeDtypeStruct((bn, rows // GT_CHUNK, GATE_ROWS, GT_CHUNK), F32))
        out_specs.append(pl.BlockSpec((1, per_tile, GATE_ROWS, GT_CHUNK), lambda b, j: (b, j, 0, 0)))
    return pl.pallas_call(
        _inproj_kernel,
        grid=(bn, nblk),
        in_specs=[
            row_spec(d),
            _resident(tail.shape), _resident(wa.shape), _resident(wm.shape),
            _resident(wuk_bd.shape), _resident(kvg.shape), _resident(convw.shape), _resident(gbias.shape),
        ],
        out_specs=out_specs,
        out_shape=out_shape,
        scratch_shapes=[pltpu.VMEM((CONV_HIST, MLQK_WIDTH), F32)],
        compiler_params=pltpu.CompilerParams(
            dimension_semantics=("arbitrary", "arbitrary"), vmem_limit_bytes=VMEM_LIMIT_BYTES),
        name="in_proj",
    )(h, tail, wa, wm, wuk_bd, kvg, convw, gbias)


def _dsa_kernel(qlat_ref, qidx_ref, wrow_ref, ckv_ref, ckvt_ref, kidx_ref, mckv_ref, mckvt_ref, mkidx_ref,
                wuv_ref, y_ref, s_ref, acc_ref, p_ref, lg_ref, *, topk):
    _, kc, tq = s_ref.shape
    i = pl.program_id(1)
    nch = ((i + 1) * tq + kc - 1) // kc
    qreal = i * tq + lax.broadcasted_iota(jnp.int32, (1, tq), 1)
    kf = float(topk)

    wrow = wrow_ref[0, 0]
    qidx = qidx_ref[0]
    q_idx_all = jnp.concatenate([qidx[:, h * IDX_DIM:(h + 1) * IDX_DIM] for h in range(IDX_HEADS)], axis=0)
    wi = [wrow[h:h + 1, :] for h in range(IDX_HEADS)]

    def scores(k_rows):
        lg = _dot_nt(k_rows, q_idx_all)
        sc = jnp.zeros((k_rows.shape[0], tq), F32)
        for h in range(IDX_HEADS):
            sc = sc + jnp.maximum(lg[:, h * tq:(h + 1) * tq], 0.0) * wi[h]
        return sc

    s_meta = scores(mkidx_ref[...])

    def score_chunk(c, lo, hi):
        sc = scores(kidx_ref[0, c])
        valid = c * kc + lax.broadcasted_iota(jnp.int32, (kc, tq), 0) <= qreal
        s_ref[c] = jnp.where(valid, sc, -jnp.inf)
        groups = (kc // REDUCE_ROWS, REDUCE_ROWS, tq)
        lo = jnp.minimum(lo, jnp.min(jnp.where(valid, sc, jnp.inf).reshape(groups), axis=0))
        hi = jnp.maximum(hi, jnp.max(jnp.where(valid, sc, -jnp.inf).reshape(groups), axis=0))
        return lo, hi

    def score_pair(c2, carry):
        lo, hi = score_chunk(2 * c2, *carry)
        return score_chunk(jnp.minimum(2 * c2 + 1, nch - 1), lo, hi)

    lo, hi = lax.fori_loop(0, (nch + 1) // 2, score_pair,
                           (jnp.full((REDUCE_ROWS, tq), jnp.inf, F32), jnp.full((REDUCE_ROWS, tq), -jnp.inf, F32)))
    lo = jnp.minimum(jnp.min(lo, axis=0, keepdims=True), jnp.min(s_meta, axis=0, keepdims=True))
    hi = jnp.maximum(jnp.max(hi, axis=0, keepdims=True), jnp.max(s_meta, axis=0, keepdims=True))

    def key_reduce(reduce, combine, per_chunk, init):
        def body(c, acc):
            x = per_chunk(s_ref[c]).reshape(kc // REDUCE_ROWS, REDUCE_ROWS, tq)
            return combine(acc, reduce(x, axis=0))
        acc = lax.fori_loop(0, nch, body, jnp.full((REDUCE_ROWS, tq), init, F32))
        return combine(reduce(acc, axis=0, keepdims=True), reduce(per_chunk(s_meta), axis=0, keepdims=True))

    def count(pred):
        return key_reduce(jnp.sum, jnp.add, lambda sc: jnp.where(pred(sc), 1.0, 0.0), 0.0)

    def max_where(pred):
        return key_reduce(jnp.max, jnp.maximum, lambda sc: jnp.where(pred(sc), sc, -jnp.inf), -jnp.inf)

    def bisect(_, carry):
        lo, hi = carry
        mid = 0.5 * lo + 0.5 * hi
        up = count(lambda sc: sc > mid) >= kf
        return jnp.where(up, mid, lo), jnp.where(up, hi, mid)

    lo, hi = lax.fori_loop(0, N_BISECT, bisect, (lo, hi))

    n_valid = (qreal + (N_META + 1)).astype(F32)
    small = n_valid <= kf
    cand = max_where(lambda sc: sc <= hi)
    done = jnp.where(small | (count(lambda sc: sc >= cand) >= kf), 1.0, 0.0)

    def not_finished(state):
        return jnp.min(state[1]) < 0.5

    def step_down(state):
        cand, done = state
        nxt = jnp.where(done > 0.5, cand, max_where(lambda sc: sc < cand))
        fin = count(lambda sc: sc >= nxt) >= kf
        return nxt, jnp.where(fin, 1.0, done)

    cand, _ = lax.while_loop(not_finished, step_down, (cand, done))
    thr = jnp.where(small, -jnp.inf, cand)
    n_gt = count(lambda sc: sc > thr)
    n_eq = count(lambda sc: sc == thr)
    need = jnp.where(small, 0.0, kf - n_gt)
    ranked_ties = jnp.max(jnp.where(n_eq > need, 1.0, 0.0)) > 0.5

    qlat = qlat_ref[0]
    n_groups = tq // ATT_GROUP
    onehot = (lax.broadcasted_iota(jnp.int32, (ATT_GROUP, ATT_GROUP), 0)
              == lax.broadcasted_iota(jnp.int32, (ATT_GROUP, ATT_GROUP), 1)).astype(BF16)
    q_aug = [jnp.concatenate(
        [jnp.concatenate([qlat[g * ATT_GROUP:(g + 1) * ATT_GROUP, h * KV_LATENT:(h + 1) * KV_LATENT], onehot], axis=1)
         for h in range(ATT_HEADS)], axis=0) for g in range(n_groups)]
    hq = ATT_HEADS * ATT_GROUP

    def lower_tri(n):
        return (lax.broadcasted_iota(jnp.int32, (n, n), 1) <= lax.broadcasted_iota(jnp.int32, (n, n), 0)).astype(BF16)

    def attention(ranked):
        def mask_bias(sc, eq_seen):
            if not ranked:
                return jnp.where(sc >= thr, 0.0, NEG_BIG).astype(BF16), eq_seen
            n = sc.shape[0]
            eq = sc == thr
            rank = _dot(lower_tri(n), jnp.where(eq, 1.0, 0.0).astype(BF16)) + eq_seen
            keep = (sc > thr) | (eq & (rank <= need))
            return jnp.where(keep, 0.0, NEG_BIG).astype(BF16), rank[n - 1:n, :]

        def logits(g, kv, bias):
            k_aug = jnp.concatenate([kv, bias[:, g * ATT_GROUP:(g + 1) * ATT_GROUP]], axis=1)
            return _dot_nt(k_aug, q_aug[g])

        def fold_in(g, c_prev, a_prev):
            acc_ref[g] = a_prev * acc_ref[g] + _dot(ckvt_ref[0, c_prev], p_ref[g])

        def next_logits(c_next, eq_seen):
            c_next = jnp.minimum(c_next, nch - 1)
            bias, eq_seen = mask_bias(s_ref[c_next], eq_seen)
            col_max = []
            for g in range(n_groups):
                s = logits(g, ckv_ref[0, c_next], bias)
                lg_ref[g] = s
                col_max.append(jnp.max(s, axis=0, keepdims=True))
            return eq_seen, col_max

        def attend(c, carry):
            out = []
            for g in range(n_groups):
                m, l, a_prev, s_max = carry[1 + 4 * g:5 + 4 * g]
                fold_in(g, c - 1, a_prev)
                m_new = jnp.maximum(m, s_max)
                a = jnp.exp2(m - m_new)
                p = jnp.exp2(lg_ref[g] - m_new)
                p_ref[g] = p.astype(BF16)
                out.append([m_new, a * l + jnp.sum(p, axis=0, keepdims=True), a])
            eq_seen, col_max = next_logits(c + 1, carry[0])
            return (eq_seen,) + tuple(x for g in range(n_groups) for x in out[g] + [col_max[g]])

        bias_m, eq_seen = mask_bias(s_meta, jnp.zeros((1, tq), F32))
        bias_0, eq_seen = mask_bias(s_ref[0], eq_seen)
        first = []
        for g in range(n_groups):
            s_m = logits(g, mckv_ref[...], bias_m)
            s_0 = logits(g, ckv_ref[0, 0], bias_0)
            m = jnp.maximum(jnp.max(s_m, axis=0, keepdims=True), jnp.max(s_0, axis=0, keepdims=True))
            p_m = jnp.exp2(s_m - m)
            p_0 = jnp.exp2(s_0 - m)
            acc_ref[g] = _dot(mckvt_ref[...], p_m.astype(BF16))
            p_ref[g] = p_0.astype(BF16)
            first.append([m, jnp.sum(p_m, axis=0, keepdims=True) + jnp.sum(p_0, axis=0, keepdims=True),
                          jnp.ones((1, hq), F32)])
        eq_seen, col_max = next_logits(1, eq_seen)
        init = (eq_seen,) + tuple(x for g in range(n_groups) for x in first[g] + [col_max[g]])
        carry = lax.fori_loop(1, nch, attend, init)
        rows = []
        for g in range(n_groups):
            _, l, a_last, _ = carry[1 + 4 * g:5 + 4 * g]
            fold_in(g, nch - 1, a_last)
            o_t = (acc_ref[g] / l).T
            rows.append(jnp.concatenate([o_t[h * ATT_GROUP:(h + 1) * ATT_GROUP] for h in range(ATT_HEADS)], axis=1))
        return _dot(jnp.concatenate(rows, axis=0).astype(BF16), wuv_ref[...]).astype(BF16)

    y_ref[0] = lax.cond(ranked_ties, lambda: attention(True), lambda: attention(False))


def _dsa_call(qlat, qidx, wrow, ckv_c, ckvt_c, kidx_c, m_ckv, m_ckvt, m_kidx, wuv_bd, topk):
    bn, rows, _ = qlat.shape
    nchunks, kc = ckv_c.shape[1], ckv_c.shape[2]
    nq = rows // Q_TILE

    def q_spec(width):
        return pl.BlockSpec((1, Q_TILE, width), lambda b, i: (b, i, 0))

    def k_spec(a):
        return pl.BlockSpec((1,) + a.shape[1:], lambda b, i: (b, 0, 0, 0))

    return pl.pallas_call(
        functools.partial(_dsa_kernel, topk=topk),
        grid=(bn, nq),
        in_specs=[q_spec(LAT_WIDTH), q_spec(IDX_WIDTH),
                  pl.BlockSpec((1, 1, GATE_ROWS, Q_TILE), lambda b, i: (b, i, 0, 0)),
                  k_spec(ckv_c), k_spec(ckvt_c), k_spec(kidx_c),
                  _resident(m_ckv.shape), _resident(m_ckvt.shape), _resident(m_kidx.shape), _resident(wuv_bd.shape)],
        out_specs=q_spec(ATT_WIDTH),
        out_shape=jax.ShapeDtypeStruct((bn, rows, ATT_WIDTH), BF16),
        scratch_shapes=[pltpu.VMEM((nchunks, kc, Q_TILE), F32),
                        pltpu.VMEM((Q_TILE // ATT_GROUP, KV_LATENT, ATT_HEADS * ATT_GROUP), F32),
                        pltpu.VMEM((Q_TILE // ATT_GROUP, kc, ATT_HEADS * ATT_GROUP), BF16),
                        pltpu.VMEM((Q_TILE // ATT_GROUP, kc, ATT_HEADS * ATT_GROUP), F32)],
        compiler_params=pltpu.CompilerParams(
            dimension_semantics=("arbitrary", "arbitrary"), vmem_limit_bytes=VMEM_LIMIT_BYTES),
        name="dsa",
    )(qlat, qidx, wrow, ckv_c, ckvt_c, kidx_c, m_ckv, m_ckvt, m_kidx, wuv_bd)


def _split3(x):
    hi = x.astype(BF16)
    r = x - hi.astype(F32)
    mid = r.astype(BF16)
    lo = (r - mid.astype(F32)).astype(BF16)
    return hi, mid, lo


ML_EXT = ML_V_DIM + 16


def _mlstm_chunk(qk, vt, g, gt, state):
    L = qk.shape[0]
    s_idx = lax.broadcasted_iota(jnp.int32, (L, L), 0)
    t_idx = lax.broadcasted_iota(jnp.int32, (L, L), 1)
    causal = s_idx <= t_idx
    b_cols = sum(_dot((t_idx <= s_idx).astype(BF16), part) for part in _split3(g))
    b_rows = sum(_dot(part, causal.astype(BF16)) for part in _split3(gt))
    ones_blk = jnp.where(lax.broadcasted_iota(jnp.int32, (ML_EXT - ML_V_DIM, L), 0) == 0, 1.0, 0.0).astype(BF16)
    kq = ML_HEADS * ML_QK_DIM

    outs, new_state = [], []
    for h in range(ML_HEADS):
        ce, m_prev = state[h]
        c_col = g[:, GATE_I0 + h:GATE_I0 + h + 1] - b_cols[:, GATE_F0 + h:GATE_F0 + h + 1]
        b_row = b_rows[GATE_F0 - GATE_W0 + h:GATE_F0 - GATE_W0 + h + 1, :]
        ig_row = gt[GATE_I0 - GATE_W0 + h:GATE_I0 - GATE_W0 + h + 1, :]
        qh = qk[:, h * ML_QK_DIM:(h + 1) * ML_QK_DIM]
        kh = qk[:, kq + h * ML_QK_DIM:kq + (h + 1) * ML_QK_DIM]
        vt_ext = jnp.concatenate([vt[h * ML_V_DIM:(h + 1) * ML_V_DIM, :], ones_blk], axis=0)

        d_t = jnp.where(causal, c_col + b_row, -jnp.inf)
        inter = b_row + m_prev
        m_t = jnp.maximum(jnp.max(d_t, axis=0, keepdims=True), inter)
        w_inter = jnp.exp(inter - m_t)
        s_t = _dot_nt(kh, qh) * jnp.exp(d_t - m_t)
        r = _dot(vt_ext, s_t.astype(BF16)) + _dot_nt(ce.astype(BF16), qh) * w_inter
        num = r[:ML_V_DIM]
        den = r[ML_V_DIM:ML_V_DIM + 1]
        hh = num / jnp.maximum(jnp.abs(den), jnp.exp(-m_t))
        mu = jnp.mean(hh, axis=0, keepdims=True)
        hc = hh - mu
        var = jnp.mean(hc * hc, axis=0, keepdims=True)
        outs.append((hc * lax.rsqrt(var + LN_EPS)).T)

        b_end = b_row[:, L - 1:L]
        g_row = b_end - b_row + ig_row
        m_new = jnp.maximum(b_end + m_prev, jnp.max(g_row, axis=1, keepdims=True))
        decay = jnp.exp(b_end + m_prev - m_new)
        weighted = (vt_ext.astype(F32) * jnp.exp(g_row - m_new)).astype(BF16)
        new_state.append((decay * ce + _dot(weighted, kh), m_new))
    return outs, new_state


def _mlstm_kernel(qk_ref, vt_ref, og_ref, gates_ref, gt_ref, mqk_ref, mvt_ref, mgates_ref, mgt_ref, ng_ref, y_ref,
                  ce0_ref, m0_ref):
    L = mqk_ref.shape[0]
    n_chunks = qk_ref.shape[1] // L
    norm_g = ng_ref[...]

    @pl.when(pl.program_id(0) == 0)
    def _():
        state = [(jnp.zeros((ML_EXT, ML_QK_DIM), F32), jnp.full((1, 1), M_INIT, F32)) for _ in range(ML_HEADS)]
        _, state = _mlstm_chunk(mqk_ref[...], mvt_ref[...], mgates_ref[...], mgt_ref[...], state)
        for h in range(ML_HEADS):
            ce0_ref[h] = state[h][0]
            m0_ref[h] = jnp.broadcast_to(state[h][1], m0_ref.shape[1:])

    state = [(ce0_ref[h], m0_ref[h][0:1, 0:1]) for h in range(ML_HEADS)]

    def body(c, flat):
        state = [(flat[2 * h], flat[2 * h + 1]) for h in range(ML_HEADS)]
        rows = pl.ds(pl.multiple_of(c * L, L), L)
        outs, state = _mlstm_chunk(qk_ref[0, rows, :], vt_ref[0, c], gates_ref[0, rows, :], gt_ref[0, c], state)
        y = jnp.concatenate(outs, axis=1) * norm_g * og_ref[0, rows, :]
        y_ref[0, rows, :] = y.astype(BF16)
        return tuple(x for pair in state for x in pair)

    lax.fori_loop(0, n_chunks, body, tuple(x for pair in state for x in pair))


def _mlstm_call(qk, vt_c, og, gates, gt_c, mqk, mvt, mgates, mgt, norm_g):
    bn, rows, _ = qk.shape

    def b_spec(a):
        return pl.BlockSpec((1,) + a.shape[1:], lambda b: (b,) + (0,) * (a.ndim - 1))

    consts = (mqk, mvt, mgates, mgt, norm_g)
    return pl.pallas_call(
        _mlstm_kernel,
        grid=(bn,),
        in_specs=[b_spec(qk), b_spec(vt_c), b_spec(og), b_spec(gates), b_spec(gt_c)]
        + [_resident(c.shape) for c in consts],
        out_specs=pl.BlockSpec((1, rows, ML_WIDTH), lambda b: (b, 0, 0)),
        out_shape=jax.ShapeDtypeStruct((bn, rows, ML_WIDTH), BF16),
        scratch_shapes=[pltpu.VMEM((ML_HEADS, ML_EXT, ML_QK_DIM), F32), pltpu.VMEM((ML_HEADS, SUBLANES, LANES), F32)],
        compiler_params=pltpu.CompilerParams(
            dimension_semantics=("arbitrary",), vmem_limit_bytes=VMEM_LIMIT_BYTES),
        name="mlstm",
    )(qk, vt_c, og, gates, gt_c, *consts)


def _out_ffn_ln_kernel(ya_ref, ym_ref, h_ref, wo_ref, g2_ref, b2_ref, wg_ref, wu_ref, wd_ref, g3_ref, b3_ref,
                       o_ref, *, alpha):
    mix = _dot(ya_ref[...], wo_ref[:ATT_WIDTH, :]) + _dot(ym_ref[...], wo_ref[ATT_WIDTH:, :])
    h2 = _layer_norm(alpha * h_ref[...] + mix, g2_ref[...], b2_ref[...])
    o_ref[...] = _ffn_ln(h2, wg_ref, wu_ref, wd_ref, g3_ref[...], b3_ref[...], alpha)


def _out_ffn_ln_call(ya, ym, h, wo, g2, b2, wg, wu, wd, g3, b3, alpha, tm):
    rows, d = h.shape

    def row_spec(width):
        return pl.BlockSpec((tm, width), lambda i: (i, 0))

    consts = (wo, g2, b2, wg, wu, wd, g3, b3)
    return pl.pallas_call(
        functools.partial(_out_ffn_ln_kernel, alpha=alpha),
        grid=(rows // tm,),
        in_specs=[row_spec(ya.shape[1]), row_spec(ym.shape[1]), row_spec(d)] + [_resident(c.shape) for c in consts],
        out_specs=row_spec(d),
        out_shape=jax.ShapeDtypeStruct((rows, d), F32),
        compiler_params=pltpu.CompilerParams(
            dimension_semantics=("arbitrary",), vmem_limit_bytes=VMEM_LIMIT_BYTES),
        name="out_ffn_ln",
    )(ya, ym, h, *consts)


def _block_diag(w):
    nh, a, b = w.shape
    eye = jnp.eye(nh, dtype=w.dtype)
    return (eye[:, None, :, None] * w[:, :, None, :]).reshape(nh * a, nh * b)


def _pad_rows(a, rows, value=0.0):
    return jnp.pad(a, ((0, rows - a.shape[0]), (0, 0)), constant_values=value)


def kernel(x, meta_tokens, ln1_g, ln1_b, ffn1_w_gate, ffn1_w_up, ffn1_w_down, w_in, w_uk, w_uv, kv_norm_g,
           conv_w, b_igate, b_fgate, ml_norm_g, w_out, ln2_g, ln2_b, ffn2_w_gate, ffn2_w_up, ffn2_w_down,
           ln3_g, ln3_b):
    depth = ln1_g.shape[0]
    assert depth == 1, "the meta-token shortcut below is only valid for a single layer"
    bsz, seq, d = x.shape
    assert seq % ROW_TILE == 0 and seq % ML_CHUNK == 0 and seq % Q_TILE == 0
    assert Q_TILE == GT_CHUNK and ML_CHUNK == GT_CHUNK and ROW_TILE % GT_CHUNK == 0
    alpha = (2 * depth) ** 0.25
    topk = min(TOPK_MAX, seq // 4)

    row2 = lambda p: p[0].reshape(1, -1).astype(F32)
    bf = lambda w: w[0].astype(BF16)

    w_t = jnp.swapaxes(w_in[0], 0, 1)
    o_qa, o_ckv, o_qi, o_ki, o_wi = 0, 512, 640, 896, 960
    o_qk, o_v, o_o, o_ig, o_fg, o_end = 964, 1476, 1988, 2500, 2504, 2508
    wa = jnp.concatenate([w_t[o_qa:o_wi], w_t[o_wi:o_qk], w_t[o_ig:o_end],
                          jnp.zeros((LANES - GATE_END, d), F32)], axis=0).astype(BF16)
    wm = w_t[o_qk:o_ig].astype(BF16)
    gbias = jnp.concatenate([jnp.zeros((GATE_I0,), F32), b_igate[0], b_fgate[0],
                             jnp.zeros((LANES - GATE_END,), F32)]).reshape(1, LANES)
    wuk_bd = jnp.stack([_block_diag(w_uk[0][2 * p:2 * p + 2]) for p in range(HEAD_PAIRS)]).astype(BF16)
    wuv_bd = _block_diag(w_uv[0]).astype(BF16)
    kvg = row2(kv_norm_g)
    convw = conv_w[0].astype(F32)
    ffn1 = (bf(ffn1_w_gate), bf(ffn1_w_up), bf(ffn1_w_down), row2(ln1_g), row2(ln1_b))

    meta = meta_tokens.astype(F32)
    h1_meta = _ffn_ln_call(meta, *ffn1, alpha, N_META)
    zero_tail = jnp.zeros((CONV_HIST, MLQK_WIDTH), F32)
    (_, m_ckv, _, m_kidx, m_qk, m_v, _, m_gates, m_tail) = _inproj_call(
        h1_meta[None], zero_tail, wa, wm, wuk_bd, kvg, convw, gbias, N_META)

    h1 = _ffn_ln_call(x.reshape(bsz * seq, d), *ffn1, alpha, FFN_TILE)
    (qlat, ckv, qidx, kidx, qk, v, og, gates, _, gates_t) = _inproj_call(
        h1.reshape(bsz, seq, d), m_tail[0], wa, wm, wuk_bd, kvg, convw, gbias, ROW_TILE)

    nchunks = seq // KEY_CHUNK
    ckv_c = ckv.reshape(bsz, nchunks, KEY_CHUNK, KV_LATENT)
    kidx_c = kidx.reshape(bsz, nchunks, KEY_CHUNK, IDX_DIM)
    y_att = _dsa_call(qlat, qidx, gates_t, ckv_c, jnp.swapaxes(ckv_c, 2, 3), kidx_c,
                      m_ckv[0], m_ckv[0].T, m_kidx[0], wuv_bd, topk)

    lane = jnp.arange(LANES)
    pad_gate = jnp.where((lane >= GATE_I0) & (lane < GATE_F0), NEG_BIG, 0.0).astype(F32)
    mg = jnp.concatenate([m_gates[0], jnp.broadcast_to(pad_gate, (ML_CHUNK - N_META, LANES))], axis=0)
    n_ml = seq // ML_CHUNK
    gate_lanes = slice(GATE_W0, GATE_W0 + GATE_ROWS)

    def chunk_t(a):
        return jnp.swapaxes(a.reshape(bsz, n_ml, ML_CHUNK, a.shape[2]), 2, 3)

    y_ml = _mlstm_call(qk, chunk_t(v), og, gates, gates_t,
                       _pad_rows(m_qk[0], ML_CHUNK), _pad_rows(m_v[0], ML_CHUNK).T, mg, mg[:, gate_lanes].T,
                       row2(ml_norm_g))

    out = _out_ffn_ln_call(
        y_att.reshape(bsz * seq, ATT_WIDTH), y_ml.reshape(bsz * seq, ML_WIDTH), h1, bf(w_out),
        row2(ln2_g), row2(ln2_b), bf(ffn2_w_gate), bf(ffn2_w_up), bf(ffn2_w_down), row2(ln3_g), row2(ln3_b),
        alpha, FFN_TILE)
    return out.reshape(bsz, seq, d)
```

```python
import functools

import jax
import jax.numpy as jnp
from jax import lax
from jax.experimental import pallas as pl
from jax.experimental.pallas import tpu as pltpu

F32 = jnp.float32
BF16 = jnp.bfloat16

N_META = 16
ATT_HEADS = 8
ATT_HEAD_DIM = 64
KV_LATENT = 128
IDX_HEADS = 4
IDX_DIM = 64
TOPK_MAX = 256
ML_HEADS = 4
ML_V_DIM = 128
ML_QK_DIM = 64
CONV_WIDTH = 4
GATE_SOFTCAP = 15.0
M_INIT = -1e30
LN_EPS = 1e-5
NEG_BIG = -1e30
LOG2_E = 1.4426950408889634

LANES = 128
SUBLANES = 8
VMEM_LIMIT_BYTES = 56 * 1024 * 1024

FF_CHUNK = 256
ROW_TILE = 512
FFN_TILE = 512
Q_TILE = 256
ATT_GROUP = 128
KEY_CHUNK = 256
N_BISECT = 16
REDUCE_ROWS = 32
ML_CHUNK = 256


def _dot(a, b):
    return jnp.dot(a, b, preferred_element_type=F32)


def _dot_nt(a, b):
    return lax.dot_general(a, b, (((1,), (1,)), ((), ())), preferred_element_type=F32)


def _layer_norm(z, g, b):
    mu = jnp.mean(z, axis=-1, keepdims=True)
    zc = z - mu
    var = jnp.mean(zc * zc, axis=-1, keepdims=True)
    return zc * lax.rsqrt(var + LN_EPS) * g + b


def _sigmoid(x):
    return 1.0 / (1.0 + jnp.exp(-x))


def _swiglu(xb, wg_ref, wu_ref, wd_ref):
    d_ff = wg_ref.shape[1]
    acc = jnp.zeros((xb.shape[0], wd_ref.shape[1]), F32)
    for c in range(d_ff // FF_CHUNK):
        sl = slice(c * FF_CHUNK, (c + 1) * FF_CHUNK)
        g = _dot(xb, wg_ref[:, sl])
        u = _dot(xb, wu_ref[:, sl])
        a = (g * _sigmoid(g) * u).astype(BF16)
        acc = acc + _dot(a, wd_ref[sl, :])
    return acc


def _ffn_ln(x, wg_ref, wu_ref, wd_ref, g, b, alpha):
    y = _swiglu(x.astype(BF16), wg_ref, wu_ref, wd_ref)
    return _layer_norm(alpha * x + 0.5 * y, g, b)


def _ffn_ln_kernel(x_ref, wg_ref, wu_ref, wd_ref, g_ref, b_ref, o_ref, *, alpha):
    o_ref[...] = _ffn_ln(x_ref[...], wg_ref, wu_ref, wd_ref, g_ref[...], b_ref[...], alpha)


def _resident(shape):
    return pl.BlockSpec(shape, lambda *_: (0,) * len(shape), pipeline_mode=pl.Buffered(1))


def _ffn_ln_call(x, wg, wu, wd, g, b, alpha, tm):
    rows, d = x.shape
    return pl.pallas_call(
        functools.partial(_ffn_ln_kernel, alpha=alpha),
        grid=(rows // tm,),
        in_specs=[
            pl.BlockSpec((tm, d), lambda i: (i, 0)),
            _resident(wg.shape), _resident(wu.shape), _resident(wd.shape),
            _resident(g.shape), _resident(b.shape),
        ],
        out_specs=pl.BlockSpec((tm, d), lambda i: (i, 0)),
        out_shape=jax.ShapeDtypeStruct((rows, d), F32),
        compiler_params=pltpu.CompilerParams(
            dimension_semantics=("arbitrary",), vmem_limit_bytes=VMEM_LIMIT_BYTES),
        name="ffn_ln",
    )(x, wg, wu, wd, g, b)


ATT_WIDTH = ATT_HEADS * ATT_HEAD_DIM
IDX_WIDTH = IDX_HEADS * IDX_DIM
MLQK_WIDTH = 2 * ML_HEADS * ML_QK_DIM
ML_WIDTH = ML_HEADS * ML_V_DIM
LAT_WIDTH = ATT_HEADS * KV_LATENT
CONV_HIST = SUBLANES
GATE_W0 = IDX_DIM
GATE_I0, GATE_F0, GATE_END = GATE_W0 + IDX_HEADS, GATE_W0 + IDX_HEADS + ML_HEADS, GATE_W0 + IDX_HEADS + 2 * ML_HEADS
GATE_ROWS = 2 * SUBLANES
GT_CHUNK = 256
HEAD_PAIRS = ATT_HEADS // 2


def _inproj_kernel(h_ref, tail_ref, wa_ref, wm_ref, wuk_ref, kvg_ref, convw_ref, gbias_ref,
                   qlat_ref, ckv_ref, qidx_ref, kidx_ref, qk_ref, v_ref, og_ref, gates_ref, tailout_ref, *rest):
    carry_ref, wa_s, wm_s = rest[-3:]
    tm = h_ref.shape[1]

    @pl.when((pl.program_id(0) == 0) & (pl.program_id(1) == 0))
    def _():
        wa_s[...] = wa_ref[...].astype(F32).T.astype(BF16)
        wm_s[...] = wm_ref[...].astype(F32).T.astype(BF16)

    @pl.when(pl.program_id(1) == 0)
    def _():
        carry_ref[...] = tail_ref[...]

    xb = h_ref[0].astype(BF16)

    pa = _dot(xb, wa_s[...])
    q_a = pa[:, :ATT_WIDTH].astype(BF16)
    c0 = ATT_WIDTH
    ckv = pa[:, c0:c0 + KV_LATENT]
    c1 = c0 + KV_LATENT
    ckv = ckv * lax.rsqrt(jnp.mean(ckv * ckv, axis=-1, keepdims=True) + LN_EPS) * kvg_ref[...]
    ckv_ref[0] = ckv.astype(BF16)
    qidx_ref[0] = pa[:, c1:c1 + IDX_WIDTH].astype(BF16)
    c2 = c1 + IDX_WIDTH
    kidx_ref[0] = pa[:, c2:c2 + IDX_DIM].astype(BF16)
    pair_in, pair_out = 2 * ATT_HEAD_DIM, 2 * KV_LATENT
    for p in range(HEAD_PAIRS):
        ql = _dot(q_a[:, p * pair_in:(p + 1) * pair_in], wuk_ref[p])
        qlat_ref[0, :, p * pair_out:(p + 1) * pair_out] = (ql * (ATT_HEAD_DIM ** -0.5 * LOG2_E)).astype(BF16)

    pm = _dot(xb, wm_s[...])
    qk_raw = pm[:, :MLQK_WIDTH]
    v_ref[0] = pm[:, MLQK_WIDTH:MLQK_WIDTH + ML_WIDTH].astype(BF16)
    og_ref[0] = _sigmoid(pm[:, MLQK_WIDTH + ML_WIDTH:])

    ext = jnp.concatenate([carry_ref[...], qk_raw], axis=0)
    cw = convw_ref[...]
    conv = jnp.zeros_like(qk_raw)
    for j in range(CONV_WIDTH):
        s0 = CONV_HIST - (CONV_WIDTH - 1) + j
        conv = conv + ext[s0:s0 + tm] * cw[j:j + 1]
    act = conv * _sigmoid(conv)
    half = MLQK_WIDTH // 2
    qk_ref[0, :, :half] = act[:, :half].astype(BF16)
    qk_ref[0, :, half:] = (act[:, half:] * (ML_QK_DIM ** -0.5)).astype(BF16)
    carry_ref[...] = qk_raw[tm - CONV_HIST:]
    tailout_ref[0] = qk_raw[tm - CONV_HIST:]

    gr = pa[:, c2 + IDX_DIM - GATE_W0:]
    lane = lax.broadcasted_iota(jnp.int32, gr.shape, 1)
    sc = GATE_SOFTCAP * jnp.tanh((gr + gbias_ref[...]) / GATE_SOFTCAP)
    lf = -(jnp.maximum(-sc, 0.0) + jnp.log1p(jnp.exp(-jnp.abs(sc))))
    w_scaled = gr * (IDX_HEADS ** -0.5 * IDX_DIM ** -0.5)
    gates = jnp.where((lane < GATE_W0) | (lane >= GATE_END), 0.0,
                      jnp.where(lane < GATE_I0, w_scaled, jnp.where(lane < GATE_F0, sc, lf)))
    gates_ref[0] = gates
    if len(rest) == 6:
        gt_ref, ckvt_ref, vt_ref = rest[:3]
        gates_t = gates.T[GATE_W0:GATE_W0 + GATE_ROWS]
        ckv_t = ckv.T.astype(BF16)
        v_t = pm[:, MLQK_WIDTH:MLQK_WIDTH + ML_WIDTH].T.astype(BF16)
        for j in range(tm // GT_CHUNK):
            piece = slice(j * GT_CHUNK, (j + 1) * GT_CHUNK)
            gt_ref[0, j] = gates_t[:, piece]
            ckvt_ref[0, j] = ckv_t[:, piece]
            vt_ref[0, j] = v_t[:, piece]


def _inproj_call(h, tail, wa, wm, wuk_bd, kvg, convw, gbias, tm):
    bn, rows, d = h.shape
    nblk = rows // tm
    emit_gt = tm % GT_CHUNK == 0

    def row_spec(width):
        return pl.BlockSpec((1, tm, width), lambda b, j: (b, j, 0))

    outs = [
        (LAT_WIDTH, BF16), (KV_LATENT, BF16), (IDX_WIDTH, BF16), (IDX_DIM, BF16),
        (MLQK_WIDTH, BF16), (ML_WIDTH, BF16), (ML_WIDTH, F32), (LANES, F32),
    ]
    out_shape = [jax.ShapeDtypeStruct((bn, rows, w), dt) for w, dt in outs]
    out_specs = [row_spec(w) for w, _ in outs]
    out_shape.append(jax.ShapeDtypeStruct((bn, CONV_HIST, MLQK_WIDTH), F32))
    out_specs.append(pl.BlockSpec((1, CONV_HIST, MLQK_WIDTH), lambda b, j: (b, 0, 0)))
    if emit_gt:
        per_tile = tm // GT_CHUNK
        for height, dt in ((GATE_ROWS, F32), (KV_LATENT, BF16), (ML_WIDTH, BF16)):
            out_shape.append(jax.ShapeDtypeStruct((bn, rows // GT_CHUNK, height, GT_CHUNK), dt))
            out_specs.append(pl.BlockSpec((1, per_tile, height, GT_CHUNK), lambda b, j: (b, j, 0, 0)))
    return pl.pallas_call(
        _inproj_kernel,
        grid=(bn, nblk),
        in_specs=[
            row_spec(d),
            _resident(tail.shape), _resident(wa.shape), _resident(wm.shape),
            _resident(wuk_bd.shape), _resident(kvg.shape), _resident(convw.shape), _resident(gbias.shape),
        ],
        out_specs=out_specs,
        out_shape=out_shape,
        scratch_shapes=[pltpu.VMEM((CONV_HIST, MLQK_WIDTH), F32),
                        pltpu.VMEM(wa.shape[::-1], BF16), pltpu.VMEM(wm.shape[::-1], BF16)],
        compiler_params=pltpu.CompilerParams(
            dimension_semantics=("arbitrary", "arbitrary"), vmem_limit_bytes=VMEM_LIMIT_BYTES),
        name="in_proj",
    )(h, tail, wa, wm, wuk_bd, kvg, convw, gbias)


def _dsa_kernel(qlat_ref, qidx_ref, wrow_ref, ckv_ref, ckvt_ref, kidx_ref, mckv_ref, mckvt_ref, mkidx_ref,
                wuv_ref, y_ref, s_ref, acc_ref, p_ref, *, topk):
    _, kc, tq = s_ref.shape
    i = pl.program_id(1)
    nch = ((i + 1) * tq + kc - 1) // kc
    qreal = i * tq + lax.broadcasted_iota(jnp.int32, (1, tq), 1)
    kf = float(topk)

    wrow = wrow_ref[0, 0]
    qidx = qidx_ref[0]
    q_idx_all = jnp.concatenate([qidx[:, h * IDX_DIM:(h + 1) * IDX_DIM] for h in range(IDX_HEADS)], axis=0)
    wi = [wrow[h:h + 1, :] for h in range(IDX_HEADS)]

    def scores(k_rows):
        lg = _dot_nt(k_rows, q_idx_all)
        sc = jnp.zeros((k_rows.shape[0], tq), F32)
        for h in range(IDX_HEADS):
            sc = sc + jnp.maximum(lg[:, h * tq:(h + 1) * tq], 0.0) * wi[h]
        return sc

    s_meta = scores(mkidx_ref[...])

    def score_chunk(c, lo, hi):
        sc = scores(kidx_ref[0, c])
        valid = c * kc + lax.broadcasted_iota(jnp.int32, (kc, tq), 0) <= qreal
        s_ref[c] = jnp.where(valid, sc, -jnp.inf)
        groups = (kc // REDUCE_ROWS, REDUCE_ROWS, tq)
        lo = jnp.minimum(lo, jnp.min(sc.reshape(groups), axis=0))
        hi = jnp.maximum(hi, jnp.max(sc.reshape(groups), axis=0))
        return lo, hi

    def score_pair(c2, carry):
        lo, hi = score_chunk(2 * c2, *carry)
        return score_chunk(jnp.minimum(2 * c2 + 1, nch - 1), lo, hi)

    lo, hi = lax.fori_loop(0, (nch + 1) // 2, score_pair,
                           (jnp.full((REDUCE_ROWS, tq), jnp.inf, F32), jnp.full((REDUCE_ROWS, tq), -jnp.inf, F32)))
    lo = jnp.minimum(jnp.min(lo, axis=0, keepdims=True), jnp.min(s_meta, axis=0, keepdims=True))
    hi = jnp.maximum(jnp.max(hi, axis=0, keepdims=True), jnp.max(s_meta, axis=0, keepdims=True))

    def key_reduce(reduce, combine, per_chunk, init):
        def body(c, acc):
            x = per_chunk(s_ref[c]).reshape(kc // REDUCE_ROWS, REDUCE_ROWS, tq)
            return combine(acc, reduce(x, axis=0))
        acc = lax.fori_loop(0, nch, body, jnp.full((REDUCE_ROWS, tq), init, F32))
        return combine(reduce(acc, axis=0, keepdims=True), reduce(per_chunk(s_meta), axis=0, keepdims=True))

    def count(pred):
        return key_reduce(jnp.sum, jnp.add, lambda sc: jnp.where(pred(sc), 1.0, 0.0), 0.0)

    def max_where(pred):
        return key_reduce(jnp.max, jnp.maximum, lambda sc: jnp.where(pred(sc), sc, -jnp.inf), -jnp.inf)

    def bisect(_, carry):
        lo, hi = carry
        mid = 0.5 * lo + 0.5 * hi
        up = count(lambda sc: sc > mid) >= kf
        return jnp.where(up, mid, lo), jnp.where(up, hi, mid)

    lo, hi = lax.fori_loop(0, N_BISECT, bisect, (lo, hi))

    n_valid = (qreal + (N_META + 1)).astype(F32)
    small = n_valid <= kf
    cand = max_where(lambda sc: sc <= hi)
    done = jnp.where(small | (count(lambda sc: sc >= cand) >= kf), 1.0, 0.0)

    def not_finished(state):
        return jnp.min(state[1]) < 0.5

    def step_down(state):
        cand, done = state
        nxt = jnp.where(done > 0.5, cand, max_where(lambda sc: sc < cand))
        fin = count(lambda sc: sc >= nxt) >= kf
        return nxt, jnp.where(fin, 1.0, done)

    cand, _ = lax.while_loop(not_finished, step_down, (cand, done))
    thr = jnp.where(small, -jnp.inf, cand)
    n_gt = count(lambda sc: sc > thr)
    n_eq = count(lambda sc: sc == thr)
    need = jnp.where(small, 0.0, kf - n_gt)
    ranked_ties = jnp.max(jnp.where(n_eq > need, 1.0, 0.0)) > 0.5

    qlat = qlat_ref[0]
    n_groups = tq // ATT_GROUP
    onehot = (lax.broadcasted_iota(jnp.int32, (ATT_GROUP, ATT_GROUP), 0)
              == lax.broadcasted_iota(jnp.int32, (ATT_GROUP, ATT_GROUP), 1)).astype(BF16)
    q_aug = [jnp.concatenate(
        [jnp.concatenate([qlat[g * ATT_GROUP:(g + 1) * ATT_GROUP, h * KV_LATENT:(h + 1) * KV_LATENT], onehot], axis=1)
         for h in range(ATT_HEADS)], axis=0) for g in range(n_groups)]
    hq = ATT_HEADS * ATT_GROUP

    def lower_tri(n):
        return (lax.broadcasted_iota(jnp.int32, (n, n), 1) <= lax.broadcasted_iota(jnp.int32, (n, n), 0)).astype(BF16)

    def attention(ranked):
        def mask_bias(sc, eq_seen):
            if not ranked:
                return jnp.where(sc >= thr, 0.0, NEG_BIG).astype(BF16), eq_seen
            n = sc.shape[0]
            eq = sc == thr
            rank = _dot(lower_tri(n), jnp.where(eq, 1.0, 0.0).astype(BF16)) + eq_seen
            keep = (sc > thr) | (eq & (rank <= need))
            return jnp.where(keep, 0.0, NEG_BIG).astype(BF16), rank[n - 1:n, :]

        def logits(g, kv, bias):
            k_aug = jnp.concatenate([kv, bias[:, g * ATT_GROUP:(g + 1) * ATT_GROUP]], axis=1)
            return _dot_nt(k_aug, q_aug[g])

        def fold_in(g, c_prev, a_prev):
            acc_ref[g] = a_prev * acc_ref[g] + _dot(ckvt_ref[0, c_prev], p_ref[g])

        def attend(c, carry):
            bias, eq_seen = mask_bias(s_ref[c], carry[0])
            out = [eq_seen]
            for g in range(n_groups):
                m, l, a_prev = carry[1 + 3 * g:4 + 3 * g]
                fold_in(g, c - 1, a_prev)
                s = logits(g, ckv_ref[0, c], bias)
                m_new = jnp.maximum(m, jnp.max(s, axis=0, keepdims=True))
                a = jnp.exp2(m - m_new)
                p = jnp.exp2(s - m_new)
                p_ref[g] = p.astype(BF16)
                out += [m_new, a * l + jnp.sum(p, axis=0, keepdims=True), a]
            return tuple(out)

        bias_m, eq_seen = mask_bias(s_meta, jnp.zeros((1, tq), F32))
        bias_0, eq_seen = mask_bias(s_ref[0], eq_seen)
        init = [eq_seen]
        for g in range(n_groups):
            s_m = logits(g, mckv_ref[...], bias_m)
            s_0 = logits(g, ckv_ref[0, 0], bias_0)
            m = jnp.maximum(jnp.max(s_m, axis=0, keepdims=True), jnp.max(s_0, axis=0, keepdims=True))
            p_m = jnp.exp2(s_m - m)
            p_0 = jnp.exp2(s_0 - m)
            acc_ref[g] = _dot(mckvt_ref[...], p_m.astype(BF16))
            p_ref[g] = p_0.astype(BF16)
            init += [m, jnp.sum(p_m, axis=0, keepdims=True) + jnp.sum(p_0, axis=0, keepdims=True),
                     jnp.ones((1, hq), F32)]
        carry = lax.fori_loop(1, nch, attend, tuple(init))
        rows = []
        for g in range(n_groups):
            _, l, a_last = carry[1 + 3 * g:4 + 3 * g]
            fold_in(g, nch - 1, a_last)
            o_t = (acc_ref[g] / l).T
            rows.append(jnp.concatenate([o_t[h * ATT_GROUP:(h + 1) * ATT_GROUP] for h in range(ATT_HEADS)], axis=1))
        return _dot(jnp.concatenate(rows, axis=0).astype(BF16), wuv_ref[...]).astype(BF16)

    y_ref[0] = lax.cond(ranked_ties, lambda: attention(True), lambda: attention(False))


def _dsa_call(qlat, qidx, wrow, ckv_c, ckvt_c, kidx_c, m_ckv, m_ckvt, m_kidx, wuv_bd, topk):
    bn, rows, _ = qlat.shape
    nchunks, kc = ckv_c.shape[1], ckv_c.shape[2]
    nq = rows // Q_TILE

    def q_spec(width):
        return pl.BlockSpec((1, Q_TILE, width), lambda b, i: (b, i, 0))

    def k_spec(a):
        return pl.BlockSpec((1,) + a.shape[1:], lambda b, i: (b, 0, 0, 0))

    return pl.pallas_call(
        functools.partial(_dsa_kernel, topk=topk),
        grid=(bn, nq),
        in_specs=[q_spec(LAT_WIDTH), q_spec(IDX_WIDTH),
                  pl.BlockSpec((1, 1, GATE_ROWS, Q_TILE), lambda b, i: (b, i, 0, 0)),
                  k_spec(ckv_c), k_spec(ckvt_c), k_spec(kidx_c),
                  _resident(m_ckv.shape), _resident(m_ckvt.shape), _resident(m_kidx.shape), _resident(wuv_bd.shape)],
        out_specs=q_spec(ATT_WIDTH),
        out_shape=jax.ShapeDtypeStruct((bn, rows, ATT_WIDTH), BF16),
        scratch_shapes=[pltpu.VMEM((nchunks, kc, Q_TILE), F32),
                        pltpu.VMEM((Q_TILE // ATT_GROUP, KV_LATENT, ATT_HEADS * ATT_GROUP), F32),
                        pltpu.VMEM((Q_TILE // ATT_GROUP, kc, ATT_HEADS * ATT_GROUP), BF16)],
        compiler_params=pltpu.CompilerParams(
            dimension_semantics=("arbitrary", "arbitrary"), vmem_limit_bytes=VMEM_LIMIT_BYTES),
        name="dsa",
    )(qlat, qidx, wrow, ckv_c, ckvt_c, kidx_c, m_ckv, m_ckvt, m_kidx, wuv_bd)


def _split3(x):
    hi = x.astype(BF16)
    r = x - hi.astype(F32)
    mid = r.astype(BF16)
    lo = (r - mid.astype(F32)).astype(BF16)
    return hi, mid, lo


ML_EXT = ML_V_DIM + 16


def _mlstm_chunk(qk, vt, g, gt, state):
    L = qk.shape[0]
    s_idx = lax.broadcasted_iota(jnp.int32, (L, L), 0)
    t_idx = lax.broadcasted_iota(jnp.int32, (L, L), 1)
    causal = s_idx <= t_idx
    b_cols = sum(_dot((t_idx <= s_idx).astype(BF16), part) for part in _split3(g))
    b_rows = sum(_dot(part, causal.astype(BF16)) for part in _split3(gt))
    ones_blk = jnp.where(lax.broadcasted_iota(jnp.int32, (ML_EXT - ML_V_DIM, L), 0) == 0, 1.0, 0.0).astype(BF16)
    kq = ML_HEADS * ML_QK_DIM

    outs, new_state = [], []
    for h in range(ML_HEADS):
        ce, m_prev = state[h]
        c_col = g[:, GATE_I0 + h:GATE_I0 + h + 1] - b_cols[:, GATE_F0 + h:GATE_F0 + h + 1]
        b_row = b_rows[GATE_F0 - GATE_W0 + h:GATE_F0 - GATE_W0 + h + 1, :]
        ig_row = gt[GATE_I0 - GATE_W0 + h:GATE_I0 - GATE_W0 + h + 1, :]
        qh = qk[:, h * ML_QK_DIM:(h + 1) * ML_QK_DIM]
        kh = qk[:, kq + h * ML_QK_DIM:kq + (h + 1) * ML_QK_DIM]
        vt_ext = jnp.concatenate([vt[h * ML_V_DIM:(h + 1) * ML_V_DIM, :], ones_blk], axis=0)

        d_t = jnp.where(causal, c_col + b_row, -jnp.inf)
        inter = b_row + m_prev
        m_t = jnp.maximum(jnp.max(d_t, axis=0, keepdims=True), inter)
        w_inter = jnp.exp(inter - m_t)
        s_t = _dot_nt(kh, qh) * jnp.exp(d_t - m_t)
        r = _dot(vt_ext, s_t.astype(BF16)) + _dot_nt(ce.astype(BF16), qh) * w_inter
        num = r[:ML_V_DIM]
        den = r[ML_V_DIM:ML_V_DIM + 1]
        hh = num / jnp.maximum(jnp.abs(den), jnp.exp(-m_t))
        mu = jnp.mean(hh, axis=0, keepdims=True)
        hc = hh - mu
        var = jnp.mean(hc * hc, axis=0, keepdims=True)
        outs.append((hc * lax.rsqrt(var + LN_EPS)).T)

        b_end = b_row[:, L - 1:L]
        g_row = b_end - b_row + ig_row
        m_new = jnp.maximum(b_end + m_prev, jnp.max(g_row, axis=1, keepdims=True))
        decay = jnp.exp(b_end + m_prev - m_new)
        weighted = (vt_ext.astype(F32) * jnp.exp(g_row - m_new)).astype(BF16)
        new_state.append((decay * ce + _dot(weighted, kh), m_new))
    return outs, new_state


def _mlstm_kernel(qk_ref, vt_ref, og_ref, gates_ref, gt_ref, mqk_ref, mvt_ref, mgates_ref, mgt_ref, ng_ref, y_ref,
                  ce0_ref, m0_ref):
    L = mqk_ref.shape[0]
    n_chunks = qk_ref.shape[1] // L
    norm_g = ng_ref[...]

    @pl.when(pl.program_id(0) == 0)
    def _():
        state = [(jnp.zeros((ML_EXT, ML_QK_DIM), F32), jnp.full((1, 1), M_INIT, F32)) for _ in range(ML_HEADS)]
        _, state = _mlstm_chunk(mqk_ref[...], mvt_ref[...], mgates_ref[...], mgt_ref[...], state)
        for h in range(ML_HEADS):
            ce0_ref[h] = state[h][0]
            m0_ref[h] = jnp.broadcast_to(state[h][1], m0_ref.shape[1:])

    state = [(ce0_ref[h], m0_ref[h][0:1, 0:1]) for h in range(ML_HEADS)]

    def body(c, flat):
        state = [(flat[2 * h], flat[2 * h + 1]) for h in range(ML_HEADS)]
        rows = pl.ds(pl.multiple_of(c * L, L), L)
        outs, state = _mlstm_chunk(qk_ref[0, rows, :], vt_ref[0, c], gates_ref[0, rows, :], gt_ref[0, c], state)
        y = jnp.concatenate(outs, axis=1) * norm_g * og_ref[0, rows, :]
        y_ref[0, rows, :] = y.astype(BF16)
        return tuple(x for pair in state for x in pair)

    lax.fori_loop(0, n_chunks, body, tuple(x for pair in state for x in pair))


def _mlstm_call(qk, vt_c, og, gates, gt_c, mqk, mvt, mgates, mgt, norm_g):
    bn, rows, _ = qk.shape

    def b_spec(a):
        return pl.BlockSpec((1,) + a.shape[1:], lambda b: (b,) + (0,) * (a.ndim - 1))

    consts = (mqk, mvt, mgates, mgt, norm_g)
    return pl.pallas_call(
        _mlstm_kernel,
        grid=(bn,),
        in_specs=[b_spec(qk), b_spec(vt_c), b_spec(og), b_spec(gates), b_spec(gt_c)]
        + [_resident(c.shape) for c in consts],
        out_specs=pl.BlockSpec((1, rows, ML_WIDTH), lambda b: (b, 0, 0)),
        out_shape=jax.ShapeDtypeStruct((bn, rows, ML_WIDTH), BF16),
        scratch_shapes=[pltpu.VMEM((ML_HEADS, ML_EXT, ML_QK_DIM), F32), pltpu.VMEM((ML_HEADS, SUBLANES, LANES), F32)],
        compiler_params=pltpu.CompilerParams(
            dimension_semantics=("arbitrary",), vmem_limit_bytes=VMEM_LIMIT_BYTES),
        name="mlstm",
    )(qk, vt_c, og, gates, gt_c, *consts)


def _out_ffn_ln_kernel(ya_ref, ym_ref, h_ref, wo_ref, g2_ref, b2_ref, wg_ref, wu_ref, wd_ref, g3_ref, b3_ref,
                       o_ref, *, alpha):
    mix = _dot(ya_ref[...], wo_ref[:ATT_WIDTH, :]) + _dot(ym_ref[...], wo_ref[ATT_WIDTH:, :])
    h2 = _layer_norm(alpha * h_ref[...] + mix, g2_ref[...], b2_ref[...])
    o_ref[...] = _ffn_ln(h2, wg_ref, wu_ref, wd_ref, g3_ref[...], b3_ref[...], alpha)


def _out_ffn_ln_call(ya, ym, h, wo, g2, b2, wg, wu, wd, g3, b3, alpha, tm):
    rows, d = h.shape

    def row_spec(width):
        return pl.BlockSpec((tm, width), lambda i: (i, 0))

    consts = (wo, g2, b2, wg, wu, wd, g3, b3)
    return pl.pallas_call(
        functools.partial(_out_ffn_ln_kernel, alpha=alpha),
        grid=(rows // tm,),
        in_specs=[row_spec(ya.shape[1]), row_spec(ym.shape[1]), row_spec(d)] + [_resident(c.shape) for c in consts],
        out_specs=row_spec(d),
        out_shape=jax.ShapeDtypeStruct((rows, d), F32),
        compiler_params=pltpu.CompilerParams(
            dimension_semantics=("arbitrary",), vmem_limit_bytes=VMEM_LIMIT_BYTES),
        name="out_ffn_ln",
    )(ya, ym, h, *consts)


def _block_diag(w):
    nh, a, b = w.shape
    eye = jnp.eye(nh, dtype=w.dtype)
    return (eye[:, None, :, None] * w[:, :, None, :]).reshape(nh * a, nh * b)


def _pad_rows(a, rows, value=0.0):
    return jnp.pad(a, ((0, rows - a.shape[0]), (0, 0)), constant_values=value)


def kernel(x, meta_tokens, ln1_g, ln1_b, ffn1_w_gate, ffn1_w_up, ffn1_w_down, w_in, w_uk, w_uv, kv_norm_g,
           conv_w, b_igate, b_fgate, ml_norm_g, w_out, ln2_g, ln2_b, ffn2_w_gate, ffn2_w_up, ffn2_w_down,
           ln3_g, ln3_b):
    depth = ln1_g.shape[0]
    assert depth == 1, "the meta-token shortcut below is only valid for a single layer"
    bsz, seq, d = x.shape
    assert seq % ROW_TILE == 0 and seq % ML_CHUNK == 0 and seq % Q_TILE == 0
    assert Q_TILE == GT_CHUNK and ML_CHUNK == GT_CHUNK and KEY_CHUNK == GT_CHUNK and ROW_TILE % GT_CHUNK == 0
    alpha = (2 * depth) ** 0.25
    topk = min(TOPK_MAX, seq // 4)

    row2 = lambda p: p[0].reshape(1, -1).astype(F32)
    bf = lambda w: w[0].astype(BF16)

    w_t = jnp.swapaxes(w_in[0], 0, 1)
    o_qa, o_ckv, o_qi, o_ki, o_wi = 0, 512, 640, 896, 960
    o_qk, o_v, o_o, o_ig, o_fg, o_end = 964, 1476, 1988, 2500, 2504, 2508
    wa = jnp.concatenate([w_t[o_qa:o_wi], w_t[o_wi:o_qk], w_t[o_ig:o_end],
                          jnp.zeros((LANES - GATE_END, d), F32)], axis=0).astype(BF16)
    wm = w_t[o_qk:o_ig].astype(BF16)
    gbias = jnp.concatenate([jnp.zeros((GATE_I0,), F32), b_igate[0], b_fgate[0],
                             jnp.zeros((LANES - GATE_END,), F32)]).reshape(1, LANES)
    wuk_bd = jnp.stack([_block_diag(w_uk[0][2 * p:2 * p + 2]) for p in range(HEAD_PAIRS)]).astype(BF16)
    wuv_bd = _block_diag(w_uv[0]).astype(BF16)
    kvg = row2(kv_norm_g)
    convw = conv_w[0].astype(F32)
    ffn1 = (bf(ffn1_w_gate), bf(ffn1_w_up), bf(ffn1_w_down), row2(ln1_g), row2(ln1_b))

    meta = meta_tokens.astype(F32)
    h1_meta = _ffn_ln_call(meta, *ffn1, alpha, N_META)
    zero_tail = jnp.zeros((CONV_HIST, MLQK_WIDTH), F32)
    (_, m_ckv, _, m_kidx, m_qk, m_v, _, m_gates, m_tail) = _inproj_call(
        h1_meta[None], zero_tail, wa, wm, wuk_bd, kvg, convw, gbias, N_META)

    h1 = _ffn_ln_call(x.reshape(bsz * seq, d), *ffn1, alpha, FFN_TILE)
    (qlat, ckv, qidx, kidx, qk, _, og, gates, _, gates_t, ckv_t, v_t) = _inproj_call(
        h1.reshape(bsz, seq, d), m_tail[0], wa, wm, wuk_bd, kvg, convw, gbias, ROW_TILE)

    nchunks = seq // KEY_CHUNK
    ckv_c = ckv.reshape(bsz, nchunks, KEY_CHUNK, KV_LATENT)
    kidx_c = kidx.reshape(bsz, nchunks, KEY_CHUNK, IDX_DIM)
    y_att = _dsa_call(qlat, qidx, gates_t, ckv_c, ckv_t, kidx_c,
                      m_ckv[0], m_ckv[0].T, m_kidx[0], wuv_bd, topk)

    lane = jnp.arange(LANES)
    pad_gate = jnp.where((lane >= GATE_I0) & (lane < GATE_F0), NEG_BIG, 0.0).astype(F32)
    mg = jnp.concatenate([m_gates[0], jnp.broadcast_to(pad_gate, (ML_CHUNK - N_META, LANES))], axis=0)
    gate_lanes = slice(GATE_W0, GATE_W0 + GATE_ROWS)
    y_ml = _mlstm_call(qk, v_t, og, gates, gates_t,
                       _pad_rows(m_qk[0], ML_CHUNK), _pad_rows(m_v[0], ML_CHUNK).T, mg, mg[:, gate_lanes].T,
                       row2(ml_norm_g))

    out = _out_ffn_ln_call(
        y_att.reshape(bsz * seq, ATT_WIDTH), y_ml.reshape(bsz * seq, ML_WIDTH), h1, bf(w_out),
        row2(ln2_g), row2(ln2_b), bf(ffn2_w_gate), bf(ffn2_w_up), bf(ffn2_w_down), row2(ln3_g), row2(ln3_b),
        alpha, FFN_TILE)
    return out.reshape(bsz, seq, d)
```

```python
import functools

import jax
import jax.numpy as jnp
from jax import lax
from jax.experimental import pallas as pl
from jax.experimental.pallas import tpu as pltpu

F32 = jnp.float32
BF16 = jnp.bfloat16

N_META = 16
ATT_HEADS = 8
ATT_HEAD_DIM = 64
KV_LATENT = 128
IDX_HEADS = 4
IDX_DIM = 64
TOPK_MAX = 256
ML_HEADS = 4
ML_V_DIM = 128
ML_QK_DIM = 64
CONV_WIDTH = 4
GATE_SOFTCAP = 15.0
M_INIT = -1e30
LN_EPS = 1e-5
NEG_BIG = -1e30
LOG2_E = 1.4426950408889634

LANES = 128
SUBLANES = 8
VMEM_LIMIT_BYTES = 56 * 1024 * 1024

FF_CHUNK = 256
ROW_TILE = 512
FFN_TILE = 512
Q_TILE = 256
ATT_GROUP = 128
KEY_CHUNK = 256
N_BISECT = 16
REDUCE_ROWS = 32
ML_BATCH = 2
ML_CHUNK = 256


def _dot(a, b):
    return jnp.dot(a, b, preferred_element_type=F32)


def _dot_nt(a, b):
    return lax.dot_general(a, b, (((1,), (1,)), ((), ())), preferred_element_type=F32)


def _layer_norm(z, g, b):
    mu = jnp.mean(z, axis=-1, keepdims=True)
    zc = z - mu
    var = jnp.mean(zc * zc, axis=-1, keepdims=True)
    return zc * lax.rsqrt(var + LN_EPS) * g + b


def _sigmoid(x):
    return 1.0 / (1.0 + jnp.exp(-x))


def _swiglu_chunk(xb, wg_c, wu_c, wd_c):
    g = _dot(xb, wg_c)
    u = _dot(xb, wu_c)
    return _dot((g * _sigmoid(g) * u).astype(BF16), wd_c)


def _ffn_ln(x, wg_s, wu_s, wd_s, g, b, alpha):
    xb = x.astype(BF16)
    acc = jnp.zeros(x.shape, F32)
    for c in range(wg_s.shape[0]):
        acc = acc + _swiglu_chunk(xb, wg_s[c], wu_s[c], wd_s[c])
    return _layer_norm(alpha * x + 0.5 * acc, g, b)


def _stage_ffn_weights(step, wg_ref, wu_ref, wd_ref, wg_s, wu_s, wd_s):
    wg_s[step] = wg_ref[...].astype(BF16)
    wu_s[step] = wu_ref[...].astype(BF16)
    wd_s[step] = wd_ref[...].astype(BF16)


def _ffn_weight_specs(d, d_ff):
    n = d_ff // FF_CHUNK
    col = pl.BlockSpec((d, FF_CHUNK), lambda i: (0, jnp.minimum(i, n - 1)))
    row = pl.BlockSpec((FF_CHUNK, d), lambda i: (jnp.minimum(i, n - 1), 0))
    scratch = [pltpu.VMEM((n, d, FF_CHUNK), BF16), pltpu.VMEM((n, d, FF_CHUNK), BF16),
               pltpu.VMEM((n, FF_CHUNK, d), BF16)]
    return n, [col, col, row], scratch


def _ffn_ln_kernel(x_ref, meta_ref, wg_ref, wu_ref, wd_ref, g_ref, b_ref, o_ref, ometa_ref,
                   wg_s, wu_s, wd_s, macc_ref, *, alpha, n_stage):
    i = pl.program_id(0)

    @pl.when(i == 0)
    def _():
        macc_ref[...] = jnp.zeros(macc_ref.shape, F32)

    @pl.when(i < n_stage)
    def _():
        _stage_ffn_weights(i, wg_ref, wu_ref, wd_ref, wg_s, wu_s, wd_s)
        macc_ref[...] += _swiglu_chunk(meta_ref[...].astype(BF16), wg_s[i], wu_s[i], wd_s[i])

    @pl.when(i == n_stage - 1)
    def _():
        ometa_ref[...] = _layer_norm(alpha * meta_ref[...] + 0.5 * macc_ref[...], g_ref[...], b_ref[...])

    @pl.when(i >= n_stage)
    def _():
        o_ref[...] = _ffn_ln(x_ref[...], wg_s, wu_s, wd_s, g_ref[...], b_ref[...], alpha)


def _resident(shape):
    return pl.BlockSpec(shape, lambda *_: (0,) * len(shape), pipeline_mode=pl.Buffered(1))


def _ffn_ln_call(x, meta, wg, wu, wd, g, b, alpha, tm):
    rows, d = x.shape
    n_stage, w_specs, w_scratch = _ffn_weight_specs(d, wg.shape[1])
    row_spec = pl.BlockSpec((tm, d), lambda i: (jnp.maximum(i - n_stage, 0), 0))
    return pl.pallas_call(
        functools.partial(_ffn_ln_kernel, alpha=alpha, n_stage=n_stage),
        grid=(n_stage + rows // tm,),
        in_specs=[row_spec, _resident(meta.shape)] + w_specs + [_resident(g.shape), _resident(b.shape)],
        out_specs=[row_spec, pl.BlockSpec(meta.shape, lambda i: (0, 0))],
        out_shape=[jax.ShapeDtypeStruct((rows, d), F32), jax.ShapeDtypeStruct(meta.shape, F32)],
        scratch_shapes=w_scratch + [pltpu.VMEM(meta.shape, F32)],
        compiler_params=pltpu.CompilerParams(
            dimension_semantics=("arbitrary",), vmem_limit_bytes=VMEM_LIMIT_BYTES),
        name="ffn_ln",
    )(x, meta, wg, wu, wd, g, b)


ATT_WIDTH = ATT_HEADS * ATT_HEAD_DIM
IDX_WIDTH = IDX_HEADS * IDX_DIM
MLQK_WIDTH = 2 * ML_HEADS * ML_QK_DIM
ML_WIDTH = ML_HEADS * ML_V_DIM
LAT_WIDTH = ATT_HEADS * KV_LATENT
CONV_HIST = SUBLANES
GATE_W0 = IDX_DIM
GATE_I0, GATE_F0, GATE_END = GATE_W0 + IDX_HEADS, GATE_W0 + IDX_HEADS + ML_HEADS, GATE_W0 + IDX_HEADS + 2 * ML_HEADS
GATE_ROWS = 2 * SUBLANES
GT_CHUNK = 256
HEAD_PAIRS = ATT_HEADS // 2


def _inproj_kernel(h_ref, tail_ref, wa_ref, wm_ref, wuk_ref, kvg_ref, convw_ref, gbias_ref,
                   qlat_ref, ckv_ref, qidx_ref, kidx_ref, qk_ref, v_ref, og_ref, gates_ref, tailout_ref, *rest):
    carry_ref, wa_s, wm_s = rest[-3:]
    tm = h_ref.shape[1]

    @pl.when((pl.program_id(0) == 0) & (pl.program_id(1) == 0))
    def _():
        wa_s[...] = wa_ref[...].astype(F32).T.astype(BF16)
        wm_s[...] = wm_ref[...].astype(F32).T.astype(BF16)

    @pl.when(pl.program_id(1) == 0)
    def _():
        carry_ref[...] = tail_ref[...]

    xb = h_ref[0].astype(BF16)

    pa = _dot(xb, wa_s[...])
    q_a = pa[:, :ATT_WIDTH].astype(BF16)
    c0 = ATT_WIDTH
    ckv = pa[:, c0:c0 + KV_LATENT]
    c1 = c0 + KV_LATENT
    ckv = ckv * lax.rsqrt(jnp.mean(ckv * ckv, axis=-1, keepdims=True) + LN_EPS) * kvg_ref[...]
    ckv_ref[0] = ckv.astype(BF16)
    qidx_ref[0] = pa[:, c1:c1 + IDX_WIDTH].astype(BF16)
    c2 = c1 + IDX_WIDTH
    kidx_ref[0] = pa[:, c2:c2 + IDX_DIM].astype(BF16)
    pair_in, pair_out = 2 * ATT_HEAD_DIM, 2 * KV_LATENT
    for p in range(HEAD_PAIRS):
        ql = _dot(q_a[:, p * pair_in:(p + 1) * pair_in], wuk_ref[p])
        qlat_ref[0, :, p * pair_out:(p + 1) * pair_out] = (ql * (ATT_HEAD_DIM ** -0.5 * LOG2_E)).astype(BF16)

    pm = _dot(xb, wm_s[...])
    qk_raw = pm[:, :MLQK_WIDTH]
    v_ref[0] = pm[:, MLQK_WIDTH:MLQK_WIDTH + ML_WIDTH].astype(BF16)
    og_ref[0] = _sigmoid(pm[:, MLQK_WIDTH + ML_WIDTH:])

    ext = jnp.concatenate([carry_ref[...], qk_raw], axis=0)
    cw = convw_ref[...]
    conv = jnp.zeros_like(qk_raw)
    for j in range(CONV_WIDTH):
        s0 = CONV_HIST - (CONV_WIDTH - 1) + j
        conv = conv + ext[s0:s0 + tm] * cw[j:j + 1]
    act = conv * _sigmoid(conv)
    half = MLQK_WIDTH // 2
    qk_ref[0, :, :half] = act[:, :half].astype(BF16)
    qk_ref[0, :, half:] = (act[:, half:] * (ML_QK_DIM ** -0.5)).astype(BF16)
    carry_ref[...] = qk_raw[tm - CONV_HIST:]
    tailout_ref[0] = qk_raw[tm - CONV_HIST:]

    gr = pa[:, c2 + IDX_DIM - GATE_W0:]
    lane = lax.broadcasted_iota(jnp.int32, gr.shape, 1)
    sc = GATE_SOFTCAP * jnp.tanh((gr + gbias_ref[...]) / GATE_SOFTCAP)
    lf = -(jnp.maximum(-sc, 0.0) + jnp.log1p(jnp.exp(-jnp.abs(sc))))
    w_scaled = gr * (IDX_HEADS ** -0.5 * IDX_DIM ** -0.5)
    gates = jnp.where((lane < GATE_W0) | (lane >= GATE_END), 0.0,
                      jnp.where(lane < GATE_I0, w_scaled, jnp.where(lane < GATE_F0, sc, lf)))
    gates_ref[0] = gates
    if len(rest) == 6:
        gt_ref, ckvt_ref, vt_ref = rest[:3]
        gates_t = gates.T[GATE_W0:GATE_W0 + GATE_ROWS]
        ckv_t = ckv.T.astype(BF16)
        v_t = pm[:, MLQK_WIDTH:MLQK_WIDTH + ML_WIDTH].T.astype(BF16)
        for j in range(tm // GT_CHUNK):
            piece = slice(j * GT_CHUNK, (j + 1) * GT_CHUNK)
            gt_ref[0, j] = gates_t[:, piece]
            ckvt_ref[0, j] = ckv_t[:, piece]
            vt_ref[0, j] = v_t[:, piece]


def _inproj_call(h, tail, wa, wm, wuk_bd, kvg, convw, gbias, tm):
    bn, rows, d = h.shape
    nblk = rows // tm
    emit_gt = tm % GT_CHUNK == 0

    def row_spec(width):
        return pl.BlockSpec((1, tm, width), lambda b, j: (b, j, 0))

    outs = [
        (LAT_WIDTH, BF16), (KV_LATENT, BF16), (IDX_WIDTH, BF16), (IDX_DIM, BF16),
        (MLQK_WIDTH, BF16), (ML_WIDTH, BF16), (ML_WIDTH, F32), (LANES, F32),
    ]
    out_shape = [jax.ShapeDtypeStruct((bn, rows, w), dt) for w, dt in outs]
    out_specs = [row_spec(w) for w, _ in outs]
    out_shape.append(jax.ShapeDtypeStruct((bn, CONV_HIST, MLQK_WIDTH), F32))
    out_specs.append(pl.BlockSpec((1, CONV_HIST, MLQK_WIDTH), lambda b, j: (b, 0, 0)))
    if emit_gt:
        per_tile = tm // GT_CHUNK
        for height, dt in ((GATE_ROWS, F32), (KV_LATENT, BF16), (ML_WIDTH, BF16)):
            out_shape.append(jax.ShapeDtypeStruct((bn, rows // GT_CHUNK, height, GT_CHUNK), dt))
            out_specs.append(pl.BlockSpec((1, per_tile, height, GT_CHUNK), lambda b, j: (b, j, 0, 0)))
    return pl.pallas_call(
        _inproj_kernel,
        grid=(bn, nblk),
        in_specs=[
            row_spec(d),
            _resident(tail.shape), _resident(wa.shape), _resident(wm.shape),
            _resident(wuk_bd.shape), _resident(kvg.shape), _resident(convw.shape), _resident(gbias.shape),
        ],
        out_specs=out_specs,
        out_shape=out_shape,
        scratch_shapes=[pltpu.VMEM((CONV_HIST, MLQK_WIDTH), F32),
                        pltpu.VMEM(wa.shape[::-1], BF16), pltpu.VMEM(wm.shape[::-1], BF16)],
        compiler_params=pltpu.CompilerParams(
            dimension_semantics=("arbitrary", "arbitrary"), vmem_limit_bytes=VMEM_LIMIT_BYTES),
        name="in_proj",
    )(h, tail, wa, wm, wuk_bd, kvg, convw, gbias)


def _dsa_kernel(qlat_ref, qidx_ref, wrow_ref, ckv_ref, ckvt_ref, kidx_ref, mckv_ref, mckvt_ref, mkidx_ref,
                wuv_ref, y_ref, s_ref, acc_ref, p_ref, *, topk):
    _, kc, tq = s_ref.shape
    i = pl.program_id(1)
    nch = ((i + 1) * tq + kc - 1) // kc
    qreal = i * tq + lax.broadcasted_iota(jnp.int32, (1, tq), 1)
    kf = float(topk)

    wrow = wrow_ref[0, 0]
    qidx = qidx_ref[0]
    q_idx_all = jnp.concatenate([qidx[:, h * IDX_DIM:(h + 1) * IDX_DIM] for h in range(IDX_HEADS)], axis=0)
    wi = [wrow[h:h + 1, :] for h in range(IDX_HEADS)]

    def scores(k_rows):
        lg = _dot_nt(k_rows, q_idx_all)
        sc = jnp.zeros((k_rows.shape[0], tq), F32)
        for h in range(IDX_HEADS):
            sc = sc + jnp.maximum(lg[:, h * tq:(h + 1) * tq], 0.0) * wi[h]
        return sc

    s_meta = scores(mkidx_ref[...])

    def score_chunk(c, lo, hi):
        sc = scores(kidx_ref[0, c])
        valid = c * kc + lax.broadcasted_iota(jnp.int32, (kc, tq), 0) <= qreal
        s_ref[c] = jnp.where(valid, sc, -jnp.inf)
        groups = (kc // REDUCE_ROWS, REDUCE_ROWS, tq)
        lo = jnp.minimum(lo, jnp.min(sc.reshape(groups), axis=0))
        hi = jnp.maximum(hi, jnp.max(sc.reshape(groups), axis=0))
        return lo, hi

    def score_pair(c2, carry):
        lo, hi = score_chunk(2 * c2, *carry)
        return score_chunk(jnp.minimum(2 * c2 + 1, nch - 1), lo, hi)

    lo, hi = lax.fori_loop(0, (nch + 1) // 2, score_pair,
                           (jnp.full((REDUCE_ROWS, tq), jnp.inf, F32), jnp.full((REDUCE_ROWS, tq), -jnp.inf, F32)))
    lo = jnp.minimum(jnp.min(lo, axis=0, keepdims=True), jnp.min(s_meta, axis=0, keepdims=True))
    hi = jnp.maximum(jnp.max(hi, axis=0, keepdims=True), jnp.max(s_meta, axis=0, keepdims=True))

    def key_reduce(reduce, combine, per_chunk, init):
        def body(c, acc):
            x = per_chunk(s_ref[c]).reshape(kc // REDUCE_ROWS, REDUCE_ROWS, tq)
            return combine(acc, reduce(x, axis=0))
        acc = lax.fori_loop(0, nch, body, jnp.full((REDUCE_ROWS, tq), init, F32))
        return combine(reduce(acc, axis=0, keepdims=True), reduce(per_chunk(s_meta), axis=0, keepdims=True))

    def count(pred):
        return key_reduce(jnp.sum, jnp.add, lambda sc: jnp.where(pred(sc), 1.0, 0.0), 0.0)

    def max_where(pred):
        return key_reduce(jnp.max, jnp.maximum, lambda sc: jnp.where(pred(sc), sc, -jnp.inf), -jnp.inf)

    def bisect(_, carry):
        lo, hi = carry
        mid = 0.5 * lo + 0.5 * hi
        up = count(lambda sc: sc > mid) >= kf
        return jnp.where(up, mid, lo), jnp.where(up, hi, mid)

    lo, hi = lax.fori_loop(0, N_BISECT, bisect, (lo, hi))

    n_valid = (qreal + (N_META + 1)).astype(F32)
    small = n_valid <= kf
    cand = max_where(lambda sc: sc <= hi)
    done = jnp.where(small | (count(lambda sc: sc >= cand) >= kf), 1.0, 0.0)

    def not_finished(state):
        return jnp.min(state[1]) < 0.5

    def step_down(state):
        cand, done = state
        nxt = jnp.where(done > 0.5, cand, max_where(lambda sc: sc < cand))
        fin = count(lambda sc: sc >= nxt) >= kf
        return nxt, jnp.where(fin, 1.0, done)

    cand, _ = lax.while_loop(not_finished, step_down, (cand, done))
    thr = jnp.where(small, -jnp.inf, cand)
    n_gt = count(lambda sc: sc > thr)
    n_eq = count(lambda sc: sc == thr)
    need = jnp.where(small, 0.0, kf - n_gt)
    ranked_ties = jnp.max(jnp.where(n_eq > need, 1.0, 0.0)) > 0.5

    qlat = qlat_ref[0]
    n_groups = tq // ATT_GROUP
    onehot = (lax.broadcasted_iota(jnp.int32, (ATT_GROUP, ATT_GROUP), 0)
              == lax.broadcasted_iota(jnp.int32, (ATT_GROUP, ATT_GROUP), 1)).astype(BF16)
    q_aug = [jnp.concatenate(
        [jnp.concatenate([qlat[g * ATT_GROUP:(g + 1) * ATT_GROUP, h * KV_LATENT:(h + 1) * KV_LATENT], onehot], axis=1)
         for h in range(ATT_HEADS)], axis=0) for g in range(n_groups)]
    hq = ATT_HEADS * ATT_GROUP

    def lower_tri(n):
        return (lax.broadcasted_iota(jnp.int32, (n, n), 1) <= lax.broadcasted_iota(jnp.int32, (n, n), 0)).astype(BF16)

    def attention(ranked):
        def mask_bias(sc, eq_seen):
            if not ranked:
                return jnp.where(sc >= thr, 0.0, NEG_BIG).astype(BF16), eq_seen
            n = sc.shape[0]
            eq = sc == thr
            rank = _dot(lower_tri(n), jnp.where(eq, 1.0, 0.0).astype(BF16)) + eq_seen
            keep = (sc > thr) | (eq & (rank <= need))
            return jnp.where(keep, 0.0, NEG_BIG).astype(BF16), rank[n - 1:n, :]

        def logits(g, kv, bias):
            k_aug = jnp.concatenate([kv, bias[:, g * ATT_GROUP:(g + 1) * ATT_GROUP]], axis=1)
            return _dot_nt(k_aug, q_aug[g])

        def fold_in(g, c_prev, a_prev):
            acc_ref[g] = a_prev * acc_ref[g] + _dot(ckvt_ref[0, c_prev], p_ref[g])

        def attend(c, carry):
            bias, eq_seen = mask_bias(s_ref[c], carry[0])
            out = [eq_seen]
            for g in range(n_groups):
                m, l, a_prev = carry[1 + 3 * g:4 + 3 * g]
                fold_in(g, c - 1, a_prev)
                s = logits(g, ckv_ref[0, c], bias)
                m_new = jnp.maximum(m, jnp.max(s, axis=0, keepdims=True))
                a = jnp.exp2(m - m_new)
                p = jnp.exp2(s - m_new)
                p_ref[g] = p.astype(BF16)
                out += [m_new, a * l + jnp.sum(p, axis=0, keepdims=True), a]
            return tuple(out)

        bias_m, eq_seen = mask_bias(s_meta, jnp.zeros((1, tq), F32))
        bias_0, eq_seen = mask_bias(s_ref[0], eq_seen)
        init = [eq_seen]
        for g in range(n_groups):
            s_m = logits(g, mckv_ref[...], bias_m)
            s_0 = logits(g, ckv_ref[0, 0], bias_0)
            m = jnp.maximum(jnp.max(s_m, axis=0, keepdims=True), jnp.max(s_0, axis=0, keepdims=True))
            p_m = jnp.exp2(s_m - m)
            p_0 = jnp.exp2(s_0 - m)
            acc_ref[g] = _dot(mckvt_ref[...], p_m.astype(BF16))
            p_ref[g] = p_0.astype(BF16)
            init += [m, jnp.sum(p_m, axis=0, keepdims=True) + jnp.sum(p_0, axis=0, keepdims=True),
                     jnp.ones((1, hq), F32)]
        carry = lax.fori_loop(1, nch, attend, tuple(init))
        rows = []
        for g in range(n_groups):
            _, l, a_last = carry[1 + 3 * g:4 + 3 * g]
            fold_in(g, nch - 1, a_last)
            o_t = (acc_ref[g] / l).T
            rows.append(jnp.concatenate([o_t[h * ATT_GROUP:(h + 1) * ATT_GROUP] for h in range(ATT_HEADS)], axis=1))
        return _dot(jnp.concatenate(rows, axis=0).astype(BF16), wuv_ref[...]).astype(BF16)

    y_ref[0] = lax.cond(ranked_ties, lambda: attention(True), lambda: attention(False))


def _dsa_call(qlat, qidx, wrow, ckv_c, ckvt_c, kidx_c, m_ckv, m_ckvt, m_kidx, wuv_bd, topk):
    bn, rows, _ = qlat.shape
    nchunks, kc = ckv_c.shape[1], ckv_c.shape[2]
    nq = rows // Q_TILE

    def q_spec(width):
        return pl.BlockSpec((1, Q_TILE, width), lambda b, i: (b, i, 0))

    def k_spec(a):
        return pl.BlockSpec((1,) + a.shape[1:], lambda b, i: (b, 0, 0, 0))

    return pl.pallas_call(
        functools.partial(_dsa_kernel, topk=topk),
        grid=(bn, nq),
        in_specs=[q_spec(LAT_WIDTH), q_spec(IDX_WIDTH),
                  pl.BlockSpec((1, 1, GATE_ROWS, Q_TILE), lambda b, i: (b, i, 0, 0)),
                  k_spec(ckv_c), k_spec(ckvt_c), k_spec(kidx_c),
                  _resident(m_ckv.shape), _resident(m_ckvt.shape), _resident(m_kidx.shape), _resident(wuv_bd.shape)],
        out_specs=q_spec(ATT_WIDTH),
        out_shape=jax.ShapeDtypeStruct((bn, rows, ATT_WIDTH), BF16),
        scratch_shapes=[pltpu.VMEM((nchunks, kc, Q_TILE), F32),
                        pltpu.VMEM((Q_TILE // ATT_GROUP, KV_LATENT, ATT_HEADS * ATT_GROUP), F32),
                        pltpu.VMEM((Q_TILE // ATT_GROUP, kc, ATT_HEADS * ATT_GROUP), BF16)],
        compiler_params=pltpu.CompilerParams(
            dimension_semantics=("arbitrary", "arbitrary"), vmem_limit_bytes=VMEM_LIMIT_BYTES),
        name="dsa",
    )(qlat, qidx, wrow, ckv_c, ckvt_c, kidx_c, m_ckv, m_ckvt, m_kidx, wuv_bd)


def _split3(x):
    hi = x.astype(BF16)
    r = x - hi.astype(F32)
    mid = r.astype(BF16)
    lo = (r - mid.astype(F32)).astype(BF16)
    return hi, mid, lo


ML_EXT = ML_V_DIM + 16


def _mlstm_chunk(qk, vt, g, gt, state):
    L = qk.shape[0]
    s_idx = lax.broadcasted_iota(jnp.int32, (L, L), 0)
    t_idx = lax.broadcasted_iota(jnp.int32, (L, L), 1)
    causal = s_idx <= t_idx
    b_cols = sum(_dot((t_idx <= s_idx).astype(BF16), part) for part in _split3(g))
    b_rows = sum(_dot(part, causal.astype(BF16)) for part in _split3(gt))
    ones_blk = jnp.where(lax.broadcasted_iota(jnp.int32, (ML_EXT - ML_V_DIM, L), 0) == 0, 1.0, 0.0).astype(BF16)
    kq = ML_HEADS * ML_QK_DIM

    outs, new_state = [], []
    for h in range(ML_HEADS):
        ce, m_prev = state[h]
        c_col = g[:, GATE_I0 + h:GATE_I0 + h + 1] - b_cols[:, GATE_F0 + h:GATE_F0 + h + 1]
        b_row = b_rows[GATE_F0 - GATE_W0 + h:GATE_F0 - GATE_W0 + h + 1, :]
        ig_row = gt[GATE_I0 - GATE_W0 + h:GATE_I0 - GATE_W0 + h + 1, :]
        qh = qk[:, h * ML_QK_DIM:(h + 1) * ML_QK_DIM]
        kh = qk[:, kq + h * ML_QK_DIM:kq + (h + 1) * ML_QK_DIM]
        vt_ext = jnp.concatenate([vt[h * ML_V_DIM:(h + 1) * ML_V_DIM, :], ones_blk], axis=0)

        d_t = jnp.where(causal, c_col + b_row, -jnp.inf)
        inter = b_row + m_prev
        m_t = jnp.maximum(jnp.max(d_t, axis=0, keepdims=True), inter)
        w_inter = jnp.exp(inter - m_t)
        s_t = _dot_nt(kh, qh) * jnp.exp(d_t - m_t)
        r = _dot(vt_ext, s_t.astype(BF16)) + _dot_nt(ce.astype(BF16), qh) * w_inter
        num = r[:ML_V_DIM]
        den = r[ML_V_DIM:ML_V_DIM + 1]
        hh = num / jnp.maximum(jnp.abs(den), jnp.exp(-m_t))
        mu = jnp.mean(hh, axis=0, keepdims=True)
        hc = hh - mu
        var = jnp.mean(hc * hc, axis=0, keepdims=True)
        outs.append((hc * lax.rsqrt(var + LN_EPS)).T)

        b_end = b_row[:, L - 1:L]
        g_row = b_end - b_row + ig_row
        m_new = jnp.maximum(b_end + m_prev, jnp.max(g_row, axis=1, keepdims=True))
        decay = jnp.exp(b_end + m_prev - m_new)
        weighted = (vt_ext.astype(F32) * jnp.exp(g_row - m_new)).astype(BF16)
        new_state.append((decay * ce + _dot(weighted, kh), m_new))
    return outs, new_state


def _mlstm_kernel(qk_ref, vt_ref, og_ref, gates_ref, gt_ref, mqk_ref, mvt_ref, mgates_ref, mgt_ref, ng_ref, y_ref,
                  ce0_ref, m0_ref):
    L = mqk_ref.shape[0]
    n_chunks = qk_ref.shape[1] // L
    norm_g = ng_ref[...]

    @pl.when(pl.program_id(0) == 0)
    def _():
        state = [(jnp.zeros((ML_EXT, ML_QK_DIM), F32), jnp.full((1, 1), M_INIT, F32)) for _ in range(ML_HEADS)]
        _, state = _mlstm_chunk(mqk_ref[...], mvt_ref[...], mgates_ref[...], mgt_ref[...], state)
        for h in range(ML_HEADS):
            ce0_ref[h] = state[h][0]
            m0_ref[h] = jnp.broadcast_to(state[h][1], m0_ref.shape[1:])

    n_b = qk_ref.shape[0]
    state = [(ce0_ref[h], m0_ref[h][0:1, 0:1]) for h in range(ML_HEADS)] * n_b

    def body(c, flat):
        rows = pl.ds(pl.multiple_of(c * L, L), L)
        new_flat = []
        for b in range(n_b):
            state = [(flat[2 * (b * ML_HEADS + h)], flat[2 * (b * ML_HEADS + h) + 1]) for h in range(ML_HEADS)]
            outs, state = _mlstm_chunk(qk_ref[b, rows, :], vt_ref[b, c], gates_ref[b, rows, :], gt_ref[b, c], state)
            y = jnp.concatenate(outs, axis=1) * norm_g * og_ref[b, rows, :]
            y_ref[b, rows, :] = y.astype(BF16)
            new_flat += [x for pair in state for x in pair]
        return tuple(new_flat)

    lax.fori_loop(0, n_chunks, body, tuple(x for pair in state for x in pair))


def _mlstm_call(qk, vt_c, og, gates, gt_c, mqk, mvt, mgates, mgt, norm_g):
    bn, rows, _ = qk.shape
    nb = ML_BATCH if bn % ML_BATCH == 0 else 1

    def b_spec(a):
        return pl.BlockSpec((nb,) + a.shape[1:], lambda b: (b,) + (0,) * (a.ndim - 1))

    consts = (mqk, mvt, mgates, mgt, norm_g)
    return pl.pallas_call(
        _mlstm_kernel,
        grid=(bn // nb,),
        in_specs=[b_spec(qk), b_spec(vt_c), b_spec(og), b_spec(gates), b_spec(gt_c)]
        + [_resident(c.shape) for c in consts],
        out_specs=pl.BlockSpec((nb, rows, ML_WIDTH), lambda b: (b, 0, 0)),
        out_shape=jax.ShapeDtypeStruct((bn, rows, ML_WIDTH), BF16),
        scratch_shapes=[pltpu.VMEM((ML_HEADS, ML_EXT, ML_QK_DIM), F32), pltpu.VMEM((ML_HEADS, SUBLANES, LANES), F32)],
        compiler_params=pltpu.CompilerParams(
            dimension_semantics=("arbitrary",), vmem_limit_bytes=VMEM_LIMIT_BYTES),
        name="mlstm",
    )(qk, vt_c, og, gates, gt_c, *consts)


def _out_ffn_ln_kernel(ya_ref, ym_ref, h_ref, wo_ref, g2_ref, b2_ref, wg_ref, wu_ref, wd_ref, g3_ref, b3_ref,
                       o_ref, wg_s, wu_s, wd_s, *, alpha, n_stage):
    i = pl.program_id(0)

    @pl.when(i < n_stage)
    def _():
        _stage_ffn_weights(i, wg_ref, wu_ref, wd_ref, wg_s, wu_s, wd_s)

    @pl.when(i >= n_stage)
    def _():
        mix = _dot(ya_ref[...], wo_ref[:ATT_WIDTH, :]) + _dot(ym_ref[...], wo_ref[ATT_WIDTH:, :])
        h2 = _layer_norm(alpha * h_ref[...] + mix, g2_ref[...], b2_ref[...])
        o_ref[...] = _ffn_ln(h2, wg_s, wu_s, wd_s, g3_ref[...], b3_ref[...], alpha)


def _out_ffn_ln_call(ya, ym, h, wo, g2, b2, wg, wu, wd, g3, b3, alpha, tm):
    rows, d = h.shape
    n_stage, w_specs, w_scratch = _ffn_weight_specs(d, wg.shape[1])

    def row_spec(width):
        return pl.BlockSpec((tm, width), lambda i: (jnp.maximum(i - n_stage, 0), 0))

    return pl.pallas_call(
        functools.partial(_out_ffn_ln_kernel, alpha=alpha, n_stage=n_stage),
        grid=(n_stage + rows // tm,),
        in_specs=[row_spec(ya.shape[1]), row_spec(ym.shape[1]), row_spec(d),
                  _resident(wo.shape), _resident(g2.shape), _resident(b2.shape)] + w_specs
        + [_resident(g3.shape), _resident(b3.shape)],
        out_specs=row_spec(d),
        out_shape=jax.ShapeDtypeStruct((rows, d), F32),
        scratch_shapes=w_scratch,
        compiler_params=pltpu.CompilerParams(
            dimension_semantics=("arbitrary",), vmem_limit_bytes=VMEM_LIMIT_BYTES),
        name="out_ffn_ln",
    )(ya, ym, h, wo, g2, b2, wg, wu, wd, g3, b3)


def _block_diag(w):
    nh, a, b = w.shape
    eye = jnp.eye(nh, dtype=w.dtype)
    return (eye[:, None, :, None] * w[:, :, None, :]).reshape(nh * a, nh * b)


def _pad_rows(a, rows, value=0.0):
    return jnp.pad(a, ((0, rows - a.shape[0]), (0, 0)), constant_values=value)


def kernel(x, meta_tokens, ln1_g, ln1_b, ffn1_w_gate, ffn1_w_up, ffn1_w_down, w_in, w_uk, w_uv, kv_norm_g,
           conv_w, b_igate, b_fgate, ml_norm_g, w_out, ln2_g, ln2_b, ffn2_w_gate, ffn2_w_up, ffn2_w_down,
           ln3_g, ln3_b):
    depth = ln1_g.shape[0]
    assert depth == 1, "the meta-token shortcut below is only valid for a single layer"
    bsz, seq, d = x.shape
    assert seq % ROW_TILE == 0 and seq % ML_CHUNK == 0 and seq % Q_TILE == 0
    assert Q_TILE == GT_CHUNK and ML_CHUNK == GT_CHUNK and KEY_CHUNK == GT_CHUNK and ROW_TILE % GT_CHUNK == 0
    alpha = (2 * depth) ** 0.25
    topk = min(TOPK_MAX, seq // 4)

    row2 = lambda p: p[0].reshape(1, -1).astype(F32)
    bf = lambda w: w[0].astype(BF16)

    w_t = jnp.swapaxes(w_in[0], 0, 1)
    o_qa, o_ckv, o_qi, o_ki, o_wi = 0, 512, 640, 896, 960
    o_qk, o_v, o_o, o_ig, o_fg, o_end = 964, 1476, 1988, 2500, 2504, 2508
    wa = jnp.concatenate([w_t[o_qa:o_wi], w_t[o_wi:o_qk], w_t[o_ig:o_end],
                          jnp.zeros((LANES - GATE_END, d), F32)], axis=0).astype(BF16)
    wm = w_t[o_qk:o_ig].astype(BF16)
    gbias = jnp.concatenate([jnp.zeros((GATE_I0,), F32), b_igate[0], b_fgate[0],
                             jnp.zeros((LANES - GATE_END,), F32)]).reshape(1, LANES)
    wuk_bd = jnp.stack([_block_diag(w_uk[0][2 * p:2 * p + 2]) for p in range(HEAD_PAIRS)]).astype(BF16)
    wuv_bd = _block_diag(w_uv[0]).astype(BF16)
    kvg = row2(kv_norm_g)
    convw = conv_w[0].astype(F32)

    h1, h1_meta = _ffn_ln_call(x.reshape(bsz * seq, d), meta_tokens.astype(F32), ffn1_w_gate[0], ffn1_w_up[0],
                               ffn1_w_down[0], row2(ln1_g), row2(ln1_b), alpha, FFN_TILE)
    zero_tail = jnp.zeros((CONV_HIST, MLQK_WIDTH), F32)
    (_, m_ckv, _, m_kidx, m_qk, m_v, _, m_gates, m_tail) = _inproj_call(
        h1_meta[None], zero_tail, wa, wm, wuk_bd, kvg, convw, gbias, N_META)

    (qlat, ckv, qidx, kidx, qk, _, og, gates, _, gates_t, ckv_t, v_t) = _inproj_call(
        h1.reshape(bsz, seq, d), m_tail[0], wa, wm, wuk_bd, kvg, convw, gbias, ROW_TILE)

    nchunks = seq // KEY_CHUNK
    ckv_c = ckv.reshape(bsz, nchunks, KEY_CHUNK, KV_LATENT)
    kidx_c = kidx.reshape(bsz, nchunks, KEY_CHUNK, IDX_DIM)
    y_att = _dsa_call(qlat, qidx, gates_t, ckv_c, ckv_t, kidx_c,
                      m_ckv[0], m_ckv[0].T, m_kidx[0], wuv_bd, topk)

    lane = jnp.arange(LANES)
    pad_gate = jnp.where((lane >= GATE_I0) & (lane < GATE_F0), NEG_BIG, 0.0).astype(F32)
    mg = jnp.concatenate([m_gates[0], jnp.broadcast_to(pad_gate, (ML_CHUNK - N_META, LANES))], axis=0)
    gate_lanes = slice(GATE_W0, GATE_W0 + GATE_ROWS)
    y_ml = _mlstm_call(qk, v_t, og, gates, gates_t,
                       _pad_rows(m_qk[0], ML_CHUNK), _pad_rows(m_v[0], ML_CHUNK).T, mg, mg[:, gate_lanes].T,
                       row2(ml_norm_g))

    out = _out_ffn_ln_call(
        y_att.reshape(bsz * seq, ATT_WIDTH), y_ml.reshape(bsz * seq, ML_WIDTH), h1, bf(w_out),
        row2(ln2_g), row2(ln2_b), ffn2_w_gate[0], ffn2_w_up[0], ffn2_w_down[0], row2(ln3_g), row2(ln3_b),
        alpha, FFN_TILE)
    return out.reshape(bsz, seq, d)
```

```python
import functools

import jax
import jax.numpy as jnp
from jax import lax
from jax.experimental import pallas as pl
from jax.experimental.pallas import tpu as pltpu

F32 = jnp.float32
BF16 = jnp.bfloat16

N_META = 16
ATT_HEADS = 8
ATT_HEAD_DIM = 64
KV_LATENT = 128
IDX_HEADS = 4
IDX_DIM = 64
TOPK_MAX = 256
ML_HEADS = 4
ML_V_DIM = 128
ML_QK_DIM = 64
CONV_WIDTH = 4
GATE_SOFTCAP = 15.0
M_INIT = -1e30
LN_EPS = 1e-5
NEG_BIG = -1e30
LOG2_E = 1.4426950408889634

LANES = 128
SUBLANES = 8
VMEM_LIMIT_BYTES = 56 * 1024 * 1024

FF_CHUNK = 256
ROW_TILE = 512
FFN_TILE = 512
Q_TILE = 256
ATT_GROUP = 128
KEY_CHUNK = 256
N_BISECT = 16
REDUCE_ROWS = 32
ML_BATCH = 2
ML_CHUNK = 256


def _dot(a, b):
    return jnp.dot(a, b, preferred_element_type=F32)


def _dot_nt(a, b):
    return lax.dot_general(a, b, (((1,), (1,)), ((), ())), preferred_element_type=F32)


def _layer_norm(z, g, b):
    mu = jnp.mean(z, axis=-1, keepdims=True)
    zc = z - mu
    var = jnp.mean(zc * zc, axis=-1, keepdims=True)
    return zc * lax.rsqrt(var + LN_EPS) * g + b


def _sigmoid(x):
    return 1.0 / (1.0 + jnp.exp(-x))


def _swiglu_chunk(xb, wg_c, wu_c, wd_c):
    g = _dot(xb, wg_c)
    u = _dot(xb, wu_c)
    return _dot((g * _sigmoid(g) * u).astype(BF16), wd_c)


def _ffn_ln(x, wg_s, wu_s, wd_s, g, b, alpha):
    xb = x.astype(BF16)
    acc = jnp.zeros(x.shape, F32)
    for c in range(wg_s.shape[0]):
        acc = acc + _swiglu_chunk(xb, wg_s[c], wu_s[c], wd_s[c])
    return _layer_norm(alpha * x + 0.5 * acc, g, b)


def _stage_ffn_weights(step, wg_ref, wu_ref, wd_ref, wg_s, wu_s, wd_s):
    wg_s[step] = wg_ref[...].astype(BF16)
    wu_s[step] = wu_ref[...].astype(BF16)
    wd_s[step] = wd_ref[...].astype(BF16)


def _ffn_weight_specs(d, d_ff):
    n = d_ff // FF_CHUNK
    col = pl.BlockSpec((d, FF_CHUNK), lambda i: (0, jnp.minimum(i, n - 1)))
    row = pl.BlockSpec((FF_CHUNK, d), lambda i: (jnp.minimum(i, n - 1), 0))
    scratch = [pltpu.VMEM((n, d, FF_CHUNK), BF16), pltpu.VMEM((n, d, FF_CHUNK), BF16),
               pltpu.VMEM((n, FF_CHUNK, d), BF16)]
    return n, [col, col, row], scratch


def _ffn_ln_kernel(x_ref, meta_ref, wg_ref, wu_ref, wd_ref, g_ref, b_ref, o_ref, ometa_ref,
                   wg_s, wu_s, wd_s, macc_ref, *, alpha, n_stage):
    i = pl.program_id(0)

    @pl.when(i == 0)
    def _():
        macc_ref[...] = jnp.zeros(macc_ref.shape, F32)

    @pl.when(i < n_stage)
    def _():
        _stage_ffn_weights(i, wg_ref, wu_ref, wd_ref, wg_s, wu_s, wd_s)
        macc_ref[...] += _swiglu_chunk(meta_ref[...].astype(BF16), wg_s[i], wu_s[i], wd_s[i])

    @pl.when(i == n_stage - 1)
    def _():
        ometa_ref[...] = _layer_norm(alpha * meta_ref[...] + 0.5 * macc_ref[...], g_ref[...], b_ref[...])

    @pl.when(i >= n_stage)
    def _():
        o_ref[...] = _ffn_ln(x_ref[...], wg_s, wu_s, wd_s, g_ref[...], b_ref[...], alpha)


def _resident(shape):
    return pl.BlockSpec(shape, lambda *_: (0,) * len(shape), pipeline_mode=pl.Buffered(1))


def _ffn_ln_call(x, meta, wg, wu, wd, g, b, alpha, tm):
    rows, d = x.shape
    n_stage, w_specs, w_scratch = _ffn_weight_specs(d, wg.shape[1])
    row_spec = pl.BlockSpec((tm, d), lambda i: (jnp.maximum(i - n_stage, 0), 0))
    return pl.pallas_call(
        functools.partial(_ffn_ln_kernel, alpha=alpha, n_stage=n_stage),
        grid=(n_stage + rows // tm,),
        in_specs=[row_spec, _resident(meta.shape)] + w_specs + [_resident(g.shape), _resident(b.shape)],
        out_specs=[row_spec, pl.BlockSpec(meta.shape, lambda i: (0, 0))],
        out_shape=[jax.ShapeDtypeStruct((rows, d), F32), jax.ShapeDtypeStruct(meta.shape, F32)],
        scratch_shapes=w_scratch + [pltpu.VMEM(meta.shape, F32)],
        compiler_params=pltpu.CompilerParams(
            dimension_semantics=("arbitrary",), vmem_limit_bytes=VMEM_LIMIT_BYTES),
        name="ffn_ln",
    )(x, meta, wg, wu, wd, g, b)


ATT_WIDTH = ATT_HEADS * ATT_HEAD_DIM
IDX_WIDTH = IDX_HEADS * IDX_DIM
MLQK_WIDTH = 2 * ML_HEADS * ML_QK_DIM
ML_WIDTH = ML_HEADS * ML_V_DIM
LAT_WIDTH = ATT_HEADS * KV_LATENT
CONV_HIST = SUBLANES
GATE_W0 = IDX_DIM
GATE_I0, GATE_F0, GATE_END = GATE_W0 + IDX_HEADS, GATE_W0 + IDX_HEADS + ML_HEADS, GATE_W0 + IDX_HEADS + 2 * ML_HEADS
GATE_ROWS = 2 * SUBLANES
GT_CHUNK = 256
HEAD_PAIRS = ATT_HEADS // 2


def _inproj_kernel(h_ref, tail_ref, wa_ref, wm_ref, wuk_ref, kvg_ref, convw_ref, gbias_ref,
                   qlat_ref, ckv_ref, qidx_ref, kidx_ref, qk_ref, v_ref, og_ref, gates_ref, tailout_ref, *rest):
    carry_ref, wa_s, wm_s = rest[-3:]
    tm = h_ref.shape[1]

    @pl.when((pl.program_id(0) == 0) & (pl.program_id(1) == 0))
    def _():
        wa_s[...] = wa_ref[...].astype(F32).T.astype(BF16)
        wm_s[...] = wm_ref[...].astype(F32).T.astype(BF16)

    @pl.when(pl.program_id(1) == 0)
    def _():
        carry_ref[...] = tail_ref[...]

    xb = h_ref[0].astype(BF16)

    pa = _dot(xb, wa_s[...])
    q_a = pa[:, :ATT_WIDTH].astype(BF16)
    c0 = ATT_WIDTH
    ckv = pa[:, c0:c0 + KV_LATENT]
    c1 = c0 + KV_LATENT
    ckv = ckv * lax.rsqrt(jnp.mean(ckv * ckv, axis=-1, keepdims=True) + LN_EPS) * kvg_ref[...]
    ckv_ref[0] = ckv.astype(BF16)
    qidx_ref[0] = pa[:, c1:c1 + IDX_WIDTH].astype(BF16)
    c2 = c1 + IDX_WIDTH
    kidx_ref[0] = pa[:, c2:c2 + IDX_DIM].astype(BF16)
    pair_in, pair_out = 2 * ATT_HEAD_DIM, 2 * KV_LATENT
    for p in range(HEAD_PAIRS):
        ql = _dot(q_a[:, p * pair_in:(p + 1) * pair_in], wuk_ref[p])
        qlat_ref[0, :, p * pair_out:(p + 1) * pair_out] = (ql * (ATT_HEAD_DIM ** -0.5 * LOG2_E)).astype(BF16)

    pm = _dot(xb, wm_s[...])
    qk_raw = pm[:, :MLQK_WIDTH]
    v_ref[0] = pm[:, MLQK_WIDTH:MLQK_WIDTH + ML_WIDTH].astype(BF16)
    og_ref[0] = _sigmoid(pm[:, MLQK_WIDTH + ML_WIDTH:])

    ext = jnp.concatenate([carry_ref[...], qk_raw], axis=0)
    cw = convw_ref[...]
    conv = jnp.zeros_like(qk_raw)
    for j in range(CONV_WIDTH):
        s0 = CONV_HIST - (CONV_WIDTH - 1) + j
        conv = conv + ext[s0:s0 + tm] * cw[j:j + 1]
    act = conv * _sigmoid(conv)
    half = MLQK_WIDTH // 2
    qk_ref[0, :, :half] = act[:, :half].astype(BF16)
    qk_ref[0, :, half:] = (act[:, half:] * (ML_QK_DIM ** -0.5)).astype(BF16)
    carry_ref[...] = qk_raw[tm - CONV_HIST:]
    tailout_ref[0] = qk_raw[tm - CONV_HIST:]

    gr = pa[:, c2 + IDX_DIM - GATE_W0:]
    lane = lax.broadcasted_iota(jnp.int32, gr.shape, 1)
    sc = GATE_SOFTCAP * jnp.tanh((gr + gbias_ref[...]) / GATE_SOFTCAP)
    lf = -(jnp.maximum(-sc, 0.0) + jnp.log1p(jnp.exp(-jnp.abs(sc))))
    w_scaled = gr * (IDX_HEADS ** -0.5 * IDX_DIM ** -0.5)
    gates = jnp.where((lane < GATE_W0) | (lane >= GATE_END), 0.0,
                      jnp.where(lane < GATE_I0, w_scaled, jnp.where(lane < GATE_F0, sc, lf)))
    gates_ref[0] = gates
    if len(rest) == 6:
        gt_ref, ckvt_ref, vt_ref = rest[:3]
        gates_t = gates.T[GATE_W0:GATE_W0 + GATE_ROWS]
        ckv_t = ckv.T.astype(BF16)
        v_t = pm[:, MLQK_WIDTH:MLQK_WIDTH + ML_WIDTH].T.astype(BF16)
        for j in range(tm // GT_CHUNK):
            piece = slice(j * GT_CHUNK, (j + 1) * GT_CHUNK)
            gt_ref[0, j] = gates_t[:, piece]
            ckvt_ref[0, j] = ckv_t[:, piece]
            vt_ref[0, j] = v_t[:, piece]


def _inproj_call(h, tail, wa, wm, wuk_bd, kvg, convw, gbias, tm):
    bn, rows, d = h.shape
    nblk = rows // tm
    emit_gt = tm % GT_CHUNK == 0

    def row_spec(width):
        return pl.BlockSpec((1, tm, width), lambda b, j: (b, j, 0))

    outs = [
        (LAT_WIDTH, BF16), (KV_LATENT, BF16), (IDX_WIDTH, BF16), (IDX_DIM, BF16),
        (MLQK_WIDTH, BF16), (ML_WIDTH, BF16), (ML_WIDTH, F32), (LANES, F32),
    ]
    out_shape = [jax.ShapeDtypeStruct((bn, rows, w), dt) for w, dt in outs]
    out_specs = [row_spec(w) for w, _ in outs]
    out_shape.append(jax.ShapeDtypeStruct((bn, CONV_HIST, MLQK_WIDTH), F32))
    out_specs.append(pl.BlockSpec((1, CONV_HIST, MLQK_WIDTH), lambda b, j: (b, 0, 0)))
    if emit_gt:
        per_tile = tm // GT_CHUNK
        for height, dt in ((GATE_ROWS, F32), (KV_LATENT, BF16), (ML_WIDTH, BF16)):
            out_shape.append(jax.ShapeDtypeStruct((bn, rows // GT_CHUNK, height, GT_CHUNK), dt))
            out_specs.append(pl.BlockSpec((1, per_tile, height, GT_CHUNK), lambda b, j: (b, j, 0, 0)))
    return pl.pallas_call(
        _inproj_kernel,
        grid=(bn, nblk),
        in_specs=[
            row_spec(d),
            _resident(tail.shape), _resident(wa.shape), _resident(wm.shape),
            _resident(wuk_bd.shape), _resident(kvg.shape), _resident(convw.shape), _resident(gbias.shape),
        ],
        out_specs=out_specs,
        out_shape=out_shape,
        scratch_shapes=[pltpu.VMEM((CONV_HIST, MLQK_WIDTH), F32),
                        pltpu.VMEM(wa.shape[::-1], BF16), pltpu.VMEM(wm.shape[::-1], BF16)],
        compiler_params=pltpu.CompilerParams(
            dimension_semantics=("arbitrary", "arbitrary"), vmem_limit_bytes=VMEM_LIMIT_BYTES),
        name="in_proj",
    )(h, tail, wa, wm, wuk_bd, kvg, convw, gbias)


def _dsa_kernel(qlat_ref, qidx_ref, wrow_ref, ckv_ref, ckvt_ref, kidx_ref, mckv_ref, mckvt_ref, mkidx_ref,
                wuv_ref, y_ref, s_ref, acc_ref, p_ref, *, topk):
    _, kc, tq = s_ref.shape
    i = pl.program_id(1)
    nch = ((i + 1) * tq + kc - 1) // kc
    qreal = i * tq + lax.broadcasted_iota(jnp.int32, (1, tq), 1)
    kf = float(topk)

    wrow = wrow_ref[0, 0]
    qidx = qidx_ref[0]
    q_idx_all = jnp.concatenate([qidx[:, h * IDX_DIM:(h + 1) * IDX_DIM] for h in range(IDX_HEADS)], axis=0)
    wi = [wrow[h:h + 1, :] for h in range(IDX_HEADS)]

    def scores(k_rows):
        lg = _dot_nt(k_rows, q_idx_all)
        sc = jnp.zeros((k_rows.shape[0], tq), F32)
        for h in range(IDX_HEADS):
            sc = sc + jnp.maximum(lg[:, h * tq:(h + 1) * tq], 0.0) * wi[h]
        return sc

    s_meta = scores(mkidx_ref[...])

    def score_chunk(c, lo, hi):
        sc = scores(kidx_ref[0, c])
        valid = c * kc + lax.broadcasted_iota(jnp.int32, (kc, tq), 0) <= qreal
        s_ref[c] = jnp.where(valid, sc, -jnp.inf)
        groups = (kc // REDUCE_ROWS, REDUCE_ROWS, tq)
        lo = jnp.minimum(lo, jnp.min(sc.reshape(groups), axis=0))
        hi = jnp.maximum(hi, jnp.max(sc.reshape(groups), axis=0))
        return lo, hi

    def score_pair(c2, carry):
        lo, hi = score_chunk(2 * c2, *carry)
        return score_chunk(jnp.minimum(2 * c2 + 1, nch - 1), lo, hi)

    lo, hi = lax.fori_loop(0, (nch + 1) // 2, score_pair,
                           (jnp.full((REDUCE_ROWS, tq), jnp.inf, F32), jnp.full((REDUCE_ROWS, tq), -jnp.inf, F32)))
    lo = jnp.minimum(jnp.min(lo, axis=0, keepdims=True), jnp.min(s_meta, axis=0, keepdims=True))
    hi = jnp.maximum(jnp.max(hi, axis=0, keepdims=True), jnp.max(s_meta, axis=0, keepdims=True))

    def key_reduce(reduce, combine, per_chunk, init):
        def body(c, acc):
            x = per_chunk(s_ref[c]).reshape(kc // REDUCE_ROWS, REDUCE_ROWS, tq)
            return combine(acc, reduce(x, axis=0))
        acc = lax.fori_loop(0, nch, body, jnp.full((REDUCE_ROWS, tq), init, F32))
        return combine(reduce(acc, axis=0, keepdims=True), reduce(per_chunk(s_meta), axis=0, keepdims=True))

    def count(pred):
        return key_reduce(jnp.sum, jnp.add, lambda sc: jnp.where(pred(sc), 1.0, 0.0), 0.0)

    def max_where(pred):
        return key_reduce(jnp.max, jnp.maximum, lambda sc: jnp.where(pred(sc), sc, -jnp.inf), -jnp.inf)

    def bisect(_, carry):
        lo, hi = carry
        mid = 0.5 * lo + 0.5 * hi
        up = count(lambda sc: sc > mid) >= kf
        return jnp.where(up, mid, lo), jnp.where(up, hi, mid)

    lo, hi = lax.fori_loop(0, N_BISECT, bisect, (lo, hi))

    n_valid = (qreal + (N_META + 1)).astype(F32)
    small = n_valid <= kf
    cand = max_where(lambda sc: sc <= hi)
    n_ge = count(lambda sc: sc >= cand)
    done = jnp.where(small | (n_ge >= kf), 1.0, 0.0)

    def not_finished(state):
        return jnp.min(state[1]) < 0.5

    def step_down(state):
        cand, done, _ = state
        nxt = jnp.where(done > 0.5, cand, max_where(lambda sc: sc < cand))
        n_ge = count(lambda sc: sc >= nxt)
        return nxt, jnp.where(n_ge >= kf, 1.0, done), n_ge

    cand, _, n_ge = lax.while_loop(not_finished, step_down, (cand, done, n_ge))
    thr = jnp.where(small, -jnp.inf, cand)
    n_eq = count(lambda sc: sc == thr)
    need = jnp.where(small, 0.0, kf - (n_ge - n_eq))
    ranked_ties = jnp.max(jnp.where(n_eq > need, 1.0, 0.0)) > 0.5

    qlat = qlat_ref[0]
    n_groups = tq // ATT_GROUP
    onehot = (lax.broadcasted_iota(jnp.int32, (ATT_GROUP, ATT_GROUP), 0)
              == lax.broadcasted_iota(jnp.int32, (ATT_GROUP, ATT_GROUP), 1)).astype(BF16)
    q_aug = [jnp.concatenate(
        [jnp.concatenate([qlat[g * ATT_GROUP:(g + 1) * ATT_GROUP, h * KV_LATENT:(h + 1) * KV_LATENT], onehot], axis=1)
         for h in range(ATT_HEADS)], axis=0) for g in range(n_groups)]
    hq = ATT_HEADS * ATT_GROUP

    def lower_tri(n):
        return (lax.broadcasted_iota(jnp.int32, (n, n), 1) <= lax.broadcasted_iota(jnp.int32, (n, n), 0)).astype(BF16)

    def attention(ranked):
        def mask_bias(sc, eq_seen):
            if not ranked:
                return jnp.where(sc >= thr, 0.0, NEG_BIG).astype(BF16), eq_seen
            n = sc.shape[0]
            eq = sc == thr
            rank = _dot(lower_tri(n), jnp.where(eq, 1.0, 0.0).astype(BF16)) + eq_seen
            keep = (sc > thr) | (eq & (rank <= need))
            return jnp.where(keep, 0.0, NEG_BIG).astype(BF16), rank[n - 1:n, :]

        def logits(g, kv, bias):
            k_aug = jnp.concatenate([kv, bias[:, g * ATT_GROUP:(g + 1) * ATT_GROUP]], axis=1)
            return _dot_nt(k_aug, q_aug[g])

        def fold_in(g, c_prev, a_prev):
            acc_ref[g] = a_prev * acc_ref[g] + _dot(ckvt_ref[0, c_prev], p_ref[g])

        def attend(c, carry):
            bias, eq_seen = mask_bias(s_ref[c], carry[0])
            out = [eq_seen]
            for g in range(n_groups):
                m, l, a_prev = carry[1 + 3 * g:4 + 3 * g]
                fold_in(g, c - 1, a_prev)
                s = logits(g, ckv_ref[0, c], bias)
                m_new = jnp.maximum(m, jnp.max(s, axis=0, keepdims=True))
                a = jnp.exp2(m - m_new)
                p = jnp.exp2(s - m_new)
                p_ref[g] = p.astype(BF16)
                out += [m_new, a * l + jnp.sum(p, axis=0, keepdims=True), a]
            return tuple(out)

        bias_m, eq_seen = mask_bias(s_meta, jnp.zeros((1, tq), F32))
        bias_0, eq_seen = mask_bias(s_ref[0], eq_seen)
        init = [eq_seen]
        for g in range(n_groups):
            s_m = logits(g, mckv_ref[...], bias_m)
            s_0 = logits(g, ckv_ref[0, 0], bias_0)
            m = jnp.maximum(jnp.max(s_m, axis=0, keepdims=True), jnp.max(s_0, axis=0, keepdims=True))
            p_m = jnp.exp2(s_m - m)
            p_0 = jnp.exp2(s_0 - m)
            acc_ref[g] = _dot(mckvt_ref[...], p_m.astype(BF16))
            p_ref[g] = p_0.astype(BF16)
            init += [m, jnp.sum(p_m, axis=0, keepdims=True) + jnp.sum(p_0, axis=0, keepdims=True),
                     jnp.ones((1, hq), F32)]
        carry = lax.fori_loop(1, nch, attend, tuple(init))
        rows = []
        for g in range(n_groups):
            _, l, a_last = carry[1 + 3 * g:4 + 3 * g]
            fold_in(g, nch - 1, a_last)
            o_t = (acc_ref[g] / l).T
            rows.append(jnp.concatenate([o_t[h * ATT_GROUP:(h + 1) * ATT_GROUP] for h in range(ATT_HEADS)], axis=1))
        return _dot(jnp.concatenate(rows, axis=0).astype(BF16), wuv_ref[...]).astype(BF16)

    y_ref[0] = lax.cond(ranked_ties, lambda: attention(True), lambda: attention(False))


def _dsa_call(qlat, qidx, wrow, ckv_c, ckvt_c, kidx_c, m_ckv, m_ckvt, m_kidx, wuv_bd, topk):
    bn, rows, _ = qlat.shape
    nchunks, kc = ckv_c.shape[1], ckv_c.shape[2]
    nq = rows // Q_TILE

    def q_spec(width):
        return pl.BlockSpec((1, Q_TILE, width), lambda b, i: (b, i, 0))

    def k_spec(a):
        return pl.BlockSpec((1,) + a.shape[1:], lambda b, i: (b, 0, 0, 0))

    return pl.pallas_call(
        functools.partial(_dsa_kernel, topk=topk),
        grid=(bn, nq),
        in_specs=[q_spec(LAT_WIDTH), q_spec(IDX_WIDTH),
                  pl.BlockSpec((1, 1, GATE_ROWS, Q_TILE), lambda b, i: (b, i, 0, 0)),
                  k_spec(ckv_c), k_spec(ckvt_c), k_spec(kidx_c),
                  _resident(m_ckv.shape), _resident(m_ckvt.shape), _resident(m_kidx.shape), _resident(wuv_bd.shape)],
        out_specs=q_spec(ATT_WIDTH),
        out_shape=jax.ShapeDtypeStruct((bn, rows, ATT_WIDTH), BF16),
        scratch_shapes=[pltpu.VMEM((nchunks, kc, Q_TILE), F32),
                        pltpu.VMEM((Q_TILE // ATT_GROUP, KV_LATENT, ATT_HEADS * ATT_GROUP), F32),
                        pltpu.VMEM((Q_TILE // ATT_GROUP, kc, ATT_HEADS * ATT_GROUP), BF16)],
        compiler_params=pltpu.CompilerParams(
            dimension_semantics=("arbitrary", "arbitrary"), vmem_limit_bytes=VMEM_LIMIT_BYTES),
        name="dsa",
    )(qlat, qidx, wrow, ckv_c, ckvt_c, kidx_c, m_ckv, m_ckvt, m_kidx, wuv_bd)


def _split3(x):
    hi = x.astype(BF16)
    r = x - hi.astype(F32)
    mid = r.astype(BF16)
    lo = (r - mid.astype(F32)).astype(BF16)
    return hi, mid, lo


ML_EXT = ML_V_DIM + 16


def _mlstm_chunk(qk, vt, g, gt, state):
    L = qk.shape[0]
    s_idx = lax.broadcasted_iota(jnp.int32, (L, L), 0)
    t_idx = lax.broadcasted_iota(jnp.int32, (L, L), 1)
    causal = s_idx <= t_idx
    b_cols = sum(_dot((t_idx <= s_idx).astype(BF16), part) for part in _split3(g))
    b_rows = sum(_dot(part, causal.astype(BF16)) for part in _split3(gt))
    ones_blk = jnp.where(lax.broadcasted_iota(jnp.int32, (ML_EXT - ML_V_DIM, L), 0) == 0, 1.0, 0.0).astype(BF16)
    kq = ML_HEADS * ML_QK_DIM

    outs, new_state = [], []
    for h in range(ML_HEADS):
        ce, m_prev = state[h]
        c_col = g[:, GATE_I0 + h:GATE_I0 + h + 1] - b_cols[:, GATE_F0 + h:GATE_F0 + h + 1]
        b_row = b_rows[GATE_F0 - GATE_W0 + h:GATE_F0 - GATE_W0 + h + 1, :]
        ig_row = gt[GATE_I0 - GATE_W0 + h:GATE_I0 - GATE_W0 + h + 1, :]
        qh = qk[:, h * ML_QK_DIM:(h + 1) * ML_QK_DIM]
        kh = qk[:, kq + h * ML_QK_DIM:kq + (h + 1) * ML_QK_DIM]
        vt_ext = jnp.concatenate([vt[h * ML_V_DIM:(h + 1) * ML_V_DIM, :], ones_blk], axis=0)

        d_t = jnp.where(causal, c_col + b_row, -jnp.inf)
        inter = b_row + m_prev
        m_t = jnp.maximum(jnp.max(d_t, axis=0, keepdims=True), inter)
        w_inter = jnp.exp(inter - m_t)
        s_t = _dot_nt(kh, qh) * jnp.exp(d_t - m_t)
        r = _dot(vt_ext, s_t.astype(BF16)) + _dot_nt(ce.astype(BF16), qh) * w_inter
        num = r[:ML_V_DIM]
        den = r[ML_V_DIM:ML_V_DIM + 1]
        hh = num / jnp.maximum(jnp.abs(den), jnp.exp(-m_t))
        mu = jnp.mean(hh, axis=0, keepdims=True)
        hc = hh - mu
        var = jnp.mean(hc * hc, axis=0, keepdims=True)
        outs.append((hc * lax.rsqrt(var + LN_EPS)).T)

        b_end = b_row[:, L - 1:L]
        g_row = b_end - b_row + ig_row
        m_new = jnp.maximum(b_end + m_prev, jnp.max(g_row, axis=1, keepdims=True))
        decay = jnp.exp(b_end + m_prev - m_new)
        weighted = (vt_ext.astype(F32) * jnp.exp(g_row - m_new)).astype(BF16)
        new_state.append((decay * ce + _dot(weighted, kh), m_new))
    return outs, new_state


def _mlstm_kernel(qk_ref, vt_ref, og_ref, gates_ref, gt_ref, mqk_ref, mvt_ref, mgates_ref, mgt_ref, ng_ref, y_ref,
                  ce0_ref, m0_ref):
    L = mqk_ref.shape[0]
    n_chunks = qk_ref.shape[1] // L
    norm_g = ng_ref[...]

    @pl.when(pl.program_id(0) == 0)
    def _():
        state = [(jnp.zeros((ML_EXT, ML_QK_DIM), F32), jnp.full((1, 1), M_INIT, F32)) for _ in range(ML_HEADS)]
        _, state = _mlstm_chunk(mqk_ref[...], mvt_ref[...], mgates_ref[...], mgt_ref[...], state)
        for h in range(ML_HEADS):
            ce0_ref[h] = state[h][0]
            m0_ref[h] = jnp.broadcast_to(state[h][1], m0_ref.shape[1:])

    n_b = qk_ref.shape[0]
    state = [(ce0_ref[h], m0_ref[h][0:1, 0:1]) for h in range(ML_HEADS)] * n_b

    def body(c, flat):
        rows = pl.ds(pl.multiple_of(c * L, L), L)
        new_flat = []
        for b in range(n_b):
            state = [(flat[2 * (b * ML_HEADS + h)], flat[2 * (b * ML_HEADS + h) + 1]) for h in range(ML_HEADS)]
            outs, state = _mlstm_chunk(qk_ref[b, rows, :], vt_ref[b, c], gates_ref[b, rows, :], gt_ref[b, c], state)
            y = jnp.concatenate(outs, axis=1) * norm_g * og_ref[b, rows, :]
            y_ref[b, rows, :] = y.astype(BF16)
            new_flat += [x for pair in state for x in pair]
        return tuple(new_flat)

    lax.fori_loop(0, n_chunks, body, tuple(x for pair in state for x in pair))


def _mlstm_call(qk, vt_c, og, gates, gt_c, mqk, mvt, mgates, mgt, norm_g):
    bn, rows, _ = qk.shape
    nb = ML_BATCH if bn % ML_BATCH == 0 else 1

    def b_spec(a):
        return pl.BlockSpec((nb,) + a.shape[1:], lambda b: (b,) + (0,) * (a.ndim - 1))

    consts = (mqk, mvt, mgates, mgt, norm_g)
    return pl.pallas_call(
        _mlstm_kernel,
        grid=(bn // nb,),
        in_specs=[b_spec(qk), b_spec(vt_c), b_spec(og), b_spec(gates), b_spec(gt_c)]
        + [_resident(c.shape) for c in consts],
        out_specs=pl.BlockSpec((nb, rows, ML_WIDTH), lambda b: (b, 0, 0)),
        out_shape=jax.ShapeDtypeStruct((bn, rows, ML_WIDTH), BF16),
        scratch_shapes=[pltpu.VMEM((ML_HEADS, ML_EXT, ML_QK_DIM), F32), pltpu.VMEM((ML_HEADS, SUBLANES, LANES), F32)],
        compiler_params=pltpu.CompilerParams(
            dimension_semantics=("arbitrary",), vmem_limit_bytes=VMEM_LIMIT_BYTES),
        name="mlstm",
    )(qk, vt_c, og, gates, gt_c, *consts)


def _out_ffn_ln_kernel(ya_ref, ym_ref, h_ref, wo_ref, g2_ref, b2_ref, wg_ref, wu_ref, wd_ref, g3_ref, b3_ref,
                       o_ref, wg_s, wu_s, wd_s, *, alpha, n_stage):
    i = pl.program_id(0)

    @pl.when(i < n_stage)
    def _():
        _stage_ffn_weights(i, wg_ref, wu_ref, wd_ref, wg_s, wu_s, wd_s)

    @pl.when(i >= n_stage)
    def _():
        mix = _dot(ya_ref[...], wo_ref[:ATT_WIDTH, :]) + _dot(ym_ref[...], wo_ref[ATT_WIDTH:, :])
        h2 = _layer_norm(alpha * h_ref[...] + mix, g2_ref[...], b2_ref[...])
        o_ref[...] = _ffn_ln(h2, wg_s, wu_s, wd_s, g3_ref[...], b3_ref[...], alpha)


def _out_ffn_ln_call(ya, ym, h, wo, g2, b2, wg, wu, wd, g3, b3, alpha, tm):
    rows, d = h.shape
    n_stage, w_specs, w_scratch = _ffn_weight_specs(d, wg.shape[1])

    def row_spec(width):
        return pl.BlockSpec((tm, width), lambda i: (jnp.maximum(i - n_stage, 0), 0))

    return pl.pallas_call(
        functools.partial(_out_ffn_ln_kernel, alpha=alpha, n_stage=n_stage),
        grid=(n_stage + rows // tm,),
        in_specs=[row_spec(ya.shape[1]), row_spec(ym.shape[1]), row_spec(d),
                  _resident(wo.shape), _resident(g2.shape), _resident(b2.shape)] + w_specs
        + [_resident(g3.shape), _resident(b3.shape)],
        out_specs=row_spec(d),
        out_shape=jax.ShapeDtypeStruct((rows, d), F32),
        scratch_shapes=w_scratch,
        compiler_params=pltpu.CompilerParams(
            dimension_semantics=("arbitrary",), vmem_limit_bytes=VMEM_LIMIT_BYTES),
        name="out_ffn_ln",
    )(ya, ym, h, wo, g2, b2, wg, wu, wd, g3, b3)


def _block_diag(w):
    nh, a, b = w.shape
    eye = jnp.eye(nh, dtype=w.dtype)
    return (eye[:, None, :, None] * w[:, :, None, :]).reshape(nh * a, nh * b)


def _pad_rows(a, rows, value=0.0):
    return jnp.pad(a, ((0, rows - a.shape[0]), (0, 0)), constant_values=value)


def kernel(x, meta_tokens, ln1_g, ln1_b, ffn1_w_gate, ffn1_w_up, ffn1_w_down, w_in, w_uk, w_uv, kv_norm_g,
           conv_w, b_igate, b_fgate, ml_norm_g, w_out, ln2_g, ln2_b, ffn2_w_gate, ffn2_w_up, ffn2_w_down,
           ln3_g, ln3_b):
    depth = ln1_g.shape[0]
    assert depth == 1, "the meta-token shortcut below is only valid for a single layer"
    bsz, seq, d = x.shape
    assert seq % ROW_TILE == 0 and seq % ML_CHUNK == 0 and seq % Q_TILE == 0
    assert Q_TILE == GT_CHUNK and ML_CHUNK == GT_CHUNK and KEY_CHUNK == GT_CHUNK and ROW_TILE % GT_CHUNK == 0
    alpha = (2 * depth) ** 0.25
    topk = min(TOPK_MAX, seq // 4)

    row2 = lambda p: p[0].reshape(1, -1).astype(F32)
    bf = lambda w: w[0].astype(BF16)

    w_t = jnp.swapaxes(w_in[0], 0, 1)
    o_qa, o_ckv, o_qi, o_ki, o_wi = 0, 512, 640, 896, 960
    o_qk, o_v, o_o, o_ig, o_fg, o_end = 964, 1476, 1988, 2500, 2504, 2508
    wa = jnp.concatenate([w_t[o_qa:o_wi], w_t[o_wi:o_qk], w_t[o_ig:o_end],
                          jnp.zeros((LANES - GATE_END, d), F32)], axis=0).astype(BF16)
    wm = w_t[o_qk:o_ig].astype(BF16)
    gbias = jnp.concatenate([jnp.zeros((GATE_I0,), F32), b_igate[0], b_fgate[0],
                             jnp.zeros((LANES - GATE_END,), F32)]).reshape(1, LANES)
    wuk_bd = jnp.stack([_block_diag(w_uk[0][2 * p:2 * p + 2]) for p in range(HEAD_PAIRS)]).astype(BF16)
    wuv_bd = _block_diag(w_uv[0]).astype(BF16)
    kvg = row2(kv_norm_g)
    convw = conv_w[0].astype(F32)

    h1, h1_meta = _ffn_ln_call(x.reshape(bsz * seq, d), meta_tokens.astype(F32), ffn1_w_gate[0], ffn1_w_up[0],
                               ffn1_w_down[0], row2(ln1_g), row2(ln1_b), alpha, FFN_TILE)
    zero_tail = jnp.zeros((CONV_HIST, MLQK_WIDTH), F32)
    (_, m_ckv, _, m_kidx, m_qk, m_v, _, m_gates, m_tail) = _inproj_call(
        h1_meta[None], zero_tail, wa, wm, wuk_bd, kvg, convw, gbias, N_META)

    (qlat, ckv, qidx, kidx, qk, _, og, gates, _, gates_t, ckv_t, v_t) = _inproj_call(
        h1.reshape(bsz, seq, d), m_tail[0], wa, wm, wuk_bd, kvg, convw, gbias, ROW_TILE)

    nchunks = seq // KEY_CHUNK
    ckv_c = ckv.reshape(bsz, nchunks, KEY_CHUNK, KV_LATENT)
    kidx_c = kidx.reshape(bsz, nchunks, KEY_CHUNK, IDX_DIM)
    y_att = _dsa_call(qlat, qidx, gates_t, ckv_c, ckv_t, kidx_c,
                      m_ckv[0], m_ckv[0].T, m_kidx[0], wuv_bd, topk)

    lane = jnp.arange(LANES)
    pad_gate = jnp.where((lane >= GATE_I0) & (lane < GATE_F0), NEG_BIG, 0.0).astype(F32)
    mg = jnp.concatenate([m_gates[0], jnp.broadcast_to(pad_gate, (ML_CHUNK - N_META, LANES))], axis=0)
    gate_lanes = slice(GATE_W0, GATE_W0 + GATE_ROWS)
    y_ml = _mlstm_call(qk, v_t, og, gates, gates_t,
                       _pad_rows(m_qk[0], ML_CHUNK), _pad_rows(m_v[0], ML_CHUNK).T, mg, mg[:, gate_lanes].T,
                       row2(ml_norm_g))

    out = _out_ffn_ln_call(
        y_att.reshape(bsz * seq, ATT_WIDTH), y_ml.reshape(bsz * seq, ML_WIDTH), h1, bf(w_out),
        row2(ln2_g), row2(ln2_b), ffn2_w_gate[0], ffn2_w_up[0], ffn2_w_down[0], row2(ln3_g), row2(ln3_b),
        alpha, FFN_TILE)
    return out.reshape(bsz, seq, d)
```

```python
import functools

import jax
import jax.numpy as jnp
from jax import lax
from jax.experimental import pallas as pl
from jax.experimental.pallas import tpu as pltpu

F32 = jnp.float32
BF16 = jnp.bfloat16

N_META = 16
ATT_HEADS = 8
ATT_HEAD_DIM = 64
KV_LATENT = 128
IDX_HEADS = 4
IDX_DIM = 64
TOPK_MAX = 256
ML_HEADS = 4
ML_V_DIM = 128
ML_QK_DIM = 64
CONV_WIDTH = 4
GATE_SOFTCAP = 15.0
M_INIT = -1e30
LN_EPS = 1e-5
NEG_BIG = -1e30
LOG2_E = 1.4426950408889634

LANES = 128
SUBLANES = 8
VMEM_LIMIT_BYTES = 56 * 1024 * 1024

FF_CHUNK = 256
ROW_TILE = 512
FFN_TILE = 512
Q_TILE = 256
ATT_GROUP = 128
KEY_CHUNK = 256
N_BISECT = 16
REDUCE_ROWS = 32
ML_BATCH = 2
ML_CHUNK = 256


def _dot(a, b):
    return jnp.dot(a, b, preferred_element_type=F32)


def _dot_nt(a, b):
    return lax.dot_general(a, b, (((1,), (1,)), ((), ())), preferred_element_type=F32)


def _layer_norm(z, g, b):
    mu = jnp.mean(z, axis=-1, keepdims=True)
    zc = z - mu
    var = jnp.mean(zc * zc, axis=-1, keepdims=True)
    return zc * lax.rsqrt(var + LN_EPS) * g + b


def _sigmoid(x):
    return 1.0 / (1.0 + jnp.exp(-x))


def _swiglu_chunk(xb, wg_c, wu_c, wd_c):
    g = _dot(xb, wg_c)
    u = _dot(xb, wu_c)
    return _dot((g * _sigmoid(g) * u).astype(BF16), wd_c)


def _ffn_ln(x, wg_s, wu_s, wd_s, g, b, alpha):
    xb = x.astype(BF16)
    acc = jnp.zeros(x.shape, F32)
    for c in range(wg_s.shape[0]):
        acc = acc + _swiglu_chunk(xb, wg_s[c], wu_s[c], wd_s[c])
    return _layer_norm(alpha * x + 0.5 * acc, g, b)


def _stage_ffn_weights(step, wg_ref, wu_ref, wd_ref, wg_s, wu_s, wd_s):
    wg_s[step] = wg_ref[...].astype(BF16)
    wu_s[step] = wu_ref[...].astype(BF16)
    wd_s[step] = wd_ref[...].astype(BF16)


def _ffn_weight_specs(d, d_ff):
    n = d_ff // FF_CHUNK
    col = pl.BlockSpec((d, FF_CHUNK), lambda i: (0, jnp.minimum(i, n - 1)))
    row = pl.BlockSpec((FF_CHUNK, d), lambda i: (jnp.minimum(i, n - 1), 0))
    scratch = [pltpu.VMEM((n, d, FF_CHUNK), BF16), pltpu.VMEM((n, d, FF_CHUNK), BF16),
               pltpu.VMEM((n, FF_CHUNK, d), BF16)]
    return n, [col, col, row], scratch


def _staged_tile_index(i, n_stage):
    return jnp.maximum(i - (n_stage - 1), 0)


def _ffn_ln_kernel(x_ref, meta_ref, wg_ref, wu_ref, wd_ref, g_ref, b_ref, o_ref, ometa_ref,
                   wg_s, wu_s, wd_s, macc_ref, acc0_ref, *, alpha, n_stage):
    i = pl.program_id(0)

    @pl.when(i == 0)
    def _():
        macc_ref[...] = jnp.zeros(macc_ref.shape, F32)
        acc0_ref[...] = jnp.zeros(acc0_ref.shape, F32)

    @pl.when(i < n_stage)
    def _():
        _stage_ffn_weights(i, wg_ref, wu_ref, wd_ref, wg_s, wu_s, wd_s)
        macc_ref[...] += _swiglu_chunk(meta_ref[...].astype(BF16), wg_s[i], wu_s[i], wd_s[i])
        acc0_ref[...] += _swiglu_chunk(x_ref[...].astype(BF16), wg_s[i], wu_s[i], wd_s[i])

    @pl.when(i == n_stage - 1)
    def _():
        ometa_ref[...] = _layer_norm(alpha * meta_ref[...] + 0.5 * macc_ref[...], g_ref[...], b_ref[...])
        o_ref[...] = _layer_norm(alpha * x_ref[...] + 0.5 * acc0_ref[...], g_ref[...], b_ref[...])

    @pl.when(i >= n_stage)
    def _():
        o_ref[...] = _ffn_ln(x_ref[...], wg_s, wu_s, wd_s, g_ref[...], b_ref[...], alpha)


def _resident(shape):
    return pl.BlockSpec(shape, lambda *_: (0,) * len(shape), pipeline_mode=pl.Buffered(1))


def _ffn_ln_call(x, meta, wg, wu, wd, g, b, alpha, tm):
    rows, d = x.shape
    n_stage, w_specs, w_scratch = _ffn_weight_specs(d, wg.shape[1])
    row_spec = pl.BlockSpec((tm, d), lambda i: (_staged_tile_index(i, n_stage), 0))
    return pl.pallas_call(
        functools.partial(_ffn_ln_kernel, alpha=alpha, n_stage=n_stage),
        grid=(n_stage - 1 + rows // tm,),
        in_specs=[row_spec, _resident(meta.shape)] + w_specs + [_resident(g.shape), _resident(b.shape)],
        out_specs=[row_spec, pl.BlockSpec(meta.shape, lambda i: (0, 0))],
        out_shape=[jax.ShapeDtypeStruct((rows, d), F32), jax.ShapeDtypeStruct(meta.shape, F32)],
        scratch_shapes=w_scratch + [pltpu.VMEM(meta.shape, F32), pltpu.VMEM((tm, d), F32)],
        compiler_params=pltpu.CompilerParams(
            dimension_semantics=("arbitrary",), vmem_limit_bytes=VMEM_LIMIT_BYTES),
        name="ffn_ln",
    )(x, meta, wg, wu, wd, g, b)


ATT_WIDTH = ATT_HEADS * ATT_HEAD_DIM
IDX_WIDTH = IDX_HEADS * IDX_DIM
MLQK_WIDTH = 2 * ML_HEADS * ML_QK_DIM
ML_WIDTH = ML_HEADS * ML_V_DIM
LAT_WIDTH = ATT_HEADS * KV_LATENT
CONV_HIST = SUBLANES
GATE_W0 = IDX_DIM
GATE_I0, GATE_F0, GATE_END = GATE_W0 + IDX_HEADS, GATE_W0 + IDX_HEADS + ML_HEADS, GATE_W0 + IDX_HEADS + 2 * ML_HEADS
GATE_ROWS = 2 * SUBLANES
GT_CHUNK = 256
HEAD_PAIRS = ATT_HEADS // 2


def _inproj_kernel(h_ref, tail_ref, wa_ref, wm_ref, wuk_ref, kvg_ref, convw_ref, gbias_ref,
                   qlat_ref, ckv_ref, qidx_ref, kidx_ref, qk_ref, v_ref, og_ref, gates_ref, tailout_ref, *rest):
    carry_ref, wa_s, wm_s = rest[-3:]
    tm = h_ref.shape[1]

    @pl.when((pl.program_id(0) == 0) & (pl.program_id(1) == 0))
    def _():
        wa_s[...] = wa_ref[...].astype(F32).T.astype(BF16)
        wm_s[...] = wm_ref[...].astype(F32).T.astype(BF16)

    @pl.when(pl.program_id(1) == 0)
    def _():
        carry_ref[...] = tail_ref[...]

    xb = h_ref[0].astype(BF16)

    pa = _dot(xb, wa_s[...])
    q_a = pa[:, :ATT_WIDTH].astype(BF16)
    c0 = ATT_WIDTH
    ckv = pa[:, c0:c0 + KV_LATENT]
    c1 = c0 + KV_LATENT
    ckv = ckv * lax.rsqrt(jnp.mean(ckv * ckv, axis=-1, keepdims=True) + LN_EPS) * kvg_ref[...]
    ckv_ref[0] = ckv.astype(BF16)
    qidx_ref[0] = pa[:, c1:c1 + IDX_WIDTH].astype(BF16)
    c2 = c1 + IDX_WIDTH
    kidx_ref[0] = pa[:, c2:c2 + IDX_DIM].astype(BF16)
    pair_in, pair_out = 2 * ATT_HEAD_DIM, 2 * KV_LATENT
    for p in range(HEAD_PAIRS):
        ql = _dot(q_a[:, p * pair_in:(p + 1) * pair_in], wuk_ref[p])
        qlat_ref[0, :, p * pair_out:(p + 1) * pair_out] = (ql * (ATT_HEAD_DIM ** -0.5 * LOG2_E)).astype(BF16)

    pm = _dot(xb, wm_s[...])
    qk_raw = pm[:, :MLQK_WIDTH]
    v_ref[0] = pm[:, MLQK_WIDTH:MLQK_WIDTH + ML_WIDTH].astype(BF16)
    og_ref[0] = _sigmoid(pm[:, MLQK_WIDTH + ML_WIDTH:])

    ext = jnp.concatenate([carry_ref[...], qk_raw], axis=0)
    cw = convw_ref[...]
    conv = jnp.zeros_like(qk_raw)
    for j in range(CONV_WIDTH):
        s0 = CONV_HIST - (CONV_WIDTH - 1) + j
        conv = conv + ext[s0:s0 + tm] * cw[j:j + 1]
    act = conv * _sigmoid(conv)
    half = MLQK_WIDTH // 2
    qk_ref[0, :, :half] = act[:, :half].astype(BF16)
    qk_ref[0, :, half:] = (act[:, half:] * (ML_QK_DIM ** -0.5)).astype(BF16)
    carry_ref[...] = qk_raw[tm - CONV_HIST:]
    tailout_ref[0] = qk_raw[tm - CONV_HIST:]

    gr = pa[:, c2 + IDX_DIM - GATE_W0:]
    lane = lax.broadcasted_iota(jnp.int32, gr.shape, 1)
    sc = GATE_SOFTCAP * jnp.tanh((gr + gbias_ref[...]) / GATE_SOFTCAP)
    lf = -(jnp.maximum(-sc, 0.0) + jnp.log1p(jnp.exp(-jnp.abs(sc))))
    w_scaled = gr * (IDX_HEADS ** -0.5 * IDX_DIM ** -0.5)
    gates = jnp.where((lane < GATE_W0) | (lane >= GATE_END), 0.0,
                      jnp.where(lane < GATE_I0, w_scaled, jnp.where(lane < GATE_F0, sc, lf)))
    gates_ref[0] = gates
    if len(rest) == 6:
        gt_ref, ckvt_ref, vt_ref = rest[:3]
        gates_t = gates.T[GATE_W0:GATE_W0 + GATE_ROWS]
        ckv_t = ckv.T.astype(BF16)
        v_t = pm[:, MLQK_WIDTH:MLQK_WIDTH + ML_WIDTH].T.astype(BF16)
        for j in range(tm // GT_CHUNK):
            piece = slice(j * GT_CHUNK, (j + 1) * GT_CHUNK)
            gt_ref[0, j] = gates_t[:, piece]
            ckvt_ref[0, j] = ckv_t[:, piece]
            vt_ref[0, j] = v_t[:, piece]


def _inproj_call(h, tail, wa, wm, wuk_bd, kvg, convw, gbias, tm):
    bn, rows, d = h.shape
    nblk = rows // tm
    emit_gt = tm % GT_CHUNK == 0

    def row_spec(width):
        return pl.BlockSpec((1, tm, width), lambda b, j: (b, j, 0))

    outs = [
        (LAT_WIDTH, BF16), (KV_LATENT, BF16), (IDX_WIDTH, BF16), (IDX_DIM, BF16),
        (MLQK_WIDTH, BF16), (ML_WIDTH, BF16), (ML_WIDTH, F32), (LANES, F32),
    ]
    out_shape = [jax.ShapeDtypeStruct((bn, rows, w), dt) for w, dt in outs]
    out_specs = [row_spec(w) for w, _ in outs]
    out_shape.append(jax.ShapeDtypeStruct((bn, CONV_HIST, MLQK_WIDTH), F32))
    out_specs.append(pl.BlockSpec((1, CONV_HIST, MLQK_WIDTH), lambda b, j: (b, 0, 0)))
    if emit_gt:
        per_tile = tm // GT_CHUNK
        for height, dt in ((GATE_ROWS, F32), (KV_LATENT, BF16), (ML_WIDTH, BF16)):
            out_shape.append(jax.ShapeDtypeStruct((bn, rows // GT_CHUNK, height, GT_CHUNK), dt))
            out_specs.append(pl.BlockSpec((1, per_tile, height, GT_CHUNK), lambda b, j: (b, j, 0, 0)))
    return pl.pallas_call(
        _inproj_kernel,
        grid=(bn, nblk),
        in_specs=[
            row_spec(d),
            _resident(tail.shape), _resident(wa.shape), _resident(wm.shape),
            _resident(wuk_bd.shape), _resident(kvg.shape), _resident(convw.shape), _resident(gbias.shape),
        ],
        out_specs=out_specs,
        out_shape=out_shape,
        scratch_shapes=[pltpu.VMEM((CONV_HIST, MLQK_WIDTH), F32),
                        pltpu.VMEM(wa.shape[::-1], BF16), pltpu.VMEM(wm.shape[::-1], BF16)],
        compiler_params=pltpu.CompilerParams(
            dimension_semantics=("arbitrary", "arbitrary"), vmem_limit_bytes=VMEM_LIMIT_BYTES),
        name="in_proj",
    )(h, tail, wa, wm, wuk_bd, kvg, convw, gbias)


def _dsa_kernel(qlat_ref, qidx_ref, wrow_ref, ckv_ref, ckvt_ref, kidx_ref, mckv_ref, mckvt_ref, mkidx_ref,
                wuv_ref, y_ref, s_ref, acc_ref, p_ref, *, topk):
    _, kc, tq = s_ref.shape
    i = pl.program_id(1)
    nch = ((i + 1) * tq + kc - 1) // kc
    qreal = i * tq + lax.broadcasted_iota(jnp.int32, (1, tq), 1)
    kf = float(topk)

    wrow = wrow_ref[0, 0]
    qidx = qidx_ref[0]
    q_idx_all = jnp.concatenate([qidx[:, h * IDX_DIM:(h + 1) * IDX_DIM] for h in range(IDX_HEADS)], axis=0)
    wi = [wrow[h:h + 1, :] for h in range(IDX_HEADS)]

    def scores(k_rows):
        lg = _dot_nt(k_rows, q_idx_all)
        sc = jnp.zeros((k_rows.shape[0], tq), F32)
        for h in range(IDX_HEADS):
            sc = sc + jnp.maximum(lg[:, h * tq:(h + 1) * tq], 0.0) * wi[h]
        return sc

    s_meta = scores(mkidx_ref[...])

    def score_chunk(c, lo, hi):
        sc = scores(kidx_ref[0, c])
        valid = c * kc + lax.broadcasted_iota(jnp.int32, (kc, tq), 0) <= qreal
        s_ref[c] = jnp.where(valid, sc, -jnp.inf)
        groups = (kc // REDUCE_ROWS, REDUCE_ROWS, tq)
        lo = jnp.minimum(lo, jnp.min(sc.reshape(groups), axis=0))
        hi = jnp.maximum(hi, jnp.max(sc.reshape(groups), axis=0))
        return lo, hi

    def score_pair(c2, carry):
        lo, hi = score_chunk(2 * c2, *carry)
        return score_chunk(jnp.minimum(2 * c2 + 1, nch - 1), lo, hi)

    lo, hi = lax.fori_loop(0, (nch + 1) // 2, score_pair,
                           (jnp.full((REDUCE_ROWS, tq), jnp.inf, F32), jnp.full((REDUCE_ROWS, tq), -jnp.inf, F32)))
    lo = jnp.minimum(jnp.min(lo, axis=0, keepdims=True), jnp.min(s_meta, axis=0, keepdims=True))
    hi = jnp.maximum(jnp.max(hi, axis=0, keepdims=True), jnp.max(s_meta, axis=0, keepdims=True))

    def key_reduce(reduce, combine, per_chunk, init):
        def body(c, acc):
            x = per_chunk(s_ref[c]).reshape(kc // REDUCE_ROWS, REDUCE_ROWS, tq)
            return combine(acc, reduce(x, axis=0))
        acc = lax.fori_loop(0, nch, body, jnp.full((REDUCE_ROWS, tq), init, F32))
        return combine(reduce(acc, axis=0, keepdims=True), reduce(per_chunk(s_meta), axis=0, keepdims=True))

    def count(pred):
        return key_reduce(jnp.sum, jnp.add, lambda sc: jnp.where(pred(sc), 1.0, 0.0), 0.0)

    def max_where(pred):
        return key_reduce(jnp.max, jnp.maximum, lambda sc: jnp.where(pred(sc), sc, -jnp.inf), -jnp.inf)

    def bisect(_, carry):
        lo, hi = carry
        mid = 0.5 * lo + 0.5 * hi
        up = count(lambda sc: sc > mid) >= kf
        return jnp.where(up, mid, lo), jnp.where(up, hi, mid)

    lo, hi = lax.fori_loop(0, N_BISECT, bisect, (lo, hi))

    n_valid = (qreal + (N_META + 1)).astype(F32)
    small = n_valid <= kf
    cand = max_where(lambda sc: sc <= hi)
    n_ge = count(lambda sc: sc >= cand)
    done = jnp.where(small | (n_ge >= kf), 1.0, 0.0)

    def not_finished(state):
        return jnp.min(state[1]) < 0.5

    def step_down(state):
        cand, done, _ = state
        nxt = jnp.where(done > 0.5, cand, max_where(lambda sc: sc < cand))
        n_ge = count(lambda sc: sc >= nxt)
        return nxt, jnp.where(n_ge >= kf, 1.0, done), n_ge

    cand, _, n_ge = lax.while_loop(not_finished, step_down, (cand, done, n_ge))
    thr = jnp.where(small, -jnp.inf, cand)
    n_eq = count(lambda sc: sc == thr)
    need = jnp.where(small, 0.0, kf - (n_ge - n_eq))
    ranked_ties = jnp.max(jnp.where(n_eq > need, 1.0, 0.0)) > 0.5

    qlat = qlat_ref[0]
    n_groups = tq // ATT_GROUP
    onehot = (lax.broadcasted_iota(jnp.int32, (ATT_GROUP, ATT_GROUP), 0)
              == lax.broadcasted_iota(jnp.int32, (ATT_GROUP, ATT_GROUP), 1)).astype(BF16)
    q_aug = [jnp.concatenate(
        [jnp.concatenate([qlat[g * ATT_GROUP:(g + 1) * ATT_GROUP, h * KV_LATENT:(h + 1) * KV_LATENT], onehot], axis=1)
         for h in range(ATT_HEADS)], axis=0) for g in range(n_groups)]
    hq = ATT_HEADS * ATT_GROUP

    def lower_tri(n):
        return (lax.broadcasted_iota(jnp.int32, (n, n), 1) <= lax.broadcasted_iota(jnp.int32, (n, n), 0)).astype(BF16)

    def attention(ranked):
        def mask_bias(sc, eq_seen):
            if not ranked:
                return jnp.where(sc >= thr, 0.0, NEG_BIG).astype(BF16), eq_seen
            n = sc.shape[0]
            eq = sc == thr
            rank = _dot(lower_tri(n), jnp.where(eq, 1.0, 0.0).astype(BF16)) + eq_seen
            keep = (sc > thr) | (eq & (rank <= need))
            return jnp.where(keep, 0.0, NEG_BIG).astype(BF16), rank[n - 1:n, :]

        def logits(g, kv, bias):
            k_aug = jnp.concatenate([kv, bias[:, g * ATT_GROUP:(g + 1) * ATT_GROUP]], axis=1)
            return _dot_nt(k_aug, q_aug[g])

        def fold_in(g, c_prev, a_prev):
            acc_ref[g] = a_prev * acc_ref[g] + _dot(ckvt_ref[0, c_prev], p_ref[g])

        def attend(c, carry):
            bias, eq_seen = mask_bias(s_ref[c], carry[0])
            out = [eq_seen]
            for g in range(n_groups):
                m, l, a_prev = carry[1 + 3 * g:4 + 3 * g]
                fold_in(g, c - 1, a_prev)
                s = logits(g, ckv_ref[0, c], bias)
                m_new = jnp.maximum(m, jnp.max(s, axis=0, keepdims=True))
                a = jnp.exp2(m - m_new)
                p = jnp.exp2(s - m_new)
                p_ref[g] = p.astype(BF16)
                out += [m_new, a * l + jnp.sum(p, axis=0, keepdims=True), a]
            return tuple(out)

        bias_m, eq_seen = mask_bias(s_meta, jnp.zeros((1, tq), F32))
        bias_0, eq_seen = mask_bias(s_ref[0], eq_seen)
        init = [eq_seen]
        for g in range(n_groups):
            s_m = logits(g, mckv_ref[...], bias_m)
            s_0 = logits(g, ckv_ref[0, 0], bias_0)
            m = jnp.maximum(jnp.max(s_m, axis=0, keepdims=True), jnp.max(s_0, axis=0, keepdims=True))
            p_m = jnp.exp2(s_m - m)
            p_0 = jnp.exp2(s_0 - m)
            acc_ref[g] = _dot(mckvt_ref[...], p_m.astype(BF16))
            p_ref[g] = p_0.astype(BF16)
            init += [m, jnp.sum(p_m, axis=0, keepdims=True) + jnp.sum(p_0, axis=0, keepdims=True),
                     jnp.ones((1, hq), F32)]
        carry = lax.fori_loop(1, nch, attend, tuple(init))
        rows = []
        for g in range(n_groups):
            _, l, a_last = carry[1 + 3 * g:4 + 3 * g]
            fold_in(g, nch - 1, a_last)
            o_t = (acc_ref[g] / l).T
            rows.append(jnp.concatenate([o_t[h * ATT_GROUP:(h + 1) * ATT_GROUP] for h in range(ATT_HEADS)], axis=1))
        return _dot(jnp.concatenate(rows, axis=0).astype(BF16), wuv_ref[...]).astype(BF16)

    y_ref[0] = lax.cond(ranked_ties, lambda: attention(True), lambda: attention(False))


def _dsa_call(qlat, qidx, wrow, ckv_c, ckvt_c, kidx_c, m_ckv, m_ckvt, m_kidx, wuv_bd, topk):
    bn, rows, _ = qlat.shape
    nchunks, kc = ckv_c.shape[1], ckv_c.shape[2]
    nq = rows // Q_TILE

    def q_spec(width):
        return pl.BlockSpec((1, Q_TILE, width), lambda b, i: (b, i, 0))

    def k_spec(a):
        return pl.BlockSpec((1,) + a.shape[1:], lambda b, i: (b, 0, 0, 0))

    return pl.pallas_call(
        functools.partial(_dsa_kernel, topk=topk),
        grid=(bn, nq),
        in_specs=[q_spec(LAT_WIDTH), q_spec(IDX_WIDTH),
                  pl.BlockSpec((1, 1, GATE_ROWS, Q_TILE), lambda b, i: (b, i, 0, 0)),
                  k_spec(ckv_c), k_spec(ckvt_c), k_spec(kidx_c),
                  _resident(m_ckv.shape), _resident(m_ckvt.shape), _resident(m_kidx.shape), _resident(wuv_bd.shape)],
        out_specs=q_spec(ATT_WIDTH),
        out_shape=jax.ShapeDtypeStruct((bn, rows, ATT_WIDTH), BF16),
        scratch_shapes=[pltpu.VMEM((nchunks, kc, Q_TILE), F32),
                        pltpu.VMEM((Q_TILE // ATT_GROUP, KV_LATENT, ATT_HEADS * ATT_GROUP), F32),
                        pltpu.VMEM((Q_TILE // ATT_GROUP, kc, ATT_HEADS * ATT_GROUP), BF16)],
        compiler_params=pltpu.CompilerParams(
            dimension_semantics=("arbitrary", "arbitrary"), vmem_limit_bytes=VMEM_LIMIT_BYTES),
        name="dsa",
    )(qlat, qidx, wrow, ckv_c, ckvt_c, kidx_c, m_ckv, m_ckvt, m_kidx, wuv_bd)


def _split3(x):
    hi = x.astype(BF16)
    r = x - hi.astype(F32)
    mid = r.astype(BF16)
    lo = (r - mid.astype(F32)).astype(BF16)
    return hi, mid, lo


ML_EXT = ML_V_DIM + 16


def _mlstm_chunk(qk, vt, g, gt, state):
    L = qk.shape[0]
    s_idx = lax.broadcasted_iota(jnp.int32, (L, L), 0)
    t_idx = lax.broadcasted_iota(jnp.int32, (L, L), 1)
    causal = s_idx <= t_idx
    b_cols = sum(_dot((t_idx <= s_idx).astype(BF16), part) for part in _split3(g))
    b_rows = sum(_dot(part, causal.astype(BF16)) for part in _split3(gt))
    ones_blk = jnp.where(lax.broadcasted_iota(jnp.int32, (ML_EXT - ML_V_DIM, L), 0) == 0, 1.0, 0.0).astype(BF16)
    kq = ML_HEADS * ML_QK_DIM

    outs, new_state = [], []
    for h in range(ML_HEADS):
        ce, m_prev = state[h]
        c_col = g[:, GATE_I0 + h:GATE_I0 + h + 1] - b_cols[:, GATE_F0 + h:GATE_F0 + h + 1]
        b_row = b_rows[GATE_F0 - GATE_W0 + h:GATE_F0 - GATE_W0 + h + 1, :]
        ig_row = gt[GATE_I0 - GATE_W0 + h:GATE_I0 - GATE_W0 + h + 1, :]
        qh = qk[:, h * ML_QK_DIM:(h + 1) * ML_QK_DIM]
        kh = qk[:, kq + h * ML_QK_DIM:kq + (h + 1) * ML_QK_DIM]
        vt_ext = jnp.concatenate([vt[h * ML_V_DIM:(h + 1) * ML_V_DIM, :], ones_blk], axis=0)

        d_t = jnp.where(causal, c_col + b_row, -jnp.inf)
        inter = b_row + m_prev
        m_t = jnp.maximum(jnp.max(d_t, axis=0, keepdims=True), inter)
        w_inter = jnp.exp(inter - m_t)
        s_t = _dot_nt(kh, qh) * jnp.exp(d_t - m_t)
        r = _dot(vt_ext, s_t.astype(BF16)) + _dot_nt(ce.astype(BF16), qh) * w_inter
        num = r[:ML_V_DIM]
        den = r[ML_V_DIM:ML_V_DIM + 1]
        hh = num / jnp.maximum(jnp.abs(den), jnp.exp(-m_t))
        mu = jnp.mean(hh, axis=0, keepdims=True)
        hc = hh - mu
        var = jnp.mean(hc * hc, axis=0, keepdims=True)
        outs.append((hc * lax.rsqrt(var + LN_EPS)).T)

        b_end = b_row[:, L - 1:L]
        g_row = b_end - b_row + ig_row
        m_new = jnp.maximum(b_end + m_prev, jnp.max(g_row, axis=1, keepdims=True))
        decay = jnp.exp(b_end + m_prev - m_new)
        weighted = (vt_ext.astype(F32) * jnp.exp(g_row - m_new)).astype(BF16)
        new_state.append((decay * ce + _dot(weighted, kh), m_new))
    return outs, new_state


def _mlstm_kernel(qk_ref, vt_ref, og_ref, gates_ref, gt_ref, mqk_ref, mvt_ref, mgates_ref, mgt_ref, ng_ref, y_ref,
                  ce0_ref, m0_ref):
    L = mqk_ref.shape[0]
    n_chunks = qk_ref.shape[1] // L
    norm_g = ng_ref[...]

    @pl.when(pl.program_id(0) == 0)
    def _():
        state = [(jnp.zeros((ML_EXT, ML_QK_DIM), F32), jnp.full((1, 1), M_INIT, F32)) for _ in range(ML_HEADS)]
        _, state = _mlstm_chunk(mqk_ref[...], mvt_ref[...], mgates_ref[...], mgt_ref[...], state)
        for h in range(ML_HEADS):
            ce0_ref[h] = state[h][0]
            m0_ref[h] = jnp.broadcast_to(state[h][1], m0_ref.shape[1:])

    n_b = qk_ref.shape[0]
    state = [(ce0_ref[h], m0_ref[h][0:1, 0:1]) for h in range(ML_HEADS)] * n_b

    def body(c, flat):
        rows = pl.ds(pl.multiple_of(c * L, L), L)
        new_flat = []
        for b in range(n_b):
            state = [(flat[2 * (b * ML_HEADS + h)], flat[2 * (b * ML_HEADS + h) + 1]) for h in range(ML_HEADS)]
            outs, state = _mlstm_chunk(qk_ref[b, rows, :], vt_ref[b, c], gates_ref[b, rows, :], gt_ref[b, c], state)
            y = jnp.concatenate(outs, axis=1) * norm_g * og_ref[b, rows, :]
            y_ref[b, rows, :] = y.astype(BF16)
            new_flat += [x for pair in state for x in pair]
        return tuple(new_flat)

    lax.fori_loop(0, n_chunks, body, tuple(x for pair in state for x in pair))


def _mlstm_call(qk, vt_c, og, gates, gt_c, mqk, mvt, mgates, mgt, norm_g):
    bn, rows, _ = qk.shape
    nb = ML_BATCH if bn % ML_BATCH == 0 else 1

    def b_spec(a):
        return pl.BlockSpec((nb,) + a.shape[1:], lambda b: (b,) + (0,) * (a.ndim - 1))

    consts = (mqk, mvt, mgates, mgt, norm_g)
    return pl.pallas_call(
        _mlstm_kernel,
        grid=(bn // nb,),
        in_specs=[b_spec(qk), b_spec(vt_c), b_spec(og), b_spec(gates), b_spec(gt_c)]
        + [_resident(c.shape) for c in consts],
        out_specs=pl.BlockSpec((nb, rows, ML_WIDTH), lambda b: (b, 0, 0)),
        out_shape=jax.ShapeDtypeStruct((bn, rows, ML_WIDTH), BF16),
        scratch_shapes=[pltpu.VMEM((ML_HEADS, ML_EXT, ML_QK_DIM), F32), pltpu.VMEM((ML_HEADS, SUBLANES, LANES), F32)],
        compiler_params=pltpu.CompilerParams(
            dimension_semantics=("arbitrary",), vmem_limit_bytes=VMEM_LIMIT_BYTES),
        name="mlstm",
    )(qk, vt_c, og, gates, gt_c, *consts)


def _out_ffn_ln_kernel(ya_ref, ym_ref, h_ref, wo_ref, g2_ref, b2_ref, wg_ref, wu_ref, wd_ref, g3_ref, b3_ref,
                       o_ref, wg_s, wu_s, wd_s, h2_ref, acc0_ref, *, alpha, n_stage):
    i = pl.program_id(0)

    def mixed():
        mix = _dot(ya_ref[...], wo_ref[:ATT_WIDTH, :]) + _dot(ym_ref[...], wo_ref[ATT_WIDTH:, :])
        return _layer_norm(alpha * h_ref[...] + mix, g2_ref[...], b2_ref[...])

    @pl.when(i == 0)
    def _():
        h2_ref[...] = mixed()
        acc0_ref[...] = jnp.zeros(acc0_ref.shape, F32)

    @pl.when(i < n_stage)
    def _():
        _stage_ffn_weights(i, wg_ref, wu_ref, wd_ref, wg_s, wu_s, wd_s)
        acc0_ref[...] += _swiglu_chunk(h2_ref[...].astype(BF16), wg_s[i], wu_s[i], wd_s[i])

    @pl.when(i == n_stage - 1)
    def _():
        o_ref[...] = _layer_norm(alpha * h2_ref[...] + 0.5 * acc0_ref[...], g3_ref[...], b3_ref[...])

    @pl.when(i >= n_stage)
    def _():
        o_ref[...] = _ffn_ln(mixed(), wg_s, wu_s, wd_s, g3_ref[...], b3_ref[...], alpha)


def _out_ffn_ln_call(ya, ym, h, wo, g2, b2, wg, wu, wd, g3, b3, alpha, tm):
    rows, d = h.shape
    n_stage, w_specs, w_scratch = _ffn_weight_specs(d, wg.shape[1])

    def row_spec(width):
        return pl.BlockSpec((tm, width), lambda i: (_staged_tile_index(i, n_stage), 0))

    return pl.pallas_call(
        functools.partial(_out_ffn_ln_kernel, alpha=alpha, n_stage=n_stage),
        grid=(n_stage - 1 + rows // tm,),
        in_specs=[row_spec(ya.shape[1]), row_spec(ym.shape[1]), row_spec(d),
                  _resident(wo.shape), _resident(g2.shape), _resident(b2.shape)] + w_specs
        + [_resident(g3.shape), _resident(b3.shape)],
        out_specs=row_spec(d),
        out_shape=jax.ShapeDtypeStruct((rows, d), F32),
        scratch_shapes=w_scratch + [pltpu.VMEM((tm, d), F32), pltpu.VMEM((tm, d), F32)],
        compiler_params=pltpu.CompilerParams(
            dimension_semantics=("arbitrary",), vmem_limit_bytes=VMEM_LIMIT_BYTES),
        name="out_ffn_ln",
    )(ya, ym, h, wo, g2, b2, wg, wu, wd, g3, b3)


def _block_diag(w):
    nh, a, b = w.shape
    eye = jnp.eye(nh, dtype=w.dtype)
    return (eye[:, None, :, None] * w[:, :, None, :]).reshape(nh * a, nh * b)


def _pad_rows(a, rows, value=0.0):
    return jnp.pad(a, ((0, rows - a.shape[0]), (0, 0)), constant_values=value)


def kernel(x, meta_tokens, ln1_g, ln1_b, ffn1_w_gate, ffn1_w_up, ffn1_w_down, w_in, w_uk, w_uv, kv_norm_g,
           conv_w, b_igate, b_fgate, ml_norm_g, w_out, ln2_g, ln2_b, ffn2_w_gate, ffn2_w_up, ffn2_w_down,
           ln3_g, ln3_b):
    depth = ln1_g.shape[0]
    assert depth == 1, "the meta-token shortcut below is only valid for a single layer"
    bsz, seq, d = x.shape
    assert seq % ROW_TILE == 0 and seq % ML_CHUNK == 0 and seq % Q_TILE == 0
    assert Q_TILE == GT_CHUNK and ML_CHUNK == GT_CHUNK and KEY_CHUNK == GT_CHUNK and ROW_TILE % GT_CHUNK == 0
    alpha = (2 * depth) ** 0.25
    topk = min(TOPK_MAX, seq // 4)

    row2 = lambda p: p[0].reshape(1, -1).astype(F32)
    bf = lambda w: w[0].astype(BF16)

    w_t = jnp.swapaxes(w_in[0], 0, 1)
    o_qa, o_ckv, o_qi, o_ki, o_wi = 0, 512, 640, 896, 960
    o_qk, o_v, o_o, o_ig, o_fg, o_end = 964, 1476, 1988, 2500, 2504, 2508
    wa = jnp.concatenate([w_t[o_qa:o_wi], w_t[o_wi:o_qk], w_t[o_ig:o_end],
                          jnp.zeros((LANES - GATE_END, d), F32)], axis=0).astype(BF16)
    wm = w_t[o_qk:o_ig].astype(BF16)
    gbias = jnp.concatenate([jnp.zeros((GATE_I0,), F32), b_igate[0], b_fgate[0],
                             jnp.zeros((LANES - GATE_END,), F32)]).reshape(1, LANES)
    wuk_bd = jnp.stack([_block_diag(w_uk[0][2 * p:2 * p + 2]) for p in range(HEAD_PAIRS)]).astype(BF16)
    wuv_bd = _block_diag(w_uv[0]).astype(BF16)
    kvg = row2(kv_norm_g)
    convw = conv_w[0].astype(F32)

    h1, h1_meta = _ffn_ln_call(x.reshape(bsz * seq, d), meta_tokens.astype(F32), ffn1_w_gate[0], ffn1_w_up[0],
                               ffn1_w_down[0], row2(ln1_g), row2(ln1_b), alpha, FFN_TILE)
    zero_tail = jnp.zeros((CONV_HIST, MLQK_WIDTH), F32)
    (_, m_ckv, _, m_kidx, m_qk, m_v, _, m_gates, m_tail) = _inproj_call(
        h1_meta[None], zero_tail, wa, wm, wuk_bd, kvg, convw, gbias, N_META)

    (qlat, ckv, qidx, kidx, qk, _, og, gates, _, gates_t, ckv_t, v_t) = _inproj_call(
        h1.reshape(bsz, seq, d), m_tail[0], wa, wm, wuk_bd, kvg, convw, gbias, ROW_TILE)

    nchunks = seq // KEY_CHUNK
    ckv_c = ckv.reshape(bsz, nchunks, KEY_CHUNK, KV_LATENT)
    kidx_c = kidx.reshape(bsz, nchunks, KEY_CHUNK, IDX_DIM)
    y_att = _dsa_call(qlat, qidx, gates_t, ckv_c, ckv_t, kidx_c,
                      m_ckv[0], m_ckv[0].T, m_kidx[0], wuv_bd, topk)

    lane = jnp.arange(LANES)
    pad_gate = jnp.where((lane >= GATE_I0) & (lane < GATE_F0), NEG_BIG, 0.0).astype(F32)
    mg = jnp.concatenate([m_gates[0], jnp.broadcast_to(pad_gate, (ML_CHUNK - N_META, LANES))], axis=0)
    gate_lanes = slice(GATE_W0, GATE_W0 + GATE_ROWS)
    y_ml = _mlstm_call(qk, v_t, og, gates, gates_t,
                       _pad_rows(m_qk[0], ML_CHUNK), _pad_rows(m_v[0], ML_CHUNK).T, mg, mg[:, gate_lanes].T,
                       row2(ml_norm_g))

    out = _out_ffn_ln_call(
        y_att.reshape(bsz * seq, ATT_WIDTH), y_ml.reshape(bsz * seq, ML_WIDTH), h1, bf(w_out),
        row2(ln2_g), row2(ln2_b), ffn2_w_gate[0], ffn2_w_up[0], ffn2_w_down[0], row2(ln3_g), row2(ln3_b),
        alpha, FFN_TILE)
    return out.reshape(bsz, seq, d)
```

```python
import functools

import jax
import jax.numpy as jnp
from jax import lax
from jax.experimental import pallas as pl
from jax.experimental.pallas import tpu as pltpu

F32 = jnp.float32
BF16 = jnp.bfloat16

N_META = 16
ATT_HEADS = 8
ATT_HEAD_DIM = 64
KV_LATENT = 128
IDX_HEADS = 4
IDX_DIM = 64
TOPK_MAX = 256
ML_HEADS = 4
ML_V_DIM = 128
ML_QK_DIM = 64
CONV_WIDTH = 4
GATE_SOFTCAP = 15.0
M_INIT = -1e30
LN_EPS = 1e-5
NEG_BIG = -1e30
LOG2_E = 1.4426950408889634

LANES = 128
SUBLANES = 8
VMEM_LIMIT_BYTES = 56 * 1024 * 1024

FF_CHUNK = 256
ROW_TILE = 512
FFN_TILE = 512
Q_TILE = 256
ATT_GROUP = 128
KEY_CHUNK = 256
N_BISECT = 20
REDUCE_ROWS = 32
ML_BATCH = 2
ML_CHUNK = 256


def _dot(a, b):
    return jnp.dot(a, b, preferred_element_type=F32)


def _dot_nt(a, b):
    return lax.dot_general(a, b, (((1,), (1,)), ((), ())), preferred_element_type=F32)


def _layer_norm(z, g, b):
    mu = jnp.mean(z, axis=-1, keepdims=True)
    zc = z - mu
    var = jnp.mean(zc * zc, axis=-1, keepdims=True)
    return zc * lax.rsqrt(var + LN_EPS) * g + b


def _sigmoid(x):
    return 1.0 / (1.0 + jnp.exp(-x))


def _swiglu_chunk(xb, wg_c, wu_c, wd_c):
    g = _dot(xb, wg_c)
    u = _dot(xb, wu_c)
    return _dot((g * _sigmoid(g) * u).astype(BF16), wd_c)


def _ffn_ln(x, wg_s, wu_s, wd_s, g, b, alpha):
    xb = x.astype(BF16)
    acc = jnp.zeros(x.shape, F32)
    for c in range(wg_s.shape[0]):
        acc = acc + _swiglu_chunk(xb, wg_s[c], wu_s[c], wd_s[c])
    return _layer_norm(alpha * x + 0.5 * acc, g, b)


def _stage_ffn_weights(step, wg_ref, wu_ref, wd_ref, wg_s, wu_s, wd_s):
    wg_s[step] = wg_ref[...].astype(BF16)
    wu_s[step] = wu_ref[...].astype(BF16)
    wd_s[step] = wd_ref[...].astype(BF16)


def _ffn_weight_specs(d, d_ff):
    n = d_ff // FF_CHUNK
    col = pl.BlockSpec((d, FF_CHUNK), lambda i: (0, jnp.minimum(i, n - 1)))
    row = pl.BlockSpec((FF_CHUNK, d), lambda i: (jnp.minimum(i, n - 1), 0))
    scratch = [pltpu.VMEM((n, d, FF_CHUNK), BF16), pltpu.VMEM((n, d, FF_CHUNK), BF16),
               pltpu.VMEM((n, FF_CHUNK, d), BF16)]
    return n, [col, col, row], scratch


def _staged_tile_index(i, n_stage):
    return jnp.maximum(i - (n_stage - 1), 0)


def _ffn_ln_kernel(x_ref, meta_ref, wg_ref, wu_ref, wd_ref, g_ref, b_ref, o_ref, ometa_ref,
                   wg_s, wu_s, wd_s, macc_ref, acc0_ref, *, alpha, n_stage):
    i = pl.program_id(0)

    @pl.when(i == 0)
    def _():
        macc_ref[...] = jnp.zeros(macc_ref.shape, F32)
        acc0_ref[...] = jnp.zeros(acc0_ref.shape, F32)

    @pl.when(i < n_stage)
    def _():
        _stage_ffn_weights(i, wg_ref, wu_ref, wd_ref, wg_s, wu_s, wd_s)
        macc_ref[...] += _swiglu_chunk(meta_ref[...].astype(BF16), wg_s[i], wu_s[i], wd_s[i])
        acc0_ref[...] += _swiglu_chunk(x_ref[...].astype(BF16), wg_s[i], wu_s[i], wd_s[i])

    @pl.when(i == n_stage - 1)
    def _():
        ometa_ref[...] = _layer_norm(alpha * meta_ref[...] + 0.5 * macc_ref[...], g_ref[...], b_ref[...])
        o_ref[...] = _layer_norm(alpha * x_ref[...] + 0.5 * acc0_ref[...], g_ref[...], b_ref[...])

    @pl.when(i >= n_stage)
    def _():
        o_ref[...] = _ffn_ln(x_ref[...], wg_s, wu_s, wd_s, g_ref[...], b_ref[...], alpha)


def _resident(shape):
    return pl.BlockSpec(shape, lambda *_: (0,) * len(shape), pipeline_mode=pl.Buffered(1))


def _ffn_ln_call(x, meta, wg, wu, wd, g, b, alpha, tm):
    rows, d = x.shape
    n_stage, w_specs, w_scratch = _ffn_weight_specs(d, wg.shape[1])
    row_spec = pl.BlockSpec((tm, d), lambda i: (_staged_tile_index(i, n_stage), 0))
    return pl.pallas_call(
        functools.partial(_ffn_ln_kernel, alpha=alpha, n_stage=n_stage),
        grid=(n_stage - 1 + rows // tm,),
        in_specs=[row_spec, _resident(meta.shape)] + w_specs + [_resident(g.shape), _resident(b.shape)],
        out_specs=[row_spec, pl.BlockSpec(meta.shape, lambda i: (0, 0))],
        out_shape=[jax.ShapeDtypeStruct((rows, d), F32), jax.ShapeDtypeStruct(meta.shape, F32)],
        scratch_shapes=w_scratch + [pltpu.VMEM(meta.shape, F32), pltpu.VMEM((tm, d), F32)],
        compiler_params=pltpu.CompilerParams(
            dimension_semantics=("arbitrary",), vmem_limit_bytes=VMEM_LIMIT_BYTES),
        name="ffn_ln",
    )(x, meta, wg, wu, wd, g, b)


ATT_WIDTH = ATT_HEADS * ATT_HEAD_DIM
IDX_WIDTH = IDX_HEADS * IDX_DIM
MLQK_WIDTH = 2 * ML_HEADS * ML_QK_DIM
ML_WIDTH = ML_HEADS * ML_V_DIM
LAT_WIDTH = ATT_HEADS * KV_LATENT
CONV_HIST = SUBLANES
GATE_W0 = IDX_DIM
GATE_I0, GATE_F0, GATE_END = GATE_W0 + IDX_HEADS, GATE_W0 + IDX_HEADS + ML_HEADS, GATE_W0 + IDX_HEADS + 2 * ML_HEADS
GATE_ROWS = 2 * SUBLANES
GT_CHUNK = 256
HEAD_PAIRS = ATT_HEADS // 2


def _inproj_kernel(h_ref, tail_ref, wa_ref, wm_ref, wuk_ref, kvg_ref, convw_ref, gbias_ref,
                   qlat_ref, ckv_ref, qidx_ref, kidx_ref, qk_ref, v_ref, og_ref, gates_ref, tailout_ref, *rest):
    carry_ref, wa_s, wm_s = rest[-3:]
    tm = h_ref.shape[1]

    @pl.when((pl.program_id(0) == 0) & (pl.program_id(1) == 0))
    def _():
        wa_s[...] = wa_ref[...].astype(F32).T.astype(BF16)
        wm_s[...] = wm_ref[...].astype(F32).T.astype(BF16)

    @pl.when(pl.program_id(1) == 0)
    def _():
        carry_ref[...] = tail_ref[...]

    xb = h_ref[0].astype(BF16)

    pa = _dot(xb, wa_s[...])
    q_a = pa[:, :ATT_WIDTH].astype(BF16)
    c0 = ATT_WIDTH
    ckv = pa[:, c0:c0 + KV_LATENT]
    c1 = c0 + KV_LATENT
    ckv = ckv * lax.rsqrt(jnp.mean(ckv * ckv, axis=-1, keepdims=True) + LN_EPS) * kvg_ref[...]
    ckv_ref[0] = ckv.astype(BF16)
    qidx_ref[0] = pa[:, c1:c1 + IDX_WIDTH].astype(BF16)
    c2 = c1 + IDX_WIDTH
    kidx_ref[0] = pa[:, c2:c2 + IDX_DIM].astype(BF16)
    pair_in, pair_out = 2 * ATT_HEAD_DIM, 2 * KV_LATENT
    for p in range(HEAD_PAIRS):
        ql = _dot(q_a[:, p * pair_in:(p + 1) * pair_in], wuk_ref[p])
        qlat_ref[0, :, p * pair_out:(p + 1) * pair_out] = (ql * (ATT_HEAD_DIM ** -0.5 * LOG2_E)).astype(BF16)

    pm = _dot(xb, wm_s[...])
    qk_raw = pm[:, :MLQK_WIDTH]
    v_ref[0] = pm[:, MLQK_WIDTH:MLQK_WIDTH + ML_WIDTH].astype(BF16)
    og_ref[0] = _sigmoid(pm[:, MLQK_WIDTH + ML_WIDTH:])

    ext = jnp.concatenate([carry_ref[...], qk_raw], axis=0)
    cw = convw_ref[...]
    conv = jnp.zeros_like(qk_raw)
    for j in range(CONV_WIDTH):
        s0 = CONV_HIST - (CONV_WIDTH - 1) + j
        conv = conv + ext[s0:s0 + tm] * cw[j:j + 1]
    act = conv * _sigmoid(conv)
    half = MLQK_WIDTH // 2
    qk_ref[0, :, :half] = act[:, :half].astype(BF16)
    qk_ref[0, :, half:] = (act[:, half:] * (ML_QK_DIM ** -0.5)).astype(BF16)
    carry_ref[...] = qk_raw[tm - CONV_HIST:]
    tailout_ref[0] = qk_raw[tm - CONV_HIST:]

    gr = pa[:, c2 + IDX_DIM - GATE_W0:]
    lane = lax.broadcasted_iota(jnp.int32, gr.shape, 1)
    sc = GATE_SOFTCAP * jnp.tanh((gr + gbias_ref[...]) / GATE_SOFTCAP)
    lf = -(jnp.maximum(-sc, 0.0) + jnp.log1p(jnp.exp(-jnp.abs(sc))))
    w_scaled = gr * (IDX_HEADS ** -0.5 * IDX_DIM ** -0.5)
    gates = jnp.where((lane < GATE_W0) | (lane >= GATE_END), 0.0,
                      jnp.where(lane < GATE_I0, w_scaled, jnp.where(lane < GATE_F0, sc, lf)))
    gates_ref[0] = gates
    if len(rest) == 6:
        gt_ref, ckvt_ref, vt_ref = rest[:3]
        gates_t = gates.T[GATE_W0:GATE_W0 + GATE_ROWS]
        ckv_t = ckv.T.astype(BF16)
        v_t = pm[:, MLQK_WIDTH:MLQK_WIDTH + ML_WIDTH].T.astype(BF16)
        for j in range(tm // GT_CHUNK):
            piece = slice(j * GT_CHUNK, (j + 1) * GT_CHUNK)
            gt_ref[0, j] = gates_t[:, piece]
            ckvt_ref[0, j] = ckv_t[:, piece]
            vt_ref[0, j] = v_t[:, piece]


def _inproj_call(h, tail, wa, wm, wuk_bd, kvg, convw, gbias, tm):
    bn, rows, d = h.shape
    nblk = rows // tm
    emit_gt = tm % GT_CHUNK == 0

    def row_spec(width):
        return pl.BlockSpec((1, tm, width), lambda b, j: (b, j, 0))

    outs = [
        (LAT_WIDTH, BF16), (KV_LATENT, BF16), (IDX_WIDTH, BF16), (IDX_DIM, BF16),
        (MLQK_WIDTH, BF16), (ML_WIDTH, BF16), (ML_WIDTH, F32), (LANES, F32),
    ]
    out_shape = [jax.ShapeDtypeStruct((bn, rows, w), dt) for w, dt in outs]
    out_specs = [row_spec(w) for w, _ in outs]
    out_shape.append(jax.ShapeDtypeStruct((bn, CONV_HIST, MLQK_WIDTH), F32))
    out_specs.append(pl.BlockSpec((1, CONV_HIST, MLQK_WIDTH), lambda b, j: (b, 0, 0)))
    if emit_gt:
        per_tile = tm // GT_CHUNK
        for height, dt in ((GATE_ROWS, F32), (KV_LATENT, BF16), (ML_WIDTH, BF16)):
            out_shape.append(jax.ShapeDtypeStruct((bn, rows // GT_CHUNK, height, GT_CHUNK), dt))
            out_specs.append(pl.BlockSpec((1, per_tile, height, GT_CHUNK), lambda b, j: (b, j, 0, 0)))
    return pl.pallas_call(
        _inproj_kernel,
        grid=(bn, nblk),
        in_specs=[
            row_spec(d),
            _resident(tail.shape), _resident(wa.shape), _resident(wm.shape),
            _resident(wuk_bd.shape), _resident(kvg.shape), _resident(convw.shape), _resident(gbias.shape),
        ],
        out_specs=out_specs,
        out_shape=out_shape,
        scratch_shapes=[pltpu.VMEM((CONV_HIST, MLQK_WIDTH), F32),
                        pltpu.VMEM(wa.shape[::-1], BF16), pltpu.VMEM(wm.shape[::-1], BF16)],
        compiler_params=pltpu.CompilerParams(
            dimension_semantics=("arbitrary", "arbitrary"), vmem_limit_bytes=VMEM_LIMIT_BYTES),
        name="in_proj",
    )(h, tail, wa, wm, wuk_bd, kvg, convw, gbias)


def _dsa_kernel(qlat_ref, qidx_ref, wrow_ref, ckv_ref, ckvt_ref, kidx_ref, mckv_ref, mckvt_ref, mkidx_ref,
                wuv_ref, y_ref, s_ref, acc_ref, p_ref, *, topk):
    _, kc, tq = s_ref.shape
    i = pl.program_id(1)
    nch = ((i + 1) * tq + kc - 1) // kc
    qreal = i * tq + lax.broadcasted_iota(jnp.int32, (1, tq), 1)
    kf = float(topk)

    wrow = wrow_ref[0, 0]
    qidx = qidx_ref[0]
    q_idx_all = jnp.concatenate([qidx[:, h * IDX_DIM:(h + 1) * IDX_DIM] for h in range(IDX_HEADS)], axis=0)
    wi = [wrow[h:h + 1, :] for h in range(IDX_HEADS)]

    def scores(k_rows):
        lg = _dot_nt(k_rows, q_idx_all)
        sc = jnp.zeros((k_rows.shape[0], tq), F32)
        for h in range(IDX_HEADS):
            sc = sc + jnp.maximum(lg[:, h * tq:(h + 1) * tq], 0.0) * wi[h]
        return sc

    s_meta = scores(mkidx_ref[...])

    def score_chunk(c, lo, hi):
        sc = scores(kidx_ref[0, c])
        valid = c * kc + lax.broadcasted_iota(jnp.int32, (kc, tq), 0) <= qreal
        s_ref[c] = jnp.where(valid, sc, -jnp.inf)
        groups = (kc // REDUCE_ROWS, REDUCE_ROWS, tq)
        lo = jnp.minimum(lo, jnp.min(sc.reshape(groups), axis=0))
        hi = jnp.maximum(hi, jnp.max(sc.reshape(groups), axis=0))
        return lo, hi

    def score_pair(c2, carry):
        lo, hi = score_chunk(2 * c2, *carry)
        return score_chunk(jnp.minimum(2 * c2 + 1, nch - 1), lo, hi)

    lo, hi = lax.fori_loop(0, (nch + 1) // 2, score_pair,
                           (jnp.full((REDUCE_ROWS, tq), jnp.inf, F32), jnp.full((REDUCE_ROWS, tq), -jnp.inf, F32)))
    lo = jnp.minimum(jnp.min(lo, axis=0, keepdims=True), jnp.min(s_meta, axis=0, keepdims=True))
    hi = jnp.maximum(jnp.max(hi, axis=0, keepdims=True), jnp.max(s_meta, axis=0, keepdims=True))

    def key_reduce(reduce, combine, per_chunk, init):
        def body(c, acc):
            x = per_chunk(s_ref[c]).reshape(kc // REDUCE_ROWS, REDUCE_ROWS, tq)
            return combine(acc, reduce(x, axis=0))
        acc = lax.fori_loop(0, nch, body, jnp.full((REDUCE_ROWS, tq), init, F32))
        return combine(reduce(acc, axis=0, keepdims=True), reduce(per_chunk(s_meta), axis=0, keepdims=True))

    def count(pred):
        return key_reduce(jnp.sum, jnp.add, lambda sc: jnp.where(pred(sc), 1.0, 0.0), 0.0)

    def max_where(pred):
        return key_reduce(jnp.max, jnp.maximum, lambda sc: jnp.where(pred(sc), sc, -jnp.inf), -jnp.inf)

    def bisect(_, carry):
        lo, hi = carry
        mid = 0.5 * lo + 0.5 * hi
        up = count(lambda sc: sc > mid) >= kf
        return jnp.where(up, mid, lo), jnp.where(up, hi, mid)

    lo, hi = lax.fori_loop(0, N_BISECT, bisect, (lo, hi))

    n_valid = (qreal + (N_META + 1)).astype(F32)
    small = n_valid <= kf
    cand = max_where(lambda sc: sc <= hi)
    n_ge = count(lambda sc: sc >= cand)
    done = jnp.where(small | (n_ge >= kf), 1.0, 0.0)

    def not_finished(state):
        return jnp.min(state[1]) < 0.5

    def step_down(state):
        cand, done, _ = state
        nxt = jnp.where(done > 0.5, cand, max_where(lambda sc: sc < cand))
        n_ge = count(lambda sc: sc >= nxt)
        return nxt, jnp.where(n_ge >= kf, 1.0, done), n_ge

    cand, _, n_ge = lax.while_loop(not_finished, step_down, (cand, done, n_ge))
    thr = jnp.where(small, -jnp.inf, cand)
    n_eq = count(lambda sc: sc == thr)
    need = jnp.where(small, 0.0, kf - (n_ge - n_eq))
    ranked_ties = jnp.max(jnp.where(n_eq > need, 1.0, 0.0)) > 0.5

    qlat = qlat_ref[0]
    n_groups = tq // ATT_GROUP
    onehot = (lax.broadcasted_iota(jnp.int32, (ATT_GROUP, ATT_GROUP), 0)
              == lax.broadcasted_iota(jnp.int32, (ATT_GROUP, ATT_GROUP), 1)).astype(BF16)
    q_aug = [jnp.concatenate(
        [jnp.concatenate([qlat[g * ATT_GROUP:(g + 1) * ATT_GROUP, h * KV_LATENT:(h + 1) * KV_LATENT], onehot], axis=1)
         for h in range(ATT_HEADS)], axis=0) for g in range(n_groups)]
    hq = ATT_HEADS * ATT_GROUP

    def lower_tri(n):
        return (lax.broadcasted_iota(jnp.int32, (n, n), 1) <= lax.broadcasted_iota(jnp.int32, (n, n), 0)).astype(BF16)

    def attention(ranked):
        def mask_bias(sc, eq_seen):
            if not ranked:
                return jnp.where(sc >= thr, 0.0, NEG_BIG).astype(BF16), eq_seen
            n = sc.shape[0]
            eq = sc == thr
            rank = _dot(lower_tri(n), jnp.where(eq, 1.0, 0.0).astype(BF16)) + eq_seen
            keep = (sc > thr) | (eq & (rank <= need))
            return jnp.where(keep, 0.0, NEG_BIG).astype(BF16), rank[n - 1:n, :]

        def logits(g, kv, bias):
            k_aug = jnp.concatenate([kv, bias[:, g * ATT_GROUP:(g + 1) * ATT_GROUP]], axis=1)
            return _dot_nt(k_aug, q_aug[g])

        def fold_in(g, c_prev, a_prev):
            acc_ref[g] = a_prev * acc_ref[g] + _dot(ckvt_ref[0, c_prev], p_ref[g])

        def attend(c, carry):
            bias, eq_seen = mask_bias(s_ref[c], carry[0])
            out = [eq_seen]
            for g in range(n_groups):
                m, l, a_prev = carry[1 + 3 * g:4 + 3 * g]
                fold_in(g, c - 1, a_prev)
                s = logits(g, ckv_ref[0, c], bias)
                m_new = jnp.maximum(m, jnp.max(s, axis=0, keepdims=True))
                a = jnp.exp2(m - m_new)
                p = jnp.exp2(s - m_new)
                p_ref[g] = p.astype(BF16)
                out += [m_new, a * l + jnp.sum(p, axis=0, keepdims=True), a]
            return tuple(out)

        bias_m, eq_seen = mask_bias(s_meta, jnp.zeros((1, tq), F32))
        bias_0, eq_seen = mask_bias(s_ref[0], eq_seen)
        init = [eq_seen]
        for g in range(n_groups):
            s_m = logits(g, mckv_ref[...], bias_m)
            s_0 = logits(g, ckv_ref[0, 0], bias_0)
            m = jnp.maximum(jnp.max(s_m, axis=0, keepdims=True), jnp.max(s_0, axis=0, keepdims=True))
            p_m = jnp.exp2(s_m - m)
            p_0 = jnp.exp2(s_0 - m)
            acc_ref[g] = _dot(mckvt_ref[...], p_m.astype(BF16))
            p_ref[g] = p_0.astype(BF16)
            init += [m, jnp.sum(p_m, axis=0, keepdims=True) + jnp.sum(p_0, axis=0, keepdims=True),
                     jnp.ones((1, hq), F32)]
        carry = lax.fori_loop(1, nch, attend, tuple(init))
        rows = []
        for g in range(n_groups):
            _, l, a_last = carry[1 + 3 * g:4 + 3 * g]
            fold_in(g, nch - 1, a_last)
            o_t = (acc_ref[g] / l).T
            rows.append(jnp.concatenate([o_t[h * ATT_GROUP:(h + 1) * ATT_GROUP] for h in range(ATT_HEADS)], axis=1))
        return _dot(jnp.concatenate(rows, axis=0).astype(BF16), wuv_ref[...]).astype(BF16)

    y_ref[0] = lax.cond(ranked_ties, lambda: attention(True), lambda: attention(False))


def _dsa_call(qlat, qidx, wrow, ckv_c, ckvt_c, kidx_c, m_ckv, m_ckvt, m_kidx, wuv_bd, topk):
    bn, rows, _ = qlat.shape
    nchunks, kc = ckv_c.shape[1], ckv_c.shape[2]
    nq = rows // Q_TILE

    def q_spec(width):
        return pl.BlockSpec((1, Q_TILE, width), lambda b, i: (b, i, 0))

    def k_spec(a):
        return pl.BlockSpec((1,) + a.shape[1:], lambda b, i: (b, 0, 0, 0))

    return pl.pallas_call(
        functools.partial(_dsa_kernel, topk=topk),
        grid=(bn, nq),
        in_specs=[q_spec(LAT_WIDTH), q_spec(IDX_WIDTH),
                  pl.BlockSpec((1, 1, GATE_ROWS, Q_TILE), lambda b, i: (b, i, 0, 0)),
                  k_spec(ckv_c), k_spec(ckvt_c), k_spec(kidx_c),
                  _resident(m_ckv.shape), _resident(m_ckvt.shape), _resident(m_kidx.shape), _resident(wuv_bd.shape)],
        out_specs=q_spec(ATT_WIDTH),
        out_shape=jax.ShapeDtypeStruct((bn, rows, ATT_WIDTH), BF16),
        scratch_shapes=[pltpu.VMEM((nchunks, kc, Q_TILE), F32),
                        pltpu.VMEM((Q_TILE // ATT_GROUP, KV_LATENT, ATT_HEADS * ATT_GROUP), F32),
                        pltpu.VMEM((Q_TILE // ATT_GROUP, kc, ATT_HEADS * ATT_GROUP), BF16)],
        compiler_params=pltpu.CompilerParams(
            dimension_semantics=("arbitrary", "arbitrary"), vmem_limit_bytes=VMEM_LIMIT_BYTES),
        name="dsa",
    )(qlat, qidx, wrow, ckv_c, ckvt_c, kidx_c, m_ckv, m_ckvt, m_kidx, wuv_bd)


def _split3(x):
    hi = x.astype(BF16)
    r = x - hi.astype(F32)
    mid = r.astype(BF16)
    lo = (r - mid.astype(F32)).astype(BF16)
    return hi, mid, lo


ML_EXT = ML_V_DIM + 16


def _mlstm_chunk(qk, vt, g, gt, state):
    L = qk.shape[0]
    s_idx = lax.broadcasted_iota(jnp.int32, (L, L), 0)
    t_idx = lax.broadcasted_iota(jnp.int32, (L, L), 1)
    causal = s_idx <= t_idx
    b_cols = sum(_dot((t_idx <= s_idx).astype(BF16), part) for part in _split3(g))
    b_rows = sum(_dot(part, causal.astype(BF16)) for part in _split3(gt))
    ones_blk = jnp.where(lax.broadcasted_iota(jnp.int32, (ML_EXT - ML_V_DIM, L), 0) == 0, 1.0, 0.0).astype(BF16)
    kq = ML_HEADS * ML_QK_DIM

    outs, new_state = [], []
    for h in range(ML_HEADS):
        ce, m_prev = state[h]
        c_col = g[:, GATE_I0 + h:GATE_I0 + h + 1] - b_cols[:, GATE_F0 + h:GATE_F0 + h + 1]
        b_row = b_rows[GATE_F0 - GATE_W0 + h:GATE_F0 - GATE_W0 + h + 1, :]
        ig_row = gt[GATE_I0 - GATE_W0 + h:GATE_I0 - GATE_W0 + h + 1, :]
        qh = qk[:, h * ML_QK_DIM:(h + 1) * ML_QK_DIM]
        kh = qk[:, kq + h * ML_QK_DIM:kq + (h + 1) * ML_QK_DIM]
        vt_ext = jnp.concatenate([vt[h * ML_V_DIM:(h + 1) * ML_V_DIM, :], ones_blk], axis=0)

        d_t = jnp.where(causal, c_col + b_row, -jnp.inf)
        inter = b_row + m_prev
        m_t = jnp.maximum(jnp.max(d_t, axis=0, keepdims=True), inter)
        w_inter = jnp.exp(inter - m_t)
        s_t = _dot_nt(kh, qh) * jnp.exp(d_t - m_t)
        r = _dot(vt_ext, s_t.astype(BF16)) + _dot_nt(ce.astype(BF16), qh) * w_inter
        num = r[:ML_V_DIM]
        den = r[ML_V_DIM:ML_V_DIM + 1]
        hh = num / jnp.maximum(jnp.abs(den), jnp.exp(-m_t))
        mu = jnp.mean(hh, axis=0, keepdims=True)
        hc = hh - mu
        var = jnp.mean(hc * hc, axis=0, keepdims=True)
        outs.append((hc * lax.rsqrt(var + LN_EPS)).T)

        b_end = b_row[:, L - 1:L]
        g_row = b_end - b_row + ig_row
        m_new = jnp.maximum(b_end + m_prev, jnp.max(g_row, axis=1, keepdims=True))
        decay = jnp.exp(b_end + m_prev - m_new)
        weighted = (vt_ext.astype(F32) * jnp.exp(g_row - m_new)).astype(BF16)
        new_state.append((decay * ce + _dot(weighted, kh), m_new))
    return outs, new_state


def _mlstm_kernel(qk_ref, vt_ref, og_ref, gates_ref, gt_ref, mqk_ref, mvt_ref, mgates_ref, mgt_ref, ng_ref, y_ref,
                  ce0_ref, m0_ref):
    L = mqk_ref.shape[0]
    n_chunks = qk_ref.shape[1] // L
    norm_g = ng_ref[...]

    @pl.when(pl.program_id(0) == 0)
    def _():
        state = [(jnp.zeros((ML_EXT, ML_QK_DIM), F32), jnp.full((1, 1), M_INIT, F32)) for _ in range(ML_HEADS)]
        _, state = _mlstm_chunk(mqk_ref[...], mvt_ref[...], mgates_ref[...], mgt_ref[...], state)
        for h in range(ML_HEADS):
            ce0_ref[h] = state[h][0]
            m0_ref[h] = jnp.broadcast_to(state[h][1], m0_ref.shape[1:])

    n_b = qk_ref.shape[0]
    state = [(ce0_ref[h], m0_ref[h][0:1, 0:1]) for h in range(ML_HEADS)] * n_b

    def body(c, flat):
        rows = pl.ds(pl.multiple_of(c * L, L), L)
        new_flat = []
        for b in range(n_b):
            state = [(flat[2 * (b * ML_HEADS + h)], flat[2 * (b * ML_HEADS + h) + 1]) for h in range(ML_HEADS)]
            outs, state = _mlstm_chunk(qk_ref[b, rows, :], vt_ref[b, c], gates_ref[b, rows, :], gt_ref[b, c], state)
            y = jnp.concatenate(outs, axis=1) * norm_g * og_ref[b, rows, :]
            y_ref[b, rows, :] = y.astype(BF16)
            new_flat += [x for pair in state for x in pair]
        return tuple(new_flat)

    lax.fori_loop(0, n_chunks, body, tuple(x for pair in state for x in pair))


def _mlstm_call(qk, vt_c, og, gates, gt_c, mqk, mvt, mgates, mgt, norm_g):
    bn, rows, _ = qk.shape
    nb = ML_BATCH if bn % ML_BATCH == 0 else 1

    def b_spec(a):
        return pl.BlockSpec((nb,) + a.shape[1:], lambda b: (b,) + (0,) * (a.ndim - 1))

    consts = (mqk, mvt, mgates, mgt, norm_g)
    return pl.pallas_call(
        _mlstm_kernel,
        grid=(bn // nb,),
        in_specs=[b_spec(qk), b_spec(vt_c), b_spec(og), b_spec(gates), b_spec(gt_c)]
        + [_resident(c.shape) for c in consts],
        out_specs=pl.BlockSpec((nb, rows, ML_WIDTH), lambda b: (b, 0, 0)),
        out_shape=jax.ShapeDtypeStruct((bn, rows, ML_WIDTH), BF16),
        scratch_shapes=[pltpu.VMEM((ML_HEADS, ML_EXT, ML_QK_DIM), F32), pltpu.VMEM((ML_HEADS, SUBLANES, LANES), F32)],
        compiler_params=pltpu.CompilerParams(
            dimension_semantics=("arbitrary",), vmem_limit_bytes=VMEM_LIMIT_BYTES),
        name="mlstm",
    )(qk, vt_c, og, gates, gt_c, *consts)


def _out_ffn_ln_kernel(ya_ref, ym_ref, h_ref, wo_ref, g2_ref, b2_ref, wg_ref, wu_ref, wd_ref, g3_ref, b3_ref,
                       o_ref, wg_s, wu_s, wd_s, h2_ref, acc0_ref, *, alpha, n_stage):
    i = pl.program_id(0)

    def mixed():
        mix = _dot(ya_ref[...], wo_ref[:ATT_WIDTH, :]) + _dot(ym_ref[...], wo_ref[ATT_WIDTH:, :])
        return _layer_norm(alpha * h_ref[...] + mix, g2_ref[...], b2_ref[...])

    @pl.when(i == 0)
    def _():
        h2_ref[...] = mixed()
        acc0_ref[...] = jnp.zeros(acc0_ref.shape, F32)

    @pl.when(i < n_stage)
    def _():
        _stage_ffn_weights(i, wg_ref, wu_ref, wd_ref, wg_s, wu_s, wd_s)
        acc0_ref[...] += _swiglu_chunk(h2_ref[...].astype(BF16), wg_s[i], wu_s[i], wd_s[i])

    @pl.when(i == n_stage - 1)
    def _():
        o_ref[...] = _layer_norm(alpha * h2_ref[...] + 0.5 * acc0_ref[...], g3_ref[...], b3_ref[...])

    @pl.when(i >= n_stage)
    def _():
        o_ref[...] = _ffn_ln(mixed(), wg_s, wu_s, wd_s, g3_ref[...], b3_ref[...], alpha)


def _out_ffn_ln_call(ya, ym, h, wo, g2, b2, wg, wu, wd, g3, b3, alpha, tm):
    rows, d = h.shape
    n_stage, w_specs, w_scratch = _ffn_weight_specs(d, wg.shape[1])

    def row_spec(width):
        return pl.BlockSpec((tm, width), lambda i: (_staged_tile_index(i, n_stage), 0))

    return pl.pallas_call(
        functools.partial(_out_ffn_ln_kernel, alpha=alpha, n_stage=n_stage),
        grid=(n_stage - 1 + rows // tm,),
        in_specs=[row_spec(ya.shape[1]), row_spec(ym.shape[1]), row_spec(d),
                  _resident(wo.shape), _resident(g2.shape), _resident(b2.shape)] + w_specs
        + [_resident(g3.shape), _resident(b3.shape)],
        out_specs=row_spec(d),
        out_shape=jax.ShapeDtypeStruct((rows, d), F32),
        scratch_shapes=w_scratch + [pltpu.VMEM((tm, d), F32), pltpu.VMEM((tm, d), F32)],
        compiler_params=pltpu.CompilerParams(
            dimension_semantics=("arbitrary",), vmem_limit_bytes=VMEM_LIMIT_BYTES),
        name="out_ffn_ln",
    )(ya, ym, h, wo, g2, b2, wg, wu, wd, g3, b3)


def _block_diag(w):
    nh, a, b = w.shape
    eye = jnp.eye(nh, dtype=w.dtype)
    return (eye[:, None, :, None] * w[:, :, None, :]).reshape(nh * a, nh * b)


def _pad_rows(a, rows, value=0.0):
    return jnp.pad(a, ((0, rows - a.shape[0]), (0, 0)), constant_values=value)


def kernel(x, meta_tokens, ln1_g, ln1_b, ffn1_w_gate, ffn1_w_up, ffn1_w_down, w_in, w_uk, w_uv, kv_norm_g,
           conv_w, b_igate, b_fgate, ml_norm_g, w_out, ln2_g, ln2_b, ffn2_w_gate, ffn2_w_up, ffn2_w_down,
           ln3_g, ln3_b):
    depth = ln1_g.shape[0]
    assert depth == 1, "the meta-token shortcut below is only valid for a single layer"
    bsz, seq, d = x.shape
    assert seq % ROW_TILE == 0 and seq % ML_CHUNK == 0 and seq % Q_TILE == 0
    assert Q_TILE == GT_CHUNK and ML_CHUNK == GT_CHUNK and KEY_CHUNK == GT_CHUNK and ROW_TILE % GT_CHUNK == 0
    alpha = (2 * depth) ** 0.25
    topk = min(TOPK_MAX, seq // 4)

    row2 = lambda p: p[0].reshape(1, -1).astype(F32)
    bf = lambda w: w[0].astype(BF16)

    w_t = jnp.swapaxes(w_in[0], 0, 1)
    o_qa, o_ckv, o_qi, o_ki, o_wi = 0, 512, 640, 896, 960
    o_qk, o_v, o_o, o_ig, o_fg, o_end = 964, 1476, 1988, 2500, 2504, 2508
    wa = jnp.concatenate([w_t[o_qa:o_wi], w_t[o_wi:o_qk], w_t[o_ig:o_end],
                          jnp.zeros((LANES - GATE_END, d), F32)], axis=0).astype(BF16)
    wm = w_t[o_qk:o_ig].astype(BF16)
    gbias = jnp.concatenate([jnp.zeros((GATE_I0,), F32), b_igate[0], b_fgate[0],
                             jnp.zeros((LANES - GATE_END,), F32)]).reshape(1, LANES)
    wuk_bd = jnp.stack([_block_diag(w_uk[0][2 * p:2 * p + 2]) for p in range(HEAD_PAIRS)]).astype(BF16)
    wuv_bd = _block_diag(w_uv[0]).astype(BF16)
    kvg = row2(kv_norm_g)
    convw = conv_w[0].astype(F32)

    h1, h1_meta = _ffn_ln_call(x.reshape(bsz * seq, d), meta_tokens.astype(F32), ffn1_w_gate[0], ffn1_w_up[0],
                               ffn1_w_down[0], row2(ln1_g), row2(ln1_b), alpha, FFN_TILE)
    zero_tail = jnp.zeros((CONV_HIST, MLQK_WIDTH), F32)
    (_, m_ckv, _, m_kidx, m_qk, m_v, _, m_gates, m_tail) = _inproj_call(
        h1_meta[None], zero_tail, wa, wm, wuk_bd, kvg, convw, gbias, N_META)

    (qlat, ckv, qidx, kidx, qk, _, og, gates, _, gates_t, ckv_t, v_t) = _inproj_call(
        h1.reshape(bsz, seq, d), m_tail[0], wa, wm, wuk_bd, kvg, convw, gbias, ROW_TILE)

    nchunks = seq // KEY_CHUNK
    ckv_c = ckv.reshape(bsz, nchunks, KEY_CHUNK, KV_LATENT)
    kidx_c = kidx.reshape(bsz, nchunks, KEY_CHUNK, IDX_DIM)
    y_att = _dsa_call(qlat, qidx, gates_t, ckv_c, ckv_t, kidx_c,
                      m_ckv[0], m_ckv[0].T, m_kidx[0], wuv_bd, topk)

    lane = jnp.arange(LANES)
    pad_gate = jnp.where((lane >= GATE_I0) & (lane < GATE_F0), NEG_BIG, 0.0).astype(F32)
    mg = jnp.concatenate([m_gates[0], jnp.broadcast_to(pad_gate, (ML_CHUNK - N_META, LANES))], axis=0)
    gate_lanes = slice(GATE_W0, GATE_W0 + GATE_ROWS)
    y_ml = _mlstm_call(qk, v_t, og, gates, gates_t,
                       _pad_rows(m_qk[0], ML_CHUNK), _pad_rows(m_v[0], ML_CHUNK).T, mg, mg[:, gate_lanes].T,
                       row2(ml_norm_g))

    out = _out_ffn_ln_call(
        y_att.reshape(bsz * seq, ATT_WIDTH), y_ml.reshape(bsz * seq, ML_WIDTH), h1, bf(w_out),
        row2(ln2_g), row2(ln2_b), ffn2_w_gate[0], ffn2_w_up[0], ffn2_w_down[0], row2(ln3_g), row2(ln3_b),
        alpha, FFN_TILE)
    return out.reshape(bsz, seq, d)
```

```python
import functools

import jax
import jax.numpy as jnp
from jax import lax
from jax.experimental import pallas as pl
from jax.experimental.pallas import tpu as pltpu

F32 = jnp.float32
BF16 = jnp.bfloat16

N_META = 16
ATT_HEADS = 8
ATT_HEAD_DIM = 64
KV_LATENT = 128
IDX_HEADS = 4
IDX_DIM = 64
TOPK_MAX = 256
ML_HEADS = 4
ML_V_DIM = 128
ML_QK_DIM = 64
CONV_WIDTH = 4
GATE_SOFTCAP = 15.0
M_INIT = -1e30
LN_EPS = 1e-5
NEG_BIG = -1e30
LOG2_E = 1.4426950408889634

LANES = 128
SUBLANES = 8
VMEM_LIMIT_BYTES = 56 * 1024 * 1024

FF_CHUNK = 256
ROW_TILE = 512
FFN_TILE = 512
Q_TILE = 256
ATT_GROUP = 128
KEY_CHUNK = 256
N_BISECT = 16
REDUCE_ROWS = 32
ML_BATCH = 2
ML_CHUNK = 256


def _dot(a, b):
    return jnp.dot(a, b, preferred_element_type=F32)


def _dot_nt(a, b):
    return lax.dot_general(a, b, (((1,), (1,)), ((), ())), preferred_element_type=F32)


def _layer_norm(z, g, b):
    mu = jnp.mean(z, axis=-1, keepdims=True)
    zc = z - mu
    var = jnp.mean(zc * zc, axis=-1, keepdims=True)
    return zc * lax.rsqrt(var + LN_EPS) * g + b


def _sigmoid(x):
    return 1.0 / (1.0 + jnp.exp(-x))


def _swiglu_chunk(xb, wg_c, wu_c, wd_c):
    g = _dot(xb, wg_c)
    u = _dot(xb, wu_c)
    return _dot((g * _sigmoid(g) * u).astype(BF16), wd_c)


def _ffn_ln(x, wg_s, wu_s, wd_s, g, b, alpha):
    xb = x.astype(BF16)
    acc = jnp.zeros(x.shape, F32)
    for c in range(wg_s.shape[0]):
        acc = acc + _swiglu_chunk(xb, wg_s[c], wu_s[c], wd_s[c])
    return _layer_norm(alpha * x + 0.5 * acc, g, b)


def _stage_ffn_weights(step, wg_ref, wu_ref, wd_ref, wg_s, wu_s, wd_s):
    wg_s[step] = wg_ref[...].astype(BF16)
    wu_s[step] = wu_ref[...].astype(BF16)
    wd_s[step] = wd_ref[...].astype(BF16)


def _ffn_weight_specs(d, d_ff):
    n = d_ff // FF_CHUNK
    col = pl.BlockSpec((d, FF_CHUNK), lambda i: (0, jnp.minimum(i, n - 1)))
    row = pl.BlockSpec((FF_CHUNK, d), lambda i: (jnp.minimum(i, n - 1), 0))
    scratch = [pltpu.VMEM((n, d, FF_CHUNK), BF16), pltpu.VMEM((n, d, FF_CHUNK), BF16),
               pltpu.VMEM((n, FF_CHUNK, d), BF16)]
    return n, [col, col, row], scratch


def _staged_tile_index(i, n_stage):
    return jnp.maximum(i - (n_stage - 1), 0)


def _ffn_ln_kernel(x_ref, meta_ref, wg_ref, wu_ref, wd_ref, g_ref, b_ref, o_ref, ometa_ref,
                   wg_s, wu_s, wd_s, macc_ref, acc0_ref, *, alpha, n_stage):
    i = pl.program_id(0)

    @pl.when(i == 0)
    def _():
        macc_ref[...] = jnp.zeros(macc_ref.shape, F32)
        acc0_ref[...] = jnp.zeros(acc0_ref.shape, F32)

    @pl.when(i < n_stage)
    def _():
        _stage_ffn_weights(i, wg_ref, wu_ref, wd_ref, wg_s, wu_s, wd_s)
        macc_ref[...] += _swiglu_chunk(meta_ref[...].astype(BF16), wg_s[i], wu_s[i], wd_s[i])
        acc0_ref[...] += _swiglu_chunk(x_ref[...].astype(BF16), wg_s[i], wu_s[i], wd_s[i])

    @pl.when(i == n_stage - 1)
    def _():
        ometa_ref[...] = _layer_norm(alpha * meta_ref[...] + 0.5 * macc_ref[...], g_ref[...], b_ref[...])
        o_ref[...] = _layer_norm(alpha * x_ref[...] + 0.5 * acc0_ref[...], g_ref[...], b_ref[...])

    @pl.when(i >= n_stage)
    def _():
        o_ref[...] = _ffn_ln(x_ref[...], wg_s, wu_s, wd_s, g_ref[...], b_ref[...], alpha)


def _resident(shape):
    return pl.BlockSpec(shape, lambda *_: (0,) * len(shape), pipeline_mode=pl.Buffered(1))


def _ffn_ln_call(x, meta, wg, wu, wd, g, b, alpha, tm):
    rows, d = x.shape
    n_stage, w_specs, w_scratch = _ffn_weight_specs(d, wg.shape[1])
    row_spec = pl.BlockSpec((tm, d), lambda i: (_staged_tile_index(i, n_stage), 0))
    return pl.pallas_call(
        functools.partial(_ffn_ln_kernel, alpha=alpha, n_stage=n_stage),
        grid=(n_stage - 1 + rows // tm,),
        in_specs=[row_spec, _resident(meta.shape)] + w_specs + [_resident(g.shape), _resident(b.shape)],
        out_specs=[row_spec, pl.BlockSpec(meta.shape, lambda i: (0, 0))],
        out_shape=[jax.ShapeDtypeStruct((rows, d), F32), jax.ShapeDtypeStruct(meta.shape, F32)],
        scratch_shapes=w_scratch + [pltpu.VMEM(meta.shape, F32), pltpu.VMEM((tm, d), F32)],
        compiler_params=pltpu.CompilerParams(
            dimension_semantics=("arbitrary",), vmem_limit_bytes=VMEM_LIMIT_BYTES),
        name="ffn_ln",
    )(x, meta, wg, wu, wd, g, b)


ATT_WIDTH = ATT_HEADS * ATT_HEAD_DIM
IDX_WIDTH = IDX_HEADS * IDX_DIM
MLQK_WIDTH = 2 * ML_HEADS * ML_QK_DIM
ML_WIDTH = ML_HEADS * ML_V_DIM
LAT_WIDTH = ATT_HEADS * KV_LATENT
CONV_HIST = SUBLANES
GATE_W0 = IDX_DIM
GATE_I0, GATE_F0, GATE_END = GATE_W0 + IDX_HEADS, GATE_W0 + IDX_HEADS + ML_HEADS, GATE_W0 + IDX_HEADS + 2 * ML_HEADS
GATE_ROWS = 2 * SUBLANES
GT_CHUNK = 256
HEAD_PAIRS = ATT_HEADS // 2


def _inproj_kernel(h_ref, tail_ref, wa_ref, wm_ref, wuk_ref, kvg_ref, convw_ref, gbias_ref,
                   qlat_ref, ckv_ref, qidx_ref, kidx_ref, qk_ref, v_ref, og_ref, gates_ref, tailout_ref, *rest):
    carry_ref, wa_s, wm_s = rest[-3:]
    tm = h_ref.shape[1]

    @pl.when((pl.program_id(0) == 0) & (pl.program_id(1) == 0))
    def _():
        wa_s[...] = wa_ref[...].astype(F32).T.astype(BF16)
        wm_s[...] = wm_ref[...].astype(F32).T.astype(BF16)

    @pl.when(pl.program_id(1) == 0)
    def _():
        carry_ref[...] = tail_ref[...]

    xb = h_ref[0].astype(BF16)

    pa = _dot(xb, wa_s[...])
    q_a = pa[:, :ATT_WIDTH].astype(BF16)
    c0 = ATT_WIDTH
    ckv = pa[:, c0:c0 + KV_LATENT]
    c1 = c0 + KV_LATENT
    ckv = ckv * lax.rsqrt(jnp.mean(ckv * ckv, axis=-1, keepdims=True) + LN_EPS) * kvg_ref[...]
    ckv_ref[0] = ckv.astype(BF16)
    qidx_ref[0] = pa[:, c1:c1 + IDX_WIDTH].astype(BF16)
    c2 = c1 + IDX_WIDTH
    kidx_ref[0] = pa[:, c2:c2 + IDX_DIM].astype(BF16)
    pair_in, pair_out = 2 * ATT_HEAD_DIM, 2 * KV_LATENT
    for p in range(HEAD_PAIRS):
        ql = _dot(q_a[:, p * pair_in:(p + 1) * pair_in], wuk_ref[p])
        qlat_ref[0, :, p * pair_out:(p + 1) * pair_out] = (ql * (ATT_HEAD_DIM ** -0.5 * LOG2_E)).astype(BF16)

    pm = _dot(xb, wm_s[...])
    qk_raw = pm[:, :MLQK_WIDTH]
    v_ref[0] = pm[:, MLQK_WIDTH:MLQK_WIDTH + ML_WIDTH].astype(BF16)
    og_ref[0] = _sigmoid(pm[:, MLQK_WIDTH + ML_WIDTH:])

    ext = jnp.concatenate([carry_ref[...], qk_raw], axis=0)
    cw = convw_ref[...]
    conv = jnp.zeros_like(qk_raw)
    for j in range(CONV_WIDTH):
        s0 = CONV_HIST - (CONV_WIDTH - 1) + j
        conv = conv + ext[s0:s0 + tm] * cw[j:j + 1]
    act = conv * _sigmoid(conv)
    half = MLQK_WIDTH // 2
    qk_ref[0, :, :half] = act[:, :half].astype(BF16)
    qk_ref[0, :, half:] = (act[:, half:] * (ML_QK_DIM ** -0.5)).astype(BF16)
    carry_ref[...] = qk_raw[tm - CONV_HIST:]
    tailout_ref[0] = qk_raw[tm - CONV_HIST:]

    gr = pa[:, c2 + IDX_DIM - GATE_W0:]
    lane = lax.broadcasted_iota(jnp.int32, gr.shape, 1)
    sc = GATE_SOFTCAP * jnp.tanh((gr + gbias_ref[...]) / GATE_SOFTCAP)
    lf = -(jnp.maximum(-sc, 0.0) + jnp.log1p(jnp.exp(-jnp.abs(sc))))
    w_scaled = gr * (IDX_HEADS ** -0.5 * IDX_DIM ** -0.5)
    gates = jnp.where((lane < GATE_W0) | (lane >= GATE_END), 0.0,
                      jnp.where(lane < GATE_I0, w_scaled, jnp.where(lane < GATE_F0, sc, lf)))
    gates_ref[0] = gates
    if len(rest) == 6:
        gt_ref, ckvt_ref, vt_ref = rest[:3]
        gates_t = gates.T[GATE_W0:GATE_W0 + GATE_ROWS]
        ckv_t = ckv.T.astype(BF16)
        v_t = pm[:, MLQK_WIDTH:MLQK_WIDTH + ML_WIDTH].T.astype(BF16)
        for j in range(tm // GT_CHUNK):
            piece = slice(j * GT_CHUNK, (j + 1) * GT_CHUNK)
            gt_ref[0, j] = gates_t[:, piece]
            ckvt_ref[0, j] = ckv_t[:, piece]
            vt_ref[0, j] = v_t[:, piece]


def _inproj_call(h, tail, wa, wm, wuk_bd, kvg, convw, gbias, tm):
    bn, rows, d = h.shape
    nblk = rows // tm
    emit_gt = tm % GT_CHUNK == 0

    def row_spec(width):
        return pl.BlockSpec((1, tm, width), lambda b, j: (b, j, 0))

    outs = [
        (LAT_WIDTH, BF16), (KV_LATENT, BF16), (IDX_WIDTH, BF16), (IDX_DIM, BF16),
        (MLQK_WIDTH, BF16), (ML_WIDTH, BF16), (ML_WIDTH, F32), (LANES, F32),
    ]
    out_shape = [jax.ShapeDtypeStruct((bn, rows, w), dt) for w, dt in outs]
    out_specs = [row_spec(w) for w, _ in outs]
    out_shape.append(jax.ShapeDtypeStruct((bn, CONV_HIST, MLQK_WIDTH), F32))
    out_specs.append(pl.BlockSpec((1, CONV_HIST, MLQK_WIDTH), lambda b, j: (b, 0, 0)))
    if emit_gt:
        per_tile = tm // GT_CHUNK
        for height, dt in ((GATE_ROWS, F32), (KV_LATENT, BF16), (ML_WIDTH, BF16)):
            out_shape.append(jax.ShapeDtypeStruct((bn, rows // GT_CHUNK, height, GT_CHUNK), dt))
            out_specs.append(pl.BlockSpec((1, per_tile, height, GT_CHUNK), lambda b, j: (b, j, 0, 0)))
    return pl.pallas_call(
        _inproj_kernel,
        grid=(bn, nblk),
        in_specs=[
            row_spec(d),
            _resident(tail.shape), _resident(wa.shape), _resident(wm.shape),
            _resident(wuk_bd.shape), _resident(kvg.shape), _resident(convw.shape), _resident(gbias.shape),
        ],
        out_specs=out_specs,
        out_shape=out_shape,
        scratch_shapes=[pltpu.VMEM((CONV_HIST, MLQK_WIDTH), F32),
                        pltpu.VMEM(wa.shape[::-1], BF16), pltpu.VMEM(wm.shape[::-1], BF16)],
        compiler_params=pltpu.CompilerParams(
            dimension_semantics=("arbitrary", "arbitrary"), vmem_limit_bytes=VMEM_LIMIT_BYTES),
        name="in_proj",
    )(h, tail, wa, wm, wuk_bd, kvg, convw, gbias)


def _dsa_kernel(qlat_ref, qidx_ref, wrow_ref, ckv_ref, ckvt_ref, kidx_ref, mckv_ref, mckvt_ref, mkidx_ref,
                wuv_ref, y_ref, s_ref, acc_ref, p_ref, *, topk):
    _, kc, tq = s_ref.shape
    i = pl.program_id(1)
    nch = ((i + 1) * tq + kc - 1) // kc
    qreal = i * tq + lax.broadcasted_iota(jnp.int32, (1, tq), 1)
    kf = float(topk)

    wrow = wrow_ref[0, 0]
    qidx = qidx_ref[0]
    q_idx_t = qidx.astype(F32).T
    q_idx_all_t = jnp.concatenate([q_idx_t[h * IDX_DIM:(h + 1) * IDX_DIM] for h in range(IDX_HEADS)],
                                  axis=1).astype(BF16)
    wi = [wrow[h:h + 1, :] for h in range(IDX_HEADS)]

    def scores(k_rows):
        lg = _dot(k_rows, q_idx_all_t)
        sc = jnp.zeros((k_rows.shape[0], tq), F32)
        for h in range(IDX_HEADS):
            sc = sc + jnp.maximum(lg[:, h * tq:(h + 1) * tq], 0.0) * wi[h]
        return sc

    s_meta = scores(mkidx_ref[...])

    def score_chunk(c, lo, hi):
        sc = scores(kidx_ref[0, c])
        valid = c * kc + lax.broadcasted_iota(jnp.int32, (kc, tq), 0) <= qreal
        s_ref[c] = jnp.where(valid, sc, -jnp.inf)
        groups = (kc // REDUCE_ROWS, REDUCE_ROWS, tq)
        lo = jnp.minimum(lo, jnp.min(sc.reshape(groups), axis=0))
        hi = jnp.maximum(hi, jnp.max(sc.reshape(groups), axis=0))
        return lo, hi

    def score_pair(c2, carry):
        lo, hi = score_chunk(2 * c2, *carry)
        return score_chunk(jnp.minimum(2 * c2 + 1, nch - 1), lo, hi)

    lo, hi = lax.fori_loop(0, (nch + 1) // 2, score_pair,
                           (jnp.full((REDUCE_ROWS, tq), jnp.inf, F32), jnp.full((REDUCE_ROWS, tq), -jnp.inf, F32)))
    lo = jnp.minimum(jnp.min(lo, axis=0, keepdims=True), jnp.min(s_meta, axis=0, keepdims=True))
    hi = jnp.maximum(jnp.max(hi, axis=0, keepdims=True), jnp.max(s_meta, axis=0, keepdims=True))

    def key_reduce(reduce, combine, per_chunk, init):
        def body(c, acc):
            x = per_chunk(s_ref[c]).reshape(kc // REDUCE_ROWS, REDUCE_ROWS, tq)
            return combine(acc, reduce(x, axis=0))
        acc = lax.fori_loop(0, nch, body, jnp.full((REDUCE_ROWS, tq), init, F32))
        return combine(reduce(acc, axis=0, keepdims=True), reduce(per_chunk(s_meta), axis=0, keepdims=True))

    def count(pred):
        return key_reduce(jnp.sum, jnp.add, lambda sc: jnp.where(pred(sc), 1.0, 0.0), 0.0)

    def max_where(pred):
        return key_reduce(jnp.max, jnp.maximum, lambda sc: jnp.where(pred(sc), sc, -jnp.inf), -jnp.inf)

    def bisect(_, carry):
        lo, hi = carry
        mid = 0.5 * lo + 0.5 * hi
        up = count(lambda sc: sc > mid) >= kf
        return jnp.where(up, mid, lo), jnp.where(up, hi, mid)

    lo, hi = lax.fori_loop(0, N_BISECT, bisect, (lo, hi))

    n_valid = (qreal + (N_META + 1)).astype(F32)
    small = n_valid <= kf
    cand = max_where(lambda sc: sc <= hi)
    n_ge = count(lambda sc: sc >= cand)
    done = jnp.where(small | (n_ge >= kf), 1.0, 0.0)

    def not_finished(state):
        return jnp.min(state[1]) < 0.5

    def step_down(state):
        cand, done, _ = state
        nxt = jnp.where(done > 0.5, cand, max_where(lambda sc: sc < cand))
        n_ge = count(lambda sc: sc >= nxt)
        return nxt, jnp.where(n_ge >= kf, 1.0, done), n_ge

    cand, _, n_ge = lax.while_loop(not_finished, step_down, (cand, done, n_ge))
    thr = jnp.where(small, -jnp.inf, cand)
    n_eq = count(lambda sc: sc == thr)
    need = jnp.where(small, 0.0, kf - (n_ge - n_eq))
    ranked_ties = jnp.max(jnp.where(n_eq > need, 1.0, 0.0)) > 0.5

    qlat = qlat_ref[0]
    n_groups = tq // ATT_GROUP
    onehot = (lax.broadcasted_iota(jnp.int32, (ATT_GROUP, ATT_GROUP), 0)
              == lax.broadcasted_iota(jnp.int32, (ATT_GROUP, ATT_GROUP), 1)).astype(BF16)
    qlat32 = qlat.astype(F32)
    q_aug_t = [jnp.concatenate(
        [jnp.concatenate([qlat32[g * ATT_GROUP:(g + 1) * ATT_GROUP, h * KV_LATENT:(h + 1) * KV_LATENT].T
                          for h in range(ATT_HEADS)], axis=1).astype(BF16),
         jnp.concatenate([onehot] * ATT_HEADS, axis=1)], axis=0)
        for g in range(n_groups)]
    hq = ATT_HEADS * ATT_GROUP

    def lower_tri(n):
        return (lax.broadcasted_iota(jnp.int32, (n, n), 1) <= lax.broadcasted_iota(jnp.int32, (n, n), 0)).astype(BF16)

    def attention(ranked):
        def mask_bias(sc, eq_seen):
            if not ranked:
                return jnp.where(sc >= thr, 0.0, NEG_BIG).astype(BF16), eq_seen
            n = sc.shape[0]
            eq = sc == thr
            rank = _dot(lower_tri(n), jnp.where(eq, 1.0, 0.0).astype(BF16)) + eq_seen
            keep = (sc > thr) | (eq & (rank <= need))
            return jnp.where(keep, 0.0, NEG_BIG).astype(BF16), rank[n - 1:n, :]

        def logits(g, kv, bias):
            k_aug = jnp.concatenate([kv, bias[:, g * ATT_GROUP:(g + 1) * ATT_GROUP]], axis=1)
            return _dot(k_aug, q_aug_t[g])

        def fold_in(g, c_prev, a_prev):
            acc_ref[g] = a_prev * acc_ref[g] + _dot(ckvt_ref[0, c_prev], p_ref[g])

        def attend(c, carry):
            bias, eq_seen = mask_bias(s_ref[c], carry[0])
            out = [eq_seen]
            for g in range(n_groups):
                m, l, a_prev = carry[1 + 3 * g:4 + 3 * g]
                fold_in(g, c - 1, a_prev)
                s = logits(g, ckv_ref[0, c], bias)
                m_new = jnp.maximum(m, jnp.max(s, axis=0, keepdims=True))
                a = jnp.exp2(m - m_new)
                p = jnp.exp2(s - m_new)
                p_ref[g] = p.astype(BF16)
                out += [m_new, a * l + jnp.sum(p, axis=0, keepdims=True), a]
            return tuple(out)

        bias_m, eq_seen = mask_bias(s_meta, jnp.zeros((1, tq), F32))
        bias_0, eq_seen = mask_bias(s_ref[0], eq_seen)
        init = [eq_seen]
        for g in range(n_groups):
            s_m = logits(g, mckv_ref[...], bias_m)
            s_0 = logits(g, ckv_ref[0, 0], bias_0)
            m = jnp.maximum(jnp.max(s_m, axis=0, keepdims=True), jnp.max(s_0, axis=0, keepdims=True))
            p_m = jnp.exp2(s_m - m)
            p_0 = jnp.exp2(s_0 - m)
            acc_ref[g] = _dot(mckvt_ref[...], p_m.astype(BF16))
            p_ref[g] = p_0.astype(BF16)
            init += [m, jnp.sum(p_m, axis=0, keepdims=True) + jnp.sum(p_0, axis=0, keepdims=True),
                     jnp.ones((1, hq), F32)]
        carry = lax.fori_loop(1, nch, attend, tuple(init))
        rows = []
        for g in range(n_groups):
            _, l, a_last = carry[1 + 3 * g:4 + 3 * g]
            fold_in(g, nch - 1, a_last)
            o_t = (acc_ref[g] / l).T
            rows.append(jnp.concatenate([o_t[h * ATT_GROUP:(h + 1) * ATT_GROUP] for h in range(ATT_HEADS)], axis=1))
        return _dot(jnp.concatenate(rows, axis=0).astype(BF16), wuv_ref[...]).astype(BF16)

    y_ref[0] = lax.cond(ranked_ties, lambda: attention(True), lambda: attention(False))


def _dsa_call(qlat, qidx, wrow, ckv_c, ckvt_c, kidx_c, m_ckv, m_ckvt, m_kidx, wuv_bd, topk):
    bn, rows, _ = qlat.shape
    nchunks, kc = ckv_c.shape[1], ckv_c.shape[2]
    nq = rows // Q_TILE

    def q_spec(width):
        return pl.BlockSpec((1, Q_TILE, width), lambda b, i: (b, i, 0))

    def k_spec(a):
        return pl.BlockSpec((1,) + a.shape[1:], lambda b, i: (b, 0, 0, 0))

    return pl.pallas_call(
        functools.partial(_dsa_kernel, topk=topk),
        grid=(bn, nq),
        in_specs=[q_spec(LAT_WIDTH), q_spec(IDX_WIDTH),
                  pl.BlockSpec((1, 1, GATE_ROWS, Q_TILE), lambda b, i: (b, i, 0, 0)),
                  k_spec(ckv_c), k_spec(ckvt_c), k_spec(kidx_c),
                  _resident(m_ckv.shape), _resident(m_ckvt.shape), _resident(m_kidx.shape), _resident(wuv_bd.shape)],
        out_specs=q_spec(ATT_WIDTH),
        out_shape=jax.ShapeDtypeStruct((bn, rows, ATT_WIDTH), BF16),
        scratch_shapes=[pltpu.VMEM((nchunks, kc, Q_TILE), F32),
                        pltpu.VMEM((Q_TILE // ATT_GROUP, KV_LATENT, ATT_HEADS * ATT_GROUP), F32),
                        pltpu.VMEM((Q_TILE // ATT_GROUP, kc, ATT_HEADS * ATT_GROUP), BF16)],
        compiler_params=pltpu.CompilerParams(
            dimension_semantics=("arbitrary", "arbitrary"), vmem_limit_bytes=VMEM_LIMIT_BYTES),
        name="dsa",
    )(qlat, qidx, wrow, ckv_c, ckvt_c, kidx_c, m_ckv, m_ckvt, m_kidx, wuv_bd)


def _split3(x):
    hi = x.astype(BF16)
    r = x - hi.astype(F32)
    mid = r.astype(BF16)
    lo = (r - mid.astype(F32)).astype(BF16)
    return hi, mid, lo


ML_EXT = ML_V_DIM + 16


def _mlstm_chunk(qk, vt, g, gt, state):
    L = qk.shape[0]
    s_idx = lax.broadcasted_iota(jnp.int32, (L, L), 0)
    t_idx = lax.broadcasted_iota(jnp.int32, (L, L), 1)
    causal = s_idx <= t_idx
    b_cols = sum(_dot((t_idx <= s_idx).astype(BF16), part) for part in _split3(g))
    b_rows = sum(_dot(part, causal.astype(BF16)) for part in _split3(gt))
    ones_blk = jnp.where(lax.broadcasted_iota(jnp.int32, (ML_EXT - ML_V_DIM, L), 0) == 0, 1.0, 0.0).astype(BF16)
    kq = ML_HEADS * ML_QK_DIM

    outs, new_state = [], []
    for h in range(ML_HEADS):
        ce, m_prev = state[h]
        c_col = g[:, GATE_I0 + h:GATE_I0 + h + 1] - b_cols[:, GATE_F0 + h:GATE_F0 + h + 1]
        b_row = b_rows[GATE_F0 - GATE_W0 + h:GATE_F0 - GATE_W0 + h + 1, :]
        ig_row = gt[GATE_I0 - GATE_W0 + h:GATE_I0 - GATE_W0 + h + 1, :]
        qh = qk[:, h * ML_QK_DIM:(h + 1) * ML_QK_DIM]
        kh = qk[:, kq + h * ML_QK_DIM:kq + (h + 1) * ML_QK_DIM]
        vt_ext = jnp.concatenate([vt[h * ML_V_DIM:(h + 1) * ML_V_DIM, :], ones_blk], axis=0)

        d_t = jnp.where(causal, c_col + b_row, -jnp.inf)
        inter = b_row + m_prev
        m_t = jnp.maximum(jnp.max(d_t, axis=0, keepdims=True), inter)
        w_inter = jnp.exp(inter - m_t)
        s_t = _dot_nt(kh, qh) * jnp.exp(d_t - m_t)
        r = _dot(vt_ext, s_t.astype(BF16)) + _dot_nt(ce.astype(BF16), qh) * w_inter
        num = r[:ML_V_DIM]
        den = r[ML_V_DIM:ML_V_DIM + 1]
        hh = num / jnp.maximum(jnp.abs(den), jnp.exp(-m_t))
        mu = jnp.mean(hh, axis=0, keepdims=True)
        hc = hh - mu
        var = jnp.mean(hc * hc, axis=0, keepdims=True)
        outs.append((hc * lax.rsqrt(var + LN_EPS)).T)

        b_end = b_row[:, L - 1:L]
        g_row = b_end - b_row + ig_row
        m_new = jnp.maximum(b_end + m_prev, jnp.max(g_row, axis=1, keepdims=True))
        decay = jnp.exp(b_end + m_prev - m_new)
        weighted = (vt_ext.astype(F32) * jnp.exp(g_row - m_new)).astype(BF16)
        new_state.append((decay * ce + _dot(weighted, kh), m_new))
    return outs, new_state


def _mlstm_kernel(qk_ref, vt_ref, og_ref, gates_ref, gt_ref, mqk_ref, mvt_ref, mgates_ref, mgt_ref, ng_ref, y_ref,
                  ce0_ref, m0_ref):
    L = mqk_ref.shape[0]
    n_chunks = qk_ref.shape[1] // L
    norm_g = ng_ref[...]

    @pl.when(pl.program_id(0) == 0)
    def _():
        state = [(jnp.zeros((ML_EXT, ML_QK_DIM), F32), jnp.full((1, 1), M_INIT, F32)) for _ in range(ML_HEADS)]
        _, state = _mlstm_chunk(mqk_ref[...], mvt_ref[...], mgates_ref[...], mgt_ref[...], state)
        for h in range(ML_HEADS):
            ce0_ref[h] = state[h][0]
            m0_ref[h] = jnp.broadcast_to(state[h][1], m0_ref.shape[1:])

    n_b = qk_ref.shape[0]
    state = [(ce0_ref[h], m0_ref[h][0:1, 0:1]) for h in range(ML_HEADS)] * n_b

    def body(c, flat):
        rows = pl.ds(pl.multiple_of(c * L, L), L)
        new_flat = []
        for b in range(n_b):
            state = [(flat[2 * (b * ML_HEADS + h)], flat[2 * (b * ML_HEADS + h) + 1]) for h in range(ML_HEADS)]
            outs, state = _mlstm_chunk(qk_ref[b, rows, :], vt_ref[b, c], gates_ref[b, rows, :], gt_ref[b, c], state)
            y = jnp.concatenate(outs, axis=1) * norm_g * og_ref[b, rows, :]
            y_ref[b, rows, :] = y.astype(BF16)
            new_flat += [x for pair in state for x in pair]
        return tuple(new_flat)

    lax.fori_loop(0, n_chunks, body, tuple(x for pair in state for x in pair))


def _mlstm_call(qk, vt_c, og, gates, gt_c, mqk, mvt, mgates, mgt, norm_g):
    bn, rows, _ = qk.shape
    nb = ML_BATCH if bn % ML_BATCH == 0 else 1

    def b_spec(a):
        return pl.BlockSpec((nb,) + a.shape[1:], lambda b: (b,) + (0,) * (a.ndim - 1))

    consts = (mqk, mvt, mgates, mgt, norm_g)
    return pl.pallas_call(
        _mlstm_kernel,
        grid=(bn // nb,),
        in_specs=[b_spec(qk), b_spec(vt_c), b_spec(og), b_spec(gates), b_spec(gt_c)]
        + [_resident(c.shape) for c in consts],
        out_specs=pl.BlockSpec((nb, rows, ML_WIDTH), lambda b: (b, 0, 0)),
        out_shape=jax.ShapeDtypeStruct((bn, rows, ML_WIDTH), BF16),
        scratch_shapes=[pltpu.VMEM((ML_HEADS, ML_EXT, ML_QK_DIM), F32), pltpu.VMEM((ML_HEADS, SUBLANES, LANES), F32)],
        compiler_params=pltpu.CompilerParams(
            dimension_semantics=("arbitrary",), vmem_limit_bytes=VMEM_LIMIT_BYTES),
        name="mlstm",
    )(qk, vt_c, og, gates, gt_c, *consts)


def _out_ffn_ln_kernel(ya_ref, ym_ref, h_ref, wo_ref, g2_ref, b2_ref, wg_ref, wu_ref, wd_ref, g3_ref, b3_ref,
                       o_ref, wg_s, wu_s, wd_s, h2_ref, acc0_ref, *, alpha, n_stage):
    i = pl.program_id(0)

    def mixed():
        mix = _dot(ya_ref[...], wo_ref[:ATT_WIDTH, :]) + _dot(ym_ref[...], wo_ref[ATT_WIDTH:, :])
        return _layer_norm(alpha * h_ref[...] + mix, g2_ref[...], b2_ref[...])

    @pl.when(i == 0)
    def _():
        h2_ref[...] = mixed()
        acc0_ref[...] = jnp.zeros(acc0_ref.shape, F32)

    @pl.when(i < n_stage)
    def _():
        _stage_ffn_weights(i, wg_ref, wu_ref, wd_ref, wg_s, wu_s, wd_s)
        acc0_ref[...] += _swiglu_chunk(h2_ref[...].astype(BF16), wg_s[i], wu_s[i], wd_s[i])

    @pl.when(i == n_stage - 1)
    def _():
        o_ref[...] = _layer_norm(alpha * h2_ref[...] + 0.5 * acc0_ref[...], g3_ref[...], b3_ref[...])

    @pl.when(i >= n_stage)
    def _():
        o_ref[...] = _ffn_ln(mixed(), wg_s, wu_s, wd_s, g3_ref[...], b3_ref[...], alpha)


def _out_ffn_ln_call(ya, ym, h, wo, g2, b2, wg, wu, wd, g3, b3, alpha, tm):
    rows, d = h.shape
    n_stage, w_specs, w_scratch = _ffn_weight_specs(d, wg.shape[1])

    def row_spec(width):
        return pl.BlockSpec((tm, width), lambda i: (_staged_tile_index(i, n_stage), 0))

    return pl.pallas_call(
        functools.partial(_out_ffn_ln_kernel, alpha=alpha, n_stage=n_stage),
        grid=(n_stage - 1 + rows // tm,),
        in_specs=[row_spec(ya.shape[1]), row_spec(ym.shape[1]), row_spec(d),
                  _resident(wo.shape), _resident(g2.shape), _resident(b2.shape)] + w_specs
        + [_resident(g3.shape), _resident(b3.shape)],
        out_specs=row_spec(d),
        out_shape=jax.ShapeDtypeStruct((rows, d), F32),
        scratch_shapes=w_scratch + [pltpu.VMEM((tm, d), F32), pltpu.VMEM((tm, d), F32)],
        compiler_params=pltpu.CompilerParams(
            dimension_semantics=("arbitrary",), vmem_limit_bytes=VMEM_LIMIT_BYTES),
        name="out_ffn_ln",
    )(ya, ym, h, wo, g2, b2, wg, wu, wd, g3, b3)


def _block_diag(w):
    nh, a, b = w.shape
    eye = jnp.eye(nh, dtype=w.dtype)
    return (eye[:, None, :, None] * w[:, :, None, :]).reshape(nh * a, nh * b)


def _pad_rows(a, rows, value=0.0):
    return jnp.pad(a, ((0, rows - a.shape[0]), (0, 0)), constant_values=value)


def kernel(x, meta_tokens, ln1_g, ln1_b, ffn1_w_gate, ffn1_w_up, ffn1_w_down, w_in, w_uk, w_uv, kv_norm_g,
           conv_w, b_igate, b_fgate, ml_norm_g, w_out, ln2_g, ln2_b, ffn2_w_gate, ffn2_w_up, ffn2_w_down,
           ln3_g, ln3_b):
    depth = ln1_g.shape[0]
    assert depth == 1, "the meta-token shortcut below is only valid for a single layer"
    bsz, seq, d = x.shape
    assert seq % ROW_TILE == 0 and seq % ML_CHUNK == 0 and seq % Q_TILE == 0
    assert Q_TILE == GT_CHUNK and ML_CHUNK == GT_CHUNK and KEY_CHUNK == GT_CHUNK and ROW_TILE % GT_CHUNK == 0
    alpha = (2 * depth) ** 0.25
    topk = min(TOPK_MAX, seq // 4)

    row2 = lambda p: p[0].reshape(1, -1).astype(F32)
    bf = lambda w: w[0].astype(BF16)

    w_t = jnp.swapaxes(w_in[0], 0, 1)
    o_qa, o_ckv, o_qi, o_ki, o_wi = 0, 512, 640, 896, 960
    o_qk, o_v, o_o, o_ig, o_fg, o_end = 964, 1476, 1988, 2500, 2504, 2508
    wa = jnp.concatenate([w_t[o_qa:o_wi], w_t[o_wi:o_qk], w_t[o_ig:o_end],
                          jnp.zeros((LANES - GATE_END, d), F32)], axis=0).astype(BF16)
    wm = w_t[o_qk:o_ig].astype(BF16)
    gbias = jnp.concatenate([jnp.zeros((GATE_I0,), F32), b_igate[0], b_fgate[0],
                             jnp.zeros((LANES - GATE_END,), F32)]).reshape(1, LANES)
    wuk_bd = jnp.stack([_block_diag(w_uk[0][2 * p:2 * p + 2]) for p in range(HEAD_PAIRS)]).astype(BF16)
    wuv_bd = _block_diag(w_uv[0]).astype(BF16)
    kvg = row2(kv_norm_g)
    convw = conv_w[0].astype(F32)

    h1, h1_meta = _ffn_ln_call(x.reshape(bsz * seq, d), meta_tokens.astype(F32), ffn1_w_gate[0], ffn1_w_up[0],
                               ffn1_w_down[0], row2(ln1_g), row2(ln1_b), alpha, FFN_TILE)
    zero_tail = jnp.zeros((CONV_HIST, MLQK_WIDTH), F32)
    (_, m_ckv, _, m_kidx, m_qk, m_v, _, m_gates, m_tail) = _inproj_call(
        h1_meta[None], zero_tail, wa, wm, wuk_bd, kvg, convw, gbias, N_META)

    (qlat, ckv, qidx, kidx, qk, _, og, gates, _, gates_t, ckv_t, v_t) = _inproj_call(
        h1.reshape(bsz, seq, d), m_tail[0], wa, wm, wuk_bd, kvg, convw, gbias, ROW_TILE)

    nchunks = seq // KEY_CHUNK
    ckv_c = ckv.reshape(bsz, nchunks, KEY_CHUNK, KV_LATENT)
    kidx_c = kidx.reshape(bsz, nchunks, KEY_CHUNK, IDX_DIM)
    y_att = _dsa_call(qlat, qidx, gates_t, ckv_c, ckv_t, kidx_c,
                      m_ckv[0], m_ckv[0].T, m_kidx[0], wuv_bd, topk)

    lane = jnp.arange(LANES)
    pad_gate = jnp.where((lane >= GATE_I0) & (lane < GATE_F0), NEG_BIG, 0.0).astype(F32)
    mg = jnp.concatenate([m_gates[0], jnp.broadcast_to(pad_gate, (ML_CHUNK - N_META, LANES))], axis=0)
    gate_lanes = slice(GATE_W0, GATE_W0 + GATE_ROWS)
    y_ml = _mlstm_call(qk, v_t, og, gates, gates_t,
                       _pad_rows(m_qk[0], ML_CHUNK), _pad_rows(m_v[0], ML_CHUNK).T, mg, mg[:, gate_lanes].T,
                       row2(ml_norm_g))

    out = _out_ffn_ln_call(
        y_att.reshape(bsz * seq, ATT_WIDTH), y_ml.reshape(bsz * seq, ML_WIDTH), h1, bf(w_out),
        row2(ln2_g), row2(ln2_b), ffn2_w_gate[0], ffn2_w_up[0], ffn2_w_down[0], row2(ln3_g), row2(ln3_b),
        alpha, FFN_TILE)
    return out.reshape(bsz, seq, d)
```

```python
import functools

import jax
import jax.numpy as jnp
from jax import lax
from jax.experimental import pallas as pl
from jax.experimental.pallas import tpu as pltpu

F32 = jnp.float32
BF16 = jnp.bfloat16

N_META = 16
ATT_HEADS = 8
ATT_HEAD_DIM = 64
KV_LATENT = 128
IDX_HEADS = 4
IDX_DIM = 64
TOPK_MAX = 256
ML_HEADS = 4
ML_V_DIM = 128
ML_QK_DIM = 64
CONV_WIDTH = 4
GATE_SOFTCAP = 15.0
M_INIT = -1e30
LN_EPS = 1e-5
NEG_BIG = -1e30
LOG2_E = 1.4426950408889634

LANES = 128
SUBLANES = 8
VMEM_LIMIT_BYTES = 56 * 1024 * 1024

FF_CHUNK = 256
ROW_TILE = 512
FFN_TILE = 512
Q_TILE = 256
ATT_GROUP = 128
KEY_CHUNK = 256
N_BISECT = 16
REDUCE_ROWS = 32
ML_BATCH = 2
ML_CHUNK = 256


def _dot(a, b):
    return jnp.dot(a, b, preferred_element_type=F32)


def _layer_norm(z, g, b):
    mu = jnp.mean(z, axis=-1, keepdims=True)
    zc = z - mu
    var = jnp.mean(zc * zc, axis=-1, keepdims=True)
    return zc * lax.rsqrt(var + LN_EPS) * g + b


def _sigmoid(x):
    return 1.0 / (1.0 + jnp.exp(-x))


def _swiglu_chunk(xb, wg_c, wu_c, wd_c):
    g = _dot(xb, wg_c)
    u = _dot(xb, wu_c)
    return _dot((g * _sigmoid(g) * u).astype(BF16), wd_c)


def _ffn_ln(x, wg_s, wu_s, wd_s, g, b, alpha):
    xb = x.astype(BF16)
    acc = jnp.zeros(x.shape, F32)
    for c in range(wg_s.shape[0]):
        acc = acc + _swiglu_chunk(xb, wg_s[c], wu_s[c], wd_s[c])
    return _layer_norm(alpha * x + 0.5 * acc, g, b)


def _stage_ffn_weights(step, wg_ref, wu_ref, wd_ref, wg_s, wu_s, wd_s):
    wg_s[step] = wg_ref[...].astype(BF16)
    wu_s[step] = wu_ref[...].astype(BF16)
    wd_s[step] = wd_ref[...].astype(BF16)


def _ffn_weight_specs(d, d_ff):
    n = d_ff // FF_CHUNK
    col = pl.BlockSpec((d, FF_CHUNK), lambda i: (0, jnp.minimum(i, n - 1)))
    row = pl.BlockSpec((FF_CHUNK, d), lambda i: (jnp.minimum(i, n - 1), 0))
    scratch = [pltpu.VMEM((n, d, FF_CHUNK), BF16), pltpu.VMEM((n, d, FF_CHUNK), BF16),
               pltpu.VMEM((n, FF_CHUNK, d), BF16)]
    return n, [col, col, row], scratch


def _staged_tile_index(i, n_stage):
    return jnp.maximum(i - (n_stage - 1), 0)


def _ffn_ln_kernel(x_ref, meta_ref, wg_ref, wu_ref, wd_ref, g_ref, b_ref, o_ref, ometa_ref,
                   wg_s, wu_s, wd_s, macc_ref, acc0_ref, *, alpha, n_stage):
    i = pl.program_id(0)

    @pl.when(i == 0)
    def _():
        macc_ref[...] = jnp.zeros(macc_ref.shape, F32)
        acc0_ref[...] = jnp.zeros(acc0_ref.shape, F32)

    @pl.when(i < n_stage)
    def _():
        _stage_ffn_weights(i, wg_ref, wu_ref, wd_ref, wg_s, wu_s, wd_s)
        macc_ref[...] += _swiglu_chunk(meta_ref[...].astype(BF16), wg_s[i], wu_s[i], wd_s[i])
        acc0_ref[...] += _swiglu_chunk(x_ref[...].astype(BF16), wg_s[i], wu_s[i], wd_s[i])

    @pl.when(i == n_stage - 1)
    def _():
        ometa_ref[...] = _layer_norm(alpha * meta_ref[...] + 0.5 * macc_ref[...], g_ref[...], b_ref[...])
        o_ref[...] = _layer_norm(alpha * x_ref[...] + 0.5 * acc0_ref[...], g_ref[...], b_ref[...])

    @pl.when(i >= n_stage)
    def _():
        o_ref[...] = _ffn_ln(x_ref[...], wg_s, wu_s, wd_s, g_ref[...], b_ref[...], alpha)


def _resident(shape):
    return pl.BlockSpec(shape, lambda *_: (0,) * len(shape), pipeline_mode=pl.Buffered(1))


def _ffn_ln_call(x, meta, wg, wu, wd, g, b, alpha, tm):
    rows, d = x.shape
    n_stage, w_specs, w_scratch = _ffn_weight_specs(d, wg.shape[1])
    row_spec = pl.BlockSpec((tm, d), lambda i: (_staged_tile_index(i, n_stage), 0))
    return pl.pallas_call(
        functools.partial(_ffn_ln_kernel, alpha=alpha, n_stage=n_stage),
        grid=(n_stage - 1 + rows // tm,),
        in_specs=[row_spec, _resident(meta.shape)] + w_specs + [_resident(g.shape), _resident(b.shape)],
        out_specs=[row_spec, pl.BlockSpec(meta.shape, lambda i: (0, 0))],
        out_shape=[jax.ShapeDtypeStruct((rows, d), F32), jax.ShapeDtypeStruct(meta.shape, F32)],
        scratch_shapes=w_scratch + [pltpu.VMEM(meta.shape, F32), pltpu.VMEM((tm, d), F32)],
        compiler_params=pltpu.CompilerParams(
            dimension_semantics=("arbitrary",), vmem_limit_bytes=VMEM_LIMIT_BYTES),
        name="ffn_ln",
    )(x, meta, wg, wu, wd, g, b)


ATT_WIDTH = ATT_HEADS * ATT_HEAD_DIM
IDX_WIDTH = IDX_HEADS * IDX_DIM
MLQK_WIDTH = 2 * ML_HEADS * ML_QK_DIM
ML_WIDTH = ML_HEADS * ML_V_DIM
LAT_WIDTH = ATT_HEADS * KV_LATENT
CONV_HIST = SUBLANES
GATE_W0 = IDX_DIM
GATE_I0, GATE_F0, GATE_END = GATE_W0 + IDX_HEADS, GATE_W0 + IDX_HEADS + ML_HEADS, GATE_W0 + IDX_HEADS + 2 * ML_HEADS
GATE_ROWS = 2 * SUBLANES
GT_CHUNK = 256
HEAD_PAIRS = ATT_HEADS // 2


def _inproj_kernel(h_ref, tail_ref, wa_ref, wm_ref, wuk_ref, kvg_ref, convw_ref, gbias_ref,
                   qlat_ref, ckv_ref, qidx_ref, kidx_ref, qk_ref, v_ref, og_ref, gates_ref, tailout_ref, *rest):
    carry_ref, wa_s, wm_s = rest[-3:]
    tm = h_ref.shape[1]

    @pl.when((pl.program_id(0) == 0) & (pl.program_id(1) == 0))
    def _():
        wa_s[...] = wa_ref[...].astype(F32).T.astype(BF16)
        wm_s[...] = wm_ref[...].astype(F32).T.astype(BF16)

    @pl.when(pl.program_id(1) == 0)
    def _():
        carry_ref[...] = tail_ref[...]

    xb = h_ref[0].astype(BF16)

    pa = _dot(xb, wa_s[...])
    q_a = pa[:, :ATT_WIDTH].astype(BF16)
    c0 = ATT_WIDTH
    ckv = pa[:, c0:c0 + KV_LATENT]
    c1 = c0 + KV_LATENT
    ckv = ckv * lax.rsqrt(jnp.mean(ckv * ckv, axis=-1, keepdims=True) + LN_EPS) * kvg_ref[...]
    ckv_ref[0] = ckv.astype(BF16)
    qidx_ref[0] = pa[:, c1:c1 + IDX_WIDTH].astype(BF16)
    c2 = c1 + IDX_WIDTH
    kidx_ref[0] = pa[:, c2:c2 + IDX_DIM].astype(BF16)
    pair_in, pair_out = 2 * ATT_HEAD_DIM, 2 * KV_LATENT
    for p in range(HEAD_PAIRS):
        ql = _dot(q_a[:, p * pair_in:(p + 1) * pair_in], wuk_ref[p])
        qlat_ref[0, :, p * pair_out:(p + 1) * pair_out] = (ql * (ATT_HEAD_DIM ** -0.5 * LOG2_E)).astype(BF16)

    pm = _dot(xb, wm_s[...])
    qk_raw = pm[:, :MLQK_WIDTH]
    v_ref[0] = pm[:, MLQK_WIDTH:MLQK_WIDTH + ML_WIDTH].astype(BF16)
    og_ref[0] = _sigmoid(pm[:, MLQK_WIDTH + ML_WIDTH:])

    ext = jnp.concatenate([carry_ref[...], qk_raw], axis=0)
    cw = convw_ref[...]
    conv = jnp.zeros_like(qk_raw)
    for j in range(CONV_WIDTH):
        s0 = CONV_HIST - (CONV_WIDTH - 1) + j
        conv = conv + ext[s0:s0 + tm] * cw[j:j + 1]
    act = conv * _sigmoid(conv)
    half = MLQK_WIDTH // 2
    qk_ref[0, :, :half] = act[:, :half].astype(BF16)
    qk_ref[0, :, half:] = (act[:, half:] * (ML_QK_DIM ** -0.5)).astype(BF16)
    carry_ref[...] = qk_raw[tm - CONV_HIST:]
    tailout_ref[0] = qk_raw[tm - CONV_HIST:]

    gr = pa[:, c2 + IDX_DIM - GATE_W0:]
    lane = lax.broadcasted_iota(jnp.int32, gr.shape, 1)
    sc = GATE_SOFTCAP * jnp.tanh((gr + gbias_ref[...]) / GATE_SOFTCAP)
    lf = -(jnp.maximum(-sc, 0.0) + jnp.log1p(jnp.exp(-jnp.abs(sc))))
    w_scaled = gr * (IDX_HEADS ** -0.5 * IDX_DIM ** -0.5)
    gates = jnp.where((lane < GATE_W0) | (lane >= GATE_END), 0.0,
                      jnp.where(lane < GATE_I0, w_scaled, jnp.where(lane < GATE_F0, sc, lf)))
    gates_ref[0] = gates
    if len(rest) == 7:
        gt_ref, ckvt_ref, vt_ref, qt_ref = rest[:4]
        gates_t = gates.T[GATE_W0:GATE_W0 + GATE_ROWS]
        ckv_t = ckv.T.astype(BF16)
        v_t = pm[:, MLQK_WIDTH:MLQK_WIDTH + ML_WIDTH].T.astype(BF16)
        q_t = act[:, :half].T.astype(BF16)
        for j in range(tm // GT_CHUNK):
            piece = slice(j * GT_CHUNK, (j + 1) * GT_CHUNK)
            gt_ref[0, j] = gates_t[:, piece]
            ckvt_ref[0, j] = ckv_t[:, piece]
            vt_ref[0, j] = v_t[:, piece]
            qt_ref[0, j] = q_t[:, piece]


def _inproj_call(h, tail, wa, wm, wuk_bd, kvg, convw, gbias, tm):
    bn, rows, d = h.shape
    nblk = rows // tm
    emit_gt = tm % GT_CHUNK == 0

    def row_spec(width):
        return pl.BlockSpec((1, tm, width), lambda b, j: (b, j, 0))

    outs = [
        (LAT_WIDTH, BF16), (KV_LATENT, BF16), (IDX_WIDTH, BF16), (IDX_DIM, BF16),
        (MLQK_WIDTH, BF16), (ML_WIDTH, BF16), (ML_WIDTH, F32), (LANES, F32),
    ]
    out_shape = [jax.ShapeDtypeStruct((bn, rows, w), dt) for w, dt in outs]
    out_specs = [row_spec(w) for w, _ in outs]
    out_shape.append(jax.ShapeDtypeStruct((bn, CONV_HIST, MLQK_WIDTH), F32))
    out_specs.append(pl.BlockSpec((1, CONV_HIST, MLQK_WIDTH), lambda b, j: (b, 0, 0)))
    if emit_gt:
        per_tile = tm // GT_CHUNK
        for height, dt in ((GATE_ROWS, F32), (KV_LATENT, BF16), (ML_WIDTH, BF16), (MLQK_WIDTH // 2, BF16)):
            out_shape.append(jax.ShapeDtypeStruct((bn, rows // GT_CHUNK, height, GT_CHUNK), dt))
            out_specs.append(pl.BlockSpec((1, per_tile, height, GT_CHUNK), lambda b, j: (b, j, 0, 0)))
    return pl.pallas_call(
        _inproj_kernel,
        grid=(bn, nblk),
        in_specs=[
            row_spec(d),
            _resident(tail.shape), _resident(wa.shape), _resident(wm.shape),
            _resident(wuk_bd.shape), _resident(kvg.shape), _resident(convw.shape), _resident(gbias.shape),
        ],
        out_specs=out_specs,
        out_shape=out_shape,
        scratch_shapes=[pltpu.VMEM((CONV_HIST, MLQK_WIDTH), F32),
                        pltpu.VMEM(wa.shape[::-1], BF16), pltpu.VMEM(wm.shape[::-1], BF16)],
        compiler_params=pltpu.CompilerParams(
            dimension_semantics=("arbitrary", "arbitrary"), vmem_limit_bytes=VMEM_LIMIT_BYTES),
        name="in_proj",
    )(h, tail, wa, wm, wuk_bd, kvg, convw, gbias)


def _dsa_kernel(qlat_ref, qidx_ref, wrow_ref, ckv_ref, ckvt_ref, kidx_ref, mckv_ref, mckvt_ref, mkidx_ref,
                wuv_ref, y_ref, s_ref, acc_ref, p_ref, *, topk):
    _, kc, tq = s_ref.shape
    i = pl.program_id(1)
    nch = ((i + 1) * tq + kc - 1) // kc
    qreal = i * tq + lax.broadcasted_iota(jnp.int32, (1, tq), 1)
    kf = float(topk)

    wrow = wrow_ref[0, 0]
    qidx = qidx_ref[0]
    q_idx_t = qidx.astype(F32).T
    q_idx_all_t = jnp.concatenate([q_idx_t[h * IDX_DIM:(h + 1) * IDX_DIM] for h in range(IDX_HEADS)],
                                  axis=1).astype(BF16)
    wi = [wrow[h:h + 1, :] for h in range(IDX_HEADS)]

    def scores(k_rows):
        lg = _dot(k_rows, q_idx_all_t)
        sc = jnp.zeros((k_rows.shape[0], tq), F32)
        for h in range(IDX_HEADS):
            sc = sc + jnp.maximum(lg[:, h * tq:(h + 1) * tq], 0.0) * wi[h]
        return sc

    s_meta = scores(mkidx_ref[...])

    def score_chunk(c, lo, hi):
        sc = scores(kidx_ref[0, c])
        valid = c * kc + lax.broadcasted_iota(jnp.int32, (kc, tq), 0) <= qreal
        s_ref[c] = jnp.where(valid, sc, -jnp.inf)
        groups = (kc // REDUCE_ROWS, REDUCE_ROWS, tq)
        lo = jnp.minimum(lo, jnp.min(sc.reshape(groups), axis=0))
        hi = jnp.maximum(hi, jnp.max(sc.reshape(groups), axis=0))
        return lo, hi

    def score_pair(c2, carry):
        lo, hi = score_chunk(2 * c2, *carry)
        return score_chunk(jnp.minimum(2 * c2 + 1, nch - 1), lo, hi)

    lo, hi = lax.fori_loop(0, (nch + 1) // 2, score_pair,
                           (jnp.full((REDUCE_ROWS, tq), jnp.inf, F32), jnp.full((REDUCE_ROWS, tq), -jnp.inf, F32)))
    lo = jnp.minimum(jnp.min(lo, axis=0, keepdims=True), jnp.min(s_meta, axis=0, keepdims=True))
    hi = jnp.maximum(jnp.max(hi, axis=0, keepdims=True), jnp.max(s_meta, axis=0, keepdims=True))

    def key_reduce(reduce, combine, per_chunk, init):
        def body(c, acc):
            x = per_chunk(s_ref[c]).reshape(kc // REDUCE_ROWS, REDUCE_ROWS, tq)
            return combine(acc, reduce(x, axis=0))
        acc = lax.fori_loop(0, nch, body, jnp.full((REDUCE_ROWS, tq), init, F32))
        return combine(reduce(acc, axis=0, keepdims=True), reduce(per_chunk(s_meta), axis=0, keepdims=True))

    def count(pred):
        return key_reduce(jnp.sum, jnp.add, lambda sc: jnp.where(pred(sc), 1.0, 0.0), 0.0)

    def max_where(pred):
        return key_reduce(jnp.max, jnp.maximum, lambda sc: jnp.where(pred(sc), sc, -jnp.inf), -jnp.inf)

    def bisect(_, carry):
        lo, hi = carry
        mid = 0.5 * lo + 0.5 * hi
        up = count(lambda sc: sc > mid) >= kf
        return jnp.where(up, mid, lo), jnp.where(up, hi, mid)

    lo, hi = lax.fori_loop(0, N_BISECT, bisect, (lo, hi))

    n_valid = (qreal + (N_META + 1)).astype(F32)
    small = n_valid <= kf
    cand = max_where(lambda sc: sc <= hi)
    n_ge = count(lambda sc: sc >= cand)
    done = jnp.where(small | (n_ge >= kf), 1.0, 0.0)

    def not_finished(state):
        return jnp.min(state[1]) < 0.5

    def step_down(state):
        cand, done, _ = state
        nxt = jnp.where(done > 0.5, cand, max_where(lambda sc: sc < cand))
        n_ge = count(lambda sc: sc >= nxt)
        return nxt, jnp.where(n_ge >= kf, 1.0, done), n_ge

    cand, _, n_ge = lax.while_loop(not_finished, step_down, (cand, done, n_ge))
    thr = jnp.where(small, -jnp.inf, cand)
    n_eq = count(lambda sc: sc == thr)
    need = jnp.where(small, 0.0, kf - (n_ge - n_eq))
    ranked_ties = jnp.max(jnp.where(n_eq > need, 1.0, 0.0)) > 0.5

    qlat = qlat_ref[0]
    n_groups = tq // ATT_GROUP
    onehot = (lax.broadcasted_iota(jnp.int32, (ATT_GROUP, ATT_GROUP), 0)
              == lax.broadcasted_iota(jnp.int32, (ATT_GROUP, ATT_GROUP), 1)).astype(BF16)
    qlat32 = qlat.astype(F32)
    q_aug_t = [jnp.concatenate(
        [jnp.concatenate([qlat32[g * ATT_GROUP:(g + 1) * ATT_GROUP, h * KV_LATENT:(h + 1) * KV_LATENT].T
                          for h in range(ATT_HEADS)], axis=1).astype(BF16),
         jnp.concatenate([onehot] * ATT_HEADS, axis=1)], axis=0)
        for g in range(n_groups)]
    hq = ATT_HEADS * ATT_GROUP

    def lower_tri(n):
        return (lax.broadcasted_iota(jnp.int32, (n, n), 1) <= lax.broadcasted_iota(jnp.int32, (n, n), 0)).astype(BF16)

    def attention(ranked):
        def mask_bias(sc, eq_seen):
            if not ranked:
                return jnp.where(sc >= thr, 0.0, NEG_BIG).astype(BF16), eq_seen
            n = sc.shape[0]
            eq = sc == thr
            rank = _dot(lower_tri(n), jnp.where(eq, 1.0, 0.0).astype(BF16)) + eq_seen
            keep = (sc > thr) | (eq & (rank <= need))
            return jnp.where(keep, 0.0, NEG_BIG).astype(BF16), rank[n - 1:n, :]

        def logits(g, kv, bias):
            k_aug = jnp.concatenate([kv, bias[:, g * ATT_GROUP:(g + 1) * ATT_GROUP]], axis=1)
            return _dot(k_aug, q_aug_t[g])

        def fold_in(g, c_prev, a_prev):
            acc_ref[g] = a_prev * acc_ref[g] + _dot(ckvt_ref[0, c_prev], p_ref[g])

        def attend(c, carry):
            bias, eq_seen = mask_bias(s_ref[c], carry[0])
            out = [eq_seen]
            for g in range(n_groups):
                m, l, a_prev = carry[1 + 3 * g:4 + 3 * g]
                fold_in(g, c - 1, a_prev)
                s = logits(g, ckv_ref[0, c], bias)
                m_new = jnp.maximum(m, jnp.max(s, axis=0, keepdims=True))
                a = jnp.exp2(m - m_new)
                p = jnp.exp2(s - m_new)
                p_ref[g] = p.astype(BF16)
                out += [m_new, a * l + jnp.sum(p, axis=0, keepdims=True), a]
            return tuple(out)

        bias_m, eq_seen = mask_bias(s_meta, jnp.zeros((1, tq), F32))
        bias_0, eq_seen = mask_bias(s_ref[0], eq_seen)
        init = [eq_seen]
        for g in range(n_groups):
            s_m = logits(g, mckv_ref[...], bias_m)
            s_0 = logits(g, ckv_ref[0, 0], bias_0)
            m = jnp.maximum(jnp.max(s_m, axis=0, keepdims=True), jnp.max(s_0, axis=0, keepdims=True))
            p_m = jnp.exp2(s_m - m)
            p_0 = jnp.exp2(s_0 - m)
            acc_ref[g] = _dot(mckvt_ref[...], p_m.astype(BF16))
            p_ref[g] = p_0.astype(BF16)
            init += [m, jnp.sum(p_m, axis=0, keepdims=True) + jnp.sum(p_0, axis=0, keepdims=True),
                     jnp.ones((1, hq), F32)]
        carry = lax.fori_loop(1, nch, attend, tuple(init))
        rows = []
        for g in range(n_groups):
            _, l, a_last = carry[1 + 3 * g:4 + 3 * g]
            fold_in(g, nch - 1, a_last)
            o_t = (acc_ref[g] / l).T
            rows.append(jnp.concatenate([o_t[h * ATT_GROUP:(h + 1) * ATT_GROUP] for h in range(ATT_HEADS)], axis=1))
        return _dot(jnp.concatenate(rows, axis=0).astype(BF16), wuv_ref[...]).astype(BF16)

    y_ref[0] = lax.cond(ranked_ties, lambda: attention(True), lambda: attention(False))


def _dsa_call(qlat, qidx, wrow, ckv_c, ckvt_c, kidx_c, m_ckv, m_ckvt, m_kidx, wuv_bd, topk):
    bn, rows, _ = qlat.shape
    nchunks, kc = ckv_c.shape[1], ckv_c.shape[2]
    nq = rows // Q_TILE

    def q_spec(width):
        return pl.BlockSpec((1, Q_TILE, width), lambda b, i: (b, i, 0))

    def k_spec(a):
        return pl.BlockSpec((1,) + a.shape[1:], lambda b, i: (b, 0, 0, 0))

    return pl.pallas_call(
        functools.partial(_dsa_kernel, topk=topk),
        grid=(bn, nq),
        in_specs=[q_spec(LAT_WIDTH), q_spec(IDX_WIDTH),
                  pl.BlockSpec((1, 1, GATE_ROWS, Q_TILE), lambda b, i: (b, i, 0, 0)),
                  k_spec(ckv_c), k_spec(ckvt_c), k_spec(kidx_c),
                  _resident(m_ckv.shape), _resident(m_ckvt.shape), _resident(m_kidx.shape), _resident(wuv_bd.shape)],
        out_specs=q_spec(ATT_WIDTH),
        out_shape=jax.ShapeDtypeStruct((bn, rows, ATT_WIDTH), BF16),
        scratch_shapes=[pltpu.VMEM((nchunks, kc, Q_TILE), F32),
                        pltpu.VMEM((Q_TILE // ATT_GROUP, KV_LATENT, ATT_HEADS * ATT_GROUP), F32),
                        pltpu.VMEM((Q_TILE // ATT_GROUP, kc, ATT_HEADS * ATT_GROUP), BF16)],
        compiler_params=pltpu.CompilerParams(
            dimension_semantics=("arbitrary", "arbitrary"), vmem_limit_bytes=VMEM_LIMIT_BYTES),
        name="dsa",
    )(qlat, qidx, wrow, ckv_c, ckvt_c, kidx_c, m_ckv, m_ckvt, m_kidx, wuv_bd)


def _split3(x):
    hi = x.astype(BF16)
    r = x - hi.astype(F32)
    mid = r.astype(BF16)
    lo = (r - mid.astype(F32)).astype(BF16)
    return hi, mid, lo


ML_EXT = ML_V_DIM + 16


def _mlstm_chunk(qk, qt, vt, g, gt, state):
    L = qk.shape[0]
    s_idx = lax.broadcasted_iota(jnp.int32, (L, L), 0)
    t_idx = lax.broadcasted_iota(jnp.int32, (L, L), 1)
    causal = s_idx <= t_idx
    b_cols = sum(_dot((t_idx <= s_idx).astype(BF16), part) for part in _split3(g))
    b_rows = sum(_dot(part, causal.astype(BF16)) for part in _split3(gt))
    ones_blk = jnp.where(lax.broadcasted_iota(jnp.int32, (ML_EXT - ML_V_DIM, L), 0) == 0, 1.0, 0.0).astype(BF16)
    kq = ML_HEADS * ML_QK_DIM

    outs, new_state = [], []
    for h in range(ML_HEADS):
        ce, m_prev = state[h]
        c_col = g[:, GATE_I0 + h:GATE_I0 + h + 1] - b_cols[:, GATE_F0 + h:GATE_F0 + h + 1]
        b_row = b_rows[GATE_F0 - GATE_W0 + h:GATE_F0 - GATE_W0 + h + 1, :]
        ig_row = gt[GATE_I0 - GATE_W0 + h:GATE_I0 - GATE_W0 + h + 1, :]
        qt_h = qt[h * ML_QK_DIM:(h + 1) * ML_QK_DIM, :]
        kh = qk[:, kq + h * ML_QK_DIM:kq + (h + 1) * ML_QK_DIM]
        vt_ext = jnp.concatenate([vt[h * ML_V_DIM:(h + 1) * ML_V_DIM, :], ones_blk], axis=0)

        d_t = jnp.where(causal, c_col + b_row, -jnp.inf)
        inter = b_row + m_prev
        m_t = jnp.maximum(jnp.max(d_t, axis=0, keepdims=True), inter)
        w_inter = jnp.exp(inter - m_t)
        s_t = _dot(kh, qt_h) * jnp.exp(d_t - m_t)
        r = _dot(vt_ext, s_t.astype(BF16)) + _dot(ce.astype(BF16), qt_h) * w_inter
        num = r[:ML_V_DIM]
        den = r[ML_V_DIM:ML_V_DIM + 1]
        hh = num / jnp.maximum(jnp.abs(den), jnp.exp(-m_t))
        mu = jnp.mean(hh, axis=0, keepdims=True)
        hc = hh - mu
        var = jnp.mean(hc * hc, axis=0, keepdims=True)
        outs.append((hc * lax.rsqrt(var + LN_EPS)).T)

        b_end = b_row[:, L - 1:L]
        g_row = b_end - b_row + ig_row
        m_new = jnp.maximum(b_end + m_prev, jnp.max(g_row, axis=1, keepdims=True))
        decay = jnp.exp(b_end + m_prev - m_new)
        weighted = (vt_ext.astype(F32) * jnp.exp(g_row - m_new)).astype(BF16)
        new_state.append((decay * ce + _dot(weighted, kh), m_new))
    return outs, new_state


def _mlstm_kernel(qk_ref, qt_ref, vt_ref, og_ref, gates_ref, gt_ref, mqk_ref, mqt_ref, mvt_ref, mgates_ref, mgt_ref,
                  ng_ref, y_ref,
                  ce0_ref, m0_ref):
    L = mqk_ref.shape[0]
    n_chunks = qk_ref.shape[1] // L
    norm_g = ng_ref[...]

    @pl.when(pl.program_id(0) == 0)
    def _():
        state = [(jnp.zeros((ML_EXT, ML_QK_DIM), F32), jnp.full((1, 1), M_INIT, F32)) for _ in range(ML_HEADS)]
        _, state = _mlstm_chunk(mqk_ref[...], mqt_ref[...], mvt_ref[...], mgates_ref[...], mgt_ref[...], state)
        for h in range(ML_HEADS):
            ce0_ref[h] = state[h][0]
            m0_ref[h] = jnp.broadcast_to(state[h][1], m0_ref.shape[1:])

    n_b = qk_ref.shape[0]
    state = [(ce0_ref[h], m0_ref[h][0:1, 0:1]) for h in range(ML_HEADS)] * n_b

    def body(c, flat):
        rows = pl.ds(pl.multiple_of(c * L, L), L)
        new_flat = []
        for b in range(n_b):
            state = [(flat[2 * (b * ML_HEADS + h)], flat[2 * (b * ML_HEADS + h) + 1]) for h in range(ML_HEADS)]
            outs, state = _mlstm_chunk(qk_ref[b, rows, :], qt_ref[b, c], vt_ref[b, c], gates_ref[b, rows, :],
                                       gt_ref[b, c], state)
            y = jnp.concatenate(outs, axis=1) * norm_g * og_ref[b, rows, :]
            y_ref[b, rows, :] = y.astype(BF16)
            new_flat += [x for pair in state for x in pair]
        return tuple(new_flat)

    lax.fori_loop(0, n_chunks, body, tuple(x for pair in state for x in pair))


def _mlstm_call(qk, qt_c, vt_c, og, gates, gt_c, mqk, mqt, mvt, mgates, mgt, norm_g):
    bn, rows, _ = qk.shape
    nb = ML_BATCH if bn % ML_BATCH == 0 else 1

    def b_spec(a):
        return pl.BlockSpec((nb,) + a.shape[1:], lambda b: (b,) + (0,) * (a.ndim - 1))

    consts = (mqk, mqt, mvt, mgates, mgt, norm_g)
    return pl.pallas_call(
        _mlstm_kernel,
        grid=(bn // nb,),
        in_specs=[b_spec(qk), b_spec(qt_c), b_spec(vt_c), b_spec(og), b_spec(gates), b_spec(gt_c)]
        + [_resident(c.shape) for c in consts],
        out_specs=pl.BlockSpec((nb, rows, ML_WIDTH), lambda b: (b, 0, 0)),
        out_shape=jax.ShapeDtypeStruct((bn, rows, ML_WIDTH), BF16),
        scratch_shapes=[pltpu.VMEM((ML_HEADS, ML_EXT, ML_QK_DIM), F32), pltpu.VMEM((ML_HEADS, SUBLANES, LANES), F32)],
        compiler_params=pltpu.CompilerParams(
            dimension_semantics=("arbitrary",), vmem_limit_bytes=VMEM_LIMIT_BYTES),
        name="mlstm",
    )(qk, qt_c, vt_c, og, gates, gt_c, *consts)


def _out_ffn_ln_kernel(ya_ref, ym_ref, h_ref, wo_ref, g2_ref, b2_ref, wg_ref, wu_ref, wd_ref, g3_ref, b3_ref,
                       o_ref, wg_s, wu_s, wd_s, h2_ref, acc0_ref, *, alpha, n_stage):
    i = pl.program_id(0)

    def mixed():
        mix = _dot(ya_ref[...], wo_ref[:ATT_WIDTH, :]) + _dot(ym_ref[...], wo_ref[ATT_WIDTH:, :])
        return _layer_norm(alpha * h_ref[...] + mix, g2_ref[...], b2_ref[...])

    @pl.when(i == 0)
    def _():
        h2_ref[...] = mixed()
        acc0_ref[...] = jnp.zeros(acc0_ref.shape, F32)

    @pl.when(i < n_stage)
    def _():
        _stage_ffn_weights(i, wg_ref, wu_ref, wd_ref, wg_s, wu_s, wd_s)
        acc0_ref[...] += _swiglu_chunk(h2_ref[...].astype(BF16), wg_s[i], wu_s[i], wd_s[i])

    @pl.when(i == n_stage - 1)
    def _():
        o_ref[...] = _layer_norm(alpha * h2_ref[...] + 0.5 * acc0_ref[...], g3_ref[...], b3_ref[...])

    @pl.when(i >= n_stage)
    def _():
        o_ref[...] = _ffn_ln(mixed(), wg_s, wu_s, wd_s, g3_ref[...], b3_ref[...], alpha)


def _out_ffn_ln_call(ya, ym, h, wo, g2, b2, wg, wu, wd, g3, b3, alpha, tm):
    rows, d = h.shape
    n_stage, w_specs, w_scratch = _ffn_weight_specs(d, wg.shape[1])

    def row_spec(width):
        return pl.BlockSpec((tm, width), lambda i: (_staged_tile_index(i, n_stage), 0))

    return pl.pallas_call(
        functools.partial(_out_ffn_ln_kernel, alpha=alpha, n_stage=n_stage),
        grid=(n_stage - 1 + rows // tm,),
        in_specs=[row_spec(ya.shape[1]), row_spec(ym.shape[1]), row_spec(d),
                  _resident(wo.shape), _resident(g2.shape), _resident(b2.shape)] + w_specs
        + [_resident(g3.shape), _resident(b3.shape)],
        out_specs=row_spec(d),
        out_shape=jax.ShapeDtypeStruct((rows, d), F32),
        scratch_shapes=w_scratch + [pltpu.VMEM((tm, d), F32), pltpu.VMEM((tm, d), F32)],
        compiler_params=pltpu.CompilerParams(
            dimension_semantics=("arbitrary",), vmem_limit_bytes=VMEM_LIMIT_BYTES),
        name="out_ffn_ln",
    )(ya, ym, h, wo, g2, b2, wg, wu, wd, g3, b3)


def _block_diag(w):
    nh, a, b = w.shape
    eye = jnp.eye(nh, dtype=w.dtype)
    return (eye[:, None, :, None] * w[:, :, None, :]).reshape(nh * a, nh * b)


def _pad_rows(a, rows, value=0.0):
    return jnp.pad(a, ((0, rows - a.shape[0]), (0, 0)), constant_values=value)


def kernel(x, meta_tokens, ln1_g, ln1_b, ffn1_w_gate, ffn1_w_up, ffn1_w_down, w_in, w_uk, w_uv, kv_norm_g,
           conv_w, b_igate, b_fgate, ml_norm_g, w_out, ln2_g, ln2_b, ffn2_w_gate, ffn2_w_up, ffn2_w_down,
           ln3_g, ln3_b):
    depth = ln1_g.shape[0]
    assert depth == 1, "the meta-token shortcut below is only valid for a single layer"
    bsz, seq, d = x.shape
    assert seq % ROW_TILE == 0 and seq % ML_CHUNK == 0 and seq % Q_TILE == 0
    assert Q_TILE == GT_CHUNK and ML_CHUNK == GT_CHUNK and KEY_CHUNK == GT_CHUNK and ROW_TILE % GT_CHUNK == 0
    alpha = (2 * depth) ** 0.25
    topk = min(TOPK_MAX, seq // 4)

    row2 = lambda p: p[0].reshape(1, -1).astype(F32)
    bf = lambda w: w[0].astype(BF16)

    w_t = jnp.swapaxes(w_in[0], 0, 1)
    o_qa, o_ckv, o_qi, o_ki, o_wi = 0, 512, 640, 896, 960
    o_qk, o_v, o_o, o_ig, o_fg, o_end = 964, 1476, 1988, 2500, 2504, 2508
    wa = jnp.concatenate([w_t[o_qa:o_wi], w_t[o_wi:o_qk], w_t[o_ig:o_end],
                          jnp.zeros((LANES - GATE_END, d), F32)], axis=0).astype(BF16)
    wm = w_t[o_qk:o_ig].astype(BF16)
    gbias = jnp.concatenate([jnp.zeros((GATE_I0,), F32), b_igate[0], b_fgate[0],
                             jnp.zeros((LANES - GATE_END,), F32)]).reshape(1, LANES)
    wuk_bd = jnp.stack([_block_diag(w_uk[0][2 * p:2 * p + 2]) for p in range(HEAD_PAIRS)]).astype(BF16)
    wuv_bd = _block_diag(w_uv[0]).astype(BF16)
    kvg = row2(kv_norm_g)
    convw = conv_w[0].astype(F32)

    h1, h1_meta = _ffn_ln_call(x.reshape(bsz * seq, d), meta_tokens.astype(F32), ffn1_w_gate[0], ffn1_w_up[0],
                               ffn1_w_down[0], row2(ln1_g), row2(ln1_b), alpha, FFN_TILE)
    zero_tail = jnp.zeros((CONV_HIST, MLQK_WIDTH), F32)
    (_, m_ckv, _, m_kidx, m_qk, m_v, _, m_gates, m_tail) = _inproj_call(
        h1_meta[None], zero_tail, wa, wm, wuk_bd, kvg, convw, gbias, N_META)

    (qlat, ckv, qidx, kidx, qk, _, og, gates, _, gates_t, ckv_t, v_t, q_t) = _inproj_call(
        h1.reshape(bsz, seq, d), m_tail[0], wa, wm, wuk_bd, kvg, convw, gbias, ROW_TILE)

    nchunks = seq // KEY_CHUNK
    ckv_c = ckv.reshape(bsz, nchunks, KEY_CHUNK, KV_LATENT)
    kidx_c = kidx.reshape(bsz, nchunks, KEY_CHUNK, IDX_DIM)
    y_att = _dsa_call(qlat, qidx, gates_t, ckv_c, ckv_t, kidx_c,
                      m_ckv[0], m_ckv[0].T, m_kidx[0], wuv_bd, topk)

    lane = jnp.arange(LANES)
    pad_gate = jnp.where((lane >= GATE_I0) & (lane < GATE_F0), NEG_BIG, 0.0).astype(F32)
    mg = jnp.concatenate([m_gates[0], jnp.broadcast_to(pad_gate, (ML_CHUNK - N_META, LANES))], axis=0)
    gate_lanes = slice(GATE_W0, GATE_W0 + GATE_ROWS)
    m_qk_pad = _pad_rows(m_qk[0], ML_CHUNK)
    y_ml = _mlstm_call(qk, q_t, v_t, og, gates, gates_t,
                       m_qk_pad, m_qk_pad[:, :MLQK_WIDTH // 2].T, _pad_rows(m_v[0], ML_CHUNK).T, mg, mg[:, gate_lanes].T,
                       row2(ml_norm_g))

    out = _out_ffn_ln_call(
        y_att.reshape(bsz * seq, ATT_WIDTH), y_ml.reshape(bsz * seq, ML_WIDTH), h1, bf(w_out),
        row2(ln2_g), row2(ln2_b), ffn2_w_gate[0], ffn2_w_up[0], ffn2_w_down[0], row2(ln3_g), row2(ln3_b),
        alpha, FFN_TILE)
    return out.reshape(bsz, seq, d)
```

```python
import functools

import jax
import jax.numpy as jnp
from jax import lax
from jax.experimental import pallas as pl
from jax.experimental.pallas import tpu as pltpu

F32 = jnp.float32
BF16 = jnp.bfloat16

N_META = 16
ATT_HEADS = 8
ATT_HEAD_DIM = 64
KV_LATENT = 128
IDX_HEADS = 4
IDX_DIM = 64
TOPK_MAX = 256
ML_HEADS = 4
ML_V_DIM = 128
ML_QK_DIM = 64
CONV_WIDTH = 4
GATE_SOFTCAP = 15.0
M_INIT = -1e30
LN_EPS = 1e-5
NEG_BIG = -1e30
LOG2_E = 1.4426950408889634

LANES = 128
SUBLANES = 8
VMEM_LIMIT_BYTES = 56 * 1024 * 1024

FF_CHUNK = 256
ROW_TILE = 512
FFN_TILE = 512
Q_TILE = 256
ATT_GROUP = 128
KEY_CHUNK = 256
N_BISECT = 16
REDUCE_ROWS = 32
ML_BATCH = 2
ML_CHUNK = 256


def _dot(a, b):
    return jnp.dot(a, b, preferred_element_type=F32)


def _layer_norm(z, g, b):
    mu = jnp.mean(z, axis=-1, keepdims=True)
    zc = z - mu
    var = jnp.mean(zc * zc, axis=-1, keepdims=True)
    return zc * lax.rsqrt(var + LN_EPS) * g + b


def _sigmoid(x):
    return 1.0 / (1.0 + jnp.exp(-x))


def _swiglu_chunk(xb, wg_c, wu_c, wd_c):
    g = _dot(xb, wg_c)
    u = _dot(xb, wu_c)
    return _dot((g * _sigmoid(g) * u).astype(BF16), wd_c)


def _ffn_ln(x, wg_s, wu_s, wd_s, g, b, alpha):
    xb = x.astype(BF16)
    acc = jnp.zeros(x.shape, F32)
    for c in range(wg_s.shape[0]):
        acc = acc + _swiglu_chunk(xb, wg_s[c], wu_s[c], wd_s[c])
    return _layer_norm(alpha * x + 0.5 * acc, g, b)


def _stage_ffn_weights(step, wg_ref, wu_ref, wd_ref, wg_s, wu_s, wd_s):
    wg_s[step] = wg_ref[...].astype(BF16)
    wu_s[step] = wu_ref[...].astype(BF16)
    wd_s[step] = wd_ref[...].astype(BF16)


def _ffn_weight_specs(d, d_ff):
    n = d_ff // FF_CHUNK
    col = pl.BlockSpec((d, FF_CHUNK), lambda i: (0, jnp.minimum(i, n - 1)))
    row = pl.BlockSpec((FF_CHUNK, d), lambda i: (jnp.minimum(i, n - 1), 0))
    scratch = [pltpu.VMEM((n, d, FF_CHUNK), BF16), pltpu.VMEM((n, d, FF_CHUNK), BF16),
               pltpu.VMEM((n, FF_CHUNK, d), BF16)]
    return n, [col, col, row], scratch


def _staged_tile_index(i, n_stage):
    return jnp.maximum(i - (n_stage - 1), 0)


def _ffn_ln_kernel(x_ref, meta_ref, wg_ref, wu_ref, wd_ref, g_ref, b_ref, o_ref, ometa_ref,
                   wg_s, wu_s, wd_s, macc_ref, acc0_ref, *, alpha, n_stage):
    i = pl.program_id(0)

    @pl.when(i == 0)
    def _():
        macc_ref[...] = jnp.zeros(macc_ref.shape, F32)
        acc0_ref[...] = jnp.zeros(acc0_ref.shape, F32)

    @pl.when(i < n_stage)
    def _():
        _stage_ffn_weights(i, wg_ref, wu_ref, wd_ref, wg_s, wu_s, wd_s)
        macc_ref[...] += _swiglu_chunk(meta_ref[...].astype(BF16), wg_s[i], wu_s[i], wd_s[i])
        acc0_ref[...] += _swiglu_chunk(x_ref[...].astype(BF16), wg_s[i], wu_s[i], wd_s[i])

    @pl.when(i == n_stage - 1)
    def _():
        ometa_ref[...] = _layer_norm(alpha * meta_ref[...] + 0.5 * macc_ref[...], g_ref[...], b_ref[...])
        o_ref[...] = _layer_norm(alpha * x_ref[...] + 0.5 * acc0_ref[...], g_ref[...], b_ref[...])

    @pl.when(i >= n_stage)
    def _():
        o_ref[...] = _ffn_ln(x_ref[...], wg_s, wu_s, wd_s, g_ref[...], b_ref[...], alpha)


def _resident(shape):
    return pl.BlockSpec(shape, lambda *_: (0,) * len(shape), pipeline_mode=pl.Buffered(1))


def _ffn_ln_call(x, meta, wg, wu, wd, g, b, alpha, tm):
    rows, d = x.shape
    n_stage, w_specs, w_scratch = _ffn_weight_specs(d, wg.shape[1])
    row_spec = pl.BlockSpec((tm, d), lambda i: (_staged_tile_index(i, n_stage), 0))
    return pl.pallas_call(
        functools.partial(_ffn_ln_kernel, alpha=alpha, n_stage=n_stage),
        grid=(n_stage - 1 + rows // tm,),
        in_specs=[row_spec, _resident(meta.shape)] + w_specs + [_resident(g.shape), _resident(b.shape)],
        out_specs=[row_spec, pl.BlockSpec(meta.shape, lambda i: (0, 0))],
        out_shape=[jax.ShapeDtypeStruct((rows, d), F32), jax.ShapeDtypeStruct(meta.shape, F32)],
        scratch_shapes=w_scratch + [pltpu.VMEM(meta.shape, F32), pltpu.VMEM((tm, d), F32)],
        compiler_params=pltpu.CompilerParams(
            dimension_semantics=("arbitrary",), vmem_limit_bytes=VMEM_LIMIT_BYTES),
        name="ffn_ln",
    )(x, meta, wg, wu, wd, g, b)


ATT_WIDTH = ATT_HEADS * ATT_HEAD_DIM
IDX_WIDTH = IDX_HEADS * IDX_DIM
MLQK_WIDTH = 2 * ML_HEADS * ML_QK_DIM
ML_WIDTH = ML_HEADS * ML_V_DIM
LAT_WIDTH = ATT_HEADS * KV_LATENT
CONV_HIST = SUBLANES
GATE_W0 = IDX_DIM
GATE_I0, GATE_F0, GATE_END = GATE_W0 + IDX_HEADS, GATE_W0 + IDX_HEADS + ML_HEADS, GATE_W0 + IDX_HEADS + 2 * ML_HEADS
GATE_ROWS = 2 * SUBLANES
GT_CHUNK = 256
HEAD_PAIRS = ATT_HEADS // 2


def _inproj_kernel(h_ref, tail_ref, wa_ref, wm_ref, wuk_ref, kvg_ref, convw_ref, gbias_ref,
                   qlat_ref, ckv_ref, qidx_ref, kidx_ref, qk_ref, v_ref, og_ref, gates_ref, tailout_ref, *rest):
    carry_ref, wa_s, wm_s = rest[-3:]
    tm = h_ref.shape[1]

    @pl.when((pl.program_id(0) == 0) & (pl.program_id(1) == 0))
    def _():
        wa_s[...] = wa_ref[...].astype(F32).T.astype(BF16)
        wm_s[...] = wm_ref[...].astype(F32).T.astype(BF16)

    @pl.when(pl.program_id(1) == 0)
    def _():
        carry_ref[...] = tail_ref[...]

    xb = h_ref[0].astype(BF16)

    pa = _dot(xb, wa_s[...])
    q_a = pa[:, :ATT_WIDTH].astype(BF16)
    c0 = ATT_WIDTH
    ckv = pa[:, c0:c0 + KV_LATENT]
    c1 = c0 + KV_LATENT
    ckv = ckv * lax.rsqrt(jnp.mean(ckv * ckv, axis=-1, keepdims=True) + LN_EPS) * kvg_ref[...]
    ckv_ref[0] = ckv.astype(BF16)
    qidx_ref[0] = pa[:, c1:c1 + IDX_WIDTH].astype(BF16)
    c2 = c1 + IDX_WIDTH
    kidx_ref[0] = pa[:, c2:c2 + IDX_DIM].astype(BF16)
    pair_in, pair_out = 2 * ATT_HEAD_DIM, 2 * KV_LATENT
    for p in range(HEAD_PAIRS):
        ql = _dot(q_a[:, p * pair_in:(p + 1) * pair_in], wuk_ref[p])
        qlat_ref[0, :, p * pair_out:(p + 1) * pair_out] = (ql * (ATT_HEAD_DIM ** -0.5 * LOG2_E)).astype(BF16)

    pm = _dot(xb, wm_s[...])
    qk_raw = pm[:, :MLQK_WIDTH]
    v_ref[0] = pm[:, MLQK_WIDTH:MLQK_WIDTH + ML_WIDTH].astype(BF16)
    og_ref[0] = _sigmoid(pm[:, MLQK_WIDTH + ML_WIDTH:])

    ext = jnp.concatenate([carry_ref[...], qk_raw], axis=0)
    cw = convw_ref[...]
    conv = jnp.zeros_like(qk_raw)
    for j in range(CONV_WIDTH):
        s0 = CONV_HIST - (CONV_WIDTH - 1) + j
        conv = conv + ext[s0:s0 + tm] * cw[j:j + 1]
    act = conv * _sigmoid(conv)
    half = MLQK_WIDTH // 2
    qk_ref[0, :, :half] = act[:, :half].astype(BF16)
    qk_ref[0, :, half:] = (act[:, half:] * (ML_QK_DIM ** -0.5)).astype(BF16)
    carry_ref[...] = qk_raw[tm - CONV_HIST:]
    tailout_ref[0] = qk_raw[tm - CONV_HIST:]

    gr = pa[:, c2 + IDX_DIM - GATE_W0:]
    lane = lax.broadcasted_iota(jnp.int32, gr.shape, 1)
    sc = GATE_SOFTCAP * jnp.tanh((gr + gbias_ref[...]) / GATE_SOFTCAP)
    lf = -(jnp.maximum(-sc, 0.0) + jnp.log1p(jnp.exp(-jnp.abs(sc))))
    w_scaled = gr * (IDX_HEADS ** -0.5 * IDX_DIM ** -0.5)
    gates = jnp.where((lane < GATE_W0) | (lane >= GATE_END), 0.0,
                      jnp.where(lane < GATE_I0, w_scaled, jnp.where(lane < GATE_F0, sc, lf)))
    gates_ref[0] = gates
    if len(rest) == 7:
        gt_ref, ckvt_ref, vt_ref, qt_ref = rest[:4]
        gates_t = gates.T[GATE_W0:GATE_W0 + GATE_ROWS]
        ckv_t = ckv.T.astype(BF16)
        v_t = pm[:, MLQK_WIDTH:MLQK_WIDTH + ML_WIDTH].T.astype(BF16)
        q_t = act[:, :half].T.astype(BF16)
        for j in range(tm // GT_CHUNK):
            piece = slice(j * GT_CHUNK, (j + 1) * GT_CHUNK)
            gt_ref[0, j] = gates_t[:, piece]
            ckvt_ref[0, j] = ckv_t[:, piece]
            vt_ref[0, j] = v_t[:, piece]
            qt_ref[0, j] = q_t[:, piece]


def _inproj_call(h, tail, wa, wm, wuk_bd, kvg, convw, gbias, tm):
    bn, rows, d = h.shape
    nblk = rows // tm
    emit_gt = tm % GT_CHUNK == 0

    def row_spec(width):
        return pl.BlockSpec((1, tm, width), lambda b, j: (b, j, 0))

    outs = [
        (LAT_WIDTH, BF16), (KV_LATENT, BF16), (IDX_WIDTH, BF16), (IDX_DIM, BF16),
        (MLQK_WIDTH, BF16), (ML_WIDTH, BF16), (ML_WIDTH, F32), (LANES, F32),
    ]
    out_shape = [jax.ShapeDtypeStruct((bn, rows, w), dt) for w, dt in outs]
    out_specs = [row_spec(w) for w, _ in outs]
    out_shape.append(jax.ShapeDtypeStruct((bn, CONV_HIST, MLQK_WIDTH), F32))
    out_specs.append(pl.BlockSpec((1, CONV_HIST, MLQK_WIDTH), lambda b, j: (b, 0, 0)))
    if emit_gt:
        per_tile = tm // GT_CHUNK
        for height, dt in ((GATE_ROWS, F32), (KV_LATENT, BF16), (ML_WIDTH, BF16), (MLQK_WIDTH // 2, BF16)):
            out_shape.append(jax.ShapeDtypeStruct((bn, rows // GT_CHUNK, height, GT_CHUNK), dt))
            out_specs.append(pl.BlockSpec((1, per_tile, height, GT_CHUNK), lambda b, j: (b, j, 0, 0)))
    return pl.pallas_call(
        _inproj_kernel,
        grid=(bn, nblk),
        in_specs=[
            row_spec(d),
            _resident(tail.shape), _resident(wa.shape), _resident(wm.shape),
            _resident(wuk_bd.shape), _resident(kvg.shape), _resident(convw.shape), _resident(gbias.shape),
        ],
        out_specs=out_specs,
        out_shape=out_shape,
        scratch_shapes=[pltpu.VMEM((CONV_HIST, MLQK_WIDTH), F32),
                        pltpu.VMEM(wa.shape[::-1], BF16), pltpu.VMEM(wm.shape[::-1], BF16)],
        compiler_params=pltpu.CompilerParams(
            dimension_semantics=("arbitrary", "arbitrary"), vmem_limit_bytes=VMEM_LIMIT_BYTES),
        name="in_proj",
    )(h, tail, wa, wm, wuk_bd, kvg, convw, gbias)


def _dsa_kernel(qlat_ref, qidx_ref, wrow_ref, ckv_ref, ckvt_ref, kidx_ref, mckv_ref, mckvt_ref, mkidx_ref,
                wuv_ref, y_ref, s_ref, acc_ref, p_ref, *, topk):
    _, kc, tq = s_ref.shape
    i = pl.program_id(1)
    nch = ((i + 1) * tq + kc - 1) // kc
    qreal = i * tq + lax.broadcasted_iota(jnp.int32, (1, tq), 1)
    kf = float(topk)

    wrow = wrow_ref[0, 0]
    qidx = qidx_ref[0]
    q_idx_t = qidx.astype(F32).T
    q_idx_all_t = jnp.concatenate([q_idx_t[h * IDX_DIM:(h + 1) * IDX_DIM] for h in range(IDX_HEADS)],
                                  axis=1).astype(BF16)
    wi = [wrow[h:h + 1, :] for h in range(IDX_HEADS)]

    def scores(k_rows):
        lg = _dot(k_rows, q_idx_all_t)
        sc = jnp.zeros((k_rows.shape[0], tq), F32)
        for h in range(IDX_HEADS):
            sc = sc + jnp.maximum(lg[:, h * tq:(h + 1) * tq], 0.0) * wi[h]
        return sc

    s_meta = scores(mkidx_ref[...])

    def score_chunk(c, lo, hi):
        sc = scores(kidx_ref[0, c])
        valid = c * kc + lax.broadcasted_iota(jnp.int32, (kc, tq), 0) <= qreal
        s_ref[c] = jnp.where(valid, sc, -jnp.inf)
        groups = (kc // REDUCE_ROWS, REDUCE_ROWS, tq)
        lo = jnp.minimum(lo, jnp.min(sc.reshape(groups), axis=0))
        hi = jnp.maximum(hi, jnp.max(sc.reshape(groups), axis=0))
        return lo, hi

    def score_pair(c2, carry):
        lo, hi = score_chunk(2 * c2, *carry)
        return score_chunk(jnp.minimum(2 * c2 + 1, nch - 1), lo, hi)

    lo, hi = lax.fori_loop(0, (nch + 1) // 2, score_pair,
                           (jnp.full((REDUCE_ROWS, tq), jnp.inf, F32), jnp.full((REDUCE_ROWS, tq), -jnp.inf, F32)))
    lo = jnp.minimum(jnp.min(lo, axis=0, keepdims=True), jnp.min(s_meta, axis=0, keepdims=True))
    hi = jnp.maximum(jnp.max(hi, axis=0, keepdims=True), jnp.max(s_meta, axis=0, keepdims=True))

    def key_reduce(reduce, combine, per_chunk, init):
        def body(c, acc):
            x = per_chunk(s_ref[c]).reshape(kc // REDUCE_ROWS, REDUCE_ROWS, tq)
            return combine(acc, reduce(x, axis=0))
        acc = lax.fori_loop(0, nch, body, jnp.full((REDUCE_ROWS, tq), init, F32))
        return combine(reduce(acc, axis=0, keepdims=True), reduce(per_chunk(s_meta), axis=0, keepdims=True))

    def count(pred):
        return key_reduce(jnp.sum, jnp.add, lambda sc: jnp.where(pred(sc), 1.0, 0.0), 0.0)

    def max_where(pred):
        return key_reduce(jnp.max, jnp.maximum, lambda sc: jnp.where(pred(sc), sc, -jnp.inf), -jnp.inf)

    def bisect(_, carry):
        lo, hi = carry
        mid = 0.5 * lo + 0.5 * hi
        up = count(lambda sc: sc > mid) >= kf
        return jnp.where(up, mid, lo), jnp.where(up, hi, mid)

    lo, hi = lax.fori_loop(0, N_BISECT, bisect, (lo, hi))

    n_valid = (qreal + (N_META + 1)).astype(F32)
    small = n_valid <= kf
    cand = max_where(lambda sc: sc <= hi)
    n_ge = count(lambda sc: sc >= cand)
    done = jnp.where(small | (n_ge >= kf), 1.0, 0.0)

    def not_finished(state):
        return jnp.min(state[1]) < 0.5

    def step_down(state):
        cand, done, _ = state
        nxt = jnp.where(done > 0.5, cand, max_where(lambda sc: sc < cand))
        n_ge = count(lambda sc: sc >= nxt)
        return nxt, jnp.where(n_ge >= kf, 1.0, done), n_ge

    cand, _, n_ge = lax.while_loop(not_finished, step_down, (cand, done, n_ge))
    thr = jnp.where(small, -jnp.inf, cand)
    n_eq = count(lambda sc: sc == thr)
    need = jnp.where(small, 0.0, kf - (n_ge - n_eq))
    ranked_ties = jnp.max(jnp.where(n_eq > need, 1.0, 0.0)) > 0.5

    qlat = qlat_ref[0]
    n_groups = tq // ATT_GROUP
    onehot = (lax.broadcasted_iota(jnp.int32, (ATT_GROUP, ATT_GROUP), 0)
              == lax.broadcasted_iota(jnp.int32, (ATT_GROUP, ATT_GROUP), 1)).astype(BF16)
    qlat32 = qlat.astype(F32)
    q_aug_t = [jnp.concatenate(
        [jnp.concatenate([qlat32[g * ATT_GROUP:(g + 1) * ATT_GROUP, h * KV_LATENT:(h + 1) * KV_LATENT].T
                          for h in range(ATT_HEADS)], axis=1).astype(BF16),
         jnp.concatenate([onehot] * ATT_HEADS, axis=1)], axis=0)
        for g in range(n_groups)]
    hq = ATT_HEADS * ATT_GROUP

    def lower_tri(n):
        return (lax.broadcasted_iota(jnp.int32, (n, n), 1) <= lax.broadcasted_iota(jnp.int32, (n, n), 0)).astype(BF16)

    def attention(ranked):
        def mask_bias(sc, eq_seen):
            if not ranked:
                return jnp.where(sc >= thr, 0.0, NEG_BIG).astype(BF16), eq_seen
            n = sc.shape[0]
            eq = sc == thr
            rank = _dot(lower_tri(n), jnp.where(eq, 1.0, 0.0).astype(BF16)) + eq_seen
            keep = (sc > thr) | (eq & (rank <= need))
            return jnp.where(keep, 0.0, NEG_BIG).astype(BF16), rank[n - 1:n, :]

        def logits(g, kv, bias):
            s = _dot(kv, q_aug_t[g][:KV_LATENT])
            b = bias[:, g * ATT_GROUP:(g + 1) * ATT_GROUP].astype(F32)
            return jnp.concatenate([s[:, h * ATT_GROUP:(h + 1) * ATT_GROUP] + b for h in range(ATT_HEADS)], axis=1)

        def fold_in(g, c_prev, a_prev):
            acc_ref[g] = a_prev * acc_ref[g] + _dot(ckvt_ref[0, c_prev], p_ref[g])

        def attend(c, carry):
            bias, eq_seen = mask_bias(s_ref[c], carry[0])
            out = [eq_seen]
            for g in range(n_groups):
                m, l, a_prev = carry[1 + 3 * g:4 + 3 * g]
                fold_in(g, c - 1, a_prev)
                s = logits(g, ckv_ref[0, c], bias)
                m_new = jnp.maximum(m, jnp.max(s, axis=0, keepdims=True))
                a = jnp.exp2(m - m_new)
                p = jnp.exp2(s - m_new)
                p_ref[g] = p.astype(BF16)
                out += [m_new, a * l + jnp.sum(p, axis=0, keepdims=True), a]
            return tuple(out)

        bias_m, eq_seen = mask_bias(s_meta, jnp.zeros((1, tq), F32))
        bias_0, eq_seen = mask_bias(s_ref[0], eq_seen)
        init = [eq_seen]
        for g in range(n_groups):
            s_m = logits(g, mckv_ref[...], bias_m)
            s_0 = logits(g, ckv_ref[0, 0], bias_0)
            m = jnp.maximum(jnp.max(s_m, axis=0, keepdims=True), jnp.max(s_0, axis=0, keepdims=True))
            p_m = jnp.exp2(s_m - m)
            p_0 = jnp.exp2(s_0 - m)
            acc_ref[g] = _dot(mckvt_ref[...], p_m.astype(BF16))
            p_ref[g] = p_0.astype(BF16)
            init += [m, jnp.sum(p_m, axis=0, keepdims=True) + jnp.sum(p_0, axis=0, keepdims=True),
                     jnp.ones((1, hq), F32)]
        carry = lax.fori_loop(1, nch, attend, tuple(init))
        rows = []
        for g in range(n_groups):
            _, l, a_last = carry[1 + 3 * g:4 + 3 * g]
            fold_in(g, nch - 1, a_last)
            o_t = (acc_ref[g] / l).T
            rows.append(jnp.concatenate([o_t[h * ATT_GROUP:(h + 1) * ATT_GROUP] for h in range(ATT_HEADS)], axis=1))
        return _dot(jnp.concatenate(rows, axis=0).astype(BF16), wuv_ref[...]).astype(BF16)

    y_ref[0] = lax.cond(ranked_ties, lambda: attention(True), lambda: attention(False))


def _dsa_call(qlat, qidx, wrow, ckv_c, ckvt_c, kidx_c, m_ckv, m_ckvt, m_kidx, wuv_bd, topk):
    bn, rows, _ = qlat.shape
    nchunks, kc = ckv_c.shape[1], ckv_c.shape[2]
    nq = rows // Q_TILE

    def q_spec(width):
        return pl.BlockSpec((1, Q_TILE, width), lambda b, i: (b, i, 0))

    def k_spec(a):
        return pl.BlockSpec((1,) + a.shape[1:], lambda b, i: (b, 0, 0, 0))

    return pl.pallas_call(
        functools.partial(_dsa_kernel, topk=topk),
        grid=(bn, nq),
        in_specs=[q_spec(LAT_WIDTH), q_spec(IDX_WIDTH),
                  pl.BlockSpec((1, 1, GATE_ROWS, Q_TILE), lambda b, i: (b, i, 0, 0)),
                  k_spec(ckv_c), k_spec(ckvt_c), k_spec(kidx_c),
                  _resident(m_ckv.shape), _resident(m_ckvt.shape), _resident(m_kidx.shape), _resident(wuv_bd.shape)],
        out_specs=q_spec(ATT_WIDTH),
        out_shape=jax.ShapeDtypeStruct((bn, rows, ATT_WIDTH), BF16),
        scratch_shapes=[pltpu.VMEM((nchunks, kc, Q_TILE), F32),
                        pltpu.VMEM((Q_TILE // ATT_GROUP, KV_LATENT, ATT_HEADS * ATT_GROUP), F32),
                        pltpu.VMEM((Q_TILE // ATT_GROUP, kc, ATT_HEADS * ATT_GROUP), BF16)],
        compiler_params=pltpu.CompilerParams(
            dimension_semantics=("arbitrary", "arbitrary"), vmem_limit_bytes=VMEM_LIMIT_BYTES),
        name="dsa",
    )(qlat, qidx, wrow, ckv_c, ckvt_c, kidx_c, m_ckv, m_ckvt, m_kidx, wuv_bd)


def _split3(x):
    hi = x.astype(BF16)
    r = x - hi.astype(F32)
    mid = r.astype(BF16)
    lo = (r - mid.astype(F32)).astype(BF16)
    return hi, mid, lo


ML_EXT = ML_V_DIM + 16


def _mlstm_chunk(qk, qt, vt, g, gt, state):
    L = qk.shape[0]
    s_idx = lax.broadcasted_iota(jnp.int32, (L, L), 0)
    t_idx = lax.broadcasted_iota(jnp.int32, (L, L), 1)
    causal = s_idx <= t_idx
    b_cols = sum(_dot((t_idx <= s_idx).astype(BF16), part) for part in _split3(g))
    b_rows = sum(_dot(part, causal.astype(BF16)) for part in _split3(gt))
    ones_blk = jnp.where(lax.broadcasted_iota(jnp.int32, (ML_EXT - ML_V_DIM, L), 0) == 0, 1.0, 0.0).astype(BF16)
    kq = ML_HEADS * ML_QK_DIM

    outs, new_state = [], []
    for h in range(ML_HEADS):
        ce, m_prev = state[h]
        c_col = g[:, GATE_I0 + h:GATE_I0 + h + 1] - b_cols[:, GATE_F0 + h:GATE_F0 + h + 1]
        b_row = b_rows[GATE_F0 - GATE_W0 + h:GATE_F0 - GATE_W0 + h + 1, :]
        ig_row = gt[GATE_I0 - GATE_W0 + h:GATE_I0 - GATE_W0 + h + 1, :]
        qt_h = qt[h * ML_QK_DIM:(h + 1) * ML_QK_DIM, :]
        kh = qk[:, kq + h * ML_QK_DIM:kq + (h + 1) * ML_QK_DIM]
        vt_ext = jnp.concatenate([vt[h * ML_V_DIM:(h + 1) * ML_V_DIM, :], ones_blk], axis=0)

        d_t = jnp.where(causal, c_col + b_row, -jnp.inf)
        inter = b_row + m_prev
        m_t = jnp.maximum(jnp.max(d_t, axis=0, keepdims=True), inter)
        w_inter = jnp.exp(inter - m_t)
        s_t = _dot(kh, qt_h) * jnp.exp(d_t - m_t)
        r = _dot(vt_ext, s_t.astype(BF16)) + _dot(ce.astype(BF16), qt_h) * w_inter
        num = r[:ML_V_DIM]
        den = r[ML_V_DIM:ML_V_DIM + 1]
        hh = num / jnp.maximum(jnp.abs(den), jnp.exp(-m_t))
        mu = jnp.mean(hh, axis=0, keepdims=True)
        hc = hh - mu
        var = jnp.mean(hc * hc, axis=0, keepdims=True)
        outs.append((hc * lax.rsqrt(var + LN_EPS)).T)

        b_end = b_row[:, L - 1:L]
        g_row = b_end - b_row + ig_row
        m_new = jnp.maximum(b_end + m_prev, jnp.max(g_row, axis=1, keepdims=True))
        decay = jnp.exp(b_end + m_prev - m_new)
        weighted = (vt_ext.astype(F32) * jnp.exp(g_row - m_new)).astype(BF16)
        new_state.append((decay * ce + _dot(weighted, kh), m_new))
    return outs, new_state


def _mlstm_kernel(qk_ref, qt_ref, vt_ref, og_ref, gates_ref, gt_ref, mqk_ref, mqt_ref, mvt_ref, mgates_ref, mgt_ref,
                  ng_ref, y_ref,
                  ce0_ref, m0_ref):
    L = mqk_ref.shape[0]
    n_chunks = qk_ref.shape[1] // L
    norm_g = ng_ref[...]

    @pl.when(pl.program_id(0) == 0)
    def _():
        state = [(jnp.zeros((ML_EXT, ML_QK_DIM), F32), jnp.full((1, 1), M_INIT, F32)) for _ in range(ML_HEADS)]
        _, state = _mlstm_chunk(mqk_ref[...], mqt_ref[...], mvt_ref[...], mgates_ref[...], mgt_ref[...], state)
        for h in range(ML_HEADS):
            ce0_ref[h] = state[h][0]
            m0_ref[h] = jnp.broadcast_to(state[h][1], m0_ref.shape[1:])

    n_b = qk_ref.shape[0]
    state = [(ce0_ref[h], m0_ref[h][0:1, 0:1]) for h in range(ML_HEADS)] * n_b

    def body(c, flat):
        rows = pl.ds(pl.multiple_of(c * L, L), L)
        new_flat = []
        for b in range(n_b):
            state = [(flat[2 * (b * ML_HEADS + h)], flat[2 * (b * ML_HEADS + h) + 1]) for h in range(ML_HEADS)]
            outs, state = _mlstm_chunk(qk_ref[b, rows, :], qt_ref[b, c], vt_ref[b, c], gates_ref[b, rows, :],
                                       gt_ref[b, c], state)
            y = jnp.concatenate(outs, axis=1) * norm_g * og_ref[b, rows, :]
            y_ref[b, rows, :] = y.astype(BF16)
            new_flat += [x for pair in state for x in pair]
        return tuple(new_flat)

    lax.fori_loop(0, n_chunks, body, tuple(x for pair in state for x in pair))


def _mlstm_call(qk, qt_c, vt_c, og, gates, gt_c, mqk, mqt, mvt, mgates, mgt, norm_g):
    bn, rows, _ = qk.shape
    nb = ML_BATCH if bn % ML_BATCH == 0 else 1

    def b_spec(a):
        return pl.BlockSpec((nb,) + a.shape[1:], lambda b: (b,) + (0,) * (a.ndim - 1))

    consts = (mqk, mqt, mvt, mgates, mgt, norm_g)
    return pl.pallas_call(
        _mlstm_kernel,
        grid=(bn // nb,),
        in_specs=[b_spec(qk), b_spec(qt_c), b_spec(vt_c), b_spec(og), b_spec(gates), b_spec(gt_c)]
        + [_resident(c.shape) for c in consts],
        out_specs=pl.BlockSpec((nb, rows, ML_WIDTH), lambda b: (b, 0, 0)),
        out_shape=jax.ShapeDtypeStruct((bn, rows, ML_WIDTH), BF16),
        scratch_shapes=[pltpu.VMEM((ML_HEADS, ML_EXT, ML_QK_DIM), F32), pltpu.VMEM((ML_HEADS, SUBLANES, LANES), F32)],
        compiler_params=pltpu.CompilerParams(
            dimension_semantics=("arbitrary",), vmem_limit_bytes=VMEM_LIMIT_BYTES),
        name="mlstm",
    )(qk, qt_c, vt_c, og, gates, gt_c, *consts)


def _out_ffn_ln_kernel(ya_ref, ym_ref, h_ref, wo_ref, g2_ref, b2_ref, wg_ref, wu_ref, wd_ref, g3_ref, b3_ref,
                       o_ref, wg_s, wu_s, wd_s, h2_ref, acc0_ref, *, alpha, n_stage):
    i = pl.program_id(0)

    def mixed():
        mix = _dot(ya_ref[...], wo_ref[:ATT_WIDTH, :]) + _dot(ym_ref[...], wo_ref[ATT_WIDTH:, :])
        return _layer_norm(alpha * h_ref[...] + mix, g2_ref[...], b2_ref[...])

    @pl.when(i == 0)
    def _():
        h2_ref[...] = mixed()
        acc0_ref[...] = jnp.zeros(acc0_ref.shape, F32)

    @pl.when(i < n_stage)
    def _():
        _stage_ffn_weights(i, wg_ref, wu_ref, wd_ref, wg_s, wu_s, wd_s)
        acc0_ref[...] += _swiglu_chunk(h2_ref[...].astype(BF16), wg_s[i], wu_s[i], wd_s[i])

    @pl.when(i == n_stage - 1)
    def _():
        o_ref[...] = _layer_norm(alpha * h2_ref[...] + 0.5 * acc0_ref[...], g3_ref[...], b3_ref[...])

    @pl.when(i >= n_stage)
    def _():
        o_ref[...] = _ffn_ln(mixed(), wg_s, wu_s, wd_s, g3_ref[...], b3_ref[...], alpha)


def _out_ffn_ln_call(ya, ym, h, wo, g2, b2, wg, wu, wd, g3, b3, alpha, tm):
    rows, d = h.shape
    n_stage, w_specs, w_scratch = _ffn_weight_specs(d, wg.shape[1])

    def row_spec(width):
        return pl.BlockSpec((tm, width), lambda i: (_staged_tile_index(i, n_stage), 0))

    return pl.pallas_call(
        functools.partial(_out_ffn_ln_kernel, alpha=alpha, n_stage=n_stage),
        grid=(n_stage - 1 + rows // tm,),
        in_specs=[row_spec(ya.shape[1]), row_spec(ym.shape[1]), row_spec(d),
                  _resident(wo.shape), _resident(g2.shape), _resident(b2.shape)] + w_specs
        + [_resident(g3.shape), _resident(b3.shape)],
        out_specs=row_spec(d),
        out_shape=jax.ShapeDtypeStruct((rows, d), F32),
        scratch_shapes=w_scratch + [pltpu.VMEM((tm, d), F32), pltpu.VMEM((tm, d), F32)],
        compiler_params=pltpu.CompilerParams(
            dimension_semantics=("arbitrary",), vmem_limit_bytes=VMEM_LIMIT_BYTES),
        name="out_ffn_ln",
    )(ya, ym, h, wo, g2, b2, wg, wu, wd, g3, b3)


def _block_diag(w):
    nh, a, b = w.shape
    eye = jnp.eye(nh, dtype=w.dtype)
    return (eye[:, None, :, None] * w[:, :, None, :]).reshape(nh * a, nh * b)


def _pad_rows(a, rows, value=0.0):
    return jnp.pad(a, ((0, rows - a.shape[0]), (0, 0)), constant_values=value)


def kernel(x, meta_tokens, ln1_g, ln1_b, ffn1_w_gate, ffn1_w_up, ffn1_w_down, w_in, w_uk, w_uv, kv_norm_g,
           conv_w, b_igate, b_fgate, ml_norm_g, w_out, ln2_g, ln2_b, ffn2_w_gate, ffn2_w_up, ffn2_w_down,
           ln3_g, ln3_b):
    depth = ln1_g.shape[0]
    assert depth == 1, "the meta-token shortcut below is only valid for a single layer"
    bsz, seq, d = x.shape
    assert seq % ROW_TILE == 0 and seq % ML_CHUNK == 0 and seq % Q_TILE == 0
    assert Q_TILE == GT_CHUNK and ML_CHUNK == GT_CHUNK and KEY_CHUNK == GT_CHUNK and ROW_TILE % GT_CHUNK == 0
    alpha = (2 * depth) ** 0.25
    topk = min(TOPK_MAX, seq // 4)

    row2 = lambda p: p[0].reshape(1, -1).astype(F32)
    bf = lambda w: w[0].astype(BF16)

    w_t = jnp.swapaxes(w_in[0], 0, 1)
    o_qa, o_ckv, o_qi, o_ki, o_wi = 0, 512, 640, 896, 960
    o_qk, o_v, o_o, o_ig, o_fg, o_end = 964, 1476, 1988, 2500, 2504, 2508
    wa = jnp.concatenate([w_t[o_qa:o_wi], w_t[o_wi:o_qk], w_t[o_ig:o_end],
                          jnp.zeros((LANES - GATE_END, d), F32)], axis=0).astype(BF16)
    wm = w_t[o_qk:o_ig].astype(BF16)
    gbias = jnp.concatenate([jnp.zeros((GATE_I0,), F32), b_igate[0], b_fgate[0],
                             jnp.zeros((LANES - GATE_END,), F32)]).reshape(1, LANES)
    wuk_bd = jnp.stack([_block_diag(w_uk[0][2 * p:2 * p + 2]) for p in range(HEAD_PAIRS)]).astype(BF16)
    wuv_bd = _block_diag(w_uv[0]).astype(BF16)
    kvg = row2(kv_norm_g)
    convw = conv_w[0].astype(F32)

    h1, h1_meta = _ffn_ln_call(x.reshape(bsz * seq, d), meta_tokens.astype(F32), ffn1_w_gate[0], ffn1_w_up[0],
                               ffn1_w_down[0], row2(ln1_g), row2(ln1_b), alpha, FFN_TILE)
    zero_tail = jnp.zeros((CONV_HIST, MLQK_WIDTH), F32)
    (_, m_ckv, _, m_kidx, m_qk, m_v, _, m_gates, m_tail) = _inproj_call(
        h1_meta[None], zero_tail, wa, wm, wuk_bd, kvg, convw, gbias, N_META)

    (qlat, ckv, qidx, kidx, qk, _, og, gates, _, gates_t, ckv_t, v_t, q_t) = _inproj_call(
        h1.reshape(bsz, seq, d), m_tail[0], wa, wm, wuk_bd, kvg, convw, gbias, ROW_TILE)

    nchunks = seq // KEY_CHUNK
    ckv_c = ckv.reshape(bsz, nchunks, KEY_CHUNK, KV_LATENT)
    kidx_c = kidx.reshape(bsz, nchunks, KEY_CHUNK, IDX_DIM)
    y_att = _dsa_call(qlat, qidx, gates_t, ckv_c, ckv_t, kidx_c,
                      m_ckv[0], m_ckv[0].T, m_kidx[0], wuv_bd, topk)

    lane = jnp.arange(LANES)
    pad_gate = jnp.where((lane >= GATE_I0) & (lane < GATE_F0), NEG_BIG, 0.0).astype(F32)
    mg = jnp.concatenate([m_gates[0], jnp.broadcast_to(pad_gate, (ML_CHUNK - N_META, LANES))], axis=0)
    gate_lanes = slice(GATE_W0, GATE_W0 + GATE_ROWS)
    m_qk_pad = _pad_rows(m_qk[0], ML_CHUNK)
    y_ml = _mlstm_call(qk, q_t, v_t, og, gates, gates_t,
                       m_qk_pad, m_qk_pad[:, :MLQK_WIDTH // 2].T, _pad_rows(m_v[0], ML_CHUNK).T, mg, mg[:, gate_lanes].T,
                       row2(ml_norm_g))

    out = _out_ffn_ln_call(
        y_att.reshape(bsz * seq, ATT_WIDTH), y_ml.reshape(bsz * seq, ML_WIDTH), h1, bf(w_out),
        row2(ln2_g), row2(ln2_b), ffn2_w_gate[0], ffn2_w_up[0], ffn2_w_down[0], row2(ln3_g), row2(ln3_b),
        alpha, FFN_TILE)
    return out.reshape(bsz, seq, d)
```

```python
import functools

import jax
import jax.numpy as jnp
from jax import lax
from jax.experimental import pallas as pl
from jax.experimental.pallas import tpu as pltpu

F32 = jnp.float32
BF16 = jnp.bfloat16

N_META = 16
ATT_HEADS = 8
ATT_HEAD_DIM = 64
KV_LATENT = 128
IDX_HEADS = 4
IDX_DIM = 64
TOPK_MAX = 256
ML_HEADS = 4
ML_V_DIM = 128
ML_QK_DIM = 64
CONV_WIDTH = 4
GATE_SOFTCAP = 15.0
M_INIT = -1e30
LN_EPS = 1e-5
NEG_BIG = -1e30
LOG2_E = 1.4426950408889634

LANES = 128
SUBLANES = 8
VMEM_BYTES_V7X = 64 * 1024 * 1024
VMEM_LIMIT_BYTES = VMEM_BYTES_V7X * 7 // 8

FF_CHUNK = 256
ROW_TILE = 512
FFN_TILE = 512
Q_TILE = 256
ATT_GROUP = 128
KEY_CHUNK = 256
N_BISECT = 16
REDUCE_ROWS = 32
ML_BATCH = 2
ML_CHUNK = 256


def _dot(a, b):
    return jnp.dot(a, b, preferred_element_type=F32)


def _layer_norm(z, g, b):
    mu = jnp.mean(z, axis=-1, keepdims=True)
    zc = z - mu
    var = jnp.mean(zc * zc, axis=-1, keepdims=True)
    return zc * lax.rsqrt(var + LN_EPS) * g + b


def _sigmoid(x):
    return 1.0 / (1.0 + jnp.exp(-x))


def _swiglu_chunk(xb, wg_c, wu_c, wd_c):
    g = _dot(xb, wg_c)
    u = _dot(xb, wu_c)
    return _dot((g * _sigmoid(g) * u).astype(BF16), wd_c)


def _ffn_ln(x, wg_s, wu_s, wd_s, g, b, alpha):
    xb = x.astype(BF16)
    acc = jnp.zeros(x.shape, F32)
    for c in range(wg_s.shape[0]):
        acc = acc + _swiglu_chunk(xb, wg_s[c], wu_s[c], wd_s[c])
    return _layer_norm(alpha * x + 0.5 * acc, g, b)


def _stage_ffn_weights(step, wg_ref, wu_ref, wd_ref, wg_s, wu_s, wd_s):
    wg_s[step] = wg_ref[...].astype(BF16)
    wu_s[step] = wu_ref[...].astype(BF16)
    wd_s[step] = wd_ref[...].astype(BF16)


def _ffn_weight_specs(d, d_ff):
    n = d_ff // FF_CHUNK
    col = pl.BlockSpec((d, FF_CHUNK), lambda i: (0, jnp.minimum(i, n - 1)))
    row = pl.BlockSpec((FF_CHUNK, d), lambda i: (jnp.minimum(i, n - 1), 0))
    scratch = [pltpu.VMEM((n, d, FF_CHUNK), BF16), pltpu.VMEM((n, d, FF_CHUNK), BF16),
               pltpu.VMEM((n, FF_CHUNK, d), BF16)]
    return n, [col, col, row], scratch


def _staged_tile_index(i, n_stage):
    return jnp.maximum(i - (n_stage - 1), 0)


def _ffn_ln_kernel(x_ref, meta_ref, wg_ref, wu_ref, wd_ref, g_ref, b_ref, o_ref, ometa_ref,
                   wg_s, wu_s, wd_s, macc_ref, acc0_ref, *, alpha, n_stage):
    i = pl.program_id(0)

    @pl.when(i == 0)
    def _():
        macc_ref[...] = jnp.zeros(macc_ref.shape, F32)
        acc0_ref[...] = jnp.zeros(acc0_ref.shape, F32)

    @pl.when(i < n_stage)
    def _():
        _stage_ffn_weights(i, wg_ref, wu_ref, wd_ref, wg_s, wu_s, wd_s)
        macc_ref[...] += _swiglu_chunk(meta_ref[...].astype(BF16), wg_s[i], wu_s[i], wd_s[i])
        acc0_ref[...] += _swiglu_chunk(x_ref[...].astype(BF16), wg_s[i], wu_s[i], wd_s[i])

    @pl.when(i == n_stage - 1)
    def _():
        ometa_ref[...] = _layer_norm(alpha * meta_ref[...] + 0.5 * macc_ref[...], g_ref[...], b_ref[...])
        o_ref[...] = _layer_norm(alpha * x_ref[...] + 0.5 * acc0_ref[...], g_ref[...], b_ref[...])

    @pl.when(i >= n_stage)
    def _():
        o_ref[...] = _ffn_ln(x_ref[...], wg_s, wu_s, wd_s, g_ref[...], b_ref[...], alpha)


def _resident(shape):
    return pl.BlockSpec(shape, lambda *_: (0,) * len(shape), pipeline_mode=pl.Buffered(1))


def _ffn_ln_call(x, meta, wg, wu, wd, g, b, alpha, tm):
    rows, d = x.shape
    n_stage, w_specs, w_scratch = _ffn_weight_specs(d, wg.shape[1])
    row_spec = pl.BlockSpec((tm, d), lambda i: (_staged_tile_index(i, n_stage), 0))
    return pl.pallas_call(
        functools.partial(_ffn_ln_kernel, alpha=alpha, n_stage=n_stage),
        grid=(n_stage - 1 + rows // tm,),
        in_specs=[row_spec, _resident(meta.shape)] + w_specs + [_resident(g.shape), _resident(b.shape)],
        out_specs=[row_spec, pl.BlockSpec(meta.shape, lambda i: (0, 0))],
        out_shape=[jax.ShapeDtypeStruct((rows, d), F32), jax.ShapeDtypeStruct(meta.shape, F32)],
        scratch_shapes=w_scratch + [pltpu.VMEM(meta.shape, F32), pltpu.VMEM((tm, d), F32)],
        compiler_params=pltpu.CompilerParams(
            dimension_semantics=("arbitrary",), vmem_limit_bytes=VMEM_LIMIT_BYTES),
        name="ffn_ln",
    )(x, meta, wg, wu, wd, g, b)


ATT_WIDTH = ATT_HEADS * ATT_HEAD_DIM
IDX_WIDTH = IDX_HEADS * IDX_DIM
MLQK_WIDTH = 2 * ML_HEADS * ML_QK_DIM
ML_WIDTH = ML_HEADS * ML_V_DIM
LAT_WIDTH = ATT_HEADS * KV_LATENT
CONV_HIST = SUBLANES
GATE_W0 = IDX_DIM
GATE_I0, GATE_F0, GATE_END = GATE_W0 + IDX_HEADS, GATE_W0 + IDX_HEADS + ML_HEADS, GATE_W0 + IDX_HEADS + 2 * ML_HEADS
GATE_ROWS = 2 * SUBLANES
GT_CHUNK = 256
HEAD_PAIRS = ATT_HEADS // 2


def _inproj_kernel(h_ref, tail_ref, wa_ref, wm_ref, wuk_ref, kvg_ref, convw_ref, gbias_ref,
                   qlat_ref, ckv_ref, qidx_ref, kidx_ref, qk_ref, v_ref, og_ref, gates_ref, tailout_ref, *rest):
    carry_ref, wa_s, wm_s = rest[-3:]
    tm = h_ref.shape[1]

    @pl.when((pl.program_id(0) == 0) & (pl.program_id(1) == 0))
    def _():
        wa_s[...] = wa_ref[...].astype(F32).T.astype(BF16)
        wm_s[...] = wm_ref[...].astype(F32).T.astype(BF16)

    @pl.when(pl.program_id(1) == 0)
    def _():
        carry_ref[...] = tail_ref[...]

    xb = h_ref[0].astype(BF16)

    pa = _dot(xb, wa_s[...])
    q_a = pa[:, :ATT_WIDTH].astype(BF16)
    c0 = ATT_WIDTH
    ckv = pa[:, c0:c0 + KV_LATENT]
    c1 = c0 + KV_LATENT
    ckv = ckv * lax.rsqrt(jnp.mean(ckv * ckv, axis=-1, keepdims=True) + LN_EPS) * kvg_ref[...]
    ckv_ref[0] = ckv.astype(BF16)
    qidx_ref[0] = pa[:, c1:c1 + IDX_WIDTH].astype(BF16)
    c2 = c1 + IDX_WIDTH
    kidx_ref[0] = pa[:, c2:c2 + IDX_DIM].astype(BF16)
    pair_in, pair_out = 2 * ATT_HEAD_DIM, 2 * KV_LATENT
    for p in range(HEAD_PAIRS):
        ql = _dot(q_a[:, p * pair_in:(p + 1) * pair_in], wuk_ref[p])
        qlat_ref[0, :, p * pair_out:(p + 1) * pair_out] = (ql * (ATT_HEAD_DIM ** -0.5 * LOG2_E)).astype(BF16)

    pm = _dot(xb, wm_s[...])
    qk_raw = pm[:, :MLQK_WIDTH]
    v_ref[0] = pm[:, MLQK_WIDTH:MLQK_WIDTH + ML_WIDTH].astype(BF16)
    og_ref[0] = _sigmoid(pm[:, MLQK_WIDTH + ML_WIDTH:])

    ext = jnp.concatenate([carry_ref[...], qk_raw], axis=0)
    cw = convw_ref[...]
    conv = jnp.zeros_like(qk_raw)
    for j in range(CONV_WIDTH):
        s0 = CONV_HIST - (CONV_WIDTH - 1) + j
        conv = conv + ext[s0:s0 + tm] * cw[j:j + 1]
    act = conv * _sigmoid(conv)
    half = MLQK_WIDTH // 2
    qk_ref[0, :, :half] = act[:, :half].astype(BF16)
    qk_ref[0, :, half:] = (act[:, half:] * (ML_QK_DIM ** -0.5)).astype(BF16)
    carry_ref[...] = qk_raw[tm - CONV_HIST:]
    tailout_ref[0] = qk_raw[tm - CONV_HIST:]

    gr = pa[:, c2 + IDX_DIM - GATE_W0:]
    lane = lax.broadcasted_iota(jnp.int32, gr.shape, 1)
    sc = GATE_SOFTCAP * jnp.tanh((gr + gbias_ref[...]) / GATE_SOFTCAP)
    lf = -(jnp.maximum(-sc, 0.0) + jnp.log1p(jnp.exp(-jnp.abs(sc))))
    w_scaled = gr * (IDX_HEADS ** -0.5 * IDX_DIM ** -0.5)
    gates = jnp.where((lane < GATE_W0) | (lane >= GATE_END), 0.0,
                      jnp.where(lane < GATE_I0, w_scaled, jnp.where(lane < GATE_F0, sc, lf)))
    gates_ref[0] = gates
    if len(rest) == 7:
        gt_ref, ckvt_ref, vt_ref, qt_ref = rest[:4]
        gates_t = gates.T[GATE_W0:GATE_W0 + GATE_ROWS]
        ckv_t = ckv.T.astype(BF16)
        v_t = pm[:, MLQK_WIDTH:MLQK_WIDTH + ML_WIDTH].T.astype(BF16)
        q_t = act[:, :half].T.astype(BF16)
        for j in range(tm // GT_CHUNK):
            piece = slice(j * GT_CHUNK, (j + 1) * GT_CHUNK)
            gt_ref[0, j] = gates_t[:, piece]
            ckvt_ref[0, j] = ckv_t[:, piece]
            vt_ref[0, j] = v_t[:, piece]
            qt_ref[0, j] = q_t[:, piece]


def _inproj_call(h, tail, wa, wm, wuk_bd, kvg, convw, gbias, tm):
    bn, rows, d = h.shape
    nblk = rows // tm
    emit_gt = tm % GT_CHUNK == 0

    def row_spec(width):
        return pl.BlockSpec((1, tm, width), lambda b, j: (b, j, 0))

    outs = [
        (LAT_WIDTH, BF16), (KV_LATENT, BF16), (IDX_WIDTH, BF16), (IDX_DIM, BF16),
        (MLQK_WIDTH, BF16), (ML_WIDTH, BF16), (ML_WIDTH, F32), (LANES, F32),
    ]
    out_shape = [jax.ShapeDtypeStruct((bn, rows, w), dt) for w, dt in outs]
    out_specs = [row_spec(w) for w, _ in outs]
    out_shape.append(jax.ShapeDtypeStruct((bn, CONV_HIST, MLQK_WIDTH), F32))
    out_specs.append(pl.BlockSpec((1, CONV_HIST, MLQK_WIDTH), lambda b, j: (b, 0, 0)))
    if emit_gt:
        per_tile = tm // GT_CHUNK
        for height, dt in ((GATE_ROWS, F32), (KV_LATENT, BF16), (ML_WIDTH, BF16), (MLQK_WIDTH // 2, BF16)):
            out_shape.append(jax.ShapeDtypeStruct((bn, rows // GT_CHUNK, height, GT_CHUNK), dt))
            out_specs.append(pl.BlockSpec((1, per_tile, height, GT_CHUNK), lambda b, j: (b, j, 0, 0)))
    return pl.pallas_call(
        _inproj_kernel,
        grid=(bn, nblk),
        in_specs=[
            row_spec(d),
            _resident(tail.shape), _resident(wa.shape), _resident(wm.shape),
            _resident(wuk_bd.shape), _resident(kvg.shape), _resident(convw.shape), _resident(gbias.shape),
        ],
        out_specs=out_specs,
        out_shape=out_shape,
        scratch_shapes=[pltpu.VMEM((CONV_HIST, MLQK_WIDTH), F32),
                        pltpu.VMEM(wa.shape[::-1], BF16), pltpu.VMEM(wm.shape[::-1], BF16)],
        compiler_params=pltpu.CompilerParams(
            dimension_semantics=("arbitrary", "arbitrary"), vmem_limit_bytes=VMEM_LIMIT_BYTES),
        name="in_proj",
    )(h, tail, wa, wm, wuk_bd, kvg, convw, gbias)


def _dsa_kernel(qlat_ref, qidx_ref, wrow_ref, ckv_ref, ckvt_ref, kidx_ref, mckv_ref, mckvt_ref, mkidx_ref,
                wuv_ref, y_ref, s_ref, acc_ref, p_ref, *, topk):
    _, kc, tq = s_ref.shape
    i = pl.program_id(1)
    nch = ((i + 1) * tq + kc - 1) // kc
    qreal = i * tq + lax.broadcasted_iota(jnp.int32, (1, tq), 1)
    kf = float(topk)

    wrow = wrow_ref[0, 0]
    qidx = qidx_ref[0]
    q_idx_t = qidx.astype(F32).T
    q_idx_all_t = jnp.concatenate([q_idx_t[h * IDX_DIM:(h + 1) * IDX_DIM] for h in range(IDX_HEADS)],
                                  axis=1).astype(BF16)
    wi = [wrow[h:h + 1, :] for h in range(IDX_HEADS)]

    def scores(k_rows):
        lg = _dot(k_rows, q_idx_all_t)
        sc = jnp.zeros((k_rows.shape[0], tq), F32)
        for h in range(IDX_HEADS):
            sc = sc + jnp.maximum(lg[:, h * tq:(h + 1) * tq], 0.0) * wi[h]
        return sc

    s_meta = scores(mkidx_ref[...])

    def score_chunk(c, lo, hi):
        sc = scores(kidx_ref[0, c])
        valid = c * kc + lax.broadcasted_iota(jnp.int32, (kc, tq), 0) <= qreal
        s_ref[c] = jnp.where(valid, sc, -jnp.inf)
        groups = (kc // REDUCE_ROWS, REDUCE_ROWS, tq)
        lo = jnp.minimum(lo, jnp.min(sc.reshape(groups), axis=0))
        hi = jnp.maximum(hi, jnp.max(sc.reshape(groups), axis=0))
        return lo, hi

    def score_pair(c2, carry):
        lo, hi = score_chunk(2 * c2, *carry)
        return score_chunk(jnp.minimum(2 * c2 + 1, nch - 1), lo, hi)

    lo, hi = lax.fori_loop(0, (nch + 1) // 2, score_pair,
                           (jnp.full((REDUCE_ROWS, tq), jnp.inf, F32), jnp.full((REDUCE_ROWS, tq), -jnp.inf, F32)))
    lo = jnp.minimum(jnp.min(lo, axis=0, keepdims=True), jnp.min(s_meta, axis=0, keepdims=True))
    hi = jnp.maximum(jnp.max(hi, axis=0, keepdims=True), jnp.max(s_meta, axis=0, keepdims=True))

    def key_reduce(reduce, combine, per_chunk, init):
        def body(c, acc):
            x = per_chunk(s_ref[c]).reshape(kc // REDUCE_ROWS, REDUCE_ROWS, tq)
            return combine(acc, reduce(x, axis=0))
        acc = lax.fori_loop(0, nch, body, jnp.full((REDUCE_ROWS, tq), init, F32))
        return combine(reduce(acc, axis=0, keepdims=True), reduce(per_chunk(s_meta), axis=0, keepdims=True))

    def count(pred):
        return key_reduce(jnp.sum, jnp.add, lambda sc: jnp.where(pred(sc), 1.0, 0.0), 0.0)

    def max_where(pred):
        return key_reduce(jnp.max, jnp.maximum, lambda sc: jnp.where(pred(sc), sc, -jnp.inf), -jnp.inf)

    def bisect(_, carry):
        lo, hi = carry
        mid = 0.5 * lo + 0.5 * hi
        up = count(lambda sc: sc > mid) >= kf
        return jnp.where(up, mid, lo), jnp.where(up, hi, mid)

    lo, hi = lax.fori_loop(0, N_BISECT, bisect, (lo, hi))

    n_valid = (qreal + (N_META + 1)).astype(F32)
    small = n_valid <= kf
    cand = max_where(lambda sc: sc <= hi)
    n_ge = count(lambda sc: sc >= cand)
    done = jnp.where(small | (n_ge >= kf), 1.0, 0.0)

    def not_finished(state):
        return jnp.min(state[1]) < 0.5

    def step_down(state):
        cand, done, _ = state
        nxt = jnp.where(done > 0.5, cand, max_where(lambda sc: sc < cand))
        n_ge = count(lambda sc: sc >= nxt)
        return nxt, jnp.where(n_ge >= kf, 1.0, done), n_ge

    cand, _, n_ge = lax.while_loop(not_finished, step_down, (cand, done, n_ge))
    thr = jnp.where(small, -jnp.inf, cand)
    n_eq = count(lambda sc: sc == thr)
    need = jnp.where(small, 0.0, kf - (n_ge - n_eq))
    ranked_ties = jnp.max(jnp.where(n_eq > need, 1.0, 0.0)) > 0.5

    qlat = qlat_ref[0]
    n_groups = tq // ATT_GROUP
    onehot = (lax.broadcasted_iota(jnp.int32, (ATT_GROUP, ATT_GROUP), 0)
              == lax.broadcasted_iota(jnp.int32, (ATT_GROUP, ATT_GROUP), 1)).astype(BF16)
    qlat32 = qlat.astype(F32)
    q_aug_t = [jnp.concatenate(
        [jnp.concatenate([qlat32[g * ATT_GROUP:(g + 1) * ATT_GROUP, h * KV_LATENT:(h + 1) * KV_LATENT].T
                          for h in range(ATT_HEADS)], axis=1).astype(BF16),
         jnp.concatenate([onehot] * ATT_HEADS, axis=1)], axis=0)
        for g in range(n_groups)]
    hq = ATT_HEADS * ATT_GROUP

    def lower_tri(n):
        return (lax.broadcasted_iota(jnp.int32, (n, n), 1) <= lax.broadcasted_iota(jnp.int32, (n, n), 0)).astype(BF16)

    def attention(ranked):
        def mask_bias(sc, eq_seen):
            if not ranked:
                return jnp.where(sc >= thr, 0.0, NEG_BIG).astype(BF16), eq_seen
            n = sc.shape[0]
            eq = sc == thr
            rank = _dot(lower_tri(n), jnp.where(eq, 1.0, 0.0).astype(BF16)) + eq_seen
            keep = (sc > thr) | (eq & (rank <= need))
            return jnp.where(keep, 0.0, NEG_BIG).astype(BF16), rank[n - 1:n, :]

        def logits(g, kv, bias):
            k_aug = jnp.concatenate([kv, bias[:, g * ATT_GROUP:(g + 1) * ATT_GROUP]], axis=1)
            return _dot(k_aug, q_aug_t[g])

        def fold_in(g, c_prev, a_prev):
            acc_ref[g] = a_prev * acc_ref[g] + _dot(ckvt_ref[0, c_prev], p_ref[g])

        def attend(c, carry):
            bias, eq_seen = mask_bias(s_ref[c], carry[0])
            out = [eq_seen]
            for g in range(n_groups):
                m, l, a_prev = carry[1 + 3 * g:4 + 3 * g]
                fold_in(g, c - 1, a_prev)
                s = logits(g, ckv_ref[0, c], bias)
                m_new = jnp.maximum(m, jnp.max(s, axis=0, keepdims=True))
                a = jnp.exp2(m - m_new)
                p = jnp.exp2(s - m_new)
                p_ref[g] = p.astype(BF16)
                out += [m_new, a * l + jnp.sum(p, axis=0, keepdims=True), a]
            return tuple(out)

        bias_m, eq_seen = mask_bias(s_meta, jnp.zeros((1, tq), F32))
        bias_0, eq_seen = mask_bias(s_ref[0], eq_seen)
        init = [eq_seen]
        for g in range(n_groups):
            s_m = logits(g, mckv_ref[...], bias_m)
            s_0 = logits(g, ckv_ref[0, 0], bias_0)
            m = jnp.maximum(jnp.max(s_m, axis=0, keepdims=True), jnp.max(s_0, axis=0, keepdims=True))
            p_m = jnp.exp2(s_m - m)
            p_0 = jnp.exp2(s_0 - m)
            acc_ref[g] = _dot(mckvt_ref[...], p_m.astype(BF16))
            p_ref[g] = p_0.astype(BF16)
            init += [m, jnp.sum(p_m, axis=0, keepdims=True) + jnp.sum(p_0, axis=0, keepdims=True),
                     jnp.ones((1, hq), F32)]
        carry = lax.fori_loop(1, nch, attend, tuple(init))
        rows = []
        for g in range(n_groups):
            _, l, a_last = carry[1 + 3 * g:4 + 3 * g]
            fold_in(g, nch - 1, a_last)
            o_t = (acc_ref[g] / l).T
            rows.append(jnp.concatenate([o_t[h * ATT_GROUP:(h + 1) * ATT_GROUP] for h in range(ATT_HEADS)], axis=1))
        return _dot(jnp.concatenate(rows, axis=0).astype(BF16), wuv_ref[...]).astype(BF16)

    y_ref[0] = lax.cond(ranked_ties, lambda: attention(True), lambda: attention(False))


def _dsa_call(qlat, qidx, wrow, ckv_c, ckvt_c, kidx_c, m_ckv, m_ckvt, m_kidx, wuv_bd, topk):
    bn, rows, _ = qlat.shape
    nchunks, kc = ckv_c.shape[1], ckv_c.shape[2]
    nq = rows // Q_TILE

    def q_spec(width):
        return pl.BlockSpec((1, Q_TILE, width), lambda b, i: (b, i, 0))

    def k_spec(a):
        return pl.BlockSpec((1,) + a.shape[1:], lambda b, i: (b, 0, 0, 0))

    return pl.pallas_call(
        functools.partial(_dsa_kernel, topk=topk),
        grid=(bn, nq),
        in_specs=[q_spec(LAT_WIDTH), q_spec(IDX_WIDTH),
                  pl.BlockSpec((1, 1, GATE_ROWS, Q_TILE), lambda b, i: (b, i, 0, 0)),
                  k_spec(ckv_c), k_spec(ckvt_c), k_spec(kidx_c),
                  _resident(m_ckv.shape), _resident(m_ckvt.shape), _resident(m_kidx.shape), _resident(wuv_bd.shape)],
        out_specs=q_spec(ATT_WIDTH),
        out_shape=jax.ShapeDtypeStruct((bn, rows, ATT_WIDTH), BF16),
        scratch_shapes=[pltpu.VMEM((nchunks, kc, Q_TILE), F32),
                        pltpu.VMEM((Q_TILE // ATT_GROUP, KV_LATENT, ATT_HEADS * ATT_GROUP), F32),
                        pltpu.VMEM((Q_TILE // ATT_GROUP, kc, ATT_HEADS * ATT_GROUP), BF16)],
        compiler_params=pltpu.CompilerParams(
            dimension_semantics=("arbitrary", "arbitrary"), vmem_limit_bytes=VMEM_LIMIT_BYTES),
        name="dsa",
    )(qlat, qidx, wrow, ckv_c, ckvt_c, kidx_c, m_ckv, m_ckvt, m_kidx, wuv_bd)


def _split3(x):
    hi = x.astype(BF16)
    r = x - hi.astype(F32)
    mid = r.astype(BF16)
    lo = (r - mid.astype(F32)).astype(BF16)
    return hi, mid, lo


ML_EXT = ML_V_DIM + 16


def _mlstm_chunk(qk, qt, vt, g, gt, state):
    L = qk.shape[0]
    s_idx = lax.broadcasted_iota(jnp.int32, (L, L), 0)
    t_idx = lax.broadcasted_iota(jnp.int32, (L, L), 1)
    causal = s_idx <= t_idx
    b_cols = sum(_dot((t_idx <= s_idx).astype(BF16), part) for part in _split3(g))
    b_rows = sum(_dot(part, causal.astype(BF16)) for part in _split3(gt))
    ones_blk = jnp.where(lax.broadcasted_iota(jnp.int32, (ML_EXT - ML_V_DIM, L), 0) == 0, 1.0, 0.0).astype(BF16)
    kq = ML_HEADS * ML_QK_DIM

    outs, new_state = [], []
    for h in range(ML_HEADS):
        ce, m_prev = state[h]
        c_col = g[:, GATE_I0 + h:GATE_I0 + h + 1] - b_cols[:, GATE_F0 + h:GATE_F0 + h + 1]
        b_row = b_rows[GATE_F0 - GATE_W0 + h:GATE_F0 - GATE_W0 + h + 1, :]
        ig_row = gt[GATE_I0 - GATE_W0 + h:GATE_I0 - GATE_W0 + h + 1, :]
        qt_h = qt[h * ML_QK_DIM:(h + 1) * ML_QK_DIM, :]
        kh = qk[:, kq + h * ML_QK_DIM:kq + (h + 1) * ML_QK_DIM]
        vt_ext = jnp.concatenate([vt[h * ML_V_DIM:(h + 1) * ML_V_DIM, :], ones_blk], axis=0)

        d_t = jnp.where(causal, c_col + b_row, -jnp.inf)
        inter = b_row + m_prev
        m_t = jnp.maximum(jnp.max(d_t, axis=0, keepdims=True), inter)
        w_inter = jnp.exp(inter - m_t)
        s_t = _dot(kh, qt_h) * jnp.exp(d_t - m_t)
        r = _dot(vt_ext, s_t.astype(BF16)) + _dot(ce.astype(BF16), qt_h) * w_inter
        num = r[:ML_V_DIM]
        den = r[ML_V_DIM:ML_V_DIM + 1]
        hh = num / jnp.maximum(jnp.abs(den), jnp.exp(-m_t))
        mu = jnp.mean(hh, axis=0, keepdims=True)
        hc = hh - mu
        var = jnp.mean(hc * hc, axis=0, keepdims=True)
        outs.append((hc * lax.rsqrt(var + LN_EPS)).T)

        b_end = b_row[:, L - 1:L]
        g_row = b_end - b_row + ig_row
        m_new = jnp.maximum(b_end + m_prev, jnp.max(g_row, axis=1, keepdims=True))
        decay = jnp.exp(b_end + m_prev - m_new)
        weighted = (vt_ext.astype(F32) * jnp.exp(g_row - m_new)).astype(BF16)
        new_state.append((decay * ce + _dot(weighted, kh), m_new))
    return outs, new_state


def _mlstm_kernel(qk_ref, qt_ref, vt_ref, og_ref, gates_ref, gt_ref, mqk_ref, mqt_ref, mvt_ref, mgates_ref, mgt_ref,
                  ng_ref, y_ref,
                  ce0_ref, m0_ref):
    L = mqk_ref.shape[0]
    n_chunks = qk_ref.shape[1] // L
    norm_g = ng_ref[...]

    @pl.when(pl.program_id(0) == 0)
    def _():
        state = [(jnp.zeros((ML_EXT, ML_QK_DIM), F32), jnp.full((1, 1), M_INIT, F32)) for _ in range(ML_HEADS)]
        _, state = _mlstm_chunk(mqk_ref[...], mqt_ref[...], mvt_ref[...], mgates_ref[...], mgt_ref[...], state)
        for h in range(ML_HEADS):
            ce0_ref[h] = state[h][0]
            m0_ref[h] = jnp.broadcast_to(state[h][1], m0_ref.shape[1:])

    n_b = qk_ref.shape[0]
    state = [(ce0_ref[h], m0_ref[h][0:1, 0:1]) for h in range(ML_HEADS)] * n_b

    def body(c, flat):
        rows = pl.ds(pl.multiple_of(c * L, L), L)
        new_flat = []
        for b in range(n_b):
            state = [(flat[2 * (b * ML_HEADS + h)], flat[2 * (b * ML_HEADS + h) + 1]) for h in range(ML_HEADS)]
            outs, state = _mlstm_chunk(qk_ref[b, rows, :], qt_ref[b, c], vt_ref[b, c], gates_ref[b, rows, :],
                                       gt_ref[b, c], state)
            y = jnp.concatenate(outs, axis=1) * norm_g * og_ref[b, rows, :]
            y_ref[b, rows, :] = y.astype(BF16)
            new_flat += [x for pair in state for x in pair]
        return tuple(new_flat)

    lax.fori_loop(0, n_chunks, body, tuple(x for pair in state for x in pair))


def _mlstm_call(qk, qt_c, vt_c, og, gates, gt_c, mqk, mqt, mvt, mgates, mgt, norm_g):
    bn, rows, _ = qk.shape
    nb = ML_BATCH if bn % ML_BATCH == 0 else 1

    def b_spec(a):
        return pl.BlockSpec((nb,) + a.shape[1:], lambda b: (b,) + (0,) * (a.ndim - 1))

    consts = (mqk, mqt, mvt, mgates, mgt, norm_g)
    return pl.pallas_call(
        _mlstm_kernel,
        grid=(bn // nb,),
        in_specs=[b_spec(qk), b_spec(qt_c), b_spec(vt_c), b_spec(og), b_spec(gates), b_spec(gt_c)]
        + [_resident(c.shape) for c in consts],
        out_specs=pl.BlockSpec((nb, rows, ML_WIDTH), lambda b: (b, 0, 0)),
        out_shape=jax.ShapeDtypeStruct((bn, rows, ML_WIDTH), BF16),
        scratch_shapes=[pltpu.VMEM((ML_HEADS, ML_EXT, ML_QK_DIM), F32), pltpu.VMEM((ML_HEADS, SUBLANES, LANES), F32)],
        compiler_params=pltpu.CompilerParams(
            dimension_semantics=("arbitrary",), vmem_limit_bytes=VMEM_LIMIT_BYTES),
        name="mlstm",
    )(qk, qt_c, vt_c, og, gates, gt_c, *consts)


def _out_ffn_ln_kernel(ya_ref, ym_ref, h_ref, wo_ref, g2_ref, b2_ref, wg_ref, wu_ref, wd_ref, g3_ref, b3_ref,
                       o_ref, wg_s, wu_s, wd_s, h2_ref, acc0_ref, *, alpha, n_stage):
    i = pl.program_id(0)

    def mixed():
        mix = _dot(ya_ref[...], wo_ref[:ATT_WIDTH, :]) + _dot(ym_ref[...], wo_ref[ATT_WIDTH:, :])
        return _layer_norm(alpha * h_ref[...] + mix, g2_ref[...], b2_ref[...])

    @pl.when(i == 0)
    def _():
        h2_ref[...] = mixed()
        acc0_ref[...] = jnp.zeros(acc0_ref.shape, F32)

    @pl.when(i < n_stage)
    def _():
        _stage_ffn_weights(i, wg_ref, wu_ref, wd_ref, wg_s, wu_s, wd_s)
        acc0_ref[...] += _swiglu_chunk(h2_ref[...].astype(BF16), wg_s[i], wu_s[i], wd_s[i])

    @pl.when(i == n_stage - 1)
    def _():
        o_ref[...] = _layer_norm(alpha * h2_ref[...] + 0.5 * acc0_ref[...], g3_ref[...], b3_ref[...])

    @pl.when(i >= n_stage)
    def _():
        o_ref[...] = _ffn_ln(mixed(), wg_s, wu_s, wd_s, g3_ref[...], b3_ref[...], alpha)


def _out_ffn_ln_call(ya, ym, h, wo, g2, b2, wg, wu, wd, g3, b3, alpha, tm):
    rows, d = h.shape
    n_stage, w_specs, w_scratch = _ffn_weight_specs(d, wg.shape[1])

    def row_spec(width):
        return pl.BlockSpec((tm, width), lambda i: (_staged_tile_index(i, n_stage), 0))

    return pl.pallas_call(
        functools.partial(_out_ffn_ln_kernel, alpha=alpha, n_stage=n_stage),
        grid=(n_stage - 1 + rows // tm,),
        in_specs=[row_spec(ya.shape[1]), row_spec(ym.shape[1]), row_spec(d),
                  _resident(wo.shape), _resident(g2.shape), _resident(b2.shape)] + w_specs
        + [_resident(g3.shape), _resident(b3.shape)],
        out_specs=row_spec(d),
        out_shape=jax.ShapeDtypeStruct((rows, d), F32),
        scratch_shapes=w_scratch + [pltpu.VMEM((tm, d), F32), pltpu.VMEM((tm, d), F32)],
        compiler_params=pltpu.CompilerParams(
            dimension_semantics=("arbitrary",), vmem_limit_bytes=VMEM_LIMIT_BYTES),
        name="out_ffn_ln",
    )(ya, ym, h, wo, g2, b2, wg, wu, wd, g3, b3)


def _block_diag(w):
    nh, a, b = w.shape
    eye = jnp.eye(nh, dtype=w.dtype)
    return (eye[:, None, :, None] * w[:, :, None, :]).reshape(nh * a, nh * b)


def _pad_rows(a, rows, value=0.0):
    return jnp.pad(a, ((0, rows - a.shape[0]), (0, 0)), constant_values=value)


def kernel(x, meta_tokens, ln1_g, ln1_b, ffn1_w_gate, ffn1_w_up, ffn1_w_down, w_in, w_uk, w_uv, kv_norm_g,
           conv_w, b_igate, b_fgate, ml_norm_g, w_out, ln2_g, ln2_b, ffn2_w_gate, ffn2_w_up, ffn2_w_down,
           ln3_g, ln3_b):
    depth = ln1_g.shape[0]
    assert depth == 1, "the meta-token shortcut below is only valid for a single layer"
    bsz, seq, d = x.shape
    assert seq % ROW_TILE == 0 and seq % ML_CHUNK == 0 and seq % Q_TILE == 0
    assert Q_TILE == GT_CHUNK and ML_CHUNK == GT_CHUNK and KEY_CHUNK == GT_CHUNK and ROW_TILE % GT_CHUNK == 0
    alpha = (2 * depth) ** 0.25
    topk = min(TOPK_MAX, seq // 4)

    row2 = lambda p: p[0].reshape(1, -1).astype(F32)
    bf = lambda w: w[0].astype(BF16)

    w_t = jnp.swapaxes(w_in[0], 0, 1)
    sizes = (ATT_WIDTH, KV_LATENT, IDX_WIDTH, IDX_DIM, IDX_HEADS, MLQK_WIDTH, ML_WIDTH, ML_WIDTH, ML_HEADS, ML_HEADS)
    offs = [sum(sizes[:n]) for n in range(len(sizes) + 1)]
    assert w_t.shape[0] == offs[-1]
    wa = jnp.concatenate([w_t[offs[0]:offs[4]], w_t[offs[4]:offs[5]], w_t[offs[8]:offs[10]],
                          jnp.zeros((LANES - GATE_END, d), F32)], axis=0).astype(BF16)
    wm = w_t[offs[5]:offs[8]].astype(BF16)
    gbias = jnp.concatenate([jnp.zeros((GATE_I0,), F32), b_igate[0], b_fgate[0],
                             jnp.zeros((LANES - GATE_END,), F32)]).reshape(1, LANES)
    wuk_bd = jnp.stack([_block_diag(w_uk[0][2 * p:2 * p + 2]) for p in range(HEAD_PAIRS)]).astype(BF16)
    wuv_bd = _block_diag(w_uv[0]).astype(BF16)
    kvg = row2(kv_norm_g)
    convw = conv_w[0].astype(F32)

    h1, h1_meta = _ffn_ln_call(x.reshape(bsz * seq, d), meta_tokens.astype(F32), ffn1_w_gate[0], ffn1_w_up[0],
                               ffn1_w_down[0], row2(ln1_g), row2(ln1_b), alpha, FFN_TILE)
    zero_tail = jnp.zeros((CONV_HIST, MLQK_WIDTH), F32)
    (_, m_ckv, _, m_kidx, m_qk, m_v, _, m_gates, m_tail) = _inproj_call(
        h1_meta[None], zero_tail, wa, wm, wuk_bd, kvg, convw, gbias, N_META)

    (qlat, ckv, qidx, kidx, qk, _, og, gates, _, gates_t, ckv_t, v_t, q_t) = _inproj_call(
        h1.reshape(bsz, seq, d), m_tail[0], wa, wm, wuk_bd, kvg, convw, gbias, ROW_TILE)

    nchunks = seq // KEY_CHUNK
    ckv_c = ckv.reshape(bsz, nchunks, KEY_CHUNK, KV_LATENT)
    kidx_c = kidx.reshape(bsz, nchunks, KEY_CHUNK, IDX_DIM)
    y_att = _dsa_call(qlat, qidx, gates_t, ckv_c, ckv_t, kidx_c,
                      m_ckv[0], m_ckv[0].T, m_kidx[0], wuv_bd, topk)

    lane = jnp.arange(LANES)
    pad_gate = jnp.where((lane >= GATE_I0) & (lane < GATE_F0), NEG_BIG, 0.0).astype(F32)
    mg = jnp.concatenate([m_gates[0], jnp.broadcast_to(pad_gate, (ML_CHUNK - N_META, LANES))], axis=0)
    gate_lanes = slice(GATE_W0, GATE_W0 + GATE_ROWS)
    m_qk_pad = _pad_rows(m_qk[0], ML_CHUNK)
    y_ml = _mlstm_call(qk, q_t, v_t, og, gates, gates_t,
                       m_qk_pad, m_qk_pad[:, :MLQK_WIDTH // 2].T, _pad_rows(m_v[0], ML_CHUNK).T, mg, mg[:, gate_lanes].T,
                       row2(ml_norm_g))

    out = _out_ffn_ln_call(
        y_att.reshape(bsz * seq, ATT_WIDTH), y_ml.reshape(bsz * seq, ML_WIDTH), h1, bf(w_out),
        row2(ln2_g), row2(ln2_b), ffn2_w_gate[0], ffn2_w_up[0], ffn2_w_down[0], row2(ln3_g), row2(ln3_b),
        alpha, FFN_TILE)
    return out.reshape(bsz, seq, d)
```

```python
import functools

import jax
import jax.numpy as jnp
from jax import lax
from jax.experimental import pallas as pl
from jax.experimental.pallas import tpu as pltpu

F32 = jnp.float32
BF16 = jnp.bfloat16

N_META = 16
ATT_HEADS = 8
ATT_HEAD_DIM = 64
KV_LATENT = 128
IDX_HEADS = 4
IDX_DIM = 64
TOPK_MAX = 256
ML_HEADS = 4
ML_V_DIM = 128
ML_QK_DIM = 64
CONV_WIDTH = 4
GATE_SOFTCAP = 15.0
M_INIT = -1e30
LN_EPS = 1e-5
NEG_BIG = -1e30
LOG2_E = 1.4426950408889634

LANES = 128
SUBLANES = 8
VMEM_BYTES_V7X = 64 * 1024 * 1024
VMEM_LIMIT_BYTES = VMEM_BYTES_V7X * 7 // 8

FF_CHUNK = 256
ROW_TILE = 512
FFN_TILE = 512
Q_TILE = 256
ATT_GROUP = 128
KEY_CHUNK = 256
N_BISECT = 16
REDUCE_ROWS = 32
ML_BATCH = 2
ML_CHUNK = 256


def _dot(a, b):
    return jnp.dot(a, b, preferred_element_type=F32)


def _layer_norm(z, g, b):
    mu = jnp.mean(z, axis=-1, keepdims=True)
    zc = z - mu
    var = jnp.mean(zc * zc, axis=-1, keepdims=True)
    return zc * lax.rsqrt(var + LN_EPS) * g + b


def _sigmoid(x):
    return 1.0 / (1.0 + jnp.exp(-x))


def _swiglu_chunk(xb, wg_c, wu_c, wd_c):
    g = _dot(xb, wg_c)
    u = _dot(xb, wu_c)
    return _dot((g * _sigmoid(g) * u).astype(BF16), wd_c)


def _ffn_ln(x, wg_s, wu_s, wd_s, g, b, alpha):
    xb = x.astype(BF16)
    acc = jnp.zeros(x.shape, F32)
    for c in range(wg_s.shape[0]):
        acc = acc + _swiglu_chunk(xb, wg_s[c], wu_s[c], wd_s[c])
    return _layer_norm(alpha * x + 0.5 * acc, g, b)


def _stage_ffn_weights(step, wg_ref, wu_ref, wd_ref, wg_s, wu_s, wd_s):
    wg_s[step] = wg_ref[...].astype(BF16)
    wu_s[step] = wu_ref[...].astype(BF16)
    wd_s[step] = wd_ref[...].astype(BF16)


def _ffn_weight_specs(d, d_ff):
    n = d_ff // FF_CHUNK
    col = pl.BlockSpec((d, FF_CHUNK), lambda i: (0, jnp.minimum(i, n - 1)))
    row = pl.BlockSpec((FF_CHUNK, d), lambda i: (jnp.minimum(i, n - 1), 0))
    scratch = [pltpu.VMEM((n, d, FF_CHUNK), BF16), pltpu.VMEM((n, d, FF_CHUNK), BF16),
               pltpu.VMEM((n, FF_CHUNK, d), BF16)]
    return n, [col, col, row], scratch


def _staged_tile_index(i, n_stage):
    return jnp.maximum(i - (n_stage - 1), 0)


def _ffn_ln_kernel(x_ref, meta_ref, wg_ref, wu_ref, wd_ref, g_ref, b_ref, o_ref, ometa_ref,
                   wg_s, wu_s, wd_s, macc_ref, acc0_ref, *, alpha, n_stage):
    i = pl.program_id(0)

    @pl.when(i == 0)
    def _():
        macc_ref[...] = jnp.zeros(macc_ref.shape, F32)
        acc0_ref[...] = jnp.zeros(acc0_ref.shape, F32)

    @pl.when(i < n_stage)
    def _():
        _stage_ffn_weights(i, wg_ref, wu_ref, wd_ref, wg_s, wu_s, wd_s)
        macc_ref[...] += _swiglu_chunk(meta_ref[...].astype(BF16), wg_s[i], wu_s[i], wd_s[i])
        acc0_ref[...] += _swiglu_chunk(x_ref[...].astype(BF16), wg_s[i], wu_s[i], wd_s[i])

    @pl.when(i == n_stage - 1)
    def _():
        ometa_ref[...] = _layer_norm(alpha * meta_ref[...] + 0.5 * macc_ref[...], g_ref[...], b_ref[...])
        o_ref[...] = _layer_norm(alpha * x_ref[...] + 0.5 * acc0_ref[...], g_ref[...], b_ref[...])

    @pl.when(i >= n_stage)
    def _():
        o_ref[...] = _ffn_ln(x_ref[...], wg_s, wu_s, wd_s, g_ref[...], b_ref[...], alpha)


def _resident(shape):
    return pl.BlockSpec(shape, lambda *_: (0,) * len(shape), pipeline_mode=pl.Buffered(1))


def _ffn_ln_call(x, meta, wg, wu, wd, g, b, alpha, tm):
    rows, d = x.shape
    n_stage, w_specs, w_scratch = _ffn_weight_specs(d, wg.shape[1])
    row_spec = pl.BlockSpec((tm, d), lambda i: (_staged_tile_index(i, n_stage), 0))
    return pl.pallas_call(
        functools.partial(_ffn_ln_kernel, alpha=alpha, n_stage=n_stage),
        grid=(n_stage - 1 + rows // tm,),
        in_specs=[row_spec, _resident(meta.shape)] + w_specs + [_resident(g.shape), _resident(b.shape)],
        out_specs=[row_spec, pl.BlockSpec(meta.shape, lambda i: (0, 0))],
        out_shape=[jax.ShapeDtypeStruct((rows, d), F32), jax.ShapeDtypeStruct(meta.shape, F32)],
        scratch_shapes=w_scratch + [pltpu.VMEM(meta.shape, F32), pltpu.VMEM((tm, d), F32)],
        compiler_params=pltpu.CompilerParams(
            dimension_semantics=("arbitrary",), vmem_limit_bytes=VMEM_LIMIT_BYTES),
        name="ffn_ln",
    )(x, meta, wg, wu, wd, g, b)


ATT_WIDTH = ATT_HEADS * ATT_HEAD_DIM
IDX_WIDTH = IDX_HEADS * IDX_DIM
MLQK_WIDTH = 2 * ML_HEADS * ML_QK_DIM
ML_WIDTH = ML_HEADS * ML_V_DIM
LAT_WIDTH = ATT_HEADS * KV_LATENT
CONV_HIST = SUBLANES
GATE_W0 = IDX_DIM
GATE_I0, GATE_F0, GATE_END = GATE_W0 + IDX_HEADS, GATE_W0 + IDX_HEADS + ML_HEADS, GATE_W0 + IDX_HEADS + 2 * ML_HEADS
GATE_ROWS = 2 * SUBLANES
GT_CHUNK = 256
ATT_EXT = KV_LATENT + 2 * SUBLANES
HEAD_PAIRS = ATT_HEADS // 2


def _inproj_kernel(h_ref, tail_ref, wa_ref, wm_ref, wuk_ref, kvg_ref, convw_ref, gbias_ref,
                   qlat_ref, ckv_ref, qidx_ref, kidx_ref, qk_ref, v_ref, og_ref, gates_ref, tailout_ref, *rest):
    carry_ref, wa_s, wm_s = rest[-3:]
    tm = h_ref.shape[1]

    @pl.when((pl.program_id(0) == 0) & (pl.program_id(1) == 0))
    def _():
        wa_s[...] = wa_ref[...].astype(F32).T.astype(BF16)
        wm_s[...] = wm_ref[...].astype(F32).T.astype(BF16)

    @pl.when(pl.program_id(1) == 0)
    def _():
        carry_ref[...] = tail_ref[...]

    xb = h_ref[0].astype(BF16)

    pa = _dot(xb, wa_s[...])
    q_a = pa[:, :ATT_WIDTH].astype(BF16)
    c0 = ATT_WIDTH
    ckv = pa[:, c0:c0 + KV_LATENT]
    c1 = c0 + KV_LATENT
    ckv = ckv * lax.rsqrt(jnp.mean(ckv * ckv, axis=-1, keepdims=True) + LN_EPS) * kvg_ref[...]
    ckv_ref[0] = ckv.astype(BF16)
    qidx_ref[0] = pa[:, c1:c1 + IDX_WIDTH].astype(BF16)
    c2 = c1 + IDX_WIDTH
    kidx_ref[0] = pa[:, c2:c2 + IDX_DIM].astype(BF16)
    pair_in, pair_out = 2 * ATT_HEAD_DIM, 2 * KV_LATENT
    for p in range(HEAD_PAIRS):
        ql = _dot(q_a[:, p * pair_in:(p + 1) * pair_in], wuk_ref[p])
        qlat_ref[0, :, p * pair_out:(p + 1) * pair_out] = (ql * (ATT_HEAD_DIM ** -0.5 * LOG2_E)).astype(BF16)

    pm = _dot(xb, wm_s[...])
    qk_raw = pm[:, :MLQK_WIDTH]
    v_ref[0] = pm[:, MLQK_WIDTH:MLQK_WIDTH + ML_WIDTH].astype(BF16)
    og_ref[0] = _sigmoid(pm[:, MLQK_WIDTH + ML_WIDTH:])

    ext = jnp.concatenate([carry_ref[...], qk_raw], axis=0)
    cw = convw_ref[...]
    conv = jnp.zeros_like(qk_raw)
    for j in range(CONV_WIDTH):
        s0 = CONV_HIST - (CONV_WIDTH - 1) + j
        conv = conv + ext[s0:s0 + tm] * cw[j:j + 1]
    act = conv * _sigmoid(conv)
    half = MLQK_WIDTH // 2
    qk_ref[0, :, :half] = act[:, :half].astype(BF16)
    qk_ref[0, :, half:] = (act[:, half:] * (ML_QK_DIM ** -0.5)).astype(BF16)
    carry_ref[...] = qk_raw[tm - CONV_HIST:]
    tailout_ref[0] = qk_raw[tm - CONV_HIST:]

    gr = pa[:, c2 + IDX_DIM - GATE_W0:]
    lane = lax.broadcasted_iota(jnp.int32, gr.shape, 1)
    sc = GATE_SOFTCAP * jnp.tanh((gr + gbias_ref[...]) / GATE_SOFTCAP)
    lf = -(jnp.maximum(-sc, 0.0) + jnp.log1p(jnp.exp(-jnp.abs(sc))))
    w_scaled = gr * (IDX_HEADS ** -0.5 * IDX_DIM ** -0.5)
    gates = jnp.where((lane < GATE_W0) | (lane >= GATE_END), 0.0,
                      jnp.where(lane < GATE_I0, w_scaled, jnp.where(lane < GATE_F0, sc, lf)))
    gates_ref[0] = gates
    if len(rest) == 7:
        gt_ref, ckvt_ref, vt_ref, qt_ref = rest[:4]
        gates_t = gates.T[GATE_W0:GATE_W0 + GATE_ROWS]
        ckv_t = ckv.T.astype(BF16)
        v_t = pm[:, MLQK_WIDTH:MLQK_WIDTH + ML_WIDTH].T.astype(BF16)
        q_t = act[:, :half].T.astype(BF16)
        ones_blk = jnp.where(lax.broadcasted_iota(jnp.int32, (ATT_EXT - KV_LATENT, GT_CHUNK), 0) == 0,
                             1.0, 0.0).astype(BF16)
        for j in range(tm // GT_CHUNK):
            piece = slice(j * GT_CHUNK, (j + 1) * GT_CHUNK)
            gt_ref[0, j] = gates_t[:, piece]
            ckvt_ref[0, j] = jnp.concatenate([ckv_t[:, piece], ones_blk], axis=0)
            vt_ref[0, j] = v_t[:, piece]
            qt_ref[0, j] = q_t[:, piece]


def _inproj_call(h, tail, wa, wm, wuk_bd, kvg, convw, gbias, tm):
    bn, rows, d = h.shape
    nblk = rows // tm
    emit_gt = tm % GT_CHUNK == 0

    def row_spec(width):
        return pl.BlockSpec((1, tm, width), lambda b, j: (b, j, 0))

    outs = [
        (LAT_WIDTH, BF16), (KV_LATENT, BF16), (IDX_WIDTH, BF16), (IDX_DIM, BF16),
        (MLQK_WIDTH, BF16), (ML_WIDTH, BF16), (ML_WIDTH, F32), (LANES, F32),
    ]
    out_shape = [jax.ShapeDtypeStruct((bn, rows, w), dt) for w, dt in outs]
    out_specs = [row_spec(w) for w, _ in outs]
    out_shape.append(jax.ShapeDtypeStruct((bn, CONV_HIST, MLQK_WIDTH), F32))
    out_specs.append(pl.BlockSpec((1, CONV_HIST, MLQK_WIDTH), lambda b, j: (b, 0, 0)))
    if emit_gt:
        per_tile = tm // GT_CHUNK
        for height, dt in ((GATE_ROWS, F32), (ATT_EXT, BF16), (ML_WIDTH, BF16), (MLQK_WIDTH // 2, BF16)):
            out_shape.append(jax.ShapeDtypeStruct((bn, rows // GT_CHUNK, height, GT_CHUNK), dt))
            out_specs.append(pl.BlockSpec((1, per_tile, height, GT_CHUNK), lambda b, j: (b, j, 0, 0)))
    return pl.pallas_call(
        _inproj_kernel,
        grid=(bn, nblk),
        in_specs=[
            row_spec(d),
            _resident(tail.shape), _resident(wa.shape), _resident(wm.shape),
            _resident(wuk_bd.shape), _resident(kvg.shape), _resident(convw.shape), _resident(gbias.shape),
        ],
        out_specs=out_specs,
        out_shape=out_shape,
        scratch_shapes=[pltpu.VMEM((CONV_HIST, MLQK_WIDTH), F32),
                        pltpu.VMEM(wa.shape[::-1], BF16), pltpu.VMEM(wm.shape[::-1], BF16)],
        compiler_params=pltpu.CompilerParams(
            dimension_semantics=("arbitrary", "arbitrary"), vmem_limit_bytes=VMEM_LIMIT_BYTES),
        name="in_proj",
    )(h, tail, wa, wm, wuk_bd, kvg, convw, gbias)


def _dsa_kernel(qlat_ref, qidx_ref, wrow_ref, ckv_ref, ckvt_ref, kidx_ref, mckv_ref, mckvt_ref, mkidx_ref,
                wuv_ref, y_ref, s_ref, acc_ref, p_ref, *, topk):
    _, kc, tq = s_ref.shape
    i = pl.program_id(1)
    nch = ((i + 1) * tq + kc - 1) // kc
    qreal = i * tq + lax.broadcasted_iota(jnp.int32, (1, tq), 1)
    kf = float(topk)

    wrow = wrow_ref[0, 0]
    qidx = qidx_ref[0]
    q_idx_t = qidx.astype(F32).T
    q_idx_all_t = jnp.concatenate([q_idx_t[h * IDX_DIM:(h + 1) * IDX_DIM] for h in range(IDX_HEADS)],
                                  axis=1).astype(BF16)
    wi = [wrow[h:h + 1, :] for h in range(IDX_HEADS)]

    def scores(k_rows):
        lg = _dot(k_rows, q_idx_all_t)
        sc = jnp.zeros((k_rows.shape[0], tq), F32)
        for h in range(IDX_HEADS):
            sc = sc + jnp.maximum(lg[:, h * tq:(h + 1) * tq], 0.0) * wi[h]
        return sc

    s_meta = scores(mkidx_ref[...])

    def score_chunk(c, lo, hi):
        sc = scores(kidx_ref[0, c])
        valid = c * kc + lax.broadcasted_iota(jnp.int32, (kc, tq), 0) <= qreal
        s_ref[c] = jnp.where(valid, sc, -jnp.inf)
        groups = (kc // REDUCE_ROWS, REDUCE_ROWS, tq)
        lo = jnp.minimum(lo, jnp.min(sc.reshape(groups), axis=0))
        hi = jnp.maximum(hi, jnp.max(sc.reshape(groups), axis=0))
        return lo, hi

    def score_pair(c2, carry):
        lo, hi = score_chunk(2 * c2, *carry)
        return score_chunk(jnp.minimum(2 * c2 + 1, nch - 1), lo, hi)

    lo, hi = lax.fori_loop(0, (nch + 1) // 2, score_pair,
                           (jnp.full((REDUCE_ROWS, tq), jnp.inf, F32), jnp.full((REDUCE_ROWS, tq), -jnp.inf, F32)))
    lo = jnp.minimum(jnp.min(lo, axis=0, keepdims=True), jnp.min(s_meta, axis=0, keepdims=True))
    hi = jnp.maximum(jnp.max(hi, axis=0, keepdims=True), jnp.max(s_meta, axis=0, keepdims=True))

    def key_reduce(reduce, combine, per_chunk, init):
        def body(c, acc):
            x = per_chunk(s_ref[c]).reshape(kc // REDUCE_ROWS, REDUCE_ROWS, tq)
            return combine(acc, reduce(x, axis=0))
        acc = lax.fori_loop(0, nch, body, jnp.full((REDUCE_ROWS, tq), init, F32))
        return combine(reduce(acc, axis=0, keepdims=True), reduce(per_chunk(s_meta), axis=0, keepdims=True))

    def count(pred):
        return key_reduce(jnp.sum, jnp.add, lambda sc: jnp.where(pred(sc), 1.0, 0.0), 0.0)

    def max_where(pred):
        return key_reduce(jnp.max, jnp.maximum, lambda sc: jnp.where(pred(sc), sc, -jnp.inf), -jnp.inf)

    def bisect(_, carry):
        lo, hi = carry
        mid = 0.5 * lo + 0.5 * hi
        up = count(lambda sc: sc > mid) >= kf
        return jnp.where(up, mid, lo), jnp.where(up, hi, mid)

    lo, hi = lax.fori_loop(0, N_BISECT, bisect, (lo, hi))

    n_valid = (qreal + (N_META + 1)).astype(F32)
    small = n_valid <= kf
    cand = max_where(lambda sc: sc <= hi)
    n_ge = count(lambda sc: sc >= cand)
    done = jnp.where(small | (n_ge >= kf), 1.0, 0.0)

    def not_finished(state):
        return jnp.min(state[1]) < 0.5

    def step_down(state):
        cand, done, _ = state
        nxt = jnp.where(done > 0.5, cand, max_where(lambda sc: sc < cand))
        n_ge = count(lambda sc: sc >= nxt)
        return nxt, jnp.where(n_ge >= kf, 1.0, done), n_ge

    cand, _, n_ge = lax.while_loop(not_finished, step_down, (cand, done, n_ge))
    thr = jnp.where(small, -jnp.inf, cand)
    n_eq = count(lambda sc: sc == thr)
    need = jnp.where(small, 0.0, kf - (n_ge - n_eq))
    ranked_ties = jnp.max(jnp.where(n_eq > need, 1.0, 0.0)) > 0.5

    qlat = qlat_ref[0]
    n_groups = tq // ATT_GROUP
    onehot = (lax.broadcasted_iota(jnp.int32, (ATT_GROUP, ATT_GROUP), 0)
              == lax.broadcasted_iota(jnp.int32, (ATT_GROUP, ATT_GROUP), 1)).astype(BF16)
    qlat32 = qlat.astype(F32)
    q_aug_t = [jnp.concatenate(
        [jnp.concatenate([qlat32[g * ATT_GROUP:(g + 1) * ATT_GROUP, h * KV_LATENT:(h + 1) * KV_LATENT].T
                          for h in range(ATT_HEADS)], axis=1).astype(BF16),
         jnp.concatenate([onehot] * ATT_HEADS, axis=1)], axis=0)
        for g in range(n_groups)]
    hq = ATT_HEADS * ATT_GROUP

    def lower_tri(n):
        return (lax.broadcasted_iota(jnp.int32, (n, n), 1) <= lax.broadcasted_iota(jnp.int32, (n, n), 0)).astype(BF16)

    def attention(ranked):
        def mask_bias(sc, eq_seen):
            if not ranked:
                return jnp.where(sc >= thr, 0.0, NEG_BIG).astype(BF16), eq_seen
            n = sc.shape[0]
            eq = sc == thr
            rank = _dot(lower_tri(n), jnp.where(eq, 1.0, 0.0).astype(BF16)) + eq_seen
            keep = (sc > thr) | (eq & (rank <= need))
            return jnp.where(keep, 0.0, NEG_BIG).astype(BF16), rank[n - 1:n, :]

        def logits(g, kv, bias):
            k_aug = jnp.concatenate([kv, bias[:, g * ATT_GROUP:(g + 1) * ATT_GROUP]], axis=1)
            return _dot(k_aug, q_aug_t[g])

        def fold_in(g, c_prev, a_prev):
            acc_ref[g] = a_prev * acc_ref[g] + _dot(ckvt_ref[0, c_prev], p_ref[g])

        def attend(c, carry):
            bias, eq_seen = mask_bias(s_ref[c], carry[0])
            out = [eq_seen]
            for g in range(n_groups):
                m, a_prev = carry[1 + 2 * g:3 + 2 * g]
                fold_in(g, c - 1, a_prev)
                s = logits(g, ckv_ref[0, c], bias)
                m_new = jnp.maximum(m, jnp.max(s, axis=0, keepdims=True))
                p_ref[g] = jnp.exp2(s - m_new).astype(BF16)
                out += [m_new, jnp.exp2(m - m_new)]
            return tuple(out)

        bias_m, eq_seen = mask_bias(s_meta, jnp.zeros((1, tq), F32))
        bias_0, eq_seen = mask_bias(s_ref[0], eq_seen)
        init = [eq_seen]
        for g in range(n_groups):
            s_m = logits(g, mckv_ref[...], bias_m)
            s_0 = logits(g, ckv_ref[0, 0], bias_0)
            m = jnp.maximum(jnp.max(s_m, axis=0, keepdims=True), jnp.max(s_0, axis=0, keepdims=True))
            acc_ref[g] = _dot(mckvt_ref[...], jnp.exp2(s_m - m).astype(BF16))
            p_ref[g] = jnp.exp2(s_0 - m).astype(BF16)
            init += [m, jnp.ones((1, hq), F32)]
        carry = lax.fori_loop(1, nch, attend, tuple(init))
        rows = []
        for g in range(n_groups):
            fold_in(g, nch - 1, carry[2 + 2 * g])
            acc = acc_ref[g]
            o_t = (acc[:KV_LATENT] / acc[KV_LATENT:KV_LATENT + 1]).T
            rows.append(jnp.concatenate([o_t[h * ATT_GROUP:(h + 1) * ATT_GROUP] for h in range(ATT_HEADS)], axis=1))
        return _dot(jnp.concatenate(rows, axis=0).astype(BF16), wuv_ref[...]).astype(BF16)

    y_ref[0] = lax.cond(ranked_ties, lambda: attention(True), lambda: attention(False))


def _dsa_call(qlat, qidx, wrow, ckv_c, ckvt_c, kidx_c, m_ckv, m_ckvt, m_kidx, wuv_bd, topk):
    bn, rows, _ = qlat.shape
    nchunks, kc = ckv_c.shape[1], ckv_c.shape[2]
    nq = rows // Q_TILE

    def q_spec(width):
        return pl.BlockSpec((1, Q_TILE, width), lambda b, i: (b, i, 0))

    def k_spec(a):
        return pl.BlockSpec((1,) + a.shape[1:], lambda b, i: (b, 0, 0, 0))

    return pl.pallas_call(
        functools.partial(_dsa_kernel, topk=topk),
        grid=(bn, nq),
        in_specs=[q_spec(LAT_WIDTH), q_spec(IDX_WIDTH),
                  pl.BlockSpec((1, 1, GATE_ROWS, Q_TILE), lambda b, i: (b, i, 0, 0)),
                  k_spec(ckv_c), k_spec(ckvt_c), k_spec(kidx_c),
                  _resident(m_ckv.shape), _resident(m_ckvt.shape), _resident(m_kidx.shape), _resident(wuv_bd.shape)],
        out_specs=q_spec(ATT_WIDTH),
        out_shape=jax.ShapeDtypeStruct((bn, rows, ATT_WIDTH), BF16),
        scratch_shapes=[pltpu.VMEM((nchunks, kc, Q_TILE), F32),
                        pltpu.VMEM((Q_TILE // ATT_GROUP, ckvt_c.shape[2], ATT_HEADS * ATT_GROUP), F32),
                        pltpu.VMEM((Q_TILE // ATT_GROUP, kc, ATT_HEADS * ATT_GROUP), BF16)],
        compiler_params=pltpu.CompilerParams(
            dimension_semantics=("arbitrary", "arbitrary"), vmem_limit_bytes=VMEM_LIMIT_BYTES),
        name="dsa",
    )(qlat, qidx, wrow, ckv_c, ckvt_c, kidx_c, m_ckv, m_ckvt, m_kidx, wuv_bd)


def _split3(x):
    hi = x.astype(BF16)
    r = x - hi.astype(F32)
    mid = r.astype(BF16)
    lo = (r - mid.astype(F32)).astype(BF16)
    return hi, mid, lo


ML_EXT = ML_V_DIM + 16


def _mlstm_chunk(qk, qt, vt, g, gt, state):
    L = qk.shape[0]
    s_idx = lax.broadcasted_iota(jnp.int32, (L, L), 0)
    t_idx = lax.broadcasted_iota(jnp.int32, (L, L), 1)
    causal = s_idx <= t_idx
    b_cols = sum(_dot((t_idx <= s_idx).astype(BF16), part) for part in _split3(g))
    b_rows = sum(_dot(part, causal.astype(BF16)) for part in _split3(gt))
    ones_blk = jnp.where(lax.broadcasted_iota(jnp.int32, (ML_EXT - ML_V_DIM, L), 0) == 0, 1.0, 0.0).astype(BF16)
    kq = ML_HEADS * ML_QK_DIM

    outs, new_state = [], []
    for h in range(ML_HEADS):
        ce, m_prev = state[h]
        c_col = g[:, GATE_I0 + h:GATE_I0 + h + 1] - b_cols[:, GATE_F0 + h:GATE_F0 + h + 1]
        b_row = b_rows[GATE_F0 - GATE_W0 + h:GATE_F0 - GATE_W0 + h + 1, :]
        ig_row = gt[GATE_I0 - GATE_W0 + h:GATE_I0 - GATE_W0 + h + 1, :]
        qt_h = qt[h * ML_QK_DIM:(h + 1) * ML_QK_DIM, :]
        kh = qk[:, kq + h * ML_QK_DIM:kq + (h + 1) * ML_QK_DIM]
        vt_ext = jnp.concatenate([vt[h * ML_V_DIM:(h + 1) * ML_V_DIM, :], ones_blk], axis=0)

        d_t = jnp.where(causal, c_col + b_row, -jnp.inf)
        inter = b_row + m_prev
        m_t = jnp.maximum(jnp.max(d_t, axis=0, keepdims=True), inter)
        w_inter = jnp.exp(inter - m_t)
        s_t = _dot(kh, qt_h) * jnp.exp(d_t - m_t)
        r = _dot(vt_ext, s_t.astype(BF16)) + _dot(ce.astype(BF16), qt_h) * w_inter
        num = r[:ML_V_DIM]
        den = r[ML_V_DIM:ML_V_DIM + 1]
        hh = num / jnp.maximum(jnp.abs(den), jnp.exp(-m_t))
        mu = jnp.mean(hh, axis=0, keepdims=True)
        hc = hh - mu
        var = jnp.mean(hc * hc, axis=0, keepdims=True)
        outs.append((hc * lax.rsqrt(var + LN_EPS)).T)

        b_end = b_row[:, L - 1:L]
        g_row = b_end - b_row + ig_row
        m_new = jnp.maximum(b_end + m_prev, jnp.max(g_row, axis=1, keepdims=True))
        decay = jnp.exp(b_end + m_prev - m_new)
        weighted = (vt_ext.astype(F32) * jnp.exp(g_row - m_new)).astype(BF16)
        new_state.append((decay * ce + _dot(weighted, kh), m_new))
    return outs, new_state


def _mlstm_kernel(qk_ref, qt_ref, vt_ref, og_ref, gates_ref, gt_ref, mqk_ref, mqt_ref, mvt_ref, mgates_ref, mgt_ref,
                  ng_ref, y_ref,
                  ce0_ref, m0_ref):
    L = mqk_ref.shape[0]
    n_chunks = qk_ref.shape[1] // L
    norm_g = ng_ref[...]

    @pl.when(pl.program_id(0) == 0)
    def _():
        state = [(jnp.zeros((ML_EXT, ML_QK_DIM), F32), jnp.full((1, 1), M_INIT, F32)) for _ in range(ML_HEADS)]
        _, state = _mlstm_chunk(mqk_ref[...], mqt_ref[...], mvt_ref[...], mgates_ref[...], mgt_ref[...], state)
        for h in range(ML_HEADS):
            ce0_ref[h] = state[h][0]
            m0_ref[h] = jnp.broadcast_to(state[h][1], m0_ref.shape[1:])

    n_b = qk_ref.shape[0]
    state = [(ce0_ref[h], m0_ref[h][0:1, 0:1]) for h in range(ML_HEADS)] * n_b

    def body(c, flat):
        rows = pl.ds(pl.multiple_of(c * L, L), L)
        new_flat = []
        for b in range(n_b):
            state = [(flat[2 * (b * ML_HEADS + h)], flat[2 * (b * ML_HEADS + h) + 1]) for h in range(ML_HEADS)]
            outs, state = _mlstm_chunk(qk_ref[b, rows, :], qt_ref[b, c], vt_ref[b, c], gates_ref[b, rows, :],
                                       gt_ref[b, c], state)
            y = jnp.concatenate(outs, axis=1) * norm_g * og_ref[b, rows, :]
            y_ref[b, rows, :] = y.astype(BF16)
            new_flat += [x for pair in state for x in pair]
        return tuple(new_flat)

    lax.fori_loop(0, n_chunks, body, tuple(x for pair in state for x in pair))


def _mlstm_call(qk, qt_c, vt_c, og, gates, gt_c, mqk, mqt, mvt, mgates, mgt, norm_g):
    bn, rows, _ = qk.shape
    nb = ML_BATCH if bn % ML_BATCH == 0 else 1

    def b_spec(a):
        return pl.BlockSpec((nb,) + a.shape[1:], lambda b: (b,) + (0,) * (a.ndim - 1))

    consts = (mqk, mqt, mvt, mgates, mgt, norm_g)
    return pl.pallas_call(
        _mlstm_kernel,
        grid=(bn // nb,),
        in_specs=[b_spec(qk), b_spec(qt_c), b_spec(vt_c), b_spec(og), b_spec(gates), b_spec(gt_c)]
        + [_resident(c.shape) for c in consts],
        out_specs=pl.BlockSpec((nb, rows, ML_WIDTH), lambda b: (b, 0, 0)),
        out_shape=jax.ShapeDtypeStruct((bn, rows, ML_WIDTH), BF16),
        scratch_shapes=[pltpu.VMEM((ML_HEADS, ML_EXT, ML_QK_DIM), F32), pltpu.VMEM((ML_HEADS, SUBLANES, LANES), F32)],
        compiler_params=pltpu.CompilerParams(
            dimension_semantics=("arbitrary",), vmem_limit_bytes=VMEM_LIMIT_BYTES),
        name="mlstm",
    )(qk, qt_c, vt_c, og, gates, gt_c, *consts)


def _out_ffn_ln_kernel(ya_ref, ym_ref, h_ref, wo_ref, g2_ref, b2_ref, wg_ref, wu_ref, wd_ref, g3_ref, b3_ref,
                       o_ref, wg_s, wu_s, wd_s, h2_ref, acc0_ref, *, alpha, n_stage):
    i = pl.program_id(0)

    def mixed():
        mix = _dot(ya_ref[...], wo_ref[:ATT_WIDTH, :]) + _dot(ym_ref[...], wo_ref[ATT_WIDTH:, :])
        return _layer_norm(alpha * h_ref[...] + mix, g2_ref[...], b2_ref[...])

    @pl.when(i == 0)
    def _():
        h2_ref[...] = mixed()
        acc0_ref[...] = jnp.zeros(acc0_ref.shape, F32)

    @pl.when(i < n_stage)
    def _():
        _stage_ffn_weights(i, wg_ref, wu_ref, wd_ref, wg_s, wu_s, wd_s)
        acc0_ref[...] += _swiglu_chunk(h2_ref[...].astype(BF16), wg_s[i], wu_s[i], wd_s[i])

    @pl.when(i == n_stage - 1)
    def _():
        o_ref[...] = _layer_norm(alpha * h2_ref[...] + 0.5 * acc0_ref[...], g3_ref[...], b3_ref[...])

    @pl.when(i >= n_stage)
    def _():
        o_ref[...] = _ffn_ln(mixed(), wg_s, wu_s, wd_s, g3_ref[...], b3_ref[...], alpha)


def _out_ffn_ln_call(ya, ym, h, wo, g2, b2, wg, wu, wd, g3, b3, alpha, tm):
    rows, d = h.shape
    n_stage, w_specs, w_scratch = _ffn_weight_specs(d, wg.shape[1])

    def row_spec(width):
        return pl.BlockSpec((tm, width), lambda i: (_staged_tile_index(i, n_stage), 0))

    return pl.pallas_call(
        functools.partial(_out_ffn_ln_kernel, alpha=alpha, n_stage=n_stage),
        grid=(n_stage - 1 + rows // tm,),
        in_specs=[row_spec(ya.shape[1]), row_spec(ym.shape[1]), row_spec(d),
                  _resident(wo.shape), _resident(g2.shape), _resident(b2.shape)] + w_specs
        + [_resident(g3.shape), _resident(b3.shape)],
        out_specs=row_spec(d),
        out_shape=jax.ShapeDtypeStruct((rows, d), F32),
        scratch_shapes=w_scratch + [pltpu.VMEM((tm, d), F32), pltpu.VMEM((tm, d), F32)],
        compiler_params=pltpu.CompilerParams(
            dimension_semantics=("arbitrary",), vmem_limit_bytes=VMEM_LIMIT_BYTES),
        name="out_ffn_ln",
    )(ya, ym, h, wo, g2, b2, wg, wu, wd, g3, b3)


def _block_diag(w):
    nh, a, b = w.shape
    eye = jnp.eye(nh, dtype=w.dtype)
    return (eye[:, None, :, None] * w[:, :, None, :]).reshape(nh * a, nh * b)


def _pad_rows(a, rows, value=0.0):
    return jnp.pad(a, ((0, rows - a.shape[0]), (0, 0)), constant_values=value)


def kernel(x, meta_tokens, ln1_g, ln1_b, ffn1_w_gate, ffn1_w_up, ffn1_w_down, w_in, w_uk, w_uv, kv_norm_g,
           conv_w, b_igate, b_fgate, ml_norm_g, w_out, ln2_g, ln2_b, ffn2_w_gate, ffn2_w_up, ffn2_w_down,
           ln3_g, ln3_b):
    depth = ln1_g.shape[0]
    assert depth == 1, "the meta-token shortcut below is only valid for a single layer"
    bsz, seq, d = x.shape
    assert seq % ROW_TILE == 0 and seq % ML_CHUNK == 0 and seq % Q_TILE == 0
    assert Q_TILE == GT_CHUNK and ML_CHUNK == GT_CHUNK and KEY_CHUNK == GT_CHUNK and ROW_TILE % GT_CHUNK == 0
    alpha = (2 * depth) ** 0.25
    topk = min(TOPK_MAX, seq // 4)

    row2 = lambda p: p[0].reshape(1, -1).astype(F32)
    bf = lambda w: w[0].astype(BF16)

    w_t = jnp.swapaxes(w_in[0], 0, 1)
    sizes = (ATT_WIDTH, KV_LATENT, IDX_WIDTH, IDX_DIM, IDX_HEADS, MLQK_WIDTH, ML_WIDTH, ML_WIDTH, ML_HEADS, ML_HEADS)
    offs = [sum(sizes[:n]) for n in range(len(sizes) + 1)]
    assert w_t.shape[0] == offs[-1]
    wa = jnp.concatenate([w_t[offs[0]:offs[4]], w_t[offs[4]:offs[5]], w_t[offs[8]:offs[10]],
                          jnp.zeros((LANES - GATE_END, d), F32)], axis=0).astype(BF16)
    wm = w_t[offs[5]:offs[8]].astype(BF16)
    gbias = jnp.concatenate([jnp.zeros((GATE_I0,), F32), b_igate[0], b_fgate[0],
                             jnp.zeros((LANES - GATE_END,), F32)]).reshape(1, LANES)
    wuk_bd = jnp.stack([_block_diag(w_uk[0][2 * p:2 * p + 2]) for p in range(HEAD_PAIRS)]).astype(BF16)
    wuv_bd = _block_diag(w_uv[0]).astype(BF16)
    kvg = row2(kv_norm_g)
    convw = conv_w[0].astype(F32)

    h1, h1_meta = _ffn_ln_call(x.reshape(bsz * seq, d), meta_tokens.astype(F32), ffn1_w_gate[0], ffn1_w_up[0],
                               ffn1_w_down[0], row2(ln1_g), row2(ln1_b), alpha, FFN_TILE)
    zero_tail = jnp.zeros((CONV_HIST, MLQK_WIDTH), F32)
    (_, m_ckv, _, m_kidx, m_qk, m_v, _, m_gates, m_tail) = _inproj_call(
        h1_meta[None], zero_tail, wa, wm, wuk_bd, kvg, convw, gbias, N_META)

    (qlat, ckv, qidx, kidx, qk, _, og, gates, _, gates_t, ckv_t, v_t, q_t) = _inproj_call(
        h1.reshape(bsz, seq, d), m_tail[0], wa, wm, wuk_bd, kvg, convw, gbias, ROW_TILE)

    nchunks = seq // KEY_CHUNK
    ckv_c = ckv.reshape(bsz, nchunks, KEY_CHUNK, KV_LATENT)
    kidx_c = kidx.reshape(bsz, nchunks, KEY_CHUNK, IDX_DIM)
    ones_rows = jnp.zeros((ATT_EXT - KV_LATENT, N_META), BF16).at[0].set(1.0)
    m_ckvt = jnp.concatenate([m_ckv[0].T, ones_rows], axis=0)
    y_att = _dsa_call(qlat, qidx, gates_t, ckv_c, ckv_t, kidx_c,
                      m_ckv[0], m_ckvt, m_kidx[0], wuv_bd, topk)

    lane = jnp.arange(LANES)
    pad_gate = jnp.where((lane >= GATE_I0) & (lane < GATE_F0), NEG_BIG, 0.0).astype(F32)
    mg = jnp.concatenate([m_gates[0], jnp.broadcast_to(pad_gate, (ML_CHUNK - N_META, LANES))], axis=0)
    gate_lanes = slice(GATE_W0, GATE_W0 + GATE_ROWS)
    m_qk_pad = _pad_rows(m_qk[0], ML_CHUNK)
    y_ml = _mlstm_call(qk, q_t, v_t, og, gates, gates_t,
                       m_qk_pad, m_qk_pad[:, :MLQK_WIDTH // 2].T, _pad_rows(m_v[0], ML_CHUNK).T, mg, mg[:, gate_lanes].T,
                       row2(ml_norm_g))

    out = _out_ffn_ln_call(
        y_att.reshape(bsz * seq, ATT_WIDTH), y_ml.reshape(bsz * seq, ML_WIDTH), h1, bf(w_out),
        row2(ln2_g), row2(ln2_b), ffn2_w_gate[0], ffn2_w_up[0], ffn2_w_down[0], row2(ln3_g), row2(ln3_b),
        alpha, FFN_TILE)
    return out.reshape(bsz, seq, d)
```

```python
import functools

import jax
import jax.numpy as jnp
from jax import lax
from jax.experimental import pallas as pl
from jax.experimental.pallas import tpu as pltpu

F32 = jnp.float32
BF16 = jnp.bfloat16

N_META = 16
ATT_HEADS = 8
ATT_HEAD_DIM = 64
KV_LATENT = 128
IDX_HEADS = 4
IDX_DIM = 64
TOPK_MAX = 256
ML_HEADS = 4
ML_V_DIM = 128
ML_QK_DIM = 64
CONV_WIDTH = 4
GATE_SOFTCAP = 15.0
M_INIT = -1e30
LN_EPS = 1e-5
NEG_BIG = -1e30
LOG2_E = 1.4426950408889634

LANES = 128
SUBLANES = 8
VMEM_BYTES_V7X = 64 * 1024 * 1024
VMEM_LIMIT_BYTES = VMEM_BYTES_V7X * 7 // 8

FF_CHUNK = 256
ROW_TILE = 512
FFN_TILE = 512
Q_TILE = 256
ATT_GROUP = 128
KEY_CHUNK = 256
N_BISECT = 16
REDUCE_ROWS = 32
ML_BATCH = 2
ML_CHUNK = 256


def _dot(a, b):
    return jnp.dot(a, b, preferred_element_type=F32)


def _layer_norm(z, g, b):
    mu = jnp.mean(z, axis=-1, keepdims=True)
    zc = z - mu
    var = jnp.mean(zc * zc, axis=-1, keepdims=True)
    return zc * lax.rsqrt(var + LN_EPS) * g + b


def _sigmoid(x):
    return 1.0 / (1.0 + jnp.exp(-x))


def _swiglu_chunk(xb, wg_c, wu_c, wd_c):
    g = _dot(xb, wg_c)
    u = _dot(xb, wu_c)
    return _dot((g * _sigmoid(g) * u).astype(BF16), wd_c)


def _ffn_ln(x, wg_s, wu_s, wd_s, g, b, alpha):
    xb = x.astype(BF16)
    acc = jnp.zeros(x.shape, F32)
    for c in range(wg_s.shape[0]):
        acc = acc + _swiglu_chunk(xb, wg_s[c], wu_s[c], wd_s[c])
    return _layer_norm(alpha * x + 0.5 * acc, g, b)


def _stage_ffn_weights(step, wg_ref, wu_ref, wd_ref, wg_s, wu_s, wd_s):
    wg_s[step] = wg_ref[...].astype(BF16)
    wu_s[step] = wu_ref[...].astype(BF16)
    wd_s[step] = wd_ref[...].astype(BF16)


def _ffn_weight_specs(d, d_ff):
    n = d_ff // FF_CHUNK
    col = pl.BlockSpec((d, FF_CHUNK), lambda i: (0, jnp.minimum(i, n - 1)))
    row = pl.BlockSpec((FF_CHUNK, d), lambda i: (jnp.minimum(i, n - 1), 0))
    scratch = [pltpu.VMEM((n, d, FF_CHUNK), BF16), pltpu.VMEM((n, d, FF_CHUNK), BF16),
               pltpu.VMEM((n, FF_CHUNK, d), BF16)]
    return n, [col, col, row], scratch


def _staged_tile_index(i, n_stage):
    return jnp.maximum(i - (n_stage - 1), 0)


def _ffn_ln_kernel(x_ref, meta_ref, wg_ref, wu_ref, wd_ref, g_ref, b_ref, o_ref, ometa_ref,
                   wg_s, wu_s, wd_s, macc_ref, acc0_ref, *, alpha, n_stage):
    i = pl.program_id(0)

    @pl.when(i == 0)
    def _():
        macc_ref[...] = jnp.zeros(macc_ref.shape, F32)
        acc0_ref[...] = jnp.zeros(acc0_ref.shape, F32)

    @pl.when(i < n_stage)
    def _():
        _stage_ffn_weights(i, wg_ref, wu_ref, wd_ref, wg_s, wu_s, wd_s)
        macc_ref[...] += _swiglu_chunk(meta_ref[...].astype(BF16), wg_s[i], wu_s[i], wd_s[i])
        acc0_ref[...] += _swiglu_chunk(x_ref[...].astype(BF16), wg_s[i], wu_s[i], wd_s[i])

    @pl.when(i == n_stage - 1)
    def _():
        ometa_ref[...] = _layer_norm(alpha * meta_ref[...] + 0.5 * macc_ref[...], g_ref[...], b_ref[...])
        o_ref[...] = _layer_norm(alpha * x_ref[...] + 0.5 * acc0_ref[...], g_ref[...], b_ref[...])

    @pl.when(i >= n_stage)
    def _():
        o_ref[...] = _ffn_ln(x_ref[...], wg_s, wu_s, wd_s, g_ref[...], b_ref[...], alpha)


def _resident(shape):
    return pl.BlockSpec(shape, lambda *_: (0,) * len(shape), pipeline_mode=pl.Buffered(1))


def _ffn_ln_call(x, meta, wg, wu, wd, g, b, alpha, tm):
    rows, d = x.shape
    n_stage, w_specs, w_scratch = _ffn_weight_specs(d, wg.shape[1])
    row_spec = pl.BlockSpec((tm, d), lambda i: (_staged_tile_index(i, n_stage), 0))
    return pl.pallas_call(
        functools.partial(_ffn_ln_kernel, alpha=alpha, n_stage=n_stage),
        grid=(n_stage - 1 + rows // tm,),
        in_specs=[row_spec, _resident(meta.shape)] + w_specs + [_resident(g.shape), _resident(b.shape)],
        out_specs=[row_spec, pl.BlockSpec(meta.shape, lambda i: (0, 0))],
        out_shape=[jax.ShapeDtypeStruct((rows, d), F32), jax.ShapeDtypeStruct(meta.shape, F32)],
        scratch_shapes=w_scratch + [pltpu.VMEM(meta.shape, F32), pltpu.VMEM((tm, d), F32)],
        compiler_params=pltpu.CompilerParams(
            dimension_semantics=("arbitrary",), vmem_limit_bytes=VMEM_LIMIT_BYTES),
        name="ffn_ln",
    )(x, meta, wg, wu, wd, g, b)


ATT_WIDTH = ATT_HEADS * ATT_HEAD_DIM
IDX_WIDTH = IDX_HEADS * IDX_DIM
MLQK_WIDTH = 2 * ML_HEADS * ML_QK_DIM
ML_WIDTH = ML_HEADS * ML_V_DIM
LAT_WIDTH = ATT_HEADS * KV_LATENT
CONV_HIST = SUBLANES
GATE_W0 = IDX_DIM
GATE_I0, GATE_F0, GATE_END = GATE_W0 + IDX_HEADS, GATE_W0 + IDX_HEADS + ML_HEADS, GATE_W0 + IDX_HEADS + 2 * ML_HEADS
GATE_ROWS = 2 * SUBLANES
GT_CHUNK = 256
ATT_EXT = KV_LATENT + 2 * SUBLANES
HEAD_PAIRS = ATT_HEADS // 2


def _inproj_kernel(h_ref, tail_ref, wa_ref, wm_ref, wuk_ref, kvg_ref, convw_ref, gbias_ref,
                   qlat_ref, ckv_ref, qidx_ref, kidx_ref, qk_ref, v_ref, og_ref, gates_ref, tailout_ref, *rest):
    carry_ref, wa_s, wm_s = rest[-3:]
    tm = h_ref.shape[1]

    @pl.when((pl.program_id(0) == 0) & (pl.program_id(1) == 0))
    def _():
        wa_s[...] = wa_ref[...].astype(F32).T.astype(BF16)
        wm_s[...] = wm_ref[...].astype(F32).T.astype(BF16)

    @pl.when(pl.program_id(1) == 0)
    def _():
        carry_ref[...] = tail_ref[...]

    xb = h_ref[0].astype(BF16)

    pa = _dot(xb, wa_s[...])
    q_a = pa[:, :ATT_WIDTH].astype(BF16)
    c0 = ATT_WIDTH
    ckv = pa[:, c0:c0 + KV_LATENT]
    c1 = c0 + KV_LATENT
    ckv = ckv * lax.rsqrt(jnp.mean(ckv * ckv, axis=-1, keepdims=True) + LN_EPS) * kvg_ref[...]
    ckv_ref[0] = ckv.astype(BF16)
    qidx_ref[0] = pa[:, c1:c1 + IDX_WIDTH].astype(BF16)
    c2 = c1 + IDX_WIDTH
    kidx_ref[0] = pa[:, c2:c2 + IDX_DIM].astype(BF16)
    pair_in, pair_out = 2 * ATT_HEAD_DIM, 2 * KV_LATENT
    for p in range(HEAD_PAIRS):
        ql = _dot(q_a[:, p * pair_in:(p + 1) * pair_in], wuk_ref[p])
        qlat_ref[0, :, p * pair_out:(p + 1) * pair_out] = (ql * (ATT_HEAD_DIM ** -0.5 * LOG2_E)).astype(BF16)

    pm = _dot(xb, wm_s[...])
    qk_raw = pm[:, :MLQK_WIDTH]
    v_ref[0] = pm[:, MLQK_WIDTH:MLQK_WIDTH + ML_WIDTH].astype(BF16)
    og_ref[0] = _sigmoid(pm[:, MLQK_WIDTH + ML_WIDTH:])

    ext = jnp.concatenate([carry_ref[...], qk_raw], axis=0)
    cw = convw_ref[...]
    conv = jnp.zeros_like(qk_raw)
    for j in range(CONV_WIDTH):
        s0 = CONV_HIST - (CONV_WIDTH - 1) + j
        conv = conv + ext[s0:s0 + tm] * cw[j:j + 1]
    act = conv * _sigmoid(conv)
    half = MLQK_WIDTH // 2
    qk_ref[0, :, :half] = act[:, :half].astype(BF16)
    qk_ref[0, :, half:] = (act[:, half:] * (ML_QK_DIM ** -0.5)).astype(BF16)
    carry_ref[...] = qk_raw[tm - CONV_HIST:]
    tailout_ref[0] = qk_raw[tm - CONV_HIST:]

    gr = pa[:, c2 + IDX_DIM - GATE_W0:]
    lane = lax.broadcasted_iota(jnp.int32, gr.shape, 1)
    sc = GATE_SOFTCAP * jnp.tanh((gr + gbias_ref[...]) / GATE_SOFTCAP)
    lf = -(jnp.maximum(-sc, 0.0) + jnp.log1p(jnp.exp(-jnp.abs(sc))))
    w_scaled = gr * (IDX_HEADS ** -0.5 * IDX_DIM ** -0.5)
    gates = jnp.where((lane < GATE_W0) | (lane >= GATE_END), 0.0,
                      jnp.where(lane < GATE_I0, w_scaled, jnp.where(lane < GATE_F0, sc, lf)))
    gates_ref[0] = gates
    if len(rest) == 7:
        gt_ref, ckvt_ref, vt_ref, qt_ref = rest[:4]
        gates_t = gates.T[GATE_W0:GATE_W0 + GATE_ROWS]
        ckv_t = ckv.T.astype(BF16)
        v_t = pm[:, MLQK_WIDTH:MLQK_WIDTH + ML_WIDTH].T.astype(BF16)
        q_t = act[:, :half].T.astype(BF16)
        ones_blk = jnp.where(lax.broadcasted_iota(jnp.int32, (ATT_EXT - KV_LATENT, GT_CHUNK), 0) == 0,
                             1.0, 0.0).astype(BF16)
        for j in range(tm // GT_CHUNK):
            piece = slice(j * GT_CHUNK, (j + 1) * GT_CHUNK)
            gt_ref[0, j] = gates_t[:, piece]
            ckvt_ref[0, j] = jnp.concatenate([ckv_t[:, piece], ones_blk], axis=0)
            vt_ref[0, j] = v_t[:, piece]
            qt_ref[0, j] = q_t[:, piece]


def _inproj_call(h, tail, wa, wm, wuk_bd, kvg, convw, gbias, tm):
    bn, rows, d = h.shape
    nblk = rows // tm
    emit_gt = tm % GT_CHUNK == 0

    def row_spec(width):
        return pl.BlockSpec((1, tm, width), lambda b, j: (b, j, 0))

    outs = [
        (LAT_WIDTH, BF16), (KV_LATENT, BF16), (IDX_WIDTH, BF16), (IDX_DIM, BF16),
        (MLQK_WIDTH, BF16), (ML_WIDTH, BF16), (ML_WIDTH, F32), (LANES, F32),
    ]
    out_shape = [jax.ShapeDtypeStruct((bn, rows, w), dt) for w, dt in outs]
    out_specs = [row_spec(w) for w, _ in outs]
    out_shape.append(jax.ShapeDtypeStruct((bn, CONV_HIST, MLQK_WIDTH), F32))
    out_specs.append(pl.BlockSpec((1, CONV_HIST, MLQK_WIDTH), lambda b, j: (b, 0, 0)))
    if emit_gt:
        per_tile = tm // GT_CHUNK
        for height, dt in ((GATE_ROWS, F32), (ATT_EXT, BF16), (ML_WIDTH, BF16), (MLQK_WIDTH // 2, BF16)):
            out_shape.append(jax.ShapeDtypeStruct((bn, rows // GT_CHUNK, height, GT_CHUNK), dt))
            out_specs.append(pl.BlockSpec((1, per_tile, height, GT_CHUNK), lambda b, j: (b, j, 0, 0)))
    return pl.pallas_call(
        _inproj_kernel,
        grid=(bn, nblk),
        in_specs=[
            row_spec(d),
            _resident(tail.shape), _resident(wa.shape), _resident(wm.shape),
            _resident(wuk_bd.shape), _resident(kvg.shape), _resident(convw.shape), _resident(gbias.shape),
        ],
        out_specs=out_specs,
        out_shape=out_shape,
        scratch_shapes=[pltpu.VMEM((CONV_HIST, MLQK_WIDTH), F32),
                        pltpu.VMEM(wa.shape[::-1], BF16), pltpu.VMEM(wm.shape[::-1], BF16)],
        compiler_params=pltpu.CompilerParams(
            dimension_semantics=("arbitrary", "arbitrary"), vmem_limit_bytes=VMEM_LIMIT_BYTES),
        name="in_proj",
    )(h, tail, wa, wm, wuk_bd, kvg, convw, gbias)


def _dsa_kernel(qlat_ref, qidx_ref, wrow_ref, ckv_ref, ckvt_ref, kidx_ref, mckv_ref, mckvt_ref, mkidx_ref,
                wuv_ref, y_ref, s_ref, acc_ref, p_ref, *, topk):
    _, kc, tq = s_ref.shape
    i = pl.program_id(1)
    nch = ((i + 1) * tq + kc - 1) // kc
    qreal = i * tq + lax.broadcasted_iota(jnp.int32, (1, tq), 1)
    kf = float(topk)

    wrow = wrow_ref[0, 0]
    qidx = qidx_ref[0]
    q_idx_t = qidx.astype(F32).T
    q_idx_all_t = jnp.concatenate([q_idx_t[h * IDX_DIM:(h + 1) * IDX_DIM] for h in range(IDX_HEADS)],
                                  axis=1).astype(BF16)
    wi = [wrow[h:h + 1, :] for h in range(IDX_HEADS)]

    def scores(k_rows):
        lg = _dot(k_rows, q_idx_all_t)
        sc = jnp.zeros((k_rows.shape[0], tq), F32)
        for h in range(IDX_HEADS):
            sc = sc + jnp.maximum(lg[:, h * tq:(h + 1) * tq], 0.0) * wi[h]
        return sc

    s_meta = scores(mkidx_ref[...])

    def score_chunk(c, lo, hi):
        sc = scores(kidx_ref[0, c])
        valid = c * kc + lax.broadcasted_iota(jnp.int32, (kc, tq), 0) <= qreal
        s_ref[c] = jnp.where(valid, sc, -jnp.inf)
        groups = (kc // REDUCE_ROWS, REDUCE_ROWS, tq)
        lo = jnp.minimum(lo, jnp.min(sc.reshape(groups), axis=0))
        hi = jnp.maximum(hi, jnp.max(sc.reshape(groups), axis=0))
        return lo, hi

    def score_pair(c2, carry):
        lo, hi = score_chunk(2 * c2, *carry)
        return score_chunk(jnp.minimum(2 * c2 + 1, nch - 1), lo, hi)

    lo, hi = lax.fori_loop(0, (nch + 1) // 2, score_pair,
                           (jnp.full((REDUCE_ROWS, tq), jnp.inf, F32), jnp.full((REDUCE_ROWS, tq), -jnp.inf, F32)))
    lo = jnp.minimum(jnp.min(lo, axis=0, keepdims=True), jnp.min(s_meta, axis=0, keepdims=True))
    hi = jnp.maximum(jnp.max(hi, axis=0, keepdims=True), jnp.max(s_meta, axis=0, keepdims=True))

    def key_reduce(reduce, combine, per_chunk, init):
        def body(c, acc):
            x = per_chunk(s_ref[c]).reshape(kc // REDUCE_ROWS, REDUCE_ROWS, tq)
            return combine(acc, reduce(x, axis=0))
        acc = lax.fori_loop(0, nch, body, jnp.full((REDUCE_ROWS, tq), init, F32))
        return combine(reduce(acc, axis=0, keepdims=True), reduce(per_chunk(s_meta), axis=0, keepdims=True))

    def count(pred):
        return key_reduce(jnp.sum, jnp.add, lambda sc: jnp.where(pred(sc), 1.0, 0.0), 0.0)

    def max_where(pred):
        return key_reduce(jnp.max, jnp.maximum, lambda sc: jnp.where(pred(sc), sc, -jnp.inf), -jnp.inf)

    def bisect(_, carry):
        lo, hi = carry
        mid = 0.5 * lo + 0.5 * hi
        up = count(lambda sc: sc > mid) >= kf
        return jnp.where(up, mid, lo), jnp.where(up, hi, mid)

    lo, hi = lax.fori_loop(0, N_BISECT, bisect, (lo, hi))

    n_valid = (qreal + (N_META + 1)).astype(F32)
    small = n_valid <= kf
    cand = max_where(lambda sc: sc <= hi)
    n_ge = count(lambda sc: sc >= cand)
    done = jnp.where(small | (n_ge >= kf), 1.0, 0.0)

    def not_finished(state):
        return jnp.min(state[1]) < 0.5

    def step_down(state):
        cand, done, _ = state
        nxt = jnp.where(done > 0.5, cand, max_where(lambda sc: sc < cand))
        n_ge = count(lambda sc: sc >= nxt)
        return nxt, jnp.where(n_ge >= kf, 1.0, done), n_ge

    cand, _, n_ge = lax.while_loop(not_finished, step_down, (cand, done, n_ge))
    thr = jnp.where(small, -jnp.inf, cand)
    n_eq = count(lambda sc: sc == thr)
    need = jnp.where(small, 0.0, kf - (n_ge - n_eq))
    ranked_ties = jnp.max(jnp.where(n_eq > need, 1.0, 0.0)) > 0.5

    qlat = qlat_ref[0]
    n_groups = tq // ATT_GROUP
    onehot = (lax.broadcasted_iota(jnp.int32, (ATT_GROUP, ATT_GROUP), 0)
              == lax.broadcasted_iota(jnp.int32, (ATT_GROUP, ATT_GROUP), 1)).astype(BF16)
    qlat32 = qlat.astype(F32)
    q_aug_t = [jnp.concatenate(
        [jnp.concatenate([qlat32[g * ATT_GROUP:(g + 1) * ATT_GROUP, h * KV_LATENT:(h + 1) * KV_LATENT].T
                          for h in range(ATT_HEADS)], axis=1).astype(BF16),
         jnp.concatenate([onehot] * ATT_HEADS, axis=1)], axis=0)
        for g in range(n_groups)]
    hq = ATT_HEADS * ATT_GROUP

    def lower_tri(n):
        return (lax.broadcasted_iota(jnp.int32, (n, n), 1) <= lax.broadcasted_iota(jnp.int32, (n, n), 0)).astype(BF16)

    def attention(ranked):
        def mask_bias(sc, eq_seen):
            if not ranked:
                return jnp.where(sc >= thr, 0.0, NEG_BIG).astype(BF16), eq_seen
            n = sc.shape[0]
            eq = sc == thr
            rank = _dot(lower_tri(n), jnp.where(eq, 1.0, 0.0).astype(BF16)) + eq_seen
            keep = (sc > thr) | (eq & (rank <= need))
            return jnp.where(keep, 0.0, NEG_BIG).astype(BF16), rank[n - 1:n, :]

        def logits(g, kv, bias):
            k_aug = jnp.concatenate([kv, bias[:, g * ATT_GROUP:(g + 1) * ATT_GROUP]], axis=1)
            return _dot(k_aug, q_aug_t[g])

        def fold_in(g, c_prev, a_prev):
            acc_ref[g] = a_prev * acc_ref[g] + _dot(ckvt_ref[0, c_prev], p_ref[g])

        def attend(c, carry):
            bias, eq_seen = mask_bias(s_ref[c], carry[0])
            out = [eq_seen]
            for g in range(n_groups):
                m, a_prev = carry[1 + 2 * g:3 + 2 * g]
                fold_in(g, c - 1, a_prev)
                s = logits(g, ckv_ref[0, c], bias)
                m_new = jnp.maximum(m, jnp.max(s, axis=0, keepdims=True))
                p_ref[g] = jnp.exp2((s - m_new).astype(BF16))
                out += [m_new, jnp.exp2(m - m_new)]
            return tuple(out)

        bias_m, eq_seen = mask_bias(s_meta, jnp.zeros((1, tq), F32))
        bias_0, eq_seen = mask_bias(s_ref[0], eq_seen)
        init = [eq_seen]
        for g in range(n_groups):
            s_m = logits(g, mckv_ref[...], bias_m)
            s_0 = logits(g, ckv_ref[0, 0], bias_0)
            m = jnp.maximum(jnp.max(s_m, axis=0, keepdims=True), jnp.max(s_0, axis=0, keepdims=True))
            acc_ref[g] = _dot(mckvt_ref[...], jnp.exp2((s_m - m).astype(BF16)))
            p_ref[g] = jnp.exp2((s_0 - m).astype(BF16))
            init += [m, jnp.ones((1, hq), F32)]
        carry = lax.fori_loop(1, nch, attend, tuple(init))
        rows = []
        for g in range(n_groups):
            fold_in(g, nch - 1, carry[2 + 2 * g])
            acc = acc_ref[g]
            o_t = (acc[:KV_LATENT] / acc[KV_LATENT:KV_LATENT + 1]).T
            rows.append(jnp.concatenate([o_t[h * ATT_GROUP:(h + 1) * ATT_GROUP] for h in range(ATT_HEADS)], axis=1))
        return _dot(jnp.concatenate(rows, axis=0).astype(BF16), wuv_ref[...]).astype(BF16)

    y_ref[0] = lax.cond(ranked_ties, lambda: attention(True), lambda: attention(False))


def _dsa_call(qlat, qidx, wrow, ckv_c, ckvt_c, kidx_c, m_ckv, m_ckvt, m_kidx, wuv_bd, topk):
    bn, rows, _ = qlat.shape
    nchunks, kc = ckv_c.shape[1], ckv_c.shape[2]
    nq = rows // Q_TILE

    def q_spec(width):
        return pl.BlockSpec((1, Q_TILE, width), lambda b, i: (b, i, 0))

    def k_spec(a):
        return pl.BlockSpec((1,) + a.shape[1:], lambda b, i: (b, 0, 0, 0))

    return pl.pallas_call(
        functools.partial(_dsa_kernel, topk=topk),
        grid=(bn, nq),
        in_specs=[q_spec(LAT_WIDTH), q_spec(IDX_WIDTH),
                  pl.BlockSpec((1, 1, GATE_ROWS, Q_TILE), lambda b, i: (b, i, 0, 0)),
                  k_spec(ckv_c), k_spec(ckvt_c), k_spec(kidx_c),
                  _resident(m_ckv.shape), _resident(m_ckvt.shape), _resident(m_kidx.shape), _resident(wuv_bd.shape)],
        out_specs=q_spec(ATT_WIDTH),
        out_shape=jax.ShapeDtypeStruct((bn, rows, ATT_WIDTH), BF16),
        scratch_shapes=[pltpu.VMEM((nchunks, kc, Q_TILE), F32),
                        pltpu.VMEM((Q_TILE // ATT_GROUP, ckvt_c.shape[2], ATT_HEADS * ATT_GROUP), F32),
                        pltpu.VMEM((Q_TILE // ATT_GROUP, kc, ATT_HEADS * ATT_GROUP), BF16)],
        compiler_params=pltpu.CompilerParams(
            dimension_semantics=("arbitrary", "arbitrary"), vmem_limit_bytes=VMEM_LIMIT_BYTES),
        name="dsa",
    )(qlat, qidx, wrow, ckv_c, ckvt_c, kidx_c, m_ckv, m_ckvt, m_kidx, wuv_bd)


def _split3(x):
    hi = x.astype(BF16)
    r = x - hi.astype(F32)
    mid = r.astype(BF16)
    lo = (r - mid.astype(F32)).astype(BF16)
    return hi, mid, lo


ML_EXT = ML_V_DIM + 16


def _mlstm_chunk(qk, qt, vt, g, gt, state):
    L = qk.shape[0]
    s_idx = lax.broadcasted_iota(jnp.int32, (L, L), 0)
    t_idx = lax.broadcasted_iota(jnp.int32, (L, L), 1)
    causal = s_idx <= t_idx
    b_cols = sum(_dot((t_idx <= s_idx).astype(BF16), part) for part in _split3(g))
    b_rows = sum(_dot(part, causal.astype(BF16)) for part in _split3(gt))
    ones_blk = jnp.where(lax.broadcasted_iota(jnp.int32, (ML_EXT - ML_V_DIM, L), 0) == 0, 1.0, 0.0).astype(BF16)
    kq = ML_HEADS * ML_QK_DIM

    outs, new_state = [], []
    for h in range(ML_HEADS):
        ce, m_prev = state[h]
        c_col = g[:, GATE_I0 + h:GATE_I0 + h + 1] - b_cols[:, GATE_F0 + h:GATE_F0 + h + 1]
        b_row = b_rows[GATE_F0 - GATE_W0 + h:GATE_F0 - GATE_W0 + h + 1, :]
        ig_row = gt[GATE_I0 - GATE_W0 + h:GATE_I0 - GATE_W0 + h + 1, :]
        qt_h = qt[h * ML_QK_DIM:(h + 1) * ML_QK_DIM, :]
        kh = qk[:, kq + h * ML_QK_DIM:kq + (h + 1) * ML_QK_DIM]
        vt_ext = jnp.concatenate([vt[h * ML_V_DIM:(h + 1) * ML_V_DIM, :], ones_blk], axis=0)

        d_t = jnp.where(causal, c_col + b_row, -jnp.inf)
        inter = b_row + m_prev
        m_t = jnp.maximum(jnp.max(d_t, axis=0, keepdims=True), inter)
        w_inter = jnp.exp(inter - m_t)
        s_t = _dot(kh, qt_h) * jnp.exp(d_t - m_t)
        r = _dot(vt_ext, s_t.astype(BF16)) + _dot(ce.astype(BF16), qt_h) * w_inter
        num = r[:ML_V_DIM]
        den = r[ML_V_DIM:ML_V_DIM + 1]
        hh = num / jnp.maximum(jnp.abs(den), jnp.exp(-m_t))
        mu = jnp.mean(hh, axis=0, keepdims=True)
        hc = hh - mu
        var = jnp.mean(hc * hc, axis=0, keepdims=True)
        outs.append((hc * lax.rsqrt(var + LN_EPS)).T)

        b_end = b_row[:, L - 1:L]
        g_row = b_end - b_row + ig_row
        m_new = jnp.maximum(b_end + m_prev, jnp.max(g_row, axis=1, keepdims=True))
        decay = jnp.exp(b_end + m_prev - m_new)
        weighted = (vt_ext.astype(F32) * jnp.exp(g_row - m_new)).astype(BF16)
        new_state.append((decay * ce + _dot(weighted, kh), m_new))
    return outs, new_state


def _mlstm_kernel(qk_ref, qt_ref, vt_ref, og_ref, gates_ref, gt_ref, mqk_ref, mqt_ref, mvt_ref, mgates_ref, mgt_ref,
                  ng_ref, y_ref,
                  ce0_ref, m0_ref):
    L = mqk_ref.shape[0]
    n_chunks = qk_ref.shape[1] // L
    norm_g = ng_ref[...]

    @pl.when(pl.program_id(0) == 0)
    def _():
        state = [(jnp.zeros((ML_EXT, ML_QK_DIM), F32), jnp.full((1, 1), M_INIT, F32)) for _ in range(ML_HEADS)]
        _, state = _mlstm_chunk(mqk_ref[...], mqt_ref[...], mvt_ref[...], mgates_ref[...], mgt_ref[...], state)
        for h in range(ML_HEADS):
            ce0_ref[h] = state[h][0]
            m0_ref[h] = jnp.broadcast_to(state[h][1], m0_ref.shape[1:])

    n_b = qk_ref.shape[0]
    state = [(ce0_ref[h], m0_ref[h][0:1, 0:1]) for h in range(ML_HEADS)] * n_b

    def body(c, flat):
        rows = pl.ds(pl.multiple_of(c * L, L), L)
        new_flat = []
        for b in range(n_b):
            state = [(flat[2 * (b * ML_HEADS + h)], flat[2 * (b * ML_HEADS + h) + 1]) for h in range(ML_HEADS)]
            outs, state = _mlstm_chunk(qk_ref[b, rows, :], qt_ref[b, c], vt_ref[b, c], gates_ref[b, rows, :],
                                       gt_ref[b, c], state)
            y = jnp.concatenate(outs, axis=1) * norm_g * og_ref[b, rows, :]
            y_ref[b, rows, :] = y.astype(BF16)
            new_flat += [x for pair in state for x in pair]
        return tuple(new_flat)

    lax.fori_loop(0, n_chunks, body, tuple(x for pair in state for x in pair))


def _mlstm_call(qk, qt_c, vt_c, og, gates, gt_c, mqk, mqt, mvt, mgates, mgt, norm_g):
    bn, rows, _ = qk.shape
    nb = ML_BATCH if bn % ML_BATCH == 0 else 1

    def b_spec(a):
        return pl.BlockSpec((nb,) + a.shape[1:], lambda b: (b,) + (0,) * (a.ndim - 1))

    consts = (mqk, mqt, mvt, mgates, mgt, norm_g)
    return pl.pallas_call(
        _mlstm_kernel,
        grid=(bn // nb,),
        in_specs=[b_spec(qk), b_spec(qt_c), b_spec(vt_c), b_spec(og), b_spec(gates), b_spec(gt_c)]
        + [_resident(c.shape) for c in consts],
        out_specs=pl.BlockSpec((nb, rows, ML_WIDTH), lambda b: (b, 0, 0)),
        out_shape=jax.ShapeDtypeStruct((bn, rows, ML_WIDTH), BF16),
        scratch_shapes=[pltpu.VMEM((ML_HEADS, ML_EXT, ML_QK_DIM), F32), pltpu.VMEM((ML_HEADS, SUBLANES, LANES), F32)],
        compiler_params=pltpu.CompilerParams(
            dimension_semantics=("arbitrary",), vmem_limit_bytes=VMEM_LIMIT_BYTES),
        name="mlstm",
    )(qk, qt_c, vt_c, og, gates, gt_c, *consts)


def _out_ffn_ln_kernel(ya_ref, ym_ref, h_ref, wo_ref, g2_ref, b2_ref, wg_ref, wu_ref, wd_ref, g3_ref, b3_ref,
                       o_ref, wg_s, wu_s, wd_s, h2_ref, acc0_ref, *, alpha, n_stage):
    i = pl.program_id(0)

    def mixed():
        mix = _dot(ya_ref[...], wo_ref[:ATT_WIDTH, :]) + _dot(ym_ref[...], wo_ref[ATT_WIDTH:, :])
        return _layer_norm(alpha * h_ref[...] + mix, g2_ref[...], b2_ref[...])

    @pl.when(i == 0)
    def _():
        h2_ref[...] = mixed()
        acc0_ref[...] = jnp.zeros(acc0_ref.shape, F32)

    @pl.when(i < n_stage)
    def _():
        _stage_ffn_weights(i, wg_ref, wu_ref, wd_ref, wg_s, wu_s, wd_s)
        acc0_ref[...] += _swiglu_chunk(h2_ref[...].astype(BF16), wg_s[i], wu_s[i], wd_s[i])

    @pl.when(i == n_stage - 1)
    def _():
        o_ref[...] = _layer_norm(alpha * h2_ref[...] + 0.5 * acc0_ref[...], g3_ref[...], b3_ref[...])

    @pl.when(i >= n_stage)
    def _():
        o_ref[...] = _ffn_ln(mixed(), wg_s, wu_s, wd_s, g3_ref[...], b3_ref[...], alpha)


def _out_ffn_ln_call(ya, ym, h, wo, g2, b2, wg, wu, wd, g3, b3, alpha, tm):
    rows, d = h.shape
    n_stage, w_specs, w_scratch = _ffn_weight_specs(d, wg.shape[1])

    def row_spec(width):
        return pl.BlockSpec((tm, width), lambda i: (_staged_tile_index(i, n_stage), 0))

    return pl.pallas_call(
        functools.partial(_out_ffn_ln_kernel, alpha=alpha, n_stage=n_stage),
        grid=(n_stage - 1 + rows // tm,),
        in_specs=[row_spec(ya.shape[1]), row_spec(ym.shape[1]), row_spec(d),
                  _resident(wo.shape), _resident(g2.shape), _resident(b2.shape)] + w_specs
        + [_resident(g3.shape), _resident(b3.shape)],
        out_specs=row_spec(d),
        out_shape=jax.ShapeDtypeStruct((rows, d), F32),
        scratch_shapes=w_scratch + [pltpu.VMEM((tm, d), F32), pltpu.VMEM((tm, d), F32)],
        compiler_params=pltpu.CompilerParams(
            dimension_semantics=("arbitrary",), vmem_limit_bytes=VMEM_LIMIT_BYTES),
        name="out_ffn_ln",
    )(ya, ym, h, wo, g2, b2, wg, wu, wd, g3, b3)


def _block_diag(w):
    nh, a, b = w.shape
    eye = jnp.eye(nh, dtype=w.dtype)
    return (eye[:, None, :, None] * w[:, :, None, :]).reshape(nh * a, nh * b)


def _pad_rows(a, rows, value=0.0):
    return jnp.pad(a, ((0, rows - a.shape[0]), (0, 0)), constant_values=value)


def kernel(x, meta_tokens, ln1_g, ln1_b, ffn1_w_gate, ffn1_w_up, ffn1_w_down, w_in, w_uk, w_uv, kv_norm_g,
           conv_w, b_igate, b_fgate, ml_norm_g, w_out, ln2_g, ln2_b, ffn2_w_gate, ffn2_w_up, ffn2_w_down,
           ln3_g, ln3_b):
    depth = ln1_g.shape[0]
    assert depth == 1, "the meta-token shortcut below is only valid for a single layer"
    bsz, seq, d = x.shape
    assert seq % ROW_TILE == 0 and seq % ML_CHUNK == 0 and seq % Q_TILE == 0
    assert Q_TILE == GT_CHUNK and ML_CHUNK == GT_CHUNK and KEY_CHUNK == GT_CHUNK and ROW_TILE % GT_CHUNK == 0
    alpha = (2 * depth) ** 0.25
    topk = min(TOPK_MAX, seq // 4)

    row2 = lambda p: p[0].reshape(1, -1).astype(F32)
    bf = lambda w: w[0].astype(BF16)

    w_t = jnp.swapaxes(w_in[0], 0, 1)
    sizes = (ATT_WIDTH, KV_LATENT, IDX_WIDTH, IDX_DIM, IDX_HEADS, MLQK_WIDTH, ML_WIDTH, ML_WIDTH, ML_HEADS, ML_HEADS)
    offs = [sum(sizes[:n]) for n in range(len(sizes) + 1)]
    assert w_t.shape[0] == offs[-1]
    wa = jnp.concatenate([w_t[offs[0]:offs[4]], w_t[offs[4]:offs[5]], w_t[offs[8]:offs[10]],
                          jnp.zeros((LANES - GATE_END, d), F32)], axis=0).astype(BF16)
    wm = w_t[offs[5]:offs[8]].astype(BF16)
    gbias = jnp.concatenate([jnp.zeros((GATE_I0,), F32), b_igate[0], b_fgate[0],
                             jnp.zeros((LANES - GATE_END,), F32)]).reshape(1, LANES)
    wuk_bd = jnp.stack([_block_diag(w_uk[0][2 * p:2 * p + 2]) for p in range(HEAD_PAIRS)]).astype(BF16)
    wuv_bd = _block_diag(w_uv[0]).astype(BF16)
    kvg = row2(kv_norm_g)
    convw = conv_w[0].astype(F32)

    h1, h1_meta = _ffn_ln_call(x.reshape(bsz * seq, d), meta_tokens.astype(F32), ffn1_w_gate[0], ffn1_w_up[0],
                               ffn1_w_down[0], row2(ln1_g), row2(ln1_b), alpha, FFN_TILE)
    zero_tail = jnp.zeros((CONV_HIST, MLQK_WIDTH), F32)
    (_, m_ckv, _, m_kidx, m_qk, m_v, _, m_gates, m_tail) = _inproj_call(
        h1_meta[None], zero_tail, wa, wm, wuk_bd, kvg, convw, gbias, N_META)

    (qlat, ckv, qidx, kidx, qk, _, og, gates, _, gates_t, ckv_t, v_t, q_t) = _inproj_call(
        h1.reshape(bsz, seq, d), m_tail[0], wa, wm, wuk_bd, kvg, convw, gbias, ROW_TILE)

    nchunks = seq // KEY_CHUNK
    ckv_c = ckv.reshape(bsz, nchunks, KEY_CHUNK, KV_LATENT)
    kidx_c = kidx.reshape(bsz, nchunks, KEY_CHUNK, IDX_DIM)
    ones_rows = jnp.zeros((ATT_EXT - KV_LATENT, N_META), BF16).at[0].set(1.0)
    m_ckvt = jnp.concatenate([m_ckv[0].T, ones_rows], axis=0)
    y_att = _dsa_call(qlat, qidx, gates_t, ckv_c, ckv_t, kidx_c,
                      m_ckv[0], m_ckvt, m_kidx[0], wuv_bd, topk)

    lane = jnp.arange(LANES)
    pad_gate = jnp.where((lane >= GATE_I0) & (lane < GATE_F0), NEG_BIG, 0.0).astype(F32)
    mg = jnp.concatenate([m_gates[0], jnp.broadcast_to(pad_gate, (ML_CHUNK - N_META, LANES))], axis=0)
    gate_lanes = slice(GATE_W0, GATE_W0 + GATE_ROWS)
    m_qk_pad = _pad_rows(m_qk[0], ML_CHUNK)
    y_ml = _mlstm_call(qk, q_t, v_t, og, gates, gates_t,
                       m_qk_pad, m_qk_pad[:, :MLQK_WIDTH // 2].T, _pad_rows(m_v[0], ML_CHUNK).T, mg, mg[:, gate_lanes].T,
                       row2(ml_norm_g))

    out = _out_ffn_ln_call(
        y_att.reshape(bsz * seq, ATT_WIDTH), y_ml.reshape(bsz * seq, ML_WIDTH), h1, bf(w_out),
        row2(ln2_g), row2(ln2_b), ffn2_w_gate[0], ffn2_w_up[0], ffn2_w_down[0], row2(ln3_g), row2(ln3_b),
        alpha, FFN_TILE)
    return out.reshape(bsz, seq, d)
```

```python
import functools

import jax
import jax.numpy as jnp
from jax import lax
from jax.experimental import pallas as pl
from jax.experimental.pallas import tpu as pltpu

F32 = jnp.float32
BF16 = jnp.bfloat16

N_META = 16
ATT_HEADS = 8
ATT_HEAD_DIM = 64
KV_LATENT = 128
IDX_HEADS = 4
IDX_DIM = 64
TOPK_MAX = 256
ML_HEADS = 4
ML_V_DIM = 128
ML_QK_DIM = 64
CONV_WIDTH = 4
GATE_SOFTCAP = 15.0
M_INIT = -1e30
LN_EPS = 1e-5
NEG_BIG = -1e30
LOG2_E = 1.4426950408889634

LANES = 128
SUBLANES = 8
VMEM_BYTES_V7X = 64 * 1024 * 1024
VMEM_LIMIT_BYTES = VMEM_BYTES_V7X * 7 // 8

FF_CHUNK = 256
ROW_TILE = 512
FFN_TILE = 512
Q_TILE = 256
ATT_GROUP = 128
KEY_CHUNK = 256
N_BISECT = 16
REDUCE_ROWS = 32
ML_BATCH = 2
ML_CHUNK = 256


def _dot(a, b):
    return jnp.dot(a, b, preferred_element_type=F32)


def _layer_norm(z, g, b):
    mu = jnp.mean(z, axis=-1, keepdims=True)
    zc = z - mu
    var = jnp.mean(zc * zc, axis=-1, keepdims=True)
    return zc * lax.rsqrt(var + LN_EPS) * g + b


def _sigmoid(x):
    return 1.0 / (1.0 + jnp.exp(-x))


def _swiglu_chunk(xb, wg_c, wu_c, wd_c):
    g = _dot(xb, wg_c)
    u = _dot(xb, wu_c)
    return _dot((g * _sigmoid(g) * u).astype(BF16), wd_c)


def _ffn_ln(x, wg_s, wu_s, wd_s, g, b, alpha):
    xb = x.astype(BF16)
    acc = jnp.zeros(x.shape, F32)
    for c in range(wg_s.shape[0]):
        acc = acc + _swiglu_chunk(xb, wg_s[c], wu_s[c], wd_s[c])
    return _layer_norm(alpha * x + 0.5 * acc, g, b)


def _stage_ffn_weights(step, wg_ref, wu_ref, wd_ref, wg_s, wu_s, wd_s):
    wg_s[step] = wg_ref[...].astype(BF16)
    wu_s[step] = wu_ref[...].astype(BF16)
    wd_s[step] = wd_ref[...].astype(BF16)


def _ffn_weight_specs(d, d_ff):
    n = d_ff // FF_CHUNK
    col = pl.BlockSpec((d, FF_CHUNK), lambda i: (0, jnp.minimum(i, n - 1)))
    row = pl.BlockSpec((FF_CHUNK, d), lambda i: (jnp.minimum(i, n - 1), 0))
    scratch = [pltpu.VMEM((n, d, FF_CHUNK), BF16), pltpu.VMEM((n, d, FF_CHUNK), BF16),
               pltpu.VMEM((n, FF_CHUNK, d), BF16)]
    return n, [col, col, row], scratch


def _staged_tile_index(i, n_stage):
    return jnp.maximum(i - (n_stage - 1), 0)


def _ffn_ln_kernel(x_ref, meta_ref, wg_ref, wu_ref, wd_ref, g_ref, b_ref, o_ref, ometa_ref,
                   wg_s, wu_s, wd_s, macc_ref, acc0_ref, *, alpha, n_stage):
    i = pl.program_id(0)

    @pl.when(i == 0)
    def _():
        macc_ref[...] = jnp.zeros(macc_ref.shape, F32)
        acc0_ref[...] = jnp.zeros(acc0_ref.shape, F32)

    @pl.when(i < n_stage)
    def _():
        _stage_ffn_weights(i, wg_ref, wu_ref, wd_ref, wg_s, wu_s, wd_s)
        macc_ref[...] += _swiglu_chunk(meta_ref[...].astype(BF16), wg_s[i], wu_s[i], wd_s[i])
        acc0_ref[...] += _swiglu_chunk(x_ref[...].astype(BF16), wg_s[i], wu_s[i], wd_s[i])

    @pl.when(i == n_stage - 1)
    def _():
        ometa_ref[...] = _layer_norm(alpha * meta_ref[...] + 0.5 * macc_ref[...], g_ref[...], b_ref[...])
        o_ref[...] = _layer_norm(alpha * x_ref[...] + 0.5 * acc0_ref[...], g_ref[...], b_ref[...])

    @pl.when(i >= n_stage)
    def _():
        o_ref[...] = _ffn_ln(x_ref[...], wg_s, wu_s, wd_s, g_ref[...], b_ref[...], alpha)


def _resident(shape):
    return pl.BlockSpec(shape, lambda *_: (0,) * len(shape), pipeline_mode=pl.Buffered(1))


def _ffn_ln_call(x, meta, wg, wu, wd, g, b, alpha, tm):
    rows, d = x.shape
    n_stage, w_specs, w_scratch = _ffn_weight_specs(d, wg.shape[1])
    row_spec = pl.BlockSpec((tm, d), lambda i: (_staged_tile_index(i, n_stage), 0))
    return pl.pallas_call(
        functools.partial(_ffn_ln_kernel, alpha=alpha, n_stage=n_stage),
        grid=(n_stage - 1 + rows // tm,),
        in_specs=[row_spec, _resident(meta.shape)] + w_specs + [_resident(g.shape), _resident(b.shape)],
        out_specs=[row_spec, pl.BlockSpec(meta.shape, lambda i: (0, 0))],
        out_shape=[jax.ShapeDtypeStruct((rows, d), F32), jax.ShapeDtypeStruct(meta.shape, F32)],
        scratch_shapes=w_scratch + [pltpu.VMEM(meta.shape, F32), pltpu.VMEM((tm, d), F32)],
        compiler_params=pltpu.CompilerParams(
            dimension_semantics=("arbitrary",), vmem_limit_bytes=VMEM_LIMIT_BYTES),
        name="ffn_ln",
    )(x, meta, wg, wu, wd, g, b)


ATT_WIDTH = ATT_HEADS * ATT_HEAD_DIM
IDX_WIDTH = IDX_HEADS * IDX_DIM
MLQK_WIDTH = 2 * ML_HEADS * ML_QK_DIM
ML_WIDTH = ML_HEADS * ML_V_DIM
LAT_WIDTH = ATT_HEADS * KV_LATENT
CONV_HIST = SUBLANES
GATE_W0 = IDX_DIM
GATE_I0, GATE_F0, GATE_END = GATE_W0 + IDX_HEADS, GATE_W0 + IDX_HEADS + ML_HEADS, GATE_W0 + IDX_HEADS + 2 * ML_HEADS
GATE_ROWS = 2 * SUBLANES
GT_CHUNK = 256
ATT_EXT = KV_LATENT + 2 * SUBLANES
HEAD_PAIRS = ATT_HEADS // 2


def _inproj_kernel(h_ref, tail_ref, wa_ref, wm_ref, wuk_ref, kvg_ref, convw_ref, gbias_ref,
                   qlat_ref, ckv_ref, qidx_ref, kidx_ref, qk_ref, v_ref, og_ref, gates_ref, tailout_ref, *rest):
    carry_ref, wa_s, wm_s = rest[-3:]
    tm = h_ref.shape[1]

    @pl.when((pl.program_id(0) == 0) & (pl.program_id(1) == 0))
    def _():
        wa_s[...] = wa_ref[...].astype(F32).T.astype(BF16)
        wm_s[...] = wm_ref[...].astype(F32).T.astype(BF16)

    @pl.when(pl.program_id(1) == 0)
    def _():
        carry_ref[...] = tail_ref[...]

    xb = h_ref[0].astype(BF16)

    pa = _dot(xb, wa_s[...])
    q_a = pa[:, :ATT_WIDTH].astype(BF16)
    c0 = ATT_WIDTH
    ckv = pa[:, c0:c0 + KV_LATENT]
    c1 = c0 + KV_LATENT
    ckv = ckv * lax.rsqrt(jnp.mean(ckv * ckv, axis=-1, keepdims=True) + LN_EPS) * kvg_ref[...]
    ckv_ref[0] = ckv.astype(BF16)
    qidx_ref[0] = pa[:, c1:c1 + IDX_WIDTH].astype(BF16)
    c2 = c1 + IDX_WIDTH
    kidx_ref[0] = pa[:, c2:c2 + IDX_DIM].astype(BF16)
    pair_in, pair_out = 2 * ATT_HEAD_DIM, 2 * KV_LATENT
    for p in range(HEAD_PAIRS):
        ql = _dot(q_a[:, p * pair_in:(p + 1) * pair_in], wuk_ref[p])
        qlat_ref[0, :, p * pair_out:(p + 1) * pair_out] = (ql * (ATT_HEAD_DIM ** -0.5 * LOG2_E)).astype(BF16)

    pm = _dot(xb, wm_s[...])
    qk_raw = pm[:, :MLQK_WIDTH]
    v_ref[0] = pm[:, MLQK_WIDTH:MLQK_WIDTH + ML_WIDTH].astype(BF16)
    og_ref[0] = _sigmoid(pm[:, MLQK_WIDTH + ML_WIDTH:])

    ext = jnp.concatenate([carry_ref[...], qk_raw], axis=0)
    cw = convw_ref[...]
    conv = jnp.zeros_like(qk_raw)
    for j in range(CONV_WIDTH):
        s0 = CONV_HIST - (CONV_WIDTH - 1) + j
        conv = conv + ext[s0:s0 + tm] * cw[j:j + 1]
    act = conv * _sigmoid(conv)
    half = MLQK_WIDTH // 2
    qk_ref[0, :, :half] = act[:, :half].astype(BF16)
    qk_ref[0, :, half:] = (act[:, half:] * (ML_QK_DIM ** -0.5)).astype(BF16)
    carry_ref[...] = qk_raw[tm - CONV_HIST:]
    tailout_ref[0] = qk_raw[tm - CONV_HIST:]

    gr = pa[:, c2 + IDX_DIM - GATE_W0:]
    lane = lax.broadcasted_iota(jnp.int32, gr.shape, 1)
    sc = GATE_SOFTCAP * jnp.tanh((gr + gbias_ref[...]) / GATE_SOFTCAP)
    lf = -(jnp.maximum(-sc, 0.0) + jnp.log1p(jnp.exp(-jnp.abs(sc))))
    w_scaled = gr * (IDX_HEADS ** -0.5 * IDX_DIM ** -0.5)
    gates = jnp.where((lane < GATE_W0) | (lane >= GATE_END), 0.0,
                      jnp.where(lane < GATE_I0, w_scaled, jnp.where(lane < GATE_F0, sc, lf)))
    gates_ref[0] = gates
    if len(rest) == 7:
        gt_ref, ckvt_ref, vt_ref, qt_ref = rest[:4]
        gates_t = gates.T[GATE_W0:GATE_W0 + GATE_ROWS]
        ckv_t = ckv.T.astype(BF16)
        v_t = pm[:, MLQK_WIDTH:MLQK_WIDTH + ML_WIDTH].T.astype(BF16)
        q_t = act[:, :half].T.astype(BF16)
        ones_blk = jnp.where(lax.broadcasted_iota(jnp.int32, (ATT_EXT - KV_LATENT, GT_CHUNK), 0) == 0,
                             1.0, 0.0).astype(BF16)
        for j in range(tm // GT_CHUNK):
            piece = slice(j * GT_CHUNK, (j + 1) * GT_CHUNK)
            gt_ref[0, j] = gates_t[:, piece]
            ckvt_ref[0, j] = jnp.concatenate([ckv_t[:, piece], ones_blk], axis=0)
            vt_ref[0, j] = v_t[:, piece]
            qt_ref[0, j] = q_t[:, piece]


def _inproj_call(h, tail, wa, wm, wuk_bd, kvg, convw, gbias, tm):
    bn, rows, d = h.shape
    nblk = rows // tm
    emit_gt = tm % GT_CHUNK == 0

    def row_spec(width):
        return pl.BlockSpec((1, tm, width), lambda b, j: (b, j, 0))

    outs = [
        (LAT_WIDTH, BF16), (KV_LATENT, BF16), (IDX_WIDTH, BF16), (IDX_DIM, BF16),
        (MLQK_WIDTH, BF16), (ML_WIDTH, BF16), (ML_WIDTH, F32), (LANES, F32),
    ]
    out_shape = [jax.ShapeDtypeStruct((bn, rows, w), dt) for w, dt in outs]
    out_specs = [row_spec(w) for w, _ in outs]
    out_shape.append(jax.ShapeDtypeStruct((bn, CONV_HIST, MLQK_WIDTH), F32))
    out_specs.append(pl.BlockSpec((1, CONV_HIST, MLQK_WIDTH), lambda b, j: (b, 0, 0)))
    if emit_gt:
        per_tile = tm // GT_CHUNK
        for height, dt in ((GATE_ROWS, F32), (ATT_EXT, BF16), (ML_WIDTH, BF16), (MLQK_WIDTH // 2, BF16)):
            out_shape.append(jax.ShapeDtypeStruct((bn, rows // GT_CHUNK, height, GT_CHUNK), dt))
            out_specs.append(pl.BlockSpec((1, per_tile, height, GT_CHUNK), lambda b, j: (b, j, 0, 0)))
    return pl.pallas_call(
        _inproj_kernel,
        grid=(bn, nblk),
        in_specs=[
            row_spec(d),
            _resident(tail.shape), _resident(wa.shape), _resident(wm.shape),
            _resident(wuk_bd.shape), _resident(kvg.shape), _resident(convw.shape), _resident(gbias.shape),
        ],
        out_specs=out_specs,
        out_shape=out_shape,
        scratch_shapes=[pltpu.VMEM((CONV_HIST, MLQK_WIDTH), F32),
                        pltpu.VMEM(wa.shape[::-1], BF16), pltpu.VMEM(wm.shape[::-1], BF16)],
        compiler_params=pltpu.CompilerParams(
            dimension_semantics=("arbitrary", "arbitrary"), vmem_limit_bytes=VMEM_LIMIT_BYTES),
        name="in_proj",
    )(h, tail, wa, wm, wuk_bd, kvg, convw, gbias)


def _dsa_kernel(qlat_ref, qidx_ref, wrow_ref, ckv_ref, ckvt_ref, kidx_ref, mckv_ref, mckvt_ref, mkidx_ref,
                wuv_ref, y_ref, s_ref, acc_ref, p_ref, *, topk):
    _, kc, tq = s_ref.shape
    i = pl.program_id(1)
    nch = ((i + 1) * tq + kc - 1) // kc
    qreal = i * tq + lax.broadcasted_iota(jnp.int32, (1, tq), 1)
    kf = float(topk)

    wrow = wrow_ref[0, 0]
    qidx = qidx_ref[0]
    q_idx_t = qidx.astype(F32).T
    q_idx_all_t = jnp.concatenate([q_idx_t[h * IDX_DIM:(h + 1) * IDX_DIM] for h in range(IDX_HEADS)],
                                  axis=1).astype(BF16)
    wi = [wrow[h:h + 1, :] for h in range(IDX_HEADS)]

    def scores(k_rows):
        lg = _dot(k_rows, q_idx_all_t)
        sc = jnp.zeros((k_rows.shape[0], tq), F32)
        for h in range(IDX_HEADS):
            sc = sc + jnp.maximum(lg[:, h * tq:(h + 1) * tq], 0.0) * wi[h]
        return sc

    s_meta = scores(mkidx_ref[...])

    def score_chunk(c, lo, hi):
        sc = scores(kidx_ref[0, c])
        valid = c * kc + lax.broadcasted_iota(jnp.int32, (kc, tq), 0) <= qreal
        s_ref[c] = jnp.where(valid, sc, -jnp.inf)
        groups = (kc // REDUCE_ROWS, REDUCE_ROWS, tq)
        lo = jnp.minimum(lo, jnp.min(sc.reshape(groups), axis=0))
        hi = jnp.maximum(hi, jnp.max(sc.reshape(groups), axis=0))
        return lo, hi

    def score_pair(c2, carry):
        lo, hi = score_chunk(2 * c2, *carry)
        return score_chunk(jnp.minimum(2 * c2 + 1, nch - 1), lo, hi)

    lo, hi = lax.fori_loop(0, (nch + 1) // 2, score_pair,
                           (jnp.full((REDUCE_ROWS, tq), jnp.inf, F32), jnp.full((REDUCE_ROWS, tq), -jnp.inf, F32)))
    lo = jnp.minimum(jnp.min(lo, axis=0, keepdims=True), jnp.min(s_meta, axis=0, keepdims=True))
    hi = jnp.maximum(jnp.max(hi, axis=0, keepdims=True), jnp.max(s_meta, axis=0, keepdims=True))

    def key_reduce(reduce, combine, per_chunk, init):
        def body(c, acc):
            x = per_chunk(s_ref[c]).reshape(kc // REDUCE_ROWS, REDUCE_ROWS, tq)
            return combine(acc, reduce(x, axis=0))
        acc = lax.fori_loop(0, nch, body, jnp.full((REDUCE_ROWS, tq), init, F32))
        return combine(reduce(acc, axis=0, keepdims=True), reduce(per_chunk(s_meta), axis=0, keepdims=True))

    def count(pred):
        return key_reduce(jnp.sum, jnp.add, lambda sc: jnp.where(pred(sc), 1.0, 0.0), 0.0)

    def max_where(pred):
        return key_reduce(jnp.max, jnp.maximum, lambda sc: jnp.where(pred(sc), sc, -jnp.inf), -jnp.inf)

    def bisect(_, carry):
        lo, hi = carry
        mid = 0.5 * lo + 0.5 * hi
        up = count(lambda sc: sc > mid) >= kf
        return jnp.where(up, mid, lo), jnp.where(up, hi, mid)

    lo, hi = lax.fori_loop(0, N_BISECT, bisect, (lo, hi))

    n_valid = (qreal + (N_META + 1)).astype(F32)
    small = n_valid <= kf
    cand = max_where(lambda sc: sc <= hi)
    n_ge = count(lambda sc: sc >= cand)
    done = jnp.where(small | (n_ge >= kf), 1.0, 0.0)

    def not_finished(state):
        return jnp.min(state[1]) < 0.5

    def step_down(state):
        cand, done, _ = state
        nxt = jnp.where(done > 0.5, cand, max_where(lambda sc: sc < cand))
        n_ge = count(lambda sc: sc >= nxt)
        return nxt, jnp.where(n_ge >= kf, 1.0, done), n_ge

    cand, _, n_ge = lax.while_loop(not_finished, step_down, (cand, done, n_ge))
    thr = jnp.where(small, -jnp.inf, cand)
    n_eq = count(lambda sc: sc == thr)
    need = jnp.where(small, 0.0, kf - (n_ge - n_eq))
    ranked_ties = jnp.max(jnp.where(n_eq > need, 1.0, 0.0)) > 0.5

    qlat = qlat_ref[0]
    n_groups = tq // ATT_GROUP
    onehot = (lax.broadcasted_iota(jnp.int32, (ATT_GROUP, ATT_GROUP), 0)
              == lax.broadcasted_iota(jnp.int32, (ATT_GROUP, ATT_GROUP), 1)).astype(BF16)
    qlat32 = qlat.astype(F32)
    q_aug_t = [jnp.concatenate(
        [jnp.concatenate([qlat32[g * ATT_GROUP:(g + 1) * ATT_GROUP, h * KV_LATENT:(h + 1) * KV_LATENT].T
                          for h in range(ATT_HEADS)], axis=1).astype(BF16),
         jnp.concatenate([onehot] * ATT_HEADS, axis=1)], axis=0)
        for g in range(n_groups)]
    hq = ATT_HEADS * ATT_GROUP

    def lower_tri(n):
        return (lax.broadcasted_iota(jnp.int32, (n, n), 1) <= lax.broadcasted_iota(jnp.int32, (n, n), 0)).astype(BF16)

    def attention(ranked):
        def mask_bias(sc, eq_seen):
            if not ranked:
                return jnp.where(sc >= thr, 0.0, NEG_BIG).astype(BF16), eq_seen
            n = sc.shape[0]
            eq = sc == thr
            rank = _dot(lower_tri(n), jnp.where(eq, 1.0, 0.0).astype(BF16)) + eq_seen
            keep = (sc > thr) | (eq & (rank <= need))
            return jnp.where(keep, 0.0, NEG_BIG).astype(BF16), rank[n - 1:n, :]

        def logits(g, kv, bias):
            k_aug = jnp.concatenate([kv, bias[:, g * ATT_GROUP:(g + 1) * ATT_GROUP]], axis=1)
            return _dot(k_aug, q_aug_t[g])

        def fold_in(g, c_prev, a_prev):
            acc_ref[g] = a_prev * acc_ref[g] + _dot(ckvt_ref[0, c_prev], p_ref[g])

        def attend(c, carry):
            bias, eq_seen = mask_bias(s_ref[c], carry[0])
            out = [eq_seen]
            for g in range(n_groups):
                m, a_prev = carry[1 + 2 * g:3 + 2 * g]
                fold_in(g, c - 1, a_prev)
                s = logits(g, ckv_ref[0, c], bias)
                m_new = jnp.maximum(m, jnp.max(s, axis=0, keepdims=True))
                p_ref[g] = jnp.exp2(s - m_new).astype(BF16)
                out += [m_new, jnp.exp2(m - m_new)]
            return tuple(out)

        bias_m, eq_seen = mask_bias(s_meta, jnp.zeros((1, tq), F32))
        bias_0, eq_seen = mask_bias(s_ref[0], eq_seen)
        kv_first = jnp.concatenate([mckv_ref[...], ckv_ref[0, 0]], axis=0)
        bias_first = jnp.concatenate([bias_m, bias_0], axis=0)
        init = [eq_seen]
        for g in range(n_groups):
            s = logits(g, kv_first, bias_first)
            m = jnp.max(s, axis=0, keepdims=True)
            p = jnp.exp2(s - m).astype(BF16)
            acc_ref[g] = _dot(mckvt_ref[...], p[:N_META])
            p_ref[g] = p[N_META:]
            init += [m, jnp.ones((1, hq), F32)]
        carry = lax.fori_loop(1, nch, attend, tuple(init))
        rows = []
        for g in range(n_groups):
            fold_in(g, nch - 1, carry[2 + 2 * g])
            acc = acc_ref[g]
            o_t = (acc[:KV_LATENT] / acc[KV_LATENT:KV_LATENT + 1]).T
            rows.append(jnp.concatenate([o_t[h * ATT_GROUP:(h + 1) * ATT_GROUP] for h in range(ATT_HEADS)], axis=1))
        o_all = jnp.concatenate(rows, axis=0).astype(BF16)
        pair = 2 * KV_LATENT
        return jnp.concatenate([_dot(o_all[:, p * pair:(p + 1) * pair], wuv_ref[p]) for p in range(HEAD_PAIRS)],
                               axis=1).astype(BF16)

    y_ref[0] = lax.cond(ranked_ties, lambda: attention(True), lambda: attention(False))


def _dsa_call(qlat, qidx, wrow, ckv_c, ckvt_c, kidx_c, m_ckv, m_ckvt, m_kidx, wuv_bd, topk):
    bn, rows, _ = qlat.shape
    nchunks, kc = ckv_c.shape[1], ckv_c.shape[2]
    nq = rows // Q_TILE

    def q_spec(width):
        return pl.BlockSpec((1, Q_TILE, width), lambda b, i: (b, i, 0))

    def k_spec(a):
        return pl.BlockSpec((1,) + a.shape[1:], lambda b, i: (b, 0, 0, 0))

    return pl.pallas_call(
        functools.partial(_dsa_kernel, topk=topk),
        grid=(bn, nq),
        in_specs=[q_spec(LAT_WIDTH), q_spec(IDX_WIDTH),
                  pl.BlockSpec((1, 1, GATE_ROWS, Q_TILE), lambda b, i: (b, i, 0, 0)),
                  k_spec(ckv_c), k_spec(ckvt_c), k_spec(kidx_c),
                  _resident(m_ckv.shape), _resident(m_ckvt.shape), _resident(m_kidx.shape), _resident(wuv_bd.shape)],
        out_specs=q_spec(ATT_WIDTH),
        out_shape=jax.ShapeDtypeStruct((bn, rows, ATT_WIDTH), BF16),
        scratch_shapes=[pltpu.VMEM((nchunks, kc, Q_TILE), F32),
                        pltpu.VMEM((Q_TILE // ATT_GROUP, ckvt_c.shape[2], ATT_HEADS * ATT_GROUP), F32),
                        pltpu.VMEM((Q_TILE // ATT_GROUP, kc, ATT_HEADS * ATT_GROUP), BF16)],
        compiler_params=pltpu.CompilerParams(
            dimension_semantics=("arbitrary", "arbitrary"), vmem_limit_bytes=VMEM_LIMIT_BYTES),
        name="dsa",
    )(qlat, qidx, wrow, ckv_c, ckvt_c, kidx_c, m_ckv, m_ckvt, m_kidx, wuv_bd)


def _split3(x):
    hi = x.astype(BF16)
    r = x - hi.astype(F32)
    mid = r.astype(BF16)
    lo = (r - mid.astype(F32)).astype(BF16)
    return hi, mid, lo


ML_EXT = ML_V_DIM + 16


def _mlstm_chunk(qk, qt, vt, g, gt, state):
    L = qk.shape[0]
    s_idx = lax.broadcasted_iota(jnp.int32, (L, L), 0)
    t_idx = lax.broadcasted_iota(jnp.int32, (L, L), 1)
    causal = s_idx <= t_idx
    b_cols = sum(_dot((t_idx <= s_idx).astype(BF16), part) for part in _split3(g))
    b_rows = sum(_dot(part, causal.astype(BF16)) for part in _split3(gt))
    ones_blk = jnp.where(lax.broadcasted_iota(jnp.int32, (ML_EXT - ML_V_DIM, L), 0) == 0, 1.0, 0.0).astype(BF16)
    kq = ML_HEADS * ML_QK_DIM

    outs, new_state = [], []
    for h in range(ML_HEADS):
        ce, m_prev = state[h]
        c_col = g[:, GATE_I0 + h:GATE_I0 + h + 1] - b_cols[:, GATE_F0 + h:GATE_F0 + h + 1]
        b_row = b_rows[GATE_F0 - GATE_W0 + h:GATE_F0 - GATE_W0 + h + 1, :]
        ig_row = gt[GATE_I0 - GATE_W0 + h:GATE_I0 - GATE_W0 + h + 1, :]
        qt_h = qt[h * ML_QK_DIM:(h + 1) * ML_QK_DIM, :]
        kh = qk[:, kq + h * ML_QK_DIM:kq + (h + 1) * ML_QK_DIM]
        vt_ext = jnp.concatenate([vt[h * ML_V_DIM:(h + 1) * ML_V_DIM, :], ones_blk], axis=0)

        d_t = jnp.where(causal, c_col + b_row, -jnp.inf)
        inter = b_row + m_prev
        m_t = jnp.maximum(jnp.max(d_t, axis=0, keepdims=True), inter)
        w_inter = jnp.exp(inter - m_t)
        s_t = _dot(kh, qt_h) * jnp.exp(d_t - m_t)
        r = _dot(vt_ext, s_t.astype(BF16)) + _dot(ce.astype(BF16), qt_h) * w_inter
        num = r[:ML_V_DIM]
        den = r[ML_V_DIM:ML_V_DIM + 1]
        hh = num / jnp.maximum(jnp.abs(den), jnp.exp(-m_t))
        mu = jnp.mean(hh, axis=0, keepdims=True)
        hc = hh - mu
        var = jnp.mean(hc * hc, axis=0, keepdims=True)
        outs.append((hc * lax.rsqrt(var + LN_EPS)).T)

        b_end = b_row[:, L - 1:L]
        g_row = b_end - b_row + ig_row
        m_new = jnp.maximum(b_end + m_prev, jnp.max(g_row, axis=1, keepdims=True))
        decay = jnp.exp(b_end + m_prev - m_new)
        weighted = (vt_ext.astype(F32) * jnp.exp(g_row - m_new)).astype(BF16)
        new_state.append((decay * ce + _dot(weighted, kh), m_new))
    return outs, new_state


def _mlstm_kernel(qk_ref, qt_ref, vt_ref, og_ref, gates_ref, gt_ref, mqk_ref, mqt_ref, mvt_ref, mgates_ref, mgt_ref,
                  ng_ref, y_ref,
                  ce0_ref, m0_ref):
    L = mqk_ref.shape[0]
    n_chunks = qk_ref.shape[1] // L
    norm_g = ng_ref[...]

    @pl.when(pl.program_id(0) == 0)
    def _():
        state = [(jnp.zeros((ML_EXT, ML_QK_DIM), F32), jnp.full((1, 1), M_INIT, F32)) for _ in range(ML_HEADS)]
        _, state = _mlstm_chunk(mqk_ref[...], mqt_ref[...], mvt_ref[...], mgates_ref[...], mgt_ref[...], state)
        for h in range(ML_HEADS):
            ce0_ref[h] = state[h][0]
            m0_ref[h] = jnp.broadcast_to(state[h][1], m0_ref.shape[1:])

    n_b = qk_ref.shape[0]
    state = [(ce0_ref[h], m0_ref[h][0:1, 0:1]) for h in range(ML_HEADS)] * n_b

    def body(c, flat):
        rows = pl.ds(pl.multiple_of(c * L, L), L)
        new_flat = []
        for b in range(n_b):
            state = [(flat[2 * (b * ML_HEADS + h)], flat[2 * (b * ML_HEADS + h) + 1]) for h in range(ML_HEADS)]
            outs, state = _mlstm_chunk(qk_ref[b, rows, :], qt_ref[b, c], vt_ref[b, c], gates_ref[b, rows, :],
                                       gt_ref[b, c], state)
            y = jnp.concatenate(outs, axis=1) * norm_g * og_ref[b, rows, :]
            y_ref[b, rows, :] = y.astype(BF16)
            new_flat += [x for pair in state for x in pair]
        return tuple(new_flat)

    lax.fori_loop(0, n_chunks, body, tuple(x for pair in state for x in pair))


def _mlstm_call(qk, qt_c, vt_c, og, gates, gt_c, mqk, mqt, mvt, mgates, mgt, norm_g):
    bn, rows, _ = qk.shape
    nb = ML_BATCH if bn % ML_BATCH == 0 else 1

    def b_spec(a):
        return pl.BlockSpec((nb,) + a.shape[1:], lambda b: (b,) + (0,) * (a.ndim - 1))

    consts = (mqk, mqt, mvt, mgates, mgt, norm_g)
    return pl.pallas_call(
        _mlstm_kernel,
        grid=(bn // nb,),
        in_specs=[b_spec(qk), b_spec(qt_c), b_spec(vt_c), b_spec(og), b_spec(gates), b_spec(gt_c)]
        + [_resident(c.shape) for c in consts],
        out_specs=pl.BlockSpec((nb, rows, ML_WIDTH), lambda b: (b, 0, 0)),
        out_shape=jax.ShapeDtypeStruct((bn, rows, ML_WIDTH), BF16),
        scratch_shapes=[pltpu.VMEM((ML_HEADS, ML_EXT, ML_QK_DIM), F32), pltpu.VMEM((ML_HEADS, SUBLANES, LANES), F32)],
        compiler_params=pltpu.CompilerParams(
            dimension_semantics=("arbitrary",), vmem_limit_bytes=VMEM_LIMIT_BYTES),
        name="mlstm",
    )(qk, qt_c, vt_c, og, gates, gt_c, *consts)


def _out_ffn_ln_kernel(ya_ref, ym_ref, h_ref, wo_ref, g2_ref, b2_ref, wg_ref, wu_ref, wd_ref, g3_ref, b3_ref,
                       o_ref, wg_s, wu_s, wd_s, h2_ref, acc0_ref, *, alpha, n_stage):
    i = pl.program_id(0)

    def mixed():
        mix = _dot(ya_ref[...], wo_ref[:ATT_WIDTH, :]) + _dot(ym_ref[...], wo_ref[ATT_WIDTH:, :])
        return _layer_norm(alpha * h_ref[...] + mix, g2_ref[...], b2_ref[...])

    @pl.when(i == 0)
    def _():
        h2_ref[...] = mixed()
        acc0_ref[...] = jnp.zeros(acc0_ref.shape, F32)

    @pl.when(i < n_stage)
    def _():
        _stage_ffn_weights(i, wg_ref, wu_ref, wd_ref, wg_s, wu_s, wd_s)
        acc0_ref[...] += _swiglu_chunk(h2_ref[...].astype(BF16), wg_s[i], wu_s[i], wd_s[i])

    @pl.when(i == n_stage - 1)
    def _():
        o_ref[...] = _layer_norm(alpha * h2_ref[...] + 0.5 * acc0_ref[...], g3_ref[...], b3_ref[...])

    @pl.when(i >= n_stage)
    def _():
        o_ref[...] = _ffn_ln(mixed(), wg_s, wu_s, wd_s, g3_ref[...], b3_ref[...], alpha)


def _out_ffn_ln_call(ya, ym, h, wo, g2, b2, wg, wu, wd, g3, b3, alpha, tm):
    rows, d = h.shape
    n_stage, w_specs, w_scratch = _ffn_weight_specs(d, wg.shape[1])

    def row_spec(width):
        return pl.BlockSpec((tm, width), lambda i: (_staged_tile_index(i, n_stage), 0))

    return pl.pallas_call(
        functools.partial(_out_ffn_ln_kernel, alpha=alpha, n_stage=n_stage),
        grid=(n_stage - 1 + rows // tm,),
        in_specs=[row_spec(ya.shape[1]), row_spec(ym.shape[1]), row_spec(d),
                  _resident(wo.shape), _resident(g2.shape), _resident(b2.shape)] + w_specs
        + [_resident(g3.shape), _resident(b3.shape)],
        out_specs=row_spec(d),
        out_shape=jax.ShapeDtypeStruct((rows, d), F32),
        scratch_shapes=w_scratch + [pltpu.VMEM((tm, d), F32), pltpu.VMEM((tm, d), F32)],
        compiler_params=pltpu.CompilerParams(
            dimension_semantics=("arbitrary",), vmem_limit_bytes=VMEM_LIMIT_BYTES),
        name="out_ffn_ln",
    )(ya, ym, h, wo, g2, b2, wg, wu, wd, g3, b3)


def _block_diag(w):
    nh, a, b = w.shape
    eye = jnp.eye(nh, dtype=w.dtype)
    return (eye[:, None, :, None] * w[:, :, None, :]).reshape(nh * a, nh * b)


def _pad_rows(a, rows, value=0.0):
    return jnp.pad(a, ((0, rows - a.shape[0]), (0, 0)), constant_values=value)


def kernel(x, meta_tokens, ln1_g, ln1_b, ffn1_w_gate, ffn1_w_up, ffn1_w_down, w_in, w_uk, w_uv, kv_norm_g,
           conv_w, b_igate, b_fgate, ml_norm_g, w_out, ln2_g, ln2_b, ffn2_w_gate, ffn2_w_up, ffn2_w_down,
           ln3_g, ln3_b):
    depth = ln1_g.shape[0]
    assert depth == 1, "the meta-token shortcut below is only valid for a single layer"
    bsz, seq, d = x.shape
    assert seq % ROW_TILE == 0 and seq % ML_CHUNK == 0 and seq % Q_TILE == 0
    assert Q_TILE == GT_CHUNK and ML_CHUNK == GT_CHUNK and KEY_CHUNK == GT_CHUNK and ROW_TILE % GT_CHUNK == 0
    alpha = (2 * depth) ** 0.25
    topk = min(TOPK_MAX, seq // 4)

    row2 = lambda p: p[0].reshape(1, -1).astype(F32)
    bf = lambda w: w[0].astype(BF16)

    w_t = jnp.swapaxes(w_in[0], 0, 1)
    sizes = (ATT_WIDTH, KV_LATENT, IDX_WIDTH, IDX_DIM, IDX_HEADS, MLQK_WIDTH, ML_WIDTH, ML_WIDTH, ML_HEADS, ML_HEADS)
    offs = [sum(sizes[:n]) for n in range(len(sizes) + 1)]
    assert w_t.shape[0] == offs[-1]
    wa = jnp.concatenate([w_t[offs[0]:offs[4]], w_t[offs[4]:offs[5]], w_t[offs[8]:offs[10]],
                          jnp.zeros((LANES - GATE_END, d), F32)], axis=0).astype(BF16)
    wm = w_t[offs[5]:offs[8]].astype(BF16)
    gbias = jnp.concatenate([jnp.zeros((GATE_I0,), F32), b_igate[0], b_fgate[0],
                             jnp.zeros((LANES - GATE_END,), F32)]).reshape(1, LANES)
    wuk_bd = jnp.stack([_block_diag(w_uk[0][2 * p:2 * p + 2]) for p in range(HEAD_PAIRS)]).astype(BF16)
    wuv_bd = jnp.stack([_block_diag(w_uv[0][2 * p:2 * p + 2]) for p in range(HEAD_PAIRS)]).astype(BF16)
    kvg = row2(kv_norm_g)
    convw = conv_w[0].astype(F32)

    h1, h1_meta = _ffn_ln_call(x.reshape(bsz * seq, d), meta_tokens.astype(F32), ffn1_w_gate[0], ffn1_w_up[0],
                               ffn1_w_down[0], row2(ln1_g), row2(ln1_b), alpha, FFN_TILE)
    zero_tail = jnp.zeros((CONV_HIST, MLQK_WIDTH), F32)
    (_, m_ckv, _, m_kidx, m_qk, m_v, _, m_gates, m_tail) = _inproj_call(
        h1_meta[None], zero_tail, wa, wm, wuk_bd, kvg, convw, gbias, N_META)

    (qlat, ckv, qidx, kidx, qk, _, og, gates, _, gates_t, ckv_t, v_t, q_t) = _inproj_call(
        h1.reshape(bsz, seq, d), m_tail[0], wa, wm, wuk_bd, kvg, convw, gbias, ROW_TILE)

    nchunks = seq // KEY_CHUNK
    ckv_c = ckv.reshape(bsz, nchunks, KEY_CHUNK, KV_LATENT)
    kidx_c = kidx.reshape(bsz, nchunks, KEY_CHUNK, IDX_DIM)
    ones_rows = jnp.zeros((ATT_EXT - KV_LATENT, N_META), BF16).at[0].set(1.0)
    m_ckvt = jnp.concatenate([m_ckv[0].T, ones_rows], axis=0)
    y_att = _dsa_call(qlat, qidx, gates_t, ckv_c, ckv_t, kidx_c,
                      m_ckv[0], m_ckvt, m_kidx[0], wuv_bd, topk)

    lane = jnp.arange(LANES)
    pad_gate = jnp.where((lane >= GATE_I0) & (lane < GATE_F0), NEG_BIG, 0.0).astype(F32)
    mg = jnp.concatenate([m_gates[0], jnp.broadcast_to(pad_gate, (ML_CHUNK - N_META, LANES))], axis=0)
    gate_lanes = slice(GATE_W0, GATE_W0 + GATE_ROWS)
    m_qk_pad = _pad_rows(m_qk[0], ML_CHUNK)
    y_ml = _mlstm_call(qk, q_t, v_t, og, gates, gates_t,
                       m_qk_pad, m_qk_pad[:, :MLQK_WIDTH // 2].T, _pad_rows(m_v[0], ML_CHUNK).T, mg, mg[:, gate_lanes].T,
                       row2(ml_norm_g))

    out = _out_ffn_ln_call(
        y_att.reshape(bsz * seq, ATT_WIDTH), y_ml.reshape(bsz * seq, ML_WIDTH), h1, bf(w_out),
        row2(ln2_g), row2(ln2_b), ffn2_w_gate[0], ffn2_w_up[0], ffn2_w_down[0], row2(ln3_g), row2(ln3_b),
        alpha, FFN_TILE)
    return out.reshape(bsz, seq, d)
```

```python
import functools

import jax
import jax.numpy as jnp
from jax import lax
from jax.experimental import pallas as pl
from jax.experimental.pallas import tpu as pltpu

F32 = jnp.float32
BF16 = jnp.bfloat16

N_META = 16
ATT_HEADS = 8
ATT_HEAD_DIM = 64
KV_LATENT = 128
IDX_HEADS = 4
IDX_DIM = 64
TOPK_MAX = 256
ML_HEADS = 4
ML_V_DIM = 128
ML_QK_DIM = 64
CONV_WIDTH = 4
GATE_SOFTCAP = 15.0
M_INIT = -1e30
LN_EPS = 1e-5
NEG_BIG = -1e30
LOG2_E = 1.4426950408889634

LANES = 128
SUBLANES = 8
VMEM_BYTES_V7X = 64 * 1024 * 1024
VMEM_LIMIT_BYTES = VMEM_BYTES_V7X * 7 // 8

FF_CHUNK = 256
ROW_TILE = 512
FFN_TILE = 512
Q_TILE = 256
ATT_GROUP = 128
KEY_CHUNK = 256
N_BISECT = 16
REDUCE_ROWS = 32
ML_BATCH = 2
ML_CHUNK = 256


def _dot(a, b):
    return jnp.dot(a, b, preferred_element_type=F32)


def _layer_norm(z, g, b):
    mu = jnp.mean(z, axis=-1, keepdims=True)
    zc = z - mu
    var = jnp.mean(zc * zc, axis=-1, keepdims=True)
    return zc * lax.rsqrt(var + LN_EPS) * g + b


def _sigmoid(x):
    return 1.0 / (1.0 + jnp.exp(-x))


def _swiglu_chunk(xb, wg_c, wu_c, wd_c):
    g = _dot(xb, wg_c)
    u = _dot(xb, wu_c)
    return _dot((g * _sigmoid(g) * u).astype(BF16), wd_c)


def _ffn_ln(x, wg_s, wu_s, wd_s, g, b, alpha):
    xb = x.astype(BF16)
    acc = jnp.zeros(x.shape, F32)
    for c in range(wg_s.shape[0]):
        acc = acc + _swiglu_chunk(xb, wg_s[c], wu_s[c], wd_s[c])
    return _layer_norm(alpha * x + 0.5 * acc, g, b)


def _stage_ffn_weights(step, wg_ref, wu_ref, wd_ref, wg_s, wu_s, wd_s):
    wg_s[step] = wg_ref[...].astype(BF16)
    wu_s[step] = wu_ref[...].astype(BF16)
    wd_s[step] = wd_ref[...].astype(BF16)


def _ffn_weight_specs(d, d_ff):
    n = d_ff // FF_CHUNK
    col = pl.BlockSpec((d, FF_CHUNK), lambda i: (0, jnp.minimum(i, n - 1)))
    row = pl.BlockSpec((FF_CHUNK, d), lambda i: (jnp.minimum(i, n - 1), 0))
    scratch = [pltpu.VMEM((n, d, FF_CHUNK), BF16), pltpu.VMEM((n, d, FF_CHUNK), BF16),
               pltpu.VMEM((n, FF_CHUNK, d), BF16)]
    return n, [col, col, row], scratch


def _staged_tile_index(i, n_stage):
    return jnp.maximum(i - (n_stage - 1), 0)


def _ffn_ln_kernel(x_ref, meta_ref, wg_ref, wu_ref, wd_ref, g_ref, b_ref, o_ref, ometa_ref,
                   wg_s, wu_s, wd_s, macc_ref, acc0_ref, *, alpha, n_stage):
    i = pl.program_id(0)

    @pl.when(i == 0)
    def _():
        macc_ref[...] = jnp.zeros(macc_ref.shape, F32)
        acc0_ref[...] = jnp.zeros(acc0_ref.shape, F32)

    @pl.when(i < n_stage)
    def _():
        _stage_ffn_weights(i, wg_ref, wu_ref, wd_ref, wg_s, wu_s, wd_s)
        macc_ref[...] += _swiglu_chunk(meta_ref[...].astype(BF16), wg_s[i], wu_s[i], wd_s[i])
        acc0_ref[...] += _swiglu_chunk(x_ref[...].astype(BF16), wg_s[i], wu_s[i], wd_s[i])

    @pl.when(i == n_stage - 1)
    def _():
        ometa_ref[...] = _layer_norm(alpha * meta_ref[...] + 0.5 * macc_ref[...], g_ref[...], b_ref[...])
        o_ref[...] = _layer_norm(alpha * x_ref[...] + 0.5 * acc0_ref[...], g_ref[...], b_ref[...])

    @pl.when(i >= n_stage)
    def _():
        o_ref[...] = _ffn_ln(x_ref[...], wg_s, wu_s, wd_s, g_ref[...], b_ref[...], alpha)


def _resident(shape):
    return pl.BlockSpec(shape, lambda *_: (0,) * len(shape), pipeline_mode=pl.Buffered(1))


def _ffn_ln_call(x, meta, wg, wu, wd, g, b, alpha, tm):
    rows, d = x.shape
    n_stage, w_specs, w_scratch = _ffn_weight_specs(d, wg.shape[1])
    row_spec = pl.BlockSpec((tm, d), lambda i: (_staged_tile_index(i, n_stage), 0))
    return pl.pallas_call(
        functools.partial(_ffn_ln_kernel, alpha=alpha, n_stage=n_stage),
        grid=(n_stage - 1 + rows // tm,),
        in_specs=[row_spec, _resident(meta.shape)] + w_specs + [_resident(g.shape), _resident(b.shape)],
        out_specs=[row_spec, pl.BlockSpec(meta.shape, lambda i: (0, 0))],
        out_shape=[jax.ShapeDtypeStruct((rows, d), F32), jax.ShapeDtypeStruct(meta.shape, F32)],
        scratch_shapes=w_scratch + [pltpu.VMEM(meta.shape, F32), pltpu.VMEM((tm, d), F32)],
        compiler_params=pltpu.CompilerParams(
            dimension_semantics=("arbitrary",), vmem_limit_bytes=VMEM_LIMIT_BYTES),
        name="ffn_ln",
    )(x, meta, wg, wu, wd, g, b)


ATT_WIDTH = ATT_HEADS * ATT_HEAD_DIM
IDX_WIDTH = IDX_HEADS * IDX_DIM
MLQK_WIDTH = 2 * ML_HEADS * ML_QK_DIM
ML_WIDTH = ML_HEADS * ML_V_DIM
LAT_WIDTH = ATT_HEADS * KV_LATENT
CONV_HIST = SUBLANES
GATE_W0 = IDX_DIM
GATE_I0, GATE_F0, GATE_END = GATE_W0 + IDX_HEADS, GATE_W0 + IDX_HEADS + ML_HEADS, GATE_W0 + IDX_HEADS + 2 * ML_HEADS
GATE_ROWS = 2 * SUBLANES
GT_CHUNK = 256
ATT_EXT = KV_LATENT + 2 * SUBLANES
HEAD_PAIRS = ATT_HEADS // 2


def _inproj_kernel(h_ref, tail_ref, wa_ref, wm_ref, wuk_ref, kvg_ref, convw_ref, gbias_ref,
                   qlat_ref, ckv_ref, qidx_ref, kidx_ref, qk_ref, v_ref, og_ref, gates_ref, tailout_ref, *rest):
    carry_ref, wa_s, wm_s = rest[-3:]
    tm = h_ref.shape[1]

    @pl.when((pl.program_id(0) == 0) & (pl.program_id(1) == 0))
    def _():
        wa_s[...] = wa_ref[...].astype(F32).T.astype(BF16)
        wm_s[...] = wm_ref[...].astype(F32).T.astype(BF16)

    @pl.when(pl.program_id(1) == 0)
    def _():
        carry_ref[...] = tail_ref[...]

    xb = h_ref[0].astype(BF16)

    pa = _dot(xb, wa_s[...])
    q_a = pa[:, :ATT_WIDTH].astype(BF16)
    c0 = ATT_WIDTH
    ckv = pa[:, c0:c0 + KV_LATENT]
    c1 = c0 + KV_LATENT
    ckv = ckv * lax.rsqrt(jnp.mean(ckv * ckv, axis=-1, keepdims=True) + LN_EPS) * kvg_ref[...]
    ckv_ref[0] = ckv.astype(BF16)
    qidx_ref[0] = pa[:, c1:c1 + IDX_WIDTH].astype(BF16)
    c2 = c1 + IDX_WIDTH
    kidx_ref[0] = pa[:, c2:c2 + IDX_DIM].astype(BF16)
    pair_in, pair_out = 2 * ATT_HEAD_DIM, 2 * KV_LATENT
    for p in range(HEAD_PAIRS):
        ql = _dot(q_a[:, p * pair_in:(p + 1) * pair_in], wuk_ref[p])
        qlat_ref[0, :, p * pair_out:(p + 1) * pair_out] = (ql * (ATT_HEAD_DIM ** -0.5 * LOG2_E)).astype(BF16)

    pm = _dot(xb, wm_s[...])
    qk_raw = pm[:, :MLQK_WIDTH]
    v_ref[0] = pm[:, MLQK_WIDTH:MLQK_WIDTH + ML_WIDTH].astype(BF16)
    og_ref[0] = _sigmoid(pm[:, MLQK_WIDTH + ML_WIDTH:])

    ext = jnp.concatenate([carry_ref[...], qk_raw], axis=0)
    cw = convw_ref[...]
    conv = jnp.zeros_like(qk_raw)
    for j in range(CONV_WIDTH):
        s0 = CONV_HIST - (CONV_WIDTH - 1) + j
        conv = conv + ext[s0:s0 + tm] * cw[j:j + 1]
    act = conv * _sigmoid(conv)
    half = MLQK_WIDTH // 2
    qk_ref[0, :, :half] = act[:, :half].astype(BF16)
    qk_ref[0, :, half:] = (act[:, half:] * (ML_QK_DIM ** -0.5)).astype(BF16)
    carry_ref[...] = qk_raw[tm - CONV_HIST:]
    tailout_ref[0] = qk_raw[tm - CONV_HIST:]

    gr = pa[:, c2 + IDX_DIM - GATE_W0:]
    lane = lax.broadcasted_iota(jnp.int32, gr.shape, 1)
    sc = GATE_SOFTCAP * jnp.tanh((gr + gbias_ref[...]) / GATE_SOFTCAP)
    lf = -(jnp.maximum(-sc, 0.0) + jnp.log1p(jnp.exp(-jnp.abs(sc))))
    w_scaled = gr * (IDX_HEADS ** -0.5 * IDX_DIM ** -0.5)
    gates = jnp.where((lane < GATE_W0) | (lane >= GATE_END), 0.0,
                      jnp.where(lane < GATE_I0, w_scaled, jnp.where(lane < GATE_F0, sc, lf)))
    gates_ref[0] = gates
    if len(rest) == 7:
        gt_ref, ckvt_ref, vt_ref, qt_ref = rest[:4]
        gates_t = gates.T[GATE_W0:GATE_W0 + GATE_ROWS]
        ckv_t = ckv.T.astype(BF16)
        v_t = pm[:, MLQK_WIDTH:MLQK_WIDTH + ML_WIDTH].T.astype(BF16)
        q_t = act[:, :half].T.astype(BF16)
        ones_blk = jnp.where(lax.broadcasted_iota(jnp.int32, (ATT_EXT - KV_LATENT, GT_CHUNK), 0) == 0,
                             1.0, 0.0).astype(BF16)
        for j in range(tm // GT_CHUNK):
            piece = slice(j * GT_CHUNK, (j + 1) * GT_CHUNK)
            gt_ref[0, j] = gates_t[:, piece]
            ckvt_ref[0, j] = jnp.concatenate([ckv_t[:, piece], ones_blk], axis=0)
            vt_ref[0, j] = v_t[:, piece]
            qt_ref[0, j] = q_t[:, piece]


def _inproj_call(h, tail, wa, wm, wuk_bd, kvg, convw, gbias, tm):
    bn, rows, d = h.shape
    nblk = rows // tm
    emit_gt = tm % GT_CHUNK == 0

    def row_spec(width):
        return pl.BlockSpec((1, tm, width), lambda b, j: (b, j, 0))

    outs = [
        (LAT_WIDTH, BF16), (KV_LATENT, BF16), (IDX_WIDTH, BF16), (IDX_DIM, BF16),
        (MLQK_WIDTH, BF16), (ML_WIDTH, BF16), (ML_WIDTH, F32), (LANES, F32),
    ]
    out_shape = [jax.ShapeDtypeStruct((bn, rows, w), dt) for w, dt in outs]
    out_specs = [row_spec(w) for w, _ in outs]
    out_shape.append(jax.ShapeDtypeStruct((bn, CONV_HIST, MLQK_WIDTH), F32))
    out_specs.append(pl.BlockSpec((1, CONV_HIST, MLQK_WIDTH), lambda b, j: (b, 0, 0)))
    if emit_gt:
        per_tile = tm // GT_CHUNK
        for height, dt in ((GATE_ROWS, F32), (ATT_EXT, BF16), (ML_WIDTH, BF16), (MLQK_WIDTH // 2, BF16)):
            out_shape.append(jax.ShapeDtypeStruct((bn, rows // GT_CHUNK, height, GT_CHUNK), dt))
            out_specs.append(pl.BlockSpec((1, per_tile, height, GT_CHUNK), lambda b, j: (b, j, 0, 0)))
    return pl.pallas_call(
        _inproj_kernel,
        grid=(bn, nblk),
        in_specs=[
            row_spec(d),
            _resident(tail.shape), _resident(wa.shape), _resident(wm.shape),
            _resident(wuk_bd.shape), _resident(kvg.shape), _resident(convw.shape), _resident(gbias.shape),
        ],
        out_specs=out_specs,
        out_shape=out_shape,
        scratch_shapes=[pltpu.VMEM((CONV_HIST, MLQK_WIDTH), F32),
                        pltpu.VMEM(wa.shape[::-1], BF16), pltpu.VMEM(wm.shape[::-1], BF16)],
        compiler_params=pltpu.CompilerParams(
            dimension_semantics=("arbitrary", "arbitrary"), vmem_limit_bytes=VMEM_LIMIT_BYTES),
        name="in_proj",
    )(h, tail, wa, wm, wuk_bd, kvg, convw, gbias)


def _dsa_kernel(qlat_ref, qidx_ref, wrow_ref, ckv_ref, ckvt_ref, kidx_ref, mckv_ref, mckvt_ref, mkidx_ref,
                wuv_ref, y_ref, s_ref, acc_ref, p_ref, *, topk):
    _, kc, tq = s_ref.shape
    i = pl.program_id(1)
    nch = ((i + 1) * tq + kc - 1) // kc
    qreal = i * tq + lax.broadcasted_iota(jnp.int32, (1, tq), 1)
    kf = float(topk)

    wrow = wrow_ref[0, 0]
    qidx = qidx_ref[0]
    q_idx_t = qidx.astype(F32).T
    q_idx_all_t = jnp.concatenate([q_idx_t[h * IDX_DIM:(h + 1) * IDX_DIM] for h in range(IDX_HEADS)],
                                  axis=1).astype(BF16)
    wi = [wrow[h:h + 1, :] for h in range(IDX_HEADS)]

    def scores(k_rows):
        lg = _dot(k_rows, q_idx_all_t)
        sc = jnp.zeros((k_rows.shape[0], tq), F32)
        for h in range(IDX_HEADS):
            sc = sc + jnp.maximum(lg[:, h * tq:(h + 1) * tq], 0.0) * wi[h]
        return sc

    s_meta = scores(mkidx_ref[...])

    def score_chunk(c, lo, hi):
        sc = scores(kidx_ref[0, c])
        valid = c * kc + lax.broadcasted_iota(jnp.int32, (kc, tq), 0) <= qreal
        s_ref[c] = jnp.where(valid, sc, -jnp.inf)
        groups = (kc // REDUCE_ROWS, REDUCE_ROWS, tq)
        lo = jnp.minimum(lo, jnp.min(sc.reshape(groups), axis=0))
        hi = jnp.maximum(hi, jnp.max(sc.reshape(groups), axis=0))
        return lo, hi

    def score_pair(c2, carry):
        lo, hi = score_chunk(2 * c2, *carry)
        return score_chunk(jnp.minimum(2 * c2 + 1, nch - 1), lo, hi)

    lo, hi = lax.fori_loop(0, (nch + 1) // 2, score_pair,
                           (jnp.full((REDUCE_ROWS, tq), jnp.inf, F32), jnp.full((REDUCE_ROWS, tq), -jnp.inf, F32)))
    lo = jnp.minimum(jnp.min(lo, axis=0, keepdims=True), jnp.min(s_meta, axis=0, keepdims=True))
    hi = jnp.maximum(jnp.max(hi, axis=0, keepdims=True), jnp.max(s_meta, axis=0, keepdims=True))

    def key_reduce(reduce, combine, per_chunk, init):
        def body(c, acc):
            x = per_chunk(s_ref[c]).reshape(kc // REDUCE_ROWS, REDUCE_ROWS, tq)
            return combine(acc, reduce(x, axis=0))
        acc = lax.fori_loop(0, nch, body, jnp.full((REDUCE_ROWS, tq), init, F32))
        return combine(reduce(acc, axis=0, keepdims=True), reduce(per_chunk(s_meta), axis=0, keepdims=True))

    def count(pred):
        return key_reduce(jnp.sum, jnp.add, lambda sc: jnp.where(pred(sc), 1.0, 0.0), 0.0)

    def max_where(pred):
        return key_reduce(jnp.max, jnp.maximum, lambda sc: jnp.where(pred(sc), sc, -jnp.inf), -jnp.inf)

    def bisect(_, carry):
        lo, hi = carry
        mid = 0.5 * lo + 0.5 * hi
        up = count(lambda sc: sc > mid) >= kf
        return jnp.where(up, mid, lo), jnp.where(up, hi, mid)

    lo, hi = lax.fori_loop(0, N_BISECT, bisect, (lo, hi))

    n_valid = (qreal + (N_META + 1)).astype(F32)
    small = n_valid <= kf
    cand = max_where(lambda sc: sc <= hi)
    n_ge = count(lambda sc: sc >= cand)
    done = jnp.where(small | (n_ge >= kf), 1.0, 0.0)

    def not_finished(state):
        return jnp.min(state[1]) < 0.5

    def step_down(state):
        cand, done, _ = state
        nxt = jnp.where(done > 0.5, cand, max_where(lambda sc: sc < cand))
        n_ge = count(lambda sc: sc >= nxt)
        return nxt, jnp.where(n_ge >= kf, 1.0, done), n_ge

    cand, _, n_ge = lax.while_loop(not_finished, step_down, step_down((cand, done, n_ge)))
    thr = jnp.where(small, -jnp.inf, cand)
    n_eq = count(lambda sc: sc == thr)
    need = jnp.where(small, 0.0, kf - (n_ge - n_eq))
    ranked_ties = jnp.max(jnp.where(n_eq > need, 1.0, 0.0)) > 0.5

    qlat = qlat_ref[0]
    n_groups = tq // ATT_GROUP
    onehot = (lax.broadcasted_iota(jnp.int32, (ATT_GROUP, ATT_GROUP), 0)
              == lax.broadcasted_iota(jnp.int32, (ATT_GROUP, ATT_GROUP), 1)).astype(BF16)
    qlat32 = qlat.astype(F32)
    q_aug_t = [jnp.concatenate(
        [jnp.concatenate([qlat32[g * ATT_GROUP:(g + 1) * ATT_GROUP, h * KV_LATENT:(h + 1) * KV_LATENT].T
                          for h in range(ATT_HEADS)], axis=1).astype(BF16),
         jnp.concatenate([onehot] * ATT_HEADS, axis=1)], axis=0)
        for g in range(n_groups)]
    hq = ATT_HEADS * ATT_GROUP

    def lower_tri(n):
        return (lax.broadcasted_iota(jnp.int32, (n, n), 1) <= lax.broadcasted_iota(jnp.int32, (n, n), 0)).astype(BF16)

    def attention(ranked):
        def mask_bias(sc, eq_seen):
            if not ranked:
                return jnp.where(sc >= thr, 0.0, NEG_BIG).astype(BF16), eq_seen
            n = sc.shape[0]
            eq = sc == thr
            rank = _dot(lower_tri(n), jnp.where(eq, 1.0, 0.0).astype(BF16)) + eq_seen
            keep = (sc > thr) | (eq & (rank <= need))
            return jnp.where(keep, 0.0, NEG_BIG).astype(BF16), rank[n - 1:n, :]

        def logits(g, kv, bias):
            k_aug = jnp.concatenate([kv, bias[:, g * ATT_GROUP:(g + 1) * ATT_GROUP]], axis=1)
            return _dot(k_aug, q_aug_t[g])

        def fold_in(g, c_prev, a_prev):
            acc_ref[g] = a_prev * acc_ref[g] + _dot(ckvt_ref[0, c_prev], p_ref[g])

        def attend(c, carry):
            bias, eq_seen = mask_bias(s_ref[c], carry[0])
            out = [eq_seen]
            for g in range(n_groups):
                m, a_prev = carry[1 + 2 * g:3 + 2 * g]
                fold_in(g, c - 1, a_prev)
                s = logits(g, ckv_ref[0, c], bias)
                m_new = jnp.maximum(m, jnp.max(s, axis=0, keepdims=True))
                p_ref[g] = jnp.exp2(s - m_new).astype(BF16)
                out += [m_new, jnp.exp2(m - m_new)]
            return tuple(out)

        bias_m, eq_seen = mask_bias(s_meta, jnp.zeros((1, tq), F32))
        bias_0, eq_seen = mask_bias(s_ref[0], eq_seen)
        kv_first = jnp.concatenate([mckv_ref[...], ckv_ref[0, 0]], axis=0)
        bias_first = jnp.concatenate([bias_m, bias_0], axis=0)
        init = [eq_seen]
        for g in range(n_groups):
            s = logits(g, kv_first, bias_first)
            m = jnp.max(s, axis=0, keepdims=True)
            p = jnp.exp2(s - m).astype(BF16)
            acc_ref[g] = _dot(mckvt_ref[...], p[:N_META])
            p_ref[g] = p[N_META:]
            init += [m, jnp.ones((1, hq), F32)]
        carry = lax.fori_loop(1, nch, attend, tuple(init))
        rows = []
        for g in range(n_groups):
            fold_in(g, nch - 1, carry[2 + 2 * g])
            acc = acc_ref[g]
            o_t = (acc[:KV_LATENT] / acc[KV_LATENT:KV_LATENT + 1]).T
            rows.append(jnp.concatenate([o_t[h * ATT_GROUP:(h + 1) * ATT_GROUP] for h in range(ATT_HEADS)], axis=1))
        o_all = jnp.concatenate(rows, axis=0).astype(BF16)
        pair = 2 * KV_LATENT
        return jnp.concatenate([_dot(o_all[:, p * pair:(p + 1) * pair], wuv_ref[p]) for p in range(HEAD_PAIRS)],
                               axis=1).astype(BF16)

    y_ref[0] = lax.cond(ranked_ties, lambda: attention(True), lambda: attention(False))


def _dsa_call(qlat, qidx, wrow, ckv_c, ckvt_c, kidx_c, m_ckv, m_ckvt, m_kidx, wuv_bd, topk):
    bn, rows, _ = qlat.shape
    nchunks, kc = ckv_c.shape[1], ckv_c.shape[2]
    nq = rows // Q_TILE

    def q_spec(width):
        return pl.BlockSpec((1, Q_TILE, width), lambda b, i: (b, i, 0))

    def k_spec(a):
        return pl.BlockSpec((1,) + a.shape[1:], lambda b, i: (b, 0, 0, 0))

    return pl.pallas_call(
        functools.partial(_dsa_kernel, topk=topk),
        grid=(bn, nq),
        in_specs=[q_spec(LAT_WIDTH), q_spec(IDX_WIDTH),
                  pl.BlockSpec((1, 1, GATE_ROWS, Q_TILE), lambda b, i: (b, i, 0, 0)),
                  k_spec(ckv_c), k_spec(ckvt_c), k_spec(kidx_c),
                  _resident(m_ckv.shape), _resident(m_ckvt.shape), _resident(m_kidx.shape), _resident(wuv_bd.shape)],
        out_specs=q_spec(ATT_WIDTH),
        out_shape=jax.ShapeDtypeStruct((bn, rows, ATT_WIDTH), BF16),
        scratch_shapes=[pltpu.VMEM((nchunks, kc, Q_TILE), F32),
                        pltpu.VMEM((Q_TILE // ATT_GROUP, ckvt_c.shape[2], ATT_HEADS * ATT_GROUP), F32),
                        pltpu.VMEM((Q_TILE // ATT_GROUP, kc, ATT_HEADS * ATT_GROUP), BF16)],
        compiler_params=pltpu.CompilerParams(
            dimension_semantics=("arbitrary", "arbitrary"), vmem_limit_bytes=VMEM_LIMIT_BYTES),
        name="dsa",
    )(qlat, qidx, wrow, ckv_c, ckvt_c, kidx_c, m_ckv, m_ckvt, m_kidx, wuv_bd)


def _split3(x):
    hi = x.astype(BF16)
    r = x - hi.astype(F32)
    mid = r.astype(BF16)
    lo = (r - mid.astype(F32)).astype(BF16)
    return hi, mid, lo


ML_EXT = ML_V_DIM + 16


def _mlstm_chunk(qk, qt, vt, g, gt, state):
    L = qk.shape[0]
    s_idx = lax.broadcasted_iota(jnp.int32, (L, L), 0)
    t_idx = lax.broadcasted_iota(jnp.int32, (L, L), 1)
    causal = s_idx <= t_idx
    w = g.shape[1]
    cols3 = _dot((t_idx <= s_idx).astype(BF16), jnp.concatenate(_split3(g), axis=1))
    b_cols = cols3[:, :w] + cols3[:, w:2 * w] + cols3[:, 2 * w:]
    r = gt.shape[0]
    rows3 = _dot(jnp.concatenate(_split3(gt), axis=0), causal.astype(BF16))
    b_rows = rows3[:r] + rows3[r:2 * r] + rows3[2 * r:]
    ones_blk = jnp.where(lax.broadcasted_iota(jnp.int32, (ML_EXT - ML_V_DIM, L), 0) == 0, 1.0, 0.0).astype(BF16)
    kq = ML_HEADS * ML_QK_DIM

    outs, new_state = [], []
    for h in range(ML_HEADS):
        ce, m_prev = state[h]
        c_col = g[:, GATE_I0 + h:GATE_I0 + h + 1] - b_cols[:, GATE_F0 + h:GATE_F0 + h + 1]
        b_row = b_rows[GATE_F0 - GATE_W0 + h:GATE_F0 - GATE_W0 + h + 1, :]
        ig_row = gt[GATE_I0 - GATE_W0 + h:GATE_I0 - GATE_W0 + h + 1, :]
        qt_h = qt[h * ML_QK_DIM:(h + 1) * ML_QK_DIM, :]
        kh = qk[:, kq + h * ML_QK_DIM:kq + (h + 1) * ML_QK_DIM]
        vt_ext = jnp.concatenate([vt[h * ML_V_DIM:(h + 1) * ML_V_DIM, :], ones_blk], axis=0)

        d_t = jnp.where(causal, c_col + b_row, -jnp.inf)
        inter = b_row + m_prev
        m_t = jnp.maximum(jnp.max(d_t, axis=0, keepdims=True), inter)
        w_inter = jnp.exp(inter - m_t)
        s_t = _dot(kh, qt_h) * jnp.exp(d_t - m_t)
        r = _dot(vt_ext, s_t.astype(BF16)) + _dot(ce.astype(BF16), qt_h) * w_inter
        num = r[:ML_V_DIM]
        den = r[ML_V_DIM:ML_V_DIM + 1]
        hh = num / jnp.maximum(jnp.abs(den), jnp.exp(-m_t))
        mu = jnp.mean(hh, axis=0, keepdims=True)
        hc = hh - mu
        var = jnp.mean(hc * hc, axis=0, keepdims=True)
        outs.append((hc * lax.rsqrt(var + LN_EPS)).T)

        b_end = b_row[:, L - 1:L]
        g_row = b_end - b_row + ig_row
        m_new = jnp.maximum(b_end + m_prev, jnp.max(g_row, axis=1, keepdims=True))
        decay = jnp.exp(b_end + m_prev - m_new)
        weighted = (vt_ext.astype(F32) * jnp.exp(g_row - m_new)).astype(BF16)
        new_state.append((decay * ce + _dot(weighted, kh), m_new))
    return outs, new_state


def _mlstm_kernel(qk_ref, qt_ref, vt_ref, og_ref, gates_ref, gt_ref, mqk_ref, mqt_ref, mvt_ref, mgates_ref, mgt_ref,
                  ng_ref, y_ref,
                  ce0_ref, m0_ref):
    L = mqk_ref.shape[0]
    n_chunks = qk_ref.shape[1] // L
    norm_g = ng_ref[...]

    @pl.when(pl.program_id(0) == 0)
    def _():
        state = [(jnp.zeros((ML_EXT, ML_QK_DIM), F32), jnp.full((1, 1), M_INIT, F32)) for _ in range(ML_HEADS)]
        _, state = _mlstm_chunk(mqk_ref[...], mqt_ref[...], mvt_ref[...], mgates_ref[...], mgt_ref[...], state)
        for h in range(ML_HEADS):
            ce0_ref[h] = state[h][0]
            m0_ref[h] = jnp.broadcast_to(state[h][1], m0_ref.shape[1:])

    n_b = qk_ref.shape[0]
    state = [(ce0_ref[h], m0_ref[h][0:1, 0:1]) for h in range(ML_HEADS)] * n_b

    def body(c, flat):
        rows = pl.ds(pl.multiple_of(c * L, L), L)
        new_flat = []
        for b in range(n_b):
            state = [(flat[2 * (b * ML_HEADS + h)], flat[2 * (b * ML_HEADS + h) + 1]) for h in range(ML_HEADS)]
            outs, state = _mlstm_chunk(qk_ref[b, rows, :], qt_ref[b, c], vt_ref[b, c], gates_ref[b, rows, :],
                                       gt_ref[b, c], state)
            y = jnp.concatenate(outs, axis=1) * norm_g * og_ref[b, rows, :]
            y_ref[b, rows, :] = y.astype(BF16)
            new_flat += [x for pair in state for x in pair]
        return tuple(new_flat)

    lax.fori_loop(0, n_chunks, body, tuple(x for pair in state for x in pair))


def _mlstm_call(qk, qt_c, vt_c, og, gates, gt_c, mqk, mqt, mvt, mgates, mgt, norm_g):
    bn, rows, _ = qk.shape
    nb = ML_BATCH if bn % ML_BATCH == 0 else 1

    def b_spec(a):
        return pl.BlockSpec((nb,) + a.shape[1:], lambda b: (b,) + (0,) * (a.ndim - 1))

    consts = (mqk, mqt, mvt, mgates, mgt, norm_g)
    return pl.pallas_call(
        _mlstm_kernel,
        grid=(bn // nb,),
        in_specs=[b_spec(qk), b_spec(qt_c), b_spec(vt_c), b_spec(og), b_spec(gates), b_spec(gt_c)]
        + [_resident(c.shape) for c in consts],
        out_specs=pl.BlockSpec((nb, rows, ML_WIDTH), lambda b: (b, 0, 0)),
        out_shape=jax.ShapeDtypeStruct((bn, rows, ML_WIDTH), BF16),
        scratch_shapes=[pltpu.VMEM((ML_HEADS, ML_EXT, ML_QK_DIM), F32), pltpu.VMEM((ML_HEADS, SUBLANES, LANES), F32)],
        compiler_params=pltpu.CompilerParams(
            dimension_semantics=("arbitrary",), vmem_limit_bytes=VMEM_LIMIT_BYTES),
        name="mlstm",
    )(qk, qt_c, vt_c, og, gates, gt_c, *consts)


def _out_ffn_ln_kernel(ya_ref, ym_ref, h_ref, wo_ref, g2_ref, b2_ref, wg_ref, wu_ref, wd_ref, g3_ref, b3_ref,
                       o_ref, wg_s, wu_s, wd_s, h2_ref, acc0_ref, *, alpha, n_stage):
    i = pl.program_id(0)

    def mixed():
        mix = _dot(ya_ref[...], wo_ref[:ATT_WIDTH, :]) + _dot(ym_ref[...], wo_ref[ATT_WIDTH:, :])
        return _layer_norm(alpha * h_ref[...] + mix, g2_ref[...], b2_ref[...])

    @pl.when(i == 0)
    def _():
        h2_ref[...] = mixed()
        acc0_ref[...] = jnp.zeros(acc0_ref.shape, F32)

    @pl.when(i < n_stage)
    def _():
        _stage_ffn_weights(i, wg_ref, wu_ref, wd_ref, wg_s, wu_s, wd_s)
        acc0_ref[...] += _swiglu_chunk(h2_ref[...].astype(BF16), wg_s[i], wu_s[i], wd_s[i])

    @pl.when(i == n_stage - 1)
    def _():
        o_ref[...] = _layer_norm(alpha * h2_ref[...] + 0.5 * acc0_ref[...], g3_ref[...], b3_ref[...])

    @pl.when(i >= n_stage)
    def _():
        o_ref[...] = _ffn_ln(mixed(), wg_s, wu_s, wd_s, g3_ref[...], b3_ref[...], alpha)


def _out_ffn_ln_call(ya, ym, h, wo, g2, b2, wg, wu, wd, g3, b3, alpha, tm):
    rows, d = h.shape
    n_stage, w_specs, w_scratch = _ffn_weight_specs(d, wg.shape[1])

    def row_spec(width):
        return pl.BlockSpec((tm, width), lambda i: (_staged_tile_index(i, n_stage), 0))

    return pl.pallas_call(
        functools.partial(_out_ffn_ln_kernel, alpha=alpha, n_stage=n_stage),
        grid=(n_stage - 1 + rows // tm,),
        in_specs=[row_spec(ya.shape[1]), row_spec(ym.shape[1]), row_spec(d),
                  _resident(wo.shape), _resident(g2.shape), _resident(b2.shape)] + w_specs
        + [_resident(g3.shape), _resident(b3.shape)],
        out_specs=row_spec(d),
        out_shape=jax.ShapeDtypeStruct((rows, d), F32),
        scratch_shapes=w_scratch + [pltpu.VMEM((tm, d), F32), pltpu.VMEM((tm, d), F32)],
        compiler_params=pltpu.CompilerParams(
            dimension_semantics=("arbitrary",), vmem_limit_bytes=VMEM_LIMIT_BYTES),
        name="out_ffn_ln",
    )(ya, ym, h, wo, g2, b2, wg, wu, wd, g3, b3)


def _block_diag(w):
    nh, a, b = w.shape
    eye = jnp.eye(nh, dtype=w.dtype)
    return (eye[:, None, :, None] * w[:, :, None, :]).reshape(nh * a, nh * b)


def _pad_rows(a, rows, value=0.0):
    return jnp.pad(a, ((0, rows - a.shape[0]), (0, 0)), constant_values=value)


def kernel(x, meta_tokens, ln1_g, ln1_b, ffn1_w_gate, ffn1_w_up, ffn1_w_down, w_in, w_uk, w_uv, kv_norm_g,
           conv_w, b_igate, b_fgate, ml_norm_g, w_out, ln2_g, ln2_b, ffn2_w_gate, ffn2_w_up, ffn2_w_down,
           ln3_g, ln3_b):
    depth = ln1_g.shape[0]
    assert depth == 1, "the meta-token shortcut below is only valid for a single layer"
    bsz, seq, d = x.shape
    assert seq % ROW_TILE == 0 and seq % ML_CHUNK == 0 and seq % Q_TILE == 0
    assert Q_TILE == GT_CHUNK and ML_CHUNK == GT_CHUNK and KEY_CHUNK == GT_CHUNK and ROW_TILE % GT_CHUNK == 0
    alpha = (2 * depth) ** 0.25
    topk = min(TOPK_MAX, seq // 4)

    row2 = lambda p: p[0].reshape(1, -1).astype(F32)
    bf = lambda w: w[0].astype(BF16)

    w_t = jnp.swapaxes(w_in[0], 0, 1)
    sizes = (ATT_WIDTH, KV_LATENT, IDX_WIDTH, IDX_DIM, IDX_HEADS, MLQK_WIDTH, ML_WIDTH, ML_WIDTH, ML_HEADS, ML_HEADS)
    offs = [sum(sizes[:n]) for n in range(len(sizes) + 1)]
    assert w_t.shape[0] == offs[-1]
    wa = jnp.concatenate([w_t[offs[0]:offs[4]], w_t[offs[4]:offs[5]], w_t[offs[8]:offs[10]],
                          jnp.zeros((LANES - GATE_END, d), F32)], axis=0).astype(BF16)
    wm = w_t[offs[5]:offs[8]].astype(BF16)
    gbias = jnp.concatenate([jnp.zeros((GATE_I0,), F32), b_igate[0], b_fgate[0],
                             jnp.zeros((LANES - GATE_END,), F32)]).reshape(1, LANES)
    wuk_bd = jnp.stack([_block_diag(w_uk[0][2 * p:2 * p + 2]) for p in range(HEAD_PAIRS)]).astype(BF16)
    wuv_bd = jnp.stack([_block_diag(w_uv[0][2 * p:2 * p + 2]) for p in range(HEAD_PAIRS)]).astype(BF16)
    kvg = row2(kv_norm_g)
    convw = conv_w[0].astype(F32)

    h1, h1_meta = _ffn_ln_call(x.reshape(bsz * seq, d), meta_tokens.astype(F32), ffn1_w_gate[0], ffn1_w_up[0],
                               ffn1_w_down[0], row2(ln1_g), row2(ln1_b), alpha, FFN_TILE)
    zero_tail = jnp.zeros((CONV_HIST, MLQK_WIDTH), F32)
    (_, m_ckv, _, m_kidx, m_qk, m_v, _, m_gates, m_tail) = _inproj_call(
        h1_meta[None], zero_tail, wa, wm, wuk_bd, kvg, convw, gbias, N_META)

    (qlat, ckv, qidx, kidx, qk, _, og, gates, _, gates_t, ckv_t, v_t, q_t) = _inproj_call(
        h1.reshape(bsz, seq, d), m_tail[0], wa, wm, wuk_bd, kvg, convw, gbias, ROW_TILE)

    nchunks = seq // KEY_CHUNK
    ckv_c = ckv.reshape(bsz, nchunks, KEY_CHUNK, KV_LATENT)
    kidx_c = kidx.reshape(bsz, nchunks, KEY_CHUNK, IDX_DIM)
    ones_rows = jnp.zeros((ATT_EXT - KV_LATENT, N_META), BF16).at[0].set(1.0)
    m_ckvt = jnp.concatenate([m_ckv[0].T, ones_rows], axis=0)
    y_att = _dsa_call(qlat, qidx, gates_t, ckv_c, ckv_t, kidx_c,
                      m_ckv[0], m_ckvt, m_kidx[0], wuv_bd, topk)

    lane = jnp.arange(LANES)
    pad_gate = jnp.where((lane >= GATE_I0) & (lane < GATE_F0), NEG_BIG, 0.0).astype(F32)
    mg = jnp.concatenate([m_gates[0], jnp.broadcast_to(pad_gate, (ML_CHUNK - N_META, LANES))], axis=0)
    gate_lanes = slice(GATE_W0, GATE_W0 + GATE_ROWS)
    m_qk_pad = _pad_rows(m_qk[0], ML_CHUNK)
    y_ml = _mlstm_call(qk, q_t, v_t, og, gates, gates_t,
                       m_qk_pad, m_qk_pad[:, :MLQK_WIDTH // 2].T, _pad_rows(m_v[0], ML_CHUNK).T, mg, mg[:, gate_lanes].T,
                       row2(ml_norm_g))

    out = _out_ffn_ln_call(
        y_att.reshape(bsz * seq, ATT_WIDTH), y_ml.reshape(bsz * seq, ML_WIDTH), h1, bf(w_out),
        row2(ln2_g), row2(ln2_b), ffn2_w_gate[0], ffn2_w_up[0], ffn2_w_down[0], row2(ln3_g), row2(ln3_b),
        alpha, FFN_TILE)
    return out.reshape(bsz, seq, d)
```

```python
import functools

import jax
import jax.numpy as jnp
from jax import lax
from jax.experimental import pallas as pl
from jax.experimental.pallas import tpu as pltpu

F32 = jnp.float32
BF16 = jnp.bfloat16

N_META = 16
ATT_HEADS = 8
ATT_HEAD_DIM = 64
KV_LATENT = 128
IDX_HEADS = 4
IDX_DIM = 64
TOPK_MAX = 256
ML_HEADS = 4
ML_V_DIM = 128
ML_QK_DIM = 64
CONV_WIDTH = 4
GATE_SOFTCAP = 15.0
M_INIT = -1e30
LN_EPS = 1e-5
NEG_BIG = -1e30
LOG2_E = 1.4426950408889634

LANES = 128
SUBLANES = 8
VMEM_BYTES_V7X = 64 * 1024 * 1024
VMEM_LIMIT_BYTES = VMEM_BYTES_V7X * 7 // 8

FF_CHUNK = 256
ROW_TILE = 512
FFN_TILE = 512
Q_TILE = 256
ATT_GROUP = 128
KEY_CHUNK = 256
N_BISECT = 16
REDUCE_ROWS = 32
ML_BATCH = 2
ML_CHUNK = 256


def _dot(a, b):
    return jnp.dot(a, b, preferred_element_type=F32)


def _layer_norm(z, g, b):
    mu = jnp.mean(z, axis=-1, keepdims=True)
    zc = z - mu
    var = jnp.mean(zc * zc, axis=-1, keepdims=True)
    return zc * lax.rsqrt(var + LN_EPS) * g + b


def _sigmoid(x):
    return 1.0 / (1.0 + jnp.exp(-x))


def _swiglu_chunk(xb, wg_c, wu_c, wd_c):
    g = _dot(xb, wg_c)
    u = _dot(xb, wu_c)
    return _dot((g * _sigmoid(g) * u).astype(BF16), wd_c)


def _ffn_ln(x, wg_s, wu_s, wd_s, g, b, alpha):
    xb = x.astype(BF16)
    acc = jnp.zeros(x.shape, F32)
    for c in range(wg_s.shape[0]):
        acc = acc + _swiglu_chunk(xb, wg_s[c], wu_s[c], wd_s[c])
    return _layer_norm(alpha * x + 0.5 * acc, g, b)


def _stage_ffn_weights(step, wg_ref, wu_ref, wd_ref, wg_s, wu_s, wd_s):
    wg_s[step] = wg_ref[...].astype(BF16)
    wu_s[step] = wu_ref[...].astype(BF16)
    wd_s[step] = wd_ref[...].astype(BF16)


def _ffn_weight_specs(d, d_ff):
    n = d_ff // FF_CHUNK
    col = pl.BlockSpec((d, FF_CHUNK), lambda i: (0, jnp.minimum(i, n - 1)))
    row = pl.BlockSpec((FF_CHUNK, d), lambda i: (jnp.minimum(i, n - 1), 0))
    scratch = [pltpu.VMEM((n, d, FF_CHUNK), BF16), pltpu.VMEM((n, d, FF_CHUNK), BF16),
               pltpu.VMEM((n, FF_CHUNK, d), BF16)]
    return n, [col, col, row], scratch


def _staged_tile_index(i, n_stage):
    return jnp.maximum(i - (n_stage - 1), 0)


def _ffn_ln_kernel(x_ref, meta_ref, wg_ref, wu_ref, wd_ref, g_ref, b_ref, o_ref, ometa_ref,
                   wg_s, wu_s, wd_s, macc_ref, acc0_ref, *, alpha, n_stage):
    i = pl.program_id(0)

    @pl.when(i == 0)
    def _():
        macc_ref[...] = jnp.zeros(macc_ref.shape, F32)
        acc0_ref[...] = jnp.zeros(acc0_ref.shape, F32)

    @pl.when(i < n_stage)
    def _():
        _stage_ffn_weights(i, wg_ref, wu_ref, wd_ref, wg_s, wu_s, wd_s)
        macc_ref[...] += _swiglu_chunk(meta_ref[...].astype(BF16), wg_s[i], wu_s[i], wd_s[i])
        acc0_ref[...] += _swiglu_chunk(x_ref[...].astype(BF16), wg_s[i], wu_s[i], wd_s[i])

    @pl.when(i == n_stage - 1)
    def _():
        ometa_ref[...] = _layer_norm(alpha * meta_ref[...] + 0.5 * macc_ref[...], g_ref[...], b_ref[...])
        o_ref[...] = _layer_norm(alpha * x_ref[...] + 0.5 * acc0_ref[...], g_ref[...], b_ref[...])

    @pl.when(i >= n_stage)
    def _():
        o_ref[...] = _ffn_ln(x_ref[...], wg_s, wu_s, wd_s, g_ref[...], b_ref[...], alpha)


def _resident(shape):
    return pl.BlockSpec(shape, lambda *_: (0,) * len(shape), pipeline_mode=pl.Buffered(1))


def _ffn_ln_call(x, meta, wg, wu, wd, g, b, alpha, tm):
    rows, d = x.shape
    n_stage, w_specs, w_scratch = _ffn_weight_specs(d, wg.shape[1])
    row_spec = pl.BlockSpec((tm, d), lambda i: (_staged_tile_index(i, n_stage), 0))
    return pl.pallas_call(
        functools.partial(_ffn_ln_kernel, alpha=alpha, n_stage=n_stage),
        grid=(n_stage - 1 + rows // tm,),
        in_specs=[row_spec, _resident(meta.shape)] + w_specs + [_resident(g.shape), _resident(b.shape)],
        out_specs=[row_spec, pl.BlockSpec(meta.shape, lambda i: (0, 0))],
        out_shape=[jax.ShapeDtypeStruct((rows, d), F32), jax.ShapeDtypeStruct(meta.shape, F32)],
        scratch_shapes=w_scratch + [pltpu.VMEM(meta.shape, F32), pltpu.VMEM((tm, d), F32)],
        compiler_params=pltpu.CompilerParams(
            dimension_semantics=("arbitrary",), vmem_limit_bytes=VMEM_LIMIT_BYTES),
        name="ffn_ln",
    )(x, meta, wg, wu, wd, g, b)


ATT_WIDTH = ATT_HEADS * ATT_HEAD_DIM
IDX_WIDTH = IDX_HEADS * IDX_DIM
MLQK_WIDTH = 2 * ML_HEADS * ML_QK_DIM
ML_WIDTH = ML_HEADS * ML_V_DIM
LAT_WIDTH = ATT_HEADS * KV_LATENT
CONV_HIST = SUBLANES
GATE_W0 = IDX_DIM
GATE_I0, GATE_F0, GATE_END = GATE_W0 + IDX_HEADS, GATE_W0 + IDX_HEADS + ML_HEADS, GATE_W0 + IDX_HEADS + 2 * ML_HEADS
GATE_ROWS = 2 * SUBLANES
GT_CHUNK = 256
ATT_EXT = KV_LATENT + 2 * SUBLANES
HEAD_PAIRS = ATT_HEADS // 2


def _inproj_kernel(h_ref, tail_ref, wa_ref, wm_ref, wuk_ref, kvg_ref, convw_ref, gbias_ref,
                   qlat_ref, ckv_ref, qidx_ref, kidx_ref, qk_ref, v_ref, og_ref, gates_ref, tailout_ref, *rest):
    carry_ref, wa_s, wm_s = rest[-3:]
    tm = h_ref.shape[1]

    @pl.when((pl.program_id(0) == 0) & (pl.program_id(1) == 0))
    def _():
        wa_s[...] = wa_ref[...].astype(F32).T.astype(BF16)
        wm_s[...] = wm_ref[...].astype(F32).T.astype(BF16)

    @pl.when(pl.program_id(1) == 0)
    def _():
        carry_ref[...] = tail_ref[...]

    xb = h_ref[0].astype(BF16)

    pa = _dot(xb, wa_s[...])
    q_a = pa[:, :ATT_WIDTH].astype(BF16)
    c0 = ATT_WIDTH
    ckv = pa[:, c0:c0 + KV_LATENT]
    c1 = c0 + KV_LATENT
    ckv = ckv * lax.rsqrt(jnp.mean(ckv * ckv, axis=-1, keepdims=True) + LN_EPS) * kvg_ref[...]
    ckv_ref[0] = ckv.astype(BF16)
    qidx_ref[0] = pa[:, c1:c1 + IDX_WIDTH].astype(BF16)
    c2 = c1 + IDX_WIDTH
    kidx_ref[0] = pa[:, c2:c2 + IDX_DIM].astype(BF16)
    pair_in, pair_out = 2 * ATT_HEAD_DIM, 2 * KV_LATENT
    q_lat = []
    for p in range(HEAD_PAIRS):
        ql = _dot(q_a[:, p * pair_in:(p + 1) * pair_in], wuk_ref[p]) * (ATT_HEAD_DIM ** -0.5 * LOG2_E)
        qlat_ref[0, :, p * pair_out:(p + 1) * pair_out] = ql.astype(BF16)
        q_lat.append(ql)

    pm = _dot(xb, wm_s[...])
    qk_raw = pm[:, :MLQK_WIDTH]
    v_ref[0] = pm[:, MLQK_WIDTH:MLQK_WIDTH + ML_WIDTH].astype(BF16)
    og_ref[0] = _sigmoid(pm[:, MLQK_WIDTH + ML_WIDTH:])

    ext = jnp.concatenate([carry_ref[...], qk_raw], axis=0)
    cw = convw_ref[...]
    conv = jnp.zeros_like(qk_raw)
    for j in range(CONV_WIDTH):
        s0 = CONV_HIST - (CONV_WIDTH - 1) + j
        conv = conv + ext[s0:s0 + tm] * cw[j:j + 1]
    act = conv * _sigmoid(conv)
    half = MLQK_WIDTH // 2
    qk_ref[0, :, :half] = act[:, :half].astype(BF16)
    qk_ref[0, :, half:] = (act[:, half:] * (ML_QK_DIM ** -0.5)).astype(BF16)
    carry_ref[...] = qk_raw[tm - CONV_HIST:]
    tailout_ref[0] = qk_raw[tm - CONV_HIST:]

    gr = pa[:, c2 + IDX_DIM - GATE_W0:]
    lane = lax.broadcasted_iota(jnp.int32, gr.shape, 1)
    sc = GATE_SOFTCAP * jnp.tanh((gr + gbias_ref[...]) / GATE_SOFTCAP)
    lf = -(jnp.maximum(-sc, 0.0) + jnp.log1p(jnp.exp(-jnp.abs(sc))))
    w_scaled = gr * (IDX_HEADS ** -0.5 * IDX_DIM ** -0.5)
    gates = jnp.where((lane < GATE_W0) | (lane >= GATE_END), 0.0,
                      jnp.where(lane < GATE_I0, w_scaled, jnp.where(lane < GATE_F0, sc, lf)))
    gates_ref[0] = gates
    if len(rest) == 8:
        gt_ref, ckvt_ref, vt_ref, qt_ref, qlatt_ref = rest[:5]
        q_lat_t = jnp.concatenate([ql.T for ql in q_lat], axis=0).astype(BF16)
        gates_t = gates.T[GATE_W0:GATE_W0 + GATE_ROWS]
        ckv_t = ckv.T.astype(BF16)
        v_t = pm[:, MLQK_WIDTH:MLQK_WIDTH + ML_WIDTH].T.astype(BF16)
        q_t = act[:, :half].T.astype(BF16)
        ones_blk = jnp.where(lax.broadcasted_iota(jnp.int32, (ATT_EXT - KV_LATENT, GT_CHUNK), 0) == 0,
                             1.0, 0.0).astype(BF16)
        for j in range(tm // GT_CHUNK):
            piece = slice(j * GT_CHUNK, (j + 1) * GT_CHUNK)
            gt_ref[0, j] = gates_t[:, piece]
            ckvt_ref[0, j] = jnp.concatenate([ckv_t[:, piece], ones_blk], axis=0)
            vt_ref[0, j] = v_t[:, piece]
            qt_ref[0, j] = q_t[:, piece]
            qlatt_ref[0, j] = q_lat_t[:, piece]


def _inproj_call(h, tail, wa, wm, wuk_bd, kvg, convw, gbias, tm):
    bn, rows, d = h.shape
    nblk = rows // tm
    emit_gt = tm % GT_CHUNK == 0

    def row_spec(width):
        return pl.BlockSpec((1, tm, width), lambda b, j: (b, j, 0))

    outs = [
        (LAT_WIDTH, BF16), (KV_LATENT, BF16), (IDX_WIDTH, BF16), (IDX_DIM, BF16),
        (MLQK_WIDTH, BF16), (ML_WIDTH, BF16), (ML_WIDTH, F32), (LANES, F32),
    ]
    out_shape = [jax.ShapeDtypeStruct((bn, rows, w), dt) for w, dt in outs]
    out_specs = [row_spec(w) for w, _ in outs]
    out_shape.append(jax.ShapeDtypeStruct((bn, CONV_HIST, MLQK_WIDTH), F32))
    out_specs.append(pl.BlockSpec((1, CONV_HIST, MLQK_WIDTH), lambda b, j: (b, 0, 0)))
    if emit_gt:
        per_tile = tm // GT_CHUNK
        for height, dt in ((GATE_ROWS, F32), (ATT_EXT, BF16), (ML_WIDTH, BF16), (MLQK_WIDTH // 2, BF16),
                           (LAT_WIDTH, BF16)):
            out_shape.append(jax.ShapeDtypeStruct((bn, rows // GT_CHUNK, height, GT_CHUNK), dt))
            out_specs.append(pl.BlockSpec((1, per_tile, height, GT_CHUNK), lambda b, j: (b, j, 0, 0)))
    return pl.pallas_call(
        _inproj_kernel,
        grid=(bn, nblk),
        in_specs=[
            row_spec(d),
            _resident(tail.shape), _resident(wa.shape), _resident(wm.shape),
            _resident(wuk_bd.shape), _resident(kvg.shape), _resident(convw.shape), _resident(gbias.shape),
        ],
        out_specs=out_specs,
        out_shape=out_shape,
        scratch_shapes=[pltpu.VMEM((CONV_HIST, MLQK_WIDTH), F32),
                        pltpu.VMEM(wa.shape[::-1], BF16), pltpu.VMEM(wm.shape[::-1], BF16)],
        compiler_params=pltpu.CompilerParams(
            dimension_semantics=("arbitrary", "arbitrary"), vmem_limit_bytes=VMEM_LIMIT_BYTES),
        name="in_proj",
    )(h, tail, wa, wm, wuk_bd, kvg, convw, gbias)


def _dsa_kernel(qlat_ref, qidx_ref, wrow_ref, ckv_ref, ckvt_ref, kidx_ref, mckv_ref, mckvt_ref, mkidx_ref,
                wuv_ref, y_ref, s_ref, acc_ref, p_ref, *, topk):
    _, kc, tq = s_ref.shape
    i = pl.program_id(1)
    nch = ((i + 1) * tq + kc - 1) // kc
    qreal = i * tq + lax.broadcasted_iota(jnp.int32, (1, tq), 1)
    kf = float(topk)

    wrow = wrow_ref[0, 0]
    qidx = qidx_ref[0]
    q_idx_t = qidx.astype(F32).T
    q_idx_all_t = jnp.concatenate([q_idx_t[h * IDX_DIM:(h + 1) * IDX_DIM] for h in range(IDX_HEADS)],
                                  axis=1).astype(BF16)
    wi = [wrow[h:h + 1, :] for h in range(IDX_HEADS)]

    def scores(k_rows):
        lg = _dot(k_rows, q_idx_all_t)
        sc = jnp.zeros((k_rows.shape[0], tq), F32)
        for h in range(IDX_HEADS):
            sc = sc + jnp.maximum(lg[:, h * tq:(h + 1) * tq], 0.0) * wi[h]
        return sc

    s_meta = scores(mkidx_ref[...])

    def score_chunk(c, lo, hi):
        sc = scores(kidx_ref[0, c])
        valid = c * kc + lax.broadcasted_iota(jnp.int32, (kc, tq), 0) <= qreal
        s_ref[c] = jnp.where(valid, sc, -jnp.inf)
        groups = (kc // REDUCE_ROWS, REDUCE_ROWS, tq)
        lo = jnp.minimum(lo, jnp.min(sc.reshape(groups), axis=0))
        hi = jnp.maximum(hi, jnp.max(sc.reshape(groups), axis=0))
        return lo, hi

    def score_pair(c2, carry):
        lo, hi = score_chunk(2 * c2, *carry)
        return score_chunk(jnp.minimum(2 * c2 + 1, nch - 1), lo, hi)

    lo, hi = lax.fori_loop(0, (nch + 1) // 2, score_pair,
                           (jnp.full((REDUCE_ROWS, tq), jnp.inf, F32), jnp.full((REDUCE_ROWS, tq), -jnp.inf, F32)))
    lo = jnp.minimum(jnp.min(lo, axis=0, keepdims=True), jnp.min(s_meta, axis=0, keepdims=True))
    hi = jnp.maximum(jnp.max(hi, axis=0, keepdims=True), jnp.max(s_meta, axis=0, keepdims=True))

    def key_reduce(reduce, combine, per_chunk, init):
        def body(c, acc):
            x = per_chunk(s_ref[c]).reshape(kc // REDUCE_ROWS, REDUCE_ROWS, tq)
            return combine(acc, reduce(x, axis=0))
        acc = lax.fori_loop(0, nch, body, jnp.full((REDUCE_ROWS, tq), init, F32))
        return combine(reduce(acc, axis=0, keepdims=True), reduce(per_chunk(s_meta), axis=0, keepdims=True))

    def count(pred):
        return key_reduce(jnp.sum, jnp.add, lambda sc: jnp.where(pred(sc), 1.0, 0.0), 0.0)

    def max_where(pred):
        return key_reduce(jnp.max, jnp.maximum, lambda sc: jnp.where(pred(sc), sc, -jnp.inf), -jnp.inf)

    def bisect(_, carry):
        lo, hi = carry
        mid = 0.5 * lo + 0.5 * hi
        up = count(lambda sc: sc > mid) >= kf
        return jnp.where(up, mid, lo), jnp.where(up, hi, mid)

    lo, hi = lax.fori_loop(0, N_BISECT, bisect, (lo, hi))

    n_valid = (qreal + (N_META + 1)).astype(F32)
    small = n_valid <= kf
    cand = max_where(lambda sc: sc <= hi)
    n_ge = count(lambda sc: sc >= cand)
    done = jnp.where(small | (n_ge >= kf), 1.0, 0.0)

    def not_finished(state):
        return jnp.min(state[1]) < 0.5

    def step_down(state):
        cand, done, _ = state
        nxt = jnp.where(done > 0.5, cand, max_where(lambda sc: sc < cand))
        n_ge = count(lambda sc: sc >= nxt)
        return nxt, jnp.where(n_ge >= kf, 1.0, done), n_ge

    cand, _, n_ge = lax.while_loop(not_finished, step_down, step_down((cand, done, n_ge)))
    thr = jnp.where(small, -jnp.inf, cand)
    n_eq = count(lambda sc: sc == thr)
    need = jnp.where(small, 0.0, kf - (n_ge - n_eq))
    ranked_ties = jnp.max(jnp.where(n_eq > need, 1.0, 0.0)) > 0.5

    qlat_t = qlat_ref[0, 0]
    n_groups = tq // ATT_GROUP
    onehot = (lax.broadcasted_iota(jnp.int32, (ATT_GROUP, ATT_GROUP), 0)
              == lax.broadcasted_iota(jnp.int32, (ATT_GROUP, ATT_GROUP), 1)).astype(BF16)
    q_aug_t = [jnp.concatenate(
        [jnp.concatenate([qlat_t[h * KV_LATENT:(h + 1) * KV_LATENT, g * ATT_GROUP:(g + 1) * ATT_GROUP]
                          for h in range(ATT_HEADS)], axis=1),
         jnp.concatenate([onehot] * ATT_HEADS, axis=1)], axis=0)
        for g in range(n_groups)]
    hq = ATT_HEADS * ATT_GROUP

    def lower_tri(n):
        return (lax.broadcasted_iota(jnp.int32, (n, n), 1) <= lax.broadcasted_iota(jnp.int32, (n, n), 0)).astype(BF16)

    def attention(ranked):
        def mask_bias(sc, eq_seen):
            if not ranked:
                return jnp.where(sc >= thr, 0.0, NEG_BIG).astype(BF16), eq_seen
            n = sc.shape[0]
            eq = sc == thr
            rank = _dot(lower_tri(n), jnp.where(eq, 1.0, 0.0).astype(BF16)) + eq_seen
            keep = (sc > thr) | (eq & (rank <= need))
            return jnp.where(keep, 0.0, NEG_BIG).astype(BF16), rank[n - 1:n, :]

        def logits(g, kv, bias):
            k_aug = jnp.concatenate([kv, bias[:, g * ATT_GROUP:(g + 1) * ATT_GROUP]], axis=1)
            return _dot(k_aug, q_aug_t[g])

        def fold_in(g, c_prev, a_prev):
            acc_ref[g] = a_prev * acc_ref[g] + _dot(ckvt_ref[0, c_prev], p_ref[g])

        def attend(c, carry):
            bias, eq_seen = mask_bias(s_ref[c], carry[0])
            out = [eq_seen]
            for g in range(n_groups):
                m, a_prev = carry[1 + 2 * g:3 + 2 * g]
                fold_in(g, c - 1, a_prev)
                s = logits(g, ckv_ref[0, c], bias)
                m_new = jnp.maximum(m, jnp.max(s, axis=0, keepdims=True))
                p_ref[g] = jnp.exp2(s - m_new).astype(BF16)
                out += [m_new, jnp.exp2(m - m_new)]
            return tuple(out)

        bias_m, eq_seen = mask_bias(s_meta, jnp.zeros((1, tq), F32))
        bias_0, eq_seen = mask_bias(s_ref[0], eq_seen)
        kv_first = jnp.concatenate([mckv_ref[...], ckv_ref[0, 0]], axis=0)
        bias_first = jnp.concatenate([bias_m, bias_0], axis=0)
        init = [eq_seen]
        for g in range(n_groups):
            s = logits(g, kv_first, bias_first)
            m = jnp.max(s, axis=0, keepdims=True)
            p = jnp.exp2(s - m).astype(BF16)
            acc_ref[g] = _dot(mckvt_ref[...], p[:N_META])
            p_ref[g] = p[N_META:]
            init += [m, jnp.ones((1, hq), F32)]
        carry = lax.fori_loop(1, nch, attend, tuple(init))
        rows = []
        for g in range(n_groups):
            fold_in(g, nch - 1, carry[2 + 2 * g])
            acc = acc_ref[g]
            o_t = (acc[:KV_LATENT] / acc[KV_LATENT:KV_LATENT + 1]).T
            rows.append(jnp.concatenate([o_t[h * ATT_GROUP:(h + 1) * ATT_GROUP] for h in range(ATT_HEADS)], axis=1))
        o_all = jnp.concatenate(rows, axis=0).astype(BF16)
        pair = 2 * KV_LATENT
        return jnp.concatenate([_dot(o_all[:, p * pair:(p + 1) * pair], wuv_ref[p]) for p in range(HEAD_PAIRS)],
                               axis=1).astype(BF16)

    y_ref[0] = lax.cond(ranked_ties, lambda: attention(True), lambda: attention(False))


def _dsa_call(qlat_t, qidx, wrow, ckv_c, ckvt_c, kidx_c, m_ckv, m_ckvt, m_kidx, wuv_bd, topk):
    bn, rows, _ = qidx.shape
    nchunks, kc = ckv_c.shape[1], ckv_c.shape[2]
    nq = rows // Q_TILE

    def q_spec(width):
        return pl.BlockSpec((1, Q_TILE, width), lambda b, i: (b, i, 0))

    def k_spec(a):
        return pl.BlockSpec((1,) + a.shape[1:], lambda b, i: (b, 0, 0, 0))

    return pl.pallas_call(
        functools.partial(_dsa_kernel, topk=topk),
        grid=(bn, nq),
        in_specs=[pl.BlockSpec((1, 1, LAT_WIDTH, Q_TILE), lambda b, i: (b, i, 0, 0)), q_spec(IDX_WIDTH),
                  pl.BlockSpec((1, 1, GATE_ROWS, Q_TILE), lambda b, i: (b, i, 0, 0)),
                  k_spec(ckv_c), k_spec(ckvt_c), k_spec(kidx_c),
                  _resident(m_ckv.shape), _resident(m_ckvt.shape), _resident(m_kidx.shape), _resident(wuv_bd.shape)],
        out_specs=q_spec(ATT_WIDTH),
        out_shape=jax.ShapeDtypeStruct((bn, rows, ATT_WIDTH), BF16),
        scratch_shapes=[pltpu.VMEM((nchunks, kc, Q_TILE), F32),
                        pltpu.VMEM((Q_TILE // ATT_GROUP, ckvt_c.shape[2], ATT_HEADS * ATT_GROUP), F32),
                        pltpu.VMEM((Q_TILE // ATT_GROUP, kc, ATT_HEADS * ATT_GROUP), BF16)],
        compiler_params=pltpu.CompilerParams(
            dimension_semantics=("arbitrary", "arbitrary"), vmem_limit_bytes=VMEM_LIMIT_BYTES),
        name="dsa",
    )(qlat_t, qidx, wrow, ckv_c, ckvt_c, kidx_c, m_ckv, m_ckvt, m_kidx, wuv_bd)


def _split3(x):
    hi = x.astype(BF16)
    r = x - hi.astype(F32)
    mid = r.astype(BF16)
    lo = (r - mid.astype(F32)).astype(BF16)
    return hi, mid, lo


ML_EXT = ML_V_DIM + 16


def _mlstm_chunk(qk, qt, vt, g, gt, state):
    L = qk.shape[0]
    s_idx = lax.broadcasted_iota(jnp.int32, (L, L), 0)
    t_idx = lax.broadcasted_iota(jnp.int32, (L, L), 1)
    causal = s_idx <= t_idx
    w = g.shape[1]
    cols3 = _dot((t_idx <= s_idx).astype(BF16), jnp.concatenate(_split3(g), axis=1))
    b_cols = cols3[:, :w] + cols3[:, w:2 * w] + cols3[:, 2 * w:]
    r = gt.shape[0]
    rows3 = _dot(jnp.concatenate(_split3(gt), axis=0), causal.astype(BF16))
    b_rows = rows3[:r] + rows3[r:2 * r] + rows3[2 * r:]
    ones_blk = jnp.where(lax.broadcasted_iota(jnp.int32, (ML_EXT - ML_V_DIM, L), 0) == 0, 1.0, 0.0).astype(BF16)
    kq = ML_HEADS * ML_QK_DIM

    outs, new_state = [], []
    for h in range(ML_HEADS):
        ce, m_prev = state[h]
        c_col = g[:, GATE_I0 + h:GATE_I0 + h + 1] - b_cols[:, GATE_F0 + h:GATE_F0 + h + 1]
        b_row = b_rows[GATE_F0 - GATE_W0 + h:GATE_F0 - GATE_W0 + h + 1, :]
        ig_row = gt[GATE_I0 - GATE_W0 + h:GATE_I0 - GATE_W0 + h + 1, :]
        qt_h = qt[h * ML_QK_DIM:(h + 1) * ML_QK_DIM, :]
        kh = qk[:, kq + h * ML_QK_DIM:kq + (h + 1) * ML_QK_DIM]
        vt_ext = jnp.concatenate([vt[h * ML_V_DIM:(h + 1) * ML_V_DIM, :], ones_blk], axis=0)

        d_t = jnp.where(causal, c_col + b_row, -jnp.inf)
        inter = b_row + m_prev
        m_t = jnp.maximum(jnp.max(d_t, axis=0, keepdims=True), inter)
        w_inter = jnp.exp(inter - m_t)
        s_t = _dot(kh, qt_h) * jnp.exp(d_t - m_t)
        r = _dot(vt_ext, s_t.astype(BF16)) + _dot(ce.astype(BF16), qt_h) * w_inter
        num = r[:ML_V_DIM]
        den = r[ML_V_DIM:ML_V_DIM + 1]
        hh = num / jnp.maximum(jnp.abs(den), jnp.exp(-m_t))
        mu = jnp.mean(hh, axis=0, keepdims=True)
        hc = hh - mu
        var = jnp.mean(hc * hc, axis=0, keepdims=True)
        outs.append((hc * lax.rsqrt(var + LN_EPS)).T)

        b_end = b_row[:, L - 1:L]
        g_row = b_end - b_row + ig_row
        m_new = jnp.maximum(b_end + m_prev, jnp.max(g_row, axis=1, keepdims=True))
        decay = jnp.exp(b_end + m_prev - m_new)
        weighted = (vt_ext.astype(F32) * jnp.exp(g_row - m_new)).astype(BF16)
        new_state.append((decay * ce + _dot(weighted, kh), m_new))
    return outs, new_state


def _mlstm_kernel(qk_ref, qt_ref, vt_ref, og_ref, gates_ref, gt_ref, mqk_ref, mqt_ref, mvt_ref, mgates_ref, mgt_ref,
                  ng_ref, y_ref,
                  ce0_ref, m0_ref):
    L = mqk_ref.shape[0]
    n_chunks = qk_ref.shape[1] // L
    norm_g = ng_ref[...]

    @pl.when(pl.program_id(0) == 0)
    def _():
        state = [(jnp.zeros((ML_EXT, ML_QK_DIM), F32), jnp.full((1, 1), M_INIT, F32)) for _ in range(ML_HEADS)]
        _, state = _mlstm_chunk(mqk_ref[...], mqt_ref[...], mvt_ref[...], mgates_ref[...], mgt_ref[...], state)
        for h in range(ML_HEADS):
            ce0_ref[h] = state[h][0]
            m0_ref[h] = jnp.broadcast_to(state[h][1], m0_ref.shape[1:])

    n_b = qk_ref.shape[0]
    state = [(ce0_ref[h], m0_ref[h][0:1, 0:1]) for h in range(ML_HEADS)] * n_b

    def body(c, flat):
        rows = pl.ds(pl.multiple_of(c * L, L), L)
        new_flat = []
        for b in range(n_b):
            state = [(flat[2 * (b * ML_HEADS + h)], flat[2 * (b * ML_HEADS + h) + 1]) for h in range(ML_HEADS)]
            outs, state = _mlstm_chunk(qk_ref[b, rows, :], qt_ref[b, c], vt_ref[b, c], gates_ref[b, rows, :],
                                       gt_ref[b, c], state)
            y = jnp.concatenate(outs, axis=1) * norm_g * og_ref[b, rows, :]
            y_ref[b, rows, :] = y.astype(BF16)
            new_flat += [x for pair in state for x in pair]
        return tuple(new_flat)

    lax.fori_loop(0, n_chunks, body, tuple(x for pair in state for x in pair))


def _mlstm_call(qk, qt_c, vt_c, og, gates, gt_c, mqk, mqt, mvt, mgates, mgt, norm_g):
    bn, rows, _ = qk.shape
    nb = ML_BATCH if bn % ML_BATCH == 0 else 1

    def b_spec(a):
        return pl.BlockSpec((nb,) + a.shape[1:], lambda b: (b,) + (0,) * (a.ndim - 1))

    consts = (mqk, mqt, mvt, mgates, mgt, norm_g)
    return pl.pallas_call(
        _mlstm_kernel,
        grid=(bn // nb,),
        in_specs=[b_spec(qk), b_spec(qt_c), b_spec(vt_c), b_spec(og), b_spec(gates), b_spec(gt_c)]
        + [_resident(c.shape) for c in consts],
        out_specs=pl.BlockSpec((nb, rows, ML_WIDTH), lambda b: (b, 0, 0)),
        out_shape=jax.ShapeDtypeStruct((bn, rows, ML_WIDTH), BF16),
        scratch_shapes=[pltpu.VMEM((ML_HEADS, ML_EXT, ML_QK_DIM), F32), pltpu.VMEM((ML_HEADS, SUBLANES, LANES), F32)],
        compiler_params=pltpu.CompilerParams(
            dimension_semantics=("arbitrary",), vmem_limit_bytes=VMEM_LIMIT_BYTES),
        name="mlstm",
    )(qk, qt_c, vt_c, og, gates, gt_c, *consts)


def _out_ffn_ln_kernel(ya_ref, ym_ref, h_ref, wo_ref, g2_ref, b2_ref, wg_ref, wu_ref, wd_ref, g3_ref, b3_ref,
                       o_ref, wg_s, wu_s, wd_s, h2_ref, acc0_ref, *, alpha, n_stage):
    i = pl.program_id(0)

    def mixed():
        mix = _dot(ya_ref[...], wo_ref[:ATT_WIDTH, :]) + _dot(ym_ref[...], wo_ref[ATT_WIDTH:, :])
        return _layer_norm(alpha * h_ref[...] + mix, g2_ref[...], b2_ref[...])

    @pl.when(i == 0)
    def _():
        h2_ref[...] = mixed()
        acc0_ref[...] = jnp.zeros(acc0_ref.shape, F32)

    @pl.when(i < n_stage)
    def _():
        _stage_ffn_weights(i, wg_ref, wu_ref, wd_ref, wg_s, wu_s, wd_s)
        acc0_ref[...] += _swiglu_chunk(h2_ref[...].astype(BF16), wg_s[i], wu_s[i], wd_s[i])

    @pl.when(i == n_stage - 1)
    def _():
        o_ref[...] = _layer_norm(alpha * h2_ref[...] + 0.5 * acc0_ref[...], g3_ref[...], b3_ref[...])

    @pl.when(i >= n_stage)
    def _():
        o_ref[...] = _ffn_ln(mixed(), wg_s, wu_s, wd_s, g3_ref[...], b3_ref[...], alpha)


def _out_ffn_ln_call(ya, ym, h, wo, g2, b2, wg, wu, wd, g3, b3, alpha, tm):
    rows, d = h.shape
    n_stage, w_specs, w_scratch = _ffn_weight_specs(d, wg.shape[1])

    def row_spec(width):
        return pl.BlockSpec((tm, width), lambda i: (_staged_tile_index(i, n_stage), 0))

    return pl.pallas_call(
        functools.partial(_out_ffn_ln_kernel, alpha=alpha, n_stage=n_stage),
        grid=(n_stage - 1 + rows // tm,),
        in_specs=[row_spec(ya.shape[1]), row_spec(ym.shape[1]), row_spec(d),
                  _resident(wo.shape), _resident(g2.shape), _resident(b2.shape)] + w_specs
        + [_resident(g3.shape), _resident(b3.shape)],
        out_specs=row_spec(d),
        out_shape=jax.ShapeDtypeStruct((rows, d), F32),
        scratch_shapes=w_scratch + [pltpu.VMEM((tm, d), F32), pltpu.VMEM((tm, d), F32)],
        compiler_params=pltpu.CompilerParams(
            dimension_semantics=("arbitrary",), vmem_limit_bytes=VMEM_LIMIT_BYTES),
        name="out_ffn_ln",
    )(ya, ym, h, wo, g2, b2, wg, wu, wd, g3, b3)


def _block_diag(w):
    nh, a, b = w.shape
    eye = jnp.eye(nh, dtype=w.dtype)
    return (eye[:, None, :, None] * w[:, :, None, :]).reshape(nh * a, nh * b)


def _pad_rows(a, rows, value=0.0):
    return jnp.pad(a, ((0, rows - a.shape[0]), (0, 0)), constant_values=value)


def kernel(x, meta_tokens, ln1_g, ln1_b, ffn1_w_gate, ffn1_w_up, ffn1_w_down, w_in, w_uk, w_uv, kv_norm_g,
           conv_w, b_igate, b_fgate, ml_norm_g, w_out, ln2_g, ln2_b, ffn2_w_gate, ffn2_w_up, ffn2_w_down,
           ln3_g, ln3_b):
    depth = ln1_g.shape[0]
    assert depth == 1, "the meta-token shortcut below is only valid for a single layer"
    bsz, seq, d = x.shape
    assert seq % ROW_TILE == 0 and seq % ML_CHUNK == 0 and seq % Q_TILE == 0
    assert Q_TILE == GT_CHUNK and ML_CHUNK == GT_CHUNK and KEY_CHUNK == GT_CHUNK and ROW_TILE % GT_CHUNK == 0
    alpha = (2 * depth) ** 0.25
    topk = min(TOPK_MAX, seq // 4)

    row2 = lambda p: p[0].reshape(1, -1).astype(F32)
    bf = lambda w: w[0].astype(BF16)

    w_t = jnp.swapaxes(w_in[0], 0, 1)
    sizes = (ATT_WIDTH, KV_LATENT, IDX_WIDTH, IDX_DIM, IDX_HEADS, MLQK_WIDTH, ML_WIDTH, ML_WIDTH, ML_HEADS, ML_HEADS)
    offs = [sum(sizes[:n]) for n in range(len(sizes) + 1)]
    assert w_t.shape[0] == offs[-1]
    wa = jnp.concatenate([w_t[offs[0]:offs[4]], w_t[offs[4]:offs[5]], w_t[offs[8]:offs[10]],
                          jnp.zeros((LANES - GATE_END, d), F32)], axis=0).astype(BF16)
    wm = w_t[offs[5]:offs[8]].astype(BF16)
    gbias = jnp.concatenate([jnp.zeros((GATE_I0,), F32), b_igate[0], b_fgate[0],
                             jnp.zeros((LANES - GATE_END,), F32)]).reshape(1, LANES)
    wuk_bd = jnp.stack([_block_diag(w_uk[0][2 * p:2 * p + 2]) for p in range(HEAD_PAIRS)]).astype(BF16)
    wuv_bd = jnp.stack([_block_diag(w_uv[0][2 * p:2 * p + 2]) for p in range(HEAD_PAIRS)]).astype(BF16)
    kvg = row2(kv_norm_g)
    convw = conv_w[0].astype(F32)

    h1, h1_meta = _ffn_ln_call(x.reshape(bsz * seq, d), meta_tokens.astype(F32), ffn1_w_gate[0], ffn1_w_up[0],
                               ffn1_w_down[0], row2(ln1_g), row2(ln1_b), alpha, FFN_TILE)
    zero_tail = jnp.zeros((CONV_HIST, MLQK_WIDTH), F32)
    (_, m_ckv, _, m_kidx, m_qk, m_v, _, m_gates, m_tail) = _inproj_call(
        h1_meta[None], zero_tail, wa, wm, wuk_bd, kvg, convw, gbias, N_META)

    (_, ckv, qidx, kidx, qk, _, og, gates, _, gates_t, ckv_t, v_t, q_t, qlat_t) = _inproj_call(
        h1.reshape(bsz, seq, d), m_tail[0], wa, wm, wuk_bd, kvg, convw, gbias, ROW_TILE)

    nchunks = seq // KEY_CHUNK
    ckv_c = ckv.reshape(bsz, nchunks, KEY_CHUNK, KV_LATENT)
    kidx_c = kidx.reshape(bsz, nchunks, KEY_CHUNK, IDX_DIM)
    ones_rows = jnp.zeros((ATT_EXT - KV_LATENT, N_META), BF16).at[0].set(1.0)
    m_ckvt = jnp.concatenate([m_ckv[0].T, ones_rows], axis=0)
    y_att = _dsa_call(qlat_t, qidx, gates_t, ckv_c, ckv_t, kidx_c,
                      m_ckv[0], m_ckvt, m_kidx[0], wuv_bd, topk)

    lane = jnp.arange(LANES)
    pad_gate = jnp.where((lane >= GATE_I0) & (lane < GATE_F0), NEG_BIG, 0.0).astype(F32)
    mg = jnp.concatenate([m_gates[0], jnp.broadcast_to(pad_gate, (ML_CHUNK - N_META, LANES))], axis=0)
    gate_lanes = slice(GATE_W0, GATE_W0 + GATE_ROWS)
    m_qk_pad = _pad_rows(m_qk[0], ML_CHUNK)
    y_ml = _mlstm_call(qk, q_t, v_t, og, gates, gates_t,
                       m_qk_pad, m_qk_pad[:, :MLQK_WIDTH // 2].T, _pad_rows(m_v[0], ML_CHUNK).T, mg, mg[:, gate_lanes].T,
                       row2(ml_norm_g))

    out = _out_ffn_ln_call(
        y_att.reshape(bsz * seq, ATT_WIDTH), y_ml.reshape(bsz * seq, ML_WIDTH), h1, bf(w_out),
        row2(ln2_g), row2(ln2_b), ffn2_w_gate[0], ffn2_w_up[0], ffn2_w_down[0], row2(ln3_g), row2(ln3_b),
        alpha, FFN_TILE)
    return out.reshape(bsz, seq, d)
```

```python
import functools

import jax
import jax.numpy as jnp
from jax import lax
from jax.experimental import pallas as pl
from jax.experimental.pallas import tpu as pltpu

F32 = jnp.float32
BF16 = jnp.bfloat16

N_META = 16
ATT_HEADS = 8
ATT_HEAD_DIM = 64
KV_LATENT = 128
IDX_HEADS = 4
IDX_DIM = 64
TOPK_MAX = 256
ML_HEADS = 4
ML_V_DIM = 128
ML_QK_DIM = 64
CONV_WIDTH = 4
GATE_SOFTCAP = 15.0
M_INIT = -1e30
LN_EPS = 1e-5
NEG_BIG = -1e30
LOG2_E = 1.4426950408889634

LANES = 128
SUBLANES = 8
VMEM_BYTES_V7X = 64 * 1024 * 1024
VMEM_LIMIT_BYTES = VMEM_BYTES_V7X * 7 // 8

FF_CHUNK = 256
ROW_TILE = 512
FFN_TILE = 512
Q_TILE = 256
ATT_GROUP = 128
KEY_CHUNK = 256
N_BISECT = 16
REDUCE_ROWS = 32
ML_BATCH = 2
ML_CHUNK = 256


def _dot(a, b):
    return jnp.dot(a, b, preferred_element_type=F32)


def _layer_norm(z, g, b):
    mu = jnp.mean(z, axis=-1, keepdims=True)
    zc = z - mu
    var = jnp.mean(zc * zc, axis=-1, keepdims=True)
    return zc * lax.rsqrt(var + LN_EPS) * g + b


def _sigmoid(x):
    return 1.0 / (1.0 + jnp.exp(-x))


def _swiglu_chunk(xb, wg_c, wu_c, wd_c):
    g = _dot(xb, wg_c)
    u = _dot(xb, wu_c)
    return _dot((g * _sigmoid(g) * u).astype(BF16), wd_c)


def _ffn_ln(x, wg_s, wu_s, wd_s, g, b, alpha):
    xb = x.astype(BF16)
    acc = jnp.zeros(x.shape, F32)
    for c in range(wg_s.shape[0]):
        acc = acc + _swiglu_chunk(xb, wg_s[c], wu_s[c], wd_s[c])
    return _layer_norm(alpha * x + 0.5 * acc, g, b)


def _stage_ffn_weights(step, wg_ref, wu_ref, wd_ref, wg_s, wu_s, wd_s):
    wg_s[step] = wg_ref[...].astype(BF16)
    wu_s[step] = wu_ref[...].astype(BF16)
    wd_s[step] = wd_ref[...].astype(BF16)


def _ffn_weight_specs(d, d_ff):
    n = d_ff // FF_CHUNK
    col = pl.BlockSpec((d, FF_CHUNK), lambda i: (0, jnp.minimum(i, n - 1)))
    row = pl.BlockSpec((FF_CHUNK, d), lambda i: (jnp.minimum(i, n - 1), 0))
    scratch = [pltpu.VMEM((n, d, FF_CHUNK), BF16), pltpu.VMEM((n, d, FF_CHUNK), BF16),
               pltpu.VMEM((n, FF_CHUNK, d), BF16)]
    return n, [col, col, row], scratch


def _staged_tile_index(i, n_stage):
    return jnp.maximum(i - (n_stage - 1), 0)


def _ffn_ln_kernel(x_ref, meta_ref, wg_ref, wu_ref, wd_ref, g_ref, b_ref, o_ref, ometa_ref,
                   wg_s, wu_s, wd_s, macc_ref, acc0_ref, *, alpha, n_stage):
    i = pl.program_id(0)

    @pl.when(i == 0)
    def _():
        macc_ref[...] = jnp.zeros(macc_ref.shape, F32)
        acc0_ref[...] = jnp.zeros(acc0_ref.shape, F32)

    @pl.when(i < n_stage)
    def _():
        _stage_ffn_weights(i, wg_ref, wu_ref, wd_ref, wg_s, wu_s, wd_s)
        macc_ref[...] += _swiglu_chunk(meta_ref[...].astype(BF16), wg_s[i], wu_s[i], wd_s[i])
        acc0_ref[...] += _swiglu_chunk(x_ref[...].astype(BF16), wg_s[i], wu_s[i], wd_s[i])

    @pl.when(i == n_stage - 1)
    def _():
        ometa_ref[...] = _layer_norm(alpha * meta_ref[...] + 0.5 * macc_ref[...], g_ref[...], b_ref[...])
        o_ref[...] = _layer_norm(alpha * x_ref[...] + 0.5 * acc0_ref[...], g_ref[...], b_ref[...])

    @pl.when(i >= n_stage)
    def _():
        o_ref[...] = _ffn_ln(x_ref[...], wg_s, wu_s, wd_s, g_ref[...], b_ref[...], alpha)


def _resident(shape):
    return pl.BlockSpec(shape, lambda *_: (0,) * len(shape), pipeline_mode=pl.Buffered(1))


def _ffn_ln_call(x, meta, wg, wu, wd, g, b, alpha, tm):
    rows, d = x.shape
    n_stage, w_specs, w_scratch = _ffn_weight_specs(d, wg.shape[1])
    row_spec = pl.BlockSpec((tm, d), lambda i: (_staged_tile_index(i, n_stage), 0))
    return pl.pallas_call(
        functools.partial(_ffn_ln_kernel, alpha=alpha, n_stage=n_stage),
        grid=(n_stage - 1 + rows // tm,),
        in_specs=[row_spec, _resident(meta.shape)] + w_specs + [_resident(g.shape), _resident(b.shape)],
        out_specs=[row_spec, pl.BlockSpec(meta.shape, lambda i: (0, 0))],
        out_shape=[jax.ShapeDtypeStruct((rows, d), F32), jax.ShapeDtypeStruct(meta.shape, F32)],
        scratch_shapes=w_scratch + [pltpu.VMEM(meta.shape, F32), pltpu.VMEM((tm, d), F32)],
        compiler_params=pltpu.CompilerParams(
            dimension_semantics=("arbitrary",), vmem_limit_bytes=VMEM_LIMIT_BYTES),
        name="ffn_ln",
    )(x, meta, wg, wu, wd, g, b)


ATT_WIDTH = ATT_HEADS * ATT_HEAD_DIM
IDX_WIDTH = IDX_HEADS * IDX_DIM
MLQK_WIDTH = 2 * ML_HEADS * ML_QK_DIM
ML_WIDTH = ML_HEADS * ML_V_DIM
LAT_WIDTH = ATT_HEADS * KV_LATENT
CONV_HIST = SUBLANES
GATE_W0 = IDX_DIM
GATE_I0, GATE_F0, GATE_END = GATE_W0 + IDX_HEADS, GATE_W0 + IDX_HEADS + ML_HEADS, GATE_W0 + IDX_HEADS + 2 * ML_HEADS
GATE_ROWS = 2 * SUBLANES
GT_CHUNK = 256
ATT_EXT = KV_LATENT + 2 * SUBLANES
HEAD_PAIRS = ATT_HEADS // 2


def _inproj_kernel(h_ref, tail_ref, wa_ref, wm_ref, wuk_ref, kvg_ref, convw_ref, gbias_ref,
                   qlat_ref, ckv_ref, qidx_ref, kidx_ref, qk_ref, v_ref, og_ref, gates_ref, tailout_ref, *rest):
    carry_ref, wa_s, wm_s = rest[-3:]
    tm = h_ref.shape[1]

    @pl.when((pl.program_id(0) == 0) & (pl.program_id(1) == 0))
    def _():
        wa_s[...] = wa_ref[...].astype(F32).T.astype(BF16)
        wm_s[...] = wm_ref[...].astype(F32).T.astype(BF16)

    @pl.when(pl.program_id(1) == 0)
    def _():
        carry_ref[...] = tail_ref[...]

    xb = h_ref[0].astype(BF16)

    pa = _dot(xb, wa_s[...])
    q_a = pa[:, :ATT_WIDTH].astype(BF16)
    c0 = ATT_WIDTH
    ckv = pa[:, c0:c0 + KV_LATENT]
    c1 = c0 + KV_LATENT
    ckv = ckv * lax.rsqrt(jnp.mean(ckv * ckv, axis=-1, keepdims=True) + LN_EPS) * kvg_ref[...]
    ckv_ref[0] = ckv.astype(BF16)
    qidx_ref[0] = pa[:, c1:c1 + IDX_WIDTH].astype(BF16)
    c2 = c1 + IDX_WIDTH
    kidx_ref[0] = pa[:, c2:c2 + IDX_DIM].astype(BF16)
    pair_in, pair_out = 2 * ATT_HEAD_DIM, 2 * KV_LATENT
    q_lat = []
    for p in range(HEAD_PAIRS):
        ql = _dot(q_a[:, p * pair_in:(p + 1) * pair_in], wuk_ref[p]) * (ATT_HEAD_DIM ** -0.5 * LOG2_E)
        qlat_ref[0, :, p * pair_out:(p + 1) * pair_out] = ql.astype(BF16)
        q_lat.append(ql)

    pm = _dot(xb, wm_s[...])
    qk_raw = pm[:, :MLQK_WIDTH]
    v_ref[0] = pm[:, MLQK_WIDTH:MLQK_WIDTH + ML_WIDTH].astype(BF16)
    og_ref[0] = _sigmoid(pm[:, MLQK_WIDTH + ML_WIDTH:])

    ext = jnp.concatenate([carry_ref[...], qk_raw], axis=0)
    cw = convw_ref[...]
    conv = jnp.zeros_like(qk_raw)
    for j in range(CONV_WIDTH):
        s0 = CONV_HIST - (CONV_WIDTH - 1) + j
        conv = conv + ext[s0:s0 + tm] * cw[j:j + 1]
    act = conv * _sigmoid(conv)
    half = MLQK_WIDTH // 2
    qk_ref[0, :, :half] = act[:, :half].astype(BF16)
    qk_ref[0, :, half:] = (act[:, half:] * (ML_QK_DIM ** -0.5)).astype(BF16)
    carry_ref[...] = qk_raw[tm - CONV_HIST:]
    tailout_ref[0] = qk_raw[tm - CONV_HIST:]

    gr = pa[:, c2 + IDX_DIM - GATE_W0:]
    lane = lax.broadcasted_iota(jnp.int32, gr.shape, 1)
    sc = GATE_SOFTCAP * jnp.tanh((gr + gbias_ref[...]) / GATE_SOFTCAP)
    lf = -(jnp.maximum(-sc, 0.0) + jnp.log1p(jnp.exp(-jnp.abs(sc))))
    w_scaled = gr * (IDX_HEADS ** -0.5 * IDX_DIM ** -0.5)
    gates = jnp.where((lane < GATE_W0) | (lane >= GATE_END), 0.0,
                      jnp.where(lane < GATE_I0, w_scaled, jnp.where(lane < GATE_F0, sc, lf)))
    gates_ref[0] = gates
    if len(rest) == 9:
        gt_ref, ckvt_ref, vt_ref, qt_ref, qlatt_ref, qidxt_ref = rest[:6]
        q_idx_t = pa[:, c1:c1 + IDX_WIDTH].T.astype(BF16)
        q_lat_t = jnp.concatenate([ql.T for ql in q_lat], axis=0).astype(BF16)
        gates_t = gates.T[GATE_W0:GATE_W0 + GATE_ROWS]
        ckv_t = ckv.T.astype(BF16)
        v_t = pm[:, MLQK_WIDTH:MLQK_WIDTH + ML_WIDTH].T.astype(BF16)
        q_t = act[:, :half].T.astype(BF16)
        ones_blk = jnp.where(lax.broadcasted_iota(jnp.int32, (ATT_EXT - KV_LATENT, GT_CHUNK), 0) == 0,
                             1.0, 0.0).astype(BF16)
        for j in range(tm // GT_CHUNK):
            piece = slice(j * GT_CHUNK, (j + 1) * GT_CHUNK)
            gt_ref[0, j] = gates_t[:, piece]
            ckvt_ref[0, j] = jnp.concatenate([ckv_t[:, piece], ones_blk], axis=0)
            vt_ref[0, j] = v_t[:, piece]
            qt_ref[0, j] = q_t[:, piece]
            qlatt_ref[0, j] = q_lat_t[:, piece]
            qidxt_ref[0, j] = q_idx_t[:, piece]


def _inproj_call(h, tail, wa, wm, wuk_bd, kvg, convw, gbias, tm):
    bn, rows, d = h.shape
    nblk = rows // tm
    emit_gt = tm % GT_CHUNK == 0

    def row_spec(width):
        return pl.BlockSpec((1, tm, width), lambda b, j: (b, j, 0))

    outs = [
        (LAT_WIDTH, BF16), (KV_LATENT, BF16), (IDX_WIDTH, BF16), (IDX_DIM, BF16),
        (MLQK_WIDTH, BF16), (ML_WIDTH, BF16), (ML_WIDTH, F32), (LANES, F32),
    ]
    out_shape = [jax.ShapeDtypeStruct((bn, rows, w), dt) for w, dt in outs]
    out_specs = [row_spec(w) for w, _ in outs]
    out_shape.append(jax.ShapeDtypeStruct((bn, CONV_HIST, MLQK_WIDTH), F32))
    out_specs.append(pl.BlockSpec((1, CONV_HIST, MLQK_WIDTH), lambda b, j: (b, 0, 0)))
    if emit_gt:
        per_tile = tm // GT_CHUNK
        for height, dt in ((GATE_ROWS, F32), (ATT_EXT, BF16), (ML_WIDTH, BF16), (MLQK_WIDTH // 2, BF16),
                           (LAT_WIDTH, BF16), (IDX_WIDTH, BF16)):
            out_shape.append(jax.ShapeDtypeStruct((bn, rows // GT_CHUNK, height, GT_CHUNK), dt))
            out_specs.append(pl.BlockSpec((1, per_tile, height, GT_CHUNK), lambda b, j: (b, j, 0, 0)))
    return pl.pallas_call(
        _inproj_kernel,
        grid=(bn, nblk),
        in_specs=[
            row_spec(d),
            _resident(tail.shape), _resident(wa.shape), _resident(wm.shape),
            _resident(wuk_bd.shape), _resident(kvg.shape), _resident(convw.shape), _resident(gbias.shape),
        ],
        out_specs=out_specs,
        out_shape=out_shape,
        scratch_shapes=[pltpu.VMEM((CONV_HIST, MLQK_WIDTH), F32),
                        pltpu.VMEM(wa.shape[::-1], BF16), pltpu.VMEM(wm.shape[::-1], BF16)],
        compiler_params=pltpu.CompilerParams(
            dimension_semantics=("arbitrary", "arbitrary"), vmem_limit_bytes=VMEM_LIMIT_BYTES),
        name="in_proj",
    )(h, tail, wa, wm, wuk_bd, kvg, convw, gbias)


def _dsa_kernel(qlat_ref, qidx_ref, wrow_ref, ckv_ref, ckvt_ref, kidx_ref, mckv_ref, mckvt_ref, mkidx_ref,
                wuv_ref, y_ref, s_ref, acc_ref, p_ref, *, topk):
    _, kc, tq = s_ref.shape
    i = pl.program_id(1)
    nch = ((i + 1) * tq + kc - 1) // kc
    qreal = i * tq + lax.broadcasted_iota(jnp.int32, (1, tq), 1)
    kf = float(topk)

    wrow = wrow_ref[0, 0]
    q_idx_t = qidx_ref[0, 0]
    q_idx_all_t = jnp.concatenate([q_idx_t[h * IDX_DIM:(h + 1) * IDX_DIM] for h in range(IDX_HEADS)],
                                  axis=1)
    wi = [wrow[h:h + 1, :] for h in range(IDX_HEADS)]

    def scores(k_rows):
        lg = _dot(k_rows, q_idx_all_t)
        sc = jnp.zeros((k_rows.shape[0], tq), F32)
        for h in range(IDX_HEADS):
            sc = sc + jnp.maximum(lg[:, h * tq:(h + 1) * tq], 0.0) * wi[h]
        return sc

    s_meta = scores(mkidx_ref[...])

    def score_chunk(c, lo, hi):
        sc = scores(kidx_ref[0, c])
        valid = c * kc + lax.broadcasted_iota(jnp.int32, (kc, tq), 0) <= qreal
        s_ref[c] = jnp.where(valid, sc, -jnp.inf)
        groups = (kc // REDUCE_ROWS, REDUCE_ROWS, tq)
        lo = jnp.minimum(lo, jnp.min(sc.reshape(groups), axis=0))
        hi = jnp.maximum(hi, jnp.max(sc.reshape(groups), axis=0))
        return lo, hi

    def score_pair(c2, carry):
        lo, hi = score_chunk(2 * c2, *carry)
        return score_chunk(jnp.minimum(2 * c2 + 1, nch - 1), lo, hi)

    lo, hi = lax.fori_loop(0, (nch + 1) // 2, score_pair,
                           (jnp.full((REDUCE_ROWS, tq), jnp.inf, F32), jnp.full((REDUCE_ROWS, tq), -jnp.inf, F32)))
    lo = jnp.minimum(jnp.min(lo, axis=0, keepdims=True), jnp.min(s_meta, axis=0, keepdims=True))
    hi = jnp.maximum(jnp.max(hi, axis=0, keepdims=True), jnp.max(s_meta, axis=0, keepdims=True))

    def key_reduce(reduce, combine, per_chunk, init):
        def body(c, acc):
            x = per_chunk(s_ref[c]).reshape(kc // REDUCE_ROWS, REDUCE_ROWS, tq)
            return combine(acc, reduce(x, axis=0))
        acc = lax.fori_loop(0, nch, body, jnp.full((REDUCE_ROWS, tq), init, F32))
        return combine(reduce(acc, axis=0, keepdims=True), reduce(per_chunk(s_meta), axis=0, keepdims=True))

    def count(pred):
        return key_reduce(jnp.sum, jnp.add, lambda sc: jnp.where(pred(sc), 1.0, 0.0), 0.0)

    def max_where(pred):
        return key_reduce(jnp.max, jnp.maximum, lambda sc: jnp.where(pred(sc), sc, -jnp.inf), -jnp.inf)

    def bisect(_, carry):
        lo, hi = carry
        mid = 0.5 * lo + 0.5 * hi
        up = count(lambda sc: sc > mid) >= kf
        return jnp.where(up, mid, lo), jnp.where(up, hi, mid)

    lo, hi = lax.fori_loop(0, N_BISECT, bisect, (lo, hi))

    n_valid = (qreal + (N_META + 1)).astype(F32)
    small = n_valid <= kf
    cand = max_where(lambda sc: sc <= hi)
    n_ge = count(lambda sc: sc >= cand)
    done = jnp.where(small | (n_ge >= kf), 1.0, 0.0)

    def not_finished(state):
        return jnp.min(state[1]) < 0.5

    def step_down(state):
        cand, done, _ = state
        nxt = jnp.where(done > 0.5, cand, max_where(lambda sc: sc < cand))
        n_ge = count(lambda sc: sc >= nxt)
        return nxt, jnp.where(n_ge >= kf, 1.0, done), n_ge

    cand, _, n_ge = lax.while_loop(not_finished, step_down, step_down((cand, done, n_ge)))
    thr = jnp.where(small, -jnp.inf, cand)
    n_eq = count(lambda sc: sc == thr)
    need = jnp.where(small, 0.0, kf - (n_ge - n_eq))
    ranked_ties = jnp.max(jnp.where(n_eq > need, 1.0, 0.0)) > 0.5

    qlat_t = qlat_ref[0, 0]
    n_groups = tq // ATT_GROUP
    onehot = (lax.broadcasted_iota(jnp.int32, (ATT_GROUP, ATT_GROUP), 0)
              == lax.broadcasted_iota(jnp.int32, (ATT_GROUP, ATT_GROUP), 1)).astype(BF16)
    q_aug_t = [jnp.concatenate(
        [jnp.concatenate([qlat_t[h * KV_LATENT:(h + 1) * KV_LATENT, g * ATT_GROUP:(g + 1) * ATT_GROUP]
                          for h in range(ATT_HEADS)], axis=1),
         jnp.concatenate([onehot] * ATT_HEADS, axis=1)], axis=0)
        for g in range(n_groups)]
    hq = ATT_HEADS * ATT_GROUP

    def lower_tri(n):
        return (lax.broadcasted_iota(jnp.int32, (n, n), 1) <= lax.broadcasted_iota(jnp.int32, (n, n), 0)).astype(BF16)

    def attention(ranked):
        def mask_bias(sc, eq_seen):
            if not ranked:
                return jnp.where(sc >= thr, 0.0, NEG_BIG).astype(BF16), eq_seen
            n = sc.shape[0]
            eq = sc == thr
            rank = _dot(lower_tri(n), jnp.where(eq, 1.0, 0.0).astype(BF16)) + eq_seen
            keep = (sc > thr) | (eq & (rank <= need))
            return jnp.where(keep, 0.0, NEG_BIG).astype(BF16), rank[n - 1:n, :]

        def logits(g, kv, bias):
            k_aug = jnp.concatenate([kv, bias[:, g * ATT_GROUP:(g + 1) * ATT_GROUP]], axis=1)
            return _dot(k_aug, q_aug_t[g])

        def fold_in(g, c_prev, a_prev):
            acc_ref[g] = a_prev * acc_ref[g] + _dot(ckvt_ref[0, c_prev], p_ref[g])

        def attend(c, carry):
            bias, eq_seen = mask_bias(s_ref[c], carry[0])
            out = [eq_seen]
            for g in range(n_groups):
                m, a_prev = carry[1 + 2 * g:3 + 2 * g]
                fold_in(g, c - 1, a_prev)
                s = logits(g, ckv_ref[0, c], bias)
                m_new = jnp.maximum(m, jnp.max(s, axis=0, keepdims=True))
                p_ref[g] = jnp.exp2(s - m_new).astype(BF16)
                out += [m_new, jnp.exp2(m - m_new)]
            return tuple(out)

        bias_m, eq_seen = mask_bias(s_meta, jnp.zeros((1, tq), F32))
        bias_0, eq_seen = mask_bias(s_ref[0], eq_seen)
        kv_first = jnp.concatenate([mckv_ref[...], ckv_ref[0, 0]], axis=0)
        bias_first = jnp.concatenate([bias_m, bias_0], axis=0)
        init = [eq_seen]
        for g in range(n_groups):
            s = logits(g, kv_first, bias_first)
            m = jnp.max(s, axis=0, keepdims=True)
            p = jnp.exp2(s - m).astype(BF16)
            acc_ref[g] = _dot(mckvt_ref[...], p[:N_META])
            p_ref[g] = p[N_META:]
            init += [m, jnp.ones((1, hq), F32)]
        carry = lax.fori_loop(1, nch, attend, tuple(init))
        rows = []
        for g in range(n_groups):
            fold_in(g, nch - 1, carry[2 + 2 * g])
            acc = acc_ref[g]
            o_t = (acc[:KV_LATENT] / acc[KV_LATENT:KV_LATENT + 1]).T
            rows.append(jnp.concatenate([o_t[h * ATT_GROUP:(h + 1) * ATT_GROUP] for h in range(ATT_HEADS)], axis=1))
        o_all = jnp.concatenate(rows, axis=0).astype(BF16)
        pair = 2 * KV_LATENT
        return jnp.concatenate([_dot(o_all[:, p * pair:(p + 1) * pair], wuv_ref[p]) for p in range(HEAD_PAIRS)],
                               axis=1).astype(BF16)

    y_ref[0] = lax.cond(ranked_ties, lambda: attention(True), lambda: attention(False))


def _dsa_call(qlat_t, qidx_t, wrow, ckv_c, ckvt_c, kidx_c, m_ckv, m_ckvt, m_kidx, wuv_bd, topk):
    bn, nq = qidx_t.shape[:2]
    rows = nq * Q_TILE
    nchunks, kc = ckv_c.shape[1], ckv_c.shape[2]

    def tile_spec(height):
        return pl.BlockSpec((1, 1, height, Q_TILE), lambda b, i: (b, i, 0, 0))

    def q_spec(width):
        return pl.BlockSpec((1, Q_TILE, width), lambda b, i: (b, i, 0))

    def k_spec(a):
        return pl.BlockSpec((1,) + a.shape[1:], lambda b, i: (b, 0, 0, 0))

    return pl.pallas_call(
        functools.partial(_dsa_kernel, topk=topk),
        grid=(bn, nq),
        in_specs=[tile_spec(LAT_WIDTH), tile_spec(IDX_WIDTH), tile_spec(GATE_ROWS),
                  k_spec(ckv_c), k_spec(ckvt_c), k_spec(kidx_c),
                  _resident(m_ckv.shape), _resident(m_ckvt.shape), _resident(m_kidx.shape), _resident(wuv_bd.shape)],
        out_specs=q_spec(ATT_WIDTH),
        out_shape=jax.ShapeDtypeStruct((bn, rows, ATT_WIDTH), BF16),
        scratch_shapes=[pltpu.VMEM((nchunks, kc, Q_TILE), F32),
                        pltpu.VMEM((Q_TILE // ATT_GROUP, ckvt_c.shape[2], ATT_HEADS * ATT_GROUP), F32),
                        pltpu.VMEM((Q_TILE // ATT_GROUP, kc, ATT_HEADS * ATT_GROUP), BF16)],
        compiler_params=pltpu.CompilerParams(
            dimension_semantics=("arbitrary", "arbitrary"), vmem_limit_bytes=VMEM_LIMIT_BYTES),
        name="dsa",
    )(qlat_t, qidx_t, wrow, ckv_c, ckvt_c, kidx_c, m_ckv, m_ckvt, m_kidx, wuv_bd)


def _split3(x):
    hi = x.astype(BF16)
    r = x - hi.astype(F32)
    mid = r.astype(BF16)
    lo = (r - mid.astype(F32)).astype(BF16)
    return hi, mid, lo


ML_EXT = ML_V_DIM + 16


def _mlstm_chunk(qk, qt, vt, g, gt, state):
    L = qk.shape[0]
    s_idx = lax.broadcasted_iota(jnp.int32, (L, L), 0)
    t_idx = lax.broadcasted_iota(jnp.int32, (L, L), 1)
    causal = s_idx <= t_idx
    w = g.shape[1]
    cols3 = _dot((t_idx <= s_idx).astype(BF16), jnp.concatenate(_split3(g), axis=1))
    b_cols = cols3[:, :w] + cols3[:, w:2 * w] + cols3[:, 2 * w:]
    r = gt.shape[0]
    rows3 = _dot(jnp.concatenate(_split3(gt), axis=0), causal.astype(BF16))
    b_rows = rows3[:r] + rows3[r:2 * r] + rows3[2 * r:]
    ones_blk = jnp.where(lax.broadcasted_iota(jnp.int32, (ML_EXT - ML_V_DIM, L), 0) == 0, 1.0, 0.0).astype(BF16)
    kq = ML_HEADS * ML_QK_DIM

    outs, new_state = [], []
    for h in range(ML_HEADS):
        ce, m_prev = state[h]
        c_col = g[:, GATE_I0 + h:GATE_I0 + h + 1] - b_cols[:, GATE_F0 + h:GATE_F0 + h + 1]
        b_row = b_rows[GATE_F0 - GATE_W0 + h:GATE_F0 - GATE_W0 + h + 1, :]
        ig_row = gt[GATE_I0 - GATE_W0 + h:GATE_I0 - GATE_W0 + h + 1, :]
        qt_h = qt[h * ML_QK_DIM:(h + 1) * ML_QK_DIM, :]
        kh = qk[:, kq + h * ML_QK_DIM:kq + (h + 1) * ML_QK_DIM]
        vt_ext = jnp.concatenate([vt[h * ML_V_DIM:(h + 1) * ML_V_DIM, :], ones_blk], axis=0)

        d_t = jnp.where(causal, c_col + b_row, -jnp.inf)
        inter = b_row + m_prev
        m_t = jnp.maximum(jnp.max(d_t, axis=0, keepdims=True), inter)
        w_inter = jnp.exp(inter - m_t)
        s_t = _dot(kh, qt_h) * jnp.exp(d_t - m_t)
        r = _dot(vt_ext, s_t.astype(BF16)) + _dot(ce.astype(BF16), qt_h) * w_inter
        num = r[:ML_V_DIM]
        den = r[ML_V_DIM:ML_V_DIM + 1]
        hh = num / jnp.maximum(jnp.abs(den), jnp.exp(-m_t))
        mu = jnp.mean(hh, axis=0, keepdims=True)
        hc = hh - mu
        var = jnp.mean(hc * hc, axis=0, keepdims=True)
        outs.append((hc * lax.rsqrt(var + LN_EPS)).T)

        b_end = b_row[:, L - 1:L]
        g_row = b_end - b_row + ig_row
        m_new = jnp.maximum(b_end + m_prev, jnp.max(g_row, axis=1, keepdims=True))
        decay = jnp.exp(b_end + m_prev - m_new)
        weighted = (vt_ext.astype(F32) * jnp.exp(g_row - m_new)).astype(BF16)
        new_state.append((decay * ce + _dot(weighted, kh), m_new))
    return outs, new_state


def _mlstm_kernel(qk_ref, qt_ref, vt_ref, og_ref, gates_ref, gt_ref, mqk_ref, mqt_ref, mvt_ref, mgates_ref, mgt_ref,
                  ng_ref, y_ref,
                  ce0_ref, m0_ref):
    L = mqk_ref.shape[0]
    n_chunks = qk_ref.shape[1] // L
    norm_g = ng_ref[...]

    @pl.when(pl.program_id(0) == 0)
    def _():
        state = [(jnp.zeros((ML_EXT, ML_QK_DIM), F32), jnp.full((1, 1), M_INIT, F32)) for _ in range(ML_HEADS)]
        _, state = _mlstm_chunk(mqk_ref[...], mqt_ref[...], mvt_ref[...], mgates_ref[...], mgt_ref[...], state)
        for h in range(ML_HEADS):
            ce0_ref[h] = state[h][0]
            m0_ref[h] = jnp.broadcast_to(state[h][1], m0_ref.shape[1:])

    n_b = qk_ref.shape[0]
    state = [(ce0_ref[h], m0_ref[h][0:1, 0:1]) for h in range(ML_HEADS)] * n_b

    def body(c, flat):
        rows = pl.ds(pl.multiple_of(c * L, L), L)
        new_flat = []
        for b in range(n_b):
            state = [(flat[2 * (b * ML_HEADS + h)], flat[2 * (b * ML_HEADS + h) + 1]) for h in range(ML_HEADS)]
            outs, state = _mlstm_chunk(qk_ref[b, rows, :], qt_ref[b, c], vt_ref[b, c], gates_ref[b, rows, :],
                                       gt_ref[b, c], state)
            y = jnp.concatenate(outs, axis=1) * norm_g * og_ref[b, rows, :]
            y_ref[b, rows, :] = y.astype(BF16)
            new_flat += [x for pair in state for x in pair]
        return tuple(new_flat)

    lax.fori_loop(0, n_chunks, body, tuple(x for pair in state for x in pair))


def _mlstm_call(qk, qt_c, vt_c, og, gates, gt_c, mqk, mqt, mvt, mgates, mgt, norm_g):
    bn, rows, _ = qk.shape
    nb = ML_BATCH if bn % ML_BATCH == 0 else 1

    def b_spec(a):
        return pl.BlockSpec((nb,) + a.shape[1:], lambda b: (b,) + (0,) * (a.ndim - 1))

    consts = (mqk, mqt, mvt, mgates, mgt, norm_g)
    return pl.pallas_call(
        _mlstm_kernel,
        grid=(bn // nb,),
        in_specs=[b_spec(qk), b_spec(qt_c), b_spec(vt_c), b_spec(og), b_spec(gates), b_spec(gt_c)]
        + [_resident(c.shape) for c in consts],
        out_specs=pl.BlockSpec((nb, rows, ML_WIDTH), lambda b: (b, 0, 0)),
        out_shape=jax.ShapeDtypeStruct((bn, rows, ML_WIDTH), BF16),
        scratch_shapes=[pltpu.VMEM((ML_HEADS, ML_EXT, ML_QK_DIM), F32), pltpu.VMEM((ML_HEADS, SUBLANES, LANES), F32)],
        compiler_params=pltpu.CompilerParams(
            dimension_semantics=("arbitrary",), vmem_limit_bytes=VMEM_LIMIT_BYTES),
        name="mlstm",
    )(qk, qt_c, vt_c, og, gates, gt_c, *consts)


def _out_ffn_ln_kernel(ya_ref, ym_ref, h_ref, wo_ref, g2_ref, b2_ref, wg_ref, wu_ref, wd_ref, g3_ref, b3_ref,
                       o_ref, wg_s, wu_s, wd_s, h2_ref, acc0_ref, *, alpha, n_stage):
    i = pl.program_id(0)

    def mixed():
        mix = _dot(ya_ref[...], wo_ref[:ATT_WIDTH, :]) + _dot(ym_ref[...], wo_ref[ATT_WIDTH:, :])
        return _layer_norm(alpha * h_ref[...] + mix, g2_ref[...], b2_ref[...])

    @pl.when(i == 0)
    def _():
        h2_ref[...] = mixed()
        acc0_ref[...] = jnp.zeros(acc0_ref.shape, F32)

    @pl.when(i < n_stage)
    def _():
        _stage_ffn_weights(i, wg_ref, wu_ref, wd_ref, wg_s, wu_s, wd_s)
        acc0_ref[...] += _swiglu_chunk(h2_ref[...].astype(BF16), wg_s[i], wu_s[i], wd_s[i])

    @pl.when(i == n_stage - 1)
    def _():
        o_ref[...] = _layer_norm(alpha * h2_ref[...] + 0.5 * acc0_ref[...], g3_ref[...], b3_ref[...])

    @pl.when(i >= n_stage)
    def _():
        o_ref[...] = _ffn_ln(mixed(), wg_s, wu_s, wd_s, g3_ref[...], b3_ref[...], alpha)


def _out_ffn_ln_call(ya, ym, h, wo, g2, b2, wg, wu, wd, g3, b3, alpha, tm):
    rows, d = h.shape
    n_stage, w_specs, w_scratch = _ffn_weight_specs(d, wg.shape[1])

    def row_spec(width):
        return pl.BlockSpec((tm, width), lambda i: (_staged_tile_index(i, n_stage), 0))

    return pl.pallas_call(
        functools.partial(_out_ffn_ln_kernel, alpha=alpha, n_stage=n_stage),
        grid=(n_stage - 1 + rows // tm,),
        in_specs=[row_spec(ya.shape[1]), row_spec(ym.shape[1]), row_spec(d),
                  _resident(wo.shape), _resident(g2.shape), _resident(b2.shape)] + w_specs
        + [_resident(g3.shape), _resident(b3.shape)],
        out_specs=row_spec(d),
        out_shape=jax.ShapeDtypeStruct((rows, d), F32),
        scratch_shapes=w_scratch + [pltpu.VMEM((tm, d), F32), pltpu.VMEM((tm, d), F32)],
        compiler_params=pltpu.CompilerParams(
            dimension_semantics=("arbitrary",), vmem_limit_bytes=VMEM_LIMIT_BYTES),
        name="out_ffn_ln",
    )(ya, ym, h, wo, g2, b2, wg, wu, wd, g3, b3)


def _block_diag(w):
    nh, a, b = w.shape
    eye = jnp.eye(nh, dtype=w.dtype)
    return (eye[:, None, :, None] * w[:, :, None, :]).reshape(nh * a, nh * b)


def _pad_rows(a, rows, value=0.0):
    return jnp.pad(a, ((0, rows - a.shape[0]), (0, 0)), constant_values=value)


def kernel(x, meta_tokens, ln1_g, ln1_b, ffn1_w_gate, ffn1_w_up, ffn1_w_down, w_in, w_uk, w_uv, kv_norm_g,
           conv_w, b_igate, b_fgate, ml_norm_g, w_out, ln2_g, ln2_b, ffn2_w_gate, ffn2_w_up, ffn2_w_down,
           ln3_g, ln3_b):
    depth = ln1_g.shape[0]
    assert depth == 1, "the meta-token shortcut below is only valid for a single layer"
    bsz, seq, d = x.shape
    assert seq % ROW_TILE == 0 and seq % ML_CHUNK == 0 and seq % Q_TILE == 0
    assert Q_TILE == GT_CHUNK and ML_CHUNK == GT_CHUNK and KEY_CHUNK == GT_CHUNK and ROW_TILE % GT_CHUNK == 0
    alpha = (2 * depth) ** 0.25
    topk = min(TOPK_MAX, seq // 4)

    row2 = lambda p: p[0].reshape(1, -1).astype(F32)
    bf = lambda w: w[0].astype(BF16)

    w_t = jnp.swapaxes(w_in[0], 0, 1)
    sizes = (ATT_WIDTH, KV_LATENT, IDX_WIDTH, IDX_DIM, IDX_HEADS, MLQK_WIDTH, ML_WIDTH, ML_WIDTH, ML_HEADS, ML_HEADS)
    offs = [sum(sizes[:n]) for n in range(len(sizes) + 1)]
    assert w_t.shape[0] == offs[-1]
    wa = jnp.concatenate([w_t[offs[0]:offs[4]], w_t[offs[4]:offs[5]], w_t[offs[8]:offs[10]],
                          jnp.zeros((LANES - GATE_END, d), F32)], axis=0).astype(BF16)
    wm = w_t[offs[5]:offs[8]].astype(BF16)
    gbias = jnp.concatenate([jnp.zeros((GATE_I0,), F32), b_igate[0], b_fgate[0],
                             jnp.zeros((LANES - GATE_END,), F32)]).reshape(1, LANES)
    wuk_bd = jnp.stack([_block_diag(w_uk[0][2 * p:2 * p + 2]) for p in range(HEAD_PAIRS)]).astype(BF16)
    wuv_bd = jnp.stack([_block_diag(w_uv[0][2 * p:2 * p + 2]) for p in range(HEAD_PAIRS)]).astype(BF16)
    kvg = row2(kv_norm_g)
    convw = conv_w[0].astype(F32)

    h1, h1_meta = _ffn_ln_call(x.reshape(bsz * seq, d), meta_tokens.astype(F32), ffn1_w_gate[0], ffn1_w_up[0],
                               ffn1_w_down[0], row2(ln1_g), row2(ln1_b), alpha, FFN_TILE)
    zero_tail = jnp.zeros((CONV_HIST, MLQK_WIDTH), F32)
    (_, m_ckv, _, m_kidx, m_qk, m_v, _, m_gates, m_tail) = _inproj_call(
        h1_meta[None], zero_tail, wa, wm, wuk_bd, kvg, convw, gbias, N_META)

    (_, ckv, _, kidx, qk, _, og, gates, _, gates_t, ckv_t, v_t, q_t, qlat_t, qidx_t) = _inproj_call(
        h1.reshape(bsz, seq, d), m_tail[0], wa, wm, wuk_bd, kvg, convw, gbias, ROW_TILE)

    nchunks = seq // KEY_CHUNK
    ckv_c = ckv.reshape(bsz, nchunks, KEY_CHUNK, KV_LATENT)
    kidx_c = kidx.reshape(bsz, nchunks, KEY_CHUNK, IDX_DIM)
    ones_rows = jnp.zeros((ATT_EXT - KV_LATENT, N_META), BF16).at[0].set(1.0)
    m_ckvt = jnp.concatenate([m_ckv[0].T, ones_rows], axis=0)
    y_att = _dsa_call(qlat_t, qidx_t, gates_t, ckv_c, ckv_t, kidx_c,
                      m_ckv[0], m_ckvt, m_kidx[0], wuv_bd, topk)

    lane = jnp.arange(LANES)
    pad_gate = jnp.where((lane >= GATE_I0) & (lane < GATE_F0), NEG_BIG, 0.0).astype(F32)
    mg = jnp.concatenate([m_gates[0], jnp.broadcast_to(pad_gate, (ML_CHUNK - N_META, LANES))], axis=0)
    gate_lanes = slice(GATE_W0, GATE_W0 + GATE_ROWS)
    m_qk_pad = _pad_rows(m_qk[0], ML_CHUNK)
    y_ml = _mlstm_call(qk, q_t, v_t, og, gates, gates_t,
                       m_qk_pad, m_qk_pad[:, :MLQK_WIDTH // 2].T, _pad_rows(m_v[0], ML_CHUNK).T, mg, mg[:, gate_lanes].T,
                       row2(ml_norm_g))

    out = _out_ffn_ln_call(
        y_att.reshape(bsz * seq, ATT_WIDTH), y_ml.reshape(bsz * seq, ML_WIDTH), h1, bf(w_out),
        row2(ln2_g), row2(ln2_b), ffn2_w_gate[0], ffn2_w_up[0], ffn2_w_down[0], row2(ln3_g), row2(ln3_b),
        alpha, FFN_TILE)
    return out.reshape(bsz, seq, d)
```

```python
import functools

import jax
import jax.numpy as jnp
from jax import lax
from jax.experimental import pallas as pl
from jax.experimental.pallas import tpu as pltpu

F32 = jnp.float32
BF16 = jnp.bfloat16

N_META = 16
ATT_HEADS = 8
ATT_HEAD_DIM = 64
KV_LATENT = 128
IDX_HEADS = 4
IDX_DIM = 64
TOPK_MAX = 256
ML_HEADS = 4
ML_V_DIM = 128
ML_QK_DIM = 64
CONV_WIDTH = 4
GATE_SOFTCAP = 15.0
M_INIT = -1e30
LN_EPS = 1e-5
NEG_BIG = -1e30
LOG2_E = 1.4426950408889634

LANES = 128
SUBLANES = 8
VMEM_BYTES_V7X = 64 * 1024 * 1024
VMEM_LIMIT_BYTES = VMEM_BYTES_V7X * 7 // 8

FF_CHUNK = 256
ROW_TILE = 512
FFN_TILE = 512
Q_TILE = 256
ATT_GROUP = 128
KEY_CHUNK = 256
N_BISECT = 16
REDUCE_ROWS = 32
ML_BATCH = 2
ML_CHUNK = 256


def _dot(a, b):
    return jnp.dot(a, b, preferred_element_type=F32)


def _layer_norm(z, g, b):
    mu = jnp.mean(z, axis=-1, keepdims=True)
    zc = z - mu
    var = jnp.mean(zc * zc, axis=-1, keepdims=True)
    return zc * lax.rsqrt(var + LN_EPS) * g + b


def _sigmoid(x):
    return 1.0 / (1.0 + jnp.exp(-x))


def _swiglu_chunk(xb, wg_c, wu_c, wd_c):
    g = _dot(xb, wg_c)
    u = _dot(xb, wu_c)
    return _dot((g * _sigmoid(g) * u).astype(BF16), wd_c)


def _ffn_ln(x, wg_s, wu_s, wd_s, g, b, alpha):
    xb = x.astype(BF16)
    acc = jnp.zeros(x.shape, F32)
    for c in range(wg_s.shape[0]):
        acc = acc + _swiglu_chunk(xb, wg_s[c], wu_s[c], wd_s[c])
    return _layer_norm(alpha * x + 0.5 * acc, g, b)


def _stage_ffn_weights(step, wg_ref, wu_ref, wd_ref, wg_s, wu_s, wd_s):
    wg_s[step] = wg_ref[...].astype(BF16)
    wu_s[step] = wu_ref[...].astype(BF16)
    wd_s[step] = wd_ref[...].astype(BF16)


def _ffn_weight_specs(d, d_ff):
    n = d_ff // FF_CHUNK
    col = pl.BlockSpec((d, FF_CHUNK), lambda i: (0, jnp.minimum(i, n - 1)))
    row = pl.BlockSpec((FF_CHUNK, d), lambda i: (jnp.minimum(i, n - 1), 0))
    scratch = [pltpu.VMEM((n, d, FF_CHUNK), BF16), pltpu.VMEM((n, d, FF_CHUNK), BF16),
               pltpu.VMEM((n, FF_CHUNK, d), BF16)]
    return n, [col, col, row], scratch


def _staged_tile_index(i, n_stage):
    return jnp.maximum(i - (n_stage - 1), 0)


def _ffn_ln_kernel(x_ref, meta_ref, wg_ref, wu_ref, wd_ref, g_ref, b_ref, o_ref, ometa_ref,
                   wg_s, wu_s, wd_s, macc_ref, acc0_ref, *, alpha, n_stage):
    i = pl.program_id(0)

    @pl.when(i == 0)
    def _():
        macc_ref[...] = jnp.zeros(macc_ref.shape, F32)
        acc0_ref[...] = jnp.zeros(acc0_ref.shape, F32)

    @pl.when(i < n_stage)
    def _():
        _stage_ffn_weights(i, wg_ref, wu_ref, wd_ref, wg_s, wu_s, wd_s)
        macc_ref[...] += _swiglu_chunk(meta_ref[...].astype(BF16), wg_s[i], wu_s[i], wd_s[i])
        acc0_ref[...] += _swiglu_chunk(x_ref[...].astype(BF16), wg_s[i], wu_s[i], wd_s[i])

    @pl.when(i == n_stage - 1)
    def _():
        ometa_ref[...] = _layer_norm(alpha * meta_ref[...] + 0.5 * macc_ref[...], g_ref[...], b_ref[...])
        o_ref[...] = _layer_norm(alpha * x_ref[...] + 0.5 * acc0_ref[...], g_ref[...], b_ref[...])

    @pl.when(i >= n_stage)
    def _():
        o_ref[...] = _ffn_ln(x_ref[...], wg_s, wu_s, wd_s, g_ref[...], b_ref[...], alpha)


def _resident(shape):
    return pl.BlockSpec(shape, lambda *_: (0,) * len(shape), pipeline_mode=pl.Buffered(1))


def _ffn_ln_call(x, meta, wg, wu, wd, g, b, alpha, tm):
    rows, d = x.shape
    n_stage, w_specs, w_scratch = _ffn_weight_specs(d, wg.shape[1])
    row_spec = pl.BlockSpec((tm, d), lambda i: (_staged_tile_index(i, n_stage), 0))
    return pl.pallas_call(
        functools.partial(_ffn_ln_kernel, alpha=alpha, n_stage=n_stage),
        grid=(n_stage - 1 + rows // tm,),
        in_specs=[row_spec, _resident(meta.shape)] + w_specs + [_resident(g.shape), _resident(b.shape)],
        out_specs=[row_spec, pl.BlockSpec(meta.shape, lambda i: (0, 0))],
        out_shape=[jax.ShapeDtypeStruct((rows, d), F32), jax.ShapeDtypeStruct(meta.shape, F32)],
        scratch_shapes=w_scratch + [pltpu.VMEM(meta.shape, F32), pltpu.VMEM((tm, d), F32)],
        compiler_params=pltpu.CompilerParams(
            dimension_semantics=("arbitrary",), vmem_limit_bytes=VMEM_LIMIT_BYTES),
        name="ffn_ln",
    )(x, meta, wg, wu, wd, g, b)


ATT_WIDTH = ATT_HEADS * ATT_HEAD_DIM
IDX_WIDTH = IDX_HEADS * IDX_DIM
MLQK_WIDTH = 2 * ML_HEADS * ML_QK_DIM
ML_WIDTH = ML_HEADS * ML_V_DIM
LAT_WIDTH = ATT_HEADS * KV_LATENT
CONV_HIST = SUBLANES
GATE_W0 = IDX_DIM
GATE_I0, GATE_F0, GATE_END = GATE_W0 + IDX_HEADS, GATE_W0 + IDX_HEADS + ML_HEADS, GATE_W0 + IDX_HEADS + 2 * ML_HEADS
GATE_ROWS = 2 * SUBLANES
GT_CHUNK = 256
ATT_EXT = KV_LATENT + 2 * SUBLANES
HEAD_PAIRS = ATT_HEADS // 2


def _inproj_kernel(h_ref, tail_ref, wa_ref, wm_ref, wuk_ref, kvg_ref, convw_ref, gbias_ref,
                   ckv_ref, kidx_ref, qk_ref, v_ref, og_ref, gates_ref, tailout_ref, *rest):
    carry_ref, wa_s, wm_s = rest[-3:]
    tm = h_ref.shape[1]

    @pl.when((pl.program_id(0) == 0) & (pl.program_id(1) == 0))
    def _():
        wa_s[...] = wa_ref[...].astype(F32).T.astype(BF16)
        wm_s[...] = wm_ref[...].astype(F32).T.astype(BF16)

    @pl.when(pl.program_id(1) == 0)
    def _():
        carry_ref[...] = tail_ref[...]

    xb = h_ref[0].astype(BF16)

    pa = _dot(xb, wa_s[...])
    q_a = pa[:, :ATT_WIDTH].astype(BF16)
    c0 = ATT_WIDTH
    ckv = pa[:, c0:c0 + KV_LATENT]
    c1 = c0 + KV_LATENT
    ckv = ckv * lax.rsqrt(jnp.mean(ckv * ckv, axis=-1, keepdims=True) + LN_EPS) * kvg_ref[...]
    ckv_ref[0] = ckv.astype(BF16)
    c2 = c1 + IDX_WIDTH
    kidx_ref[0] = pa[:, c2:c2 + IDX_DIM].astype(BF16)
    pair_in = 2 * ATT_HEAD_DIM
    q_lat = [_dot(q_a[:, p * pair_in:(p + 1) * pair_in], wuk_ref[p]) * (ATT_HEAD_DIM ** -0.5 * LOG2_E)
             for p in range(HEAD_PAIRS)]

    pm = _dot(xb, wm_s[...])
    qk_raw = pm[:, :MLQK_WIDTH]
    v_ref[0] = pm[:, MLQK_WIDTH:MLQK_WIDTH + ML_WIDTH].astype(BF16)
    og_ref[0] = _sigmoid(pm[:, MLQK_WIDTH + ML_WIDTH:])

    ext = jnp.concatenate([carry_ref[...], qk_raw], axis=0)
    cw = convw_ref[...]
    conv = jnp.zeros_like(qk_raw)
    for j in range(CONV_WIDTH):
        s0 = CONV_HIST - (CONV_WIDTH - 1) + j
        conv = conv + ext[s0:s0 + tm] * cw[j:j + 1]
    act = conv * _sigmoid(conv)
    half = MLQK_WIDTH // 2
    qk_ref[0, :, :half] = act[:, :half].astype(BF16)
    qk_ref[0, :, half:] = (act[:, half:] * (ML_QK_DIM ** -0.5)).astype(BF16)
    carry_ref[...] = qk_raw[tm - CONV_HIST:]
    tailout_ref[0] = qk_raw[tm - CONV_HIST:]

    gr = pa[:, c2 + IDX_DIM - GATE_W0:]
    lane = lax.broadcasted_iota(jnp.int32, gr.shape, 1)
    sc = GATE_SOFTCAP * jnp.tanh((gr + gbias_ref[...]) / GATE_SOFTCAP)
    lf = -(jnp.maximum(-sc, 0.0) + jnp.log1p(jnp.exp(-jnp.abs(sc))))
    w_scaled = gr * (IDX_HEADS ** -0.5 * IDX_DIM ** -0.5)
    gates = jnp.where((lane < GATE_W0) | (lane >= GATE_END), 0.0,
                      jnp.where(lane < GATE_I0, w_scaled, jnp.where(lane < GATE_F0, sc, lf)))
    gates_ref[0] = gates
    if len(rest) == 9:
        gt_ref, ckvt_ref, vt_ref, qt_ref, qlatt_ref, qidxt_ref = rest[:6]
        q_idx_t = pa[:, c1:c1 + IDX_WIDTH].T.astype(BF16)
        q_lat_t = jnp.concatenate([ql.T for ql in q_lat], axis=0).astype(BF16)
        gates_t = gates.T[GATE_W0:GATE_W0 + GATE_ROWS]
        ckv_t = ckv.T.astype(BF16)
        v_t = pm[:, MLQK_WIDTH:MLQK_WIDTH + ML_WIDTH].T.astype(BF16)
        q_t = act[:, :half].T.astype(BF16)
        ones_blk = jnp.where(lax.broadcasted_iota(jnp.int32, (ATT_EXT - KV_LATENT, GT_CHUNK), 0) == 0,
                             1.0, 0.0).astype(BF16)
        for j in range(tm // GT_CHUNK):
            piece = slice(j * GT_CHUNK, (j + 1) * GT_CHUNK)
            gt_ref[0, j] = gates_t[:, piece]
            ckvt_ref[0, j] = jnp.concatenate([ckv_t[:, piece], ones_blk], axis=0)
            vt_ref[0, j] = v_t[:, piece]
            qt_ref[0, j] = q_t[:, piece]
            qlatt_ref[0, j] = q_lat_t[:, piece]
            qidxt_ref[0, j] = q_idx_t[:, piece]


def _inproj_call(h, tail, wa, wm, wuk_bd, kvg, convw, gbias, tm):
    bn, rows, d = h.shape
    nblk = rows // tm
    emit_gt = tm % GT_CHUNK == 0

    def row_spec(width):
        return pl.BlockSpec((1, tm, width), lambda b, j: (b, j, 0))

    outs = [
        (KV_LATENT, BF16), (IDX_DIM, BF16),
        (MLQK_WIDTH, BF16), (ML_WIDTH, BF16), (ML_WIDTH, F32), (LANES, F32),
    ]
    out_shape = [jax.ShapeDtypeStruct((bn, rows, w), dt) for w, dt in outs]
    out_specs = [row_spec(w) for w, _ in outs]
    out_shape.append(jax.ShapeDtypeStruct((bn, CONV_HIST, MLQK_WIDTH), F32))
    out_specs.append(pl.BlockSpec((1, CONV_HIST, MLQK_WIDTH), lambda b, j: (b, 0, 0)))
    if emit_gt:
        per_tile = tm // GT_CHUNK
        for height, dt in ((GATE_ROWS, F32), (ATT_EXT, BF16), (ML_WIDTH, BF16), (MLQK_WIDTH // 2, BF16),
                           (LAT_WIDTH, BF16), (IDX_WIDTH, BF16)):
            out_shape.append(jax.ShapeDtypeStruct((bn, rows // GT_CHUNK, height, GT_CHUNK), dt))
            out_specs.append(pl.BlockSpec((1, per_tile, height, GT_CHUNK), lambda b, j: (b, j, 0, 0)))
    return pl.pallas_call(
        _inproj_kernel,
        grid=(bn, nblk),
        in_specs=[
            row_spec(d),
            _resident(tail.shape), _resident(wa.shape), _resident(wm.shape),
            _resident(wuk_bd.shape), _resident(kvg.shape), _resident(convw.shape), _resident(gbias.shape),
        ],
        out_specs=out_specs,
        out_shape=out_shape,
        scratch_shapes=[pltpu.VMEM((CONV_HIST, MLQK_WIDTH), F32),
                        pltpu.VMEM(wa.shape[::-1], BF16), pltpu.VMEM(wm.shape[::-1], BF16)],
        compiler_params=pltpu.CompilerParams(
            dimension_semantics=("arbitrary", "arbitrary"), vmem_limit_bytes=VMEM_LIMIT_BYTES),
        name="in_proj",
    )(h, tail, wa, wm, wuk_bd, kvg, convw, gbias)


def _dsa_kernel(qlat_ref, qidx_ref, wrow_ref, ckv_ref, ckvt_ref, kidx_ref, mckv_ref, mckvt_ref, mkidx_ref,
                wuv_ref, y_ref, s_ref, acc_ref, p_ref, *, topk):
    _, kc, tq = s_ref.shape
    i = pl.program_id(1)
    nch = ((i + 1) * tq + kc - 1) // kc
    qreal = i * tq + lax.broadcasted_iota(jnp.int32, (1, tq), 1)
    kf = float(topk)

    wrow = wrow_ref[0, 0]
    q_idx_t = qidx_ref[0, 0]
    q_idx_all_t = jnp.concatenate([q_idx_t[h * IDX_DIM:(h + 1) * IDX_DIM] for h in range(IDX_HEADS)],
                                  axis=1)
    wi = [wrow[h:h + 1, :] for h in range(IDX_HEADS)]

    def scores(k_rows):
        lg = _dot(k_rows, q_idx_all_t)
        sc = jnp.zeros((k_rows.shape[0], tq), F32)
        for h in range(IDX_HEADS):
            sc = sc + jnp.maximum(lg[:, h * tq:(h + 1) * tq], 0.0) * wi[h]
        return sc

    s_meta = scores(mkidx_ref[...])

    def score_chunk(c, lo, hi):
        sc = scores(kidx_ref[0, c])
        valid = c * kc + lax.broadcasted_iota(jnp.int32, (kc, tq), 0) <= qreal
        s_ref[c] = jnp.where(valid, sc, -jnp.inf)
        groups = (kc // REDUCE_ROWS, REDUCE_ROWS, tq)
        lo = jnp.minimum(lo, jnp.min(sc.reshape(groups), axis=0))
        hi = jnp.maximum(hi, jnp.max(sc.reshape(groups), axis=0))
        return lo, hi

    def score_pair(c2, carry):
        lo, hi = score_chunk(2 * c2, *carry)
        return score_chunk(jnp.minimum(2 * c2 + 1, nch - 1), lo, hi)

    lo, hi = lax.fori_loop(0, (nch + 1) // 2, score_pair,
                           (jnp.full((REDUCE_ROWS, tq), jnp.inf, F32), jnp.full((REDUCE_ROWS, tq), -jnp.inf, F32)))
    lo = jnp.minimum(jnp.min(lo, axis=0, keepdims=True), jnp.min(s_meta, axis=0, keepdims=True))
    hi = jnp.maximum(jnp.max(hi, axis=0, keepdims=True), jnp.max(s_meta, axis=0, keepdims=True))

    def key_reduce(reduce, combine, per_chunk, init):
        def body(c, acc):
            x = per_chunk(s_ref[c]).reshape(kc // REDUCE_ROWS, REDUCE_ROWS, tq)
            return combine(acc, reduce(x, axis=0))
        acc = lax.fori_loop(0, nch, body, jnp.full((REDUCE_ROWS, tq), init, F32))
        return combine(reduce(acc, axis=0, keepdims=True), reduce(per_chunk(s_meta), axis=0, keepdims=True))

    def count(pred):
        return key_reduce(jnp.sum, jnp.add, lambda sc: jnp.where(pred(sc), 1.0, 0.0), 0.0)

    def max_where(pred):
        return key_reduce(jnp.max, jnp.maximum, lambda sc: jnp.where(pred(sc), sc, -jnp.inf), -jnp.inf)

    def bisect(_, carry):
        lo, hi = carry
        mid = 0.5 * lo + 0.5 * hi
        up = count(lambda sc: sc > mid) >= kf
        return jnp.where(up, mid, lo), jnp.where(up, hi, mid)

    lo, hi = lax.fori_loop(0, N_BISECT, bisect, (lo, hi))

    n_valid = (qreal + (N_META + 1)).astype(F32)
    small = n_valid <= kf
    cand = max_where(lambda sc: sc <= hi)
    n_ge = count(lambda sc: sc >= cand)
    done = jnp.where(small | (n_ge >= kf), 1.0, 0.0)

    def not_finished(state):
        return jnp.min(state[1]) < 0.5

    def step_down(state):
        cand, done, _ = state
        nxt = jnp.where(done > 0.5, cand, max_where(lambda sc: sc < cand))
        n_ge = count(lambda sc: sc >= nxt)
        return nxt, jnp.where(n_ge >= kf, 1.0, done), n_ge

    cand, _, n_ge = lax.while_loop(not_finished, step_down, step_down((cand, done, n_ge)))
    thr = jnp.where(small, -jnp.inf, cand)
    n_eq = count(lambda sc: sc == thr)
    need = jnp.where(small, 0.0, kf - (n_ge - n_eq))
    ranked_ties = jnp.max(jnp.where(n_eq > need, 1.0, 0.0)) > 0.5

    qlat_t = qlat_ref[0, 0]
    n_groups = tq // ATT_GROUP
    onehot = (lax.broadcasted_iota(jnp.int32, (ATT_GROUP, ATT_GROUP), 0)
              == lax.broadcasted_iota(jnp.int32, (ATT_GROUP, ATT_GROUP), 1)).astype(BF16)
    q_aug_t = [jnp.concatenate(
        [jnp.concatenate([qlat_t[h * KV_LATENT:(h + 1) * KV_LATENT, g * ATT_GROUP:(g + 1) * ATT_GROUP]
                          for h in range(ATT_HEADS)], axis=1),
         jnp.concatenate([onehot] * ATT_HEADS, axis=1)], axis=0)
        for g in range(n_groups)]
    hq = ATT_HEADS * ATT_GROUP

    def lower_tri(n):
        return (lax.broadcasted_iota(jnp.int32, (n, n), 1) <= lax.broadcasted_iota(jnp.int32, (n, n), 0)).astype(BF16)

    def attention(ranked):
        def mask_bias(sc, eq_seen):
            if not ranked:
                return jnp.where(sc >= thr, 0.0, NEG_BIG).astype(BF16), eq_seen
            n = sc.shape[0]
            eq = sc == thr
            rank = _dot(lower_tri(n), jnp.where(eq, 1.0, 0.0).astype(BF16)) + eq_seen
            keep = (sc > thr) | (eq & (rank <= need))
            return jnp.where(keep, 0.0, NEG_BIG).astype(BF16), rank[n - 1:n, :]

        def logits(g, kv, bias):
            k_aug = jnp.concatenate([kv, bias[:, g * ATT_GROUP:(g + 1) * ATT_GROUP]], axis=1)
            return _dot(k_aug, q_aug_t[g])

        def fold_in(g, c_prev, a_prev):
            acc_ref[g] = a_prev * acc_ref[g] + _dot(ckvt_ref[0, c_prev], p_ref[g])

        def attend(c, carry):
            bias, eq_seen = mask_bias(s_ref[c], carry[0])
            out = [eq_seen]
            for g in range(n_groups):
                m, a_prev = carry[1 + 2 * g:3 + 2 * g]
                fold_in(g, c - 1, a_prev)
                s = logits(g, ckv_ref[0, c], bias)
                m_new = jnp.maximum(m, jnp.max(s, axis=0, keepdims=True))
                p_ref[g] = jnp.exp2(s - m_new).astype(BF16)
                out += [m_new, jnp.exp2(m - m_new)]
            return tuple(out)

        bias_m, eq_seen = mask_bias(s_meta, jnp.zeros((1, tq), F32))
        bias_0, eq_seen = mask_bias(s_ref[0], eq_seen)
        kv_first = jnp.concatenate([mckv_ref[...], ckv_ref[0, 0]], axis=0)
        bias_first = jnp.concatenate([bias_m, bias_0], axis=0)
        init = [eq_seen]
        for g in range(n_groups):
            s = logits(g, kv_first, bias_first)
            m = jnp.max(s, axis=0, keepdims=True)
            p = jnp.exp2(s - m).astype(BF16)
            acc_ref[g] = _dot(mckvt_ref[...], p[:N_META])
            p_ref[g] = p[N_META:]
            init += [m, jnp.ones((1, hq), F32)]
        carry = lax.fori_loop(1, nch, attend, tuple(init))
        rows = []
        for g in range(n_groups):
            fold_in(g, nch - 1, carry[2 + 2 * g])
            acc = acc_ref[g]
            o_t = (acc[:KV_LATENT] / acc[KV_LATENT:KV_LATENT + 1]).T
            rows.append(jnp.concatenate([o_t[h * ATT_GROUP:(h + 1) * ATT_GROUP] for h in range(ATT_HEADS)], axis=1))
        o_all = jnp.concatenate(rows, axis=0).astype(BF16)
        pair = 2 * KV_LATENT
        return jnp.concatenate([_dot(o_all[:, p * pair:(p + 1) * pair], wuv_ref[p]) for p in range(HEAD_PAIRS)],
                               axis=1).astype(BF16)

    y_ref[0] = lax.cond(ranked_ties, lambda: attention(True), lambda: attention(False))


def _dsa_call(qlat_t, qidx_t, wrow, ckv_c, ckvt_c, kidx_c, m_ckv, m_ckvt, m_kidx, wuv_bd, topk):
    bn, nq = qidx_t.shape[:2]
    rows = nq * Q_TILE
    nchunks, kc = ckv_c.shape[1], ckv_c.shape[2]

    def tile_spec(height):
        return pl.BlockSpec((1, 1, height, Q_TILE), lambda b, i: (b, i, 0, 0))

    def q_spec(width):
        return pl.BlockSpec((1, Q_TILE, width), lambda b, i: (b, i, 0))

    def k_spec(a):
        return pl.BlockSpec((1,) + a.shape[1:], lambda b, i: (b, 0, 0, 0))

    return pl.pallas_call(
        functools.partial(_dsa_kernel, topk=topk),
        grid=(bn, nq),
        in_specs=[tile_spec(LAT_WIDTH), tile_spec(IDX_WIDTH), tile_spec(GATE_ROWS),
                  k_spec(ckv_c), k_spec(ckvt_c), k_spec(kidx_c),
                  _resident(m_ckv.shape), _resident(m_ckvt.shape), _resident(m_kidx.shape), _resident(wuv_bd.shape)],
        out_specs=q_spec(ATT_WIDTH),
        out_shape=jax.ShapeDtypeStruct((bn, rows, ATT_WIDTH), BF16),
        scratch_shapes=[pltpu.VMEM((nchunks, kc, Q_TILE), F32),
                        pltpu.VMEM((Q_TILE // ATT_GROUP, ckvt_c.shape[2], ATT_HEADS * ATT_GROUP), F32),
                        pltpu.VMEM((Q_TILE // ATT_GROUP, kc, ATT_HEADS * ATT_GROUP), BF16)],
        compiler_params=pltpu.CompilerParams(
            dimension_semantics=("arbitrary", "arbitrary"), vmem_limit_bytes=VMEM_LIMIT_BYTES),
        name="dsa",
    )(qlat_t, qidx_t, wrow, ckv_c, ckvt_c, kidx_c, m_ckv, m_ckvt, m_kidx, wuv_bd)


def _split3(x):
    hi = x.astype(BF16)
    r = x - hi.astype(F32)
    mid = r.astype(BF16)
    lo = (r - mid.astype(F32)).astype(BF16)
    return hi, mid, lo


ML_EXT = ML_V_DIM + 16


def _mlstm_chunk(qk, qt, vt, g, gt, state):
    L = qk.shape[0]
    s_idx = lax.broadcasted_iota(jnp.int32, (L, L), 0)
    t_idx = lax.broadcasted_iota(jnp.int32, (L, L), 1)
    causal = s_idx <= t_idx
    w = g.shape[1]
    cols3 = _dot((t_idx <= s_idx).astype(BF16), jnp.concatenate(_split3(g), axis=1))
    b_cols = cols3[:, :w] + cols3[:, w:2 * w] + cols3[:, 2 * w:]
    r = gt.shape[0]
    rows3 = _dot(jnp.concatenate(_split3(gt), axis=0), causal.astype(BF16))
    b_rows = rows3[:r] + rows3[r:2 * r] + rows3[2 * r:]
    ones_blk = jnp.where(lax.broadcasted_iota(jnp.int32, (ML_EXT - ML_V_DIM, L), 0) == 0, 1.0, 0.0).astype(BF16)
    kq = ML_HEADS * ML_QK_DIM

    outs, new_state = [], []
    for h in range(ML_HEADS):
        ce, m_prev = state[h]
        c_col = g[:, GATE_I0 + h:GATE_I0 + h + 1] - b_cols[:, GATE_F0 + h:GATE_F0 + h + 1]
        b_row = b_rows[GATE_F0 - GATE_W0 + h:GATE_F0 - GATE_W0 + h + 1, :]
        ig_row = gt[GATE_I0 - GATE_W0 + h:GATE_I0 - GATE_W0 + h + 1, :]
        qt_h = qt[h * ML_QK_DIM:(h + 1) * ML_QK_DIM, :]
        kh = qk[:, kq + h * ML_QK_DIM:kq + (h + 1) * ML_QK_DIM]
        vt_ext = jnp.concatenate([vt[h * ML_V_DIM:(h + 1) * ML_V_DIM, :], ones_blk], axis=0)

        d_t = jnp.where(causal, c_col + b_row, -jnp.inf)
        inter = b_row + m_prev
        m_t = jnp.maximum(jnp.max(d_t, axis=0, keepdims=True), inter)
        w_inter = jnp.exp(inter - m_t)
        s_t = _dot(kh, qt_h) * jnp.exp(d_t - m_t)
        r = _dot(vt_ext, s_t.astype(BF16)) + _dot(ce.astype(BF16), qt_h) * w_inter
        num = r[:ML_V_DIM]
        den = r[ML_V_DIM:ML_V_DIM + 1]
        hh = num / jnp.maximum(jnp.abs(den), jnp.exp(-m_t))
        mu = jnp.mean(hh, axis=0, keepdims=True)
        hc = hh - mu
        var = jnp.mean(hc * hc, axis=0, keepdims=True)
        outs.append((hc * lax.rsqrt(var + LN_EPS)).T)

        b_end = b_row[:, L - 1:L]
        g_row = b_end - b_row + ig_row
        m_new = jnp.maximum(b_end + m_prev, jnp.max(g_row, axis=1, keepdims=True))
        decay = jnp.exp(b_end + m_prev - m_new)
        weighted = (vt_ext.astype(F32) * jnp.exp(g_row - m_new)).astype(BF16)
        new_state.append((decay * ce + _dot(weighted, kh), m_new))
    return outs, new_state


def _mlstm_kernel(qk_ref, qt_ref, vt_ref, og_ref, gates_ref, gt_ref, mqk_ref, mqt_ref, mvt_ref, mgates_ref, mgt_ref,
                  ng_ref, y_ref,
                  ce0_ref, m0_ref):
    L = mqk_ref.shape[0]
    n_chunks = qk_ref.shape[1] // L
    norm_g = ng_ref[...]

    @pl.when(pl.program_id(0) == 0)
    def _():
        state = [(jnp.zeros((ML_EXT, ML_QK_DIM), F32), jnp.full((1, 1), M_INIT, F32)) for _ in range(ML_HEADS)]
        _, state = _mlstm_chunk(mqk_ref[...], mqt_ref[...], mvt_ref[...], mgates_ref[...], mgt_ref[...], state)
        for h in range(ML_HEADS):
            ce0_ref[h] = state[h][0]
            m0_ref[h] = jnp.broadcast_to(state[h][1], m0_ref.shape[1:])

    n_b = qk_ref.shape[0]
    state = [(ce0_ref[h], m0_ref[h][0:1, 0:1]) for h in range(ML_HEADS)] * n_b

    def body(c, flat):
        rows = pl.ds(pl.multiple_of(c * L, L), L)
        new_flat = []
        for b in range(n_b):
            state = [(flat[2 * (b * ML_HEADS + h)], flat[2 * (b * ML_HEADS + h) + 1]) for h in range(ML_HEADS)]
            outs, state = _mlstm_chunk(qk_ref[b, rows, :], qt_ref[b, c], vt_ref[b, c], gates_ref[b, rows, :],
                                       gt_ref[b, c], state)
            y = jnp.concatenate(outs, axis=1) * norm_g * og_ref[b, rows, :]
            y_ref[b, rows, :] = y.astype(BF16)
            new_flat += [x for pair in state for x in pair]
        return tuple(new_flat)

    lax.fori_loop(0, n_chunks, body, tuple(x for pair in state for x in pair))


def _mlstm_call(qk, qt_c, vt_c, og, gates, gt_c, mqk, mqt, mvt, mgates, mgt, norm_g):
    bn, rows, _ = qk.shape
    nb = ML_BATCH if bn % ML_BATCH == 0 else 1

    def b_spec(a):
        return pl.BlockSpec((nb,) + a.shape[1:], lambda b: (b,) + (0,) * (a.ndim - 1))

    consts = (mqk, mqt, mvt, mgates, mgt, norm_g)
    return pl.pallas_call(
        _mlstm_kernel,
        grid=(bn // nb,),
        in_specs=[b_spec(qk), b_spec(qt_c), b_spec(vt_c), b_spec(og), b_spec(gates), b_spec(gt_c)]
        + [_resident(c.shape) for c in consts],
        out_specs=pl.BlockSpec((nb, rows, ML_WIDTH), lambda b: (b, 0, 0)),
        out_shape=jax.ShapeDtypeStruct((bn, rows, ML_WIDTH), BF16),
        scratch_shapes=[pltpu.VMEM((ML_HEADS, ML_EXT, ML_QK_DIM), F32), pltpu.VMEM((ML_HEADS, SUBLANES, LANES), F32)],
        compiler_params=pltpu.CompilerParams(
            dimension_semantics=("arbitrary",), vmem_limit_bytes=VMEM_LIMIT_BYTES),
        name="mlstm",
    )(qk, qt_c, vt_c, og, gates, gt_c, *consts)


def _out_ffn_ln_kernel(ya_ref, ym_ref, h_ref, wo_ref, g2_ref, b2_ref, wg_ref, wu_ref, wd_ref, g3_ref, b3_ref,
                       o_ref, wg_s, wu_s, wd_s, h2_ref, acc0_ref, *, alpha, n_stage):
    i = pl.program_id(0)

    def mixed():
        mix = _dot(ya_ref[...], wo_ref[:ATT_WIDTH, :]) + _dot(ym_ref[...], wo_ref[ATT_WIDTH:, :])
        return _layer_norm(alpha * h_ref[...] + mix, g2_ref[...], b2_ref[...])

    @pl.when(i == 0)
    def _():
        h2_ref[...] = mixed()
        acc0_ref[...] = jnp.zeros(acc0_ref.shape, F32)

    @pl.when(i < n_stage)
    def _():
        _stage_ffn_weights(i, wg_ref, wu_ref, wd_ref, wg_s, wu_s, wd_s)
        acc0_ref[...] += _swiglu_chunk(h2_ref[...].astype(BF16), wg_s[i], wu_s[i], wd_s[i])

    @pl.when(i == n_stage - 1)
    def _():
        o_ref[...] = _layer_norm(alpha * h2_ref[...] + 0.5 * acc0_ref[...], g3_ref[...], b3_ref[...])

    @pl.when(i >= n_stage)
    def _():
        o_ref[...] = _ffn_ln(mixed(), wg_s, wu_s, wd_s, g3_ref[...], b3_ref[...], alpha)


def _out_ffn_ln_call(ya, ym, h, wo, g2, b2, wg, wu, wd, g3, b3, alpha, tm):
    rows, d = h.shape
    n_stage, w_specs, w_scratch = _ffn_weight_specs(d, wg.shape[1])

    def row_spec(width):
        return pl.BlockSpec((tm, width), lambda i: (_staged_tile_index(i, n_stage), 0))

    return pl.pallas_call(
        functools.partial(_out_ffn_ln_kernel, alpha=alpha, n_stage=n_stage),
        grid=(n_stage - 1 + rows // tm,),
        in_specs=[row_spec(ya.shape[1]), row_spec(ym.shape[1]), row_spec(d),
                  _resident(wo.shape), _resident(g2.shape), _resident(b2.shape)] + w_specs
        + [_resident(g3.shape), _resident(b3.shape)],
        out_specs=row_spec(d),
        out_shape=jax.ShapeDtypeStruct((rows, d), F32),
        scratch_shapes=w_scratch + [pltpu.VMEM((tm, d), F32), pltpu.VMEM((tm, d), F32)],
        compiler_params=pltpu.CompilerParams(
            dimension_semantics=("arbitrary",), vmem_limit_bytes=VMEM_LIMIT_BYTES),
        name="out_ffn_ln",
    )(ya, ym, h, wo, g2, b2, wg, wu, wd, g3, b3)


def _block_diag(w):
    nh, a, b = w.shape
    eye = jnp.eye(nh, dtype=w.dtype)
    return (eye[:, None, :, None] * w[:, :, None, :]).reshape(nh * a, nh * b)


def _pad_rows(a, rows, value=0.0):
    return jnp.pad(a, ((0, rows - a.shape[0]), (0, 0)), constant_values=value)


def kernel(x, meta_tokens, ln1_g, ln1_b, ffn1_w_gate, ffn1_w_up, ffn1_w_down, w_in, w_uk, w_uv, kv_norm_g,
           conv_w, b_igate, b_fgate, ml_norm_g, w_out, ln2_g, ln2_b, ffn2_w_gate, ffn2_w_up, ffn2_w_down,
           ln3_g, ln3_b):
    depth = ln1_g.shape[0]
    assert depth == 1, "the meta-token shortcut below is only valid for a single layer"
    bsz, seq, d = x.shape
    assert seq % ROW_TILE == 0 and seq % ML_CHUNK == 0 and seq % Q_TILE == 0
    assert Q_TILE == GT_CHUNK and ML_CHUNK == GT_CHUNK and KEY_CHUNK == GT_CHUNK and ROW_TILE % GT_CHUNK == 0
    alpha = (2 * depth) ** 0.25
    topk = min(TOPK_MAX, seq // 4)

    row2 = lambda p: p[0].reshape(1, -1).astype(F32)
    bf = lambda w: w[0].astype(BF16)

    w_t = jnp.swapaxes(w_in[0], 0, 1)
    sizes = (ATT_WIDTH, KV_LATENT, IDX_WIDTH, IDX_DIM, IDX_HEADS, MLQK_WIDTH, ML_WIDTH, ML_WIDTH, ML_HEADS, ML_HEADS)
    offs = [sum(sizes[:n]) for n in range(len(sizes) + 1)]
    assert w_t.shape[0] == offs[-1]
    wa = jnp.concatenate([w_t[offs[0]:offs[4]], w_t[offs[4]:offs[5]], w_t[offs[8]:offs[10]],
                          jnp.zeros((LANES - GATE_END, d), F32)], axis=0).astype(BF16)
    wm = w_t[offs[5]:offs[8]].astype(BF16)
    gbias = jnp.concatenate([jnp.zeros((GATE_I0,), F32), b_igate[0], b_fgate[0],
                             jnp.zeros((LANES - GATE_END,), F32)]).reshape(1, LANES)
    wuk_bd = jnp.stack([_block_diag(w_uk[0][2 * p:2 * p + 2]) for p in range(HEAD_PAIRS)]).astype(BF16)
    wuv_bd = jnp.stack([_block_diag(w_uv[0][2 * p:2 * p + 2]) for p in range(HEAD_PAIRS)]).astype(BF16)
    kvg = row2(kv_norm_g)
    convw = conv_w[0].astype(F32)

    h1, h1_meta = _ffn_ln_call(x.reshape(bsz * seq, d), meta_tokens.astype(F32), ffn1_w_gate[0], ffn1_w_up[0],
                               ffn1_w_down[0], row2(ln1_g), row2(ln1_b), alpha, FFN_TILE)
    zero_tail = jnp.zeros((CONV_HIST, MLQK_WIDTH), F32)
    (m_ckv, m_kidx, m_qk, m_v, _, m_gates, m_tail) = _inproj_call(
        h1_meta[None], zero_tail, wa, wm, wuk_bd, kvg, convw, gbias, N_META)

    (ckv, kidx, qk, _, og, gates, _, gates_t, ckv_t, v_t, q_t, qlat_t, qidx_t) = _inproj_call(
        h1.reshape(bsz, seq, d), m_tail[0], wa, wm, wuk_bd, kvg, convw, gbias, ROW_TILE)

    nchunks = seq // KEY_CHUNK
    ckv_c = ckv.reshape(bsz, nchunks, KEY_CHUNK, KV_LATENT)
    kidx_c = kidx.reshape(bsz, nchunks, KEY_CHUNK, IDX_DIM)
    ones_rows = jnp.zeros((ATT_EXT - KV_LATENT, N_META), BF16).at[0].set(1.0)
    m_ckvt = jnp.concatenate([m_ckv[0].T, ones_rows], axis=0)
    y_att = _dsa_call(qlat_t, qidx_t, gates_t, ckv_c, ckv_t, kidx_c,
                      m_ckv[0], m_ckvt, m_kidx[0], wuv_bd, topk)

    lane = jnp.arange(LANES)
    pad_gate = jnp.where((lane >= GATE_I0) & (lane < GATE_F0), NEG_BIG, 0.0).astype(F32)
    mg = jnp.concatenate([m_gates[0], jnp.broadcast_to(pad_gate, (ML_CHUNK - N_META, LANES))], axis=0)
    gate_lanes = slice(GATE_W0, GATE_W0 + GATE_ROWS)
    m_qk_pad = _pad_rows(m_qk[0], ML_CHUNK)
    y_ml = _mlstm_call(qk, q_t, v_t, og, gates, gates_t,
                       m_qk_pad, m_qk_pad[:, :MLQK_WIDTH // 2].T, _pad_rows(m_v[0], ML_CHUNK).T, mg, mg[:, gate_lanes].T,
                       row2(ml_norm_g))

    out = _out_ffn_ln_call(
        y_att.reshape(bsz * seq, ATT_WIDTH), y_ml.reshape(bsz * seq, ML_WIDTH), h1, bf(w_out),
        row2(ln2_g), row2(ln2_b), ffn2_w_gate[0], ffn2_w_up[0], ffn2_w_down[0], row2(ln3_g), row2(ln3_b),
        alpha, FFN_TILE)
    return out.reshape(bsz, seq, d)
```

```python
import functools

import jax
import jax.numpy as jnp
from jax import lax
from jax.experimental import pallas as pl
from jax.experimental.pallas import tpu as pltpu

F32 = jnp.float32
BF16 = jnp.bfloat16

N_META = 16
ATT_HEADS = 8
ATT_HEAD_DIM = 64
KV_LATENT = 128
IDX_HEADS = 4
IDX_DIM = 64
TOPK_MAX = 256
ML_HEADS = 4
ML_V_DIM = 128
ML_QK_DIM = 64
CONV_WIDTH = 4
GATE_SOFTCAP = 15.0
M_INIT = -1e30
LN_EPS = 1e-5
NEG_BIG = -1e30
LOG2_E = 1.4426950408889634

LANES = 128
SUBLANES = 8
VMEM_BYTES_V7X = 64 * 1024 * 1024
VMEM_LIMIT_BYTES = VMEM_BYTES_V7X * 7 // 8

FF_CHUNK = 256
ROW_TILE = 512
FFN_TILE = 512
Q_TILE = 256
ATT_GROUP = 128
KEY_CHUNK = 256
N_BISECT = 16
REDUCE_ROWS = 32
ML_BATCH = 2
ML_CHUNK = 256


def _dot(a, b):
    return jnp.dot(a, b, preferred_element_type=F32)


def _layer_norm(z, g, b):
    mu = jnp.mean(z, axis=-1, keepdims=True)
    zc = z - mu
    var = jnp.mean(zc * zc, axis=-1, keepdims=True)
    return zc * lax.rsqrt(var + LN_EPS) * g + b


def _sigmoid(x):
    return 1.0 / (1.0 + jnp.exp(-x))


def _swiglu_chunk(xb, wg_c, wu_c, wd_c):
    g = _dot(xb, wg_c)
    u = _dot(xb, wu_c)
    return _dot((g * _sigmoid(g) * u).astype(BF16), wd_c)


def _ffn_ln(x, wg_s, wu_s, wd_s, g, b, alpha):
    xb = x.astype(BF16)
    acc = jnp.zeros(x.shape, F32)
    for c in range(wg_s.shape[0]):
        acc = acc + _swiglu_chunk(xb, wg_s[c], wu_s[c], wd_s[c])
    return _layer_norm(alpha * x + 0.5 * acc, g, b)


def _stage_ffn_weights(step, wg_ref, wu_ref, wd_ref, wg_s, wu_s, wd_s):
    wg_s[step] = wg_ref[...].astype(BF16)
    wu_s[step] = wu_ref[...].astype(BF16)
    wd_s[step] = wd_ref[...].astype(BF16)


def _ffn_weight_specs(d, d_ff):
    n = d_ff // FF_CHUNK
    col = pl.BlockSpec((d, FF_CHUNK), lambda i: (0, jnp.minimum(i, n - 1)))
    row = pl.BlockSpec((FF_CHUNK, d), lambda i: (jnp.minimum(i, n - 1), 0))
    scratch = [pltpu.VMEM((n, d, FF_CHUNK), BF16), pltpu.VMEM((n, d, FF_CHUNK), BF16),
               pltpu.VMEM((n, FF_CHUNK, d), BF16)]
    return n, [col, col, row], scratch


def _staged_tile_index(i, n_stage):
    return jnp.maximum(i - (n_stage - 1), 0)


def _ffn_ln_kernel(x_ref, meta_ref, wg_ref, wu_ref, wd_ref, g_ref, b_ref, o_ref, ometa_ref,
                   wg_s, wu_s, wd_s, macc_ref, acc0_ref, *, alpha, n_stage):
    i = pl.program_id(0)

    @pl.when(i == 0)
    def _():
        macc_ref[...] = jnp.zeros(macc_ref.shape, F32)
        acc0_ref[...] = jnp.zeros(acc0_ref.shape, F32)

    @pl.when(i < n_stage)
    def _():
        _stage_ffn_weights(i, wg_ref, wu_ref, wd_ref, wg_s, wu_s, wd_s)
        macc_ref[...] += _swiglu_chunk(meta_ref[...].astype(BF16), wg_s[i], wu_s[i], wd_s[i])
        acc0_ref[...] += _swiglu_chunk(x_ref[...].astype(BF16), wg_s[i], wu_s[i], wd_s[i])

    @pl.when(i == n_stage - 1)
    def _():
        ometa_ref[...] = _layer_norm(alpha * meta_ref[...] + 0.5 * macc_ref[...], g_ref[...], b_ref[...])
        o_ref[...] = _layer_norm(alpha * x_ref[...] + 0.5 * acc0_ref[...], g_ref[...], b_ref[...])

    @pl.when(i >= n_stage)
    def _():
        o_ref[...] = _ffn_ln(x_ref[...], wg_s, wu_s, wd_s, g_ref[...], b_ref[...], alpha)


def _resident(shape):
    return pl.BlockSpec(shape, lambda *_: (0,) * len(shape), pipeline_mode=pl.Buffered(1))


def _ffn_ln_call(x, meta, wg, wu, wd, g, b, alpha, tm):
    rows, d = x.shape
    n_stage, w_specs, w_scratch = _ffn_weight_specs(d, wg.shape[1])
    row_spec = pl.BlockSpec((tm, d), lambda i: (_staged_tile_index(i, n_stage), 0))
    return pl.pallas_call(
        functools.partial(_ffn_ln_kernel, alpha=alpha, n_stage=n_stage),
        grid=(n_stage - 1 + rows // tm,),
        in_specs=[row_spec, _resident(meta.shape)] + w_specs + [_resident(g.shape), _resident(b.shape)],
        out_specs=[row_spec, pl.BlockSpec(meta.shape, lambda i: (0, 0))],
        out_shape=[jax.ShapeDtypeStruct((rows, d), F32), jax.ShapeDtypeStruct(meta.shape, F32)],
        scratch_shapes=w_scratch + [pltpu.VMEM(meta.shape, F32), pltpu.VMEM((tm, d), F32)],
        compiler_params=pltpu.CompilerParams(
            dimension_semantics=("arbitrary",), vmem_limit_bytes=VMEM_LIMIT_BYTES),
        name="ffn_ln",
    )(x, meta, wg, wu, wd, g, b)


ATT_WIDTH = ATT_HEADS * ATT_HEAD_DIM
IDX_WIDTH = IDX_HEADS * IDX_DIM
MLQK_WIDTH = 2 * ML_HEADS * ML_QK_DIM
ML_WIDTH = ML_HEADS * ML_V_DIM
LAT_WIDTH = ATT_HEADS * KV_LATENT
CONV_HIST = SUBLANES
GATE_W0 = IDX_DIM
GATE_I0, GATE_F0, GATE_END = GATE_W0 + IDX_HEADS, GATE_W0 + IDX_HEADS + ML_HEADS, GATE_W0 + IDX_HEADS + 2 * ML_HEADS
GATE_ROWS = 2 * SUBLANES
GT_CHUNK = 256
ATT_EXT = KV_LATENT + 2 * SUBLANES
HEAD_PAIRS = ATT_HEADS // 2


def _inproj_kernel(h_ref, tail_ref, wa_ref, wm_ref, wuk_ref, kvg_ref, convw_ref, gbias_ref,
                   ckv_ref, kidx_ref, qk_ref, v_ref, og_ref, gates_ref, tailout_ref, *rest):
    carry_ref, wa_s, wm_s = rest[-3:]
    tm = h_ref.shape[1]

    @pl.when((pl.program_id(0) == 0) & (pl.program_id(1) == 0))
    def _():
        wa_s[...] = wa_ref[...].astype(F32).T.astype(BF16)
        wm_s[...] = wm_ref[...].astype(F32).T.astype(BF16)

    @pl.when(pl.program_id(1) == 0)
    def _():
        carry_ref[...] = tail_ref[...]

    xb = h_ref[0].astype(BF16)

    pa = _dot(xb, wa_s[...])
    q_a = pa[:, :ATT_WIDTH].astype(BF16)
    c0 = ATT_WIDTH
    ckv = pa[:, c0:c0 + KV_LATENT]
    c1 = c0 + KV_LATENT
    ckv = ckv * lax.rsqrt(jnp.mean(ckv * ckv, axis=-1, keepdims=True) + LN_EPS) * kvg_ref[...]
    ckv_ref[0] = ckv.astype(BF16)
    c2 = c1 + IDX_WIDTH
    kidx_ref[0] = pa[:, c2:c2 + IDX_DIM].astype(BF16)
    pair_in = 2 * ATT_HEAD_DIM
    q_lat = [_dot(q_a[:, p * pair_in:(p + 1) * pair_in], wuk_ref[p]) * (ATT_HEAD_DIM ** -0.5 * LOG2_E)
             for p in range(HEAD_PAIRS)]

    pm = _dot(xb, wm_s[...])
    qk_raw = pm[:, :MLQK_WIDTH]
    v_ref[0] = pm[:, MLQK_WIDTH:MLQK_WIDTH + ML_WIDTH].astype(BF16)
    og_ref[0] = _sigmoid(pm[:, MLQK_WIDTH + ML_WIDTH:])

    ext = jnp.concatenate([carry_ref[...], qk_raw], axis=0)
    cw = convw_ref[...]
    conv = jnp.zeros_like(qk_raw)
    for j in range(CONV_WIDTH):
        s0 = CONV_HIST - (CONV_WIDTH - 1) + j
        conv = conv + ext[s0:s0 + tm] * cw[j:j + 1]
    act = conv * _sigmoid(conv)
    half = MLQK_WIDTH // 2
    qk_ref[0, :, :half] = act[:, :half].astype(BF16)
    qk_ref[0, :, half:] = (act[:, half:] * (ML_QK_DIM ** -0.5)).astype(BF16)
    carry_ref[...] = qk_raw[tm - CONV_HIST:]
    tailout_ref[0] = qk_raw[tm - CONV_HIST:]

    gr = pa[:, c2 + IDX_DIM - GATE_W0:]
    lane = lax.broadcasted_iota(jnp.int32, gr.shape, 1)
    sc = GATE_SOFTCAP * jnp.tanh((gr + gbias_ref[...]) / GATE_SOFTCAP)
    lf = -(jnp.maximum(-sc, 0.0) + jnp.log1p(jnp.exp(-jnp.abs(sc))))
    w_scaled = gr * (IDX_HEADS ** -0.5 * IDX_DIM ** -0.5)
    gates = jnp.where((lane < GATE_W0) | (lane >= GATE_END), 0.0,
                      jnp.where(lane < GATE_I0, w_scaled, jnp.where(lane < GATE_F0, sc, lf)))
    gates_ref[0] = gates
    if len(rest) == 9:
        gt_ref, ckvt_ref, vt_ref, qt_ref, qlatt_ref, qidxt_ref = rest[:6]
        q_idx_t = pa[:, c1:c1 + IDX_WIDTH].T.astype(BF16)
        q_lat_t = jnp.concatenate([ql.T for ql in q_lat], axis=0).astype(BF16)
        gates_t = gates.T[GATE_W0:GATE_W0 + GATE_ROWS]
        ckv_t = ckv.T.astype(BF16)
        v_t = pm[:, MLQK_WIDTH:MLQK_WIDTH + ML_WIDTH].T.astype(BF16)
        q_t = act[:, :half].T.astype(BF16)
        ones_blk = jnp.where(lax.broadcasted_iota(jnp.int32, (ATT_EXT - KV_LATENT, GT_CHUNK), 0) == 0,
                             1.0, 0.0).astype(BF16)
        for j in range(tm // GT_CHUNK):
            piece = slice(j * GT_CHUNK, (j + 1) * GT_CHUNK)
            gt_ref[0, j] = gates_t[:, piece]
            ckvt_ref[0, j] = jnp.concatenate([ckv_t[:, piece], ones_blk], axis=0)
            vt_ref[0, j] = v_t[:, piece]
            qt_ref[0, j] = q_t[:, piece]
            qlatt_ref[0, j] = q_lat_t[:, piece]
            qidxt_ref[0, j] = q_idx_t[:, piece]


def _inproj_call(h, tail, wa, wm, wuk_bd, kvg, convw, gbias, tm):
    bn, rows, d = h.shape
    nblk = rows // tm
    emit_gt = tm % GT_CHUNK == 0

    def row_spec(width):
        return pl.BlockSpec((1, tm, width), lambda b, j: (b, j, 0))

    outs = [
        (KV_LATENT, BF16), (IDX_DIM, BF16),
        (MLQK_WIDTH, BF16), (ML_WIDTH, BF16), (ML_WIDTH, F32), (LANES, F32),
    ]
    out_shape = [jax.ShapeDtypeStruct((bn, rows, w), dt) for w, dt in outs]
    out_specs = [row_spec(w) for w, _ in outs]
    out_shape.append(jax.ShapeDtypeStruct((bn, CONV_HIST, MLQK_WIDTH), F32))
    out_specs.append(pl.BlockSpec((1, CONV_HIST, MLQK_WIDTH), lambda b, j: (b, 0, 0)))
    if emit_gt:
        per_tile = tm // GT_CHUNK
        for height, dt in ((GATE_ROWS, F32), (ATT_EXT, BF16), (ML_WIDTH, BF16), (MLQK_WIDTH // 2, BF16),
                           (LAT_WIDTH, BF16), (IDX_WIDTH, BF16)):
            out_shape.append(jax.ShapeDtypeStruct((bn, rows // GT_CHUNK, height, GT_CHUNK), dt))
            out_specs.append(pl.BlockSpec((1, per_tile, height, GT_CHUNK), lambda b, j: (b, j, 0, 0)))
    return pl.pallas_call(
        _inproj_kernel,
        grid=(bn, nblk),
        in_specs=[
            row_spec(d),
            _resident(tail.shape), _resident(wa.shape), _resident(wm.shape),
            _resident(wuk_bd.shape), _resident(kvg.shape), _resident(convw.shape), _resident(gbias.shape),
        ],
        out_specs=out_specs,
        out_shape=out_shape,
        scratch_shapes=[pltpu.VMEM((CONV_HIST, MLQK_WIDTH), F32),
                        pltpu.VMEM(wa.shape[::-1], BF16), pltpu.VMEM(wm.shape[::-1], BF16)],
        compiler_params=pltpu.CompilerParams(
            dimension_semantics=("arbitrary", "arbitrary"), vmem_limit_bytes=VMEM_LIMIT_BYTES),
        name="in_proj",
    )(h, tail, wa, wm, wuk_bd, kvg, convw, gbias)


def _dsa_kernel(qlat_ref, qidx_ref, wrow_ref, ckv_ref, ckvt_ref, kidx_ref, mckv_ref, mckvt_ref, mkidx_ref,
                wuv_ref, y_ref, s_ref, acc_ref, p_ref, *, topk):
    _, kc, tq = s_ref.shape
    i = pl.program_id(1)
    nch = ((i + 1) * tq + kc - 1) // kc
    qreal = i * tq + lax.broadcasted_iota(jnp.int32, (1, tq), 1)
    kf = float(topk)

    wrow = wrow_ref[0, 0]
    q_idx_t = qidx_ref[0, 0]
    q_idx_all_t = jnp.concatenate([q_idx_t[h * IDX_DIM:(h + 1) * IDX_DIM] for h in range(IDX_HEADS)],
                                  axis=1)
    wi = [wrow[h:h + 1, :] for h in range(IDX_HEADS)]

    def scores(k_rows):
        lg = _dot(k_rows, q_idx_all_t)
        sc = jnp.zeros((k_rows.shape[0], tq), F32)
        for h in range(IDX_HEADS):
            sc = sc + jnp.maximum(lg[:, h * tq:(h + 1) * tq], 0.0) * wi[h]
        return sc

    s_meta = scores(mkidx_ref[...])

    def score_chunk(c, lo, hi):
        sc = scores(kidx_ref[0, c])
        valid = c * kc + lax.broadcasted_iota(jnp.int32, (kc, tq), 0) <= qreal
        s_ref[c] = jnp.where(valid, sc, -jnp.inf)
        groups = (kc // REDUCE_ROWS, REDUCE_ROWS, tq)
        lo = jnp.minimum(lo, jnp.min(sc.reshape(groups), axis=0))
        hi = jnp.maximum(hi, jnp.max(sc.reshape(groups), axis=0))
        return lo, hi

    def score_pair(c2, carry):
        lo, hi = score_chunk(2 * c2, *carry)
        return score_chunk(jnp.minimum(2 * c2 + 1, nch - 1), lo, hi)

    lo, hi = lax.fori_loop(0, (nch + 1) // 2, score_pair,
                           (jnp.full((REDUCE_ROWS, tq), jnp.inf, F32), jnp.full((REDUCE_ROWS, tq), -jnp.inf, F32)))
    lo = jnp.minimum(jnp.min(lo, axis=0, keepdims=True), jnp.min(s_meta, axis=0, keepdims=True))
    hi = jnp.maximum(jnp.max(hi, axis=0, keepdims=True), jnp.max(s_meta, axis=0, keepdims=True))

    def key_reduce(reduce, combine, per_chunk, init):
        def body(c, acc):
            x = per_chunk(s_ref[c]).reshape(kc // REDUCE_ROWS, REDUCE_ROWS, tq)
            return combine(acc, reduce(x, axis=0))
        acc = lax.fori_loop(0, nch, body, jnp.full((REDUCE_ROWS, tq), init, F32))
        return combine(reduce(acc, axis=0, keepdims=True), reduce(per_chunk(s_meta), axis=0, keepdims=True))

    def count(pred):
        return key_reduce(jnp.sum, jnp.add, lambda sc: jnp.where(pred(sc), 1.0, 0.0), 0.0)

    def max_where(pred):
        return key_reduce(jnp.max, jnp.maximum, lambda sc: jnp.where(pred(sc), sc, -jnp.inf), -jnp.inf)

    def bisect(_, carry):
        lo, hi = carry
        mid = 0.5 * lo + 0.5 * hi
        up = count(lambda sc: sc > mid) >= kf
        return jnp.where(up, mid, lo), jnp.where(up, hi, mid)

    lo, hi = lax.fori_loop(0, N_BISECT, bisect, (lo, hi))

    n_valid = (qreal + (N_META + 1)).astype(F32)
    small = n_valid <= kf
    cand = max_where(lambda sc: sc <= hi)
    n_ge = count(lambda sc: sc >= cand)
    done = jnp.where(small | (n_ge >= kf), 1.0, 0.0)

    def not_finished(state):
        return jnp.min(state[1]) < 0.5

    def step_down(state):
        cand, done, _ = state
        nxt = jnp.where(done > 0.5, cand, max_where(lambda sc: sc < cand))
        n_ge = count(lambda sc: sc >= nxt)
        return nxt, jnp.where(n_ge >= kf, 1.0, done), n_ge

    cand, _, n_ge = lax.while_loop(not_finished, step_down, step_down((cand, done, n_ge)))
    thr = jnp.where(small, -jnp.inf, cand)
    n_eq = count(lambda sc: sc == thr)
    need = jnp.where(small, 0.0, kf - (n_ge - n_eq))
    ranked_ties = jnp.max(jnp.where(n_eq > need, 1.0, 0.0)) > 0.5

    qlat_t = qlat_ref[0, 0]
    n_groups = tq // ATT_GROUP
    onehot = (lax.broadcasted_iota(jnp.int32, (ATT_GROUP, ATT_GROUP), 0)
              == lax.broadcasted_iota(jnp.int32, (ATT_GROUP, ATT_GROUP), 1)).astype(BF16)
    q_aug_t = [jnp.concatenate(
        [jnp.concatenate([qlat_t[h * KV_LATENT:(h + 1) * KV_LATENT, g * ATT_GROUP:(g + 1) * ATT_GROUP]
                          for h in range(ATT_HEADS)], axis=1),
         jnp.concatenate([onehot] * ATT_HEADS, axis=1)], axis=0)
        for g in range(n_groups)]
    hq = ATT_HEADS * ATT_GROUP

    def lower_tri(n):
        return (lax.broadcasted_iota(jnp.int32, (n, n), 1) <= lax.broadcasted_iota(jnp.int32, (n, n), 0)).astype(BF16)

    def attention(ranked):
        def mask_bias(sc, eq_seen):
            if not ranked:
                return jnp.where(sc >= thr, 0.0, NEG_BIG).astype(BF16), eq_seen
            n = sc.shape[0]
            eq = sc == thr
            rank = _dot(lower_tri(n), jnp.where(eq, 1.0, 0.0).astype(BF16)) + eq_seen
            keep = (sc > thr) | (eq & (rank <= need))
            return jnp.where(keep, 0.0, NEG_BIG).astype(BF16), rank[n - 1:n, :]

        def logits(g, kv, bias):
            k_aug = jnp.concatenate([kv, bias[:, g * ATT_GROUP:(g + 1) * ATT_GROUP]], axis=1)
            return _dot(k_aug, q_aug_t[g])

        def fold_in(g, c_prev, a_prev):
            acc_ref[g] = a_prev * acc_ref[g] + _dot(ckvt_ref[0, c_prev], p_ref[g])

        def attend(c, carry):
            bias, eq_seen = mask_bias(s_ref[c], carry[0])
            out = [eq_seen]
            for g in range(n_groups):
                m, a_prev = carry[1 + 2 * g:3 + 2 * g]
                fold_in(g, c - 1, a_prev)
                s = logits(g, ckv_ref[0, c], bias)
                m_new = jnp.maximum(m, jnp.max(s, axis=0, keepdims=True))
                p_ref[g] = jnp.exp2(s - m_new).astype(BF16)
                out += [m_new, jnp.exp2(m - m_new)]
            return tuple(out)

        bias_m, eq_seen = mask_bias(s_meta, jnp.zeros((1, tq), F32))
        bias_0, eq_seen = mask_bias(s_ref[0], eq_seen)
        kv_first = jnp.concatenate([mckv_ref[...], ckv_ref[0, 0]], axis=0)
        bias_first = jnp.concatenate([bias_m, bias_0], axis=0)
        init = [eq_seen]
        for g in range(n_groups):
            s = logits(g, kv_first, bias_first)
            m = jnp.max(s, axis=0, keepdims=True)
            p = jnp.exp2(s - m).astype(BF16)
            acc_ref[g] = _dot(mckvt_ref[...], p[:N_META])
            p_ref[g] = p[N_META:]
            init += [m, jnp.ones((1, hq), F32)]
        carry = lax.fori_loop(1, nch, attend, tuple(init))
        rows = []
        for g in range(n_groups):
            fold_in(g, nch - 1, carry[2 + 2 * g])
            acc = acc_ref[g]
            o_t = (acc[:KV_LATENT] / acc[KV_LATENT:KV_LATENT + 1]).T
            rows.append(jnp.concatenate([o_t[h * ATT_GROUP:(h + 1) * ATT_GROUP] for h in range(ATT_HEADS)], axis=1))
        o_all = jnp.concatenate(rows, axis=0).astype(BF16)
        pair = 2 * KV_LATENT
        return jnp.concatenate([_dot(o_all[:, p * pair:(p + 1) * pair], wuv_ref[p]) for p in range(HEAD_PAIRS)],
                               axis=1).astype(BF16)

    y_ref[0] = lax.cond(ranked_ties, lambda: attention(True), lambda: attention(False))


def _dsa_call(qlat_t, qidx_t, wrow, ckv_c, ckvt_c, kidx_c, m_ckv, m_ckvt, m_kidx, wuv_bd, topk):
    bn, nq = qidx_t.shape[:2]
    rows = nq * Q_TILE
    nchunks, kc = ckv_c.shape[1], ckv_c.shape[2]

    def tile_spec(height):
        return pl.BlockSpec((1, 1, height, Q_TILE), lambda b, i: (b, i, 0, 0))

    def q_spec(width):
        return pl.BlockSpec((1, Q_TILE, width), lambda b, i: (b, i, 0))

    def k_spec(a):
        return pl.BlockSpec((1,) + a.shape[1:], lambda b, i: (b, 0, 0, 0))

    return pl.pallas_call(
        functools.partial(_dsa_kernel, topk=topk),
        grid=(bn, nq),
        in_specs=[tile_spec(LAT_WIDTH), tile_spec(IDX_WIDTH), tile_spec(GATE_ROWS),
                  k_spec(ckv_c), k_spec(ckvt_c), k_spec(kidx_c),
                  _resident(m_ckv.shape), _resident(m_ckvt.shape), _resident(m_kidx.shape), _resident(wuv_bd.shape)],
        out_specs=q_spec(ATT_WIDTH),
        out_shape=jax.ShapeDtypeStruct((bn, rows, ATT_WIDTH), BF16),
        scratch_shapes=[pltpu.VMEM((nchunks, kc, Q_TILE), F32),
                        pltpu.VMEM((Q_TILE // ATT_GROUP, ckvt_c.shape[2], ATT_HEADS * ATT_GROUP), F32),
                        pltpu.VMEM((Q_TILE // ATT_GROUP, kc, ATT_HEADS * ATT_GROUP), BF16)],
        compiler_params=pltpu.CompilerParams(
            dimension_semantics=("arbitrary", "arbitrary"), vmem_limit_bytes=VMEM_LIMIT_BYTES),
        name="dsa",
    )(qlat_t, qidx_t, wrow, ckv_c, ckvt_c, kidx_c, m_ckv, m_ckvt, m_kidx, wuv_bd)


def _split3(x):
    hi = x.astype(BF16)
    r = x - hi.astype(F32)
    mid = r.astype(BF16)
    lo = (r - mid.astype(F32)).astype(BF16)
    return hi, mid, lo


ML_EXT = ML_V_DIM + 2 * SUBLANES


def _mlstm_chunk(qk, qt, vt, g, gt, state):
    L = qk.shape[0]
    s_idx = lax.broadcasted_iota(jnp.int32, (L, L), 0)
    t_idx = lax.broadcasted_iota(jnp.int32, (L, L), 1)
    causal = s_idx <= t_idx
    w = g.shape[1]
    cols3 = _dot((t_idx <= s_idx).astype(BF16), jnp.concatenate(_split3(g), axis=1))
    b_cols = cols3[:, :w] + cols3[:, w:2 * w] + cols3[:, 2 * w:]
    r = gt.shape[0]
    rows3 = _dot(jnp.concatenate(_split3(gt), axis=0), causal.astype(BF16))
    b_rows = rows3[:r] + rows3[r:2 * r] + rows3[2 * r:]
    ones_blk = jnp.where(lax.broadcasted_iota(jnp.int32, (ML_EXT - ML_V_DIM, L), 0) == 0, 1.0, 0.0).astype(BF16)
    kq = ML_HEADS * ML_QK_DIM

    outs, new_state = [], []
    for h in range(ML_HEADS):
        ce, m_prev = state[h]
        c_col = g[:, GATE_I0 + h:GATE_I0 + h + 1] - b_cols[:, GATE_F0 + h:GATE_F0 + h + 1]
        b_row = b_rows[GATE_F0 - GATE_W0 + h:GATE_F0 - GATE_W0 + h + 1, :]
        ig_row = gt[GATE_I0 - GATE_W0 + h:GATE_I0 - GATE_W0 + h + 1, :]
        qt_h = qt[h * ML_QK_DIM:(h + 1) * ML_QK_DIM, :]
        kh = qk[:, kq + h * ML_QK_DIM:kq + (h + 1) * ML_QK_DIM]
        vt_ext = jnp.concatenate([vt[h * ML_V_DIM:(h + 1) * ML_V_DIM, :], ones_blk], axis=0)

        d_t = jnp.where(causal, c_col + b_row, -jnp.inf)
        inter = b_row + m_prev
        m_t = jnp.maximum(jnp.max(d_t, axis=0, keepdims=True), inter)
        w_inter = jnp.exp(inter - m_t)
        s_t = _dot(kh, qt_h) * jnp.exp(d_t - m_t)
        r = _dot(vt_ext, s_t.astype(BF16)) + _dot(ce.astype(BF16), qt_h) * w_inter
        num = r[:ML_V_DIM]
        den = r[ML_V_DIM:ML_V_DIM + 1]
        hh = num / jnp.maximum(jnp.abs(den), jnp.exp(-m_t))
        mu = jnp.mean(hh, axis=0, keepdims=True)
        hc = hh - mu
        var = jnp.mean(hc * hc, axis=0, keepdims=True)
        outs.append((hc * lax.rsqrt(var + LN_EPS)).T)

        b_end = b_row[:, L - 1:L]
        g_row = b_end - b_row + ig_row
        m_new = jnp.maximum(b_end + m_prev, jnp.max(g_row, axis=1, keepdims=True))
        decay = jnp.exp(b_end + m_prev - m_new)
        weighted = (vt_ext.astype(F32) * jnp.exp(g_row - m_new)).astype(BF16)
        new_state.append((decay * ce + _dot(weighted, kh), m_new))
    return outs, new_state


def _mlstm_kernel(qk_ref, qt_ref, vt_ref, og_ref, gates_ref, gt_ref, mqk_ref, mqt_ref, mvt_ref, mgates_ref, mgt_ref,
                  ng_ref, y_ref,
                  ce0_ref, m0_ref):
    L = mqk_ref.shape[0]
    n_chunks = qk_ref.shape[1] // L
    norm_g = ng_ref[...]

    @pl.when(pl.program_id(0) == 0)
    def _():
        state = [(jnp.zeros((ML_EXT, ML_QK_DIM), F32), jnp.full((1, 1), M_INIT, F32)) for _ in range(ML_HEADS)]
        _, state = _mlstm_chunk(mqk_ref[...], mqt_ref[...], mvt_ref[...], mgates_ref[...], mgt_ref[...], state)
        for h in range(ML_HEADS):
            ce0_ref[h] = state[h][0]
            m0_ref[h] = jnp.broadcast_to(state[h][1], m0_ref.shape[1:])

    n_b = qk_ref.shape[0]
    state = [(ce0_ref[h], m0_ref[h][0:1, 0:1]) for h in range(ML_HEADS)] * n_b

    def body(c, flat):
        rows = pl.ds(pl.multiple_of(c * L, L), L)
        new_flat = []
        for b in range(n_b):
            state = [(flat[2 * (b * ML_HEADS + h)], flat[2 * (b * ML_HEADS + h) + 1]) for h in range(ML_HEADS)]
            outs, state = _mlstm_chunk(qk_ref[b, rows, :], qt_ref[b, c], vt_ref[b, c], gates_ref[b, rows, :],
                                       gt_ref[b, c], state)
            y = jnp.concatenate(outs, axis=1) * norm_g * og_ref[b, rows, :]
            y_ref[b, rows, :] = y.astype(BF16)
            new_flat += [x for pair in state for x in pair]
        return tuple(new_flat)

    lax.fori_loop(0, n_chunks, body, tuple(x for pair in state for x in pair))


def _mlstm_call(qk, qt_c, vt_c, og, gates, gt_c, mqk, mqt, mvt, mgates, mgt, norm_g):
    bn, rows, _ = qk.shape
    nb = ML_BATCH if bn % ML_BATCH == 0 else 1

    def b_spec(a):
        return pl.BlockSpec((nb,) + a.shape[1:], lambda b: (b,) + (0,) * (a.ndim - 1))

    consts = (mqk, mqt, mvt, mgates, mgt, norm_g)
    return pl.pallas_call(
        _mlstm_kernel,
        grid=(bn // nb,),
        in_specs=[b_spec(qk), b_spec(qt_c), b_spec(vt_c), b_spec(og), b_spec(gates), b_spec(gt_c)]
        + [_resident(c.shape) for c in consts],
        out_specs=pl.BlockSpec((nb, rows, ML_WIDTH), lambda b: (b, 0, 0)),
        out_shape=jax.ShapeDtypeStruct((bn, rows, ML_WIDTH), BF16),
        scratch_shapes=[pltpu.VMEM((ML_HEADS, ML_EXT, ML_QK_DIM), F32), pltpu.VMEM((ML_HEADS, SUBLANES, LANES), F32)],
        compiler_params=pltpu.CompilerParams(
            dimension_semantics=("arbitrary",), vmem_limit_bytes=VMEM_LIMIT_BYTES),
        name="mlstm",
    )(qk, qt_c, vt_c, og, gates, gt_c, *consts)


def _out_ffn_ln_kernel(ya_ref, ym_ref, h_ref, wo_ref, g2_ref, b2_ref, wg_ref, wu_ref, wd_ref, g3_ref, b3_ref,
                       o_ref, wg_s, wu_s, wd_s, h2_ref, acc0_ref, *, alpha, n_stage):
    i = pl.program_id(0)

    def mixed():
        mix = _dot(ya_ref[...], wo_ref[:ATT_WIDTH, :]) + _dot(ym_ref[...], wo_ref[ATT_WIDTH:, :])
        return _layer_norm(alpha * h_ref[...] + mix, g2_ref[...], b2_ref[...])

    @pl.when(i == 0)
    def _():
        h2_ref[...] = mixed()
        acc0_ref[...] = jnp.zeros(acc0_ref.shape, F32)

    @pl.when(i < n_stage)
    def _():
        _stage_ffn_weights(i, wg_ref, wu_ref, wd_ref, wg_s, wu_s, wd_s)
        acc0_ref[...] += _swiglu_chunk(h2_ref[...].astype(BF16), wg_s[i], wu_s[i], wd_s[i])

    @pl.when(i == n_stage - 1)
    def _():
        o_ref[...] = _layer_norm(alpha * h2_ref[...] + 0.5 * acc0_ref[...], g3_ref[...], b3_ref[...])

    @pl.when(i >= n_stage)
    def _():
        o_ref[...] = _ffn_ln(mixed(), wg_s, wu_s, wd_s, g3_ref[...], b3_ref[...], alpha)


def _out_ffn_ln_call(ya, ym, h, wo, g2, b2, wg, wu, wd, g3, b3, alpha, tm):
    rows, d = h.shape
    n_stage, w_specs, w_scratch = _ffn_weight_specs(d, wg.shape[1])

    def row_spec(width):
        return pl.BlockSpec((tm, width), lambda i: (_staged_tile_index(i, n_stage), 0))

    return pl.pallas_call(
        functools.partial(_out_ffn_ln_kernel, alpha=alpha, n_stage=n_stage),
        grid=(n_stage - 1 + rows // tm,),
        in_specs=[row_spec(ya.shape[1]), row_spec(ym.shape[1]), row_spec(d),
                  _resident(wo.shape), _resident(g2.shape), _resident(b2.shape)] + w_specs
        + [_resident(g3.shape), _resident(b3.shape)],
        out_specs=row_spec(d),
        out_shape=jax.ShapeDtypeStruct((rows, d), F32),
        scratch_shapes=w_scratch + [pltpu.VMEM((tm, d), F32), pltpu.VMEM((tm, d), F32)],
        compiler_params=pltpu.CompilerParams(
            dimension_semantics=("arbitrary",), vmem_limit_bytes=VMEM_LIMIT_BYTES),
        name="out_ffn_ln",
    )(ya, ym, h, wo, g2, b2, wg, wu, wd, g3, b3)


def _block_diag(w):
    nh, a, b = w.shape
    eye = jnp.eye(nh, dtype=w.dtype)
    return (eye[:, None, :, None] * w[:, :, None, :]).reshape(nh * a, nh * b)


def _pad_rows(a, rows, value=0.0):
    return jnp.pad(a, ((0, rows - a.shape[0]), (0, 0)), constant_values=value)


def kernel(x, meta_tokens, ln1_g, ln1_b, ffn1_w_gate, ffn1_w_up, ffn1_w_down, w_in, w_uk, w_uv, kv_norm_g,
           conv_w, b_igate, b_fgate, ml_norm_g, w_out, ln2_g, ln2_b, ffn2_w_gate, ffn2_w_up, ffn2_w_down,
           ln3_g, ln3_b):
    depth = ln1_g.shape[0]
    assert depth == 1, "the meta-token shortcut below is only valid for a single layer"
    bsz, seq, d = x.shape
    assert seq % ROW_TILE == 0 and seq % ML_CHUNK == 0 and seq % Q_TILE == 0
    assert Q_TILE == GT_CHUNK and ML_CHUNK == GT_CHUNK and KEY_CHUNK == GT_CHUNK and ROW_TILE % GT_CHUNK == 0
    alpha = (2 * depth) ** 0.25
    topk = min(TOPK_MAX, seq // 4)

    row2 = lambda p: p[0].reshape(1, -1).astype(F32)
    bf = lambda w: w[0].astype(BF16)

    w_t = jnp.swapaxes(w_in[0], 0, 1)
    sizes = (ATT_WIDTH, KV_LATENT, IDX_WIDTH, IDX_DIM, IDX_HEADS, MLQK_WIDTH, ML_WIDTH, ML_WIDTH, ML_HEADS, ML_HEADS)
    offs = [sum(sizes[:n]) for n in range(len(sizes) + 1)]
    assert w_t.shape[0] == offs[-1]
    wa = jnp.concatenate([w_t[offs[0]:offs[4]], w_t[offs[4]:offs[5]], w_t[offs[8]:offs[10]],
                          jnp.zeros((LANES - GATE_END, d), F32)], axis=0).astype(BF16)
    wm = w_t[offs[5]:offs[8]].astype(BF16)
    gbias = jnp.concatenate([jnp.zeros((GATE_I0,), F32), b_igate[0], b_fgate[0],
                             jnp.zeros((LANES - GATE_END,), F32)]).reshape(1, LANES)
    wuk_bd = jnp.stack([_block_diag(w_uk[0][2 * p:2 * p + 2]) for p in range(HEAD_PAIRS)]).astype(BF16)
    wuv_bd = jnp.stack([_block_diag(w_uv[0][2 * p:2 * p + 2]) for p in range(HEAD_PAIRS)]).astype(BF16)
    kvg = row2(kv_norm_g)
    convw = conv_w[0].astype(F32)

    h1, h1_meta = _ffn_ln_call(x.reshape(bsz * seq, d), meta_tokens.astype(F32), ffn1_w_gate[0], ffn1_w_up[0],
                               ffn1_w_down[0], row2(ln1_g), row2(ln1_b), alpha, FFN_TILE)
    zero_tail = jnp.zeros((CONV_HIST, MLQK_WIDTH), F32)
    (m_ckv, m_kidx, m_qk, m_v, _, m_gates, m_tail) = _inproj_call(
        h1_meta[None], zero_tail, wa, wm, wuk_bd, kvg, convw, gbias, N_META)

    (ckv, kidx, qk, _, og, gates, _, gates_t, ckv_t, v_t, q_t, qlat_t, qidx_t) = _inproj_call(
        h1.reshape(bsz, seq, d), m_tail[0], wa, wm, wuk_bd, kvg, convw, gbias, ROW_TILE)

    nchunks = seq // KEY_CHUNK
    ckv_c = ckv.reshape(bsz, nchunks, KEY_CHUNK, KV_LATENT)
    kidx_c = kidx.reshape(bsz, nchunks, KEY_CHUNK, IDX_DIM)
    ones_rows = jnp.zeros((ATT_EXT - KV_LATENT, N_META), BF16).at[0].set(1.0)
    m_ckvt = jnp.concatenate([m_ckv[0].T, ones_rows], axis=0)
    y_att = _dsa_call(qlat_t, qidx_t, gates_t, ckv_c, ckv_t, kidx_c,
                      m_ckv[0], m_ckvt, m_kidx[0], wuv_bd, topk)

    lane = jnp.arange(LANES)
    pad_gate = jnp.where((lane >= GATE_I0) & (lane < GATE_F0), NEG_BIG, 0.0).astype(F32)
    mg = jnp.concatenate([m_gates[0], jnp.broadcast_to(pad_gate, (ML_CHUNK - N_META, LANES))], axis=0)
    gate_lanes = slice(GATE_W0, GATE_W0 + GATE_ROWS)
    m_qk_pad = _pad_rows(m_qk[0], ML_CHUNK)
    y_ml = _mlstm_call(qk, q_t, v_t, og, gates, gates_t,
                       m_qk_pad, m_qk_pad[:, :MLQK_WIDTH // 2].T, _pad_rows(m_v[0], ML_CHUNK).T, mg, mg[:, gate_lanes].T,
                       row2(ml_norm_g))

    out = _out_ffn_ln_call(
        y_att.reshape(bsz * seq, ATT_WIDTH), y_ml.reshape(bsz * seq, ML_WIDTH), h1, bf(w_out),
        row2(ln2_g), row2(ln2_b), ffn2_w_gate[0], ffn2_w_up[0], ffn2_w_down[0], row2(ln3_g), row2(ln3_b),
        alpha, FFN_TILE)
    return out.reshape(bsz, seq, d)
```
